```python
import math
import jax, jax.numpy as jnp
from jax import lax
import numpy as np

D_MODEL = 1024
BATCH = 8
SEQ = 8192
DEPTH = 2

ATT_HEADS = 16
ATT_KV_HEADS = 2
ATT_HEAD_DIM = 64
ATT_WIDTH = ATT_HEADS * ATT_HEAD_DIM
KV_WIDTH = ATT_KV_HEADS * ATT_HEAD_DIM
WINDOW = 128
ATT_BLOCK = 128
REL_BUCKETS = 32
REL_MAX_DIST = 128
SG_GROUPS = 8
SG_CHUNK = 128
SG_WIDTH = 1024
SG_GROUP_DIM = SG_WIDTH // SG_GROUPS
SSM_WIDTH = 2 * D_MODEL
SSM_HEAD_DIM = 64
SSM_HEADS = SSM_WIDTH // SSM_HEAD_DIM
SSM_GROUPS = 4
SSM_STATE = 128
SSM_CONV = 4
SSM_CHUNK = 128
SSM_CONV_DIM = SSM_WIDTH + 2 * SSM_GROUPS * SSM_STATE
N_BRANCHES = 3
IN_SIZES = (ATT_WIDTH, KV_WIDTH, KV_WIDTH, ATT_WIDTH,
            SG_WIDTH, SG_WIDTH, SG_WIDTH,
            SSM_WIDTH, SSM_CONV_DIM, SSM_HEADS,
            N_BRANCHES * D_MODEL)
IN_COLS = sum(IN_SIZES)
EPS = 1e-6

kernel_name = "hybrid_swa_sgu_ssd_gated_merge"


def _split_points():
    return [int(v) for v in np.cumsum(np.array(IN_SIZES))[:-1]]


def rms_norm(x, g):
    xf = x.astype(jnp.float32)
    y = xf * lax.rsqrt(jnp.mean(xf * xf, axis=-1, keepdims=True) + EPS)
    return (y * g.astype(jnp.float32)).astype(x.dtype)


def t5_causal_bucket(dist):
    max_exact = REL_BUCKETS // 2
    dist_f = jnp.maximum(dist, 1).astype(jnp.float32)
    large = max_exact + (jnp.log(dist_f / max_exact) / math.log(REL_MAX_DIST / max_exact)
                         * (REL_BUCKETS - max_exact)).astype(jnp.int32)
    large = jnp.minimum(large, REL_BUCKETS - 1)
    return jnp.where(dist < max_exact, dist, large)


def sliding_window_attention(q, k, v, sinks, rel_bias):
    bsz, seq = q.shape[:2]
    nb = seq // ATT_BLOCK
    grp = ATT_HEADS // ATT_KV_HEADS
    qb = q.reshape(bsz, nb, ATT_BLOCK, ATT_KV_HEADS, grp, ATT_HEAD_DIM) * (ATT_HEAD_DIM ** -0.5)

    def band(t):
        tb = t.reshape(bsz, nb, ATT_BLOCK, ATT_KV_HEADS, ATT_HEAD_DIM)
        prev = jnp.pad(tb, ((0, 0), (1, 0), (0, 0), (0, 0), (0, 0)))[:, :-1]
        return jnp.concatenate([prev, tb], axis=2)

    kk, vv = band(k), band(v)
    logits = jnp.einsum('bnqkgd,bnskd->bnkgqs', qb, kk).astype(jnp.float32)

    qi = jnp.arange(ATT_BLOCK, dtype=jnp.int32)[:, None]
    kj = jnp.arange(2 * ATT_BLOCK, dtype=jnp.int32)[None, :]
    dist = qi + ATT_BLOCK - kj
    in_window = (dist >= 0) & (dist < WINDOW)
    key_exists = (jnp.arange(nb)[:, None] > 0) | (kj >= ATT_BLOCK)
    mask = in_window[None] & key_exists[:, None, :]

    bias = rel_bias.astype(jnp.float32)[t5_causal_bucket(jnp.maximum(dist, 0))]
    bias = jnp.transpose(bias, (2, 0, 1)).reshape(ATT_KV_HEADS, grp, ATT_BLOCK, 2 * ATT_BLOCK)
    logits = jnp.where(mask[None, :, None, None], logits + bias[None, None], -jnp.inf)

    sink = sinks.astype(jnp.float32).reshape(ATT_KV_HEADS, grp)[None, None, :, :, None, None]
    m = jnp.maximum(jnp.max(logits, axis=-1, keepdims=True), sink)
    p = jnp.exp(logits - m)
    p = p / (jnp.sum(p, axis=-1, keepdims=True) + jnp.exp(sink - m))
    out = jnp.einsum('bnkgqs,bnskd->bnqkgd', p.astype(vv.dtype), vv)
    return out.reshape(bsz, seq, ATT_WIDTH)


def chunked_spatial_gate(u, v, ln_g, ln_b, w_s, b_s):
    bsz, seq = u.shape[:2]
    nc = seq // SG_CHUNK
    vf = v.astype(jnp.float32)
    mu = jnp.mean(vf, axis=-1, keepdims=True)
    var = jnp.mean(jnp.square(vf - mu), axis=-1, keepdims=True)
    vn = ((vf - mu) * lax.rsqrt(var + EPS) * ln_g.astype(jnp.float32) + ln_b.astype(jnp.float32)).astype(v.dtype)
    vc = vn.reshape(bsz, nc, SG_CHUNK, SG_GROUPS, SG_GROUP_DIM)
    causal = jnp.tril(jnp.ones((SG_CHUNK, SG_CHUNK), dtype=bool))
    w = jnp.where(causal[None], w_s, jnp.zeros_like(w_s))
    mixed = jnp.einsum('gts,bcsgd->bctgd', w, vc) + jnp.transpose(b_s)[None, None, :, :, None]
    return u * mixed.reshape(bsz, seq, SG_WIDTH)


def causal_depthwise_conv(x, w, b):
    ch = x.shape[-1]
    y = lax.conv_general_dilated(x, w[:, None, :].astype(x.dtype), window_strides=(1,),
                                 padding=[(SSM_CONV - 1, 0)],
                                 dimension_numbers=('NWC', 'WIO', 'NWC'),
                                 feature_group_count=ch)
    return y + b


def ssd_mixer(z, xbc, dt_raw, conv_w, conv_b, dt_bias, a_log, d_skip, norm_g):
    bsz, seq = z.shape[:2]
    nc = seq // SSM_CHUNK
    hpg = SSM_HEADS // SSM_GROUPS
    L = SSM_CHUNK
    xbc = jax.nn.silu(causal_depthwise_conv(xbc, conv_w, conv_b))
    gn = SSM_GROUPS * SSM_STATE
    xs = xbc[..., :SSM_WIDTH]
    b_in = xbc[..., SSM_WIDTH:SSM_WIDTH + gn]
    c_in = xbc[..., SSM_WIDTH + gn:]

    dt = jax.nn.softplus(dt_raw.astype(jnp.float32) + dt_bias.astype(jnp.float32))
    a = -jnp.exp(a_log.astype(jnp.float32))
    x_heads = xs.astype(jnp.float32).reshape(bsz, seq, SSM_HEADS, SSM_HEAD_DIM)
    xdt = (x_heads * dt[..., None]).reshape(bsz, nc, L, SSM_GROUPS, hpg, SSM_HEAD_DIM)
    bc = b_in.astype(jnp.float32).reshape(bsz, nc, L, SSM_GROUPS, SSM_STATE)
    cc = c_in.astype(jnp.float32).reshape(bsz, nc, L, SSM_GROUPS, SSM_STATE)

    a_dt = (dt * a).reshape(bsz, nc, L, SSM_GROUPS, hpg).transpose(0, 3, 4, 1, 2)
    a_cs = jnp.cumsum(a_dt, axis=-1)

    causal = jnp.tril(jnp.ones((L, L), dtype=bool))
    seg = a_cs[..., :, None] - a_cs[..., None, :]
    decay_in = jnp.exp(jnp.where(causal, seg, -jnp.inf))
    cb = jnp.einsum('bclgn,bcsgn->bcgls', cc, bc)
    y_diag = jnp.einsum('bcgls,bgjcls,bcsgjp->bclgjp', cb, decay_in, xdt)

    decay_to_end = jnp.exp(a_cs[..., -1:] - a_cs)
    states = jnp.einsum('bcsgn,bgjcs,bcsgjp->bcgjpn', bc, decay_to_end, xdt)
    chunk_decay = jnp.exp(a_cs[..., -1])

    def carry_state(h, inp):
        dec, st = inp
        return h * dec[..., None, None] + st, h

    init = jnp.zeros((bsz, SSM_GROUPS, hpg, SSM_HEAD_DIM, SSM_STATE), jnp.float32)
    _, prev = lax.scan(carry_state, init, (jnp.moveaxis(chunk_decay, -1, 0), jnp.moveaxis(states, 1, 0)))
    prev = jnp.moveaxis(prev, 0, 1)
    y_off = jnp.einsum('bclgn,bcgjpn,bgjcl->bclgjp', cc, prev, jnp.exp(a_cs))

    y = (y_diag + y_off).reshape(bsz, seq, SSM_HEADS, SSM_HEAD_DIM) + d_skip.astype(jnp.float32)[:, None] * x_heads
    y = y.reshape(bsz, seq, SSM_WIDTH) * jax.nn.silu(z.astype(jnp.float32))
    yg = y.reshape(bsz, seq, SSM_GROUPS, SSM_WIDTH // SSM_GROUPS)
    yg = yg * lax.rsqrt(jnp.mean(yg * yg, axis=-1, keepdims=True) + EPS)
    y = yg.reshape(bsz, seq, SSM_WIDTH) * norm_g.astype(jnp.float32)
    return y.astype(z.dtype)


def hybrid_layer(x, w_in, g_pre, g_post, rel_bias, sinks, sg_ln_g, sg_ln_b, sg_w, sg_b,
                 conv_w, conv_b, dt_bias, a_log, d_skip, ssm_norm_g,
                 w_br_att, w_br_sg, w_br_ssm, w_out):
    bsz, seq = x.shape[:2]
    h = rms_norm(x, g_pre)
    proj = jnp.einsum('bsd,dc->bsc', h, w_in)
    (q, k, v, z_a, u, v_s, z_s, z_m, xbc, dt_raw, gate_logits) = jnp.split(proj, _split_points(), axis=-1)

    q = q.reshape(bsz, seq, ATT_HEADS, ATT_HEAD_DIM)
    k = k.reshape(bsz, seq, ATT_KV_HEADS, ATT_HEAD_DIM)
    v = v.reshape(bsz, seq, ATT_KV_HEADS, ATT_HEAD_DIM)
    y_att = sliding_window_attention(q, k, v, sinks, rel_bias) * jax.nn.silu(z_a)
    y_sg = chunked_spatial_gate(u, v_s, sg_ln_g, sg_ln_b, sg_w, sg_b) * jax.nn.silu(z_s)
    y_ssm = ssd_mixer(z_m, xbc, dt_raw, conv_w, conv_b, dt_bias, a_log, d_skip, ssm_norm_g)

    gates = jax.nn.sigmoid(gate_logits.reshape(bsz, seq, N_BRANCHES, D_MODEL))
    merged = (gates[:, :, 0] * (y_att @ w_br_att)
              + gates[:, :, 1] * (y_sg @ w_br_sg)
              + gates[:, :, 2] * (y_ssm @ w_br_ssm))
    out = merged @ w_out
    return x + rms_norm(out, g_post)


def _fwd_setup_inputs(seed: int = 0) -> dict:
    key = jax.random.key(seed)
    ks = jax.random.split(key, 24)
    f32 = jnp.float32
    nrm = lambda k, shape, s: (jax.random.normal(k, shape, f32) * s)
    dt0 = jnp.exp(jax.random.uniform(ks[12], (DEPTH, SSM_HEADS), f32, math.log(1e-3), math.log(1e-1)))
    return {
        "x": nrm(ks[0], (BATCH, SEQ, D_MODEL), 1.0),
        "w_in": nrm(ks[1], (DEPTH, D_MODEL, IN_COLS), D_MODEL ** -0.5),
        "norm_pre": 1.0 + nrm(ks[2], (DEPTH, D_MODEL), 0.02),
        "norm_post": 1.0 + nrm(ks[3], (DEPTH, D_MODEL), 0.02),
        "rel_bias": nrm(ks[4], (REL_BUCKETS, ATT_HEADS), 0.5),
        "att_sinks": nrm(ks[5], (DEPTH, ATT_HEADS), 0.5),
        "sg_ln_g": 1.0 + nrm(ks[6], (DEPTH, SG_WIDTH), 0.02),
        "sg_ln_b": nrm(ks[7], (DEPTH, SG_WIDTH), 0.02),
        "sg_w": nrm(ks[8], (DEPTH, SG_GROUPS, SG_CHUNK, SG_CHUNK), SG_CHUNK ** -0.5),
        "sg_b": 1.0 + nrm(ks[9], (DEPTH, SG_GROUPS, SG_CHUNK), 0.02),
        "ssm_conv_w": nrm(ks[10], (DEPTH, SSM_CONV, SSM_CONV_DIM), SSM_CONV ** -0.5),
        "ssm_conv_b": nrm(ks[11], (DEPTH, SSM_CONV_DIM), 0.02),
        "ssm_dt_bias": dt0 + jnp.log(-jnp.expm1(-dt0)),
        "ssm_a_log": jnp.log(jax.random.uniform(ks[13], (DEPTH, SSM_HEADS), f32, 1.0, 16.0)),
        "ssm_d": 1.0 + nrm(ks[14], (DEPTH, SSM_HEADS), 0.02),
        "ssm_norm_g": 1.0 + nrm(ks[15], (DEPTH, SSM_WIDTH), 0.02),
        "w_br_att": nrm(ks[16], (DEPTH, ATT_WIDTH, D_MODEL), ATT_WIDTH ** -0.5),
        "w_br_sg": nrm(ks[17], (DEPTH, SG_WIDTH, D_MODEL), SG_WIDTH ** -0.5),
        "w_br_ssm": nrm(ks[18], (DEPTH, SSM_WIDTH, D_MODEL), SSM_WIDTH ** -0.5),
        "w_out": nrm(ks[19], (DEPTH, D_MODEL, D_MODEL), D_MODEL ** -0.5),
    }


def _fwd_reference(x, w_in, norm_pre, norm_post, rel_bias, att_sinks, sg_ln_g, sg_ln_b, sg_w, sg_b,
              ssm_conv_w, ssm_conv_b, ssm_dt_bias, ssm_a_log, ssm_d, ssm_norm_g,
              w_br_att, w_br_sg, w_br_ssm, w_out):
    for layer in range(DEPTH):
        x = hybrid_layer(x, w_in[layer], norm_pre[layer], norm_post[layer], rel_bias,
                         att_sinks[layer], sg_ln_g[layer], sg_ln_b[layer], sg_w[layer], sg_b[layer],
                         ssm_conv_w[layer], ssm_conv_b[layer], ssm_dt_bias[layer], ssm_a_log[layer],
                         ssm_d[layer], ssm_norm_g[layer],
                         w_br_att[layer], w_br_sg[layer], w_br_ssm[layer], w_out[layer])
    return x


import jax as _jax
import jax.numpy as _jnp

TWIN_FORMAT = 'train_step'
FWD_PARAMS = ['x', 'w_in', 'norm_pre', 'norm_post', 'rel_bias', 'att_sinks', 'sg_ln_g', 'sg_ln_b', 'sg_w', 'sg_b', 'ssm_conv_w', 'ssm_conv_b', 'ssm_dt_bias', 'ssm_a_log', 'ssm_d', 'ssm_norm_g', 'w_br_att', 'w_br_sg', 'w_br_ssm', 'w_out']
TWIN_WEIGHTS = ['w_in', 'norm_pre', 'norm_post', 'rel_bias', 'att_sinks', 'sg_ln_g', 'sg_ln_b', 'sg_w', 'sg_b', 'ssm_conv_w', 'ssm_conv_b', 'ssm_dt_bias', 'ssm_a_log', 'ssm_d', 'ssm_norm_g', 'w_br_att', 'w_br_sg', 'w_br_ssm', 'w_out']
TWIN_DIFF_INPUT = 'x'
TWIN_INPUTS = ['x', 'w_in', 'norm_pre', 'norm_post', 'rel_bias', 'att_sinks', 'sg_ln_g', 'sg_ln_b', 'sg_w', 'sg_b', 'ssm_conv_w', 'ssm_conv_b', 'ssm_dt_bias', 'ssm_a_log', 'ssm_d', 'ssm_norm_g', 'w_br_att', 'w_br_sg', 'w_br_ssm', 'w_out', 'loss_target', 'm_w_in', 'm_norm_pre', 'm_norm_post', 'm_rel_bias', 'm_att_sinks', 'm_sg_ln_g', 'm_sg_ln_b', 'm_sg_w', 'm_sg_b', 'm_ssm_conv_w', 'm_ssm_conv_b', 'm_ssm_dt_bias', 'm_ssm_a_log', 'm_ssm_d', 'm_ssm_norm_g', 'm_w_br_att', 'm_w_br_sg', 'm_w_br_ssm', 'm_w_out', 'v_w_in', 'v_norm_pre', 'v_norm_post', 'v_rel_bias', 'v_att_sinks', 'v_sg_ln_g', 'v_sg_ln_b', 'v_sg_w', 'v_sg_b', 'v_ssm_conv_w', 'v_ssm_conv_b', 'v_ssm_dt_bias', 'v_ssm_a_log', 'v_ssm_d', 'v_ssm_norm_g', 'v_w_br_att', 'v_w_br_sg', 'v_w_br_ssm', 'v_w_out']
TWIN_OUTPUTS = ['loss', 'grad_x', 'grad_w_in', 'grad_norm_pre', 'grad_norm_post', 'grad_rel_bias', 'grad_att_sinks', 'grad_sg_ln_g', 'grad_sg_ln_b', 'grad_sg_w', 'grad_sg_b', 'grad_ssm_conv_w', 'grad_ssm_conv_b', 'grad_ssm_dt_bias', 'grad_ssm_a_log', 'grad_ssm_d', 'grad_ssm_norm_g', 'grad_w_br_att', 'grad_w_br_sg', 'grad_w_br_ssm', 'grad_w_out', 'delta_w_in', 'delta_norm_pre', 'delta_norm_post', 'delta_rel_bias', 'delta_att_sinks', 'delta_sg_ln_g', 'delta_sg_ln_b', 'delta_sg_w', 'delta_sg_b', 'delta_ssm_conv_w', 'delta_ssm_conv_b', 'delta_ssm_dt_bias', 'delta_ssm_a_log', 'delta_ssm_d', 'delta_ssm_norm_g', 'delta_w_br_att', 'delta_w_br_sg', 'delta_w_br_ssm', 'delta_w_out', 'new_m_w_in', 'new_m_norm_pre', 'new_m_norm_post', 'new_m_rel_bias', 'new_m_att_sinks', 'new_m_sg_ln_g', 'new_m_sg_ln_b', 'new_m_sg_w', 'new_m_sg_b', 'new_m_ssm_conv_w', 'new_m_ssm_conv_b', 'new_m_ssm_dt_bias', 'new_m_ssm_a_log', 'new_m_ssm_d', 'new_m_ssm_norm_g', 'new_m_w_br_att', 'new_m_w_br_sg', 'new_m_w_br_ssm', 'new_m_w_out', 'new_v_w_in', 'new_v_norm_pre', 'new_v_norm_post', 'new_v_rel_bias', 'new_v_att_sinks', 'new_v_sg_ln_g', 'new_v_sg_ln_b', 'new_v_sg_w', 'new_v_sg_b', 'new_v_ssm_conv_w', 'new_v_ssm_conv_b', 'new_v_ssm_dt_bias', 'new_v_ssm_a_log', 'new_v_ssm_d', 'new_v_ssm_norm_g', 'new_v_w_br_att', 'new_v_w_br_sg', 'new_v_w_br_ssm', 'new_v_w_out']
TWIN_LEAF_KINDS = {'loss': 'loss', 'grad_x': 'grad_x', 'grad_w_in': 'grad_w', 'grad_norm_pre': 'grad_w', 'grad_norm_post': 'grad_w', 'grad_rel_bias': 'grad_w', 'grad_att_sinks': 'grad_w', 'grad_sg_ln_g': 'grad_w', 'grad_sg_ln_b': 'grad_w', 'grad_sg_w': 'grad_w', 'grad_sg_b': 'grad_w', 'grad_ssm_conv_w': 'grad_w', 'grad_ssm_conv_b': 'grad_w', 'grad_ssm_dt_bias': 'grad_w', 'grad_ssm_a_log': 'grad_w', 'grad_ssm_d': 'grad_w', 'grad_ssm_norm_g': 'grad_w', 'grad_w_br_att': 'grad_w', 'grad_w_br_sg': 'grad_w', 'grad_w_br_ssm': 'grad_w', 'grad_w_out': 'grad_w', 'delta_w_in': 'delta_w', 'delta_norm_pre': 'delta_w', 'delta_norm_post': 'delta_w', 'delta_rel_bias': 'delta_w', 'delta_att_sinks': 'delta_w', 'delta_sg_ln_g': 'delta_w', 'delta_sg_ln_b': 'delta_w', 'delta_sg_w': 'delta_w', 'delta_sg_b': 'delta_w', 'delta_ssm_conv_w': 'delta_w', 'delta_ssm_conv_b': 'delta_w', 'delta_ssm_dt_bias': 'delta_w', 'delta_ssm_a_log': 'delta_w', 'delta_ssm_d': 'delta_w', 'delta_ssm_norm_g': 'delta_w', 'delta_w_br_att': 'delta_w', 'delta_w_br_sg': 'delta_w', 'delta_w_br_ssm': 'delta_w', 'delta_w_out': 'delta_w', 'new_m_w_in': 'new_m', 'new_m_norm_pre': 'new_m', 'new_m_norm_post': 'new_m', 'new_m_rel_bias': 'new_m', 'new_m_att_sinks': 'new_m', 'new_m_sg_ln_g': 'new_m', 'new_m_sg_ln_b': 'new_m', 'new_m_sg_w': 'new_m', 'new_m_sg_b': 'new_m', 'new_m_ssm_conv_w': 'new_m', 'new_m_ssm_conv_b': 'new_m', 'new_m_ssm_dt_bias': 'new_m', 'new_m_ssm_a_log': 'new_m', 'new_m_ssm_d': 'new_m', 'new_m_ssm_norm_g': 'new_m', 'new_m_w_br_att': 'new_m', 'new_m_w_br_sg': 'new_m', 'new_m_w_br_ssm': 'new_m', 'new_m_w_out': 'new_m', 'new_v_w_in': 'new_v', 'new_v_norm_pre': 'new_v', 'new_v_norm_post': 'new_v', 'new_v_rel_bias': 'new_v', 'new_v_att_sinks': 'new_v', 'new_v_sg_ln_g': 'new_v', 'new_v_sg_ln_b': 'new_v', 'new_v_sg_w': 'new_v', 'new_v_sg_b': 'new_v', 'new_v_ssm_conv_w': 'new_v', 'new_v_ssm_conv_b': 'new_v', 'new_v_ssm_dt_bias': 'new_v', 'new_v_ssm_a_log': 'new_v', 'new_v_ssm_d': 'new_v', 'new_v_ssm_norm_g': 'new_v', 'new_v_w_br_att': 'new_v', 'new_v_w_br_sg': 'new_v', 'new_v_w_br_ssm': 'new_v', 'new_v_w_out': 'new_v'}


def _forward(args):
    return _fwd_reference(*[args[k] for k in FWD_PARAMS])


def _output_shape():
    def fwd():
        inp = _fwd_setup_inputs(0)
        return _fwd_reference(*[inp[k] for k in FWD_PARAMS])
    out = _jax.eval_shape(fwd)
    return out.shape, out.dtype

N_MICROBATCH = 1
ADAM_LR = 0.001
ADAM_B1 = 0.9
ADAM_B2 = 0.999
ADAM_EPS = 1e-08
ADAM_WD = 0.01
ADAM_STEP = 10
PER_EXAMPLE_BATCH_AXIS = {'x': 0, 'loss_target': 0}
SHARED_INPUTS = []
_WEIGHT_DTYPES = {'w_in': _jnp.float32, 'norm_pre': _jnp.float32, 'norm_post': _jnp.float32, 'rel_bias': _jnp.float32, 'att_sinks': _jnp.float32, 'sg_ln_g': _jnp.float32, 'sg_ln_b': _jnp.float32, 'sg_w': _jnp.float32, 'sg_b': _jnp.float32, 'ssm_conv_w': _jnp.float32, 'ssm_conv_b': _jnp.float32, 'ssm_dt_bias': _jnp.float32, 'ssm_a_log': _jnp.float32, 'ssm_d': _jnp.float32, 'ssm_norm_g': _jnp.float32, 'w_br_att': _jnp.float32, 'w_br_sg': _jnp.float32, 'w_br_ssm': _jnp.float32, 'w_out': _jnp.float32}
MOMENT_SCALE = {'w_in': 2.999765e-01, 'norm_pre': 1.047987e+00, 'norm_post': 6.403809e+01, 'rel_bias': 1.005571e-01, 'att_sinks': 5.967828e-02, 'sg_ln_g': 1.836620e-01, 'sg_ln_b': 1.878163e-01, 'sg_w': 1.838554e-01, 'sg_b': 2.786384e-01, 'ssm_conv_w': 5.363624e-01, 'ssm_conv_b': 1.829083e+00, 'ssm_dt_bias': 7.238185e-01, 'ssm_a_log': 2.958522e+00, 'ssm_d': 3.491267e+00, 'ssm_norm_g': 9.814821e-01, 'w_br_att': 7.789545e-02, 'w_br_sg': 4.319644e-01, 'w_br_ssm': 1.285437e+00, 'w_out': 1.366755e+00}


def _to_microbatches(a, axis):
    t = _jnp.moveaxis(a, axis, 0)
    t = t.reshape((N_MICROBATCH, t.shape[0] // N_MICROBATCH) + t.shape[1:])
    return _jnp.moveaxis(t, 1, axis + 1)


def setup_inputs(seed: int = 0) -> dict:
    inp = _fwd_setup_inputs(seed)
    key = _jax.random.fold_in(_jax.random.key(seed), 7919)
    shape, _ = _output_shape()
    out = dict(inp)
    out["loss_target"] = _jax.random.normal(_jax.random.fold_in(key, 0), shape, _jnp.float32)
    for i, name in enumerate(TWIN_WEIGHTS):
        w = inp[name].astype(_jnp.float32)
        if MOMENT_SCALE is None:
            s = _jnp.sqrt(_jnp.mean(_jnp.square(w)) + 1e-30)
        else:
            s = MOMENT_SCALE[name]
        km, kv = _jax.random.split(_jax.random.fold_in(key, i + 1))
        out[name] = w
        out["m_" + name] = s * _jax.random.normal(km, w.shape, _jnp.float32)
        out["v_" + name] = (s * s) * _jax.random.uniform(kv, w.shape, _jnp.float32, 0.5, 1.5)
    if N_MICROBATCH > 1:
        for name, axis in PER_EXAMPLE_BATCH_AXIS.items():
            out[name] = _to_microbatches(out[name], axis)
    return {'x': out['x'], 'w_in': out['w_in'], 'norm_pre': out['norm_pre'], 'norm_post': out['norm_post'], 'rel_bias': out['rel_bias'], 'att_sinks': out['att_sinks'], 'sg_ln_g': out['sg_ln_g'], 'sg_ln_b': out['sg_ln_b'], 'sg_w': out['sg_w'], 'sg_b': out['sg_b'], 'ssm_conv_w': out['ssm_conv_w'], 'ssm_conv_b': out['ssm_conv_b'], 'ssm_dt_bias': out['ssm_dt_bias'], 'ssm_a_log': out['ssm_a_log'], 'ssm_d': out['ssm_d'], 'ssm_norm_g': out['ssm_norm_g'], 'w_br_att': out['w_br_att'], 'w_br_sg': out['w_br_sg'], 'w_br_ssm': out['w_br_ssm'], 'w_out': out['w_out'], 'loss_target': out['loss_target'], 'm_w_in': out['m_w_in'], 'm_norm_pre': out['m_norm_pre'], 'm_norm_post': out['m_norm_post'], 'm_rel_bias': out['m_rel_bias'], 'm_att_sinks': out['m_att_sinks'], 'm_sg_ln_g': out['m_sg_ln_g'], 'm_sg_ln_b': out['m_sg_ln_b'], 'm_sg_w': out['m_sg_w'], 'm_sg_b': out['m_sg_b'], 'm_ssm_conv_w': out['m_ssm_conv_w'], 'm_ssm_conv_b': out['m_ssm_conv_b'], 'm_ssm_dt_bias': out['m_ssm_dt_bias'], 'm_ssm_a_log': out['m_ssm_a_log'], 'm_ssm_d': out['m_ssm_d'], 'm_ssm_norm_g': out['m_ssm_norm_g'], 'm_w_br_att': out['m_w_br_att'], 'm_w_br_sg': out['m_w_br_sg'], 'm_w_br_ssm': out['m_w_br_ssm'], 'm_w_out': out['m_w_out'], 'v_w_in': out['v_w_in'], 'v_norm_pre': out['v_norm_pre'], 'v_norm_post': out['v_norm_post'], 'v_rel_bias': out['v_rel_bias'], 'v_att_sinks': out['v_att_sinks'], 'v_sg_ln_g': out['v_sg_ln_g'], 'v_sg_ln_b': out['v_sg_ln_b'], 'v_sg_w': out['v_sg_w'], 'v_sg_b': out['v_sg_b'], 'v_ssm_conv_w': out['v_ssm_conv_w'], 'v_ssm_conv_b': out['v_ssm_conv_b'], 'v_ssm_dt_bias': out['v_ssm_dt_bias'], 'v_ssm_a_log': out['v_ssm_a_log'], 'v_ssm_d': out['v_ssm_d'], 'v_ssm_norm_g': out['v_ssm_norm_g'], 'v_w_br_att': out['v_w_br_att'], 'v_w_br_sg': out['v_w_br_sg'], 'v_w_br_ssm': out['v_w_br_ssm'], 'v_w_out': out['v_w_out']}


def _loss(weights, diff, rest, loss_target):
    with _jax.named_scope("forward"):
        args = {**rest, TWIN_DIFF_INPUT: diff, **{k: w.astype(_WEIGHT_DTYPES[k]) for k, w in weights.items()}}
        y = _forward(args)
    with _jax.named_scope("loss_head"):
        err = _jnp.square(y.astype(_jnp.float32) - loss_target)
        return 0.5 * _jnp.sum(_jnp.mean(err, axis=-1)) if err.ndim else 0.5 * err


def _adamw(w, g, m, v):
    m = ADAM_B1 * m + (1.0 - ADAM_B1) * g
    v = ADAM_B2 * v + (1.0 - ADAM_B2) * _jnp.square(g)
    m_hat = m / (1.0 - ADAM_B1 ** ADAM_STEP)
    v_hat = v / (1.0 - ADAM_B2 ** ADAM_STEP)
    delta = -ADAM_LR * (m_hat / (_jnp.sqrt(v_hat) + ADAM_EPS) + ADAM_WD * w)
    return delta, m, v


def reference(x, w_in, norm_pre, norm_post, rel_bias, att_sinks, sg_ln_g, sg_ln_b, sg_w, sg_b, ssm_conv_w, ssm_conv_b, ssm_dt_bias, ssm_a_log, ssm_d, ssm_norm_g, w_br_att, w_br_sg, w_br_ssm, w_out, loss_target, m_w_in, m_norm_pre, m_norm_post, m_rel_bias, m_att_sinks, m_sg_ln_g, m_sg_ln_b, m_sg_w, m_sg_b, m_ssm_conv_w, m_ssm_conv_b, m_ssm_dt_bias, m_ssm_a_log, m_ssm_d, m_ssm_norm_g, m_w_br_att, m_w_br_sg, m_w_br_ssm, m_w_out, v_w_in, v_norm_pre, v_norm_post, v_rel_bias, v_att_sinks, v_sg_ln_g, v_sg_ln_b, v_sg_w, v_sg_b, v_ssm_conv_w, v_ssm_conv_b, v_ssm_dt_bias, v_ssm_a_log, v_ssm_d, v_ssm_norm_g, v_w_br_att, v_w_br_sg, v_w_br_ssm, v_w_out):
    given = dict(x=x, w_in=w_in, norm_pre=norm_pre, norm_post=norm_post, rel_bias=rel_bias, att_sinks=att_sinks, sg_ln_g=sg_ln_g, sg_ln_b=sg_ln_b, sg_w=sg_w, sg_b=sg_b, ssm_conv_w=ssm_conv_w, ssm_conv_b=ssm_conv_b, ssm_dt_bias=ssm_dt_bias, ssm_a_log=ssm_a_log, ssm_d=ssm_d, ssm_norm_g=ssm_norm_g, w_br_att=w_br_att, w_br_sg=w_br_sg, w_br_ssm=w_br_ssm, w_out=w_out, loss_target=loss_target, m_w_in=m_w_in, m_norm_pre=m_norm_pre, m_norm_post=m_norm_post, m_rel_bias=m_rel_bias, m_att_sinks=m_att_sinks, m_sg_ln_g=m_sg_ln_g, m_sg_ln_b=m_sg_ln_b, m_sg_w=m_sg_w, m_sg_b=m_sg_b, m_ssm_conv_w=m_ssm_conv_w, m_ssm_conv_b=m_ssm_conv_b, m_ssm_dt_bias=m_ssm_dt_bias, m_ssm_a_log=m_ssm_a_log, m_ssm_d=m_ssm_d, m_ssm_norm_g=m_ssm_norm_g, m_w_br_att=m_w_br_att, m_w_br_sg=m_w_br_sg, m_w_br_ssm=m_w_br_ssm, m_w_out=m_w_out, v_w_in=v_w_in, v_norm_pre=v_norm_pre, v_norm_post=v_norm_post, v_rel_bias=v_rel_bias, v_att_sinks=v_att_sinks, v_sg_ln_g=v_sg_ln_g, v_sg_ln_b=v_sg_ln_b, v_sg_w=v_sg_w, v_sg_b=v_sg_b, v_ssm_conv_w=v_ssm_conv_w, v_ssm_conv_b=v_ssm_conv_b, v_ssm_dt_bias=v_ssm_dt_bias, v_ssm_a_log=v_ssm_a_log, v_ssm_d=v_ssm_d, v_ssm_norm_g=v_ssm_norm_g, v_w_br_att=v_w_br_att, v_w_br_sg=v_w_br_sg, v_w_br_ssm=v_w_br_ssm, v_w_out=v_w_out)
    weights = {n: given[n] for n in TWIN_WEIGHTS}
    shared = {n: given[n] for n in SHARED_INPUTS}
    per_example = {n: given[n] for n in ['x']}
    grad_fn = _jax.value_and_grad(_loss, argnums=(0, 1))

    def one_microbatch(ex, loss_target):
        ex = dict(ex)
        diff = ex.pop(TWIN_DIFF_INPUT)
        return grad_fn(weights, diff, {**shared, **ex}, loss_target)

    if N_MICROBATCH == 1:
        loss, (grad_w, grad_x) = one_microbatch(per_example, given["loss_target"])
    else:
        def body(carry, xs):
            loss_sum, grad_sum = carry
            l_k, (gw_k, gx_k) = one_microbatch(xs[0], xs[1])
            with _jax.named_scope("update"):
                return (loss_sum + l_k, _jax.tree.map(_jnp.add, grad_sum, gw_k)), gx_k

        init = (_jnp.zeros((), _jnp.float32), _jax.tree.map(_jnp.zeros_like, weights))
        (loss, grad_w), grad_x = _jax.lax.scan(body, init, (per_example, given["loss_target"]))
    with _jax.named_scope("update"):
        delta_w, new_m, new_v = {}, {}, {}
        for n in TWIN_WEIGHTS:
            delta_w[n], new_m[n], new_v[n] = _adamw(weights[n], grad_w[n], given["m_" + n], given["v_" + n])
    return (loss, grad_x, *[grad_w[n] for n in TWIN_WEIGHTS], *[delta_w[n] for n in TWIN_WEIGHTS],
            *[new_m[n] for n in TWIN_WEIGHTS], *[new_v[n] for n in TWIN_WEIGHTS])
```

```python
import math

import jax
import jax.numpy as jnp
from jax import lax
from jax.experimental import pallas as pl
from jax.experimental.pallas import tpu as pltpu

F32 = jnp.float32
MXU_DTYPE = jnp.bfloat16
ACT_DTYPE = jnp.bfloat16
WIRE_DTYPE = jnp.bfloat16
HI = lax.Precision.HIGHEST
MESH = pl.DeviceIdType.MESH

D_MODEL = 1024
N_DEV = 8
ATT_HEADS = 16
HEAD_DIM = 64
BLK = 128
SG_GROUPS = 8
SSM_WIDTH = 2048
SSM_HEADS = 32
SSM_GROUPS = 4
SSM_GW = SSM_WIDTH // SSM_GROUPS
CONV_DIM = 3072
REL_BUCKETS = 32
EPS = 1e-6
NEG = -1e30

ATT_COLS = 2304
SG_COLS = 3072
SSM_COLS = 5376
GATE_COLS = 3072
DT_OFF = 5120

VMEM_LIMIT_V7X = 56 * 2 ** 20

ADAM_LR, ADAM_B1, ADAM_B2, ADAM_EPS, ADAM_WD, ADAM_STEP = 0.001, 0.9, 0.999, 1e-08, 0.01, 10

BIG_ROWS = 4800
WIN_ROWS = 1712
LAYER_ROWS = WIN_ROWS + 128 + 128 + 256 + 128
SMALL_ROWS = 384
ADAM_TILE = 192
SMALL_TILE = 128


def _cparams(sem=None):
    return pltpu.CompilerParams(dimension_semantics=sem, vmem_limit_bytes=VMEM_LIMIT_V7X)


def _dot(a, b):
    return jnp.dot(a.astype(MXU_DTYPE), b.astype(MXU_DTYPE), preferred_element_type=F32)


def _dot_nt(a, b):
    return lax.dot_general(a.astype(MXU_DTYPE), b.astype(MXU_DTYPE), (((1,), (1,)), ((), ())),
                           preferred_element_type=F32)


def _dot_tn(a, b):
    return lax.dot_general(a.astype(MXU_DTYPE), b.astype(MXU_DTYPE), (((0,), (0,)), ((), ())),
                           preferred_element_type=F32)


def _dot_hi(a, b):
    return jnp.dot(a, b, precision=HI, preferred_element_type=F32)


def _dot_hi_nt(a, b):
    return lax.dot_general(a, b, (((1,), (1,)), ((), ())), precision=HI, preferred_element_type=F32)


def _sig(x):
    return 1.0 / (1.0 + jnp.exp(-x))


def _dsilu(x, s):
    return s * (1.0 + x * (1.0 - s))


def _full(shape):
    nd = len(shape)
    return pl.BlockSpec(shape, lambda *_: (0,) * nd)


def rmsnorm_fwd(x, g):
    s, d = x.shape
    tm = min(512, s)

    def body(x_ref, g_ref, o_ref):
        xv = x_ref[...]
        r = lax.rsqrt(jnp.mean(xv * xv, axis=-1, keepdims=True) + EPS)
        o_ref[...] = (xv * r * g_ref[...]).astype(o_ref.dtype)

    return pl.pallas_call(
        body, name="rmsnorm_fwd", grid=(s // tm,),
        in_specs=[pl.BlockSpec((tm, d), lambda i: (i, 0)), _full((1, d))],
        out_specs=pl.BlockSpec((tm, d), lambda i: (i, 0)),
        out_shape=jax.ShapeDtypeStruct((s, d), ACT_DTYPE),
        compiler_params=_cparams(("parallel",)),
    )(x, g)


def mm_nn(a, b, tn, name):
    s, k = a.shape
    n = b.shape[1]
    tm = min(1024, s)

    def body(a_ref, b_ref, o_ref):
        o_ref[...] = _dot(a_ref[...], b_ref[...]).astype(o_ref.dtype)

    return pl.pallas_call(
        body, name=name, grid=(s // tm, n // tn),
        in_specs=[pl.BlockSpec((tm, k), lambda i, j: (i, 0)), pl.BlockSpec((k, tn), lambda i, j: (0, j))],
        out_specs=pl.BlockSpec((tm, tn), lambda i, j: (i, j)),
        out_shape=jax.ShapeDtypeStruct((s, n), ACT_DTYPE),
        compiler_params=_cparams(("parallel", "arbitrary")),
    )(a, b)


def mm_tn(a, b, tn, name):
    s, k = a.shape
    n = b.shape[1]
    ts = min(512, s)

    def body(a_ref, b_ref, o_ref):
        @pl.when(pl.program_id(1) == 0)
        def _():
            o_ref[...] = jnp.zeros_like(o_ref)

        o_ref[...] += _dot_tn(a_ref[...], b_ref[...])

    return pl.pallas_call(
        body, name=name, grid=(n // tn, s // ts),
        in_specs=[pl.BlockSpec((ts, k), lambda j, t: (t, 0)), pl.BlockSpec((ts, tn), lambda j, t: (t, j))],
        out_specs=pl.BlockSpec((k, tn), lambda j, t: (0, j)),
        out_shape=jax.ShapeDtypeStruct((k, n), F32),
        compiler_params=_cparams(("parallel", "arbitrary")),
    )(a, b)


def dh_norm_bwd(dslabs, wslabs, x, g, dres):
    s, d = x.shape
    tm = min(512, s)
    tk = 768
    counts = [ds.shape[1] // tk for ds in dslabs]
    starts = [sum(counts[:i]) for i in range(len(counts))]
    nk = sum(counts)
    ns = len(dslabs)

    def body(*refs):
        d_refs, w_refs = refs[:ns], refs[ns:2 * ns]
        x_ref, g_ref, dres_ref, dx_ref, dg_ref, acc_ref = refs[2 * ns:]
        i, k = pl.program_id(0), pl.program_id(1)

        @pl.when(k == 0)
        def _():
            acc_ref[...] = jnp.zeros_like(acc_ref)

        for q in range(ns):
            @pl.when((k >= starts[q]) & (k < starts[q] + counts[q]))
            def _(q=q):
                acc_ref[...] += _dot_nt(d_refs[q][...], w_refs[q][...])

        @pl.when((i == 0) & (k == 0))
        def _():
            dg_ref[...] = jnp.zeros_like(dg_ref)

        @pl.when(k == nk - 1)
        def _():
            xv = x_ref[...]
            r = lax.rsqrt(jnp.mean(xv * xv, axis=-1, keepdims=True) + EPS)
            xn = xv * r
            dh = acc_ref[...]
            dg_ref[...] += jnp.sum(dh * xn, axis=0, keepdims=True)
            dxn = dh * g_ref[...]
            dx_ref[...] = dres_ref[...] + r * (dxn - xn * jnp.mean(dxn * xn, axis=-1, keepdims=True))

    def clamp(q):
        return lambda i, k: (i, jnp.clip(k - starts[q], 0, counts[q] - 1))

    def clamp_w(q):
        return lambda i, k: (0, jnp.clip(k - starts[q], 0, counts[q] - 1))

    in_specs = ([pl.BlockSpec((tm, tk), clamp(q)) for q in range(ns)]
                + [pl.BlockSpec((d, tk), clamp_w(q)) for q in range(ns)]
                + [pl.BlockSpec((tm, d), lambda i, k: (i, 0)), _full((1, d)),
                   pl.BlockSpec((tm, d), lambda i, k: (i, 0))])
    return pl.pallas_call(
        body, name="dh_norm_bwd", grid=(s // tm, nk), in_specs=in_specs,
        out_specs=[pl.BlockSpec((tm, d), lambda i, k: (i, 0)), _full((1, d))],
        out_shape=[jax.ShapeDtypeStruct((s, d), F32), jax.ShapeDtypeStruct((1, d), F32)],
        scratch_shapes=[pltpu.VMEM((tm, d), F32)],
        compiler_params=_cparams(("arbitrary", "arbitrary")),
    )(*dslabs, *wslabs, x, g, dres)


def bias_table(rel_bias_t, onehot_t, maskadd):
    n = onehot_t.shape[1]
    tn = 8192

    def body(r_ref, o_ref, m_ref, out_ref):
        out_ref[...] = _dot_hi(r_ref[...], o_ref[...]) + m_ref[...]

    return pl.pallas_call(
        body, name="bias_table", grid=(n // tn,),
        in_specs=[_full((ATT_HEADS, REL_BUCKETS)), pl.BlockSpec((REL_BUCKETS, tn), lambda j: (0, j)),
                  pl.BlockSpec((1, tn), lambda j: (0, j))],
        out_specs=pl.BlockSpec((ATT_HEADS, tn), lambda j: (0, j)),
        out_shape=jax.ShapeDtypeStruct((ATT_HEADS, n), F32),
        compiler_params=_cparams(("parallel",)),
    )(rel_bias_t, onehot_t, maskadd)


def bias_table_bwd(dbias, onehot_t):
    n = onehot_t.shape[1]
    tn = 8192

    def body(d_ref, o_ref, out_ref):
        @pl.when(pl.program_id(0) == 0)
        def _():
            out_ref[...] = jnp.zeros_like(out_ref)

        out_ref[...] += _dot_hi_nt(d_ref[...], o_ref[...])

    return pl.pallas_call(
        body, name="bias_table_bwd", grid=(n // tn,),
        in_specs=[pl.BlockSpec((ATT_HEADS, tn), lambda j: (0, j)), pl.BlockSpec((REL_BUCKETS, tn), lambda j: (0, j))],
        out_specs=_full((ATT_HEADS, REL_BUCKETS)),
        out_shape=jax.ShapeDtypeStruct((ATT_HEADS, REL_BUCKETS), F32),
        compiler_params=_cparams(("arbitrary",)),
    )(dbias, onehot_t)


def _att_head(q, kcat, vcat, bias_h, sink, first_mask):
    l = _dot_nt(q, kcat) * (HEAD_DIM ** -0.5) + bias_h
    l = jnp.where(first_mask, NEG, l)
    m = jnp.maximum(jnp.max(l, axis=1, keepdims=True), sink)
    p = jnp.exp(l - m)
    es = jnp.exp(sink - m)
    inv = 1.0 / (jnp.sum(p, axis=1, keepdims=True) + es)
    p = p * inv
    return p, es * inv, _dot(p, vcat)


def _kv_cat(kvp, kvc, g):
    lo = g * HEAD_DIM
    kcat = jnp.concatenate([kvp[:, lo:lo + HEAD_DIM], kvc[:, lo:lo + HEAD_DIM]], axis=0)
    vcat = jnp.concatenate([kvp[:, 128 + lo:128 + lo + HEAD_DIM], kvc[:, 128 + lo:128 + lo + HEAD_DIM]], axis=0)
    return kcat, vcat


def attn_fwd(pa, bias, sinks):
    s = pa.shape[0]
    nb = s // BLK

    def body(pa_ref, kvp_ref, bias_ref, sink_ref, y_ref, o_scr):
        n = pl.program_id(0)
        q = pa_ref[:, 0:1024].astype(F32)
        z = pa_ref[:, 1024:2048].astype(F32)
        kvc = pa_ref[:, 2048:2304].astype(F32)
        kvp = kvp_ref[...].astype(F32)
        col = lax.broadcasted_iota(jnp.int32, (BLK, 2 * BLK), 1)
        first_mask = (n == 0) & (col < BLK)
        for g in range(2):
            kcat, vcat = _kv_cat(kvp, kvc, g)
            for j in range(8):
                h = g * 8 + j
                _, _, o = _att_head(q[:, h * 64:(h + 1) * 64], kcat, vcat, bias_ref[h],
                                    sink_ref[0:1, h:h + 1], first_mask)
                o_scr[:, h * 64:(h + 1) * 64] = o
        y_ref[...] = (o_scr[...] * z * _sig(z)).astype(y_ref.dtype)

    return pl.pallas_call(
        body, name="attn_fwd", grid=(nb,),
        in_specs=[pl.BlockSpec((BLK, ATT_COLS), lambda n: (n, 0)),
                  pl.BlockSpec((BLK, 256), lambda n: (jnp.maximum(n - 1, 0), 8)),
                  _full((ATT_HEADS, BLK, 2 * BLK)), _full((1, 128))],
        out_specs=pl.BlockSpec((BLK, 1024), lambda n: (n, 0)),
        out_shape=jax.ShapeDtypeStruct((s, 1024), ACT_DTYPE),
        scratch_shapes=[pltpu.VMEM((BLK, 1024), F32)],
        compiler_params=_cparams(("arbitrary",)),
    )(pa, pa, bias, sinks)


def attn_bwd(pa, dy, bias, sinks, dbias_in):
    s = pa.shape[0]
    nb = s // BLK

    def body(pa_ref, kvp_ref, dy_ref, bias_ref, sink_ref, dbin_ref,
             dpa_ref, dbias_ref, dsink_ref, carry, cur, prv, dq_scr, dz_scr):
        i = pl.program_id(0)
        n = nb - 1 - i

        @pl.when(i == 0)
        def _():
            dbias_ref[...] = dbin_ref[...]
            dsink_ref[...] = jnp.zeros_like(dsink_ref)
            carry[...] = jnp.zeros_like(carry)

        q = pa_ref[:, 0:1024].astype(F32)
        z = pa_ref[:, 1024:2048].astype(F32)
        kvc = pa_ref[:, 2048:2304].astype(F32)
        kvp = kvp_ref[...].astype(F32)
        dy = dy_ref[...].astype(F32)
        sz = _sig(z)
        d_o = dy * z * sz
        dzf = dy * _dsilu(z, sz)
        col = lax.broadcasted_iota(jnp.int32, (BLK, 2 * BLK), 1)
        first_mask = (n == 0) & (col < BLK)
        lane = lax.broadcasted_iota(jnp.int32, (1, 128), 1)
        dsink = jnp.zeros((1, 128), F32)
        scale = HEAD_DIM ** -0.5
        for g in range(2):
            kcat, vcat = _kv_cat(kvp, kvc, g)
            dk = jnp.zeros((2 * BLK, HEAD_DIM), F32)
            dv = jnp.zeros((2 * BLK, HEAD_DIM), F32)
            for j in range(8):
                h = g * 8 + j
                sl = slice(h * 64, (h + 1) * 64)
                qh = q[:, sl]
                p, psink, o = _att_head(qh, kcat, vcat, bias_ref[h], sink_ref[0:1, h:h + 1], first_mask)
                doh = d_o[:, sl]
                dz_scr[:, sl] = dzf[:, sl] * o
                delta = jnp.sum(doh * o, axis=1, keepdims=True)
                dl = p * (_dot_nt(doh, vcat) - delta)
                dsink = dsink + jnp.where(lane == h, -jnp.sum(psink * delta, axis=0, keepdims=True), 0.0)
                dbias_ref[h] += dl
                dq_scr[:, sl] = _dot(dl, kcat) * scale
                dk = dk + _dot_tn(dl, qh) * scale
                dv = dv + _dot_tn(p, doh)
            lo = g * HEAD_DIM
            prv[:, lo:lo + 64] = dk[0:BLK]
            cur[:, lo:lo + 64] = dk[BLK:2 * BLK]
            prv[:, 128 + lo:128 + lo + 64] = dv[0:BLK]
            cur[:, 128 + lo:128 + lo + 64] = dv[BLK:2 * BLK]
        dsink_ref[...] += dsink
        dpa_ref[:, 0:1024] = dq_scr[...].astype(dpa_ref.dtype)
        dpa_ref[:, 1024:2048] = dz_scr[...].astype(dpa_ref.dtype)
        dpa_ref[:, 2048:2304] = (cur[...] + carry[...]).astype(dpa_ref.dtype)
        carry[...] = prv[...]

    return pl.pallas_call(
        body, name="attn_bwd", grid=(nb,),
        in_specs=[pl.BlockSpec((BLK, ATT_COLS), lambda i: (nb - 1 - i, 0)),
                  pl.BlockSpec((BLK, 256), lambda i: (jnp.maximum(nb - 2 - i, 0), 8)),
                  pl.BlockSpec((BLK, 1024), lambda i: (nb - 1 - i, 0)),
                  _full((ATT_HEADS, BLK, 2 * BLK)), _full((1, 128)), _full((ATT_HEADS, BLK, 2 * BLK))],
        out_specs=[pl.BlockSpec((BLK, ATT_COLS), lambda i: (nb - 1 - i, 0)),
                   _full((ATT_HEADS, BLK, 2 * BLK)), _full((1, 128))],
        out_shape=[jax.ShapeDtypeStruct((s, ATT_COLS), ACT_DTYPE),
                   jax.ShapeDtypeStruct((ATT_HEADS, BLK, 2 * BLK), F32),
                   jax.ShapeDtypeStruct((1, 128), F32)],
        scratch_shapes=[pltpu.VMEM((BLK, 256), F32), pltpu.VMEM((BLK, 256), F32), pltpu.VMEM((BLK, 256), F32),
                        pltpu.VMEM((BLK, 1024), F32), pltpu.VMEM((BLK, 1024), F32)],
        compiler_params=_cparams(("arbitrary",)),
    )(pa, pa, dy, bias, sinks, dbias_in)


def _layernorm(v, g, b):
    mu = jnp.mean(v, axis=-1, keepdims=True)
    vc = v - mu
    rstd = lax.rsqrt(jnp.mean(vc * vc, axis=-1, keepdims=True) + EPS)
    xhat = vc * rstd
    return xhat, rstd, xhat * g + b


def sgu_fwd(ps, ln_g, ln_b, w_tril, b_t):
    s = ps.shape[0]

    def body(ps_ref, g_ref, b_ref, w_ref, bt_ref, y_ref):
        u = ps_ref[:, 0:1024].astype(F32)
        v = ps_ref[:, 1024:2048].astype(F32)
        z = ps_ref[:, 2048:3072].astype(F32)
        _, _, vn = _layernorm(v, g_ref[...], b_ref[...])
        gate = u * z * _sig(z)
        for g in range(SG_GROUPS):
            sl = slice(g * 128, (g + 1) * 128)
            mixed = _dot(w_ref[g], vn[:, sl]) + bt_ref[:, g:g + 1]
            y_ref[:, sl] = (gate[:, sl] * mixed).astype(y_ref.dtype)

    return pl.pallas_call(
        body, name="sgu_fwd", grid=(s // BLK,),
        in_specs=[pl.BlockSpec((BLK, SG_COLS), lambda c: (c, 0)), _full((1, 1024)), _full((1, 1024)),
                  _full((SG_GROUPS, BLK, BLK)), _full((BLK, 128))],
        out_specs=pl.BlockSpec((BLK, 1024), lambda c: (c, 0)),
        out_shape=jax.ShapeDtypeStruct((s, 1024), ACT_DTYPE),
        compiler_params=_cparams(("parallel",)),
    )(ps, ln_g, ln_b, w_tril, b_t)


def sgu_bwd(ps, dy, ln_g, ln_b, w_tril, w_tril_t, b_t):
    s = ps.shape[0]

    def body(ps_ref, dy_ref, g_ref, b_ref, w_ref, wt_ref, bt_ref, dps_ref, dw_ref, dbt_ref, dg_ref, db_ref, dvn_scr):
        @pl.when(pl.program_id(0) == 0)
        def _():
            dw_ref[...] = jnp.zeros_like(dw_ref)
            dbt_ref[...] = jnp.zeros_like(dbt_ref)
            dg_ref[...] = jnp.zeros_like(dg_ref)
            db_ref[...] = jnp.zeros_like(db_ref)

        u = ps_ref[:, 0:1024].astype(F32)
        v = ps_ref[:, 1024:2048].astype(F32)
        z = ps_ref[:, 2048:3072].astype(F32)
        dy = dy_ref[...].astype(F32)
        xhat, rstd, vn = _layernorm(v, g_ref[...], b_ref[...])
        sz = _sig(z)
        silu = z * sz
        row = lax.broadcasted_iota(jnp.int32, (BLK, BLK), 0)
        colm = lax.broadcasted_iota(jnp.int32, (BLK, BLK), 1)
        tril = row >= colm
        dbt = jnp.zeros((BLK, 128), F32)
        for g in range(SG_GROUPS):
            sl = slice(g * 128, (g + 1) * 128)
            vng = vn[:, sl]
            mixed = _dot(w_ref[g], vng) + bt_ref[:, g:g + 1]
            dyg, ug = dy[:, sl], u[:, sl]
            dps_ref[:, sl] = (dyg * mixed * silu[:, sl]).astype(dps_ref.dtype)
            dps_ref[:, 2048 + g * 128:2048 + (g + 1) * 128] = (
                dyg * ug * mixed * _dsilu(z[:, sl], sz[:, sl])).astype(dps_ref.dtype)
            dm = dyg * ug * silu[:, sl]
            dw_ref[g] += jnp.where(tril, _dot_nt(dm, vng), 0.0)
            dbt = dbt + jnp.where(colm == g, jnp.sum(dm, axis=1, keepdims=True), 0.0)
            dvn_scr[:, sl] = _dot(wt_ref[g], dm)
        dbt_ref[...] += dbt
        dvn = dvn_scr[...]
        dg_ref[...] += jnp.sum(dvn * xhat, axis=0, keepdims=True)
        db_ref[...] += jnp.sum(dvn, axis=0, keepdims=True)
        dxh = dvn * g_ref[...]
        dv = rstd * (dxh - jnp.mean(dxh, axis=-1, keepdims=True)
                     - xhat * jnp.mean(dxh * xhat, axis=-1, keepdims=True))
        dps_ref[:, 1024:2048] = dv.astype(dps_ref.dtype)

    return pl.pallas_call(
        body, name="sgu_bwd", grid=(s // BLK,),
        in_specs=[pl.BlockSpec((BLK, SG_COLS), lambda c: (c, 0)), pl.BlockSpec((BLK, 1024), lambda c: (c, 0)),
                  _full((1, 1024)), _full((1, 1024)), _full((SG_GROUPS, BLK, BLK)), _full((SG_GROUPS, BLK, BLK)),
                  _full((BLK, 128))],
        out_specs=[pl.BlockSpec((BLK, SG_COLS), lambda c: (c, 0)), _full((SG_GROUPS, BLK, BLK)), _full((BLK, 128)),
                   _full((1, 1024)), _full((1, 1024))],
        out_shape=[jax.ShapeDtypeStruct((s, SG_COLS), ACT_DTYPE), jax.ShapeDtypeStruct((SG_GROUPS, BLK, BLK), F32),
                   jax.ShapeDtypeStruct((BLK, 128), F32), jax.ShapeDtypeStruct((1, 1024), F32),
                   jax.ShapeDtypeStruct((1, 1024), F32)],
        scratch_shapes=[pltpu.VMEM((BLK, 1024), F32)],
        compiler_params=_cparams(("arbitrary",)),
    )(ps, dy, ln_g, ln_b, w_tril, w_tril_t, b_t)


def _shift_down(cur, prev16, k):
    if k == 0:
        return cur
    r = pltpu.roll(cur, k, 0)
    rp = pltpu.roll(prev16, k, 0)
    row = lax.broadcasted_iota(jnp.int32, (8, cur.shape[1]), 0)
    return jnp.concatenate([jnp.where(row < k, rp[0:8], r[0:8]), r[8:]], axis=0)


def _shift_up(cur, next16, k):
    if k == 0:
        return cur
    n = cur.shape[0]
    r = pltpu.roll(cur, n - k, 0)
    rn = pltpu.roll(next16, 16 - k, 0)
    row = lax.broadcasted_iota(jnp.int32, (8, cur.shape[1]), 0)
    return jnp.concatenate([r[:n - 8], jnp.where(row >= 8 - k, rn[8:16], r[n - 8:])], axis=0)


def _bcast8(v):
    return jnp.broadcast_to(v, (8, v.shape[1]))


class _Ssd:
    def __init__(self, xbc, prev16, dtr, cw, cbias, dtb, alog, dsk, tri, e):
        pre = cbias + cw[3:4] * xbc
        self.shifted = [xbc]
        for k in (1, 2, 3):
            sh = _shift_down(xbc, prev16, k)
            self.shifted.append(sh)
            pre = pre + cw[3 - k:4 - k] * sh
        self.pre = pre
        self.sg = _sig(pre)
        act = pre * self.sg
        self.xs = act[:, 0:SSM_WIDTH]
        self.bm = act[:, SSM_WIDTH:SSM_WIDTH + 512]
        self.cm = act[:, SSM_WIDTH + 512:CONV_DIM]
        self.dtp = dtr + dtb
        self.dt = jnp.maximum(self.dtp, 0.0) + jnp.log(1.0 + jnp.exp(-jnp.abs(self.dtp)))
        self.a = -jnp.exp(alog)
        self.acs = _dot_hi(tri, self.dt * self.a)
        self.acs_t = self.acs.T
        tot = self.acs[BLK - 1:BLK]
        self.ecs = jnp.exp(self.acs)
        self.dte = jnp.exp(tot - self.acs)
        self.cd = jnp.exp(tot)
        self.dt_x = _dot_hi(self.dt, e)
        self.ecs_x = _dot_hi(self.ecs, e)
        self.dte_x = _dot_hi(self.dte, e)
        self.cd_x = _dot_hi(_bcast8(self.cd), e)[0:1]
        self.d_x = _dot_hi(_bcast8(dsk), e)[0:1]
        self.xdt = self.xs * self.dt_x
        row = lax.broadcasted_iota(jnp.int32, (BLK, BLK), 0)
        col = lax.broadcasted_iota(jnp.int32, (BLK, BLK), 1)
        self.tril = row >= col

    def group(self, g):
        sl = slice(g * 128, (g + 1) * 128)
        bg, cg = self.bm[:, sl], self.cm[:, sl]
        return bg, cg, _dot_nt(cg, bg)

    def decay(self, h):
        seg = self.acs[:, h:h + 1] - self.acs_t[h:h + 1, :]
        return jnp.exp(jnp.where(self.tril, seg, NEG))

    def y_pre_gate(self, ht_of, yd_scr, yoff_scr):
        for g in range(SSM_GROUPS):
            bg, cg, cb = self.group(g)
            for j in range(8):
                h = g * 8 + j
                sl = slice(h * 64, (h + 1) * 64)
                yd_scr[:, sl] = _dot(cb * self.decay(h), self.xdt[:, sl])
            gs = slice(g * SSM_GW, (g + 1) * SSM_GW)
            yoff_scr[:, gs] = _dot(cg, ht_of(g)) * self.ecs_x[:, gs]
        return yd_scr[...] + yoff_scr[...] + self.d_x * self.xs


def _ssd_consts():
    hh = lax.broadcasted_iota(jnp.int32, (128, SSM_WIDTH), 0)
    ch = lax.broadcasted_iota(jnp.int32, (128, SSM_WIDTH), 1)
    e = (ch // 64 == hh).astype(F32)
    row = lax.broadcasted_iota(jnp.int32, (BLK, BLK), 0)
    col = lax.broadcasted_iota(jnp.int32, (BLK, BLK), 1)
    tri = (row >= col).astype(F32)
    return tri, e


def _pad_lanes(v, n=128):
    return jnp.pad(v, ((0, 0), (0, n - v.shape[1])))


def ssd_fwd(pm, cw, cbias, dtb, alog, dsk, ng):
    s = pm.shape[0]
    nc = s // BLK
    tri, e = _ssd_consts()

    def body(pm_ref, prev_ref, cw_ref, cb_ref, dtb_ref, al_ref, d_ref, ng_ref, tri_ref, e_ref,
             y_ref, st_ref, ht_ref, yd_scr, yoff_scr):
        c = pl.program_id(0)

        @pl.when(c == 0)
        def _():
            ht_ref[...] = jnp.zeros_like(ht_ref)

        xbc = pm_ref[:, 0:CONV_DIM].astype(F32)
        prev16 = jnp.where(c == 0, 0.0, prev_ref[...].astype(F32))
        f = _Ssd(xbc, prev16, pm_ref[:, DT_OFF:DT_OFF + 128].astype(F32), cw_ref[...], cb_ref[...], dtb_ref[...],
                 al_ref[...], d_ref[...], tri_ref[...], e_ref[...])
        st_ref[0] = ht_ref[...]
        y = f.y_pre_gate(lambda g: ht_ref[g], yd_scr, yoff_scr)
        for g in range(SSM_GROUPS):
            bg, _, _ = f.group(g)
            gs = slice(g * SSM_GW, (g + 1) * SSM_GW)
            ht_ref[g] = ht_ref[g] * f.cd_x[:, gs] + _dot_tn(bg, f.xdt[:, gs] * f.dte_x[:, gs])
        z = pm_ref[:, CONV_DIM:CONV_DIM + SSM_WIDTH].astype(F32)
        ypre = y * z * _sig(z)
        for g in range(SSM_GROUPS):
            gs = slice(g * SSM_GW, (g + 1) * SSM_GW)
            yg = ypre[:, gs]
            rr = lax.rsqrt(jnp.mean(yg * yg, axis=-1, keepdims=True) + EPS)
            y_ref[:, gs] = (yg * rr * ng_ref[:, gs]).astype(y_ref.dtype)

    return pl.pallas_call(
        body, name="ssd_fwd", grid=(nc,),
        in_specs=[pl.BlockSpec((BLK, SSM_COLS), lambda c: (c, 0)),
                  pl.BlockSpec((16, CONV_DIM), lambda c: (jnp.maximum(8 * c - 1, 0), 0)),
                  _full((4, CONV_DIM)), _full((1, CONV_DIM)), _full((1, 128)), _full((1, 128)), _full((1, 128)),
                  _full((1, SSM_WIDTH)), _full((BLK, BLK)), _full((128, SSM_WIDTH))],
        out_specs=[pl.BlockSpec((BLK, SSM_WIDTH), lambda c: (c, 0)),
                   pl.BlockSpec((1, SSM_GROUPS, 128, SSM_GW), lambda c: (c, 0, 0, 0))],
        out_shape=[jax.ShapeDtypeStruct((s, SSM_WIDTH), ACT_DTYPE),
                   jax.ShapeDtypeStruct((nc, SSM_GROUPS, 128, SSM_GW), F32)],
        scratch_shapes=[pltpu.VMEM((SSM_GROUPS, 128, SSM_GW), F32), pltpu.VMEM((BLK, SSM_WIDTH), F32),
                        pltpu.VMEM((BLK, SSM_WIDTH), F32)],
        compiler_params=_cparams(("arbitrary",)),
    )(pm, pm, cw, cbias, dtb, alog, dsk, ng, tri, e)


def ssd_bwd(pm, dy, states, cw, cbias, dtb, alog, dsk, ng):
    s = pm.shape[0]
    nc = s // BLK
    tri, e = _ssd_consts()
    tri_t, e_t = tri.T, e.T

    def body(pm_ref, prev_ref, dy_ref, st_ref, cw_ref, cb_ref, dtb_ref, al_ref, d_ref, ng_ref,
             tri_ref, trit_ref, e_ref, et_ref,
             dpm_ref, dcw_ref, dcb_ref, dvec_ref, dng_ref,
             dht_ref, dcar_ref, yd_scr, yoff_scr, dx_scr, r2_scr, hs_scr, da_scr, dat_scr, dd_scr, dbc_scr):
        i = pl.program_id(0)
        n = nc - 1 - i

        @pl.when(i == 0)
        def _():
            dht_ref[...] = jnp.zeros_like(dht_ref)
            dcar_ref[...] = jnp.zeros_like(dcar_ref)
            dcw_ref[...] = jnp.zeros_like(dcw_ref)
            dcb_ref[...] = jnp.zeros_like(dcb_ref)
            dvec_ref[...] = jnp.zeros_like(dvec_ref)
            dng_ref[...] = jnp.zeros_like(dng_ref)
            dd_scr[...] = jnp.zeros_like(dd_scr)
            da_scr[...] = jnp.zeros_like(da_scr)
            dat_scr[...] = jnp.zeros_like(dat_scr)

        xbc = pm_ref[:, 0:CONV_DIM].astype(F32)
        prev16 = jnp.where(n == 0, 0.0, prev_ref[...].astype(F32))
        cw = cw_ref[...]
        f = _Ssd(xbc, prev16, pm_ref[:, DT_OFF:DT_OFF + 128].astype(F32), cw, cb_ref[...], dtb_ref[...],
                 al_ref[...], d_ref[...], tri_ref[...], e_ref[...])
        et = et_ref[...]
        y = f.y_pre_gate(lambda g: st_ref[0, g], yd_scr, yoff_scr)

        z = pm_ref[:, CONV_DIM:CONV_DIM + SSM_WIDTH].astype(F32)
        dyv = dy_ref[...].astype(F32)
        sz = _sig(z)
        silu = z * sz
        ypre = y * silu
        for g in range(SSM_GROUPS):
            gs = slice(g * SSM_GW, (g + 1) * SSM_GW)
            yg = ypre[:, gs]
            rr = lax.rsqrt(jnp.mean(yg * yg, axis=-1, keepdims=True) + EPS)
            nrm = yg * rr
            dng_ref[:, gs] += jnp.sum(dyv[:, gs] * nrm, axis=0, keepdims=True)
            dn = dyv[:, gs] * ng_ref[:, gs]
            dx_scr[:, gs] = rr * (dn - nrm * jnp.mean(dn * nrm, axis=-1, keepdims=True))
        dypre = dx_scr[...]
        d_y = dypre * silu
        dpm_ref[:, CONV_DIM:CONV_DIM + SSM_WIDTH] = (dypre * y * _dsilu(z, sz)).astype(dpm_ref.dtype)

        for g in range(SSM_GROUPS):
            bg, cg, cb = f.group(g)
            gs = slice(g * SSM_GW, (g + 1) * SSM_GW)
            htg = st_ref[0, g]
            dhn = dht_ref[g]
            dcb = jnp.zeros((BLK, BLK), F32)
            for j in range(8):
                h = g * 8 + j
                sl = slice(h * 64, (h + 1) * 64)
                dec = f.decay(h)
                dyh = d_y[:, sl]
                dmd = _dot_nt(dyh, f.xdt[:, sl]) * dec
                dcb = dcb + dmd
                gm = dmd * cb
                da_scr[:, h:h + 1] = jnp.sum(gm, axis=1, keepdims=True)
                dat_scr[h:h + 1, :] = jnp.sum(gm, axis=0, keepdims=True)
                dx_scr[:, sl] = _dot_tn(cb * dec, dyh)
            dz = f.ecs_x[:, gs] * d_y[:, gs]
            dbc_scr[:, 512 + g * 128:512 + (g + 1) * 128] = _dot(dcb, bg) + _dot_nt(dz, htg)
            dbc_scr[:, g * 128:(g + 1) * 128] = _dot_tn(dcb, cg) + _dot_nt(f.xdt[:, gs] * f.dte_x[:, gs], dhn)
            dws = _dot(bg, dhn)
            dx_scr[:, gs] += f.dte_x[:, gs] * dws
            r2_scr[:, gs] = dws * f.xdt[:, gs]
            hs_scr[:, gs] = _bcast8(jnp.sum(dhn * htg, axis=0, keepdims=True))
            dht_ref[g] = f.cd_x[:, gs] * dhn + _dot_tn(cg, dz)
        d_x = dx_scr[...]
        r1 = _dot_hi(d_y * yoff_scr[...], et)
        r2 = _dot_hi(r2_scr[...], et) * f.dte
        dcd = _dot_hi(hs_scr[...], et)[0:1]
        d_tot = jnp.sum(r2, axis=0, keepdims=True) + dcd * f.cd
        row = lax.broadcasted_iota(jnp.int32, (BLK, 128), 0)
        d_a = da_scr[...] - dat_scr[...].T + r1 - r2 + jnp.where(row == BLK - 1, d_tot, 0.0)
        dadt = _dot_hi(trit_ref[...], d_a)
        ddt = dadt * f.a + _dot_hi(d_x * f.xs, et)
        lane = lax.broadcasted_iota(jnp.int32, (BLK, 128), 1)
        dr = jnp.where(lane < SSM_HEADS, ddt * _sig(f.dtp), 0.0)
        dvec_ref[0:1, :] += jnp.sum(dr, axis=0, keepdims=True)
        dvec_ref[1:2, :] += jnp.sum(dadt * f.dt, axis=0, keepdims=True) * f.a
        dd_scr[...] += _bcast8(jnp.sum(d_y * f.xs, axis=0, keepdims=True))
        dpm_ref[:, DT_OFF:DT_OFF + 128] = dr.astype(dpm_ref.dtype)
        dpm_ref[:, DT_OFF + 128:SSM_COLS] = jnp.zeros((BLK, 128), dpm_ref.dtype)

        dxs = d_x * f.dt_x + f.d_x * d_y
        dact = jnp.concatenate([dxs, dbc_scr[...]], axis=1)
        dpre = dact * _dsilu(f.pre, f.sg)
        dcb_ref[...] += jnp.sum(dpre, axis=0, keepdims=True)
        dxraw = jnp.zeros((BLK, CONV_DIM), F32)
        nxt = dcar_ref[...]
        for k in range(4):
            dcw_ref[3 - k:4 - k, :] += jnp.sum(dpre * f.shifted[k], axis=0, keepdims=True)
            dxraw = dxraw + cw[3 - k:4 - k] * _shift_up(dpre, nxt, k)
        dcar_ref[...] = dpre[0:16]
        dpm_ref[:, 0:CONV_DIM] = dxraw.astype(dpm_ref.dtype)

        @pl.when(i == nc - 1)
        def _():
            dvec_ref[2:3, :] = _dot_hi(dd_scr[...], et)[0:1]

    return pl.pallas_call(
        body, name="ssd_bwd", grid=(nc,),
        in_specs=[pl.BlockSpec((BLK, SSM_COLS), lambda i: (nc - 1 - i, 0)),
                  pl.BlockSpec((16, CONV_DIM), lambda i: (jnp.maximum(8 * (nc - 1 - i) - 1, 0), 0)),
                  pl.BlockSpec((BLK, SSM_WIDTH), lambda i: (nc - 1 - i, 0)),
                  pl.BlockSpec((1, SSM_GROUPS, 128, SSM_GW), lambda i: (nc - 1 - i, 0, 0, 0)),
                  _full((4, CONV_DIM)), _full((1, CONV_DIM)), _full((1, 128)), _full((1, 128)), _full((1, 128)),
                  _full((1, SSM_WIDTH)), _full((BLK, BLK)), _full((BLK, BLK)), _full((128, SSM_WIDTH)),
                  _full((SSM_WIDTH, 128))],
        out_specs=[pl.BlockSpec((BLK, SSM_COLS), lambda i: (nc - 1 - i, 0)),
                   _full((8, CONV_DIM)), _full((1, CONV_DIM)), _full((8, 128)), _full((1, SSM_WIDTH))],
        out_shape=[jax.ShapeDtypeStruct((s, SSM_COLS), ACT_DTYPE), jax.ShapeDtypeStruct((8, CONV_DIM), F32),
                   jax.ShapeDtypeStruct((1, CONV_DIM), F32), jax.ShapeDtypeStruct((8, 128), F32),
                   jax.ShapeDtypeStruct((1, SSM_WIDTH), F32)],
        scratch_shapes=[pltpu.VMEM((SSM_GROUPS, 128, SSM_GW), F32), pltpu.VMEM((16, CONV_DIM), F32),
                        pltpu.VMEM((BLK, SSM_WIDTH), F32), pltpu.VMEM((BLK, SSM_WIDTH), F32),
                        pltpu.VMEM((BLK, SSM_WIDTH), F32), pltpu.VMEM((BLK, SSM_WIDTH), F32),
                        pltpu.VMEM((8, SSM_WIDTH), F32), pltpu.VMEM((BLK, 128), F32), pltpu.VMEM((128, BLK), F32),
                        pltpu.VMEM((8, SSM_WIDTH), F32), pltpu.VMEM((BLK, 1024), F32)],
        compiler_params=_cparams(("arbitrary",)),
    )(pm, pm, dy, states, cw, cbias, dtb, alog, dsk, ng, tri, tri_t, e, e_t)


def merge_fwd(x, ya, ys, ym, pg, wa, ws, wm, wo, g_post):
    s, d = x.shape
    tm = min(256, s)

    def body(x_ref, ya_ref, ys_ref, ym_ref, pg_ref, wa_ref, ws_ref, wm_ref, wo_ref, g_ref,
             xo_ref, ba_ref, bs_ref, bm_ref, mg_ref, out_ref):
        ba = _dot(ya_ref[...], wa_ref[...])
        bs = _dot(ys_ref[...], ws_ref[...])
        bm = _dot(ym_ref[...], wm_ref[...])
        merged = (_sig(pg_ref[:, 0:d].astype(F32)) * ba + _sig(pg_ref[:, d:2 * d].astype(F32)) * bs
                  + _sig(pg_ref[:, 2 * d:3 * d].astype(F32)) * bm)
        out = _dot(merged, wo_ref[...])
        r = lax.rsqrt(jnp.mean(out * out, axis=-1, keepdims=True) + EPS)
        xo_ref[...] = x_ref[...] + out * r * g_ref[...]
        ba_ref[...] = ba.astype(ba_ref.dtype)
        bs_ref[...] = bs.astype(bs_ref.dtype)
        bm_ref[...] = bm.astype(bm_ref.dtype)
        mg_ref[...] = merged.astype(mg_ref.dtype)
        out_ref[...] = out.astype(out_ref.dtype)

    rows = lambda w: pl.BlockSpec((tm, w), lambda i: (i, 0))
    act = jax.ShapeDtypeStruct((s, d), ACT_DTYPE)
    return pl.pallas_call(
        body, name="merge_fwd", grid=(s // tm,),
        in_specs=[rows(d), rows(d), rows(d), rows(2 * d), rows(3 * d), _full((d, d)), _full((d, d)),
                  _full((2 * d, d)), _full((d, d)), _full((1, d))],
        out_specs=[rows(d)] * 6,
        out_shape=[jax.ShapeDtypeStruct((s, d), F32), act, act, act, act, act],
        compiler_params=_cparams(("parallel",)),
    )(x, ya, ys, ym, pg, wa, ws, wm, wo, g_post)


def merge_bwd(dx, out_s, pg, ba, bs, bm, wa, ws, wm, wo, g_post):
    s, d = dx.shape
    tm = min(256, s)

    def body(dx_ref, out_ref, pg_ref, ba_ref, bs_ref, bm_ref, wa_ref, ws_ref, wm_ref, wo_ref, g_ref,
             dout_ref, dba_ref, dbs_ref, dbm_ref, dpg_ref, dya_ref, dys_ref, dym_ref, dg_ref):
        @pl.when(pl.program_id(0) == 0)
        def _():
            dg_ref[...] = jnp.zeros_like(dg_ref)

        o = out_ref[...].astype(F32)
        dxv = dx_ref[...]
        r = lax.rsqrt(jnp.mean(o * o, axis=-1, keepdims=True) + EPS)
        nrm = o * r
        dg_ref[...] += jnp.sum(dxv * nrm, axis=0, keepdims=True)
        dn = dxv * g_ref[...]
        dout = r * (dn - nrm * jnp.mean(dn * nrm, axis=-1, keepdims=True))
        dout_ref[...] = dout.astype(dout_ref.dtype)
        dmerged = _dot_nt(dout, wo_ref[...])
        for q, (b_ref, db_ref, w_ref, dy_ref) in enumerate(((ba_ref, dba_ref, wa_ref, dya_ref),
                                                            (bs_ref, dbs_ref, ws_ref, dys_ref),
                                                            (bm_ref, dbm_ref, wm_ref, dym_ref))):
            gt = _sig(pg_ref[:, q * d:(q + 1) * d].astype(F32))
            db = dmerged * gt
            db_ref[...] = db.astype(db_ref.dtype)
            dpg_ref[:, q * d:(q + 1) * d] = (dmerged * b_ref[...].astype(F32) * gt * (1.0 - gt)).astype(dpg_ref.dtype)
            dy_ref[...] = _dot_nt(db, w_ref[...]).astype(dy_ref.dtype)

    rows = lambda w: pl.BlockSpec((tm, w), lambda i: (i, 0))
    act = lambda w: jax.ShapeDtypeStruct((s, w), ACT_DTYPE)
    return pl.pallas_call(
        body, name="merge_bwd", grid=(s // tm,),
        in_specs=[rows(d), rows(d), rows(3 * d), rows(d), rows(d), rows(d), _full((d, d)), _full((d, d)),
                  _full((2 * d, d)), _full((d, d)), _full((1, d))],
        out_specs=[rows(d), rows(d), rows(d), rows(d), rows(3 * d), rows(d), rows(d), rows(2 * d), _full((1, d))],
        out_shape=[act(d), act(d), act(d), act(d), act(3 * d), act(d), act(d), act(2 * d),
                   jax.ShapeDtypeStruct((1, d), F32)],
        compiler_params=_cparams(("arbitrary",)),
    )(dx, out_s, pg, ba, bs, bm, wa, ws, wm, wo, g_post)


def loss_grad(y, target):
    s, d = y.shape
    tm = min(512, s)

    def body(y_ref, t_ref, dy_ref, l_ref):
        @pl.when(pl.program_id(0) == 0)
        def _():
            l_ref[...] = jnp.zeros_like(l_ref)

        err = y_ref[...] - t_ref[...]
        dy_ref[...] = err * (1.0 / d)
        part = jnp.sum(jnp.sum(err * err, axis=-1, keepdims=True) * (1.0 / d), axis=0, keepdims=True)
        l_ref[...] += 0.5 * jnp.broadcast_to(part, l_ref.shape)

    return pl.pallas_call(
        body, name="loss_grad", grid=(s // tm,),
        in_specs=[pl.BlockSpec((tm, d), lambda i: (i, 0)), pl.BlockSpec((tm, d), lambda i: (i, 0))],
        out_specs=[pl.BlockSpec((tm, d), lambda i: (i, 0)), _full((8, 128))],
        out_shape=[jax.ShapeDtypeStruct((s, d), F32), jax.ShapeDtypeStruct((8, 128), F32)],
        compiler_params=_cparams(("arbitrary",)),
    )(y, target)


def _mesh_pos():
    x, y, c = lax.axis_index("x"), lax.axis_index("y"), lax.axis_index("c")
    return x, y, c, 4 * x + 2 * y + c


def _peer(x, y, c, k):
    px = 1 - x if k & 4 else x
    py = 1 - y if k & 2 else y
    pc = 1 - c if k & 1 else c
    return (px, py, pc), 4 * px + 2 * py + pc


def all_gather(big, small):
    def body(big_ref, small_ref, obig_ref, osmall_ref, send_sems, recv_sems, local_sems):
        x, y, c, me = _mesh_pos()
        locals_ = [pltpu.make_async_copy(big_ref, obig_ref.at[me], local_sems.at[0]),
                   pltpu.make_async_copy(small_ref, osmall_ref.at[me], local_sems.at[1])]
        for cp in locals_:
            cp.start()
        sends = []
        for k in range(1, N_DEV):
            peer, _ = _peer(x, y, c, k)
            for q, (src, dst) in enumerate(((big_ref, obig_ref), (small_ref, osmall_ref))):
                cp = pltpu.make_async_remote_copy(src_ref=src, dst_ref=dst.at[me], send_sem=send_sems.at[q, k - 1],
                                                  recv_sem=recv_sems.at[q, k - 1], device_id=peer, device_id_type=MESH)
                cp.start()
                sends.append(cp)
        for k in range(1, N_DEV):
            peer, pidx = _peer(x, y, c, k)
            for q, (src, dst) in enumerate(((big_ref, obig_ref), (small_ref, osmall_ref))):
                pltpu.make_async_remote_copy(src_ref=src, dst_ref=dst.at[pidx], send_sem=send_sems.at[q, k - 1],
                                             recv_sem=recv_sems.at[q, k - 1], device_id=peer,
                                             device_id_type=MESH).wait_recv()
        for cp in sends:
            cp.wait_send()
        for cp in locals_:
            cp.wait()

    any_spec = pl.BlockSpec(memory_space=pl.ANY)
    return pl.pallas_call(
        body, name="all_gather",
        in_specs=[any_spec, any_spec], out_specs=[any_spec, any_spec],
        out_shape=[jax.ShapeDtypeStruct((N_DEV,) + big.shape, big.dtype),
                   jax.ShapeDtypeStruct((N_DEV,) + small.shape, small.dtype)],
        scratch_shapes=[pltpu.SemaphoreType.DMA((2, N_DEV - 1)), pltpu.SemaphoreType.DMA((2, N_DEV - 1)),
                        pltpu.SemaphoreType.DMA((2,))],
    )(big, small)


def reduce_scatter(big, small):
    def body(big_ref, small_ref, obig_ref, osmall_ref, send_sems, recv_sems, local_sems):
        x, y, c, me = _mesh_pos()
        locals_ = [pltpu.make_async_copy(big_ref.at[me], obig_ref.at[me], local_sems.at[0]),
                   pltpu.make_async_copy(small_ref, osmall_ref.at[me], local_sems.at[1])]
        for cp in locals_:
            cp.start()
        sends = []
        for k in range(1, N_DEV):
            peer, pidx = _peer(x, y, c, k)
            for q, (src, dst) in enumerate(((big_ref.at[pidx], obig_ref), (small_ref, osmall_ref))):
                cp = pltpu.make_async_remote_copy(src_ref=src, dst_ref=dst.at[me], send_sem=send_sems.at[q, k - 1],
                                                  recv_sem=recv_sems.at[q, k - 1], device_id=peer, device_id_type=MESH)
                cp.start()
                sends.append(cp)
        for k in range(1, N_DEV):
            peer, pidx = _peer(x, y, c, k)
            for q, (src, dst) in enumerate(((big_ref.at[pidx], obig_ref), (small_ref, osmall_ref))):
                pltpu.make_async_remote_copy(src_ref=src, dst_ref=dst.at[pidx], send_sem=send_sems.at[q, k - 1],
                                             recv_sem=recv_sems.at[q, k - 1], device_id=peer,
                                             device_id_type=MESH).wait_recv()
        for cp in sends:
            cp.wait_send()
        for cp in locals_:
            cp.wait()

    any_spec = pl.BlockSpec(memory_space=pl.ANY)
    return pl.pallas_call(
        body, name="reduce_scatter",
        in_specs=[any_spec, any_spec], out_specs=[any_spec, any_spec],
        out_shape=[jax.ShapeDtypeStruct(big.shape, big.dtype),
                   jax.ShapeDtypeStruct((N_DEV,) + small.shape, small.dtype)],
        scratch_shapes=[pltpu.SemaphoreType.DMA((2, N_DEV - 1)), pltpu.SemaphoreType.DMA((2, N_DEV - 1)),
                        pltpu.SemaphoreType.DMA((2,))],
    )(big, small)


def adamw(parts, w, m, v, tile, name):
    npart, r, d = parts.shape

    def body(p_ref, w_ref, m_ref, v_ref, g_ref, dw_ref, nm_ref, nv_ref):
        g = p_ref[0].astype(F32)
        for q in range(1, npart):
            g = g + p_ref[q].astype(F32)
        g_ref[...] = g
        nm = ADAM_B1 * m_ref[...] + (1.0 - ADAM_B1) * g
        nv = ADAM_B2 * v_ref[...] + (1.0 - ADAM_B2) * (g * g)
        nm_ref[...] = nm
        nv_ref[...] = nv
        m_hat = nm / (1.0 - ADAM_B1 ** ADAM_STEP)
        v_hat = nv / (1.0 - ADAM_B2 ** ADAM_STEP)
        dw_ref[...] = -ADAM_LR * (m_hat / (jnp.sqrt(v_hat) + ADAM_EPS) + ADAM_WD * w_ref[...])

    rows = pl.BlockSpec((tile, d), lambda i: (i, 0))
    out = jax.ShapeDtypeStruct((r, d), F32)
    return pl.pallas_call(
        body, name=name, grid=(r // tile,),
        in_specs=[pl.BlockSpec((npart, tile, d), lambda i: (0, i, 0)), rows, rows, rows],
        out_specs=[rows] * 4, out_shape=[out] * 4,
        compiler_params=_cparams(("parallel",)),
    )(parts, w, m, v)


def _pad_rows(a, rows):
    return jnp.pad(a, ((0, rows - a.shape[0]), (0, 0)))


def _pack_big(w_in, w_att, w_sg, w_ssm, w_out):
    parts = []
    for l in range(2):
        parts += [_pad_rows(w_in[l].reshape(-1, D_MODEL), WIN_ROWS), w_att[l], w_sg[l], w_ssm[l], w_out[l]]
    return _pad_rows(jnp.concatenate(parts, axis=0), BIG_ROWS)


def _unpack_big(p):
    outs = [[], [], [], [], []]
    for l in range(2):
        o = l * LAYER_ROWS
        outs[0].append(p[o:o + 1700].reshape(D_MODEL, 1700))
        o += WIN_ROWS
        for q, rws in enumerate((128, 128, 256, 128)):
            outs[1 + q].append(p[o:o + rws])
            o += rws
    return [jnp.stack(t) for t in outs]


SMALL_SIZES = (("norm_pre", 2048), ("norm_post", 2048), ("rel_bias", 512), ("att_sinks", 32), ("sg_ln_g", 2048),
               ("sg_ln_b", 2048), ("sg_w", 262144), ("sg_b", 2048), ("ssm_conv_b", 6144), ("ssm_dt_bias", 64),
               ("ssm_a_log", 64), ("ssm_d", 64), ("ssm_norm_g", 4096), ("conv_w_full", 24576))


def _pack_small(d):
    parts = []
    for name, size in SMALL_SIZES:
        rows = 8 * (-(-size // (8 * D_MODEL)))
        flat = d[name].reshape(-1) if name in d else jnp.zeros((size,), F32)
        parts.append(jnp.pad(flat, (0, rows * D_MODEL - size)).reshape(rows, D_MODEL))
    return _pad_rows(jnp.concatenate(parts, axis=0), SMALL_ROWS)


def _unpack_small(p, shapes):
    out, o = {}, 0
    for name, size in SMALL_SIZES:
        rows = 8 * (-(-size // (8 * D_MODEL)))
        if name in shapes:
            out[name] = p[o:o + rows].reshape(-1)[:size].reshape(shapes[name])
        o += rows
    return out


def _bucket_onehot_t():
    qi = jnp.arange(BLK, dtype=jnp.int32)[:, None]
    kj = jnp.arange(2 * BLK, dtype=jnp.int32)[None, :]
    dist = qi + BLK - kj
    in_window = (dist >= 0) & (dist < BLK)
    dd = jnp.maximum(dist, 0)
    max_exact = REL_BUCKETS // 2
    dist_f = jnp.maximum(dd, 1).astype(F32)
    large = max_exact + (jnp.log(dist_f / max_exact) / math.log(128 / max_exact)
                         * (REL_BUCKETS - max_exact)).astype(jnp.int32)
    large = jnp.minimum(large, REL_BUCKETS - 1)
    bucket = jnp.where(dd < max_exact, dd, large).reshape(1, -1)
    onehot_t = (bucket == jnp.arange(REL_BUCKETS, dtype=jnp.int32)[:, None]).astype(F32)
    maskadd = jnp.where(in_window, 0.0, NEG).astype(F32).reshape(1, -1)
    return onehot_t, maskadd


WEIGHTS = ['w_in', 'norm_pre', 'norm_post', 'rel_bias', 'att_sinks', 'sg_ln_g', 'sg_ln_b', 'sg_w', 'sg_b',
           'ssm_conv_w', 'ssm_conv_b', 'ssm_dt_bias', 'ssm_a_log', 'ssm_d', 'ssm_norm_g',
           'w_br_att', 'w_br_sg', 'w_br_ssm', 'w_out']
BIG = ('w_in', 'w_br_att', 'w_br_sg', 'w_br_ssm', 'w_out')


def _split_w_in(w):
    zeros = jnp.zeros((D_MODEL, 224), w.dtype)
    w_att = jnp.concatenate([w[:, 0:1024], w[:, 1280:2304], w[:, 1024:1280]], axis=1)
    w_sg = w[:, 2304:5376]
    w_ssm = jnp.concatenate([w[:, 7424:10496], w[:, 5376:7424], w[:, 10496:10528], zeros], axis=1)
    w_gate = w[:, 10528:13600]
    return w_att, w_sg, w_ssm, w_gate


def _join_dw_in(d_att, d_sg, d_ssm, d_gate):
    return jnp.concatenate([d_att[:, 0:1024], d_att[:, 2048:2304], d_att[:, 1024:2048], d_sg,
                            d_ssm[:, 3072:5120], d_ssm[:, 0:3072], d_ssm[:, 5120:5152], d_gate], axis=1)


def kernel(x, w_in, norm_pre, norm_post, rel_bias, att_sinks, sg_ln_g, sg_ln_b, sg_w, sg_b, ssm_conv_w, ssm_conv_b, ssm_dt_bias, ssm_a_log, ssm_d, ssm_norm_g, w_br_att, w_br_sg, w_br_ssm, w_out, loss_target, m_w_in, m_norm_pre, m_norm_post, m_rel_bias, m_att_sinks, m_sg_ln_g, m_sg_ln_b, m_sg_w, m_sg_b, m_ssm_conv_w, m_ssm_conv_b, m_ssm_dt_bias, m_ssm_a_log, m_ssm_d, m_ssm_norm_g, m_w_br_att, m_w_br_sg, m_w_br_ssm, m_w_out, v_w_in, v_norm_pre, v_norm_post, v_rel_bias, v_att_sinks, v_sg_ln_g, v_sg_ln_b, v_sg_w, v_sg_b, v_ssm_conv_w, v_ssm_conv_b, v_ssm_dt_bias, v_ssm_a_log, v_ssm_d, v_ssm_norm_g, v_w_br_att, v_w_br_sg, v_w_br_ssm, v_w_out):
    w = dict(w_in=w_in, norm_pre=norm_pre, norm_post=norm_post, rel_bias=rel_bias, att_sinks=att_sinks,
             sg_ln_g=sg_ln_g, sg_ln_b=sg_ln_b, sg_w=sg_w, sg_b=sg_b, ssm_conv_w=ssm_conv_w, ssm_conv_b=ssm_conv_b,
             ssm_dt_bias=ssm_dt_bias, ssm_a_log=ssm_a_log, ssm_d=ssm_d, ssm_norm_g=ssm_norm_g,
             w_br_att=w_br_att, w_br_sg=w_br_sg, w_br_ssm=w_br_ssm, w_out=w_out)
    mom = dict(w_in=m_w_in, norm_pre=m_norm_pre, norm_post=m_norm_post, rel_bias=m_rel_bias, att_sinks=m_att_sinks,
               sg_ln_g=m_sg_ln_g, sg_ln_b=m_sg_ln_b, sg_w=m_sg_w, sg_b=m_sg_b, ssm_conv_w=m_ssm_conv_w,
               ssm_conv_b=m_ssm_conv_b, ssm_dt_bias=m_ssm_dt_bias, ssm_a_log=m_ssm_a_log, ssm_d=m_ssm_d,
               ssm_norm_g=m_ssm_norm_g, w_br_att=m_w_br_att, w_br_sg=m_w_br_sg, w_br_ssm=m_w_br_ssm, w_out=m_w_out)
    var = dict(w_in=v_w_in, norm_pre=v_norm_pre, norm_post=v_norm_post, rel_bias=v_rel_bias, att_sinks=v_att_sinks,
               sg_ln_g=v_sg_ln_g, sg_ln_b=v_sg_ln_b, sg_w=v_sg_w, sg_b=v_sg_b, ssm_conv_w=v_ssm_conv_w,
               ssm_conv_b=v_ssm_conv_b, ssm_dt_bias=v_ssm_dt_bias, ssm_a_log=v_ssm_a_log, ssm_d=v_ssm_d,
               ssm_norm_g=v_ssm_norm_g, w_br_att=v_w_br_att, w_br_sg=v_w_br_sg, w_br_ssm=v_w_br_ssm, w_out=v_w_out)
    xs0 = x[0]
    target = loss_target[0]
    my_dev = 4 * lax.axis_index("x") + 2 * lax.axis_index("y") + lax.axis_index("c")

    big_shard = _pack_big(*[w[n] for n in BIG])
    conv_shard = _pad_rows(ssm_conv_w.reshape(-1, D_MODEL), 8)
    gathered, gathered_conv = all_gather(big_shard.astype(WIRE_DTYPE), conv_shard)
    conv_full = gathered_conv[:, 0:3].reshape(N_DEV, 2, 4, 384).transpose(1, 2, 0, 3).reshape(2, 4, CONV_DIM)

    layers = []
    for l in range(2):
        o = l * LAYER_ROWS
        wi = gathered[:, o:o + 1700].reshape(N_DEV, D_MODEL, 1700).transpose(1, 0, 2).reshape(D_MODEL, 13600)
        o += WIN_ROWS
        lw = dict(zip(("in_att", "in_sg", "in_ssm", "in_gate"), _split_w_in(wi)))
        for name, rws in (("att", 128), ("sg", 128), ("ssm", 256), ("out", 128)):
            lw[name] = gathered[:, o:o + rws].reshape(N_DEV * rws, D_MODEL).astype(MXU_DTYPE)
            o += rws
        for name in ("in_att", "in_sg", "in_ssm", "in_gate"):
            lw[name] = lw[name].astype(MXU_DTYPE)
        tril = jnp.tril(jnp.ones((BLK, BLK), bool))
        sgw = jnp.where(tril[None], sg_w[l], 0.0)
        lw.update(
            g_pre=norm_pre[l][None], g_post=norm_post[l][None], sinks=_pad_lanes(att_sinks[l][None]),
            ln_g=sg_ln_g[l][None], ln_b=sg_ln_b[l][None], sgw=sgw.astype(MXU_DTYPE),
            sgw_t=sgw.transpose(0, 2, 1).astype(MXU_DTYPE), sgb_t=_pad_lanes(sg_b[l].T),
            cw=conv_full[l], cb=ssm_conv_b[l][None], dtb=_pad_lanes(ssm_dt_bias[l][None]),
            alog=_pad_lanes(ssm_a_log[l][None]), dsk=_pad_lanes(ssm_d[l][None]), ng=ssm_norm_g[l][None])
        layers.append(lw)

    onehot_t, maskadd = _bucket_onehot_t()
    bias = bias_table(rel_bias.T, onehot_t, maskadd).reshape(ATT_HEADS, BLK, 2 * BLK)

    saved = []
    xl = xs0
    for lw in layers:
        h = rmsnorm_fwd(xl, lw["g_pre"])
        pa = mm_nn(h, lw["in_att"], 768, "proj_att")
        ps = mm_nn(h, lw["in_sg"], 1024, "proj_sg")
        pm = mm_nn(h, lw["in_ssm"], 768, "proj_ssm")
        pg = mm_nn(h, lw["in_gate"], 1024, "proj_gate")
        ya = attn_fwd(pa, bias, lw["sinks"])
        ys = sgu_fwd(ps, lw["ln_g"], lw["ln_b"], lw["sgw"], lw["sgb_t"])
        ym, states = ssd_fwd(pm, lw["cw"], lw["cb"], lw["dtb"], lw["alog"], lw["dsk"], lw["ng"])
        x_next, ba, bs, bm, merged, out_s = merge_fwd(xl, ya, ys, ym, pg, lw["att"], lw["sg"], lw["ssm"], lw["out"],
                                                      lw["g_post"])
        saved.append(dict(x=xl, h=h, pa=pa, ps=ps, pm=pm, pg=pg, ya=ya, ys=ys, ym=ym, states=states, ba=ba, bs=bs,
                          bm=bm, merged=merged, out_s=out_s))
        xl = x_next

    dx, loss_part = loss_grad(xl, target)
    loss = lax.psum(loss_part[0, 0], ("x", "y", "c"))

    dbias = jnp.zeros((ATT_HEADS, BLK, 2 * BLK), F32)
    big_grads = [None, None]
    small = {n: [None, None] for n in ("norm_pre", "norm_post", "att_sinks", "sg_ln_g", "sg_ln_b", "sg_w", "sg_b",
                                       "ssm_conv_b", "ssm_dt_bias", "ssm_a_log", "ssm_d", "ssm_norm_g",
                                       "conv_w_full")}
    for l in (1, 0):
        lw, sv = layers[l], saved[l]
        dout, dba, dbs, dbm, dpg, dya, dys, dym, dg_post = merge_bwd(
            dx, sv["out_s"], sv["pg"], sv["ba"], sv["bs"], sv["bm"], lw["att"], lw["sg"], lw["ssm"], lw["out"],
            lw["g_post"])
        dw_out = mm_tn(sv["merged"], dout, 512, "dw_out")
        dw_att = mm_tn(sv["ya"], dba, 512, "dw_br_att")
        dw_sg = mm_tn(sv["ys"], dbs, 512, "dw_br_sg")
        dw_ssm = mm_tn(sv["ym"], dbm, 512, "dw_br_ssm")
        dpa, dbias, dsinks = attn_bwd(sv["pa"], dya, bias, lw["sinks"], dbias)
        dps, dsgw, dsgb_t, dln_g, dln_b = sgu_bwd(sv["ps"], dys, lw["ln_g"], lw["ln_b"], lw["sgw"], lw["sgw_t"],
                                                  lw["sgb_t"])
        dpm, dcw, dcb, dvec, dng = ssd_bwd(sv["pm"], dym, sv["states"], lw["cw"], lw["cb"], lw["dtb"], lw["alog"],
                                           lw["dsk"], lw["ng"])
        dw_in = _join_dw_in(mm_tn(sv["h"], dpa, 768, "dw_in_att"), mm_tn(sv["h"], dps, 768, "dw_in_sg"),
                            mm_tn(sv["h"], dpm, 768, "dw_in_ssm"), mm_tn(sv["h"], dpg, 768, "dw_in_gate"))
        dx, dg_pre = dh_norm_bwd([dpa, dps, dpm, dpg], [lw["in_att"], lw["in_sg"], lw["in_ssm"], lw["in_gate"]],
                                 sv["x"], lw["g_pre"], dx)
        parts = [_pad_rows_3d(dw_in.reshape(D_MODEL, N_DEV, 1700).transpose(1, 0, 2).reshape(N_DEV, 1700, D_MODEL),
                              WIN_ROWS),
                 dw_att.reshape(N_DEV, 128, D_MODEL), dw_sg.reshape(N_DEV, 128, D_MODEL),
                 dw_ssm.reshape(N_DEV, 256, D_MODEL), dw_out.reshape(N_DEV, 128, D_MODEL)]
        big_grads[l] = jnp.concatenate(parts, axis=1)
        small["norm_pre"][l] = dg_pre[0]
        small["norm_post"][l] = dg_post[0]
        small["att_sinks"][l] = dsinks[0, :ATT_HEADS]
        small["sg_ln_g"][l] = dln_g[0]
        small["sg_ln_b"][l] = dln_b[0]
        small["sg_w"][l] = dsgw
        small["sg_b"][l] = dsgb_t[:, :SG_GROUPS].T
        small["ssm_conv_b"][l] = dcb[0]
        small["ssm_dt_bias"][l] = dvec[0, :SSM_HEADS]
        small["ssm_a_log"][l] = dvec[1, :SSM_HEADS]
        small["ssm_d"][l] = dvec[2, :SSM_HEADS]
        small["ssm_norm_g"][l] = dng[0]
        small["conv_w_full"][l] = dcw[0:4]
    grad_x = dx
    d_rel_bias = bias_table_bwd(dbias.reshape(ATT_HEADS, -1), onehot_t).T

    big_full = jnp.concatenate(big_grads, axis=1)
    big_full = jnp.pad(big_full, ((0, 0), (0, BIG_ROWS - big_full.shape[1]), (0, 0))).astype(WIRE_DTYPE)
    small_d = {n: jnp.stack(v) for n, v in small.items()}
    small_d["rel_bias"] = d_rel_bias
    recv_big, recv_small = reduce_scatter(big_full, _pack_small(small_d))

    g_b, dw_b, nm_b, nv_b = adamw(recv_big, _pack_big(*[w[n] for n in BIG]), _pack_big(*[mom[n] for n in BIG]),
                                  _pack_big(*[var[n] for n in BIG]), ADAM_TILE, "adamw_big")
    small_names = [n for n, _ in SMALL_SIZES if n != "conv_w_full"]
    g_s, dw_s, nm_s, nv_s = adamw(recv_small, _pack_small({n: w[n] for n in small_names}),
                                  _pack_small({n: mom[n] for n in small_names}),
                                  _pack_small({n: var[n] for n in small_names}), SMALL_TILE, "adamw_small")
    shapes = {n: w[n].shape for n in small_names}
    shapes["conv_w_full"] = (2, 4, CONV_DIM)
    g_conv_full = _unpack_small(g_s, shapes)["conv_w_full"]
    g_conv = lax.dynamic_slice_in_dim(g_conv_full, my_dev * 384, 384, axis=2)
    pack_conv = lambda a: _pad_rows(a.reshape(-1, D_MODEL), 8)
    g_c, dw_c, nm_c, nv_c = adamw(pack_conv(g_conv)[None], pack_conv(ssm_conv_w), pack_conv(m_ssm_conv_w),
                                  pack_conv(v_ssm_conv_w), 8, "adamw_conv")

    results = {}
    for tag, pb, psm, pc in (("grad", g_b, g_s, g_c), ("delta", dw_b, dw_s, dw_c), ("new_m", nm_b, nm_s, nm_c),
                             ("new_v", nv_b, nv_s, nv_c)):
        r = dict(zip(BIG, _unpack_big(pb)))
        r.update(_unpack_small(psm, {n: w[n].shape for n in small_names}))
        r["ssm_conv_w"] = pc[0:3].reshape(2, 4, 384)
        results[tag] = r
    outs = [loss, grad_x[None]]
    for tag in ("grad", "delta", "new_m", "new_v"):
        outs += [results[tag][n] for n in WEIGHTS]
    return tuple(outs)


def _pad_rows_3d(a, rows):
    return jnp.pad(a, ((0, 0), (0, rows - a.shape[1]), (0, 0)))
```

```python
import math

import jax
import jax.numpy as jnp
from jax import lax
from jax.experimental import pallas as pl
from jax.experimental.pallas import tpu as pltpu

F32 = jnp.float32
MXU_DTYPE = jnp.bfloat16
ACT_DTYPE = jnp.bfloat16
WIRE_DTYPE = jnp.bfloat16
HI = lax.Precision.HIGHEST
MESH = pl.DeviceIdType.MESH

D_MODEL = 1024
N_DEV = 8
ATT_HEADS = 16
HEAD_DIM = 64
BLK = 128
SG_GROUPS = 8
SSM_WIDTH = 2048
SSM_HEADS = 32
SSM_GROUPS = 4
SSM_GW = SSM_WIDTH // SSM_GROUPS
CONV_DIM = 3072
REL_BUCKETS = 32
EPS = 1e-6
NEG = -1e30

ATT_COLS = 2304
SG_COLS = 3072
SSM_COLS = 5376
GATE_COLS = 3072
DT_OFF = 5120

VMEM_LIMIT_V7X = 56 * 2 ** 20

ADAM_LR, ADAM_B1, ADAM_B2, ADAM_EPS, ADAM_WD, ADAM_STEP = 0.001, 0.9, 0.999, 1e-08, 0.01, 10

WIN_SHARD = 1700
WIN_LANES = 1792
REST_PARTS = (128, 128, 256, 128)
REST_ROWS = 2 * sum(REST_PARTS)
REST_TILE = 256
WIN_TILE = 128
SMALL_ROWS = 384
SMALL_TILE = 128


def _cparams(sem=None):
    return pltpu.CompilerParams(dimension_semantics=sem, vmem_limit_bytes=VMEM_LIMIT_V7X)


def _dot(a, b):
    return jnp.dot(a.astype(MXU_DTYPE), b.astype(MXU_DTYPE), preferred_element_type=F32)


def _dot_nt(a, b):
    return lax.dot_general(a.astype(MXU_DTYPE), b.astype(MXU_DTYPE), (((1,), (1,)), ((), ())),
                           preferred_element_type=F32)


def _dot_tn(a, b):
    return lax.dot_general(a.astype(MXU_DTYPE), b.astype(MXU_DTYPE), (((0,), (0,)), ((), ())),
                           preferred_element_type=F32)


def _dot_hi(a, b):
    return jnp.dot(a, b, precision=HI, preferred_element_type=F32)


def _dot_onehot(a, onehot):
    hi = a.astype(jnp.bfloat16)
    lo = (a - hi.astype(F32)).astype(jnp.bfloat16)
    return (jnp.dot(hi, onehot, preferred_element_type=F32) + jnp.dot(lo, onehot, preferred_element_type=F32))


def _dot_hi_nt(a, b):
    return lax.dot_general(a, b, (((1,), (1,)), ((), ())), precision=HI, preferred_element_type=F32)


def _sig(x):
    return 1.0 / (1.0 + jnp.exp(-x))


def _dsilu(x, s):
    return s * (1.0 + x * (1.0 - s))


def _full(shape):
    nd = len(shape)
    return pl.BlockSpec(shape, lambda *_: (0,) * nd)


def rmsnorm_fwd(x, g):
    s, d = x.shape
    tm = min(512, s)

    def body(x_ref, g_ref, o_ref):
        xv = x_ref[...]
        r = lax.rsqrt(jnp.mean(xv * xv, axis=-1, keepdims=True) + EPS)
        o_ref[...] = (xv * r * g_ref[...]).astype(o_ref.dtype)

    return pl.pallas_call(
        body, name="rmsnorm_fwd", grid=(s // tm,),
        in_specs=[pl.BlockSpec((tm, d), lambda i: (i, 0)), _full((1, d))],
        out_specs=pl.BlockSpec((tm, d), lambda i: (i, 0)),
        out_shape=jax.ShapeDtypeStruct((s, d), ACT_DTYPE),
        compiler_params=_cparams(("parallel",)),
    )(x, g)


def mm_nn(a, b, tn, name):
    s, k = a.shape
    n = b.shape[1]
    tm = min(2048, s)

    def body(a_ref, b_ref, o_ref):
        o_ref[...] = _dot(a_ref[...], b_ref[...]).astype(o_ref.dtype)

    return pl.pallas_call(
        body, name=name, grid=(s // tm, n // tn),
        in_specs=[pl.BlockSpec((tm, k), lambda i, j: (i, 0)), pl.BlockSpec((k, tn), lambda i, j: (0, j))],
        out_specs=pl.BlockSpec((tm, tn), lambda i, j: (i, j)),
        out_shape=jax.ShapeDtypeStruct((s, n), ACT_DTYPE),
        compiler_params=_cparams(("parallel", "arbitrary")),
    )(a, b)


def mm_tn(a, b, tn, name):
    s, k = a.shape
    n = b.shape[1]
    ts = min(512, s)

    def body(a_ref, b_ref, o_ref):
        @pl.when(pl.program_id(1) == 0)
        def _():
            o_ref[...] = jnp.zeros_like(o_ref)

        o_ref[...] += _dot_tn(a_ref[...], b_ref[...])

    return pl.pallas_call(
        body, name=name, grid=(n // tn, s // ts),
        in_specs=[pl.BlockSpec((ts, k), lambda j, t: (t, 0)), pl.BlockSpec((ts, tn), lambda j, t: (t, j))],
        out_specs=pl.BlockSpec((k, tn), lambda j, t: (0, j)),
        out_shape=jax.ShapeDtypeStruct((k, n), F32),
        compiler_params=_cparams(("parallel", "arbitrary")),
    )(a, b)


def dh_norm_bwd(dslabs, wslabs, x, g, dres):
    s, d = x.shape
    tm = min(1024, s)
    tk = 768
    counts = [ds.shape[1] // tk for ds in dslabs]
    starts = [sum(counts[:i]) for i in range(len(counts))]
    nk = sum(counts)
    ns = len(dslabs)

    def mm_body(*refs):
        d_refs, w_refs, dh_ref = refs[:ns], refs[ns:2 * ns], refs[2 * ns]
        k = pl.program_id(1)

        @pl.when(k == 0)
        def _():
            dh_ref[...] = jnp.zeros_like(dh_ref)

        for q in range(ns):
            @pl.when((k >= starts[q]) & (k < starts[q] + counts[q]))
            def _(q=q):
                dh_ref[...] += _dot_nt(d_refs[q][...], w_refs[q][...])

    def clamp(q):
        return lambda i, k: (i, jnp.clip(k - starts[q], 0, counts[q] - 1))

    def clamp_w(q):
        return lambda i, k: (0, jnp.clip(k - starts[q], 0, counts[q] - 1))

    dh = pl.pallas_call(
        mm_body, name="dh_matmul", grid=(s // tm, nk),
        in_specs=([pl.BlockSpec((tm, tk), clamp(q)) for q in range(ns)]
                  + [pl.BlockSpec((d, tk), clamp_w(q)) for q in range(ns)]),
        out_specs=pl.BlockSpec((tm, d), lambda i, k: (i, 0)),
        out_shape=jax.ShapeDtypeStruct((s, d), F32),
        compiler_params=_cparams(("parallel", "arbitrary")),
    )(*dslabs, *wslabs)

    te = min(512, s)

    def norm_body(dh_ref, x_ref, g_ref, dres_ref, dx_ref, dg_ref):
        @pl.when(pl.program_id(0) == 0)
        def _():
            dg_ref[...] = jnp.zeros_like(dg_ref)

        xv = x_ref[...]
        r = lax.rsqrt(jnp.mean(xv * xv, axis=-1, keepdims=True) + EPS)
        xn = xv * r
        dhv = dh_ref[...]
        dg_ref[...] += jnp.sum(dhv * xn, axis=0, keepdims=True)
        dxn = dhv * g_ref[...]
        dx_ref[...] = dres_ref[...] + r * (dxn - xn * jnp.mean(dxn * xn, axis=-1, keepdims=True))

    rows = pl.BlockSpec((te, d), lambda i: (i, 0))
    return pl.pallas_call(
        norm_body, name="norm_bwd", grid=(s // te,),
        in_specs=[rows, rows, _full((1, d)), rows],
        out_specs=[rows, _full((1, d))],
        out_shape=[jax.ShapeDtypeStruct((s, d), F32), jax.ShapeDtypeStruct((1, d), F32)],
        compiler_params=_cparams(("arbitrary",)),
    )(dh, x, g, dres)


def bias_table(rel_bias_t, onehot_t, maskadd):
    n = onehot_t.shape[1]
    tn = 8192

    def body(r_ref, o_ref, m_ref, out_ref):
        out_ref[...] = _dot_hi(r_ref[...], o_ref[...]) + m_ref[...]

    return pl.pallas_call(
        body, name="bias_table", grid=(n // tn,),
        in_specs=[_full((ATT_HEADS, REL_BUCKETS)), pl.BlockSpec((REL_BUCKETS, tn), lambda j: (0, j)),
                  pl.BlockSpec((1, tn), lambda j: (0, j))],
        out_specs=pl.BlockSpec((ATT_HEADS, tn), lambda j: (0, j)),
        out_shape=jax.ShapeDtypeStruct((ATT_HEADS, n), F32),
        compiler_params=_cparams(("parallel",)),
    )(rel_bias_t, onehot_t, maskadd)


def bias_table_bwd(dbias, onehot_t):
    n = onehot_t.shape[1]
    tn = 8192

    def body(d_ref, o_ref, out_ref):
        @pl.when(pl.program_id(0) == 0)
        def _():
            out_ref[...] = jnp.zeros_like(out_ref)

        out_ref[...] += _dot_hi_nt(d_ref[...], o_ref[...])

    return pl.pallas_call(
        body, name="bias_table_bwd", grid=(n // tn,),
        in_specs=[pl.BlockSpec((ATT_HEADS, tn), lambda j: (0, j)), pl.BlockSpec((REL_BUCKETS, tn), lambda j: (0, j))],
        out_specs=_full((ATT_HEADS, REL_BUCKETS)),
        out_shape=jax.ShapeDtypeStruct((ATT_HEADS, REL_BUCKETS), F32),
        compiler_params=_cparams(("arbitrary",)),
    )(dbias, onehot_t)


def _fold(full, tri):
    return jnp.where(tri, full[:, BLK:2 * BLK], full[:, 0:BLK])


def _unfold(folded, tri):
    return jnp.concatenate([jnp.where(tri, 0.0, folded), jnp.where(tri, folded, 0.0)], axis=1)


def _att_head(q, kcat, vcat, bias_h, sink, tri, no_prev):
    l = _fold(_dot_nt(q, kcat), tri) * (HEAD_DIM ** -0.5) + bias_h
    l = jnp.where(no_prev, NEG, l)
    m = jnp.maximum(jnp.max(l, axis=1, keepdims=True), sink)
    p = jnp.exp(l - m)
    es = jnp.exp(sink - m)
    inv = 1.0 / (jnp.sum(p, axis=1, keepdims=True) + es)
    p = p * inv
    pcat = _unfold(p, tri)
    return p, pcat, es * inv, _dot(pcat, vcat)


def _kv_cat(kvp, kvc, g):
    lo = g * HEAD_DIM
    kcat = jnp.concatenate([kvp[:, lo:lo + HEAD_DIM], kvc[:, lo:lo + HEAD_DIM]], axis=0)
    vcat = jnp.concatenate([kvp[:, 128 + lo:128 + lo + HEAD_DIM], kvc[:, 128 + lo:128 + lo + HEAD_DIM]], axis=0)
    return kcat, vcat


def attn_fwd(pa, bias, sinks):
    s = pa.shape[0]
    nb = s // BLK

    def body(pa_ref, kvp_ref, bias_ref, sink_ref, y_ref, o_scr):
        n = pl.program_id(0)
        q = pa_ref[:, 0:1024].astype(F32)
        z = pa_ref[:, 1024:2048].astype(F32)
        kvc = pa_ref[:, 2048:2304].astype(F32)
        kvp = kvp_ref[...].astype(F32)
        tri = (lax.broadcasted_iota(jnp.int32, (BLK, BLK), 0) >= lax.broadcasted_iota(jnp.int32, (BLK, BLK), 1))
        no_prev = (n == 0) & jnp.logical_not(tri)
        for g in range(2):
            kcat, vcat = _kv_cat(kvp, kvc, g)
            for j in range(8):
                h = g * 8 + j
                _, _, _, o = _att_head(q[:, h * 64:(h + 1) * 64], kcat, vcat, bias_ref[h],
                                       sink_ref[0:1, h:h + 1], tri, no_prev)
                o_scr[:, h * 64:(h + 1) * 64] = o
        y_ref[...] = (o_scr[...] * z * _sig(z)).astype(y_ref.dtype)

    return pl.pallas_call(
        body, name="attn_fwd", grid=(nb,),
        in_specs=[pl.BlockSpec((BLK, ATT_COLS), lambda n: (n, 0)),
                  pl.BlockSpec((BLK, 256), lambda n: (jnp.maximum(n - 1, 0), 8)),
                  _full((ATT_HEADS, BLK, BLK)), _full((1, 128))],
        out_specs=pl.BlockSpec((BLK, 1024), lambda n: (n, 0)),
        out_shape=jax.ShapeDtypeStruct((s, 1024), ACT_DTYPE),
        scratch_shapes=[pltpu.VMEM((BLK, 1024), F32)],
        compiler_params=_cparams(("arbitrary",)),
    )(pa, pa, bias, sinks)


def attn_bwd(pa, dy, bias, sinks, dbias_in):
    s = pa.shape[0]
    nb = s // BLK

    def body(pa_ref, kvp_ref, dy_ref, bias_ref, sink_ref, dbin_ref,
             dpa_ref, dbias_ref, dsink_ref, carry, cur, prv, dq_scr, dz_scr):
        i = pl.program_id(0)
        n = nb - 1 - i

        @pl.when(i == 0)
        def _():
            dbias_ref[...] = dbin_ref[...]
            dsink_ref[...] = jnp.zeros_like(dsink_ref)
            carry[...] = jnp.zeros_like(carry)

        q = pa_ref[:, 0:1024].astype(F32)
        z = pa_ref[:, 1024:2048].astype(F32)
        kvc = pa_ref[:, 2048:2304].astype(F32)
        kvp = kvp_ref[...].astype(F32)
        dy = dy_ref[...].astype(F32)
        sz = _sig(z)
        d_o = dy * z * sz
        dzf = dy * _dsilu(z, sz)
        tri = (lax.broadcasted_iota(jnp.int32, (BLK, BLK), 0) >= lax.broadcasted_iota(jnp.int32, (BLK, BLK), 1))
        no_prev = (n == 0) & jnp.logical_not(tri)
        lane = lax.broadcasted_iota(jnp.int32, (1, 128), 1)
        dsink = jnp.zeros((1, 128), F32)
        scale = HEAD_DIM ** -0.5
        for g in range(2):
            kcat, vcat = _kv_cat(kvp, kvc, g)
            dk = jnp.zeros((2 * BLK, HEAD_DIM), F32)
            dv = jnp.zeros((2 * BLK, HEAD_DIM), F32)
            for j in range(8):
                h = g * 8 + j
                sl = slice(h * 64, (h + 1) * 64)
                qh = q[:, sl]
                p, pcat, psink, o = _att_head(qh, kcat, vcat, bias_ref[h], sink_ref[0:1, h:h + 1], tri, no_prev)
                doh = d_o[:, sl]
                dz_scr[:, sl] = dzf[:, sl] * o
                delta = jnp.sum(doh * o, axis=1, keepdims=True)
                dl = p * (_fold(_dot_nt(doh, vcat), tri) - delta)
                dsink = dsink + jnp.where(lane == h, -jnp.sum(psink * delta, axis=0, keepdims=True), 0.0)
                dbias_ref[h] += dl
                dlcat = _unfold(dl, tri)
                dq_scr[:, sl] = _dot(dlcat, kcat) * scale
                dk = dk + _dot_tn(dlcat, qh) * scale
                dv = dv + _dot_tn(pcat, doh)
            lo = g * HEAD_DIM
            prv[:, lo:lo + 64] = dk[0:BLK]
            cur[:, lo:lo + 64] = dk[BLK:2 * BLK]
            prv[:, 128 + lo:128 + lo + 64] = dv[0:BLK]
            cur[:, 128 + lo:128 + lo + 64] = dv[BLK:2 * BLK]
        dsink_ref[...] += dsink
        dpa_ref[:, 0:1024] = dq_scr[...].astype(dpa_ref.dtype)
        dpa_ref[:, 1024:2048] = dz_scr[...].astype(dpa_ref.dtype)
        dpa_ref[:, 2048:2304] = (cur[...] + carry[...]).astype(dpa_ref.dtype)
        carry[...] = prv[...]

    return pl.pallas_call(
        body, name="attn_bwd", grid=(nb,),
        in_specs=[pl.BlockSpec((BLK, ATT_COLS), lambda i: (nb - 1 - i, 0)),
                  pl.BlockSpec((BLK, 256), lambda i: (jnp.maximum(nb - 2 - i, 0), 8)),
                  pl.BlockSpec((BLK, 1024), lambda i: (nb - 1 - i, 0)),
                  _full((ATT_HEADS, BLK, BLK)), _full((1, 128)), _full((ATT_HEADS, BLK, BLK))],
        out_specs=[pl.BlockSpec((BLK, ATT_COLS), lambda i: (nb - 1 - i, 0)),
                   _full((ATT_HEADS, BLK, BLK)), _full((1, 128))],
        out_shape=[jax.ShapeDtypeStruct((s, ATT_COLS), ACT_DTYPE),
                   jax.ShapeDtypeStruct((ATT_HEADS, BLK, BLK), F32),
                   jax.ShapeDtypeStruct((1, 128), F32)],
        scratch_shapes=[pltpu.VMEM((BLK, 256), F32), pltpu.VMEM((BLK, 256), F32), pltpu.VMEM((BLK, 256), F32),
                        pltpu.VMEM((BLK, 1024), F32), pltpu.VMEM((BLK, 1024), F32)],
        compiler_params=_cparams(("arbitrary",)),
    )(pa, pa, dy, bias, sinks, dbias_in)


def _layernorm(v, g, b):
    mu = jnp.mean(v, axis=-1, keepdims=True)
    vc = v - mu
    rstd = lax.rsqrt(jnp.mean(vc * vc, axis=-1, keepdims=True) + EPS)
    xhat = vc * rstd
    return xhat, rstd, xhat * g + b


def sgu_fwd(ps, ln_g, ln_b, w_tril, b_t):
    s = ps.shape[0]

    def body(ps_ref, g_ref, b_ref, w_ref, bt_ref, y_ref):
        u = ps_ref[:, 0:1024].astype(F32)
        v = ps_ref[:, 1024:2048].astype(F32)
        z = ps_ref[:, 2048:3072].astype(F32)
        _, _, vn = _layernorm(v, g_ref[...], b_ref[...])
        gate = u * z * _sig(z)
        for g in range(SG_GROUPS):
            sl = slice(g * 128, (g + 1) * 128)
            mixed = _dot(w_ref[g], vn[:, sl]) + bt_ref[:, g:g + 1]
            y_ref[:, sl] = (gate[:, sl] * mixed).astype(y_ref.dtype)

    return pl.pallas_call(
        body, name="sgu_fwd", grid=(s // BLK,),
        in_specs=[pl.BlockSpec((BLK, SG_COLS), lambda c: (c, 0)), _full((1, 1024)), _full((1, 1024)),
                  _full((SG_GROUPS, BLK, BLK)), _full((BLK, 128))],
        out_specs=pl.BlockSpec((BLK, 1024), lambda c: (c, 0)),
        out_shape=jax.ShapeDtypeStruct((s, 1024), ACT_DTYPE),
        compiler_params=_cparams(("parallel",)),
    )(ps, ln_g, ln_b, w_tril, b_t)


def sgu_bwd(ps, dy, ln_g, ln_b, w_tril, w_tril_t, b_t):
    s = ps.shape[0]

    def body(ps_ref, dy_ref, g_ref, b_ref, w_ref, wt_ref, bt_ref, dps_ref, dw_ref, dbt_ref, dg_ref, db_ref, dvn_scr):
        @pl.when(pl.program_id(0) == 0)
        def _():
            dw_ref[...] = jnp.zeros_like(dw_ref)
            dbt_ref[...] = jnp.zeros_like(dbt_ref)
            dg_ref[...] = jnp.zeros_like(dg_ref)
            db_ref[...] = jnp.zeros_like(db_ref)

        u = ps_ref[:, 0:1024].astype(F32)
        v = ps_ref[:, 1024:2048].astype(F32)
        z = ps_ref[:, 2048:3072].astype(F32)
        dy = dy_ref[...].astype(F32)
        xhat, rstd, vn = _layernorm(v, g_ref[...], b_ref[...])
        sz = _sig(z)
        silu = z * sz
        row = lax.broadcasted_iota(jnp.int32, (BLK, BLK), 0)
        colm = lax.broadcasted_iota(jnp.int32, (BLK, BLK), 1)
        tril = row >= colm
        dbt = jnp.zeros((BLK, 128), F32)
        for g in range(SG_GROUPS):
            sl = slice(g * 128, (g + 1) * 128)
            vng = vn[:, sl]
            mixed = _dot(w_ref[g], vng) + bt_ref[:, g:g + 1]
            dyg, ug = dy[:, sl], u[:, sl]
            dps_ref[:, sl] = (dyg * mixed * silu[:, sl]).astype(dps_ref.dtype)
            dps_ref[:, 2048 + g * 128:2048 + (g + 1) * 128] = (
                dyg * ug * mixed * _dsilu(z[:, sl], sz[:, sl])).astype(dps_ref.dtype)
            dm = dyg * ug * silu[:, sl]
            dw_ref[g] += jnp.where(tril, _dot_nt(dm, vng), 0.0)
            dbt = dbt + jnp.where(colm == g, jnp.sum(dm, axis=1, keepdims=True), 0.0)
            dvn_scr[:, sl] = _dot(wt_ref[g], dm)
        dbt_ref[...] += dbt
        dvn = dvn_scr[...]
        dg_ref[...] += jnp.sum(dvn * xhat, axis=0, keepdims=True)
        db_ref[...] += jnp.sum(dvn, axis=0, keepdims=True)
        dxh = dvn * g_ref[...]
        dv = rstd * (dxh - jnp.mean(dxh, axis=-1, keepdims=True)
                     - xhat * jnp.mean(dxh * xhat, axis=-1, keepdims=True))
        dps_ref[:, 1024:2048] = dv.astype(dps_ref.dtype)

    return pl.pallas_call(
        body, name="sgu_bwd", grid=(s // BLK,),
        in_specs=[pl.BlockSpec((BLK, SG_COLS), lambda c: (c, 0)), pl.BlockSpec((BLK, 1024), lambda c: (c, 0)),
                  _full((1, 1024)), _full((1, 1024)), _full((SG_GROUPS, BLK, BLK)), _full((SG_GROUPS, BLK, BLK)),
                  _full((BLK, 128))],
        out_specs=[pl.BlockSpec((BLK, SG_COLS), lambda c: (c, 0)), _full((SG_GROUPS, BLK, BLK)), _full((BLK, 128)),
                   _full((1, 1024)), _full((1, 1024))],
        out_shape=[jax.ShapeDtypeStruct((s, SG_COLS), ACT_DTYPE), jax.ShapeDtypeStruct((SG_GROUPS, BLK, BLK), F32),
                   jax.ShapeDtypeStruct((BLK, 128), F32), jax.ShapeDtypeStruct((1, 1024), F32),
                   jax.ShapeDtypeStruct((1, 1024), F32)],
        scratch_shapes=[pltpu.VMEM((BLK, 1024), F32)],
        compiler_params=_cparams(("arbitrary",)),
    )(ps, dy, ln_g, ln_b, w_tril, w_tril_t, b_t)


def _shift_down(cur, prev16, k):
    if k == 0:
        return cur
    r = pltpu.roll(cur, k, 0)
    rp = pltpu.roll(prev16, k, 0)
    row = lax.broadcasted_iota(jnp.int32, (8, cur.shape[1]), 0)
    return jnp.concatenate([jnp.where(row < k, rp[0:8], r[0:8]), r[8:]], axis=0)


def _shift_up(cur, next16, k):
    if k == 0:
        return cur
    n = cur.shape[0]
    r = pltpu.roll(cur, n - k, 0)
    rn = pltpu.roll(next16, 16 - k, 0)
    row = lax.broadcasted_iota(jnp.int32, (8, cur.shape[1]), 0)
    return jnp.concatenate([r[:n - 8], jnp.where(row >= 8 - k, rn[8:16], r[n - 8:])], axis=0)


def _bcast8(v):
    return jnp.broadcast_to(v, (16, v.shape[1]))


class _Ssd:
    def __init__(self, xbc, prev16, dtr, cw, cbias, dtb, alog, dsk, tri, e):
        pre = cbias + cw[3:4] * xbc
        self.shifted = [xbc]
        for k in (1, 2, 3):
            sh = _shift_down(xbc, prev16, k)
            self.shifted.append(sh)
            pre = pre + cw[3 - k:4 - k] * sh
        self.pre = pre
        self.sg = _sig(pre)
        act = pre * self.sg
        self.xs = act[:, 0:SSM_WIDTH]
        self.bm = act[:, SSM_WIDTH:SSM_WIDTH + 512]
        self.cm = act[:, SSM_WIDTH + 512:CONV_DIM]
        self.dtp = dtr + dtb
        self.dt = jnp.maximum(self.dtp, 0.0) + jnp.log(1.0 + jnp.exp(-jnp.abs(self.dtp)))
        self.a = -jnp.exp(alog)
        self.acs = _dot_hi(tri, self.dt * self.a)
        self.acs_t = self.acs.T
        tot = self.acs[BLK - 1:BLK]
        self.ecs = jnp.exp(self.acs)
        self.dte = jnp.exp(tot - self.acs)
        self.cd = jnp.exp(tot)
        self.dt_x = _dot_onehot(self.dt, e)
        self.ecs_x = _dot_onehot(self.ecs, e)
        self.dte_x = _dot_onehot(self.dte, e)
        self.cd_x = _dot_onehot(_bcast8(self.cd), e)[0:1]
        self.d_x = _dot_onehot(_bcast8(dsk), e)[0:1]
        self.xdt = self.xs * self.dt_x
        row = lax.broadcasted_iota(jnp.int32, (BLK, BLK), 0)
        col = lax.broadcasted_iota(jnp.int32, (BLK, BLK), 1)
        self.tril = row >= col

    def group(self, g):
        sl = slice(g * 128, (g + 1) * 128)
        bg, cg = self.bm[:, sl], self.cm[:, sl]
        return bg, cg, _dot_nt(cg, bg)

    def decay(self, h):
        seg = self.acs[:, h:h + 1] - self.acs_t[h:h + 1, :]
        return jnp.exp(jnp.where(self.tril, seg, NEG))

    def y_pre_gate(self, ht_of, yd_scr, yoff_scr):
        for g in range(SSM_GROUPS):
            bg, cg, cb = self.group(g)
            for j in range(8):
                h = g * 8 + j
                sl = slice(h * 64, (h + 1) * 64)
                yd_scr[:, sl] = _dot(cb * self.decay(h), self.xdt[:, sl])
            gs = slice(g * SSM_GW, (g + 1) * SSM_GW)
            yoff_scr[:, gs] = _dot(cg, ht_of(g)) * self.ecs_x[:, gs]
        return yd_scr[...] + yoff_scr[...] + self.d_x * self.xs


def _ssd_consts():
    hh = lax.broadcasted_iota(jnp.int32, (128, SSM_WIDTH), 0)
    ch = lax.broadcasted_iota(jnp.int32, (128, SSM_WIDTH), 1)
    e = (ch // 64 == hh).astype(jnp.bfloat16)
    row = lax.broadcasted_iota(jnp.int32, (BLK, BLK), 0)
    col = lax.broadcasted_iota(jnp.int32, (BLK, BLK), 1)
    tri = (row >= col).astype(F32)
    return tri, e


def _pad_lanes(v, n=128):
    return jnp.pad(v, ((0, 0), (0, n - v.shape[1])))


def ssd_fwd(pm, cw, cbias, dtb, alog, dsk, ng):
    s = pm.shape[0]
    nc = s // BLK
    tri, e = _ssd_consts()

    def body(pm_ref, prev_ref, cw_ref, cb_ref, dtb_ref, al_ref, d_ref, ng_ref, tri_ref, e_ref,
             y_ref, st_ref, ht_ref, yd_scr, yoff_scr):
        c = pl.program_id(0)

        @pl.when(c == 0)
        def _():
            ht_ref[...] = jnp.zeros_like(ht_ref)

        xbc = pm_ref[:, 0:CONV_DIM].astype(F32)
        prev16 = jnp.where(c == 0, 0.0, prev_ref[...].astype(F32))
        f = _Ssd(xbc, prev16, pm_ref[:, DT_OFF:DT_OFF + 128].astype(F32), cw_ref[...], cb_ref[...], dtb_ref[...],
                 al_ref[...], d_ref[...], tri_ref[...], e_ref[...])
        st_ref[0] = ht_ref[...]
        y = f.y_pre_gate(lambda g: ht_ref[g], yd_scr, yoff_scr)
        for g in range(SSM_GROUPS):
            bg, _, _ = f.group(g)
            gs = slice(g * SSM_GW, (g + 1) * SSM_GW)
            ht_ref[g] = ht_ref[g] * f.cd_x[:, gs] + _dot_tn(bg, f.xdt[:, gs] * f.dte_x[:, gs])
        z = pm_ref[:, CONV_DIM:CONV_DIM + SSM_WIDTH].astype(F32)
        ypre = y * z * _sig(z)
        for g in range(SSM_GROUPS):
            gs = slice(g * SSM_GW, (g + 1) * SSM_GW)
            yg = ypre[:, gs]
            rr = lax.rsqrt(jnp.mean(yg * yg, axis=-1, keepdims=True) + EPS)
            y_ref[:, gs] = (yg * rr * ng_ref[:, gs]).astype(y_ref.dtype)

    return pl.pallas_call(
        body, name="ssd_fwd", grid=(nc,),
        in_specs=[pl.BlockSpec((BLK, SSM_COLS), lambda c: (c, 0)),
                  pl.BlockSpec((16, CONV_DIM), lambda c: (jnp.maximum(8 * c - 1, 0), 0)),
                  _full((4, CONV_DIM)), _full((1, CONV_DIM)), _full((1, 128)), _full((1, 128)), _full((1, 128)),
                  _full((1, SSM_WIDTH)), _full((BLK, BLK)), _full((128, SSM_WIDTH))],
        out_specs=[pl.BlockSpec((BLK, SSM_WIDTH), lambda c: (c, 0)),
                   pl.BlockSpec((1, SSM_GROUPS, 128, SSM_GW), lambda c: (c, 0, 0, 0))],
        out_shape=[jax.ShapeDtypeStruct((s, SSM_WIDTH), ACT_DTYPE),
                   jax.ShapeDtypeStruct((nc, SSM_GROUPS, 128, SSM_GW), F32)],
        scratch_shapes=[pltpu.VMEM((SSM_GROUPS, 128, SSM_GW), F32), pltpu.VMEM((BLK, SSM_WIDTH), F32),
                        pltpu.VMEM((BLK, SSM_WIDTH), F32)],
        compiler_params=_cparams(("arbitrary",)),
    )(pm, pm, cw, cbias, dtb, alog, dsk, ng, tri, e)


def ssd_bwd(pm, dy, states, cw, cbias, dtb, alog, dsk, ng):
    s = pm.shape[0]
    nc = s // BLK
    tri, e = _ssd_consts()
    tri_t, e_t = tri.T, e.T

    def body(pm_ref, prev_ref, dy_ref, st_ref, cw_ref, cb_ref, dtb_ref, al_ref, d_ref, ng_ref,
             tri_ref, trit_ref, e_ref, et_ref,
             dpm_ref, dcw_ref, dcb_ref, dvec_ref, dng_ref,
             dht_ref, dcar_ref, yd_scr, yoff_scr, dx_scr, r2_scr, hs_scr, da_scr, dat_scr, dd_scr, dbc_scr):
        i = pl.program_id(0)
        n = nc - 1 - i

        @pl.when(i == 0)
        def _():
            dht_ref[...] = jnp.zeros_like(dht_ref)
            dcar_ref[...] = jnp.zeros_like(dcar_ref)
            dcw_ref[...] = jnp.zeros_like(dcw_ref)
            dcb_ref[...] = jnp.zeros_like(dcb_ref)
            dvec_ref[...] = jnp.zeros_like(dvec_ref)
            dng_ref[...] = jnp.zeros_like(dng_ref)
            dd_scr[...] = jnp.zeros_like(dd_scr)
            da_scr[...] = jnp.zeros_like(da_scr)
            dat_scr[...] = jnp.zeros_like(dat_scr)

        xbc = pm_ref[:, 0:CONV_DIM].astype(F32)
        prev16 = jnp.where(n == 0, 0.0, prev_ref[...].astype(F32))
        cw = cw_ref[...]
        f = _Ssd(xbc, prev16, pm_ref[:, DT_OFF:DT_OFF + 128].astype(F32), cw, cb_ref[...], dtb_ref[...],
                 al_ref[...], d_ref[...], tri_ref[...], e_ref[...])
        et = et_ref[...]
        y = f.y_pre_gate(lambda g: st_ref[0, g], yd_scr, yoff_scr)

        z = pm_ref[:, CONV_DIM:CONV_DIM + SSM_WIDTH].astype(F32)
        dyv = dy_ref[...].astype(F32)
        sz = _sig(z)
        silu = z * sz
        ypre = y * silu
        for g in range(SSM_GROUPS):
            gs = slice(g * SSM_GW, (g + 1) * SSM_GW)
            yg = ypre[:, gs]
            rr = lax.rsqrt(jnp.mean(yg * yg, axis=-1, keepdims=True) + EPS)
            nrm = yg * rr
            dng_ref[:, gs] += jnp.sum(dyv[:, gs] * nrm, axis=0, keepdims=True)
            dn = dyv[:, gs] * ng_ref[:, gs]
            dx_scr[:, gs] = rr * (dn - nrm * jnp.mean(dn * nrm, axis=-1, keepdims=True))
        dypre = dx_scr[...]
        d_y = dypre * silu
        dpm_ref[:, CONV_DIM:CONV_DIM + SSM_WIDTH] = (dypre * y * _dsilu(z, sz)).astype(dpm_ref.dtype)

        for g in range(SSM_GROUPS):
            bg, cg, cb = f.group(g)
            gs = slice(g * SSM_GW, (g + 1) * SSM_GW)
            htg = st_ref[0, g]
            dhn = dht_ref[g]
            dcb = jnp.zeros((BLK, BLK), F32)
            for j in range(8):
                h = g * 8 + j
                sl = slice(h * 64, (h + 1) * 64)
                dec = f.decay(h)
                dyh = d_y[:, sl]
                dmd = _dot_nt(dyh, f.xdt[:, sl]) * dec
                dcb = dcb + dmd
                gm = dmd * cb
                da_scr[:, h:h + 1] = jnp.sum(gm, axis=1, keepdims=True)
                dat_scr[h:h + 1, :] = jnp.sum(gm, axis=0, keepdims=True)
                dx_scr[:, sl] = _dot_tn(cb * dec, dyh)
            dz = f.ecs_x[:, gs] * d_y[:, gs]
            dbc_scr[:, 512 + g * 128:512 + (g + 1) * 128] = _dot(dcb, bg) + _dot_nt(dz, htg)
            dbc_scr[:, g * 128:(g + 1) * 128] = _dot_tn(dcb, cg) + _dot_nt(f.xdt[:, gs] * f.dte_x[:, gs], dhn)
            dws = _dot(bg, dhn)
            dx_scr[:, gs] += f.dte_x[:, gs] * dws
            r2_scr[:, gs] = dws * f.xdt[:, gs]
            hs_scr[:, gs] = _bcast8(jnp.sum(dhn * htg, axis=0, keepdims=True))
            dht_ref[g] = f.cd_x[:, gs] * dhn + _dot_tn(cg, dz)
        d_x = dx_scr[...]
        r1 = _dot_onehot(d_y * yoff_scr[...], et)
        r2 = _dot_onehot(r2_scr[...], et) * f.dte
        dcd = _dot_onehot(hs_scr[...], et)[0:1]
        d_tot = jnp.sum(r2, axis=0, keepdims=True) + dcd * f.cd
        row = lax.broadcasted_iota(jnp.int32, (BLK, 128), 0)
        d_a = da_scr[...] - dat_scr[...].T + r1 - r2 + jnp.where(row == BLK - 1, d_tot, 0.0)
        dadt = _dot_hi(trit_ref[...], d_a)
        ddt = dadt * f.a + _dot_onehot(d_x * f.xs, et)
        lane = lax.broadcasted_iota(jnp.int32, (BLK, 128), 1)
        dr = jnp.where(lane < SSM_HEADS, ddt * _sig(f.dtp), 0.0)
        dvec_ref[0:1, :] += jnp.sum(dr, axis=0, keepdims=True)
        dvec_ref[1:2, :] += jnp.sum(dadt * f.dt, axis=0, keepdims=True) * f.a
        dd_scr[...] += _bcast8(jnp.sum(d_y * f.xs, axis=0, keepdims=True))
        dpm_ref[:, DT_OFF:DT_OFF + 128] = dr.astype(dpm_ref.dtype)
        dpm_ref[:, DT_OFF + 128:SSM_COLS] = jnp.zeros((BLK, 128), dpm_ref.dtype)

        dxs = d_x * f.dt_x + f.d_x * d_y
        dact = jnp.concatenate([dxs, dbc_scr[...]], axis=1)
        dpre = dact * _dsilu(f.pre, f.sg)
        dcb_ref[...] += jnp.sum(dpre, axis=0, keepdims=True)
        dxraw = jnp.zeros((BLK, CONV_DIM), F32)
        nxt = dcar_ref[...]
        for k in range(4):
            dcw_ref[3 - k:4 - k, :] += jnp.sum(dpre * f.shifted[k], axis=0, keepdims=True)
            dxraw = dxraw + cw[3 - k:4 - k] * _shift_up(dpre, nxt, k)
        dcar_ref[...] = dpre[0:16]
        dpm_ref[:, 0:CONV_DIM] = dxraw.astype(dpm_ref.dtype)

        @pl.when(i == nc - 1)
        def _():
            dvec_ref[2:3, :] = _dot_onehot(dd_scr[...], et)[0:1]

    return pl.pallas_call(
        body, name="ssd_bwd", grid=(nc,),
        in_specs=[pl.BlockSpec((BLK, SSM_COLS), lambda i: (nc - 1 - i, 0)),
                  pl.BlockSpec((16, CONV_DIM), lambda i: (jnp.maximum(8 * (nc - 1 - i) - 1, 0), 0)),
                  pl.BlockSpec((BLK, SSM_WIDTH), lambda i: (nc - 1 - i, 0)),
                  pl.BlockSpec((1, SSM_GROUPS, 128, SSM_GW), lambda i: (nc - 1 - i, 0, 0, 0)),
                  _full((4, CONV_DIM)), _full((1, CONV_DIM)), _full((1, 128)), _full((1, 128)), _full((1, 128)),
                  _full((1, SSM_WIDTH)), _full((BLK, BLK)), _full((BLK, BLK)), _full((128, SSM_WIDTH)),
                  _full((SSM_WIDTH, 128))],
        out_specs=[pl.BlockSpec((BLK, SSM_COLS), lambda i: (nc - 1 - i, 0)),
                   _full((8, CONV_DIM)), _full((1, CONV_DIM)), _full((8, 128)), _full((1, SSM_WIDTH))],
        out_shape=[jax.ShapeDtypeStruct((s, SSM_COLS), ACT_DTYPE), jax.ShapeDtypeStruct((8, CONV_DIM), F32),
                   jax.ShapeDtypeStruct((1, CONV_DIM), F32), jax.ShapeDtypeStruct((8, 128), F32),
                   jax.ShapeDtypeStruct((1, SSM_WIDTH), F32)],
        scratch_shapes=[pltpu.VMEM((SSM_GROUPS, 128, SSM_GW), F32), pltpu.VMEM((16, CONV_DIM), F32),
                        pltpu.VMEM((BLK, SSM_WIDTH), F32), pltpu.VMEM((BLK, SSM_WIDTH), F32),
                        pltpu.VMEM((BLK, SSM_WIDTH), F32), pltpu.VMEM((BLK, SSM_WIDTH), F32),
                        pltpu.VMEM((16, SSM_WIDTH), F32), pltpu.VMEM((BLK, 128), F32), pltpu.VMEM((128, BLK), F32),
                        pltpu.VMEM((16, SSM_WIDTH), F32), pltpu.VMEM((BLK, 1024), F32)],
        compiler_params=_cparams(("arbitrary",)),
    )(pm, pm, dy, states, cw, cbias, dtb, alog, dsk, ng, tri, tri_t, e, e_t)


def merge_fwd(x, ya, ys, ym, pg, wa, ws, wm, wo, g_post):
    s, d = x.shape
    tm = min(256, s)

    def body(x_ref, ya_ref, ys_ref, ym_ref, pg_ref, wa_ref, ws_ref, wm_ref, wo_ref, g_ref,
             xo_ref, ba_ref, bs_ref, bm_ref, mg_ref, out_ref):
        ba = _dot(ya_ref[...], wa_ref[...])
        bs = _dot(ys_ref[...], ws_ref[...])
        bm = _dot(ym_ref[...], wm_ref[...])
        merged = (_sig(pg_ref[:, 0:d].astype(F32)) * ba + _sig(pg_ref[:, d:2 * d].astype(F32)) * bs
                  + _sig(pg_ref[:, 2 * d:3 * d].astype(F32)) * bm)
        out = _dot(merged, wo_ref[...])
        r = lax.rsqrt(jnp.mean(out * out, axis=-1, keepdims=True) + EPS)
        xo_ref[...] = x_ref[...] + out * r * g_ref[...]
        ba_ref[...] = ba.astype(ba_ref.dtype)
        bs_ref[...] = bs.astype(bs_ref.dtype)
        bm_ref[...] = bm.astype(bm_ref.dtype)
        mg_ref[...] = merged.astype(mg_ref.dtype)
        out_ref[...] = out.astype(out_ref.dtype)

    rows = lambda w: pl.BlockSpec((tm, w), lambda i: (i, 0))
    act = jax.ShapeDtypeStruct((s, d), ACT_DTYPE)
    return pl.pallas_call(
        body, name="merge_fwd", grid=(s // tm,),
        in_specs=[rows(d), rows(d), rows(d), rows(2 * d), rows(3 * d), _full((d, d)), _full((d, d)),
                  _full((2 * d, d)), _full((d, d)), _full((1, d))],
        out_specs=[rows(d)] * 6,
        out_shape=[jax.ShapeDtypeStruct((s, d), F32), act, act, act, act, act],
        compiler_params=_cparams(("parallel",)),
    )(x, ya, ys, ym, pg, wa, ws, wm, wo, g_post)


def merge_bwd(dx, out_s, pg, ba, bs, bm, wa, ws, wm, wo, g_post):
    s, d = dx.shape
    tm = min(256, s)

    def body(dx_ref, out_ref, pg_ref, ba_ref, bs_ref, bm_ref, wa_ref, ws_ref, wm_ref, wo_ref, g_ref,
             dout_ref, dba_ref, dbs_ref, dbm_ref, dpg_ref, dya_ref, dys_ref, dym_ref, dg_ref):
        @pl.when(pl.program_id(0) == 0)
        def _():
            dg_ref[...] = jnp.zeros_like(dg_ref)

        o = out_ref[...].astype(F32)
        dxv = dx_ref[...]
        r = lax.rsqrt(jnp.mean(o * o, axis=-1, keepdims=True) + EPS)
        nrm = o * r
        dg_ref[...] += jnp.sum(dxv * nrm, axis=0, keepdims=True)
        dn = dxv * g_ref[...]
        dout = r * (dn - nrm * jnp.mean(dn * nrm, axis=-1, keepdims=True))
        dout_ref[...] = dout.astype(dout_ref.dtype)
        dmerged = _dot_nt(dout, wo_ref[...])
        for q, (b_ref, db_ref, w_ref, dy_ref) in enumerate(((ba_ref, dba_ref, wa_ref, dya_ref),
                                                            (bs_ref, dbs_ref, ws_ref, dys_ref),
                                                            (bm_ref, dbm_ref, wm_ref, dym_ref))):
            gt = _sig(pg_ref[:, q * d:(q + 1) * d].astype(F32))
            db = dmerged * gt
            db_ref[...] = db.astype(db_ref.dtype)
            dpg_ref[:, q * d:(q + 1) * d] = (dmerged * b_ref[...].astype(F32) * gt * (1.0 - gt)).astype(dpg_ref.dtype)
            dy_ref[...] = _dot_nt(db, w_ref[...]).astype(dy_ref.dtype)

    rows = lambda w: pl.BlockSpec((tm, w), lambda i: (i, 0))
    act = lambda w: jax.ShapeDtypeStruct((s, w), ACT_DTYPE)
    return pl.pallas_call(
        body, name="merge_bwd", grid=(s // tm,),
        in_specs=[rows(d), rows(d), rows(3 * d), rows(d), rows(d), rows(d), _full((d, d)), _full((d, d)),
                  _full((2 * d, d)), _full((d, d)), _full((1, d))],
        out_specs=[rows(d), rows(d), rows(d), rows(d), rows(3 * d), rows(d), rows(d), rows(2 * d), _full((1, d))],
        out_shape=[act(d), act(d), act(d), act(d), act(3 * d), act(d), act(d), act(2 * d),
                   jax.ShapeDtypeStruct((1, d), F32)],
        compiler_params=_cparams(("arbitrary",)),
    )(dx, out_s, pg, ba, bs, bm, wa, ws, wm, wo, g_post)


def loss_grad(y, target):
    s, d = y.shape
    tm = min(512, s)

    def body(y_ref, t_ref, dy_ref, l_ref):
        @pl.when(pl.program_id(0) == 0)
        def _():
            l_ref[...] = jnp.zeros_like(l_ref)

        err = y_ref[...] - t_ref[...]
        dy_ref[...] = err * (1.0 / d)
        part = jnp.sum(jnp.sum(err * err, axis=-1, keepdims=True) * (1.0 / d), axis=0, keepdims=True)
        l_ref[...] += 0.5 * jnp.broadcast_to(part, l_ref.shape)

    return pl.pallas_call(
        body, name="loss_grad", grid=(s // tm,),
        in_specs=[pl.BlockSpec((tm, d), lambda i: (i, 0)), pl.BlockSpec((tm, d), lambda i: (i, 0))],
        out_specs=[pl.BlockSpec((tm, d), lambda i: (i, 0)), _full((8, 128))],
        out_shape=[jax.ShapeDtypeStruct((s, d), F32), jax.ShapeDtypeStruct((8, 128), F32)],
        compiler_params=_cparams(("arbitrary",)),
    )(y, target)


def _mesh_pos():
    x, y, c = lax.axis_index("x"), lax.axis_index("y"), lax.axis_index("c")
    return x, y, c, 4 * x + 2 * y + c


def _peer(x, y, c, k):
    px = 1 - x if k & 4 else x
    py = 1 - y if k & 2 else y
    pc = 1 - c if k & 1 else c
    return (px, py, pc), 4 * px + 2 * py + pc


def exchange(scattered, gathered, name):
    ns = len(scattered)
    arrays = list(scattered) + list(gathered)
    na = len(arrays)

    def body(*refs):
        ins, outs = refs[:na], refs[na:2 * na]
        send_sems, recv_sems, local_sems = refs[2 * na:]
        x, y, c, me = _mesh_pos()

        def src(q, slot):
            return ins[q].at[slot] if q < ns else ins[q]

        locals_ = [pltpu.make_async_copy(src(q, me), outs[q].at[me], local_sems.at[q]) for q in range(na)]
        for cp in locals_:
            cp.start()
        sends = []
        for k in range(1, N_DEV):
            peer, pidx = _peer(x, y, c, k)
            for q in range(na):
                cp = pltpu.make_async_remote_copy(src_ref=src(q, pidx), dst_ref=outs[q].at[me],
                                                  send_sem=send_sems.at[q, k - 1], recv_sem=recv_sems.at[q, k - 1],
                                                  device_id=peer, device_id_type=MESH)
                cp.start()
                sends.append(cp)
        for k in range(1, N_DEV):
            peer, pidx = _peer(x, y, c, k)
            for q in range(na):
                pltpu.make_async_remote_copy(src_ref=src(q, pidx), dst_ref=outs[q].at[pidx],
                                             send_sem=send_sems.at[q, k - 1], recv_sem=recv_sems.at[q, k - 1],
                                             device_id=peer, device_id_type=MESH).wait_recv()
        for cp in sends:
            cp.wait_send()
        for cp in locals_:
            cp.wait()

    any_spec = pl.BlockSpec(memory_space=pl.ANY)
    out_shape = ([jax.ShapeDtypeStruct(a.shape, a.dtype) for a in scattered]
                 + [jax.ShapeDtypeStruct((N_DEV,) + a.shape, a.dtype) for a in gathered])
    return pl.pallas_call(
        body, name=name, in_specs=[any_spec] * na, out_specs=[any_spec] * na, out_shape=out_shape,
        scratch_shapes=[pltpu.SemaphoreType.DMA((na, N_DEV - 1)), pltpu.SemaphoreType.DMA((na, N_DEV - 1)),
                        pltpu.SemaphoreType.DMA((na,))],
    )(*arrays)


def adamw(parts, w, m, v, tile, name):
    npart, r, dp = parts.shape
    d = w.shape[1]

    def body(p_ref, w_ref, m_ref, v_ref, g_ref, dw_ref, nm_ref, nv_ref):
        g = p_ref[0, :, 0:d].astype(F32)
        for q in range(1, npart):
            g = g + p_ref[q, :, 0:d].astype(F32)
        g_ref[...] = g
        nm = ADAM_B1 * m_ref[...] + (1.0 - ADAM_B1) * g
        nv = ADAM_B2 * v_ref[...] + (1.0 - ADAM_B2) * (g * g)
        nm_ref[...] = nm
        nv_ref[...] = nv
        m_hat = nm / (1.0 - ADAM_B1 ** ADAM_STEP)
        v_hat = nv / (1.0 - ADAM_B2 ** ADAM_STEP)
        dw_ref[...] = -ADAM_LR * (m_hat / (jnp.sqrt(v_hat) + ADAM_EPS) + ADAM_WD * w_ref[...])

    rows = pl.BlockSpec((tile, d), lambda i: (i, 0))
    out = jax.ShapeDtypeStruct((r, d), F32)
    return pl.pallas_call(
        body, name=name, grid=(r // tile,),
        in_specs=[pl.BlockSpec((npart, tile, dp), lambda i: (0, i, 0)), rows, rows, rows],
        out_specs=[rows] * 4, out_shape=[out] * 4,
        compiler_params=_cparams(("parallel",)),
    )(parts, w, m, v)


def _pad_rows(a, rows):
    return jnp.pad(a, ((0, rows - a.shape[0]), (0, 0)))


def _pack_rest(w_att, w_sg, w_ssm, w_out):
    parts = []
    for l in range(2):
        parts += [w_att[l], w_sg[l], w_ssm[l], w_out[l]]
    return jnp.concatenate(parts, axis=0)


def _unpack_rest(p):
    outs = [[], [], [], []]
    o = 0
    for l in range(2):
        for q, rws in enumerate(REST_PARTS):
            outs[q].append(p[o:o + rws])
            o += rws
    return [jnp.stack(t) for t in outs]


def _pack_win(w_in):
    return jnp.pad(w_in.reshape(2 * D_MODEL, WIN_SHARD), ((0, 0), (0, WIN_LANES - WIN_SHARD)))


W_IN_MAP = ((0, 1024, "att", 0), (1024, 1280, "att", 2048), (1280, 2304, "att", 1024), (2304, 5376, "sg", 0),
            (5376, 7424, "ssm", 3072), (7424, 10496, "ssm", 0), (10496, 10528, "ssm", 5120), (10528, 13600, "gate", 0))
SLAB_COLS = {"att": ATT_COLS, "sg": SG_COLS, "ssm": SSM_COLS, "gate": GATE_COLS}


def _slabs_from_shards(g):
    slabs = {}
    for name, width in SLAB_COLS.items():
        pieces, filled = [], 0
        for ga, gb, _, off in sorted((m for m in W_IN_MAP if m[2] == name), key=lambda m: m[3]):
            assert off == filled
            a = ga
            while a < gb:
                d = a // WIN_SHARD
                hi = min(gb, WIN_SHARD * (d + 1))
                pieces.append(g[d, :, a - WIN_SHARD * d:hi - WIN_SHARD * d])
                a = hi
            filled += gb - ga
        if filled < width:
            pieces.append(jnp.zeros((D_MODEL, width - filled), g.dtype))
        slabs[name] = jnp.concatenate(pieces, axis=1)
    return slabs


def _shards_from_slabs(dslabs):
    out = []
    for d in range(N_DEV):
        a, b = WIN_SHARD * d, WIN_SHARD * (d + 1)
        pieces = []
        for ga, gb, name, off in W_IN_MAP:
            lo, hi = max(a, ga), min(b, gb)
            if lo < hi:
                pieces.append(dslabs[name][:, off + lo - ga:off + hi - ga])
        pieces.append(jnp.zeros((D_MODEL, WIN_LANES - WIN_SHARD), pieces[0].dtype))
        out.append(jnp.concatenate(pieces, axis=1).astype(WIRE_DTYPE))
    return jnp.stack(out)


SMALL_SIZES = (("norm_pre", 2048), ("norm_post", 2048), ("rel_bias", 512), ("att_sinks", 32), ("sg_ln_g", 2048),
               ("sg_ln_b", 2048), ("sg_w", 262144), ("sg_b", 2048), ("ssm_conv_b", 6144), ("ssm_dt_bias", 64),
               ("ssm_a_log", 64), ("ssm_d", 64), ("ssm_norm_g", 4096), ("conv_w_full", 24576))


def _pack_small(d):
    parts = []
    for name, size in SMALL_SIZES:
        rows = 8 * (-(-size // (8 * D_MODEL)))
        flat = d[name].reshape(-1) if name in d else jnp.zeros((size,), F32)
        parts.append(jnp.pad(flat, (0, rows * D_MODEL - size)).reshape(rows, D_MODEL))
    return _pad_rows(jnp.concatenate(parts, axis=0), SMALL_ROWS)


def _unpack_small(p, shapes):
    out, o = {}, 0
    for name, size in SMALL_SIZES:
        rows = 8 * (-(-size // (8 * D_MODEL)))
        if name in shapes:
            out[name] = p[o:o + rows].reshape(-1)[:size].reshape(shapes[name])
        o += rows
    return out


def _bucket_onehot_t():
    qi = jnp.arange(BLK, dtype=jnp.int32)[:, None]
    kj = jnp.arange(BLK, dtype=jnp.int32)[None, :]
    dd = (qi - kj) & (BLK - 1)
    in_window = dd >= 0
    max_exact = REL_BUCKETS // 2
    dist_f = jnp.maximum(dd, 1).astype(F32)
    large = max_exact + (jnp.log(dist_f / max_exact) / math.log(128 / max_exact)
                         * (REL_BUCKETS - max_exact)).astype(jnp.int32)
    large = jnp.minimum(large, REL_BUCKETS - 1)
    bucket = jnp.where(dd < max_exact, dd, large).reshape(1, -1)
    onehot_t = (bucket == jnp.arange(REL_BUCKETS, dtype=jnp.int32)[:, None]).astype(F32)
    maskadd = jnp.where(in_window, 0.0, NEG).astype(F32).reshape(1, -1)
    return onehot_t, maskadd


WEIGHTS = ['w_in', 'norm_pre', 'norm_post', 'rel_bias', 'att_sinks', 'sg_ln_g', 'sg_ln_b', 'sg_w', 'sg_b',
           'ssm_conv_w', 'ssm_conv_b', 'ssm_dt_bias', 'ssm_a_log', 'ssm_d', 'ssm_norm_g',
           'w_br_att', 'w_br_sg', 'w_br_ssm', 'w_out']
REST = ('w_br_att', 'w_br_sg', 'w_br_ssm', 'w_out')


def kernel(x, w_in, norm_pre, norm_post, rel_bias, att_sinks, sg_ln_g, sg_ln_b, sg_w, sg_b, ssm_conv_w, ssm_conv_b, ssm_dt_bias, ssm_a_log, ssm_d, ssm_norm_g, w_br_att, w_br_sg, w_br_ssm, w_out, loss_target, m_w_in, m_norm_pre, m_norm_post, m_rel_bias, m_att_sinks, m_sg_ln_g, m_sg_ln_b, m_sg_w, m_sg_b, m_ssm_conv_w, m_ssm_conv_b, m_ssm_dt_bias, m_ssm_a_log, m_ssm_d, m_ssm_norm_g, m_w_br_att, m_w_br_sg, m_w_br_ssm, m_w_out, v_w_in, v_norm_pre, v_norm_post, v_rel_bias, v_att_sinks, v_sg_ln_g, v_sg_ln_b, v_sg_w, v_sg_b, v_ssm_conv_w, v_ssm_conv_b, v_ssm_dt_bias, v_ssm_a_log, v_ssm_d, v_ssm_norm_g, v_w_br_att, v_w_br_sg, v_w_br_ssm, v_w_out):
    w = dict(w_in=w_in, norm_pre=norm_pre, norm_post=norm_post, rel_bias=rel_bias, att_sinks=att_sinks,
             sg_ln_g=sg_ln_g, sg_ln_b=sg_ln_b, sg_w=sg_w, sg_b=sg_b, ssm_conv_w=ssm_conv_w, ssm_conv_b=ssm_conv_b,
             ssm_dt_bias=ssm_dt_bias, ssm_a_log=ssm_a_log, ssm_d=ssm_d, ssm_norm_g=ssm_norm_g,
             w_br_att=w_br_att, w_br_sg=w_br_sg, w_br_ssm=w_br_ssm, w_out=w_out)
    mom = dict(w_in=m_w_in, norm_pre=m_norm_pre, norm_post=m_norm_post, rel_bias=m_rel_bias, att_sinks=m_att_sinks,
               sg_ln_g=m_sg_ln_g, sg_ln_b=m_sg_ln_b, sg_w=m_sg_w, sg_b=m_sg_b, ssm_conv_w=m_ssm_conv_w,
               ssm_conv_b=m_ssm_conv_b, ssm_dt_bias=m_ssm_dt_bias, ssm_a_log=m_ssm_a_log, ssm_d=m_ssm_d,
               ssm_norm_g=m_ssm_norm_g, w_br_att=m_w_br_att, w_br_sg=m_w_br_sg, w_br_ssm=m_w_br_ssm, w_out=m_w_out)
    var = dict(w_in=v_w_in, norm_pre=v_norm_pre, norm_post=v_norm_post, rel_bias=v_rel_bias, att_sinks=v_att_sinks,
               sg_ln_g=v_sg_ln_g, sg_ln_b=v_sg_ln_b, sg_w=v_sg_w, sg_b=v_sg_b, ssm_conv_w=v_ssm_conv_w,
               ssm_conv_b=v_ssm_conv_b, ssm_dt_bias=v_ssm_dt_bias, ssm_a_log=v_ssm_a_log, ssm_d=v_ssm_d,
               ssm_norm_g=v_ssm_norm_g, w_br_att=v_w_br_att, w_br_sg=v_w_br_sg, w_br_ssm=v_w_br_ssm, w_out=v_w_out)
    xs0 = x[0]
    target = loss_target[0]
    my_dev = 4 * lax.axis_index("x") + 2 * lax.axis_index("y") + lax.axis_index("c")

    conv_shard = _pad_rows(ssm_conv_w.reshape(-1, D_MODEL), 8)
    g_win, g_rest, gathered_conv = exchange(
        [], [_pack_win(w_in).astype(WIRE_DTYPE), _pack_rest(*[w[n] for n in REST]).astype(WIRE_DTYPE), conv_shard],
        "all_gather")
    conv_full = gathered_conv[:, 0:3].reshape(N_DEV, 2, 4, 384).transpose(1, 2, 0, 3).reshape(2, 4, CONV_DIM)

    layers = []
    o = 0
    for l in range(2):
        slabs = _slabs_from_shards(g_win[:, l * D_MODEL:(l + 1) * D_MODEL])
        lw = {"in_" + name: slab.astype(MXU_DTYPE) for name, slab in slabs.items()}
        for name, rws in zip(("att", "sg", "ssm", "out"), REST_PARTS):
            lw[name] = g_rest[:, o:o + rws].reshape(N_DEV * rws, D_MODEL).astype(MXU_DTYPE)
            o += rws
        tril = jnp.tril(jnp.ones((BLK, BLK), bool))
        sgw = jnp.where(tril[None], sg_w[l], 0.0)
        lw.update(
            g_pre=norm_pre[l][None], g_post=norm_post[l][None], sinks=_pad_lanes(att_sinks[l][None]),
            ln_g=sg_ln_g[l][None], ln_b=sg_ln_b[l][None], sgw=sgw.astype(MXU_DTYPE),
            sgw_t=sgw.transpose(0, 2, 1).astype(MXU_DTYPE), sgb_t=_pad_lanes(sg_b[l].T),
            cw=conv_full[l], cb=ssm_conv_b[l][None], dtb=_pad_lanes(ssm_dt_bias[l][None]),
            alog=_pad_lanes(ssm_a_log[l][None]), dsk=_pad_lanes(ssm_d[l][None]), ng=ssm_norm_g[l][None])
        layers.append(lw)

    onehot_t, maskadd = _bucket_onehot_t()
    bias = bias_table(rel_bias.T, onehot_t, maskadd).reshape(ATT_HEADS, BLK, BLK)

    saved = []
    xl = xs0
    for lw in layers:
        h = rmsnorm_fwd(xl, lw["g_pre"])
        pa = mm_nn(h, lw["in_att"], 1152, "proj_att")
        ps = mm_nn(h, lw["in_sg"], 1536, "proj_sg")
        pm = mm_nn(h, lw["in_ssm"], 1792, "proj_ssm")
        pg = mm_nn(h, lw["in_gate"], 1536, "proj_gate")
        ya = attn_fwd(pa, bias, lw["sinks"])
        ys = sgu_fwd(ps, lw["ln_g"], lw["ln_b"], lw["sgw"], lw["sgb_t"])
        ym, states = ssd_fwd(pm, lw["cw"], lw["cb"], lw["dtb"], lw["alog"], lw["dsk"], lw["ng"])
        x_next, ba, bs, bm, merged, out_s = merge_fwd(xl, ya, ys, ym, pg, lw["att"], lw["sg"], lw["ssm"], lw["out"],
                                                      lw["g_post"])
        saved.append(dict(x=xl, h=h, pa=pa, ps=ps, pm=pm, pg=pg, ya=ya, ys=ys, ym=ym, states=states, ba=ba, bs=bs,
                          bm=bm, merged=merged, out_s=out_s))
        xl = x_next

    dx, loss_part = loss_grad(xl, target)
    loss = lax.psum(loss_part[0, 0], ("x", "y", "c"))

    dbias = jnp.zeros((ATT_HEADS, BLK, BLK), F32)
    win_grads, rest_grads = [None, None], [None, None]
    small = {n: [None, None] for n in ("norm_pre", "norm_post", "att_sinks", "sg_ln_g", "sg_ln_b", "sg_w", "sg_b",
                                       "ssm_conv_b", "ssm_dt_bias", "ssm_a_log", "ssm_d", "ssm_norm_g",
                                       "conv_w_full")}
    for l in (1, 0):
        lw, sv = layers[l], saved[l]
        dout, dba, dbs, dbm, dpg, dya, dys, dym, dg_post = merge_bwd(
            dx, sv["out_s"], sv["pg"], sv["ba"], sv["bs"], sv["bm"], lw["att"], lw["sg"], lw["ssm"], lw["out"],
            lw["g_post"])
        dw_out = mm_tn(sv["merged"], dout, 1024, "dw_out")
        dw_att = mm_tn(sv["ya"], dba, 1024, "dw_br_att")
        dw_sg = mm_tn(sv["ys"], dbs, 1024, "dw_br_sg")
        dw_ssm = mm_tn(sv["ym"], dbm, 1024, "dw_br_ssm")
        dpa, dbias, dsinks = attn_bwd(sv["pa"], dya, bias, lw["sinks"], dbias)
        dps, dsgw, dsgb_t, dln_g, dln_b = sgu_bwd(sv["ps"], dys, lw["ln_g"], lw["ln_b"], lw["sgw"], lw["sgw_t"],
                                                  lw["sgb_t"])
        dpm, dcw, dcb, dvec, dng = ssd_bwd(sv["pm"], dym, sv["states"], lw["cw"], lw["cb"], lw["dtb"], lw["alog"],
                                           lw["dsk"], lw["ng"])
        dslabs = dict(att=mm_tn(sv["h"], dpa, 2304, "dw_in_att"), sg=mm_tn(sv["h"], dps, 3072, "dw_in_sg"),
                      ssm=mm_tn(sv["h"], dpm, 2688, "dw_in_ssm"), gate=mm_tn(sv["h"], dpg, 3072, "dw_in_gate"))
        dx, dg_pre = dh_norm_bwd([dpa, dps, dpm, dpg], [lw["in_att"], lw["in_sg"], lw["in_ssm"], lw["in_gate"]],
                                 sv["x"], lw["g_pre"], dx)
        win_grads[l] = _shards_from_slabs(dslabs)
        rest_grads[l] = jnp.concatenate(
            [dw_att.reshape(N_DEV, 128, D_MODEL), dw_sg.reshape(N_DEV, 128, D_MODEL),
             dw_ssm.reshape(N_DEV, 256, D_MODEL), dw_out.reshape(N_DEV, 128, D_MODEL)], axis=1).astype(WIRE_DTYPE)
        small["norm_pre"][l] = dg_pre[0]
        small["norm_post"][l] = dg_post[0]
        small["att_sinks"][l] = dsinks[0, :ATT_HEADS]
        small["sg_ln_g"][l] = dln_g[0]
        small["sg_ln_b"][l] = dln_b[0]
        small["sg_w"][l] = dsgw
        small["sg_b"][l] = dsgb_t[:, :SG_GROUPS].T
        small["ssm_conv_b"][l] = dcb[0]
        small["ssm_dt_bias"][l] = dvec[0, :SSM_HEADS]
        small["ssm_a_log"][l] = dvec[1, :SSM_HEADS]
        small["ssm_d"][l] = dvec[2, :SSM_HEADS]
        small["ssm_norm_g"][l] = dng[0]
        small["conv_w_full"][l] = dcw[0:4]
    grad_x = dx
    d_rel_bias = bias_table_bwd(dbias.reshape(ATT_HEADS, -1), onehot_t).T

    small_d = {n: jnp.stack(v) for n, v in small.items()}
    small_d["rel_bias"] = d_rel_bias
    recv_win, recv_rest, recv_small = exchange(
        [jnp.concatenate(win_grads, axis=1), jnp.concatenate(rest_grads, axis=1)], [_pack_small(small_d)],
        "reduce_scatter")

    flat_win = lambda a: a.reshape(2 * D_MODEL, WIN_SHARD)
    res_win = adamw(recv_win, flat_win(w_in), flat_win(m_w_in), flat_win(v_w_in), WIN_TILE, "adamw_w_in")
    res_rest = adamw(recv_rest, _pack_rest(*[w[n] for n in REST]), _pack_rest(*[mom[n] for n in REST]),
                     _pack_rest(*[var[n] for n in REST]), REST_TILE, "adamw_rest")
    small_names = [n for n, _ in SMALL_SIZES if n != "conv_w_full"]
    g_s, dw_s, nm_s, nv_s = adamw(recv_small, _pack_small({n: w[n] for n in small_names}),
                                  _pack_small({n: mom[n] for n in small_names}),
                                  _pack_small({n: var[n] for n in small_names}), SMALL_TILE, "adamw_small")
    shapes = {n: w[n].shape for n in small_names}
    shapes["conv_w_full"] = (2, 4, CONV_DIM)
    g_conv_full = _unpack_small(g_s, shapes)["conv_w_full"]
    g_conv = lax.dynamic_slice_in_dim(g_conv_full, my_dev * 384, 384, axis=2)
    pack_conv = lambda a: _pad_rows(a.reshape(-1, D_MODEL), 8)
    g_c, dw_c, nm_c, nv_c = adamw(pack_conv(g_conv)[None], pack_conv(ssm_conv_w), pack_conv(m_ssm_conv_w),
                                  pack_conv(v_ssm_conv_w), 8, "adamw_conv")

    results = {}
    for q, (tag, psm, pc) in enumerate((("grad", g_s, g_c), ("delta", dw_s, dw_c), ("new_m", nm_s, nm_c),
                                        ("new_v", nv_s, nv_c))):
        r = dict(zip(REST, _unpack_rest(res_rest[q])))
        r["w_in"] = res_win[q].reshape(2, D_MODEL, WIN_SHARD)
        r.update(_unpack_small(psm, {n: w[n].shape for n in small_names}))
        r["ssm_conv_w"] = pc[0:3].reshape(2, 4, 384)
        results[tag] = r
    outs = [loss, grad_x[None]]
    for tag in ("grad", "delta", "new_m", "new_v"):
        outs += [results[tag][n] for n in WEIGHTS]
    return tuple(outs)
```

```python
import math

import jax
import jax.numpy as jnp
from jax import lax
from jax.experimental import pallas as pl
from jax.experimental.pallas import tpu as pltpu

F32 = jnp.float32
MXU_DTYPE = jnp.bfloat16
ACT_DTYPE = jnp.bfloat16
WIRE_DTYPE = jnp.bfloat16
HI = lax.Precision.HIGHEST
MESH = pl.DeviceIdType.MESH

D_MODEL = 1024
N_DEV = 8
ATT_HEADS = 16
HEAD_DIM = 64
BLK = 128
SG_GROUPS = 8
SSM_WIDTH = 2048
SSM_HEADS = 32
SSM_GROUPS = 4
SSM_GW = SSM_WIDTH // SSM_GROUPS
CONV_DIM = 3072
REL_BUCKETS = 32
EPS = 1e-6
NEG = -1e30

ATT_COLS = 2304
SG_COLS = 3072
SSM_COLS = 5376
GATE_COLS = 3072
DT_OFF = 5120

VMEM_LIMIT_V7X = 56 * 2 ** 20

ADAM_LR, ADAM_B1, ADAM_B2, ADAM_EPS, ADAM_WD, ADAM_STEP = 0.001, 0.9, 0.999, 1e-08, 0.01, 10

WIN_SHARD = 1700
WIN_LANES = 1792
REST_PARTS = (128, 128, 256, 128)
LAYER_REST = sum(REST_PARTS)
REST_TILE = 128
WIN_TILE = 128
SMALL_ROWS = 384
SMALL_TILE = 128


def _cparams(sem=None):
    return pltpu.CompilerParams(dimension_semantics=sem, vmem_limit_bytes=VMEM_LIMIT_V7X)


def _dot(a, b):
    return jnp.dot(a.astype(MXU_DTYPE), b.astype(MXU_DTYPE), preferred_element_type=F32)


def _dot_nt(a, b):
    return lax.dot_general(a.astype(MXU_DTYPE), b.astype(MXU_DTYPE), (((1,), (1,)), ((), ())),
                           preferred_element_type=F32)


def _dot_tn(a, b):
    return lax.dot_general(a.astype(MXU_DTYPE), b.astype(MXU_DTYPE), (((0,), (0,)), ((), ())),
                           preferred_element_type=F32)


def _dot_hi(a, b):
    return jnp.dot(a, b, precision=HI, preferred_element_type=F32)


def _dot_onehot(a, onehot):
    hi = a.astype(jnp.bfloat16)
    lo = (a - hi.astype(F32)).astype(jnp.bfloat16)
    return (jnp.dot(hi, onehot, preferred_element_type=F32) + jnp.dot(lo, onehot, preferred_element_type=F32))


def _dot_hi_nt(a, b):
    return lax.dot_general(a, b, (((1,), (1,)), ((), ())), precision=HI, preferred_element_type=F32)


def _sig(x):
    return 1.0 / (1.0 + jnp.exp(-x))


def _dsilu(x, s):
    return s * (1.0 + x * (1.0 - s))


def _full(shape):
    nd = len(shape)
    return pl.BlockSpec(shape, lambda *_: (0,) * nd)


def rmsnorm_fwd(x, g):
    s, d = x.shape
    tm = min(512, s)

    def body(x_ref, g_ref, o_ref):
        xv = x_ref[...]
        r = lax.rsqrt(jnp.mean(xv * xv, axis=-1, keepdims=True) + EPS)
        o_ref[...] = (xv * r * g_ref[...]).astype(o_ref.dtype)

    return pl.pallas_call(
        body, name="rmsnorm_fwd", grid=(s // tm,),
        in_specs=[pl.BlockSpec((tm, d), lambda i: (i, 0)), _full((1, d))],
        out_specs=pl.BlockSpec((tm, d), lambda i: (i, 0)),
        out_shape=jax.ShapeDtypeStruct((s, d), ACT_DTYPE),
        compiler_params=_cparams(("parallel",)),
    )(x, g)


def mm_nn(a, b, tn, name):
    s, k = a.shape
    n = b.shape[1]
    tm = min(2048, s)

    def body(a_ref, b_ref, o_ref):
        o_ref[...] = _dot(a_ref[...], b_ref[...]).astype(o_ref.dtype)

    return pl.pallas_call(
        body, name=name, grid=(s // tm, n // tn),
        in_specs=[pl.BlockSpec((tm, k), lambda i, j: (i, 0)), pl.BlockSpec((k, tn), lambda i, j: (0, j))],
        out_specs=pl.BlockSpec((tm, tn), lambda i, j: (i, j)),
        out_shape=jax.ShapeDtypeStruct((s, n), ACT_DTYPE),
        compiler_params=_cparams(("parallel", "arbitrary")),
    )(a, b)


def mm_tn(a, b, tn, name):
    s, k = a.shape
    n = b.shape[1]
    ts = min(512, s)

    def body(a_ref, b_ref, o_ref):
        @pl.when(pl.program_id(1) == 0)
        def _():
            o_ref[...] = jnp.zeros_like(o_ref)

        o_ref[...] += _dot_tn(a_ref[...], b_ref[...])

    return pl.pallas_call(
        body, name=name, grid=(n // tn, s // ts),
        in_specs=[pl.BlockSpec((ts, k), lambda j, t: (t, 0)), pl.BlockSpec((ts, tn), lambda j, t: (t, j))],
        out_specs=pl.BlockSpec((k, tn), lambda j, t: (0, j)),
        out_shape=jax.ShapeDtypeStruct((k, n), F32),
        compiler_params=_cparams(("parallel", "arbitrary")),
    )(a, b)


def dh_norm_bwd(dslabs, wslabs, x, g, dres):
    s, d = x.shape
    tm = min(1024, s)
    tk = 768
    counts = [ds.shape[1] // tk for ds in dslabs]
    starts = [sum(counts[:i]) for i in range(len(counts))]
    nk = sum(counts)
    ns = len(dslabs)

    def mm_body(*refs):
        d_refs, w_refs, dh_ref = refs[:ns], refs[ns:2 * ns], refs[2 * ns]
        k = pl.program_id(1)

        @pl.when(k == 0)
        def _():
            dh_ref[...] = jnp.zeros_like(dh_ref)

        for q in range(ns):
            @pl.when((k >= starts[q]) & (k < starts[q] + counts[q]))
            def _(q=q):
                dh_ref[...] += _dot_nt(d_refs[q][...], w_refs[q][...])

    def clamp(q):
        return lambda i, k: (i, jnp.clip(k - starts[q], 0, counts[q] - 1))

    def clamp_w(q):
        return lambda i, k: (0, jnp.clip(k - starts[q], 0, counts[q] - 1))

    dh = pl.pallas_call(
        mm_body, name="dh_matmul", grid=(s // tm, nk),
        in_specs=([pl.BlockSpec((tm, tk), clamp(q)) for q in range(ns)]
                  + [pl.BlockSpec((d, tk), clamp_w(q)) for q in range(ns)]),
        out_specs=pl.BlockSpec((tm, d), lambda i, k: (i, 0)),
        out_shape=jax.ShapeDtypeStruct((s, d), F32),
        compiler_params=_cparams(("parallel", "arbitrary")),
    )(*dslabs, *wslabs)

    te = min(512, s)

    def norm_body(dh_ref, x_ref, g_ref, dres_ref, dx_ref, dg_ref):
        @pl.when(pl.program_id(0) == 0)
        def _():
            dg_ref[...] = jnp.zeros_like(dg_ref)

        xv = x_ref[...]
        r = lax.rsqrt(jnp.mean(xv * xv, axis=-1, keepdims=True) + EPS)
        xn = xv * r
        dhv = dh_ref[...]
        dg_ref[...] += jnp.sum(dhv * xn, axis=0, keepdims=True)
        dxn = dhv * g_ref[...]
        dx_ref[...] = dres_ref[...] + r * (dxn - xn * jnp.mean(dxn * xn, axis=-1, keepdims=True))

    rows = pl.BlockSpec((te, d), lambda i: (i, 0))
    return pl.pallas_call(
        norm_body, name="norm_bwd", grid=(s // te,),
        in_specs=[rows, rows, _full((1, d)), rows],
        out_specs=[rows, _full((1, d))],
        out_shape=[jax.ShapeDtypeStruct((s, d), F32), jax.ShapeDtypeStruct((1, d), F32)],
        compiler_params=_cparams(("arbitrary",)),
    )(dh, x, g, dres)


def bias_table(rel_bias_t, onehot_t, maskadd):
    n = onehot_t.shape[1]
    tn = 8192

    def body(r_ref, o_ref, m_ref, out_ref):
        out_ref[...] = _dot_hi(r_ref[...], o_ref[...]) + m_ref[...]

    return pl.pallas_call(
        body, name="bias_table", grid=(n // tn,),
        in_specs=[_full((ATT_HEADS, REL_BUCKETS)), pl.BlockSpec((REL_BUCKETS, tn), lambda j: (0, j)),
                  pl.BlockSpec((1, tn), lambda j: (0, j))],
        out_specs=pl.BlockSpec((ATT_HEADS, tn), lambda j: (0, j)),
        out_shape=jax.ShapeDtypeStruct((ATT_HEADS, n), F32),
        compiler_params=_cparams(("parallel",)),
    )(rel_bias_t, onehot_t, maskadd)


def bias_table_bwd(dbias, onehot_t):
    n = onehot_t.shape[1]
    tn = 8192

    def body(d_ref, o_ref, out_ref):
        @pl.when(pl.program_id(0) == 0)
        def _():
            out_ref[...] = jnp.zeros_like(out_ref)

        out_ref[...] += _dot_hi_nt(d_ref[...], o_ref[...])

    return pl.pallas_call(
        body, name="bias_table_bwd", grid=(n // tn,),
        in_specs=[pl.BlockSpec((ATT_HEADS, tn), lambda j: (0, j)), pl.BlockSpec((REL_BUCKETS, tn), lambda j: (0, j))],
        out_specs=_full((ATT_HEADS, REL_BUCKETS)),
        out_shape=jax.ShapeDtypeStruct((ATT_HEADS, REL_BUCKETS), F32),
        compiler_params=_cparams(("arbitrary",)),
    )(dbias, onehot_t)


def _fold(full, tri):
    return jnp.where(tri, full[:, BLK:2 * BLK], full[:, 0:BLK])


def _unfold(folded, tri):
    return jnp.concatenate([jnp.where(tri, 0.0, folded), jnp.where(tri, folded, 0.0)], axis=1)


def _att_head(q, kcat, vcat, bias_h, sink, tri, no_prev):
    l = _fold(_dot_nt(q, kcat), tri) * (HEAD_DIM ** -0.5) + bias_h
    l = jnp.where(no_prev, NEG, l)
    m = jnp.maximum(jnp.max(l, axis=1, keepdims=True), sink)
    p = jnp.exp(l - m)
    es = jnp.exp(sink - m)
    inv = 1.0 / (jnp.sum(p, axis=1, keepdims=True) + es)
    p = p * inv
    pcat = _unfold(p, tri)
    return p, pcat, es * inv, _dot(pcat, vcat)


def _kv_cat(kvp, kvc, g):
    lo = g * HEAD_DIM
    kcat = jnp.concatenate([kvp[:, lo:lo + HEAD_DIM], kvc[:, lo:lo + HEAD_DIM]], axis=0)
    vcat = jnp.concatenate([kvp[:, 128 + lo:128 + lo + HEAD_DIM], kvc[:, 128 + lo:128 + lo + HEAD_DIM]], axis=0)
    return kcat, vcat


def _split_hosted(refs, n_in, n_out, n_scratch, ex):
    na = ex.na if ex is not None else 0
    o = 0
    parts = []
    for cnt in (n_in, na, n_out, na, n_scratch, 3 if ex is not None else 0):
        parts.append(refs[o:o + cnt])
        o += cnt
    own_in, ex_in, own_out, ex_out, own_scr, ex_sems = parts
    return (own_in, own_out, own_scr), (ex_in, ex_out, ex_sems)


def attn_fwd(pa, bias, sinks, ex=None):
    s = pa.shape[0]
    nb = s // BLK

    def body(*refs):
        ((pa_ref, kvp_ref, bias_ref, sink_ref), (y_ref,), (o_scr,)), hosted = _split_hosted(refs, 4, 1, 1, ex)
        n = pl.program_id(0)
        if ex is not None:
            @pl.when(n == 0)
            def _():
                ex.start(*hosted)

            @pl.when(n == nb - 1)
            def _():
                ex.wait(*hosted)

        q = pa_ref[:, 0:1024].astype(F32)
        z = pa_ref[:, 1024:2048].astype(F32)
        kvc = pa_ref[:, 2048:2304].astype(F32)
        kvp = kvp_ref[...].astype(F32)
        tri = (lax.broadcasted_iota(jnp.int32, (BLK, BLK), 0) >= lax.broadcasted_iota(jnp.int32, (BLK, BLK), 1))
        no_prev = (n == 0) & jnp.logical_not(tri)
        for g in range(2):
            kcat, vcat = _kv_cat(kvp, kvc, g)
            for j in range(8):
                h = g * 8 + j
                _, _, _, o = _att_head(q[:, h * 64:(h + 1) * 64], kcat, vcat, bias_ref[h],
                                       sink_ref[0:1, h:h + 1], tri, no_prev)
                o_scr[:, h * 64:(h + 1) * 64] = o
        y_ref[...] = (o_scr[...] * z * _sig(z)).astype(y_ref.dtype)

    hosted = ex is not None
    res = pl.pallas_call(
        body, name="attn_fwd_gather" if hosted else "attn_fwd", grid=(nb,),
        in_specs=[pl.BlockSpec((BLK, ATT_COLS), lambda n: (n, 0)),
                  pl.BlockSpec((BLK, 256), lambda n: (jnp.maximum(n - 1, 0), 8)),
                  _full((ATT_HEADS, BLK, BLK)), _full((1, 128))] + (ex.in_specs if hosted else []),
        out_specs=[pl.BlockSpec((BLK, 1024), lambda n: (n, 0))] + (ex.out_specs if hosted else []),
        out_shape=[jax.ShapeDtypeStruct((s, 1024), ACT_DTYPE)] + (ex.out_shape if hosted else []),
        scratch_shapes=[pltpu.VMEM((BLK, 1024), F32)] + (ex.scratch if hosted else []),
        compiler_params=_cparams(("arbitrary",)),
    )(pa, pa, bias, sinks, *(ex.arrays if hosted else []))
    return (res[0], res[1:]) if hosted else res[0]


def attn_bwd(pa, dy, bias, sinks, dbias_in, ex=None):
    s = pa.shape[0]
    nb = s // BLK

    def body(*refs):
        ((pa_ref, kvp_ref, dy_ref, bias_ref, sink_ref, dbin_ref), (dpa_ref, dbias_ref, dsink_ref),
         (carry, cur, prv, dq_scr, dz_scr)), hosted = _split_hosted(refs, 6, 3, 5, ex)
        i = pl.program_id(0)
        n = nb - 1 - i

        @pl.when(i == 0)
        def _():
            dbias_ref[...] = dbin_ref[...]
            dsink_ref[...] = jnp.zeros_like(dsink_ref)
            carry[...] = jnp.zeros_like(carry)

        if ex is not None:
            @pl.when(i == 0)
            def _():
                ex.start(*hosted)

            @pl.when(i == nb - 1)
            def _():
                ex.wait(*hosted)

        q = pa_ref[:, 0:1024].astype(F32)
        z = pa_ref[:, 1024:2048].astype(F32)
        kvc = pa_ref[:, 2048:2304].astype(F32)
        kvp = kvp_ref[...].astype(F32)
        dy = dy_ref[...].astype(F32)
        sz = _sig(z)
        d_o = dy * z * sz
        dzf = dy * _dsilu(z, sz)
        tri = (lax.broadcasted_iota(jnp.int32, (BLK, BLK), 0) >= lax.broadcasted_iota(jnp.int32, (BLK, BLK), 1))
        no_prev = (n == 0) & jnp.logical_not(tri)
        lane = lax.broadcasted_iota(jnp.int32, (1, 128), 1)
        dsink = jnp.zeros((1, 128), F32)
        scale = HEAD_DIM ** -0.5
        for g in range(2):
            kcat, vcat = _kv_cat(kvp, kvc, g)
            dk = jnp.zeros((2 * BLK, HEAD_DIM), F32)
            dv = jnp.zeros((2 * BLK, HEAD_DIM), F32)
            for j in range(8):
                h = g * 8 + j
                sl = slice(h * 64, (h + 1) * 64)
                qh = q[:, sl]
                p, pcat, psink, o = _att_head(qh, kcat, vcat, bias_ref[h], sink_ref[0:1, h:h + 1], tri, no_prev)
                doh = d_o[:, sl]
                dz_scr[:, sl] = dzf[:, sl] * o
                delta = jnp.sum(doh * o, axis=1, keepdims=True)
                dl = p * (_fold(_dot_nt(doh, vcat), tri) - delta)
                dsink = dsink + jnp.where(lane == h, -jnp.sum(psink * delta, axis=0, keepdims=True), 0.0)
                dbias_ref[h] += dl
                dlcat = _unfold(dl, tri)
                dq_scr[:, sl] = _dot(dlcat, kcat) * scale
                dk = dk + _dot_tn(dlcat, qh) * scale
                dv = dv + _dot_tn(pcat, doh)
            lo = g * HEAD_DIM
            prv[:, lo:lo + 64] = dk[0:BLK]
            cur[:, lo:lo + 64] = dk[BLK:2 * BLK]
            prv[:, 128 + lo:128 + lo + 64] = dv[0:BLK]
            cur[:, 128 + lo:128 + lo + 64] = dv[BLK:2 * BLK]
        dsink_ref[...] += dsink
        dpa_ref[:, 0:1024] = dq_scr[...].astype(dpa_ref.dtype)
        dpa_ref[:, 1024:2048] = dz_scr[...].astype(dpa_ref.dtype)
        dpa_ref[:, 2048:2304] = (cur[...] + carry[...]).astype(dpa_ref.dtype)
        carry[...] = prv[...]

    hosted = ex is not None
    res = pl.pallas_call(
        body, name="attn_bwd_scatter" if hosted else "attn_bwd", grid=(nb,),
        in_specs=[pl.BlockSpec((BLK, ATT_COLS), lambda i: (nb - 1 - i, 0)),
                  pl.BlockSpec((BLK, 256), lambda i: (jnp.maximum(nb - 2 - i, 0), 8)),
                  pl.BlockSpec((BLK, 1024), lambda i: (nb - 1 - i, 0)),
                  _full((ATT_HEADS, BLK, BLK)), _full((1, 128)), _full((ATT_HEADS, BLK, BLK))]
        + (ex.in_specs if hosted else []),
        out_specs=[pl.BlockSpec((BLK, ATT_COLS), lambda i: (nb - 1 - i, 0)),
                   _full((ATT_HEADS, BLK, BLK)), _full((1, 128))] + (ex.out_specs if hosted else []),
        out_shape=[jax.ShapeDtypeStruct((s, ATT_COLS), ACT_DTYPE),
                   jax.ShapeDtypeStruct((ATT_HEADS, BLK, BLK), F32),
                   jax.ShapeDtypeStruct((1, 128), F32)] + (ex.out_shape if hosted else []),
        scratch_shapes=[pltpu.VMEM((BLK, 256), F32), pltpu.VMEM((BLK, 256), F32), pltpu.VMEM((BLK, 256), F32),
                        pltpu.VMEM((BLK, 1024), F32), pltpu.VMEM((BLK, 1024), F32)] + (ex.scratch if hosted else []),
        compiler_params=_cparams(("arbitrary",)),
    )(pa, pa, dy, bias, sinks, dbias_in, *(ex.arrays if hosted else []))
    return (res[0], res[1], res[2], res[3:]) if hosted else tuple(res)


def _layernorm(v, g, b):
    mu = jnp.mean(v, axis=-1, keepdims=True)
    vc = v - mu
    rstd = lax.rsqrt(jnp.mean(vc * vc, axis=-1, keepdims=True) + EPS)
    xhat = vc * rstd
    return xhat, rstd, xhat * g + b


def sgu_fwd(ps, ln_g, ln_b, w_tril, b_t):
    s = ps.shape[0]

    def body(ps_ref, g_ref, b_ref, w_ref, bt_ref, y_ref):
        u = ps_ref[:, 0:1024].astype(F32)
        v = ps_ref[:, 1024:2048].astype(F32)
        z = ps_ref[:, 2048:3072].astype(F32)
        _, _, vn = _layernorm(v, g_ref[...], b_ref[...])
        gate = u * z * _sig(z)
        for g in range(SG_GROUPS):
            sl = slice(g * 128, (g + 1) * 128)
            mixed = _dot(w_ref[g], vn[:, sl]) + bt_ref[:, g:g + 1]
            y_ref[:, sl] = (gate[:, sl] * mixed).astype(y_ref.dtype)

    return pl.pallas_call(
        body, name="sgu_fwd", grid=(s // BLK,),
        in_specs=[pl.BlockSpec((BLK, SG_COLS), lambda c: (c, 0)), _full((1, 1024)), _full((1, 1024)),
                  _full((SG_GROUPS, BLK, BLK)), _full((BLK, 128))],
        out_specs=pl.BlockSpec((BLK, 1024), lambda c: (c, 0)),
        out_shape=jax.ShapeDtypeStruct((s, 1024), ACT_DTYPE),
        compiler_params=_cparams(("parallel",)),
    )(ps, ln_g, ln_b, w_tril, b_t)


def sgu_bwd(ps, dy, ln_g, ln_b, w_tril, w_tril_t, b_t):
    s = ps.shape[0]

    def body(ps_ref, dy_ref, g_ref, b_ref, w_ref, wt_ref, bt_ref, dps_ref, dw_ref, dbt_ref, dg_ref, db_ref, dvn_scr):
        @pl.when(pl.program_id(0) == 0)
        def _():
            dw_ref[...] = jnp.zeros_like(dw_ref)
            dbt_ref[...] = jnp.zeros_like(dbt_ref)
            dg_ref[...] = jnp.zeros_like(dg_ref)
            db_ref[...] = jnp.zeros_like(db_ref)

        u = ps_ref[:, 0:1024].astype(F32)
        v = ps_ref[:, 1024:2048].astype(F32)
        z = ps_ref[:, 2048:3072].astype(F32)
        dy = dy_ref[...].astype(F32)
        xhat, rstd, vn = _layernorm(v, g_ref[...], b_ref[...])
        sz = _sig(z)
        silu = z * sz
        row = lax.broadcasted_iota(jnp.int32, (BLK, BLK), 0)
        colm = lax.broadcasted_iota(jnp.int32, (BLK, BLK), 1)
        tril = row >= colm
        dbt = jnp.zeros((BLK, 128), F32)
        for g in range(SG_GROUPS):
            sl = slice(g * 128, (g + 1) * 128)
            vng = vn[:, sl]
            mixed = _dot(w_ref[g], vng) + bt_ref[:, g:g + 1]
            dyg, ug = dy[:, sl], u[:, sl]
            dps_ref[:, sl] = (dyg * mixed * silu[:, sl]).astype(dps_ref.dtype)
            dps_ref[:, 2048 + g * 128:2048 + (g + 1) * 128] = (
                dyg * ug * mixed * _dsilu(z[:, sl], sz[:, sl])).astype(dps_ref.dtype)
            dm = dyg * ug * silu[:, sl]
            dw_ref[g] += jnp.where(tril, _dot_nt(dm, vng), 0.0)
            dbt = dbt + jnp.where(colm == g, jnp.sum(dm, axis=1, keepdims=True), 0.0)
            dvn_scr[:, sl] = _dot(wt_ref[g], dm)
        dbt_ref[...] += dbt
        dvn = dvn_scr[...]
        dg_ref[...] += jnp.sum(dvn * xhat, axis=0, keepdims=True)
        db_ref[...] += jnp.sum(dvn, axis=0, keepdims=True)
        dxh = dvn * g_ref[...]
        dv = rstd * (dxh - jnp.mean(dxh, axis=-1, keepdims=True)
                     - xhat * jnp.mean(dxh * xhat, axis=-1, keepdims=True))
        dps_ref[:, 1024:2048] = dv.astype(dps_ref.dtype)

    return pl.pallas_call(
        body, name="sgu_bwd", grid=(s // BLK,),
        in_specs=[pl.BlockSpec((BLK, SG_COLS), lambda c: (c, 0)), pl.BlockSpec((BLK, 1024), lambda c: (c, 0)),
                  _full((1, 1024)), _full((1, 1024)), _full((SG_GROUPS, BLK, BLK)), _full((SG_GROUPS, BLK, BLK)),
                  _full((BLK, 128))],
        out_specs=[pl.BlockSpec((BLK, SG_COLS), lambda c: (c, 0)), _full((SG_GROUPS, BLK, BLK)), _full((BLK, 128)),
                   _full((1, 1024)), _full((1, 1024))],
        out_shape=[jax.ShapeDtypeStruct((s, SG_COLS), ACT_DTYPE), jax.ShapeDtypeStruct((SG_GROUPS, BLK, BLK), F32),
                   jax.ShapeDtypeStruct((BLK, 128), F32), jax.ShapeDtypeStruct((1, 1024), F32),
                   jax.ShapeDtypeStruct((1, 1024), F32)],
        scratch_shapes=[pltpu.VMEM((BLK, 1024), F32)],
        compiler_params=_cparams(("arbitrary",)),
    )(ps, dy, ln_g, ln_b, w_tril, w_tril_t, b_t)


def _shift_down(cur, prev16, k):
    if k == 0:
        return cur
    r = pltpu.roll(cur, k, 0)
    rp = pltpu.roll(prev16, k, 0)
    row = lax.broadcasted_iota(jnp.int32, (8, cur.shape[1]), 0)
    return jnp.concatenate([jnp.where(row < k, rp[0:8], r[0:8]), r[8:]], axis=0)


def _shift_up(cur, next16, k):
    if k == 0:
        return cur
    n = cur.shape[0]
    r = pltpu.roll(cur, n - k, 0)
    rn = pltpu.roll(next16, 16 - k, 0)
    row = lax.broadcasted_iota(jnp.int32, (8, cur.shape[1]), 0)
    return jnp.concatenate([r[:n - 8], jnp.where(row >= 8 - k, rn[8:16], r[n - 8:])], axis=0)


def _bcast8(v):
    return jnp.broadcast_to(v, (16, v.shape[1]))


class _Ssd:
    def __init__(self, xbc, prev16, dtr, cw, cbias, dtb, alog, dsk, tri, e):
        pre = cbias + cw[3:4] * xbc
        self.shifted = [xbc]
        for k in (1, 2, 3):
            sh = _shift_down(xbc, prev16, k)
            self.shifted.append(sh)
            pre = pre + cw[3 - k:4 - k] * sh
        self.pre = pre
        self.sg = _sig(pre)
        act = pre * self.sg
        self.xs = act[:, 0:SSM_WIDTH]
        self.bm = act[:, SSM_WIDTH:SSM_WIDTH + 512]
        self.cm = act[:, SSM_WIDTH + 512:CONV_DIM]
        self.dtp = dtr + dtb
        self.dt = jnp.maximum(self.dtp, 0.0) + jnp.log(1.0 + jnp.exp(-jnp.abs(self.dtp)))
        self.a = -jnp.exp(alog)
        self.acs = _dot_hi(tri, self.dt * self.a)
        self.acs_t = self.acs.T
        tot = self.acs[BLK - 1:BLK]
        self.ecs = jnp.exp(self.acs)
        self.dte = jnp.exp(tot - self.acs)
        self.cd = jnp.exp(tot)
        self.dt_x = _dot_onehot(self.dt, e)
        self.ecs_x = _dot_onehot(self.ecs, e)
        self.dte_x = _dot_onehot(self.dte, e)
        self.cd_x = _dot_onehot(_bcast8(self.cd), e)[0:1]
        self.d_x = _dot_onehot(_bcast8(dsk), e)[0:1]
        self.xdt = self.xs * self.dt_x
        row = lax.broadcasted_iota(jnp.int32, (BLK, BLK), 0)
        col = lax.broadcasted_iota(jnp.int32, (BLK, BLK), 1)
        self.tril = row >= col

    def group(self, g):
        sl = slice(g * 128, (g + 1) * 128)
        bg, cg = self.bm[:, sl], self.cm[:, sl]
        return bg, cg, _dot_nt(cg, bg)

    def decay(self, h):
        seg = self.acs[:, h:h + 1] - self.acs_t[h:h + 1, :]
        return jnp.exp(jnp.where(self.tril, seg, NEG))

    def y_pre_gate(self, ht_of, yd_scr, yoff_scr):
        for g in range(SSM_GROUPS):
            bg, cg, cb = self.group(g)
            for j in range(8):
                h = g * 8 + j
                sl = slice(h * 64, (h + 1) * 64)
                yd_scr[:, sl] = _dot(cb * self.decay(h), self.xdt[:, sl])
            gs = slice(g * SSM_GW, (g + 1) * SSM_GW)
            yoff_scr[:, gs] = _dot(cg, ht_of(g)) * self.ecs_x[:, gs]
        return yd_scr[...] + yoff_scr[...] + self.d_x * self.xs


def _ssd_consts():
    hh = lax.broadcasted_iota(jnp.int32, (128, SSM_WIDTH), 0)
    ch = lax.broadcasted_iota(jnp.int32, (128, SSM_WIDTH), 1)
    e = (ch // 64 == hh).astype(jnp.bfloat16)
    row = lax.broadcasted_iota(jnp.int32, (BLK, BLK), 0)
    col = lax.broadcasted_iota(jnp.int32, (BLK, BLK), 1)
    tri = (row >= col).astype(F32)
    return tri, e


def _pad_lanes(v, n=128):
    return jnp.pad(v, ((0, 0), (0, n - v.shape[1])))


def ssd_fwd(pm, cw, cbias, dtb, alog, dsk, ng):
    s = pm.shape[0]
    nc = s // BLK
    tri, e = _ssd_consts()

    def body(pm_ref, prev_ref, cw_ref, cb_ref, dtb_ref, al_ref, d_ref, ng_ref, tri_ref, e_ref,
             y_ref, st_ref, ht_ref, yd_scr, yoff_scr):
        c = pl.program_id(0)

        @pl.when(c == 0)
        def _():
            ht_ref[...] = jnp.zeros_like(ht_ref)

        xbc = pm_ref[:, 0:CONV_DIM].astype(F32)
        prev16 = jnp.where(c == 0, 0.0, prev_ref[...].astype(F32))
        f = _Ssd(xbc, prev16, pm_ref[:, DT_OFF:DT_OFF + 128].astype(F32), cw_ref[...], cb_ref[...], dtb_ref[...],
                 al_ref[...], d_ref[...], tri_ref[...], e_ref[...])
        st_ref[0] = ht_ref[...]
        y = f.y_pre_gate(lambda g: ht_ref[g], yd_scr, yoff_scr)
        for g in range(SSM_GROUPS):
            bg, _, _ = f.group(g)
            gs = slice(g * SSM_GW, (g + 1) * SSM_GW)
            ht_ref[g] = ht_ref[g] * f.cd_x[:, gs] + _dot_tn(bg, f.xdt[:, gs] * f.dte_x[:, gs])
        z = pm_ref[:, CONV_DIM:CONV_DIM + SSM_WIDTH].astype(F32)
        ypre = y * z * _sig(z)
        for g in range(SSM_GROUPS):
            gs = slice(g * SSM_GW, (g + 1) * SSM_GW)
            yg = ypre[:, gs]
            rr = lax.rsqrt(jnp.mean(yg * yg, axis=-1, keepdims=True) + EPS)
            y_ref[:, gs] = (yg * rr * ng_ref[:, gs]).astype(y_ref.dtype)

    return pl.pallas_call(
        body, name="ssd_fwd", grid=(nc,),
        in_specs=[pl.BlockSpec((BLK, SSM_COLS), lambda c: (c, 0)),
                  pl.BlockSpec((16, CONV_DIM), lambda c: (jnp.maximum(8 * c - 1, 0), 0)),
                  _full((4, CONV_DIM)), _full((1, CONV_DIM)), _full((1, 128)), _full((1, 128)), _full((1, 128)),
                  _full((1, SSM_WIDTH)), _full((BLK, BLK)), _full((128, SSM_WIDTH))],
        out_specs=[pl.BlockSpec((BLK, SSM_WIDTH), lambda c: (c, 0)),
                   pl.BlockSpec((1, SSM_GROUPS, 128, SSM_GW), lambda c: (c, 0, 0, 0))],
        out_shape=[jax.ShapeDtypeStruct((s, SSM_WIDTH), ACT_DTYPE),
                   jax.ShapeDtypeStruct((nc, SSM_GROUPS, 128, SSM_GW), F32)],
        scratch_shapes=[pltpu.VMEM((SSM_GROUPS, 128, SSM_GW), F32), pltpu.VMEM((BLK, SSM_WIDTH), F32),
                        pltpu.VMEM((BLK, SSM_WIDTH), F32)],
        compiler_params=_cparams(("arbitrary",)),
    )(pm, pm, cw, cbias, dtb, alog, dsk, ng, tri, e)


def ssd_bwd(pm, dy, states, cw, cbias, dtb, alog, dsk, ng):
    s = pm.shape[0]
    nc = s // BLK
    tri, e = _ssd_consts()
    tri_t, e_t = tri.T, e.T

    def body(pm_ref, prev_ref, dy_ref, st_ref, cw_ref, cb_ref, dtb_ref, al_ref, d_ref, ng_ref,
             tri_ref, trit_ref, e_ref, et_ref,
             dpm_ref, dcw_ref, dcb_ref, dvec_ref, dng_ref,
             dht_ref, dcar_ref, yd_scr, yoff_scr, dx_scr, r2_scr, hs_scr, da_scr, dat_scr, dd_scr, dbc_scr):
        i = pl.program_id(0)
        n = nc - 1 - i

        @pl.when(i == 0)
        def _():
            dht_ref[...] = jnp.zeros_like(dht_ref)
            dcar_ref[...] = jnp.zeros_like(dcar_ref)
            dcw_ref[...] = jnp.zeros_like(dcw_ref)
            dcb_ref[...] = jnp.zeros_like(dcb_ref)
            dvec_ref[...] = jnp.zeros_like(dvec_ref)
            dng_ref[...] = jnp.zeros_like(dng_ref)
            dd_scr[...] = jnp.zeros_like(dd_scr)
            da_scr[...] = jnp.zeros_like(da_scr)
            dat_scr[...] = jnp.zeros_like(dat_scr)

        xbc = pm_ref[:, 0:CONV_DIM].astype(F32)
        prev16 = jnp.where(n == 0, 0.0, prev_ref[...].astype(F32))
        cw = cw_ref[...]
        f = _Ssd(xbc, prev16, pm_ref[:, DT_OFF:DT_OFF + 128].astype(F32), cw, cb_ref[...], dtb_ref[...],
                 al_ref[...], d_ref[...], tri_ref[...], e_ref[...])
        et = et_ref[...]
        y = f.y_pre_gate(lambda g: st_ref[0, g], yd_scr, yoff_scr)

        z = pm_ref[:, CONV_DIM:CONV_DIM + SSM_WIDTH].astype(F32)
        dyv = dy_ref[...].astype(F32)
        sz = _sig(z)
        silu = z * sz
        ypre = y * silu
        for g in range(SSM_GROUPS):
            gs = slice(g * SSM_GW, (g + 1) * SSM_GW)
            yg = ypre[:, gs]
            rr = lax.rsqrt(jnp.mean(yg * yg, axis=-1, keepdims=True) + EPS)
            nrm = yg * rr
            dng_ref[:, gs] += jnp.sum(dyv[:, gs] * nrm, axis=0, keepdims=True)
            dn = dyv[:, gs] * ng_ref[:, gs]
            dx_scr[:, gs] = rr * (dn - nrm * jnp.mean(dn * nrm, axis=-1, keepdims=True))
        dypre = dx_scr[...]
        d_y = dypre * silu
        dpm_ref[:, CONV_DIM:CONV_DIM + SSM_WIDTH] = (dypre * y * _dsilu(z, sz)).astype(dpm_ref.dtype)

        for g in range(SSM_GROUPS):
            bg, cg, cb = f.group(g)
            gs = slice(g * SSM_GW, (g + 1) * SSM_GW)
            htg = st_ref[0, g]
            dhn = dht_ref[g]
            dcb = jnp.zeros((BLK, BLK), F32)
            for j in range(8):
                h = g * 8 + j
                sl = slice(h * 64, (h + 1) * 64)
                dec = f.decay(h)
                dyh = d_y[:, sl]
                dmd = _dot_nt(dyh, f.xdt[:, sl]) * dec
                dcb = dcb + dmd
                gm = dmd * cb
                da_scr[:, h:h + 1] = jnp.sum(gm, axis=1, keepdims=True)
                dat_scr[h:h + 1, :] = jnp.sum(gm, axis=0, keepdims=True)
                dx_scr[:, sl] = _dot_tn(cb * dec, dyh)
            dz = f.ecs_x[:, gs] * d_y[:, gs]
            dbc_scr[:, 512 + g * 128:512 + (g + 1) * 128] = _dot(dcb, bg) + _dot_nt(dz, htg)
            dbc_scr[:, g * 128:(g + 1) * 128] = _dot_tn(dcb, cg) + _dot_nt(f.xdt[:, gs] * f.dte_x[:, gs], dhn)
            dws = _dot(bg, dhn)
            dx_scr[:, gs] += f.dte_x[:, gs] * dws
            r2_scr[:, gs] = dws * f.xdt[:, gs]
            hs_scr[:, gs] = _bcast8(jnp.sum(dhn * htg, axis=0, keepdims=True))
            dht_ref[g] = f.cd_x[:, gs] * dhn + _dot_tn(cg, dz)
        d_x = dx_scr[...]
        r1 = _dot_onehot(d_y * yoff_scr[...], et)
        r2 = _dot_onehot(r2_scr[...], et) * f.dte
        dcd = _dot_onehot(hs_scr[...], et)[0:1]
        d_tot = jnp.sum(r2, axis=0, keepdims=True) + dcd * f.cd
        row = lax.broadcasted_iota(jnp.int32, (BLK, 128), 0)
        d_a = da_scr[...] - dat_scr[...].T + r1 - r2 + jnp.where(row == BLK - 1, d_tot, 0.0)
        dadt = _dot_hi(trit_ref[...], d_a)
        ddt = dadt * f.a + _dot_onehot(d_x * f.xs, et)
        lane = lax.broadcasted_iota(jnp.int32, (BLK, 128), 1)
        dr = jnp.where(lane < SSM_HEADS, ddt * _sig(f.dtp), 0.0)
        dvec_ref[0:1, :] += jnp.sum(dr, axis=0, keepdims=True)
        dvec_ref[1:2, :] += jnp.sum(dadt * f.dt, axis=0, keepdims=True) * f.a
        dd_scr[...] += _bcast8(jnp.sum(d_y * f.xs, axis=0, keepdims=True))
        dpm_ref[:, DT_OFF:DT_OFF + 128] = dr.astype(dpm_ref.dtype)
        dpm_ref[:, DT_OFF + 128:SSM_COLS] = jnp.zeros((BLK, 128), dpm_ref.dtype)

        dxs = d_x * f.dt_x + f.d_x * d_y
        dact = jnp.concatenate([dxs, dbc_scr[...]], axis=1)
        dpre = dact * _dsilu(f.pre, f.sg)
        dcb_ref[...] += jnp.sum(dpre, axis=0, keepdims=True)
        dxraw = jnp.zeros((BLK, CONV_DIM), F32)
        nxt = dcar_ref[...]
        for k in range(4):
            dcw_ref[3 - k:4 - k, :] += jnp.sum(dpre * f.shifted[k], axis=0, keepdims=True)
            dxraw = dxraw + cw[3 - k:4 - k] * _shift_up(dpre, nxt, k)
        dcar_ref[...] = dpre[0:16]
        dpm_ref[:, 0:CONV_DIM] = dxraw.astype(dpm_ref.dtype)

        @pl.when(i == nc - 1)
        def _():
            dvec_ref[2:3, :] = _dot_onehot(dd_scr[...], et)[0:1]

    return pl.pallas_call(
        body, name="ssd_bwd", grid=(nc,),
        in_specs=[pl.BlockSpec((BLK, SSM_COLS), lambda i: (nc - 1 - i, 0)),
                  pl.BlockSpec((16, CONV_DIM), lambda i: (jnp.maximum(8 * (nc - 1 - i) - 1, 0), 0)),
                  pl.BlockSpec((BLK, SSM_WIDTH), lambda i: (nc - 1 - i, 0)),
                  pl.BlockSpec((1, SSM_GROUPS, 128, SSM_GW), lambda i: (nc - 1 - i, 0, 0, 0)),
                  _full((4, CONV_DIM)), _full((1, CONV_DIM)), _full((1, 128)), _full((1, 128)), _full((1, 128)),
                  _full((1, SSM_WIDTH)), _full((BLK, BLK)), _full((BLK, BLK)), _full((128, SSM_WIDTH)),
                  _full((SSM_WIDTH, 128))],
        out_specs=[pl.BlockSpec((BLK, SSM_COLS), lambda i: (nc - 1 - i, 0)),
                   _full((8, CONV_DIM)), _full((1, CONV_DIM)), _full((8, 128)), _full((1, SSM_WIDTH))],
        out_shape=[jax.ShapeDtypeStruct((s, SSM_COLS), ACT_DTYPE), jax.ShapeDtypeStruct((8, CONV_DIM), F32),
                   jax.ShapeDtypeStruct((1, CONV_DIM), F32), jax.ShapeDtypeStruct((8, 128), F32),
                   jax.ShapeDtypeStruct((1, SSM_WIDTH), F32)],
        scratch_shapes=[pltpu.VMEM((SSM_GROUPS, 128, SSM_GW), F32), pltpu.VMEM((16, CONV_DIM), F32),
                        pltpu.VMEM((BLK, SSM_WIDTH), F32), pltpu.VMEM((BLK, SSM_WIDTH), F32),
                        pltpu.VMEM((BLK, SSM_WIDTH), F32), pltpu.VMEM((BLK, SSM_WIDTH), F32),
                        pltpu.VMEM((16, SSM_WIDTH), F32), pltpu.VMEM((BLK, 128), F32), pltpu.VMEM((128, BLK), F32),
                        pltpu.VMEM((16, SSM_WIDTH), F32), pltpu.VMEM((BLK, 1024), F32)],
        compiler_params=_cparams(("arbitrary",)),
    )(pm, pm, dy, states, cw, cbias, dtb, alog, dsk, ng, tri, tri_t, e, e_t)


def merge_fwd(x, ya, ys, ym, pg, wa, ws, wm, wo, g_post):
    s, d = x.shape
    tm = min(256, s)

    def body(x_ref, ya_ref, ys_ref, ym_ref, pg_ref, wa_ref, ws_ref, wm_ref, wo_ref, g_ref,
             xo_ref, ba_ref, bs_ref, bm_ref, mg_ref, out_ref):
        ba = _dot(ya_ref[...], wa_ref[...])
        bs = _dot(ys_ref[...], ws_ref[...])
        bm = _dot(ym_ref[...], wm_ref[...])
        merged = (_sig(pg_ref[:, 0:d].astype(F32)) * ba + _sig(pg_ref[:, d:2 * d].astype(F32)) * bs
                  + _sig(pg_ref[:, 2 * d:3 * d].astype(F32)) * bm)
        out = _dot(merged, wo_ref[...])
        r = lax.rsqrt(jnp.mean(out * out, axis=-1, keepdims=True) + EPS)
        xo_ref[...] = x_ref[...] + out * r * g_ref[...]
        ba_ref[...] = ba.astype(ba_ref.dtype)
        bs_ref[...] = bs.astype(bs_ref.dtype)
        bm_ref[...] = bm.astype(bm_ref.dtype)
        mg_ref[...] = merged.astype(mg_ref.dtype)
        out_ref[...] = out.astype(out_ref.dtype)

    rows = lambda w: pl.BlockSpec((tm, w), lambda i: (i, 0))
    act = jax.ShapeDtypeStruct((s, d), ACT_DTYPE)
    return pl.pallas_call(
        body, name="merge_fwd", grid=(s // tm,),
        in_specs=[rows(d), rows(d), rows(d), rows(2 * d), rows(3 * d), _full((d, d)), _full((d, d)),
                  _full((2 * d, d)), _full((d, d)), _full((1, d))],
        out_specs=[rows(d)] * 6,
        out_shape=[jax.ShapeDtypeStruct((s, d), F32), act, act, act, act, act],
        compiler_params=_cparams(("parallel",)),
    )(x, ya, ys, ym, pg, wa, ws, wm, wo, g_post)


def merge_bwd(dx, out_s, pg, ba, bs, bm, wa, ws, wm, wo, g_post):
    s, d = dx.shape
    tm = min(256, s)

    def body(dx_ref, out_ref, pg_ref, ba_ref, bs_ref, bm_ref, wa_ref, ws_ref, wm_ref, wo_ref, g_ref,
             dout_ref, dba_ref, dbs_ref, dbm_ref, dpg_ref, dya_ref, dys_ref, dym_ref, dg_ref):
        @pl.when(pl.program_id(0) == 0)
        def _():
            dg_ref[...] = jnp.zeros_like(dg_ref)

        o = out_ref[...].astype(F32)
        dxv = dx_ref[...]
        r = lax.rsqrt(jnp.mean(o * o, axis=-1, keepdims=True) + EPS)
        nrm = o * r
        dg_ref[...] += jnp.sum(dxv * nrm, axis=0, keepdims=True)
        dn = dxv * g_ref[...]
        dout = r * (dn - nrm * jnp.mean(dn * nrm, axis=-1, keepdims=True))
        dout_ref[...] = dout.astype(dout_ref.dtype)
        dmerged = _dot_nt(dout, wo_ref[...])
        for q, (b_ref, db_ref, w_ref, dy_ref) in enumerate(((ba_ref, dba_ref, wa_ref, dya_ref),
                                                            (bs_ref, dbs_ref, ws_ref, dys_ref),
                                                            (bm_ref, dbm_ref, wm_ref, dym_ref))):
            gt = _sig(pg_ref[:, q * d:(q + 1) * d].astype(F32))
            db = dmerged * gt
            db_ref[...] = db.astype(db_ref.dtype)
            dpg_ref[:, q * d:(q + 1) * d] = (dmerged * b_ref[...].astype(F32) * gt * (1.0 - gt)).astype(dpg_ref.dtype)
            dy_ref[...] = _dot_nt(db, w_ref[...]).astype(dy_ref.dtype)

    rows = lambda w: pl.BlockSpec((tm, w), lambda i: (i, 0))
    act = lambda w: jax.ShapeDtypeStruct((s, w), ACT_DTYPE)
    return pl.pallas_call(
        body, name="merge_bwd", grid=(s // tm,),
        in_specs=[rows(d), rows(d), rows(3 * d), rows(d), rows(d), rows(d), _full((d, d)), _full((d, d)),
                  _full((2 * d, d)), _full((d, d)), _full((1, d))],
        out_specs=[rows(d), rows(d), rows(d), rows(d), rows(3 * d), rows(d), rows(d), rows(2 * d), _full((1, d))],
        out_shape=[act(d), act(d), act(d), act(d), act(3 * d), act(d), act(d), act(2 * d),
                   jax.ShapeDtypeStruct((1, d), F32)],
        compiler_params=_cparams(("arbitrary",)),
    )(dx, out_s, pg, ba, bs, bm, wa, ws, wm, wo, g_post)


def loss_grad(y, target):
    s, d = y.shape
    tm = min(512, s)

    def body(y_ref, t_ref, dy_ref, l_ref):
        @pl.when(pl.program_id(0) == 0)
        def _():
            l_ref[...] = jnp.zeros_like(l_ref)

        err = y_ref[...] - t_ref[...]
        dy_ref[...] = err * (1.0 / d)
        part = jnp.sum(jnp.sum(err * err, axis=-1, keepdims=True) * (1.0 / d), axis=0, keepdims=True)
        l_ref[...] += 0.5 * jnp.broadcast_to(part, l_ref.shape)

    return pl.pallas_call(
        body, name="loss_grad", grid=(s // tm,),
        in_specs=[pl.BlockSpec((tm, d), lambda i: (i, 0)), pl.BlockSpec((tm, d), lambda i: (i, 0))],
        out_specs=[pl.BlockSpec((tm, d), lambda i: (i, 0)), _full((8, 128))],
        out_shape=[jax.ShapeDtypeStruct((s, d), F32), jax.ShapeDtypeStruct((8, 128), F32)],
        compiler_params=_cparams(("arbitrary",)),
    )(y, target)


def _mesh_pos():
    x, y, c = lax.axis_index("x"), lax.axis_index("y"), lax.axis_index("c")
    return x, y, c, 4 * x + 2 * y + c


def _peer(x, y, c, k):
    px = 1 - x if k & 4 else x
    py = 1 - y if k & 2 else y
    pc = 1 - c if k & 1 else c
    return (px, py, pc), 4 * px + 2 * py + pc


class Exchange:
    def __init__(self, scattered, gathered):
        self.ns = len(scattered)
        self.arrays = list(scattered) + list(gathered)
        self.na = len(self.arrays)
        any_spec = pl.BlockSpec(memory_space=pl.ANY)
        self.in_specs = [any_spec] * self.na
        self.out_specs = [any_spec] * self.na
        self.out_shape = ([jax.ShapeDtypeStruct(a.shape, a.dtype) for a in scattered]
                          + [jax.ShapeDtypeStruct((N_DEV,) + a.shape, a.dtype) for a in gathered])
        self.scratch = [pltpu.SemaphoreType.DMA((self.na, N_DEV - 1)), pltpu.SemaphoreType.DMA((self.na, N_DEV - 1)),
                        pltpu.SemaphoreType.DMA((self.na,))]

    def _src(self, ins, q, slot):
        return ins[q].at[slot] if q < self.ns else ins[q]

    def _local(self, ins, outs, sems):
        me = _mesh_pos()[3]
        return [pltpu.make_async_copy(self._src(ins, q, me), outs[q].at[me], sems[2].at[q]) for q in range(self.na)]

    def _remote(self, ins, outs, sems, incoming):
        x, y, c, me = _mesh_pos()
        copies = []
        for k in range(1, N_DEV):
            peer, pidx = _peer(x, y, c, k)
            for q in range(self.na):
                copies.append(pltpu.make_async_remote_copy(
                    src_ref=self._src(ins, q, pidx), dst_ref=outs[q].at[pidx if incoming else me],
                    send_sem=sems[0].at[q, k - 1], recv_sem=sems[1].at[q, k - 1], device_id=peer,
                    device_id_type=MESH))
        return copies

    def start(self, ins, outs, sems):
        for cp in self._local(ins, outs, sems) + self._remote(ins, outs, sems, incoming=False):
            cp.start()

    def wait(self, ins, outs, sems):
        for cp in self._remote(ins, outs, sems, incoming=True):
            cp.wait_recv()
        for cp in self._remote(ins, outs, sems, incoming=False):
            cp.wait_send()
        for cp in self._local(ins, outs, sems):
            cp.wait()


def exchange(scattered, gathered, name):
    ex = Exchange(scattered, gathered)

    def body(*refs):
        ins, outs, sems = refs[:ex.na], refs[ex.na:2 * ex.na], refs[2 * ex.na:]
        ex.start(ins, outs, sems)
        ex.wait(ins, outs, sems)

    return pl.pallas_call(body, name=name, in_specs=ex.in_specs, out_specs=ex.out_specs, out_shape=ex.out_shape,
                          scratch_shapes=ex.scratch)(*ex.arrays)


def adamw(parts, w, m, v, tile, name):
    npart, r, dp = parts.shape
    d = w.shape[1]

    def body(p_ref, w_ref, m_ref, v_ref, g_ref, dw_ref, nm_ref, nv_ref):
        g = p_ref[0, :, 0:d].astype(F32)
        for q in range(1, npart):
            g = g + p_ref[q, :, 0:d].astype(F32)
        g_ref[...] = g
        nm = ADAM_B1 * m_ref[...] + (1.0 - ADAM_B1) * g
        nv = ADAM_B2 * v_ref[...] + (1.0 - ADAM_B2) * (g * g)
        nm_ref[...] = nm
        nv_ref[...] = nv
        m_hat = nm / (1.0 - ADAM_B1 ** ADAM_STEP)
        v_hat = nv / (1.0 - ADAM_B2 ** ADAM_STEP)
        dw_ref[...] = -ADAM_LR * (m_hat / (jnp.sqrt(v_hat) + ADAM_EPS) + ADAM_WD * w_ref[...])

    rows = pl.BlockSpec((tile, d), lambda i: (i, 0))
    out = jax.ShapeDtypeStruct((r, d), F32)
    return pl.pallas_call(
        body, name=name, grid=(r // tile,),
        in_specs=[pl.BlockSpec((npart, tile, dp), lambda i: (0, i, 0)), rows, rows, rows],
        out_specs=[rows] * 4, out_shape=[out] * 4,
        compiler_params=_cparams(("parallel",)),
    )(parts, w, m, v)


def _pad_rows(a, rows):
    return jnp.pad(a, ((0, rows - a.shape[0]), (0, 0)))


def _pack_rest(w_att, w_sg, w_ssm, w_out):
    parts = []
    for l in range(2):
        parts += [w_att[l], w_sg[l], w_ssm[l], w_out[l]]
    return jnp.concatenate(parts, axis=0)


def _unpack_rest(p):
    outs = [[], [], [], []]
    o = 0
    for l in range(2):
        for q, rws in enumerate(REST_PARTS):
            outs[q].append(p[o:o + rws])
            o += rws
    return [jnp.stack(t) for t in outs]


def _pack_win(w_in):
    return jnp.pad(w_in.reshape(2 * D_MODEL, WIN_SHARD), ((0, 0), (0, WIN_LANES - WIN_SHARD)))


W_IN_MAP = ((0, 1024, "att", 0), (1024, 1280, "att", 2048), (1280, 2304, "att", 1024), (2304, 5376, "sg", 0),
            (5376, 7424, "ssm", 3072), (7424, 10496, "ssm", 0), (10496, 10528, "ssm", 5120), (10528, 13600, "gate", 0))
SLAB_COLS = {"att": ATT_COLS, "sg": SG_COLS, "ssm": SSM_COLS, "gate": GATE_COLS}


def _slabs_from_shards(g):
    slabs = {}
    for name, width in SLAB_COLS.items():
        pieces, filled = [], 0
        for ga, gb, _, off in sorted((m for m in W_IN_MAP if m[2] == name), key=lambda m: m[3]):
            assert off == filled
            a = ga
            while a < gb:
                d = a // WIN_SHARD
                hi = min(gb, WIN_SHARD * (d + 1))
                pieces.append(g[d, :, a - WIN_SHARD * d:hi - WIN_SHARD * d])
                a = hi
            filled += gb - ga
        if filled < width:
            pieces.append(jnp.zeros((D_MODEL, width - filled), g.dtype))
        slabs[name] = jnp.concatenate(pieces, axis=1)
    return slabs


def _shards_from_slabs(dslabs):
    out = []
    for d in range(N_DEV):
        a, b = WIN_SHARD * d, WIN_SHARD * (d + 1)
        pieces = []
        for ga, gb, name, off in W_IN_MAP:
            lo, hi = max(a, ga), min(b, gb)
            if lo < hi:
                pieces.append(dslabs[name][:, off + lo - ga:off + hi - ga])
        pieces.append(jnp.zeros((D_MODEL, WIN_LANES - WIN_SHARD), pieces[0].dtype))
        out.append(jnp.concatenate(pieces, axis=1).astype(WIRE_DTYPE))
    return jnp.stack(out)


SMALL_SIZES = (("norm_pre", 2048), ("norm_post", 2048), ("rel_bias", 512), ("att_sinks", 32), ("sg_ln_g", 2048),
               ("sg_ln_b", 2048), ("sg_w", 262144), ("sg_b", 2048), ("ssm_conv_b", 6144), ("ssm_dt_bias", 64),
               ("ssm_a_log", 64), ("ssm_d", 64), ("ssm_norm_g", 4096), ("conv_w_full", 24576))


def _pack_small(d):
    parts = []
    for name, size in SMALL_SIZES:
        rows = 8 * (-(-size // (8 * D_MODEL)))
        flat = d[name].reshape(-1) if name in d else jnp.zeros((size,), F32)
        parts.append(jnp.pad(flat, (0, rows * D_MODEL - size)).reshape(rows, D_MODEL))
    return _pad_rows(jnp.concatenate(parts, axis=0), SMALL_ROWS)


def _unpack_small(p, shapes):
    out, o = {}, 0
    for name, size in SMALL_SIZES:
        rows = 8 * (-(-size // (8 * D_MODEL)))
        if name in shapes:
            out[name] = p[o:o + rows].reshape(-1)[:size].reshape(shapes[name])
        o += rows
    return out


def _bucket_onehot_t():
    qi = jnp.arange(BLK, dtype=jnp.int32)[:, None]
    kj = jnp.arange(BLK, dtype=jnp.int32)[None, :]
    dd = (qi - kj) & (BLK - 1)
    in_window = dd >= 0
    max_exact = REL_BUCKETS // 2
    dist_f = jnp.maximum(dd, 1).astype(F32)
    large = max_exact + (jnp.log(dist_f / max_exact) / math.log(128 / max_exact)
                         * (REL_BUCKETS - max_exact)).astype(jnp.int32)
    large = jnp.minimum(large, REL_BUCKETS - 1)
    bucket = jnp.where(dd < max_exact, dd, large).reshape(1, -1)
    onehot_t = (bucket == jnp.arange(REL_BUCKETS, dtype=jnp.int32)[:, None]).astype(F32)
    maskadd = jnp.where(in_window, 0.0, NEG).astype(F32).reshape(1, -1)
    return onehot_t, maskadd


WEIGHTS = ['w_in', 'norm_pre', 'norm_post', 'rel_bias', 'att_sinks', 'sg_ln_g', 'sg_ln_b', 'sg_w', 'sg_b',
           'ssm_conv_w', 'ssm_conv_b', 'ssm_dt_bias', 'ssm_a_log', 'ssm_d', 'ssm_norm_g',
           'w_br_att', 'w_br_sg', 'w_br_ssm', 'w_out']
REST = ('w_br_att', 'w_br_sg', 'w_br_ssm', 'w_out')


def kernel(x, w_in, norm_pre, norm_post, rel_bias, att_sinks, sg_ln_g, sg_ln_b, sg_w, sg_b, ssm_conv_w, ssm_conv_b, ssm_dt_bias, ssm_a_log, ssm_d, ssm_norm_g, w_br_att, w_br_sg, w_br_ssm, w_out, loss_target, m_w_in, m_norm_pre, m_norm_post, m_rel_bias, m_att_sinks, m_sg_ln_g, m_sg_ln_b, m_sg_w, m_sg_b, m_ssm_conv_w, m_ssm_conv_b, m_ssm_dt_bias, m_ssm_a_log, m_ssm_d, m_ssm_norm_g, m_w_br_att, m_w_br_sg, m_w_br_ssm, m_w_out, v_w_in, v_norm_pre, v_norm_post, v_rel_bias, v_att_sinks, v_sg_ln_g, v_sg_ln_b, v_sg_w, v_sg_b, v_ssm_conv_w, v_ssm_conv_b, v_ssm_dt_bias, v_ssm_a_log, v_ssm_d, v_ssm_norm_g, v_w_br_att, v_w_br_sg, v_w_br_ssm, v_w_out):
    w = dict(w_in=w_in, norm_pre=norm_pre, norm_post=norm_post, rel_bias=rel_bias, att_sinks=att_sinks,
             sg_ln_g=sg_ln_g, sg_ln_b=sg_ln_b, sg_w=sg_w, sg_b=sg_b, ssm_conv_w=ssm_conv_w, ssm_conv_b=ssm_conv_b,
             ssm_dt_bias=ssm_dt_bias, ssm_a_log=ssm_a_log, ssm_d=ssm_d, ssm_norm_g=ssm_norm_g,
             w_br_att=w_br_att, w_br_sg=w_br_sg, w_br_ssm=w_br_ssm, w_out=w_out)
    mom = dict(w_in=m_w_in, norm_pre=m_norm_pre, norm_post=m_norm_post, rel_bias=m_rel_bias, att_sinks=m_att_sinks,
               sg_ln_g=m_sg_ln_g, sg_ln_b=m_sg_ln_b, sg_w=m_sg_w, sg_b=m_sg_b, ssm_conv_w=m_ssm_conv_w,
               ssm_conv_b=m_ssm_conv_b, ssm_dt_bias=m_ssm_dt_bias, ssm_a_log=m_ssm_a_log, ssm_d=m_ssm_d,
               ssm_norm_g=m_ssm_norm_g, w_br_att=m_w_br_att, w_br_sg=m_w_br_sg, w_br_ssm=m_w_br_ssm, w_out=m_w_out)
    var = dict(w_in=v_w_in, norm_pre=v_norm_pre, norm_post=v_norm_post, rel_bias=v_rel_bias, att_sinks=v_att_sinks,
               sg_ln_g=v_sg_ln_g, sg_ln_b=v_sg_ln_b, sg_w=v_sg_w, sg_b=v_sg_b, ssm_conv_w=v_ssm_conv_w,
               ssm_conv_b=v_ssm_conv_b, ssm_dt_bias=v_ssm_dt_bias, ssm_a_log=v_ssm_a_log, ssm_d=v_ssm_d,
               ssm_norm_g=v_ssm_norm_g, w_br_att=v_w_br_att, w_br_sg=v_w_br_sg, w_br_ssm=v_w_br_ssm, w_out=v_w_out)
    xs0 = x[0]
    target = loss_target[0]
    my_dev = 4 * lax.axis_index("x") + 2 * lax.axis_index("y") + lax.axis_index("c")

    conv_shard = _pad_rows(ssm_conv_w.reshape(-1, D_MODEL), 8)
    win_shard = _pack_win(w_in).astype(WIRE_DTYPE)
    rest_shard = _pack_rest(*[w[n] for n in REST]).astype(WIRE_DTYPE)
    layer_shards = [[win_shard[l * D_MODEL:(l + 1) * D_MODEL], rest_shard[l * LAYER_REST:(l + 1) * LAYER_REST]]
                    for l in range(2)]
    g_win0, g_rest0, gathered_conv = exchange([], layer_shards[0] + [conv_shard], "all_gather")
    conv_full = gathered_conv[:, 0:3].reshape(N_DEV, 2, 4, 384).transpose(1, 2, 0, 3).reshape(2, 4, CONV_DIM)

    def layer_weights(l, g_win, g_rest):
        slabs = _slabs_from_shards(g_win)
        lw = {"in_" + name: slab.astype(MXU_DTYPE) for name, slab in slabs.items()}
        o = 0
        for name, rws in zip(("att", "sg", "ssm", "out"), REST_PARTS):
            lw[name] = g_rest[:, o:o + rws].reshape(N_DEV * rws, D_MODEL).astype(MXU_DTYPE)
            o += rws
        tril = jnp.tril(jnp.ones((BLK, BLK), bool))
        sgw = jnp.where(tril[None], sg_w[l], 0.0)
        lw.update(
            g_pre=norm_pre[l][None], g_post=norm_post[l][None], sinks=_pad_lanes(att_sinks[l][None]),
            ln_g=sg_ln_g[l][None], ln_b=sg_ln_b[l][None], sgw=sgw.astype(MXU_DTYPE),
            sgw_t=sgw.transpose(0, 2, 1).astype(MXU_DTYPE), sgb_t=_pad_lanes(sg_b[l].T),
            cw=conv_full[l], cb=ssm_conv_b[l][None], dtb=_pad_lanes(ssm_dt_bias[l][None]),
            alog=_pad_lanes(ssm_a_log[l][None]), dsk=_pad_lanes(ssm_d[l][None]), ng=ssm_norm_g[l][None])
        return lw

    onehot_t, maskadd = _bucket_onehot_t()
    bias = bias_table(rel_bias.T, onehot_t, maskadd).reshape(ATT_HEADS, BLK, BLK)

    saved = []
    xl = xs0
    layers = [layer_weights(0, g_win0, g_rest0)]
    for l in range(2):
        lw = layers[l]
        h = rmsnorm_fwd(xl, lw["g_pre"])
        pa = mm_nn(h, lw["in_att"], 1152, "proj_att")
        ps = mm_nn(h, lw["in_sg"], 1536, "proj_sg")
        pm = mm_nn(h, lw["in_ssm"], 1792, "proj_ssm")
        pg = mm_nn(h, lw["in_gate"], 1536, "proj_gate")
        if l == 0:
            ya, (g_win1, g_rest1) = attn_fwd(pa, bias, lw["sinks"], Exchange([], layer_shards[1]))
            layers.append(layer_weights(1, g_win1, g_rest1))
        else:
            ya = attn_fwd(pa, bias, lw["sinks"])
        ys = sgu_fwd(ps, lw["ln_g"], lw["ln_b"], lw["sgw"], lw["sgb_t"])
        ym, states = ssd_fwd(pm, lw["cw"], lw["cb"], lw["dtb"], lw["alog"], lw["dsk"], lw["ng"])
        x_next, ba, bs, bm, merged, out_s = merge_fwd(xl, ya, ys, ym, pg, lw["att"], lw["sg"], lw["ssm"], lw["out"],
                                                      lw["g_post"])
        saved.append(dict(x=xl, h=h, pa=pa, ps=ps, pm=pm, pg=pg, ya=ya, ys=ys, ym=ym, states=states, ba=ba, bs=bs,
                          bm=bm, merged=merged, out_s=out_s))
        xl = x_next

    dx, loss_part = loss_grad(xl, target)
    loss = lax.psum(loss_part[0, 0], ("x", "y", "c"))

    dbias = jnp.zeros((ATT_HEADS, BLK, BLK), F32)
    win_grads, rest_grads = [None, None], [None, None]
    small = {n: [None, None] for n in ("norm_pre", "norm_post", "att_sinks", "sg_ln_g", "sg_ln_b", "sg_w", "sg_b",
                                       "ssm_conv_b", "ssm_dt_bias", "ssm_a_log", "ssm_d", "ssm_norm_g",
                                       "conv_w_full")}
    for l in (1, 0):
        lw, sv = layers[l], saved[l]
        dout, dba, dbs, dbm, dpg, dya, dys, dym, dg_post = merge_bwd(
            dx, sv["out_s"], sv["pg"], sv["ba"], sv["bs"], sv["bm"], lw["att"], lw["sg"], lw["ssm"], lw["out"],
            lw["g_post"])
        dw_out = mm_tn(sv["merged"], dout, 1024, "dw_out")
        dw_att = mm_tn(sv["ya"], dba, 1024, "dw_br_att")
        dw_sg = mm_tn(sv["ys"], dbs, 1024, "dw_br_sg")
        dw_ssm = mm_tn(sv["ym"], dbm, 1024, "dw_br_ssm")
        if l == 0:
            dpa, dbias, dsinks, (recv_win1, recv_rest1) = attn_bwd(
                sv["pa"], dya, bias, lw["sinks"], dbias, Exchange([win_grads[1], rest_grads[1]], []))
        else:
            dpa, dbias, dsinks = attn_bwd(sv["pa"], dya, bias, lw["sinks"], dbias)
        dps, dsgw, dsgb_t, dln_g, dln_b = sgu_bwd(sv["ps"], dys, lw["ln_g"], lw["ln_b"], lw["sgw"], lw["sgw_t"],
                                                  lw["sgb_t"])
        dpm, dcw, dcb, dvec, dng = ssd_bwd(sv["pm"], dym, sv["states"], lw["cw"], lw["cb"], lw["dtb"], lw["alog"],
                                           lw["dsk"], lw["ng"])
        dslabs = dict(att=mm_tn(sv["h"], dpa, 2304, "dw_in_att"), sg=mm_tn(sv["h"], dps, 3072, "dw_in_sg"),
                      ssm=mm_tn(sv["h"], dpm, 2688, "dw_in_ssm"), gate=mm_tn(sv["h"], dpg, 3072, "dw_in_gate"))
        dx, dg_pre = dh_norm_bwd([dpa, dps, dpm, dpg], [lw["in_att"], lw["in_sg"], lw["in_ssm"], lw["in_gate"]],
                                 sv["x"], lw["g_pre"], dx)
        win_grads[l] = _shards_from_slabs(dslabs)
        rest_grads[l] = jnp.concatenate(
            [dw_att.reshape(N_DEV, 128, D_MODEL), dw_sg.reshape(N_DEV, 128, D_MODEL),
             dw_ssm.reshape(N_DEV, 256, D_MODEL), dw_out.reshape(N_DEV, 128, D_MODEL)], axis=1).astype(WIRE_DTYPE)
        small["norm_pre"][l] = dg_pre[0]
        small["norm_post"][l] = dg_post[0]
        small["att_sinks"][l] = dsinks[0, :ATT_HEADS]
        small["sg_ln_g"][l] = dln_g[0]
        small["sg_ln_b"][l] = dln_b[0]
        small["sg_w"][l] = dsgw
        small["sg_b"][l] = dsgb_t[:, :SG_GROUPS].T
        small["ssm_conv_b"][l] = dcb[0]
        small["ssm_dt_bias"][l] = dvec[0, :SSM_HEADS]
        small["ssm_a_log"][l] = dvec[1, :SSM_HEADS]
        small["ssm_d"][l] = dvec[2, :SSM_HEADS]
        small["ssm_norm_g"][l] = dng[0]
        small["conv_w_full"][l] = dcw[0:4]
    grad_x = dx
    d_rel_bias = bias_table_bwd(dbias.reshape(ATT_HEADS, -1), onehot_t).T

    small_d = {n: jnp.stack(v) for n, v in small.items()}
    small_d["rel_bias"] = d_rel_bias
    recv_win0, recv_rest0, recv_small = exchange([win_grads[0], rest_grads[0]], [_pack_small(small_d)],
                                                 "reduce_scatter")

    res_win, res_rest = [], []
    for l, (recv_win, recv_rest) in enumerate(((recv_win0, recv_rest0), (recv_win1, recv_rest1))):
        res_win.append(adamw(recv_win, w_in[l], m_w_in[l], v_w_in[l], WIN_TILE, "adamw_w_in"))
        layer_rest = lambda t: jnp.concatenate([t[n][l] for n in REST], axis=0)
        res_rest.append(adamw(recv_rest, layer_rest(w), layer_rest(mom), layer_rest(var), REST_TILE, "adamw_rest"))
    res_win = [jnp.stack([res_win[0][q], res_win[1][q]]) for q in range(4)]
    res_rest = [jnp.concatenate([res_rest[0][q], res_rest[1][q]], axis=0) for q in range(4)]
    small_names = [n for n, _ in SMALL_SIZES if n != "conv_w_full"]
    g_s, dw_s, nm_s, nv_s = adamw(recv_small, _pack_small({n: w[n] for n in small_names}),
                                  _pack_small({n: mom[n] for n in small_names}),
                                  _pack_small({n: var[n] for n in small_names}), SMALL_TILE, "adamw_small")
    shapes = {n: w[n].shape for n in small_names}
    shapes["conv_w_full"] = (2, 4, CONV_DIM)
    g_conv_full = _unpack_small(g_s, shapes)["conv_w_full"]
    g_conv = lax.dynamic_slice_in_dim(g_conv_full, my_dev * 384, 384, axis=2)
    pack_conv = lambda a: _pad_rows(a.reshape(-1, D_MODEL), 8)
    g_c, dw_c, nm_c, nv_c = adamw(pack_conv(g_conv)[None], pack_conv(ssm_conv_w), pack_conv(m_ssm_conv_w),
                                  pack_conv(v_ssm_conv_w), 8, "adamw_conv")

    results = {}
    for q, (tag, psm, pc) in enumerate((("grad", g_s, g_c), ("delta", dw_s, dw_c), ("new_m", nm_s, nm_c),
                                        ("new_v", nv_s, nv_c))):
        r = dict(zip(REST, _unpack_rest(res_rest[q])))
        r["w_in"] = res_win[q]
        r.update(_unpack_small(psm, {n: w[n].shape for n in small_names}))
        r["ssm_conv_w"] = pc[0:3].reshape(2, 4, 384)
        results[tag] = r
    outs = [loss, grad_x[None]]
    for tag in ("grad", "delta", "new_m", "new_v"):
        outs += [results[tag][n] for n in WEIGHTS]
    return tuple(outs)
```

```python
import math

import jax
import jax.numpy as jnp
from jax import lax
from jax.experimental import pallas as pl
from jax.experimental.pallas import tpu as pltpu

F32 = jnp.float32
MXU_DTYPE = jnp.bfloat16
ACT_DTYPE = jnp.bfloat16
WIRE_DTYPE = jnp.bfloat16
HI = lax.Precision.HIGHEST
MESH = pl.DeviceIdType.MESH

D_MODEL = 1024
N_DEV = 8
ATT_HEADS = 16
HEAD_DIM = 64
BLK = 128
SG_GROUPS = 8
SSM_WIDTH = 2048
SSM_HEADS = 32
SSM_GROUPS = 4
SSM_GW = SSM_WIDTH // SSM_GROUPS
CONV_DIM = 3072
REL_BUCKETS = 32
EPS = 1e-6
NEG = -1e30

ATT_COLS = 2304
SG_COLS = 3072
SSM_COLS = 5376
GATE_COLS = 3072
DT_OFF = 5120

VMEM_LIMIT_V7X = 56 * 2 ** 20

ADAM_LR, ADAM_B1, ADAM_B2, ADAM_EPS, ADAM_WD, ADAM_STEP = 0.001, 0.9, 0.999, 1e-08, 0.01, 10

WIN_SHARD = 1700
WIN_LANES = 1792
REST_PARTS = (128, 128, 256, 128)
LAYER_REST = sum(REST_PARTS)
REST_TILE = 128
WIN_TILE = 128
SMALL_ROWS = 384
SMALL_TILE = 128


def _cparams(sem=None):
    return pltpu.CompilerParams(dimension_semantics=sem, vmem_limit_bytes=VMEM_LIMIT_V7X)


def _dot(a, b):
    return jnp.dot(a.astype(MXU_DTYPE), b.astype(MXU_DTYPE), preferred_element_type=F32)


def _dot_nt(a, b):
    return lax.dot_general(a.astype(MXU_DTYPE), b.astype(MXU_DTYPE), (((1,), (1,)), ((), ())),
                           preferred_element_type=F32)


def _dot_tn(a, b):
    return lax.dot_general(a.astype(MXU_DTYPE), b.astype(MXU_DTYPE), (((0,), (0,)), ((), ())),
                           preferred_element_type=F32)


def _dot_hi(a, b):
    return jnp.dot(a, b, precision=HI, preferred_element_type=F32)


def _dot_onehot(a, onehot):
    hi = a.astype(jnp.bfloat16)
    lo = (a - hi.astype(F32)).astype(jnp.bfloat16)
    return (jnp.dot(hi, onehot, preferred_element_type=F32) + jnp.dot(lo, onehot, preferred_element_type=F32))


def _dot_hi_nt(a, b):
    return lax.dot_general(a, b, (((1,), (1,)), ((), ())), precision=HI, preferred_element_type=F32)


def _sig(x):
    return 1.0 / (1.0 + jnp.exp(-x))


def _dsilu(x, s):
    return s * (1.0 + x * (1.0 - s))


def _full(shape):
    nd = len(shape)
    return pl.BlockSpec(shape, lambda *_: (0,) * nd)


def rmsnorm_fwd(x, g):
    s, d = x.shape
    tm = min(512, s)

    def body(x_ref, g_ref, o_ref):
        xv = x_ref[...]
        r = lax.rsqrt(jnp.mean(xv * xv, axis=-1, keepdims=True) + EPS)
        o_ref[...] = (xv * r * g_ref[...]).astype(o_ref.dtype)

    return pl.pallas_call(
        body, name="rmsnorm_fwd", grid=(s // tm,),
        in_specs=[pl.BlockSpec((tm, d), lambda i: (i, 0)), _full((1, d))],
        out_specs=pl.BlockSpec((tm, d), lambda i: (i, 0)),
        out_shape=jax.ShapeDtypeStruct((s, d), ACT_DTYPE),
        compiler_params=_cparams(("parallel",)),
    )(x, g)


def mm_nn(a, b, tn, name):
    s, k = a.shape
    n = b.shape[1]
    tm = min(2048, s)

    def body(a_ref, b_ref, o_ref):
        o_ref[...] = _dot(a_ref[...], b_ref[...]).astype(o_ref.dtype)

    return pl.pallas_call(
        body, name=name, grid=(s // tm, n // tn),
        in_specs=[pl.BlockSpec((tm, k), lambda i, j: (i, 0)), pl.BlockSpec((k, tn), lambda i, j: (0, j))],
        out_specs=pl.BlockSpec((tm, tn), lambda i, j: (i, j)),
        out_shape=jax.ShapeDtypeStruct((s, n), ACT_DTYPE),
        compiler_params=_cparams(("parallel", "arbitrary")),
    )(a, b)


def mm_tn(a, b, tn, name):
    s, k = a.shape
    n = b.shape[1]
    ts = min(512, s)

    def body(a_ref, b_ref, o_ref):
        @pl.when(pl.program_id(1) == 0)
        def _():
            o_ref[...] = jnp.zeros_like(o_ref)

        o_ref[...] += _dot_tn(a_ref[...], b_ref[...])

    return pl.pallas_call(
        body, name=name, grid=(n // tn, s // ts),
        in_specs=[pl.BlockSpec((ts, k), lambda j, t: (t, 0)), pl.BlockSpec((ts, tn), lambda j, t: (t, j))],
        out_specs=pl.BlockSpec((k, tn), lambda j, t: (0, j)),
        out_shape=jax.ShapeDtypeStruct((k, n), F32),
        compiler_params=_cparams(("parallel", "arbitrary")),
    )(a, b)


def dh_norm_bwd(dslabs, wslabs, x, g, dres, ex=None):
    s, d = x.shape
    tm = min(1024, s)
    tk = 768
    counts = [ds.shape[1] // tk for ds in dslabs]
    starts = [sum(counts[:i]) for i in range(len(counts))]
    nk = sum(counts)
    ns = len(dslabs)

    hosted = ex is not None
    ni = s // tm

    def mm_body(*refs):
        (own_in, (dh_ref,), _), hosted_refs = _split_hosted(refs, 2 * ns, 1, 0, ex)
        d_refs, w_refs = own_in[:ns], own_in[ns:]
        i, k = pl.program_id(0), pl.program_id(1)
        if hosted:
            @pl.when((i == 0) & (k == 0))
            def _():
                ex.start(*hosted_refs)

            @pl.when((i == ni - 1) & (k == nk - 1))
            def _():
                ex.wait(*hosted_refs)

        @pl.when(k == 0)
        def _():
            dh_ref[...] = jnp.zeros_like(dh_ref)

        for q in range(ns):
            @pl.when((k >= starts[q]) & (k < starts[q] + counts[q]))
            def _(q=q):
                dh_ref[...] += _dot_nt(d_refs[q][...], w_refs[q][...])

    def clamp(q):
        return lambda i, k: (i, jnp.clip(k - starts[q], 0, counts[q] - 1))

    def clamp_w(q):
        return lambda i, k: (0, jnp.clip(k - starts[q], 0, counts[q] - 1))

    res = pl.pallas_call(
        mm_body, name="dh_matmul_scatter" if hosted else "dh_matmul", grid=(ni, nk),
        in_specs=([pl.BlockSpec((tm, tk), clamp(q)) for q in range(ns)]
                  + [pl.BlockSpec((d, tk), clamp_w(q)) for q in range(ns)] + (ex.in_specs if hosted else [])),
        out_specs=[pl.BlockSpec((tm, d), lambda i, k: (i, 0))] + (ex.out_specs if hosted else []),
        out_shape=[jax.ShapeDtypeStruct((s, d), F32)] + (ex.out_shape if hosted else []),
        scratch_shapes=ex.scratch if hosted else [],
        compiler_params=_cparams(("arbitrary" if hosted else "parallel", "arbitrary")),
    )(*dslabs, *wslabs, *(ex.arrays if hosted else []))
    dh, ex_results = res[0], res[1:]

    te = min(512, s)

    def norm_body(dh_ref, x_ref, g_ref, dres_ref, dx_ref, dg_ref):
        @pl.when(pl.program_id(0) == 0)
        def _():
            dg_ref[...] = jnp.zeros_like(dg_ref)

        xv = x_ref[...]
        r = lax.rsqrt(jnp.mean(xv * xv, axis=-1, keepdims=True) + EPS)
        xn = xv * r
        dhv = dh_ref[...]
        dg_ref[...] += jnp.sum(dhv * xn, axis=0, keepdims=True)
        dxn = dhv * g_ref[...]
        dx_ref[...] = dres_ref[...] + r * (dxn - xn * jnp.mean(dxn * xn, axis=-1, keepdims=True))

    rows = pl.BlockSpec((te, d), lambda i: (i, 0))
    dx, dg = pl.pallas_call(
        norm_body, name="norm_bwd", grid=(s // te,),
        in_specs=[rows, rows, _full((1, d)), rows],
        out_specs=[rows, _full((1, d))],
        out_shape=[jax.ShapeDtypeStruct((s, d), F32), jax.ShapeDtypeStruct((1, d), F32)],
        compiler_params=_cparams(("arbitrary",)),
    )(dh, x, g, dres)
    return (dx, dg, ex_results) if hosted else (dx, dg)


def bias_table(rel_bias_t, onehot_t, maskadd):
    n = onehot_t.shape[1]
    tn = 8192

    def body(r_ref, o_ref, m_ref, out_ref):
        out_ref[...] = _dot_hi(r_ref[...], o_ref[...]) + m_ref[...]

    return pl.pallas_call(
        body, name="bias_table", grid=(n // tn,),
        in_specs=[_full((ATT_HEADS, REL_BUCKETS)), pl.BlockSpec((REL_BUCKETS, tn), lambda j: (0, j)),
                  pl.BlockSpec((1, tn), lambda j: (0, j))],
        out_specs=pl.BlockSpec((ATT_HEADS, tn), lambda j: (0, j)),
        out_shape=jax.ShapeDtypeStruct((ATT_HEADS, n), F32),
        compiler_params=_cparams(("parallel",)),
    )(rel_bias_t, onehot_t, maskadd)


def bias_table_bwd(dbias, onehot_t):
    n = onehot_t.shape[1]
    tn = 8192

    def body(d_ref, o_ref, out_ref):
        @pl.when(pl.program_id(0) == 0)
        def _():
            out_ref[...] = jnp.zeros_like(out_ref)

        out_ref[...] += _dot_hi_nt(d_ref[...], o_ref[...])

    return pl.pallas_call(
        body, name="bias_table_bwd", grid=(n // tn,),
        in_specs=[pl.BlockSpec((ATT_HEADS, tn), lambda j: (0, j)), pl.BlockSpec((REL_BUCKETS, tn), lambda j: (0, j))],
        out_specs=_full((ATT_HEADS, REL_BUCKETS)),
        out_shape=jax.ShapeDtypeStruct((ATT_HEADS, REL_BUCKETS), F32),
        compiler_params=_cparams(("arbitrary",)),
    )(dbias, onehot_t)


def _fold(full, tri):
    return jnp.where(tri, full[:, BLK:2 * BLK], full[:, 0:BLK])


def _unfold(folded, tri):
    return jnp.concatenate([jnp.where(tri, 0.0, folded), jnp.where(tri, folded, 0.0)], axis=1)


def _att_head(q, kcat, vcat, bias_h, sink, tri, no_prev):
    l = _fold(_dot_nt(q, kcat), tri) * (HEAD_DIM ** -0.5) + bias_h
    l = jnp.where(no_prev, NEG, l)
    m = jnp.maximum(jnp.max(l, axis=1, keepdims=True), sink)
    p = jnp.exp(l - m)
    es = jnp.exp(sink - m)
    inv = 1.0 / (jnp.sum(p, axis=1, keepdims=True) + es)
    p = p * inv
    pcat = _unfold(p, tri)
    return p, pcat, es * inv, _dot(pcat, vcat)


def _kv_cat(kvp, kvc, g):
    lo = g * HEAD_DIM
    kcat = jnp.concatenate([kvp[:, lo:lo + HEAD_DIM], kvc[:, lo:lo + HEAD_DIM]], axis=0)
    vcat = jnp.concatenate([kvp[:, 128 + lo:128 + lo + HEAD_DIM], kvc[:, 128 + lo:128 + lo + HEAD_DIM]], axis=0)
    return kcat, vcat


def _split_hosted(refs, n_in, n_out, n_scratch, ex):
    na = ex.na if ex is not None else 0
    o = 0
    parts = []
    for cnt in (n_in, na, n_out, na, n_scratch, 3 if ex is not None else 0):
        parts.append(refs[o:o + cnt])
        o += cnt
    own_in, ex_in, own_out, ex_out, own_scr, ex_sems = parts
    return (own_in, own_out, own_scr), (ex_in, ex_out, ex_sems)


def attn_fwd(pa, bias, sinks, ex=None):
    s = pa.shape[0]
    nb = s // BLK

    def body(*refs):
        ((pa_ref, kvp_ref, bias_ref, sink_ref), (y_ref,), (o_scr,)), hosted = _split_hosted(refs, 4, 1, 1, ex)
        n = pl.program_id(0)
        if ex is not None:
            @pl.when(n == 0)
            def _():
                ex.start(*hosted)

            @pl.when(n == nb - 1)
            def _():
                ex.wait(*hosted)

        q = pa_ref[:, 0:1024].astype(F32)
        z = pa_ref[:, 1024:2048].astype(F32)
        kvc = pa_ref[:, 2048:2304].astype(F32)
        kvp = kvp_ref[...].astype(F32)
        tri = (lax.broadcasted_iota(jnp.int32, (BLK, BLK), 0) >= lax.broadcasted_iota(jnp.int32, (BLK, BLK), 1))
        no_prev = (n == 0) & jnp.logical_not(tri)
        for g in range(2):
            kcat, vcat = _kv_cat(kvp, kvc, g)
            for j in range(8):
                h = g * 8 + j
                _, _, _, o = _att_head(q[:, h * 64:(h + 1) * 64], kcat, vcat, bias_ref[h],
                                       sink_ref[0:1, h:h + 1], tri, no_prev)
                o_scr[:, h * 64:(h + 1) * 64] = o
        y_ref[...] = (o_scr[...] * z * _sig(z)).astype(y_ref.dtype)

    hosted = ex is not None
    res = pl.pallas_call(
        body, name="attn_fwd_gather" if hosted else "attn_fwd", grid=(nb,),
        in_specs=[pl.BlockSpec((BLK, ATT_COLS), lambda n: (n, 0)),
                  pl.BlockSpec((BLK, 256), lambda n: (jnp.maximum(n - 1, 0), 8)),
                  _full((ATT_HEADS, BLK, BLK)), _full((1, 128))] + (ex.in_specs if hosted else []),
        out_specs=[pl.BlockSpec((BLK, 1024), lambda n: (n, 0))] + (ex.out_specs if hosted else []),
        out_shape=[jax.ShapeDtypeStruct((s, 1024), ACT_DTYPE)] + (ex.out_shape if hosted else []),
        scratch_shapes=[pltpu.VMEM((BLK, 1024), F32)] + (ex.scratch if hosted else []),
        compiler_params=_cparams(("arbitrary",)),
    )(pa, pa, bias, sinks, *(ex.arrays if hosted else []))
    return (res[0], res[1:]) if hosted else res[0]


def attn_bwd(pa, dy, bias, sinks, dbias_in, ex=None):
    s = pa.shape[0]
    nb = s // BLK

    def body(*refs):
        ((pa_ref, kvp_ref, dy_ref, bias_ref, sink_ref, dbin_ref), (dpa_ref, dbias_ref, dsink_ref),
         (carry, cur, prv, dq_scr, dz_scr)), hosted = _split_hosted(refs, 6, 3, 5, ex)
        i = pl.program_id(0)
        n = nb - 1 - i

        @pl.when(i == 0)
        def _():
            dbias_ref[...] = dbin_ref[...]
            dsink_ref[...] = jnp.zeros_like(dsink_ref)
            carry[...] = jnp.zeros_like(carry)

        if ex is not None:
            @pl.when(i == 0)
            def _():
                ex.start(*hosted)

            @pl.when(i == nb - 1)
            def _():
                ex.wait(*hosted)

        q = pa_ref[:, 0:1024].astype(F32)
        z = pa_ref[:, 1024:2048].astype(F32)
        kvc = pa_ref[:, 2048:2304].astype(F32)
        kvp = kvp_ref[...].astype(F32)
        dy = dy_ref[...].astype(F32)
        sz = _sig(z)
        d_o = dy * z * sz
        dzf = dy * _dsilu(z, sz)
        tri = (lax.broadcasted_iota(jnp.int32, (BLK, BLK), 0) >= lax.broadcasted_iota(jnp.int32, (BLK, BLK), 1))
        no_prev = (n == 0) & jnp.logical_not(tri)
        lane = lax.broadcasted_iota(jnp.int32, (1, 128), 1)
        dsink = jnp.zeros((1, 128), F32)
        scale = HEAD_DIM ** -0.5
        for g in range(2):
            kcat, vcat = _kv_cat(kvp, kvc, g)
            dk = jnp.zeros((2 * BLK, HEAD_DIM), F32)
            dv = jnp.zeros((2 * BLK, HEAD_DIM), F32)
            for j in range(8):
                h = g * 8 + j
                sl = slice(h * 64, (h + 1) * 64)
                qh = q[:, sl]
                p, pcat, psink, o = _att_head(qh, kcat, vcat, bias_ref[h], sink_ref[0:1, h:h + 1], tri, no_prev)
                doh = d_o[:, sl]
                dz_scr[:, sl] = dzf[:, sl] * o
                delta = jnp.sum(doh * o, axis=1, keepdims=True)
                dl = p * (_fold(_dot_nt(doh, vcat), tri) - delta)
                dsink = dsink + jnp.where(lane == h, -jnp.sum(psink * delta, axis=0, keepdims=True), 0.0)
                dbias_ref[h] += dl
                dlcat = _unfold(dl, tri)
                dq_scr[:, sl] = _dot(dlcat, kcat) * scale
                dk = dk + _dot_tn(dlcat, qh) * scale
                dv = dv + _dot_tn(pcat, doh)
            lo = g * HEAD_DIM
            prv[:, lo:lo + 64] = dk[0:BLK]
            cur[:, lo:lo + 64] = dk[BLK:2 * BLK]
            prv[:, 128 + lo:128 + lo + 64] = dv[0:BLK]
            cur[:, 128 + lo:128 + lo + 64] = dv[BLK:2 * BLK]
        dsink_ref[...] += dsink
        dpa_ref[:, 0:1024] = dq_scr[...].astype(dpa_ref.dtype)
        dpa_ref[:, 1024:2048] = dz_scr[...].astype(dpa_ref.dtype)
        dpa_ref[:, 2048:2304] = (cur[...] + carry[...]).astype(dpa_ref.dtype)
        carry[...] = prv[...]

    hosted = ex is not None
    res = pl.pallas_call(
        body, name="attn_bwd_scatter" if hosted else "attn_bwd", grid=(nb,),
        in_specs=[pl.BlockSpec((BLK, ATT_COLS), lambda i: (nb - 1 - i, 0)),
                  pl.BlockSpec((BLK, 256), lambda i: (jnp.maximum(nb - 2 - i, 0), 8)),
                  pl.BlockSpec((BLK, 1024), lambda i: (nb - 1 - i, 0)),
                  _full((ATT_HEADS, BLK, BLK)), _full((1, 128)), _full((ATT_HEADS, BLK, BLK))]
        + (ex.in_specs if hosted else []),
        out_specs=[pl.BlockSpec((BLK, ATT_COLS), lambda i: (nb - 1 - i, 0)),
                   _full((ATT_HEADS, BLK, BLK)), _full((1, 128))] + (ex.out_specs if hosted else []),
        out_shape=[jax.ShapeDtypeStruct((s, ATT_COLS), ACT_DTYPE),
                   jax.ShapeDtypeStruct((ATT_HEADS, BLK, BLK), F32),
                   jax.ShapeDtypeStruct((1, 128), F32)] + (ex.out_shape if hosted else []),
        scratch_shapes=[pltpu.VMEM((BLK, 256), F32), pltpu.VMEM((BLK, 256), F32), pltpu.VMEM((BLK, 256), F32),
                        pltpu.VMEM((BLK, 1024), F32), pltpu.VMEM((BLK, 1024), F32)] + (ex.scratch if hosted else []),
        compiler_params=_cparams(("arbitrary",)),
    )(pa, pa, dy, bias, sinks, dbias_in, *(ex.arrays if hosted else []))
    return (res[0], res[1], res[2], res[3:]) if hosted else tuple(res)


def _layernorm(v, g, b):
    mu = jnp.mean(v, axis=-1, keepdims=True)
    vc = v - mu
    rstd = lax.rsqrt(jnp.mean(vc * vc, axis=-1, keepdims=True) + EPS)
    xhat = vc * rstd
    return xhat, rstd, xhat * g + b


def sgu_fwd(ps, ln_g, ln_b, w_tril, b_t):
    s = ps.shape[0]

    def body(ps_ref, g_ref, b_ref, w_ref, bt_ref, y_ref):
        u = ps_ref[:, 0:1024].astype(F32)
        v = ps_ref[:, 1024:2048].astype(F32)
        z = ps_ref[:, 2048:3072].astype(F32)
        _, _, vn = _layernorm(v, g_ref[...], b_ref[...])
        gate = u * z * _sig(z)
        for g in range(SG_GROUPS):
            sl = slice(g * 128, (g + 1) * 128)
            mixed = _dot(w_ref[g], vn[:, sl]) + bt_ref[:, g:g + 1]
            y_ref[:, sl] = (gate[:, sl] * mixed).astype(y_ref.dtype)

    return pl.pallas_call(
        body, name="sgu_fwd", grid=(s // BLK,),
        in_specs=[pl.BlockSpec((BLK, SG_COLS), lambda c: (c, 0)), _full((1, 1024)), _full((1, 1024)),
                  _full((SG_GROUPS, BLK, BLK)), _full((BLK, 128))],
        out_specs=pl.BlockSpec((BLK, 1024), lambda c: (c, 0)),
        out_shape=jax.ShapeDtypeStruct((s, 1024), ACT_DTYPE),
        compiler_params=_cparams(("parallel",)),
    )(ps, ln_g, ln_b, w_tril, b_t)


def sgu_bwd(ps, dy, ln_g, ln_b, w_tril, w_tril_t, b_t):
    s = ps.shape[0]

    def body(ps_ref, dy_ref, g_ref, b_ref, w_ref, wt_ref, bt_ref, dps_ref, dw_ref, dbt_ref, dg_ref, db_ref, dvn_scr):
        @pl.when(pl.program_id(0) == 0)
        def _():
            dw_ref[...] = jnp.zeros_like(dw_ref)
            dbt_ref[...] = jnp.zeros_like(dbt_ref)
            dg_ref[...] = jnp.zeros_like(dg_ref)
            db_ref[...] = jnp.zeros_like(db_ref)

        u = ps_ref[:, 0:1024].astype(F32)
        v = ps_ref[:, 1024:2048].astype(F32)
        z = ps_ref[:, 2048:3072].astype(F32)
        dy = dy_ref[...].astype(F32)
        xhat, rstd, vn = _layernorm(v, g_ref[...], b_ref[...])
        sz = _sig(z)
        silu = z * sz
        row = lax.broadcasted_iota(jnp.int32, (BLK, BLK), 0)
        colm = lax.broadcasted_iota(jnp.int32, (BLK, BLK), 1)
        tril = row >= colm
        dbt = jnp.zeros((BLK, 128), F32)
        for g in range(SG_GROUPS):
            sl = slice(g * 128, (g + 1) * 128)
            vng = vn[:, sl]
            mixed = _dot(w_ref[g], vng) + bt_ref[:, g:g + 1]
            dyg, ug = dy[:, sl], u[:, sl]
            dps_ref[:, sl] = (dyg * mixed * silu[:, sl]).astype(dps_ref.dtype)
            dps_ref[:, 2048 + g * 128:2048 + (g + 1) * 128] = (
                dyg * ug * mixed * _dsilu(z[:, sl], sz[:, sl])).astype(dps_ref.dtype)
            dm = dyg * ug * silu[:, sl]
            dw_ref[g] += jnp.where(tril, _dot_nt(dm, vng), 0.0)
            dbt = dbt + jnp.where(colm == g, jnp.sum(dm, axis=1, keepdims=True), 0.0)
            dvn_scr[:, sl] = _dot(wt_ref[g], dm)
        dbt_ref[...] += dbt
        dvn = dvn_scr[...]
        dg_ref[...] += jnp.sum(dvn * xhat, axis=0, keepdims=True)
        db_ref[...] += jnp.sum(dvn, axis=0, keepdims=True)
        dxh = dvn * g_ref[...]
        dv = rstd * (dxh - jnp.mean(dxh, axis=-1, keepdims=True)
                     - xhat * jnp.mean(dxh * xhat, axis=-1, keepdims=True))
        dps_ref[:, 1024:2048] = dv.astype(dps_ref.dtype)

    return pl.pallas_call(
        body, name="sgu_bwd", grid=(s // BLK,),
        in_specs=[pl.BlockSpec((BLK, SG_COLS), lambda c: (c, 0)), pl.BlockSpec((BLK, 1024), lambda c: (c, 0)),
                  _full((1, 1024)), _full((1, 1024)), _full((SG_GROUPS, BLK, BLK)), _full((SG_GROUPS, BLK, BLK)),
                  _full((BLK, 128))],
        out_specs=[pl.BlockSpec((BLK, SG_COLS), lambda c: (c, 0)), _full((SG_GROUPS, BLK, BLK)), _full((BLK, 128)),
                   _full((1, 1024)), _full((1, 1024))],
        out_shape=[jax.ShapeDtypeStruct((s, SG_COLS), ACT_DTYPE), jax.ShapeDtypeStruct((SG_GROUPS, BLK, BLK), F32),
                   jax.ShapeDtypeStruct((BLK, 128), F32), jax.ShapeDtypeStruct((1, 1024), F32),
                   jax.ShapeDtypeStruct((1, 1024), F32)],
        scratch_shapes=[pltpu.VMEM((BLK, 1024), F32)],
        compiler_params=_cparams(("arbitrary",)),
    )(ps, dy, ln_g, ln_b, w_tril, w_tril_t, b_t)


def _shift_down(cur, prev16, k):
    if k == 0:
        return cur
    r = pltpu.roll(cur, k, 0)
    rp = pltpu.roll(prev16, k, 0)
    row = lax.broadcasted_iota(jnp.int32, (8, cur.shape[1]), 0)
    return jnp.concatenate([jnp.where(row < k, rp[0:8], r[0:8]), r[8:]], axis=0)


def _shift_up(cur, next16, k):
    if k == 0:
        return cur
    n = cur.shape[0]
    r = pltpu.roll(cur, n - k, 0)
    rn = pltpu.roll(next16, 16 - k, 0)
    row = lax.broadcasted_iota(jnp.int32, (8, cur.shape[1]), 0)
    return jnp.concatenate([r[:n - 8], jnp.where(row >= 8 - k, rn[8:16], r[n - 8:])], axis=0)


def _bcast8(v):
    return jnp.broadcast_to(v, (16, v.shape[1]))


class _Ssd:
    def __init__(self, xbc, prev16, dtr, cw, cbias, dtb, alog, dsk, tri, e):
        pre = cbias + cw[3:4] * xbc
        self.shifted = [xbc]
        for k in (1, 2, 3):
            sh = _shift_down(xbc, prev16, k)
            self.shifted.append(sh)
            pre = pre + cw[3 - k:4 - k] * sh
        self.pre = pre
        self.sg = _sig(pre)
        act = pre * self.sg
        self.xs = act[:, 0:SSM_WIDTH]
        self.bm = act[:, SSM_WIDTH:SSM_WIDTH + 512]
        self.cm = act[:, SSM_WIDTH + 512:CONV_DIM]
        self.dtp = dtr + dtb
        self.dt = jnp.maximum(self.dtp, 0.0) + jnp.log(1.0 + jnp.exp(-jnp.abs(self.dtp)))
        self.a = -jnp.exp(alog)
        self.acs = _dot_hi(tri, self.dt * self.a)
        self.acs_t = self.acs.T
        tot = self.acs[BLK - 1:BLK]
        self.ecs = jnp.exp(self.acs)
        self.dte = jnp.exp(tot - self.acs)
        self.cd = jnp.exp(tot)
        self.dt_x = _dot_onehot(self.dt, e)
        self.ecs_x = _dot_onehot(self.ecs, e)
        self.dte_x = _dot_onehot(self.dte, e)
        self.cd_x = _dot_onehot(_bcast8(self.cd), e)[0:1]
        self.d_x = _dot_onehot(_bcast8(dsk), e)[0:1]
        self.xdt = self.xs * self.dt_x
        row = lax.broadcasted_iota(jnp.int32, (BLK, BLK), 0)
        col = lax.broadcasted_iota(jnp.int32, (BLK, BLK), 1)
        self.tril = row >= col

    def group(self, g):
        sl = slice(g * 128, (g + 1) * 128)
        bg, cg = self.bm[:, sl], self.cm[:, sl]
        return bg, cg, _dot_nt(cg, bg)

    def decay(self, h):
        seg = self.acs[:, h:h + 1] - self.acs_t[h:h + 1, :]
        return jnp.exp(jnp.where(self.tril, seg, NEG))

    def y_pre_gate(self, ht_of, yd_scr, yoff_scr):
        for g in range(SSM_GROUPS):
            bg, cg, cb = self.group(g)
            for j in range(8):
                h = g * 8 + j
                sl = slice(h * 64, (h + 1) * 64)
                yd_scr[:, sl] = _dot(cb * self.decay(h), self.xdt[:, sl])
            gs = slice(g * SSM_GW, (g + 1) * SSM_GW)
            yoff_scr[:, gs] = _dot(cg, ht_of(g)) * self.ecs_x[:, gs]
        return yd_scr[...] + yoff_scr[...] + self.d_x * self.xs


def _ssd_consts():
    hh = lax.broadcasted_iota(jnp.int32, (128, SSM_WIDTH), 0)
    ch = lax.broadcasted_iota(jnp.int32, (128, SSM_WIDTH), 1)
    e = (ch // 64 == hh).astype(jnp.bfloat16)
    row = lax.broadcasted_iota(jnp.int32, (BLK, BLK), 0)
    col = lax.broadcasted_iota(jnp.int32, (BLK, BLK), 1)
    tri = (row >= col).astype(F32)
    return tri, e


def _pad_lanes(v, n=128):
    return jnp.pad(v, ((0, 0), (0, n - v.shape[1])))


def ssd_fwd(pm, cw, cbias, dtb, alog, dsk, ng):
    s = pm.shape[0]
    nc = s // BLK
    tri, e = _ssd_consts()

    def body(pm_ref, prev_ref, cw_ref, cb_ref, dtb_ref, al_ref, d_ref, ng_ref, tri_ref, e_ref,
             y_ref, st_ref, ht_ref, yd_scr, yoff_scr):
        c = pl.program_id(0)

        @pl.when(c == 0)
        def _():
            ht_ref[...] = jnp.zeros_like(ht_ref)

        xbc = pm_ref[:, 0:CONV_DIM].astype(F32)
        prev16 = jnp.where(c == 0, 0.0, prev_ref[...].astype(F32))
        f = _Ssd(xbc, prev16, pm_ref[:, DT_OFF:DT_OFF + 128].astype(F32), cw_ref[...], cb_ref[...], dtb_ref[...],
                 al_ref[...], d_ref[...], tri_ref[...], e_ref[...])
        st_ref[0] = ht_ref[...]
        y = f.y_pre_gate(lambda g: ht_ref[g], yd_scr, yoff_scr)
        for g in range(SSM_GROUPS):
            bg, _, _ = f.group(g)
            gs = slice(g * SSM_GW, (g + 1) * SSM_GW)
            ht_ref[g] = ht_ref[g] * f.cd_x[:, gs] + _dot_tn(bg, f.xdt[:, gs] * f.dte_x[:, gs])
        z = pm_ref[:, CONV_DIM:CONV_DIM + SSM_WIDTH].astype(F32)
        ypre = y * z * _sig(z)
        for g in range(SSM_GROUPS):
            gs = slice(g * SSM_GW, (g + 1) * SSM_GW)
            yg = ypre[:, gs]
            rr = lax.rsqrt(jnp.mean(yg * yg, axis=-1, keepdims=True) + EPS)
            y_ref[:, gs] = (yg * rr * ng_ref[:, gs]).astype(y_ref.dtype)

    return pl.pallas_call(
        body, name="ssd_fwd", grid=(nc,),
        in_specs=[pl.BlockSpec((BLK, SSM_COLS), lambda c: (c, 0)),
                  pl.BlockSpec((16, CONV_DIM), lambda c: (jnp.maximum(8 * c - 1, 0), 0)),
                  _full((4, CONV_DIM)), _full((1, CONV_DIM)), _full((1, 128)), _full((1, 128)), _full((1, 128)),
                  _full((1, SSM_WIDTH)), _full((BLK, BLK)), _full((128, SSM_WIDTH))],
        out_specs=[pl.BlockSpec((BLK, SSM_WIDTH), lambda c: (c, 0)),
                   pl.BlockSpec((1, SSM_GROUPS, 128, SSM_GW), lambda c: (c, 0, 0, 0))],
        out_shape=[jax.ShapeDtypeStruct((s, SSM_WIDTH), ACT_DTYPE),
                   jax.ShapeDtypeStruct((nc, SSM_GROUPS, 128, SSM_GW), F32)],
        scratch_shapes=[pltpu.VMEM((SSM_GROUPS, 128, SSM_GW), F32), pltpu.VMEM((BLK, SSM_WIDTH), F32),
                        pltpu.VMEM((BLK, SSM_WIDTH), F32)],
        compiler_params=_cparams(("arbitrary",)),
    )(pm, pm, cw, cbias, dtb, alog, dsk, ng, tri, e)


def ssd_bwd(pm, dy, states, cw, cbias, dtb, alog, dsk, ng):
    s = pm.shape[0]
    nc = s // BLK
    tri, e = _ssd_consts()
    tri_t, e_t = tri.T, e.T

    def body(pm_ref, prev_ref, dy_ref, st_ref, cw_ref, cb_ref, dtb_ref, al_ref, d_ref, ng_ref,
             tri_ref, trit_ref, e_ref, et_ref,
             dpm_ref, dcw_ref, dcb_ref, dvec_ref, dng_ref,
             dht_ref, dcar_ref, yd_scr, yoff_scr, dx_scr, r2_scr, hs_scr, da_scr, dat_scr, dd_scr, dbc_scr):
        i = pl.program_id(0)
        n = nc - 1 - i

        @pl.when(i == 0)
        def _():
            dht_ref[...] = jnp.zeros_like(dht_ref)
            dcar_ref[...] = jnp.zeros_like(dcar_ref)
            dcw_ref[...] = jnp.zeros_like(dcw_ref)
            dcb_ref[...] = jnp.zeros_like(dcb_ref)
            dvec_ref[...] = jnp.zeros_like(dvec_ref)
            dng_ref[...] = jnp.zeros_like(dng_ref)
            dd_scr[...] = jnp.zeros_like(dd_scr)
            da_scr[...] = jnp.zeros_like(da_scr)
            dat_scr[...] = jnp.zeros_like(dat_scr)

        xbc = pm_ref[:, 0:CONV_DIM].astype(F32)
        prev16 = jnp.where(n == 0, 0.0, prev_ref[...].astype(F32))
        cw = cw_ref[...]
        f = _Ssd(xbc, prev16, pm_ref[:, DT_OFF:DT_OFF + 128].astype(F32), cw, cb_ref[...], dtb_ref[...],
                 al_ref[...], d_ref[...], tri_ref[...], e_ref[...])
        et = et_ref[...]
        y = f.y_pre_gate(lambda g: st_ref[0, g], yd_scr, yoff_scr)

        z = pm_ref[:, CONV_DIM:CONV_DIM + SSM_WIDTH].astype(F32)
        dyv = dy_ref[...].astype(F32)
        sz = _sig(z)
        silu = z * sz
        ypre = y * silu
        for g in range(SSM_GROUPS):
            gs = slice(g * SSM_GW, (g + 1) * SSM_GW)
            yg = ypre[:, gs]
            rr = lax.rsqrt(jnp.mean(yg * yg, axis=-1, keepdims=True) + EPS)
            nrm = yg * rr
            dng_ref[:, gs] += jnp.sum(dyv[:, gs] * nrm, axis=0, keepdims=True)
            dn = dyv[:, gs] * ng_ref[:, gs]
            dx_scr[:, gs] = rr * (dn - nrm * jnp.mean(dn * nrm, axis=-1, keepdims=True))
        dypre = dx_scr[...]
        d_y = dypre * silu
        dpm_ref[:, CONV_DIM:CONV_DIM + SSM_WIDTH] = (dypre * y * _dsilu(z, sz)).astype(dpm_ref.dtype)

        for g in range(SSM_GROUPS):
            bg, cg, cb = f.group(g)
            gs = slice(g * SSM_GW, (g + 1) * SSM_GW)
            htg = st_ref[0, g]
            dhn = dht_ref[g]
            dcb = jnp.zeros((BLK, BLK), F32)
            for j in range(8):
                h = g * 8 + j
                sl = slice(h * 64, (h + 1) * 64)
                dec = f.decay(h)
                dyh = d_y[:, sl]
                dmd = _dot_nt(dyh, f.xdt[:, sl]) * dec
                dcb = dcb + dmd
                gm = dmd * cb
                da_scr[:, h:h + 1] = jnp.sum(gm, axis=1, keepdims=True)
                dat_scr[h:h + 1, :] = jnp.sum(gm, axis=0, keepdims=True)
                dx_scr[:, sl] = _dot_tn(cb * dec, dyh)
            dz = f.ecs_x[:, gs] * d_y[:, gs]
            dbc_scr[:, 512 + g * 128:512 + (g + 1) * 128] = _dot(dcb, bg) + _dot_nt(dz, htg)
            dbc_scr[:, g * 128:(g + 1) * 128] = _dot_tn(dcb, cg) + _dot_nt(f.xdt[:, gs] * f.dte_x[:, gs], dhn)
            dws = _dot(bg, dhn)
            dx_scr[:, gs] += f.dte_x[:, gs] * dws
            r2_scr[:, gs] = dws * f.xdt[:, gs]
            hs_scr[:, gs] = _bcast8(jnp.sum(dhn * htg, axis=0, keepdims=True))
            dht_ref[g] = f.cd_x[:, gs] * dhn + _dot_tn(cg, dz)
        d_x = dx_scr[...]
        r1 = _dot_onehot(d_y * yoff_scr[...], et)
        r2 = _dot_onehot(r2_scr[...], et) * f.dte
        dcd = _dot_onehot(hs_scr[...], et)[0:1]
        d_tot = jnp.sum(r2, axis=0, keepdims=True) + dcd * f.cd
        row = lax.broadcasted_iota(jnp.int32, (BLK, 128), 0)
        d_a = da_scr[...] - dat_scr[...].T + r1 - r2 + jnp.where(row == BLK - 1, d_tot, 0.0)
        dadt = _dot_hi(trit_ref[...], d_a)
        ddt = dadt * f.a + _dot_onehot(d_x * f.xs, et)
        lane = lax.broadcasted_iota(jnp.int32, (BLK, 128), 1)
        dr = jnp.where(lane < SSM_HEADS, ddt * _sig(f.dtp), 0.0)
        dvec_ref[0:1, :] += jnp.sum(dr, axis=0, keepdims=True)
        dvec_ref[1:2, :] += jnp.sum(dadt * f.dt, axis=0, keepdims=True) * f.a
        dd_scr[...] += _bcast8(jnp.sum(d_y * f.xs, axis=0, keepdims=True))
        dpm_ref[:, DT_OFF:DT_OFF + 128] = dr.astype(dpm_ref.dtype)
        dpm_ref[:, DT_OFF + 128:SSM_COLS] = jnp.zeros((BLK, 128), dpm_ref.dtype)

        dxs = d_x * f.dt_x + f.d_x * d_y
        dact = jnp.concatenate([dxs, dbc_scr[...]], axis=1)
        dpre = dact * _dsilu(f.pre, f.sg)
        dcb_ref[...] += jnp.sum(dpre, axis=0, keepdims=True)
        dxraw = jnp.zeros((BLK, CONV_DIM), F32)
        nxt = dcar_ref[...]
        for k in range(4):
            dcw_ref[3 - k:4 - k, :] += jnp.sum(dpre * f.shifted[k], axis=0, keepdims=True)
            dxraw = dxraw + cw[3 - k:4 - k] * _shift_up(dpre, nxt, k)
        dcar_ref[...] = dpre[0:16]
        dpm_ref[:, 0:CONV_DIM] = dxraw.astype(dpm_ref.dtype)

        @pl.when(i == nc - 1)
        def _():
            dvec_ref[2:3, :] = _dot_onehot(dd_scr[...], et)[0:1]

    return pl.pallas_call(
        body, name="ssd_bwd", grid=(nc,),
        in_specs=[pl.BlockSpec((BLK, SSM_COLS), lambda i: (nc - 1 - i, 0)),
                  pl.BlockSpec((16, CONV_DIM), lambda i: (jnp.maximum(8 * (nc - 1 - i) - 1, 0), 0)),
                  pl.BlockSpec((BLK, SSM_WIDTH), lambda i: (nc - 1 - i, 0)),
                  pl.BlockSpec((1, SSM_GROUPS, 128, SSM_GW), lambda i: (nc - 1 - i, 0, 0, 0)),
                  _full((4, CONV_DIM)), _full((1, CONV_DIM)), _full((1, 128)), _full((1, 128)), _full((1, 128)),
                  _full((1, SSM_WIDTH)), _full((BLK, BLK)), _full((BLK, BLK)), _full((128, SSM_WIDTH)),
                  _full((SSM_WIDTH, 128))],
        out_specs=[pl.BlockSpec((BLK, SSM_COLS), lambda i: (nc - 1 - i, 0)),
                   _full((8, CONV_DIM)), _full((1, CONV_DIM)), _full((8, 128)), _full((1, SSM_WIDTH))],
        out_shape=[jax.ShapeDtypeStruct((s, SSM_COLS), ACT_DTYPE), jax.ShapeDtypeStruct((8, CONV_DIM), F32),
                   jax.ShapeDtypeStruct((1, CONV_DIM), F32), jax.ShapeDtypeStruct((8, 128), F32),
                   jax.ShapeDtypeStruct((1, SSM_WIDTH), F32)],
        scratch_shapes=[pltpu.VMEM((SSM_GROUPS, 128, SSM_GW), F32), pltpu.VMEM((16, CONV_DIM), F32),
                        pltpu.VMEM((BLK, SSM_WIDTH), F32), pltpu.VMEM((BLK, SSM_WIDTH), F32),
                        pltpu.VMEM((BLK, SSM_WIDTH), F32), pltpu.VMEM((BLK, SSM_WIDTH), F32),
                        pltpu.VMEM((16, SSM_WIDTH), F32), pltpu.VMEM((BLK, 128), F32), pltpu.VMEM((128, BLK), F32),
                        pltpu.VMEM((16, SSM_WIDTH), F32), pltpu.VMEM((BLK, 1024), F32)],
        compiler_params=_cparams(("arbitrary",)),
    )(pm, pm, dy, states, cw, cbias, dtb, alog, dsk, ng, tri, tri_t, e, e_t)


def merge_fwd(x, ya, ys, ym, pg, wa, ws, wm, wo, g_post):
    s, d = x.shape
    tm = min(256, s)

    def body(x_ref, ya_ref, ys_ref, ym_ref, pg_ref, wa_ref, ws_ref, wm_ref, wo_ref, g_ref,
             xo_ref, ba_ref, bs_ref, bm_ref, mg_ref, out_ref):
        ba = _dot(ya_ref[...], wa_ref[...])
        bs = _dot(ys_ref[...], ws_ref[...])
        bm = _dot(ym_ref[...], wm_ref[...])
        merged = (_sig(pg_ref[:, 0:d].astype(F32)) * ba + _sig(pg_ref[:, d:2 * d].astype(F32)) * bs
                  + _sig(pg_ref[:, 2 * d:3 * d].astype(F32)) * bm)
        out = _dot(merged, wo_ref[...])
        r = lax.rsqrt(jnp.mean(out * out, axis=-1, keepdims=True) + EPS)
        xo_ref[...] = x_ref[...] + out * r * g_ref[...]
        ba_ref[...] = ba.astype(ba_ref.dtype)
        bs_ref[...] = bs.astype(bs_ref.dtype)
        bm_ref[...] = bm.astype(bm_ref.dtype)
        mg_ref[...] = merged.astype(mg_ref.dtype)
        out_ref[...] = out.astype(out_ref.dtype)

    rows = lambda w: pl.BlockSpec((tm, w), lambda i: (i, 0))
    act = jax.ShapeDtypeStruct((s, d), ACT_DTYPE)
    return pl.pallas_call(
        body, name="merge_fwd", grid=(s // tm,),
        in_specs=[rows(d), rows(d), rows(d), rows(2 * d), rows(3 * d), _full((d, d)), _full((d, d)),
                  _full((2 * d, d)), _full((d, d)), _full((1, d))],
        out_specs=[rows(d)] * 6,
        out_shape=[jax.ShapeDtypeStruct((s, d), F32), act, act, act, act, act],
        compiler_params=_cparams(("parallel",)),
    )(x, ya, ys, ym, pg, wa, ws, wm, wo, g_post)


def merge_bwd(dx, out_s, pg, ba, bs, bm, wa, ws, wm, wo, g_post):
    s, d = dx.shape
    tm = min(256, s)

    def body(dx_ref, out_ref, pg_ref, ba_ref, bs_ref, bm_ref, wa_ref, ws_ref, wm_ref, wo_ref, g_ref,
             dout_ref, dba_ref, dbs_ref, dbm_ref, dpg_ref, dya_ref, dys_ref, dym_ref, dg_ref):
        @pl.when(pl.program_id(0) == 0)
        def _():
            dg_ref[...] = jnp.zeros_like(dg_ref)

        o = out_ref[...].astype(F32)
        dxv = dx_ref[...]
        r = lax.rsqrt(jnp.mean(o * o, axis=-1, keepdims=True) + EPS)
        nrm = o * r
        dg_ref[...] += jnp.sum(dxv * nrm, axis=0, keepdims=True)
        dn = dxv * g_ref[...]
        dout = r * (dn - nrm * jnp.mean(dn * nrm, axis=-1, keepdims=True))
        dout_ref[...] = dout.astype(dout_ref.dtype)
        dmerged = _dot_nt(dout, wo_ref[...])
        for q, (b_ref, db_ref, w_ref, dy_ref) in enumerate(((ba_ref, dba_ref, wa_ref, dya_ref),
                                                            (bs_ref, dbs_ref, ws_ref, dys_ref),
                                                            (bm_ref, dbm_ref, wm_ref, dym_ref))):
            gt = _sig(pg_ref[:, q * d:(q + 1) * d].astype(F32))
            db = dmerged * gt
            db_ref[...] = db.astype(db_ref.dtype)
            dpg_ref[:, q * d:(q + 1) * d] = (dmerged * b_ref[...].astype(F32) * gt * (1.0 - gt)).astype(dpg_ref.dtype)
            dy_ref[...] = _dot_nt(db, w_ref[...]).astype(dy_ref.dtype)

    rows = lambda w: pl.BlockSpec((tm, w), lambda i: (i, 0))
    act = lambda w: jax.ShapeDtypeStruct((s, w), ACT_DTYPE)
    return pl.pallas_call(
        body, name="merge_bwd", grid=(s // tm,),
        in_specs=[rows(d), rows(d), rows(3 * d), rows(d), rows(d), rows(d), _full((d, d)), _full((d, d)),
                  _full((2 * d, d)), _full((d, d)), _full((1, d))],
        out_specs=[rows(d), rows(d), rows(d), rows(d), rows(3 * d), rows(d), rows(d), rows(2 * d), _full((1, d))],
        out_shape=[act(d), act(d), act(d), act(d), act(3 * d), act(d), act(d), act(2 * d),
                   jax.ShapeDtypeStruct((1, d), F32)],
        compiler_params=_cparams(("arbitrary",)),
    )(dx, out_s, pg, ba, bs, bm, wa, ws, wm, wo, g_post)


def loss_grad(y, target):
    s, d = y.shape
    tm = min(512, s)

    def body(y_ref, t_ref, dy_ref, l_ref):
        @pl.when(pl.program_id(0) == 0)
        def _():
            l_ref[...] = jnp.zeros_like(l_ref)

        err = y_ref[...] - t_ref[...]
        dy_ref[...] = err * (1.0 / d)
        part = jnp.sum(jnp.sum(err * err, axis=-1, keepdims=True) * (1.0 / d), axis=0, keepdims=True)
        l_ref[...] += 0.5 * jnp.broadcast_to(part, l_ref.shape)

    return pl.pallas_call(
        body, name="loss_grad", grid=(s // tm,),
        in_specs=[pl.BlockSpec((tm, d), lambda i: (i, 0)), pl.BlockSpec((tm, d), lambda i: (i, 0))],
        out_specs=[pl.BlockSpec((tm, d), lambda i: (i, 0)), _full((8, 128))],
        out_shape=[jax.ShapeDtypeStruct((s, d), F32), jax.ShapeDtypeStruct((8, 128), F32)],
        compiler_params=_cparams(("arbitrary",)),
    )(y, target)


def _mesh_pos():
    x, y, c = lax.axis_index("x"), lax.axis_index("y"), lax.axis_index("c")
    return x, y, c, 4 * x + 2 * y + c


def _peer(x, y, c, k):
    px = 1 - x if k & 4 else x
    py = 1 - y if k & 2 else y
    pc = 1 - c if k & 1 else c
    return (px, py, pc), 4 * px + 2 * py + pc


class Exchange:
    def __init__(self, scattered, gathered):
        self.ns = len(scattered)
        self.arrays = list(scattered) + list(gathered)
        self.na = len(self.arrays)
        any_spec = pl.BlockSpec(memory_space=pl.ANY)
        self.in_specs = [any_spec] * self.na
        self.out_specs = [any_spec] * self.na
        self.out_shape = ([jax.ShapeDtypeStruct(a.shape, a.dtype) for a in scattered]
                          + [jax.ShapeDtypeStruct((N_DEV,) + a.shape, a.dtype) for a in gathered])
        self.scratch = [pltpu.SemaphoreType.DMA((self.na, N_DEV - 1)), pltpu.SemaphoreType.DMA((self.na, N_DEV - 1)),
                        pltpu.SemaphoreType.DMA((self.na,))]

    def _src(self, ins, q, slot):
        return ins[q].at[slot] if q < self.ns else ins[q]

    def _local(self, ins, outs, sems):
        me = _mesh_pos()[3]
        return [pltpu.make_async_copy(self._src(ins, q, me), outs[q].at[me], sems[2].at[q]) for q in range(self.na)]

    def _remote(self, ins, outs, sems, incoming):
        x, y, c, me = _mesh_pos()
        copies = []
        for k in range(1, N_DEV):
            peer, pidx = _peer(x, y, c, k)
            for q in range(self.na):
                copies.append(pltpu.make_async_remote_copy(
                    src_ref=self._src(ins, q, pidx), dst_ref=outs[q].at[pidx if incoming else me],
                    send_sem=sems[0].at[q, k - 1], recv_sem=sems[1].at[q, k - 1], device_id=peer,
                    device_id_type=MESH))
        return copies

    def start(self, ins, outs, sems):
        for cp in self._local(ins, outs, sems) + self._remote(ins, outs, sems, incoming=False):
            cp.start()

    def wait(self, ins, outs, sems):
        for cp in self._remote(ins, outs, sems, incoming=True):
            cp.wait_recv()
        for cp in self._remote(ins, outs, sems, incoming=False):
            cp.wait_send()
        for cp in self._local(ins, outs, sems):
            cp.wait()


def exchange(scattered, gathered, name):
    ex = Exchange(scattered, gathered)

    def body(*refs):
        ins, outs, sems = refs[:ex.na], refs[ex.na:2 * ex.na], refs[2 * ex.na:]
        ex.start(ins, outs, sems)
        ex.wait(ins, outs, sems)

    return pl.pallas_call(body, name=name, in_specs=ex.in_specs, out_specs=ex.out_specs, out_shape=ex.out_shape,
                          scratch_shapes=ex.scratch)(*ex.arrays)


def adamw(parts, w, m, v, tile, name):
    npart, r, dp = parts.shape
    d = w.shape[1]

    def body(p_ref, w_ref, m_ref, v_ref, g_ref, dw_ref, nm_ref, nv_ref):
        g = p_ref[0, :, 0:d].astype(F32)
        for q in range(1, npart):
            g = g + p_ref[q, :, 0:d].astype(F32)
        g_ref[...] = g
        nm = ADAM_B1 * m_ref[...] + (1.0 - ADAM_B1) * g
        nv = ADAM_B2 * v_ref[...] + (1.0 - ADAM_B2) * (g * g)
        nm_ref[...] = nm
        nv_ref[...] = nv
        m_hat = nm / (1.0 - ADAM_B1 ** ADAM_STEP)
        v_hat = nv / (1.0 - ADAM_B2 ** ADAM_STEP)
        dw_ref[...] = -ADAM_LR * (m_hat / (jnp.sqrt(v_hat) + ADAM_EPS) + ADAM_WD * w_ref[...])

    rows = pl.BlockSpec((tile, d), lambda i: (i, 0))
    out = jax.ShapeDtypeStruct((r, d), F32)
    return pl.pallas_call(
        body, name=name, grid=(r // tile,),
        in_specs=[pl.BlockSpec((npart, tile, dp), lambda i: (0, i, 0)), rows, rows, rows],
        out_specs=[rows] * 4, out_shape=[out] * 4,
        compiler_params=_cparams(("parallel",)),
    )(parts, w, m, v)


def _pad_rows(a, rows):
    return jnp.pad(a, ((0, rows - a.shape[0]), (0, 0)))


def _pack_rest(w_att, w_sg, w_ssm, w_out):
    parts = []
    for l in range(2):
        parts += [w_att[l], w_sg[l], w_ssm[l], w_out[l]]
    return jnp.concatenate(parts, axis=0)


def _unpack_rest(p):
    outs = [[], [], [], []]
    o = 0
    for l in range(2):
        for q, rws in enumerate(REST_PARTS):
            outs[q].append(p[o:o + rws])
            o += rws
    return [jnp.stack(t) for t in outs]


def _pack_win(w_in):
    return jnp.pad(w_in.reshape(2 * D_MODEL, WIN_SHARD), ((0, 0), (0, WIN_LANES - WIN_SHARD)))


W_IN_MAP = ((0, 1024, "att", 0), (1024, 1280, "att", 2048), (1280, 2304, "att", 1024), (2304, 5376, "sg", 0),
            (5376, 7424, "ssm", 3072), (7424, 10496, "ssm", 0), (10496, 10528, "ssm", 5120), (10528, 13600, "gate", 0))
SLAB_COLS = {"att": ATT_COLS, "sg": SG_COLS, "ssm": SSM_COLS, "gate": GATE_COLS}


def _slabs_from_shards(g):
    slabs = {}
    for name, width in SLAB_COLS.items():
        pieces, filled = [], 0
        for ga, gb, _, off in sorted((m for m in W_IN_MAP if m[2] == name), key=lambda m: m[3]):
            assert off == filled
            a = ga
            while a < gb:
                d = a // WIN_SHARD
                hi = min(gb, WIN_SHARD * (d + 1))
                pieces.append(g[d, :, a - WIN_SHARD * d:hi - WIN_SHARD * d])
                a = hi
            filled += gb - ga
        if filled < width:
            pieces.append(jnp.zeros((D_MODEL, width - filled), g.dtype))
        slabs[name] = jnp.concatenate(pieces, axis=1)
    return slabs


def _shards_from_slabs(dslabs):
    out = []
    for d in range(N_DEV):
        a, b = WIN_SHARD * d, WIN_SHARD * (d + 1)
        pieces = []
        for ga, gb, name, off in W_IN_MAP:
            lo, hi = max(a, ga), min(b, gb)
            if lo < hi:
                pieces.append(dslabs[name][:, off + lo - ga:off + hi - ga])
        pieces.append(jnp.zeros((D_MODEL, WIN_LANES - WIN_SHARD), pieces[0].dtype))
        out.append(jnp.concatenate(pieces, axis=1).astype(WIRE_DTYPE))
    return jnp.stack(out)


SMALL_SIZES = (("norm_pre", 2048), ("norm_post", 2048), ("rel_bias", 512), ("att_sinks", 32), ("sg_ln_g", 2048),
               ("sg_ln_b", 2048), ("sg_w", 262144), ("sg_b", 2048), ("ssm_conv_b", 6144), ("ssm_dt_bias", 64),
               ("ssm_a_log", 64), ("ssm_d", 64), ("ssm_norm_g", 4096), ("conv_w_full", 24576))


def _pack_small(d):
    parts = []
    for name, size in SMALL_SIZES:
        rows = 8 * (-(-size // (8 * D_MODEL)))
        flat = d[name].reshape(-1) if name in d else jnp.zeros((size,), F32)
        parts.append(jnp.pad(flat, (0, rows * D_MODEL - size)).reshape(rows, D_MODEL))
    return _pad_rows(jnp.concatenate(parts, axis=0), SMALL_ROWS)


def _unpack_small(p, shapes):
    out, o = {}, 0
    for name, size in SMALL_SIZES:
        rows = 8 * (-(-size // (8 * D_MODEL)))
        if name in shapes:
            out[name] = p[o:o + rows].reshape(-1)[:size].reshape(shapes[name])
        o += rows
    return out


def _bucket_onehot_t():
    qi = jnp.arange(BLK, dtype=jnp.int32)[:, None]
    kj = jnp.arange(BLK, dtype=jnp.int32)[None, :]
    dd = (qi - kj) & (BLK - 1)
    in_window = dd >= 0
    max_exact = REL_BUCKETS // 2
    dist_f = jnp.maximum(dd, 1).astype(F32)
    large = max_exact + (jnp.log(dist_f / max_exact) / math.log(128 / max_exact)
                         * (REL_BUCKETS - max_exact)).astype(jnp.int32)
    large = jnp.minimum(large, REL_BUCKETS - 1)
    bucket = jnp.where(dd < max_exact, dd, large).reshape(1, -1)
    onehot_t = (bucket == jnp.arange(REL_BUCKETS, dtype=jnp.int32)[:, None]).astype(F32)
    maskadd = jnp.where(in_window, 0.0, NEG).astype(F32).reshape(1, -1)
    return onehot_t, maskadd


WEIGHTS = ['w_in', 'norm_pre', 'norm_post', 'rel_bias', 'att_sinks', 'sg_ln_g', 'sg_ln_b', 'sg_w', 'sg_b',
           'ssm_conv_w', 'ssm_conv_b', 'ssm_dt_bias', 'ssm_a_log', 'ssm_d', 'ssm_norm_g',
           'w_br_att', 'w_br_sg', 'w_br_ssm', 'w_out']
REST = ('w_br_att', 'w_br_sg', 'w_br_ssm', 'w_out')


def kernel(x, w_in, norm_pre, norm_post, rel_bias, att_sinks, sg_ln_g, sg_ln_b, sg_w, sg_b, ssm_conv_w, ssm_conv_b, ssm_dt_bias, ssm_a_log, ssm_d, ssm_norm_g, w_br_att, w_br_sg, w_br_ssm, w_out, loss_target, m_w_in, m_norm_pre, m_norm_post, m_rel_bias, m_att_sinks, m_sg_ln_g, m_sg_ln_b, m_sg_w, m_sg_b, m_ssm_conv_w, m_ssm_conv_b, m_ssm_dt_bias, m_ssm_a_log, m_ssm_d, m_ssm_norm_g, m_w_br_att, m_w_br_sg, m_w_br_ssm, m_w_out, v_w_in, v_norm_pre, v_norm_post, v_rel_bias, v_att_sinks, v_sg_ln_g, v_sg_ln_b, v_sg_w, v_sg_b, v_ssm_conv_w, v_ssm_conv_b, v_ssm_dt_bias, v_ssm_a_log, v_ssm_d, v_ssm_norm_g, v_w_br_att, v_w_br_sg, v_w_br_ssm, v_w_out):
    w = dict(w_in=w_in, norm_pre=norm_pre, norm_post=norm_post, rel_bias=rel_bias, att_sinks=att_sinks,
             sg_ln_g=sg_ln_g, sg_ln_b=sg_ln_b, sg_w=sg_w, sg_b=sg_b, ssm_conv_w=ssm_conv_w, ssm_conv_b=ssm_conv_b,
             ssm_dt_bias=ssm_dt_bias, ssm_a_log=ssm_a_log, ssm_d=ssm_d, ssm_norm_g=ssm_norm_g,
             w_br_att=w_br_att, w_br_sg=w_br_sg, w_br_ssm=w_br_ssm, w_out=w_out)
    mom = dict(w_in=m_w_in, norm_pre=m_norm_pre, norm_post=m_norm_post, rel_bias=m_rel_bias, att_sinks=m_att_sinks,
               sg_ln_g=m_sg_ln_g, sg_ln_b=m_sg_ln_b, sg_w=m_sg_w, sg_b=m_sg_b, ssm_conv_w=m_ssm_conv_w,
               ssm_conv_b=m_ssm_conv_b, ssm_dt_bias=m_ssm_dt_bias, ssm_a_log=m_ssm_a_log, ssm_d=m_ssm_d,
               ssm_norm_g=m_ssm_norm_g, w_br_att=m_w_br_att, w_br_sg=m_w_br_sg, w_br_ssm=m_w_br_ssm, w_out=m_w_out)
    var = dict(w_in=v_w_in, norm_pre=v_norm_pre, norm_post=v_norm_post, rel_bias=v_rel_bias, att_sinks=v_att_sinks,
               sg_ln_g=v_sg_ln_g, sg_ln_b=v_sg_ln_b, sg_w=v_sg_w, sg_b=v_sg_b, ssm_conv_w=v_ssm_conv_w,
               ssm_conv_b=v_ssm_conv_b, ssm_dt_bias=v_ssm_dt_bias, ssm_a_log=v_ssm_a_log, ssm_d=v_ssm_d,
               ssm_norm_g=v_ssm_norm_g, w_br_att=v_w_br_att, w_br_sg=v_w_br_sg, w_br_ssm=v_w_br_ssm, w_out=v_w_out)
    xs0 = x[0]
    target = loss_target[0]
    my_dev = 4 * lax.axis_index("x") + 2 * lax.axis_index("y") + lax.axis_index("c")

    conv_shard = _pad_rows(ssm_conv_w.reshape(-1, D_MODEL), 8)
    win_shard = _pack_win(w_in).astype(WIRE_DTYPE)
    rest_shard = _pack_rest(*[w[n] for n in REST]).astype(WIRE_DTYPE)
    layer_shards = [[win_shard[l * D_MODEL:(l + 1) * D_MODEL], rest_shard[l * LAYER_REST:(l + 1) * LAYER_REST]]
                    for l in range(2)]
    g_win0, g_rest0, gathered_conv = exchange([], layer_shards[0] + [conv_shard], "all_gather")
    conv_full = gathered_conv[:, 0:3].reshape(N_DEV, 2, 4, 384).transpose(1, 2, 0, 3).reshape(2, 4, CONV_DIM)

    def layer_weights(l, g_win, g_rest):
        slabs = _slabs_from_shards(g_win)
        lw = {"in_" + name: slab.astype(MXU_DTYPE) for name, slab in slabs.items()}
        o = 0
        for name, rws in zip(("att", "sg", "ssm", "out"), REST_PARTS):
            lw[name] = g_rest[:, o:o + rws].reshape(N_DEV * rws, D_MODEL).astype(MXU_DTYPE)
            o += rws
        tril = jnp.tril(jnp.ones((BLK, BLK), bool))
        sgw = jnp.where(tril[None], sg_w[l], 0.0)
        lw.update(
            g_pre=norm_pre[l][None], g_post=norm_post[l][None], sinks=_pad_lanes(att_sinks[l][None]),
            ln_g=sg_ln_g[l][None], ln_b=sg_ln_b[l][None], sgw=sgw.astype(MXU_DTYPE),
            sgw_t=sgw.transpose(0, 2, 1).astype(MXU_DTYPE), sgb_t=_pad_lanes(sg_b[l].T),
            cw=conv_full[l], cb=ssm_conv_b[l][None], dtb=_pad_lanes(ssm_dt_bias[l][None]),
            alog=_pad_lanes(ssm_a_log[l][None]), dsk=_pad_lanes(ssm_d[l][None]), ng=ssm_norm_g[l][None])
        return lw

    onehot_t, maskadd = _bucket_onehot_t()
    bias = bias_table(rel_bias.T, onehot_t, maskadd).reshape(ATT_HEADS, BLK, BLK)

    saved = []
    xl = xs0
    layers = [layer_weights(0, g_win0, g_rest0)]
    for l in range(2):
        lw = layers[l]
        h = rmsnorm_fwd(xl, lw["g_pre"])
        pa = mm_nn(h, lw["in_att"], 1152, "proj_att")
        ps = mm_nn(h, lw["in_sg"], 1536, "proj_sg")
        pm = mm_nn(h, lw["in_ssm"], 1792, "proj_ssm")
        pg = mm_nn(h, lw["in_gate"], 1536, "proj_gate")
        if l == 0:
            ya, (g_win1, g_rest1) = attn_fwd(pa, bias, lw["sinks"], Exchange([], layer_shards[1]))
            layers.append(layer_weights(1, g_win1, g_rest1))
        else:
            ya = attn_fwd(pa, bias, lw["sinks"])
        ys = sgu_fwd(ps, lw["ln_g"], lw["ln_b"], lw["sgw"], lw["sgb_t"])
        ym, states = ssd_fwd(pm, lw["cw"], lw["cb"], lw["dtb"], lw["alog"], lw["dsk"], lw["ng"])
        x_next, ba, bs, bm, merged, out_s = merge_fwd(xl, ya, ys, ym, pg, lw["att"], lw["sg"], lw["ssm"], lw["out"],
                                                      lw["g_post"])
        saved.append(dict(x=xl, h=h, pa=pa, ps=ps, pm=pm, pg=pg, ya=ya, ys=ys, ym=ym, states=states, ba=ba, bs=bs,
                          bm=bm, merged=merged, out_s=out_s))
        xl = x_next

    dx, loss_part = loss_grad(xl, target)
    loss = lax.psum(loss_part[0, 0], ("x", "y", "c"))

    dbias = jnp.zeros((ATT_HEADS, BLK, BLK), F32)
    win_grads, rest_grads = [None, None], [None, None]
    small = {n: [None, None] for n in ("norm_pre", "norm_post", "att_sinks", "sg_ln_g", "sg_ln_b", "sg_w", "sg_b",
                                       "ssm_conv_b", "ssm_dt_bias", "ssm_a_log", "ssm_d", "ssm_norm_g",
                                       "conv_w_full")}
    for l in (1, 0):
        lw, sv = layers[l], saved[l]
        dout, dba, dbs, dbm, dpg, dya, dys, dym, dg_post = merge_bwd(
            dx, sv["out_s"], sv["pg"], sv["ba"], sv["bs"], sv["bm"], lw["att"], lw["sg"], lw["ssm"], lw["out"],
            lw["g_post"])
        dw_out = mm_tn(sv["merged"], dout, 1024, "dw_out")
        dw_att = mm_tn(sv["ya"], dba, 1024, "dw_br_att")
        dw_sg = mm_tn(sv["ys"], dbs, 1024, "dw_br_sg")
        dw_ssm = mm_tn(sv["ym"], dbm, 1024, "dw_br_ssm")
        rest_grads[l] = jnp.concatenate(
            [dw_att.reshape(N_DEV, 128, D_MODEL), dw_sg.reshape(N_DEV, 128, D_MODEL),
             dw_ssm.reshape(N_DEV, 256, D_MODEL), dw_out.reshape(N_DEV, 128, D_MODEL)], axis=1).astype(WIRE_DTYPE)
        if l == 0:
            dpa, dbias, dsinks, (recv_win1, recv_rest1, recv_rest0) = attn_bwd(
                sv["pa"], dya, bias, lw["sinks"], dbias, Exchange([win_grads[1], rest_grads[1], rest_grads[0]], []))
        else:
            dpa, dbias, dsinks = attn_bwd(sv["pa"], dya, bias, lw["sinks"], dbias)
        dps, dsgw, dsgb_t, dln_g, dln_b = sgu_bwd(sv["ps"], dys, lw["ln_g"], lw["ln_b"], lw["sgw"], lw["sgw_t"],
                                                  lw["sgb_t"])
        dpm, dcw, dcb, dvec, dng = ssd_bwd(sv["pm"], dym, sv["states"], lw["cw"], lw["cb"], lw["dtb"], lw["alog"],
                                           lw["dsk"], lw["ng"])
        dslabs = dict(att=mm_tn(sv["h"], dpa, 2304, "dw_in_att"), sg=mm_tn(sv["h"], dps, 3072, "dw_in_sg"),
                      ssm=mm_tn(sv["h"], dpm, 2688, "dw_in_ssm"), gate=mm_tn(sv["h"], dpg, 3072, "dw_in_gate"))
        win_grads[l] = _shards_from_slabs(dslabs)
        dh_args = ([dpa, dps, dpm, dpg], [lw["in_att"], lw["in_sg"], lw["in_ssm"], lw["in_gate"]], sv["x"],
                   lw["g_pre"], dx)
        if l == 0:
            dx, dg_pre, (recv_win0,) = dh_norm_bwd(*dh_args, Exchange([win_grads[0]], []))
        else:
            dx, dg_pre = dh_norm_bwd(*dh_args)
        small["norm_pre"][l] = dg_pre[0]
        small["norm_post"][l] = dg_post[0]
        small["att_sinks"][l] = dsinks[0, :ATT_HEADS]
        small["sg_ln_g"][l] = dln_g[0]
        small["sg_ln_b"][l] = dln_b[0]
        small["sg_w"][l] = dsgw
        small["sg_b"][l] = dsgb_t[:, :SG_GROUPS].T
        small["ssm_conv_b"][l] = dcb[0]
        small["ssm_dt_bias"][l] = dvec[0, :SSM_HEADS]
        small["ssm_a_log"][l] = dvec[1, :SSM_HEADS]
        small["ssm_d"][l] = dvec[2, :SSM_HEADS]
        small["ssm_norm_g"][l] = dng[0]
        small["conv_w_full"][l] = dcw[0:4]
    grad_x = dx
    d_rel_bias = bias_table_bwd(dbias.reshape(ATT_HEADS, -1), onehot_t).T

    small_d = {n: jnp.stack(v) for n, v in small.items()}
    small_d["rel_bias"] = d_rel_bias
    (recv_small,) = exchange([], [_pack_small(small_d)], "gather_small_grads")

    res_win, res_rest = [], []
    for l, (recv_win, recv_rest) in enumerate(((recv_win0, recv_rest0), (recv_win1, recv_rest1))):
        res_win.append(adamw(recv_win, w_in[l], m_w_in[l], v_w_in[l], WIN_TILE, "adamw_w_in"))
        layer_rest = lambda t: jnp.concatenate([t[n][l] for n in REST], axis=0)
        res_rest.append(adamw(recv_rest, layer_rest(w), layer_rest(mom), layer_rest(var), REST_TILE, "adamw_rest"))
    res_win = [jnp.stack([res_win[0][q], res_win[1][q]]) for q in range(4)]
    res_rest = [jnp.concatenate([res_rest[0][q], res_rest[1][q]], axis=0) for q in range(4)]
    small_names = [n for n, _ in SMALL_SIZES if n != "conv_w_full"]
    g_s, dw_s, nm_s, nv_s = adamw(recv_small, _pack_small({n: w[n] for n in small_names}),
                                  _pack_small({n: mom[n] for n in small_names}),
                                  _pack_small({n: var[n] for n in small_names}), SMALL_TILE, "adamw_small")
    shapes = {n: w[n].shape for n in small_names}
    shapes["conv_w_full"] = (2, 4, CONV_DIM)
    g_conv_full = _unpack_small(g_s, shapes)["conv_w_full"]
    g_conv = lax.dynamic_slice_in_dim(g_conv_full, my_dev * 384, 384, axis=2)
    pack_conv = lambda a: _pad_rows(a.reshape(-1, D_MODEL), 8)
    g_c, dw_c, nm_c, nv_c = adamw(pack_conv(g_conv)[None], pack_conv(ssm_conv_w), pack_conv(m_ssm_conv_w),
                                  pack_conv(v_ssm_conv_w), 8, "adamw_conv")

    results = {}
    for q, (tag, psm, pc) in enumerate((("grad", g_s, g_c), ("delta", dw_s, dw_c), ("new_m", nm_s, nm_c),
                                        ("new_v", nv_s, nv_c))):
        r = dict(zip(REST, _unpack_rest(res_rest[q])))
        r["w_in"] = res_win[q]
        r.update(_unpack_small(psm, {n: w[n].shape for n in small_names}))
        r["ssm_conv_w"] = pc[0:3].reshape(2, 4, 384)
        results[tag] = r
    outs = [loss, grad_x[None]]
    for tag in ("grad", "delta", "new_m", "new_v"):
        outs += [results[tag][n] for n in WEIGHTS]
    return tuple(outs)
```

```python
import math

import jax
import jax.numpy as jnp
from jax import lax
from jax.experimental import pallas as pl
from jax.experimental.pallas import tpu as pltpu

F32 = jnp.float32
MXU_DTYPE = jnp.bfloat16
ACT_DTYPE = jnp.bfloat16
WIRE_DTYPE = jnp.bfloat16
HI = lax.Precision.HIGHEST
MESH = pl.DeviceIdType.MESH

D_MODEL = 1024
N_DEV = 8
ATT_HEADS = 16
HEAD_DIM = 64
BLK = 128
SG_GROUPS = 8
SSM_WIDTH = 2048
SSM_HEADS = 32
SSM_GROUPS = 4
SSM_GW = SSM_WIDTH // SSM_GROUPS
CONV_DIM = 3072
REL_BUCKETS = 32
EPS = 1e-6
NEG = -1e30

ATT_COLS = 2304
SG_COLS = 3072
SSM_COLS = 5376
GATE_COLS = 3072
DT_OFF = 5120

VMEM_LIMIT_V7X = 56 * 2 ** 20

ADAM_LR, ADAM_B1, ADAM_B2, ADAM_EPS, ADAM_WD, ADAM_STEP = 0.001, 0.9, 0.999, 1e-08, 0.01, 10

WIN_SHARD = 1700
WIN_LANES = 1792
REST_PARTS = (128, 128, 256, 128)
LAYER_REST = sum(REST_PARTS)
REST_TILE = 128
WIN_TILE = 128
SMALL_ROWS = 384
SMALL_TILE = 128


def _cparams(sem=None):
    return pltpu.CompilerParams(dimension_semantics=sem, vmem_limit_bytes=VMEM_LIMIT_V7X)


def _dot(a, b):
    return jnp.dot(a.astype(MXU_DTYPE), b.astype(MXU_DTYPE), preferred_element_type=F32)


def _dot_nt(a, b):
    return lax.dot_general(a.astype(MXU_DTYPE), b.astype(MXU_DTYPE), (((1,), (1,)), ((), ())),
                           preferred_element_type=F32)


def _dot_tn(a, b):
    return lax.dot_general(a.astype(MXU_DTYPE), b.astype(MXU_DTYPE), (((0,), (0,)), ((), ())),
                           preferred_element_type=F32)


def _dot_hi(a, b):
    return jnp.dot(a, b, precision=HI, preferred_element_type=F32)


def _dot_onehot(a, onehot):
    hi = a.astype(jnp.bfloat16)
    lo = (a - hi.astype(F32)).astype(jnp.bfloat16)
    return (jnp.dot(hi, onehot, preferred_element_type=F32) + jnp.dot(lo, onehot, preferred_element_type=F32))


def _dot_hi_nt(a, b):
    return lax.dot_general(a, b, (((1,), (1,)), ((), ())), precision=HI, preferred_element_type=F32)


def _sig(x):
    return 1.0 / (1.0 + jnp.exp(-x))


def _dsilu(x, s):
    return s * (1.0 + x * (1.0 - s))


def _full(shape):
    nd = len(shape)
    return pl.BlockSpec(shape, lambda *_: (0,) * nd)


def rmsnorm_fwd(x, g):
    s, d = x.shape
    tm = min(512, s)

    def body(x_ref, g_ref, o_ref):
        xv = x_ref[...]
        r = lax.rsqrt(jnp.mean(xv * xv, axis=-1, keepdims=True) + EPS)
        o_ref[...] = (xv * r * g_ref[...]).astype(o_ref.dtype)

    return pl.pallas_call(
        body, name="rmsnorm_fwd", grid=(s // tm,),
        in_specs=[pl.BlockSpec((tm, d), lambda i: (i, 0)), _full((1, d))],
        out_specs=pl.BlockSpec((tm, d), lambda i: (i, 0)),
        out_shape=jax.ShapeDtypeStruct((s, d), ACT_DTYPE),
        compiler_params=_cparams(("parallel",)),
    )(x, g)


def mm_nn(a, b, tn, name):
    s, k = a.shape
    n = b.shape[1]
    tm = min(2048, s)

    def body(a_ref, b_ref, o_ref):
        o_ref[...] = _dot(a_ref[...], b_ref[...]).astype(o_ref.dtype)

    return pl.pallas_call(
        body, name=name, grid=(s // tm, n // tn),
        in_specs=[pl.BlockSpec((tm, k), lambda i, j: (i, 0)), pl.BlockSpec((k, tn), lambda i, j: (0, j))],
        out_specs=pl.BlockSpec((tm, tn), lambda i, j: (i, j)),
        out_shape=jax.ShapeDtypeStruct((s, n), ACT_DTYPE),
        compiler_params=_cparams(("parallel", "arbitrary")),
    )(a, b)


def mm_tn(a, b, tn, name):
    s, k = a.shape
    n = b.shape[1]
    ts = min(512, s)

    def body(a_ref, b_ref, o_ref):
        @pl.when(pl.program_id(1) == 0)
        def _():
            o_ref[...] = jnp.zeros_like(o_ref)

        o_ref[...] += _dot_tn(a_ref[...], b_ref[...])

    return pl.pallas_call(
        body, name=name, grid=(n // tn, s // ts),
        in_specs=[pl.BlockSpec((ts, k), lambda j, t: (t, 0)), pl.BlockSpec((ts, tn), lambda j, t: (t, j))],
        out_specs=pl.BlockSpec((k, tn), lambda j, t: (0, j)),
        out_shape=jax.ShapeDtypeStruct((k, n), F32),
        compiler_params=_cparams(("parallel", "arbitrary")),
    )(a, b)


def dh_norm_bwd(dslabs, wslabs, x, g, dres, ex=None):
    s, d = x.shape
    tm = min(1024, s)
    tk = 768
    counts = [ds.shape[1] // tk for ds in dslabs]
    starts = [sum(counts[:i]) for i in range(len(counts))]
    nk = sum(counts)
    ns = len(dslabs)

    hosted = ex is not None
    ni = s // tm

    def mm_body(*refs):
        (own_in, (dh_ref,), _), hosted_refs = _split_hosted(refs, 2 * ns, 1, 0, ex)
        d_refs, w_refs = own_in[:ns], own_in[ns:]
        i, k = pl.program_id(0), pl.program_id(1)
        if hosted:
            @pl.when((i == 0) & (k == 0))
            def _():
                ex.start(*hosted_refs)

            @pl.when((i == ni - 1) & (k == nk - 1))
            def _():
                ex.wait(*hosted_refs)

        @pl.when(k == 0)
        def _():
            dh_ref[...] = jnp.zeros_like(dh_ref)

        for q in range(ns):
            @pl.when((k >= starts[q]) & (k < starts[q] + counts[q]))
            def _(q=q):
                dh_ref[...] += _dot_nt(d_refs[q][...], w_refs[q][...])

    def clamp(q):
        return lambda i, k: (i, jnp.clip(k - starts[q], 0, counts[q] - 1))

    def clamp_w(q):
        return lambda i, k: (0, jnp.clip(k - starts[q], 0, counts[q] - 1))

    res = pl.pallas_call(
        mm_body, name="dh_matmul_scatter" if hosted else "dh_matmul", grid=(ni, nk),
        in_specs=([pl.BlockSpec((tm, tk), clamp(q)) for q in range(ns)]
                  + [pl.BlockSpec((d, tk), clamp_w(q)) for q in range(ns)] + (ex.in_specs if hosted else [])),
        out_specs=[pl.BlockSpec((tm, d), lambda i, k: (i, 0))] + (ex.out_specs if hosted else []),
        out_shape=[jax.ShapeDtypeStruct((s, d), F32)] + (ex.out_shape if hosted else []),
        scratch_shapes=ex.scratch if hosted else [],
        compiler_params=_cparams(("arbitrary" if hosted else "parallel", "arbitrary")),
    )(*dslabs, *wslabs, *(ex.arrays if hosted else []))
    dh, ex_results = res[0], res[1:]

    te = min(512, s)

    def norm_body(dh_ref, x_ref, g_ref, dres_ref, dx_ref, dg_ref):
        @pl.when(pl.program_id(0) == 0)
        def _():
            dg_ref[...] = jnp.zeros_like(dg_ref)

        xv = x_ref[...]
        r = lax.rsqrt(jnp.mean(xv * xv, axis=-1, keepdims=True) + EPS)
        xn = xv * r
        dhv = dh_ref[...]
        dg_ref[...] += jnp.sum(dhv * xn, axis=0, keepdims=True)
        dxn = dhv * g_ref[...]
        dx_ref[...] = dres_ref[...] + r * (dxn - xn * jnp.mean(dxn * xn, axis=-1, keepdims=True))

    rows = pl.BlockSpec((te, d), lambda i: (i, 0))
    dx, dg = pl.pallas_call(
        norm_body, name="norm_bwd", grid=(s // te,),
        in_specs=[rows, rows, _full((1, d)), rows],
        out_specs=[rows, _full((1, d))],
        out_shape=[jax.ShapeDtypeStruct((s, d), F32), jax.ShapeDtypeStruct((1, d), F32)],
        compiler_params=_cparams(("arbitrary",)),
    )(dh, x, g, dres)
    return (dx, dg, ex_results) if hosted else (dx, dg)


def bias_table(rel_bias_t, onehot_t, maskadd):
    n = onehot_t.shape[1]
    tn = 8192

    def body(r_ref, o_ref, m_ref, out_ref):
        out_ref[...] = _dot_hi(r_ref[...], o_ref[...]) + m_ref[...]

    return pl.pallas_call(
        body, name="bias_table", grid=(n // tn,),
        in_specs=[_full((ATT_HEADS, REL_BUCKETS)), pl.BlockSpec((REL_BUCKETS, tn), lambda j: (0, j)),
                  pl.BlockSpec((1, tn), lambda j: (0, j))],
        out_specs=pl.BlockSpec((ATT_HEADS, tn), lambda j: (0, j)),
        out_shape=jax.ShapeDtypeStruct((ATT_HEADS, n), F32),
        compiler_params=_cparams(("parallel",)),
    )(rel_bias_t, onehot_t, maskadd)


def bias_table_bwd(dbias, onehot_t):
    n = onehot_t.shape[1]
    tn = 8192

    def body(d_ref, o_ref, out_ref):
        @pl.when(pl.program_id(0) == 0)
        def _():
            out_ref[...] = jnp.zeros_like(out_ref)

        out_ref[...] += _dot_hi_nt(d_ref[...], o_ref[...])

    return pl.pallas_call(
        body, name="bias_table_bwd", grid=(n // tn,),
        in_specs=[pl.BlockSpec((ATT_HEADS, tn), lambda j: (0, j)), pl.BlockSpec((REL_BUCKETS, tn), lambda j: (0, j))],
        out_specs=_full((ATT_HEADS, REL_BUCKETS)),
        out_shape=jax.ShapeDtypeStruct((ATT_HEADS, REL_BUCKETS), F32),
        compiler_params=_cparams(("arbitrary",)),
    )(dbias, onehot_t)


def _fold(full, tri):
    return jnp.where(tri, full[BLK:2 * BLK], full[0:BLK])


def _unfold(folded, tri):
    return jnp.concatenate([jnp.where(tri, 0.0, folded), jnp.where(tri, folded, 0.0)], axis=0)


def _att_head(qt, kcat, vt_cat, bias_h, sink, tri, no_prev):
    l = _fold(_dot(kcat, qt), tri) * (HEAD_DIM ** -0.5) + bias_h
    l = jnp.where(no_prev, NEG, l)
    m = jnp.maximum(jnp.max(l, axis=0, keepdims=True), sink)
    p = jnp.exp(l - m)
    es = jnp.exp(sink - m)
    inv = 1.0 / (jnp.sum(p, axis=0, keepdims=True) + es)
    p = p * inv
    pcat = _unfold(p, tri)
    return p, pcat, es * inv, _dot(vt_cat, pcat)


def _kv_cat(kvp, kvc, g):
    lo = g * HEAD_DIM
    kt_cat = jnp.concatenate([kvp[lo:lo + HEAD_DIM], kvc[lo:lo + HEAD_DIM]], axis=1)
    vt_cat = jnp.concatenate([kvp[128 + lo:128 + lo + HEAD_DIM], kvc[128 + lo:128 + lo + HEAD_DIM]], axis=1)
    return kt_cat, vt_cat


def _tri_masks(n):
    tri = (lax.broadcasted_iota(jnp.int32, (BLK, BLK), 0) <= lax.broadcasted_iota(jnp.int32, (BLK, BLK), 1))
    return tri, (n == 0) & jnp.logical_not(tri)


def _split_hosted(refs, n_in, n_out, n_scratch, ex):
    na = ex.na if ex is not None else 0
    o = 0
    parts = []
    for cnt in (n_in, na, n_out, na, n_scratch, 3 if ex is not None else 0):
        parts.append(refs[o:o + cnt])
        o += cnt
    own_in, ex_in, own_out, ex_out, own_scr, ex_sems = parts
    return (own_in, own_out, own_scr), (ex_in, ex_out, ex_sems)


def attn_fwd(pa, bias, sinks, ex=None):
    s = pa.shape[1]
    nb = s // BLK

    def body(*refs):
        ((pa_ref, kvp_ref, bias_ref, sink_ref), (y_ref,), _), hosted = _split_hosted(refs, 4, 1, 0, ex)
        n = pl.program_id(0)
        if ex is not None:
            @pl.when(n == 0)
            def _():
                ex.start(*hosted)

            @pl.when(n == nb - 1)
            def _():
                ex.wait(*hosted)

        kvc = pa_ref[2048:2304, :]
        kvp = kvp_ref[...]
        tri, no_prev = _tri_masks(n)
        for g in range(2):
            kt_cat, vt_cat = _kv_cat(kvp, kvc, g)
            kcat = kt_cat.astype(F32).T
            for j in range(8):
                h = g * 8 + j
                rows = slice(h * 64, (h + 1) * 64)
                _, _, _, o = _att_head(pa_ref[rows, :], kcat, vt_cat, bias_ref[h], sink_ref[0:1, h:h + 1], tri, no_prev)
                z = pa_ref[1024 + h * 64:1024 + (h + 1) * 64, :].astype(F32)
                y_ref[rows, :] = (o * z * _sig(z)).astype(y_ref.dtype)

    hosted = ex is not None
    res = pl.pallas_call(
        body, name="attn_fwd_gather" if hosted else "attn_fwd", grid=(nb,),
        in_specs=[pl.BlockSpec((ATT_COLS, BLK), lambda n: (0, n)),
                  pl.BlockSpec((256, BLK), lambda n: (8, jnp.maximum(n - 1, 0))),
                  _full((ATT_HEADS, BLK, BLK)), _full((1, 128))] + (ex.in_specs if hosted else []),
        out_specs=[pl.BlockSpec((1024, BLK), lambda n: (0, n))] + (ex.out_specs if hosted else []),
        out_shape=[jax.ShapeDtypeStruct((1024, s), ACT_DTYPE)] + (ex.out_shape if hosted else []),
        scratch_shapes=ex.scratch if hosted else [],
        compiler_params=_cparams(("arbitrary",)),
    )(pa, pa, bias, sinks, *(ex.arrays if hosted else []))
    return (res[0], res[1:]) if hosted else res[0]


def attn_bwd(pa, dy, bias, sinks, dbias_in, ex=None):
    s = pa.shape[1]
    nb = s // BLK

    def body(*refs):
        ((pa_ref, kvp_ref, dy_ref, bias_ref, sink_ref, dbin_ref), (dpa_ref, dbias_ref, dsink_ref),
         (carry,)), hosted = _split_hosted(refs, 6, 3, 1, ex)
        i = pl.program_id(0)
        n = nb - 1 - i

        @pl.when(i == 0)
        def _():
            dbias_ref[...] = dbin_ref[...]
            dsink_ref[...] = jnp.zeros_like(dsink_ref)
            carry[...] = jnp.zeros_like(carry)

        if ex is not None:
            @pl.when(i == 0)
            def _():
                ex.start(*hosted)

            @pl.when(i == nb - 1)
            def _():
                ex.wait(*hosted)

        kvc = pa_ref[2048:2304, :]
        kvp = kvp_ref[...]
        tri, no_prev = _tri_masks(n)
        lane = lax.broadcasted_iota(jnp.int32, (1, 128), 1)
        dsink = jnp.zeros((1, 128), F32)
        scale = HEAD_DIM ** -0.5
        for g in range(2):
            kt_cat, vt_cat = _kv_cat(kvp, kvc, g)
            kcat = kt_cat.astype(F32).T
            vcat = vt_cat.astype(F32).T
            dk = jnp.zeros((HEAD_DIM, 2 * BLK), F32)
            dv = jnp.zeros((HEAD_DIM, 2 * BLK), F32)
            for j in range(8):
                h = g * 8 + j
                rows = slice(h * 64, (h + 1) * 64)
                qt = pa_ref[rows, :]
                p, pcat, psink, o = _att_head(qt, kcat, vt_cat, bias_ref[h], sink_ref[0:1, h:h + 1], tri, no_prev)
                z = pa_ref[1024 + h * 64:1024 + (h + 1) * 64, :].astype(F32)
                dyh = dy_ref[rows, :].astype(F32)
                sz = _sig(z)
                doh = dyh * z * sz
                dpa_ref[1024 + h * 64:1024 + (h + 1) * 64, :] = (dyh * _dsilu(z, sz) * o).astype(dpa_ref.dtype)
                delta = jnp.sum(doh * o, axis=0, keepdims=True)
                dl = p * (_fold(_dot(vcat, doh), tri) - delta)
                dsink = dsink + jnp.where(lane == h, -jnp.sum(psink * delta, axis=1, keepdims=True), 0.0)
                dbias_ref[h] += dl
                dlcat = _unfold(dl, tri)
                dpa_ref[rows, :] = (_dot(kt_cat, dlcat) * scale).astype(dpa_ref.dtype)
                dk = dk + _dot_nt(qt, dlcat) * scale
                dv = dv + _dot_nt(doh, pcat)
            for q, dkv in enumerate((dk, dv)):
                r0 = q * 128 + g * HEAD_DIM
                dpa_ref[2048 + r0:2048 + r0 + HEAD_DIM, :] = (
                    dkv[:, BLK:2 * BLK] + carry[r0:r0 + HEAD_DIM, :]).astype(dpa_ref.dtype)
                carry[r0:r0 + HEAD_DIM, :] = dkv[:, 0:BLK]
        dsink_ref[...] += dsink

    hosted = ex is not None
    res = pl.pallas_call(
        body, name="attn_bwd_scatter" if hosted else "attn_bwd", grid=(nb,),
        in_specs=[pl.BlockSpec((ATT_COLS, BLK), lambda i: (0, nb - 1 - i)),
                  pl.BlockSpec((256, BLK), lambda i: (8, jnp.maximum(nb - 2 - i, 0))),
                  pl.BlockSpec((1024, BLK), lambda i: (0, nb - 1 - i)),
                  _full((ATT_HEADS, BLK, BLK)), _full((1, 128)), _full((ATT_HEADS, BLK, BLK))]
        + (ex.in_specs if hosted else []),
        out_specs=[pl.BlockSpec((ATT_COLS, BLK), lambda i: (0, nb - 1 - i)),
                   _full((ATT_HEADS, BLK, BLK)), _full((1, 128))] + (ex.out_specs if hosted else []),
        out_shape=[jax.ShapeDtypeStruct((ATT_COLS, s), ACT_DTYPE),
                   jax.ShapeDtypeStruct((ATT_HEADS, BLK, BLK), F32),
                   jax.ShapeDtypeStruct((1, 128), F32)] + (ex.out_shape if hosted else []),
        scratch_shapes=[pltpu.VMEM((256, BLK), F32)] + (ex.scratch if hosted else []),
        compiler_params=_cparams(("arbitrary",)),
    )(pa, pa, dy, bias, sinks, dbias_in, *(ex.arrays if hosted else []))
    return (res[0], res[1], res[2], res[3:]) if hosted else tuple(res)


def _layernorm(v, g, b):
    mu = jnp.mean(v, axis=-1, keepdims=True)
    vc = v - mu
    rstd = lax.rsqrt(jnp.mean(vc * vc, axis=-1, keepdims=True) + EPS)
    xhat = vc * rstd
    return xhat, rstd, xhat * g + b


def sgu_fwd(ps, ln_g, ln_b, w_tril, b_t):
    s = ps.shape[0]

    def body(ps_ref, g_ref, b_ref, w_ref, bt_ref, y_ref):
        u = ps_ref[:, 0:1024].astype(F32)
        v = ps_ref[:, 1024:2048].astype(F32)
        z = ps_ref[:, 2048:3072].astype(F32)
        _, _, vn = _layernorm(v, g_ref[...], b_ref[...])
        gate = u * z * _sig(z)
        for g in range(SG_GROUPS):
            sl = slice(g * 128, (g + 1) * 128)
            mixed = _dot(w_ref[g], vn[:, sl]) + bt_ref[:, g:g + 1]
            y_ref[:, sl] = (gate[:, sl] * mixed).astype(y_ref.dtype)

    return pl.pallas_call(
        body, name="sgu_fwd", grid=(s // BLK,),
        in_specs=[pl.BlockSpec((BLK, SG_COLS), lambda c: (c, 0)), _full((1, 1024)), _full((1, 1024)),
                  _full((SG_GROUPS, BLK, BLK)), _full((BLK, 128))],
        out_specs=pl.BlockSpec((BLK, 1024), lambda c: (c, 0)),
        out_shape=jax.ShapeDtypeStruct((s, 1024), ACT_DTYPE),
        compiler_params=_cparams(("parallel",)),
    )(ps, ln_g, ln_b, w_tril, b_t)


def sgu_bwd(ps, dy, ln_g, ln_b, w_tril, w_tril_t, b_t):
    s = ps.shape[0]

    def body(ps_ref, dy_ref, g_ref, b_ref, w_ref, wt_ref, bt_ref, dps_ref, dw_ref, dbt_ref, dg_ref, db_ref, dvn_scr):
        @pl.when(pl.program_id(0) == 0)
        def _():
            dw_ref[...] = jnp.zeros_like(dw_ref)
            dbt_ref[...] = jnp.zeros_like(dbt_ref)
            dg_ref[...] = jnp.zeros_like(dg_ref)
            db_ref[...] = jnp.zeros_like(db_ref)

        u = ps_ref[:, 0:1024].astype(F32)
        v = ps_ref[:, 1024:2048].astype(F32)
        z = ps_ref[:, 2048:3072].astype(F32)
        dy = dy_ref[...].astype(F32)
        xhat, rstd, vn = _layernorm(v, g_ref[...], b_ref[...])
        sz = _sig(z)
        silu = z * sz
        row = lax.broadcasted_iota(jnp.int32, (BLK, BLK), 0)
        colm = lax.broadcasted_iota(jnp.int32, (BLK, BLK), 1)
        tril = row >= colm
        dbt = jnp.zeros((BLK, 128), F32)
        for g in range(SG_GROUPS):
            sl = slice(g * 128, (g + 1) * 128)
            vng = vn[:, sl]
            mixed = _dot(w_ref[g], vng) + bt_ref[:, g:g + 1]
            dyg, ug = dy[:, sl], u[:, sl]
            dps_ref[:, sl] = (dyg * mixed * silu[:, sl]).astype(dps_ref.dtype)
            dps_ref[:, 2048 + g * 128:2048 + (g + 1) * 128] = (
                dyg * ug * mixed * _dsilu(z[:, sl], sz[:, sl])).astype(dps_ref.dtype)
            dm = dyg * ug * silu[:, sl]
            dw_ref[g] += jnp.where(tril, _dot_nt(dm, vng), 0.0)
            dbt = dbt + jnp.where(colm == g, jnp.sum(dm, axis=1, keepdims=True), 0.0)
            dvn_scr[:, sl] = _dot(wt_ref[g], dm)
        dbt_ref[...] += dbt
        dvn = dvn_scr[...]
        dg_ref[...] += jnp.sum(dvn * xhat, axis=0, keepdims=True)
        db_ref[...] += jnp.sum(dvn, axis=0, keepdims=True)
        dxh = dvn * g_ref[...]
        dv = rstd * (dxh - jnp.mean(dxh, axis=-1, keepdims=True)
                     - xhat * jnp.mean(dxh * xhat, axis=-1, keepdims=True))
        dps_ref[:, 1024:2048] = dv.astype(dps_ref.dtype)

    return pl.pallas_call(
        body, name="sgu_bwd", grid=(s // BLK,),
        in_specs=[pl.BlockSpec((BLK, SG_COLS), lambda c: (c, 0)), pl.BlockSpec((BLK, 1024), lambda c: (c, 0)),
                  _full((1, 1024)), _full((1, 1024)), _full((SG_GROUPS, BLK, BLK)), _full((SG_GROUPS, BLK, BLK)),
                  _full((BLK, 128))],
        out_specs=[pl.BlockSpec((BLK, SG_COLS), lambda c: (c, 0)), _full((SG_GROUPS, BLK, BLK)), _full((BLK, 128)),
                   _full((1, 1024)), _full((1, 1024))],
        out_shape=[jax.ShapeDtypeStruct((s, SG_COLS), ACT_DTYPE), jax.ShapeDtypeStruct((SG_GROUPS, BLK, BLK), F32),
                   jax.ShapeDtypeStruct((BLK, 128), F32), jax.ShapeDtypeStruct((1, 1024), F32),
                   jax.ShapeDtypeStruct((1, 1024), F32)],
        scratch_shapes=[pltpu.VMEM((BLK, 1024), F32)],
        compiler_params=_cparams(("arbitrary",)),
    )(ps, dy, ln_g, ln_b, w_tril, w_tril_t, b_t)


def _shift_down(cur, prev16, k):
    if k == 0:
        return cur
    r = pltpu.roll(cur, k, 0)
    rp = pltpu.roll(prev16, k, 0)
    row = lax.broadcasted_iota(jnp.int32, (8, cur.shape[1]), 0)
    return jnp.concatenate([jnp.where(row < k, rp[0:8], r[0:8]), r[8:]], axis=0)


def _shift_up(cur, next16, k):
    if k == 0:
        return cur
    n = cur.shape[0]
    r = pltpu.roll(cur, n - k, 0)
    rn = pltpu.roll(next16, 16 - k, 0)
    row = lax.broadcasted_iota(jnp.int32, (8, cur.shape[1]), 0)
    return jnp.concatenate([r[:n - 8], jnp.where(row >= 8 - k, rn[8:16], r[n - 8:])], axis=0)


def _bcast8(v):
    return jnp.broadcast_to(v, (16, v.shape[1]))


class _Ssd:
    def __init__(self, xbc, prev16, dtr, cw, cbias, dtb, alog, dsk, tri, e):
        pre = cbias + cw[3:4] * xbc
        self.shifted = [xbc]
        for k in (1, 2, 3):
            sh = _shift_down(xbc, prev16, k)
            self.shifted.append(sh)
            pre = pre + cw[3 - k:4 - k] * sh
        self.pre = pre
        self.sg = _sig(pre)
        act = pre * self.sg
        self.xs = act[:, 0:SSM_WIDTH]
        self.bm = act[:, SSM_WIDTH:SSM_WIDTH + 512]
        self.cm = act[:, SSM_WIDTH + 512:CONV_DIM]
        self.dtp = dtr + dtb
        self.dt = jnp.maximum(self.dtp, 0.0) + jnp.log(1.0 + jnp.exp(-jnp.abs(self.dtp)))
        self.a = -jnp.exp(alog)
        self.acs = _dot_hi(tri, self.dt * self.a)
        self.acs_t = self.acs.T
        tot = self.acs[BLK - 1:BLK]
        self.ecs = jnp.exp(self.acs)
        self.dte = jnp.exp(tot - self.acs)
        self.cd = jnp.exp(tot)
        self.dt_x = _dot_onehot(self.dt, e)
        self.ecs_x = _dot_onehot(self.ecs, e)
        self.dte_x = _dot_onehot(self.dte, e)
        self.cd_x = _dot_onehot(_bcast8(self.cd), e)[0:1]
        self.d_x = _dot_onehot(_bcast8(dsk), e)[0:1]
        self.xdt = self.xs * self.dt_x
        row = lax.broadcasted_iota(jnp.int32, (BLK, BLK), 0)
        col = lax.broadcasted_iota(jnp.int32, (BLK, BLK), 1)
        self.tril = row >= col

    def group(self, g):
        sl = slice(g * 128, (g + 1) * 128)
        bg, cg = self.bm[:, sl], self.cm[:, sl]
        return bg, cg, _dot_nt(cg, bg)

    def decay(self, h):
        seg = self.acs[:, h:h + 1] - self.acs_t[h:h + 1, :]
        return jnp.exp(jnp.where(self.tril, seg, NEG))

    def y_pre_gate(self, ht_of, yd_scr, yoff_scr):
        for g in range(SSM_GROUPS):
            bg, cg, cb = self.group(g)
            for j in range(8):
                h = g * 8 + j
                sl = slice(h * 64, (h + 1) * 64)
                yd_scr[:, sl] = _dot(cb * self.decay(h), self.xdt[:, sl])
            gs = slice(g * SSM_GW, (g + 1) * SSM_GW)
            yoff_scr[:, gs] = _dot(cg, ht_of(g)) * self.ecs_x[:, gs]
        return yd_scr[...] + yoff_scr[...] + self.d_x * self.xs


def _ssd_consts():
    hh = lax.broadcasted_iota(jnp.int32, (128, SSM_WIDTH), 0)
    ch = lax.broadcasted_iota(jnp.int32, (128, SSM_WIDTH), 1)
    e = (ch // 64 == hh).astype(jnp.bfloat16)
    row = lax.broadcasted_iota(jnp.int32, (BLK, BLK), 0)
    col = lax.broadcasted_iota(jnp.int32, (BLK, BLK), 1)
    tri = (row >= col).astype(F32)
    return tri, e


def _pad_lanes(v, n=128):
    return jnp.pad(v, ((0, 0), (0, n - v.shape[1])))


def ssd_fwd(pm, cw, cbias, dtb, alog, dsk, ng):
    s = pm.shape[0]
    nc = s // BLK
    tri, e = _ssd_consts()

    def body(pm_ref, prev_ref, cw_ref, cb_ref, dtb_ref, al_ref, d_ref, ng_ref, tri_ref, e_ref,
             y_ref, st_ref, ht_ref, yd_scr, yoff_scr):
        c = pl.program_id(0)

        @pl.when(c == 0)
        def _():
            ht_ref[...] = jnp.zeros_like(ht_ref)

        xbc = pm_ref[:, 0:CONV_DIM].astype(F32)
        prev16 = jnp.where(c == 0, 0.0, prev_ref[...].astype(F32))
        f = _Ssd(xbc, prev16, pm_ref[:, DT_OFF:DT_OFF + 128].astype(F32), cw_ref[...], cb_ref[...], dtb_ref[...],
                 al_ref[...], d_ref[...], tri_ref[...], e_ref[...])
        st_ref[0] = ht_ref[...]
        y = f.y_pre_gate(lambda g: ht_ref[g], yd_scr, yoff_scr)
        for g in range(SSM_GROUPS):
            bg, _, _ = f.group(g)
            gs = slice(g * SSM_GW, (g + 1) * SSM_GW)
            ht_ref[g] = ht_ref[g] * f.cd_x[:, gs] + _dot_tn(bg, f.xdt[:, gs] * f.dte_x[:, gs])
        z = pm_ref[:, CONV_DIM:CONV_DIM + SSM_WIDTH].astype(F32)
        ypre = y * z * _sig(z)
        for g in range(SSM_GROUPS):
            gs = slice(g * SSM_GW, (g + 1) * SSM_GW)
            yg = ypre[:, gs]
            rr = lax.rsqrt(jnp.mean(yg * yg, axis=-1, keepdims=True) + EPS)
            y_ref[:, gs] = (yg * rr * ng_ref[:, gs]).astype(y_ref.dtype)

    return pl.pallas_call(
        body, name="ssd_fwd", grid=(nc,),
        in_specs=[pl.BlockSpec((BLK, SSM_COLS), lambda c: (c, 0)),
                  pl.BlockSpec((16, CONV_DIM), lambda c: (jnp.maximum(8 * c - 1, 0), 0)),
                  _full((4, CONV_DIM)), _full((1, CONV_DIM)), _full((1, 128)), _full((1, 128)), _full((1, 128)),
                  _full((1, SSM_WIDTH)), _full((BLK, BLK)), _full((128, SSM_WIDTH))],
        out_specs=[pl.BlockSpec((BLK, SSM_WIDTH), lambda c: (c, 0)),
                   pl.BlockSpec((1, SSM_GROUPS, 128, SSM_GW), lambda c: (c, 0, 0, 0))],
        out_shape=[jax.ShapeDtypeStruct((s, SSM_WIDTH), ACT_DTYPE),
                   jax.ShapeDtypeStruct((nc, SSM_GROUPS, 128, SSM_GW), F32)],
        scratch_shapes=[pltpu.VMEM((SSM_GROUPS, 128, SSM_GW), F32), pltpu.VMEM((BLK, SSM_WIDTH), F32),
                        pltpu.VMEM((BLK, SSM_WIDTH), F32)],
        compiler_params=_cparams(("arbitrary",)),
    )(pm, pm, cw, cbias, dtb, alog, dsk, ng, tri, e)


def ssd_bwd(pm, dy, states, cw, cbias, dtb, alog, dsk, ng):
    s = pm.shape[0]
    nc = s // BLK
    tri, e = _ssd_consts()
    tri_t, e_t = tri.T, e.T

    def body(pm_ref, prev_ref, dy_ref, st_ref, cw_ref, cb_ref, dtb_ref, al_ref, d_ref, ng_ref,
             tri_ref, trit_ref, e_ref, et_ref,
             dpm_ref, dcw_ref, dcb_ref, dvec_ref, dng_ref,
             dht_ref, dcar_ref, yd_scr, yoff_scr, dx_scr, r2_scr, hs_scr, da_scr, dat_scr, dd_scr, dbc_scr):
        i = pl.program_id(0)
        n = nc - 1 - i

        @pl.when(i == 0)
        def _():
            dht_ref[...] = jnp.zeros_like(dht_ref)
            dcar_ref[...] = jnp.zeros_like(dcar_ref)
            dcw_ref[...] = jnp.zeros_like(dcw_ref)
            dcb_ref[...] = jnp.zeros_like(dcb_ref)
            dvec_ref[...] = jnp.zeros_like(dvec_ref)
            dng_ref[...] = jnp.zeros_like(dng_ref)
            dd_scr[...] = jnp.zeros_like(dd_scr)
            da_scr[...] = jnp.zeros_like(da_scr)
            dat_scr[...] = jnp.zeros_like(dat_scr)

        xbc = pm_ref[:, 0:CONV_DIM].astype(F32)
        prev16 = jnp.where(n == 0, 0.0, prev_ref[...].astype(F32))
        cw = cw_ref[...]
        f = _Ssd(xbc, prev16, pm_ref[:, DT_OFF:DT_OFF + 128].astype(F32), cw, cb_ref[...], dtb_ref[...],
                 al_ref[...], d_ref[...], tri_ref[...], e_ref[...])
        et = et_ref[...]
        y = f.y_pre_gate(lambda g: st_ref[0, g], yd_scr, yoff_scr)

        z = pm_ref[:, CONV_DIM:CONV_DIM + SSM_WIDTH].astype(F32)
        dyv = dy_ref[...].astype(F32)
        sz = _sig(z)
        silu = z * sz
        ypre = y * silu
        for g in range(SSM_GROUPS):
            gs = slice(g * SSM_GW, (g + 1) * SSM_GW)
            yg = ypre[:, gs]
            rr = lax.rsqrt(jnp.mean(yg * yg, axis=-1, keepdims=True) + EPS)
            nrm = yg * rr
            dng_ref[:, gs] += jnp.sum(dyv[:, gs] * nrm, axis=0, keepdims=True)
            dn = dyv[:, gs] * ng_ref[:, gs]
            dx_scr[:, gs] = rr * (dn - nrm * jnp.mean(dn * nrm, axis=-1, keepdims=True))
        dypre = dx_scr[...]
        d_y = dypre * silu
        dpm_ref[:, CONV_DIM:CONV_DIM + SSM_WIDTH] = (dypre * y * _dsilu(z, sz)).astype(dpm_ref.dtype)

        for g in range(SSM_GROUPS):
            bg, cg, cb = f.group(g)
            gs = slice(g * SSM_GW, (g + 1) * SSM_GW)
            htg = st_ref[0, g]
            dhn = dht_ref[g]
            dcb = jnp.zeros((BLK, BLK), F32)
            for j in range(8):
                h = g * 8 + j
                sl = slice(h * 64, (h + 1) * 64)
                dec = f.decay(h)
                dyh = d_y[:, sl]
                dmd = _dot_nt(dyh, f.xdt[:, sl]) * dec
                dcb = dcb + dmd
                gm = dmd * cb
                da_scr[:, h:h + 1] = jnp.sum(gm, axis=1, keepdims=True)
                dat_scr[h:h + 1, :] = jnp.sum(gm, axis=0, keepdims=True)
                dx_scr[:, sl] = _dot_tn(cb * dec, dyh)
            dz = f.ecs_x[:, gs] * d_y[:, gs]
            dbc_scr[:, 512 + g * 128:512 + (g + 1) * 128] = _dot(dcb, bg) + _dot_nt(dz, htg)
            dbc_scr[:, g * 128:(g + 1) * 128] = _dot_tn(dcb, cg) + _dot_nt(f.xdt[:, gs] * f.dte_x[:, gs], dhn)
            dws = _dot(bg, dhn)
            dx_scr[:, gs] += f.dte_x[:, gs] * dws
            r2_scr[:, gs] = dws * f.xdt[:, gs]
            hs_scr[:, gs] = _bcast8(jnp.sum(dhn * htg, axis=0, keepdims=True))
            dht_ref[g] = f.cd_x[:, gs] * dhn + _dot_tn(cg, dz)
        d_x = dx_scr[...]
        r1 = _dot_onehot(d_y * yoff_scr[...], et)
        r2 = _dot_onehot(r2_scr[...], et) * f.dte
        dcd = _dot_onehot(hs_scr[...], et)[0:1]
        d_tot = jnp.sum(r2, axis=0, keepdims=True) + dcd * f.cd
        row = lax.broadcasted_iota(jnp.int32, (BLK, 128), 0)
        d_a = da_scr[...] - dat_scr[...].T + r1 - r2 + jnp.where(row == BLK - 1, d_tot, 0.0)
        dadt = _dot_hi(trit_ref[...], d_a)
        ddt = dadt * f.a + _dot_onehot(d_x * f.xs, et)
        lane = lax.broadcasted_iota(jnp.int32, (BLK, 128), 1)
        dr = jnp.where(lane < SSM_HEADS, ddt * _sig(f.dtp), 0.0)
        dvec_ref[0:1, :] += jnp.sum(dr, axis=0, keepdims=True)
        dvec_ref[1:2, :] += jnp.sum(dadt * f.dt, axis=0, keepdims=True) * f.a
        dd_scr[...] += _bcast8(jnp.sum(d_y * f.xs, axis=0, keepdims=True))
        dpm_ref[:, DT_OFF:DT_OFF + 128] = dr.astype(dpm_ref.dtype)
        dpm_ref[:, DT_OFF + 128:SSM_COLS] = jnp.zeros((BLK, 128), dpm_ref.dtype)

        dxs = d_x * f.dt_x + f.d_x * d_y
        dact = jnp.concatenate([dxs, dbc_scr[...]], axis=1)
        dpre = dact * _dsilu(f.pre, f.sg)
        dcb_ref[...] += jnp.sum(dpre, axis=0, keepdims=True)
        dxraw = jnp.zeros((BLK, CONV_DIM), F32)
        nxt = dcar_ref[...]
        for k in range(4):
            dcw_ref[3 - k:4 - k, :] += jnp.sum(dpre * f.shifted[k], axis=0, keepdims=True)
            dxraw = dxraw + cw[3 - k:4 - k] * _shift_up(dpre, nxt, k)
        dcar_ref[...] = dpre[0:16]
        dpm_ref[:, 0:CONV_DIM] = dxraw.astype(dpm_ref.dtype)

        @pl.when(i == nc - 1)
        def _():
            dvec_ref[2:3, :] = _dot_onehot(dd_scr[...], et)[0:1]

    return pl.pallas_call(
        body, name="ssd_bwd", grid=(nc,),
        in_specs=[pl.BlockSpec((BLK, SSM_COLS), lambda i: (nc - 1 - i, 0)),
                  pl.BlockSpec((16, CONV_DIM), lambda i: (jnp.maximum(8 * (nc - 1 - i) - 1, 0), 0)),
                  pl.BlockSpec((BLK, SSM_WIDTH), lambda i: (nc - 1 - i, 0)),
                  pl.BlockSpec((1, SSM_GROUPS, 128, SSM_GW), lambda i: (nc - 1 - i, 0, 0, 0)),
                  _full((4, CONV_DIM)), _full((1, CONV_DIM)), _full((1, 128)), _full((1, 128)), _full((1, 128)),
                  _full((1, SSM_WIDTH)), _full((BLK, BLK)), _full((BLK, BLK)), _full((128, SSM_WIDTH)),
                  _full((SSM_WIDTH, 128))],
        out_specs=[pl.BlockSpec((BLK, SSM_COLS), lambda i: (nc - 1 - i, 0)),
                   _full((8, CONV_DIM)), _full((1, CONV_DIM)), _full((8, 128)), _full((1, SSM_WIDTH))],
        out_shape=[jax.ShapeDtypeStruct((s, SSM_COLS), ACT_DTYPE), jax.ShapeDtypeStruct((8, CONV_DIM), F32),
                   jax.ShapeDtypeStruct((1, CONV_DIM), F32), jax.ShapeDtypeStruct((8, 128), F32),
                   jax.ShapeDtypeStruct((1, SSM_WIDTH), F32)],
        scratch_shapes=[pltpu.VMEM((SSM_GROUPS, 128, SSM_GW), F32), pltpu.VMEM((16, CONV_DIM), F32),
                        pltpu.VMEM((BLK, SSM_WIDTH), F32), pltpu.VMEM((BLK, SSM_WIDTH), F32),
                        pltpu.VMEM((BLK, SSM_WIDTH), F32), pltpu.VMEM((BLK, SSM_WIDTH), F32),
                        pltpu.VMEM((16, SSM_WIDTH), F32), pltpu.VMEM((BLK, 128), F32), pltpu.VMEM((128, BLK), F32),
                        pltpu.VMEM((16, SSM_WIDTH), F32), pltpu.VMEM((BLK, 1024), F32)],
        compiler_params=_cparams(("arbitrary",)),
    )(pm, pm, dy, states, cw, cbias, dtb, alog, dsk, ng, tri, tri_t, e, e_t)


def merge_fwd(x, ya, ys, ym, pg, wa, ws, wm, wo, g_post):
    s, d = x.shape
    tm = min(256, s)

    def body(x_ref, ya_ref, ys_ref, ym_ref, pg_ref, wa_ref, ws_ref, wm_ref, wo_ref, g_ref,
             xo_ref, ba_ref, bs_ref, bm_ref, mg_ref, out_ref):
        ba = _dot(ya_ref[...], wa_ref[...])
        bs = _dot(ys_ref[...], ws_ref[...])
        bm = _dot(ym_ref[...], wm_ref[...])
        merged = (_sig(pg_ref[:, 0:d].astype(F32)) * ba + _sig(pg_ref[:, d:2 * d].astype(F32)) * bs
                  + _sig(pg_ref[:, 2 * d:3 * d].astype(F32)) * bm)
        out = _dot(merged, wo_ref[...])
        r = lax.rsqrt(jnp.mean(out * out, axis=-1, keepdims=True) + EPS)
        xo_ref[...] = x_ref[...] + out * r * g_ref[...]
        ba_ref[...] = ba.astype(ba_ref.dtype)
        bs_ref[...] = bs.astype(bs_ref.dtype)
        bm_ref[...] = bm.astype(bm_ref.dtype)
        mg_ref[...] = merged.astype(mg_ref.dtype)
        out_ref[...] = out.astype(out_ref.dtype)

    rows = lambda w: pl.BlockSpec((tm, w), lambda i: (i, 0))
    act = jax.ShapeDtypeStruct((s, d), ACT_DTYPE)
    return pl.pallas_call(
        body, name="merge_fwd", grid=(s // tm,),
        in_specs=[rows(d), rows(d), rows(d), rows(2 * d), rows(3 * d), _full((d, d)), _full((d, d)),
                  _full((2 * d, d)), _full((d, d)), _full((1, d))],
        out_specs=[rows(d)] * 6,
        out_shape=[jax.ShapeDtypeStruct((s, d), F32), act, act, act, act, act],
        compiler_params=_cparams(("parallel",)),
    )(x, ya, ys, ym, pg, wa, ws, wm, wo, g_post)


def merge_bwd(dx, out_s, pg, ba, bs, bm, wa, ws, wm, wo, g_post):
    s, d = dx.shape
    tm = min(256, s)

    def body(dx_ref, out_ref, pg_ref, ba_ref, bs_ref, bm_ref, wa_ref, ws_ref, wm_ref, wo_ref, g_ref,
             dout_ref, dba_ref, dbs_ref, dbm_ref, dpg_ref, dya_ref, dys_ref, dym_ref, dg_ref):
        @pl.when(pl.program_id(0) == 0)
        def _():
            dg_ref[...] = jnp.zeros_like(dg_ref)

        o = out_ref[...].astype(F32)
        dxv = dx_ref[...]
        r = lax.rsqrt(jnp.mean(o * o, axis=-1, keepdims=True) + EPS)
        nrm = o * r
        dg_ref[...] += jnp.sum(dxv * nrm, axis=0, keepdims=True)
        dn = dxv * g_ref[...]
        dout = r * (dn - nrm * jnp.mean(dn * nrm, axis=-1, keepdims=True))
        dout_ref[...] = dout.astype(dout_ref.dtype)
        dmerged = _dot_nt(dout, wo_ref[...])
        for q, (b_ref, db_ref, w_ref, dy_ref) in enumerate(((ba_ref, dba_ref, wa_ref, dya_ref),
                                                            (bs_ref, dbs_ref, ws_ref, dys_ref),
                                                            (bm_ref, dbm_ref, wm_ref, dym_ref))):
            gt = _sig(pg_ref[:, q * d:(q + 1) * d].astype(F32))
            db = dmerged * gt
            db_ref[...] = db.astype(db_ref.dtype)
            dpg_ref[:, q * d:(q + 1) * d] = (dmerged * b_ref[...].astype(F32) * gt * (1.0 - gt)).astype(dpg_ref.dtype)
            dy_ref[...] = _dot_nt(db, w_ref[...]).astype(dy_ref.dtype)

    rows = lambda w: pl.BlockSpec((tm, w), lambda i: (i, 0))
    act = lambda w: jax.ShapeDtypeStruct((s, w), ACT_DTYPE)
    return pl.pallas_call(
        body, name="merge_bwd", grid=(s // tm,),
        in_specs=[rows(d), rows(d), rows(3 * d), rows(d), rows(d), rows(d), _full((d, d)), _full((d, d)),
                  _full((2 * d, d)), _full((d, d)), _full((1, d))],
        out_specs=[rows(d), rows(d), rows(d), rows(d), rows(3 * d), rows(d), rows(d), rows(2 * d), _full((1, d))],
        out_shape=[act(d), act(d), act(d), act(d), act(3 * d), act(d), act(d), act(2 * d),
                   jax.ShapeDtypeStruct((1, d), F32)],
        compiler_params=_cparams(("arbitrary",)),
    )(dx, out_s, pg, ba, bs, bm, wa, ws, wm, wo, g_post)


def loss_grad(y, target):
    s, d = y.shape
    tm = min(512, s)

    def body(y_ref, t_ref, dy_ref, l_ref):
        @pl.when(pl.program_id(0) == 0)
        def _():
            l_ref[...] = jnp.zeros_like(l_ref)

        err = y_ref[...] - t_ref[...]
        dy_ref[...] = err * (1.0 / d)
        part = jnp.sum(jnp.sum(err * err, axis=-1, keepdims=True) * (1.0 / d), axis=0, keepdims=True)
        l_ref[...] += 0.5 * jnp.broadcast_to(part, l_ref.shape)

    return pl.pallas_call(
        body, name="loss_grad", grid=(s // tm,),
        in_specs=[pl.BlockSpec((tm, d), lambda i: (i, 0)), pl.BlockSpec((tm, d), lambda i: (i, 0))],
        out_specs=[pl.BlockSpec((tm, d), lambda i: (i, 0)), _full((8, 128))],
        out_shape=[jax.ShapeDtypeStruct((s, d), F32), jax.ShapeDtypeStruct((8, 128), F32)],
        compiler_params=_cparams(("arbitrary",)),
    )(y, target)


def _mesh_pos():
    x, y, c = lax.axis_index("x"), lax.axis_index("y"), lax.axis_index("c")
    return x, y, c, 4 * x + 2 * y + c


def _peer(x, y, c, k):
    px = 1 - x if k & 4 else x
    py = 1 - y if k & 2 else y
    pc = 1 - c if k & 1 else c
    return (px, py, pc), 4 * px + 2 * py + pc


class Exchange:
    def __init__(self, scattered, gathered):
        self.ns = len(scattered)
        self.arrays = list(scattered) + list(gathered)
        self.na = len(self.arrays)
        any_spec = pl.BlockSpec(memory_space=pl.ANY)
        self.in_specs = [any_spec] * self.na
        self.out_specs = [any_spec] * self.na
        self.out_shape = ([jax.ShapeDtypeStruct(a.shape, a.dtype) for a in scattered]
                          + [jax.ShapeDtypeStruct((N_DEV,) + a.shape, a.dtype) for a in gathered])
        self.scratch = [pltpu.SemaphoreType.DMA((self.na, N_DEV - 1)), pltpu.SemaphoreType.DMA((self.na, N_DEV - 1)),
                        pltpu.SemaphoreType.DMA((self.na,))]

    def _src(self, ins, q, slot):
        return ins[q].at[slot] if q < self.ns else ins[q]

    def _local(self, ins, outs, sems):
        me = _mesh_pos()[3]
        return [pltpu.make_async_copy(self._src(ins, q, me), outs[q].at[me], sems[2].at[q]) for q in range(self.na)]

    def _remote(self, ins, outs, sems, incoming):
        x, y, c, me = _mesh_pos()
        copies = []
        for k in range(1, N_DEV):
            peer, pidx = _peer(x, y, c, k)
            for q in range(self.na):
                copies.append(pltpu.make_async_remote_copy(
                    src_ref=self._src(ins, q, pidx), dst_ref=outs[q].at[pidx if incoming else me],
                    send_sem=sems[0].at[q, k - 1], recv_sem=sems[1].at[q, k - 1], device_id=peer,
                    device_id_type=MESH))
        return copies

    def start(self, ins, outs, sems):
        for cp in self._local(ins, outs, sems) + self._remote(ins, outs, sems, incoming=False):
            cp.start()

    def wait(self, ins, outs, sems):
        for cp in self._remote(ins, outs, sems, incoming=True):
            cp.wait_recv()
        for cp in self._remote(ins, outs, sems, incoming=False):
            cp.wait_send()
        for cp in self._local(ins, outs, sems):
            cp.wait()


def exchange(scattered, gathered, name):
    ex = Exchange(scattered, gathered)

    def body(*refs):
        ins, outs, sems = refs[:ex.na], refs[ex.na:2 * ex.na], refs[2 * ex.na:]
        ex.start(ins, outs, sems)
        ex.wait(ins, outs, sems)

    return pl.pallas_call(body, name=name, in_specs=ex.in_specs, out_specs=ex.out_specs, out_shape=ex.out_shape,
                          scratch_shapes=ex.scratch)(*ex.arrays)


def adamw(parts, w, m, v, tile, name):
    npart, r, dp = parts.shape
    d = w.shape[1]

    def body(p_ref, w_ref, m_ref, v_ref, g_ref, dw_ref, nm_ref, nv_ref):
        g = p_ref[0, :, 0:d].astype(F32)
        for q in range(1, npart):
            g = g + p_ref[q, :, 0:d].astype(F32)
        g_ref[...] = g
        nm = ADAM_B1 * m_ref[...] + (1.0 - ADAM_B1) * g
        nv = ADAM_B2 * v_ref[...] + (1.0 - ADAM_B2) * (g * g)
        nm_ref[...] = nm
        nv_ref[...] = nv
        m_hat = nm / (1.0 - ADAM_B1 ** ADAM_STEP)
        v_hat = nv / (1.0 - ADAM_B2 ** ADAM_STEP)
        dw_ref[...] = -ADAM_LR * (m_hat / (jnp.sqrt(v_hat) + ADAM_EPS) + ADAM_WD * w_ref[...])

    rows = pl.BlockSpec((tile, d), lambda i: (i, 0))
    out = jax.ShapeDtypeStruct((r, d), F32)
    return pl.pallas_call(
        body, name=name, grid=(r // tile,),
        in_specs=[pl.BlockSpec((npart, tile, dp), lambda i: (0, i, 0)), rows, rows, rows],
        out_specs=[rows] * 4, out_shape=[out] * 4,
        compiler_params=_cparams(("parallel",)),
    )(parts, w, m, v)


def _pad_rows(a, rows):
    return jnp.pad(a, ((0, rows - a.shape[0]), (0, 0)))


def _pack_rest(w_att, w_sg, w_ssm, w_out):
    parts = []
    for l in range(2):
        parts += [w_att[l], w_sg[l], w_ssm[l], w_out[l]]
    return jnp.concatenate(parts, axis=0)


def _unpack_rest(p):
    outs = [[], [], [], []]
    o = 0
    for l in range(2):
        for q, rws in enumerate(REST_PARTS):
            outs[q].append(p[o:o + rws])
            o += rws
    return [jnp.stack(t) for t in outs]


def _pack_win(w_in):
    return jnp.pad(w_in.reshape(2 * D_MODEL, WIN_SHARD), ((0, 0), (0, WIN_LANES - WIN_SHARD)))


W_IN_MAP = ((0, 1024, "att", 0), (1024, 1280, "att", 2048), (1280, 2304, "att", 1024), (2304, 5376, "sg", 0),
            (5376, 7424, "ssm", 3072), (7424, 10496, "ssm", 0), (10496, 10528, "ssm", 5120), (10528, 13600, "gate", 0))
SLAB_COLS = {"att": ATT_COLS, "sg": SG_COLS, "ssm": SSM_COLS, "gate": GATE_COLS}


def _slabs_from_shards(g):
    slabs = {}
    for name, width in SLAB_COLS.items():
        pieces, filled = [], 0
        for ga, gb, _, off in sorted((m for m in W_IN_MAP if m[2] == name), key=lambda m: m[3]):
            assert off == filled
            a = ga
            while a < gb:
                d = a // WIN_SHARD
                hi = min(gb, WIN_SHARD * (d + 1))
                pieces.append(g[d, :, a - WIN_SHARD * d:hi - WIN_SHARD * d])
                a = hi
            filled += gb - ga
        if filled < width:
            pieces.append(jnp.zeros((D_MODEL, width - filled), g.dtype))
        slabs[name] = jnp.concatenate(pieces, axis=1)
    return slabs


def _shards_from_slabs(dslabs):
    out = []
    for d in range(N_DEV):
        a, b = WIN_SHARD * d, WIN_SHARD * (d + 1)
        pieces = []
        for ga, gb, name, off in W_IN_MAP:
            lo, hi = max(a, ga), min(b, gb)
            if lo < hi:
                pieces.append(dslabs[name][:, off + lo - ga:off + hi - ga])
        pieces.append(jnp.zeros((D_MODEL, WIN_LANES - WIN_SHARD), pieces[0].dtype))
        out.append(jnp.concatenate(pieces, axis=1).astype(WIRE_DTYPE))
    return jnp.stack(out)


SMALL_SIZES = (("norm_pre", 2048), ("norm_post", 2048), ("rel_bias", 512), ("att_sinks", 32), ("sg_ln_g", 2048),
               ("sg_ln_b", 2048), ("sg_w", 262144), ("sg_b", 2048), ("ssm_conv_b", 6144), ("ssm_dt_bias", 64),
               ("ssm_a_log", 64), ("ssm_d", 64), ("ssm_norm_g", 4096), ("conv_w_full", 24576))


def _pack_small(d):
    parts = []
    for name, size in SMALL_SIZES:
        rows = 8 * (-(-size // (8 * D_MODEL)))
        flat = d[name].reshape(-1) if name in d else jnp.zeros((size,), F32)
        parts.append(jnp.pad(flat, (0, rows * D_MODEL - size)).reshape(rows, D_MODEL))
    return _pad_rows(jnp.concatenate(parts, axis=0), SMALL_ROWS)


def _unpack_small(p, shapes):
    out, o = {}, 0
    for name, size in SMALL_SIZES:
        rows = 8 * (-(-size // (8 * D_MODEL)))
        if name in shapes:
            out[name] = p[o:o + rows].reshape(-1)[:size].reshape(shapes[name])
        o += rows
    return out


def _bucket_onehot_t():
    qi = jnp.arange(BLK, dtype=jnp.int32)[None, :]
    kj = jnp.arange(BLK, dtype=jnp.int32)[:, None]
    dd = (qi - kj) & (BLK - 1)
    in_window = dd >= 0
    max_exact = REL_BUCKETS // 2
    dist_f = jnp.maximum(dd, 1).astype(F32)
    large = max_exact + (jnp.log(dist_f / max_exact) / math.log(128 / max_exact)
                         * (REL_BUCKETS - max_exact)).astype(jnp.int32)
    large = jnp.minimum(large, REL_BUCKETS - 1)
    bucket = jnp.where(dd < max_exact, dd, large).reshape(1, -1)
    onehot_t = (bucket == jnp.arange(REL_BUCKETS, dtype=jnp.int32)[:, None]).astype(F32)
    maskadd = jnp.where(in_window, 0.0, NEG).astype(F32).reshape(1, -1)
    return onehot_t, maskadd


WEIGHTS = ['w_in', 'norm_pre', 'norm_post', 'rel_bias', 'att_sinks', 'sg_ln_g', 'sg_ln_b', 'sg_w', 'sg_b',
           'ssm_conv_w', 'ssm_conv_b', 'ssm_dt_bias', 'ssm_a_log', 'ssm_d', 'ssm_norm_g',
           'w_br_att', 'w_br_sg', 'w_br_ssm', 'w_out']
REST = ('w_br_att', 'w_br_sg', 'w_br_ssm', 'w_out')


def kernel(x, w_in, norm_pre, norm_post, rel_bias, att_sinks, sg_ln_g, sg_ln_b, sg_w, sg_b, ssm_conv_w, ssm_conv_b, ssm_dt_bias, ssm_a_log, ssm_d, ssm_norm_g, w_br_att, w_br_sg, w_br_ssm, w_out, loss_target, m_w_in, m_norm_pre, m_norm_post, m_rel_bias, m_att_sinks, m_sg_ln_g, m_sg_ln_b, m_sg_w, m_sg_b, m_ssm_conv_w, m_ssm_conv_b, m_ssm_dt_bias, m_ssm_a_log, m_ssm_d, m_ssm_norm_g, m_w_br_att, m_w_br_sg, m_w_br_ssm, m_w_out, v_w_in, v_norm_pre, v_norm_post, v_rel_bias, v_att_sinks, v_sg_ln_g, v_sg_ln_b, v_sg_w, v_sg_b, v_ssm_conv_w, v_ssm_conv_b, v_ssm_dt_bias, v_ssm_a_log, v_ssm_d, v_ssm_norm_g, v_w_br_att, v_w_br_sg, v_w_br_ssm, v_w_out):
    w = dict(w_in=w_in, norm_pre=norm_pre, norm_post=norm_post, rel_bias=rel_bias, att_sinks=att_sinks,
             sg_ln_g=sg_ln_g, sg_ln_b=sg_ln_b, sg_w=sg_w, sg_b=sg_b, ssm_conv_w=ssm_conv_w, ssm_conv_b=ssm_conv_b,
             ssm_dt_bias=ssm_dt_bias, ssm_a_log=ssm_a_log, ssm_d=ssm_d, ssm_norm_g=ssm_norm_g,
             w_br_att=w_br_att, w_br_sg=w_br_sg, w_br_ssm=w_br_ssm, w_out=w_out)
    mom = dict(w_in=m_w_in, norm_pre=m_norm_pre, norm_post=m_norm_post, rel_bias=m_rel_bias, att_sinks=m_att_sinks,
               sg_ln_g=m_sg_ln_g, sg_ln_b=m_sg_ln_b, sg_w=m_sg_w, sg_b=m_sg_b, ssm_conv_w=m_ssm_conv_w,
               ssm_conv_b=m_ssm_conv_b, ssm_dt_bias=m_ssm_dt_bias, ssm_a_log=m_ssm_a_log, ssm_d=m_ssm_d,
               ssm_norm_g=m_ssm_norm_g, w_br_att=m_w_br_att, w_br_sg=m_w_br_sg, w_br_ssm=m_w_br_ssm, w_out=m_w_out)
    var = dict(w_in=v_w_in, norm_pre=v_norm_pre, norm_post=v_norm_post, rel_bias=v_rel_bias, att_sinks=v_att_sinks,
               sg_ln_g=v_sg_ln_g, sg_ln_b=v_sg_ln_b, sg_w=v_sg_w, sg_b=v_sg_b, ssm_conv_w=v_ssm_conv_w,
               ssm_conv_b=v_ssm_conv_b, ssm_dt_bias=v_ssm_dt_bias, ssm_a_log=v_ssm_a_log, ssm_d=v_ssm_d,
               ssm_norm_g=v_ssm_norm_g, w_br_att=v_w_br_att, w_br_sg=v_w_br_sg, w_br_ssm=v_w_br_ssm, w_out=v_w_out)
    xs0 = x[0]
    target = loss_target[0]
    my_dev = 4 * lax.axis_index("x") + 2 * lax.axis_index("y") + lax.axis_index("c")

    conv_shard = _pad_rows(ssm_conv_w.reshape(-1, D_MODEL), 8)
    win_shard = _pack_win(w_in).astype(WIRE_DTYPE)
    rest_shard = _pack_rest(*[w[n] for n in REST]).astype(WIRE_DTYPE)
    layer_shards = [[win_shard[l * D_MODEL:(l + 1) * D_MODEL], rest_shard[l * LAYER_REST:(l + 1) * LAYER_REST]]
                    for l in range(2)]
    g_win0, g_rest0, gathered_conv = exchange([], layer_shards[0] + [conv_shard], "all_gather")
    conv_full = gathered_conv[:, 0:3].reshape(N_DEV, 2, 4, 384).transpose(1, 2, 0, 3).reshape(2, 4, CONV_DIM)

    def layer_weights(l, g_win, g_rest):
        slabs = _slabs_from_shards(g_win)
        lw = {"in_" + name: slab.astype(MXU_DTYPE) for name, slab in slabs.items()}
        o = 0
        for name, rws in zip(("att", "sg", "ssm", "out"), REST_PARTS):
            lw[name] = g_rest[:, o:o + rws].reshape(N_DEV * rws, D_MODEL).astype(MXU_DTYPE)
            o += rws
        tril = jnp.tril(jnp.ones((BLK, BLK), bool))
        sgw = jnp.where(tril[None], sg_w[l], 0.0)
        lw.update(
            g_pre=norm_pre[l][None], g_post=norm_post[l][None], sinks=_pad_lanes(att_sinks[l][None]),
            ln_g=sg_ln_g[l][None], ln_b=sg_ln_b[l][None], sgw=sgw.astype(MXU_DTYPE),
            sgw_t=sgw.transpose(0, 2, 1).astype(MXU_DTYPE), sgb_t=_pad_lanes(sg_b[l].T),
            cw=conv_full[l], cb=ssm_conv_b[l][None], dtb=_pad_lanes(ssm_dt_bias[l][None]),
            alog=_pad_lanes(ssm_a_log[l][None]), dsk=_pad_lanes(ssm_d[l][None]), ng=ssm_norm_g[l][None])
        return lw

    onehot_t, maskadd = _bucket_onehot_t()
    bias = bias_table(rel_bias.T, onehot_t, maskadd).reshape(ATT_HEADS, BLK, BLK)

    saved = []
    xl = xs0
    layers = [layer_weights(0, g_win0, g_rest0)]
    for l in range(2):
        lw = layers[l]
        h = rmsnorm_fwd(xl, lw["g_pre"])
        pa = mm_nn(h, lw["in_att"], 1152, "proj_att").T
        ps = mm_nn(h, lw["in_sg"], 1536, "proj_sg")
        pm = mm_nn(h, lw["in_ssm"], 1792, "proj_ssm")
        pg = mm_nn(h, lw["in_gate"], 1536, "proj_gate")
        if l == 0:
            ya_t, (g_win1, g_rest1) = attn_fwd(pa, bias, lw["sinks"], Exchange([], layer_shards[1]))
            layers.append(layer_weights(1, g_win1, g_rest1))
        else:
            ya_t = attn_fwd(pa, bias, lw["sinks"])
        ya = ya_t.T
        ys = sgu_fwd(ps, lw["ln_g"], lw["ln_b"], lw["sgw"], lw["sgb_t"])
        ym, states = ssd_fwd(pm, lw["cw"], lw["cb"], lw["dtb"], lw["alog"], lw["dsk"], lw["ng"])
        x_next, ba, bs, bm, merged, out_s = merge_fwd(xl, ya, ys, ym, pg, lw["att"], lw["sg"], lw["ssm"], lw["out"],
                                                      lw["g_post"])
        saved.append(dict(x=xl, h=h, pa=pa, ps=ps, pm=pm, pg=pg, ya=ya, ys=ys, ym=ym, states=states, ba=ba, bs=bs,
                          bm=bm, merged=merged, out_s=out_s))
        xl = x_next

    dx, loss_part = loss_grad(xl, target)
    loss = lax.psum(loss_part[0, 0], ("x", "y", "c"))

    dbias = jnp.zeros((ATT_HEADS, BLK, BLK), F32)
    win_grads, rest_grads = [None, None], [None, None]
    small = {n: [None, None] for n in ("norm_pre", "norm_post", "att_sinks", "sg_ln_g", "sg_ln_b", "sg_w", "sg_b",
                                       "ssm_conv_b", "ssm_dt_bias", "ssm_a_log", "ssm_d", "ssm_norm_g",
                                       "conv_w_full")}
    for l in (1, 0):
        lw, sv = layers[l], saved[l]
        dout, dba, dbs, dbm, dpg, dya, dys, dym, dg_post = merge_bwd(
            dx, sv["out_s"], sv["pg"], sv["ba"], sv["bs"], sv["bm"], lw["att"], lw["sg"], lw["ssm"], lw["out"],
            lw["g_post"])
        dw_out = mm_tn(sv["merged"], dout, 1024, "dw_out")
        dw_att = mm_tn(sv["ya"], dba, 1024, "dw_br_att")
        dw_sg = mm_tn(sv["ys"], dbs, 1024, "dw_br_sg")
        dw_ssm = mm_tn(sv["ym"], dbm, 1024, "dw_br_ssm")
        rest_grads[l] = jnp.concatenate(
            [dw_att.reshape(N_DEV, 128, D_MODEL), dw_sg.reshape(N_DEV, 128, D_MODEL),
             dw_ssm.reshape(N_DEV, 256, D_MODEL), dw_out.reshape(N_DEV, 128, D_MODEL)], axis=1).astype(WIRE_DTYPE)
        if l == 0:
            dpa_t, dbias, dsinks, (recv_win1, recv_rest1, recv_rest0) = attn_bwd(
                sv["pa"], dya.T, bias, lw["sinks"], dbias,
                Exchange([win_grads[1], rest_grads[1], rest_grads[0]], []))
        else:
            dpa_t, dbias, dsinks = attn_bwd(sv["pa"], dya.T, bias, lw["sinks"], dbias)
        dpa = dpa_t.T
        dps, dsgw, dsgb_t, dln_g, dln_b = sgu_bwd(sv["ps"], dys, lw["ln_g"], lw["ln_b"], lw["sgw"], lw["sgw_t"],
                                                  lw["sgb_t"])
        dpm, dcw, dcb, dvec, dng = ssd_bwd(sv["pm"], dym, sv["states"], lw["cw"], lw["cb"], lw["dtb"], lw["alog"],
                                           lw["dsk"], lw["ng"])
        dslabs = dict(att=mm_tn(sv["h"], dpa, 2304, "dw_in_att"), sg=mm_tn(sv["h"], dps, 3072, "dw_in_sg"),
                      ssm=mm_tn(sv["h"], dpm, 2688, "dw_in_ssm"), gate=mm_tn(sv["h"], dpg, 3072, "dw_in_gate"))
        win_grads[l] = _shards_from_slabs(dslabs)
        dh_args = ([dpa, dps, dpm, dpg], [lw["in_att"], lw["in_sg"], lw["in_ssm"], lw["in_gate"]], sv["x"],
                   lw["g_pre"], dx)
        if l == 0:
            dx, dg_pre, (recv_win0,) = dh_norm_bwd(*dh_args, Exchange([win_grads[0]], []))
        else:
            dx, dg_pre = dh_norm_bwd(*dh_args)
        small["norm_pre"][l] = dg_pre[0]
        small["norm_post"][l] = dg_post[0]
        small["att_sinks"][l] = dsinks[0, :ATT_HEADS]
        small["sg_ln_g"][l] = dln_g[0]
        small["sg_ln_b"][l] = dln_b[0]
        small["sg_w"][l] = dsgw
        small["sg_b"][l] = dsgb_t[:, :SG_GROUPS].T
        small["ssm_conv_b"][l] = dcb[0]
        small["ssm_dt_bias"][l] = dvec[0, :SSM_HEADS]
        small["ssm_a_log"][l] = dvec[1, :SSM_HEADS]
        small["ssm_d"][l] = dvec[2, :SSM_HEADS]
        small["ssm_norm_g"][l] = dng[0]
        small["conv_w_full"][l] = dcw[0:4]
    grad_x = dx
    d_rel_bias = bias_table_bwd(dbias.reshape(ATT_HEADS, -1), onehot_t).T

    small_d = {n: jnp.stack(v) for n, v in small.items()}
    small_d["rel_bias"] = d_rel_bias
    (recv_small,) = exchange([], [_pack_small(small_d)], "gather_small_grads")

    res_win, res_rest = [], []
    for l, (recv_win, recv_rest) in enumerate(((recv_win0, recv_rest0), (recv_win1, recv_rest1))):
        res_win.append(adamw(recv_win, w_in[l], m_w_in[l], v_w_in[l], WIN_TILE, "adamw_w_in"))
        layer_rest = lambda t: jnp.concatenate([t[n][l] for n in REST], axis=0)
        res_rest.append(adamw(recv_rest, layer_rest(w), layer_rest(mom), layer_rest(var), REST_TILE, "adamw_rest"))
    res_win = [jnp.stack([res_win[0][q], res_win[1][q]]) for q in range(4)]
    res_rest = [jnp.concatenate([res_rest[0][q], res_rest[1][q]], axis=0) for q in range(4)]
    small_names = [n for n, _ in SMALL_SIZES if n != "conv_w_full"]
    g_s, dw_s, nm_s, nv_s = adamw(recv_small, _pack_small({n: w[n] for n in small_names}),
                                  _pack_small({n: mom[n] for n in small_names}),
                                  _pack_small({n: var[n] for n in small_names}), SMALL_TILE, "adamw_small")
    shapes = {n: w[n].shape for n in small_names}
    shapes["conv_w_full"] = (2, 4, CONV_DIM)
    g_conv_full = _unpack_small(g_s, shapes)["conv_w_full"]
    g_conv = lax.dynamic_slice_in_dim(g_conv_full, my_dev * 384, 384, axis=2)
    pack_conv = lambda a: _pad_rows(a.reshape(-1, D_MODEL), 8)
    g_c, dw_c, nm_c, nv_c = adamw(pack_conv(g_conv)[None], pack_conv(ssm_conv_w), pack_conv(m_ssm_conv_w),
                                  pack_conv(v_ssm_conv_w), 8, "adamw_conv")

    results = {}
    for q, (tag, psm, pc) in enumerate((("grad", g_s, g_c), ("delta", dw_s, dw_c), ("new_m", nm_s, nm_c),
                                        ("new_v", nv_s, nv_c))):
        r = dict(zip(REST, _unpack_rest(res_rest[q])))
        r["w_in"] = res_win[q]
        r.update(_unpack_small(psm, {n: w[n].shape for n in small_names}))
        r["ssm_conv_w"] = pc[0:3].reshape(2, 4, 384)
        results[tag] = r
    outs = [loss, grad_x[None]]
    for tag in ("grad", "delta", "new_m", "new_v"):
        outs += [results[tag][n] for n in WEIGHTS]
    return tuple(outs)
```

```python
import math

import jax
import jax.numpy as jnp
from jax import lax
from jax.experimental import pallas as pl
from jax.experimental.pallas import tpu as pltpu

F32 = jnp.float32
MXU_DTYPE = jnp.bfloat16
ACT_DTYPE = jnp.bfloat16
WIRE_DTYPE = jnp.bfloat16
HI = lax.Precision.HIGHEST
MESH = pl.DeviceIdType.MESH

D_MODEL = 1024
N_DEV = 8
ATT_HEADS = 16
HEAD_DIM = 64
BLK = 128
SG_GROUPS = 8
SSM_WIDTH = 2048
SSM_HEADS = 32
SSM_GROUPS = 4
SSM_GW = SSM_WIDTH // SSM_GROUPS
CONV_DIM = 3072
REL_BUCKETS = 32
EPS = 1e-6
NEG = -1e30

ATT_COLS = 2304
SG_COLS = 3072
SSM_COLS = 5376
GATE_COLS = 3072
DT_OFF = 5120

VMEM_LIMIT_V7X = 56 * 2 ** 20

ADAM_LR, ADAM_B1, ADAM_B2, ADAM_EPS, ADAM_WD, ADAM_STEP = 0.001, 0.9, 0.999, 1e-08, 0.01, 10

WIN_SHARD = 1700
WIN_LANES = 1792
REST_PARTS = (128, 128, 256, 128)
LAYER_REST = sum(REST_PARTS)
REST_TILE = 128
WIN_TILE = 128
SMALL_ROWS = 384
SMALL_TILE = 128


def _cparams(sem=None):
    return pltpu.CompilerParams(dimension_semantics=sem, vmem_limit_bytes=VMEM_LIMIT_V7X)


def _dot(a, b):
    return jnp.dot(a.astype(MXU_DTYPE), b.astype(MXU_DTYPE), preferred_element_type=F32)


def _dot_nt(a, b):
    return lax.dot_general(a.astype(MXU_DTYPE), b.astype(MXU_DTYPE), (((1,), (1,)), ((), ())),
                           preferred_element_type=F32)


def _dot_tn(a, b):
    return lax.dot_general(a.astype(MXU_DTYPE), b.astype(MXU_DTYPE), (((0,), (0,)), ((), ())),
                           preferred_element_type=F32)


def _dot_hi(a, b):
    return jnp.dot(a, b, precision=HI, preferred_element_type=F32)


def _dot_onehot(a, onehot):
    hi = a.astype(jnp.bfloat16)
    lo = (a - hi.astype(F32)).astype(jnp.bfloat16)
    return (jnp.dot(hi, onehot, preferred_element_type=F32) + jnp.dot(lo, onehot, preferred_element_type=F32))


def _dot_hi_nt(a, b):
    return lax.dot_general(a, b, (((1,), (1,)), ((), ())), precision=HI, preferred_element_type=F32)


def _sig(x):
    return 1.0 / (1.0 + jnp.exp(-x))


def _dsilu(x, s):
    return s * (1.0 + x * (1.0 - s))


def _full(shape):
    nd = len(shape)
    return pl.BlockSpec(shape, lambda *_: (0,) * nd)


def rmsnorm_fwd(x, g):
    s, d = x.shape
    tm = min(512, s)

    def body(x_ref, g_ref, o_ref):
        xv = x_ref[...]
        r = lax.rsqrt(jnp.mean(xv * xv, axis=-1, keepdims=True) + EPS)
        o_ref[...] = (xv * r * g_ref[...]).astype(o_ref.dtype)

    return pl.pallas_call(
        body, name="rmsnorm_fwd", grid=(s // tm,),
        in_specs=[pl.BlockSpec((tm, d), lambda i: (i, 0)), _full((1, d))],
        out_specs=pl.BlockSpec((tm, d), lambda i: (i, 0)),
        out_shape=jax.ShapeDtypeStruct((s, d), ACT_DTYPE),
        compiler_params=_cparams(("parallel",)),
    )(x, g)


def mm_nn(a, b, tn, name):
    s, k = a.shape
    n = b.shape[1]
    tm = min(2048, s)

    def body(a_ref, b_ref, o_ref):
        o_ref[...] = _dot(a_ref[...], b_ref[...]).astype(o_ref.dtype)

    return pl.pallas_call(
        body, name=name, grid=(s // tm, n // tn),
        in_specs=[pl.BlockSpec((tm, k), lambda i, j: (i, 0)), pl.BlockSpec((k, tn), lambda i, j: (0, j))],
        out_specs=pl.BlockSpec((tm, tn), lambda i, j: (i, j)),
        out_shape=jax.ShapeDtypeStruct((s, n), ACT_DTYPE),
        compiler_params=_cparams(("parallel", "arbitrary")),
    )(a, b)


def mm_tn(a, b, tn, name):
    s, k = a.shape
    n = b.shape[1]
    ts = min(512, s)

    def body(a_ref, b_ref, o_ref):
        @pl.when(pl.program_id(1) == 0)
        def _():
            o_ref[...] = jnp.zeros_like(o_ref)

        o_ref[...] += _dot_tn(a_ref[...], b_ref[...])

    return pl.pallas_call(
        body, name=name, grid=(n // tn, s // ts),
        in_specs=[pl.BlockSpec((ts, k), lambda j, t: (t, 0)), pl.BlockSpec((ts, tn), lambda j, t: (t, j))],
        out_specs=pl.BlockSpec((k, tn), lambda j, t: (0, j)),
        out_shape=jax.ShapeDtypeStruct((k, n), F32),
        compiler_params=_cparams(("parallel", "arbitrary")),
    )(a, b)


def dh_norm_bwd(dslabs, wslabs, x, g, dres, ex=None):
    s, d = x.shape
    tm = min(1024, s)
    tk = 768
    counts = [ds.shape[1] // tk for ds in dslabs]
    starts = [sum(counts[:i]) for i in range(len(counts))]
    nk = sum(counts)
    ns = len(dslabs)

    hosted = ex is not None
    ni = s // tm

    def mm_body(*refs):
        (own_in, (dh_ref,), _), hosted_refs = _split_hosted(refs, 2 * ns, 1, 0, ex)
        d_refs, w_refs = own_in[:ns], own_in[ns:]
        i, k = pl.program_id(0), pl.program_id(1)
        if hosted:
            @pl.when((i == 0) & (k == 0))
            def _():
                ex.start(*hosted_refs)

            @pl.when((i == ni - 1) & (k == nk - 1))
            def _():
                ex.wait(*hosted_refs)

        @pl.when(k == 0)
        def _():
            dh_ref[...] = jnp.zeros_like(dh_ref)

        for q in range(ns):
            @pl.when((k >= starts[q]) & (k < starts[q] + counts[q]))
            def _(q=q):
                dh_ref[...] += _dot_nt(d_refs[q][...], w_refs[q][...])

    def clamp(q):
        return lambda i, k: (i, jnp.clip(k - starts[q], 0, counts[q] - 1))

    def clamp_w(q):
        return lambda i, k: (0, jnp.clip(k - starts[q], 0, counts[q] - 1))

    res = pl.pallas_call(
        mm_body, name="dh_matmul_scatter" if hosted else "dh_matmul", grid=(ni, nk),
        in_specs=([pl.BlockSpec((tm, tk), clamp(q)) for q in range(ns)]
                  + [pl.BlockSpec((d, tk), clamp_w(q)) for q in range(ns)] + (ex.in_specs if hosted else [])),
        out_specs=[pl.BlockSpec((tm, d), lambda i, k: (i, 0))] + (ex.out_specs if hosted else []),
        out_shape=[jax.ShapeDtypeStruct((s, d), F32)] + (ex.out_shape if hosted else []),
        scratch_shapes=ex.scratch if hosted else [],
        compiler_params=_cparams(("arbitrary" if hosted else "parallel", "arbitrary")),
    )(*dslabs, *wslabs, *(ex.arrays if hosted else []))
    dh, ex_results = res[0], res[1:]

    te = min(512, s)

    def norm_body(dh_ref, x_ref, g_ref, dres_ref, dx_ref, dg_ref):
        @pl.when(pl.program_id(0) == 0)
        def _():
            dg_ref[...] = jnp.zeros_like(dg_ref)

        xv = x_ref[...]
        r = lax.rsqrt(jnp.mean(xv * xv, axis=-1, keepdims=True) + EPS)
        xn = xv * r
        dhv = dh_ref[...]
        dg_ref[...] += jnp.sum(dhv * xn, axis=0, keepdims=True)
        dxn = dhv * g_ref[...]
        dx_ref[...] = dres_ref[...] + r * (dxn - xn * jnp.mean(dxn * xn, axis=-1, keepdims=True))

    rows = pl.BlockSpec((te, d), lambda i: (i, 0))
    dx, dg = pl.pallas_call(
        norm_body, name="norm_bwd", grid=(s // te,),
        in_specs=[rows, rows, _full((1, d)), rows],
        out_specs=[rows, _full((1, d))],
        out_shape=[jax.ShapeDtypeStruct((s, d), F32), jax.ShapeDtypeStruct((1, d), F32)],
        compiler_params=_cparams(("arbitrary",)),
    )(dh, x, g, dres)
    return (dx, dg, ex_results) if hosted else (dx, dg)


def bias_table(rel_bias_t, onehot_t, maskadd):
    n = onehot_t.shape[1]
    tn = 8192

    def body(r_ref, o_ref, m_ref, out_ref):
        out_ref[...] = _dot_hi(r_ref[...], o_ref[...]) + m_ref[...]

    return pl.pallas_call(
        body, name="bias_table", grid=(n // tn,),
        in_specs=[_full((ATT_HEADS, REL_BUCKETS)), pl.BlockSpec((REL_BUCKETS, tn), lambda j: (0, j)),
                  pl.BlockSpec((1, tn), lambda j: (0, j))],
        out_specs=pl.BlockSpec((ATT_HEADS, tn), lambda j: (0, j)),
        out_shape=jax.ShapeDtypeStruct((ATT_HEADS, n), F32),
        compiler_params=_cparams(("parallel",)),
    )(rel_bias_t, onehot_t, maskadd)


def bias_table_bwd(dbias, onehot_t):
    n = onehot_t.shape[1]
    tn = 8192

    def body(d_ref, o_ref, out_ref):
        @pl.when(pl.program_id(0) == 0)
        def _():
            out_ref[...] = jnp.zeros_like(out_ref)

        out_ref[...] += _dot_hi_nt(d_ref[...], o_ref[...])

    return pl.pallas_call(
        body, name="bias_table_bwd", grid=(n // tn,),
        in_specs=[pl.BlockSpec((ATT_HEADS, tn), lambda j: (0, j)), pl.BlockSpec((REL_BUCKETS, tn), lambda j: (0, j))],
        out_specs=_full((ATT_HEADS, REL_BUCKETS)),
        out_shape=jax.ShapeDtypeStruct((ATT_HEADS, REL_BUCKETS), F32),
        compiler_params=_cparams(("arbitrary",)),
    )(dbias, onehot_t)


def _fold(full, tri):
    return jnp.where(tri, full[BLK:2 * BLK], full[0:BLK])


def _unfold(folded, tri):
    return jnp.concatenate([jnp.where(tri, 0.0, folded), jnp.where(tri, folded, 0.0)], axis=0)


GROUP_HEADS = ATT_HEADS // 2
GROUP_LANES = GROUP_HEADS * BLK


def _att_group(qg, kcat, vt_cat, bias_g, sink_g, tri, no_prev):
    l = _fold(_dot(kcat, qg), tri) * (HEAD_DIM ** -0.5) + bias_g
    l = jnp.where(no_prev, NEG, l)
    m = jnp.maximum(jnp.max(l, axis=0, keepdims=True), sink_g)
    p = jnp.exp(l - m)
    es = jnp.exp(sink_g - m)
    inv = 1.0 / (jnp.sum(p, axis=0, keepdims=True) + es)
    p = p * inv
    pcat = _unfold(p, tri)
    return p, pcat, es * inv, _dot(vt_cat, pcat)


def _heads_to_lanes(ref, row0):
    return jnp.concatenate([ref[row0 + j * HEAD_DIM:row0 + (j + 1) * HEAD_DIM, :] for j in range(GROUP_HEADS)], axis=1)


def _lanes_to_heads(ref, row0, val):
    for j in range(GROUP_HEADS):
        ref[row0 + j * HEAD_DIM:row0 + (j + 1) * HEAD_DIM, :] = val[:, j * BLK:(j + 1) * BLK].astype(ref.dtype)


def _kv_cat(kvp, kvc, g):
    lo = g * HEAD_DIM
    kt_cat = jnp.concatenate([kvp[lo:lo + HEAD_DIM], kvc[lo:lo + HEAD_DIM]], axis=1)
    vt_cat = jnp.concatenate([kvp[128 + lo:128 + lo + HEAD_DIM], kvc[128 + lo:128 + lo + HEAD_DIM]], axis=1)
    return kt_cat, vt_cat


def _tri_masks(n):
    row = lax.broadcasted_iota(jnp.int32, (BLK, GROUP_LANES), 0)
    query = lax.broadcasted_iota(jnp.int32, (BLK, GROUP_LANES), 1) & (BLK - 1)
    tri = row <= query
    return tri, (n == 0) & jnp.logical_not(tri)


def _split_hosted(refs, n_in, n_out, n_scratch, ex):
    na = ex.na if ex is not None else 0
    o = 0
    parts = []
    for cnt in (n_in, na, n_out, na, n_scratch, 3 if ex is not None else 0):
        parts.append(refs[o:o + cnt])
        o += cnt
    own_in, ex_in, own_out, ex_out, own_scr, ex_sems = parts
    return (own_in, own_out, own_scr), (ex_in, ex_out, ex_sems)


def _call_hosting(body, name, nsteps, in_specs, out_specs, out_shape, scratch, args, ex):
    n_in, n_out, n_scr = len(in_specs), len(out_specs), len(scratch)
    hosted = ex is not None

    def full_body(*refs):
        (own_in, own_out, own_scr), hosted_refs = _split_hosted(refs, n_in, n_out, n_scr, ex)
        if hosted:
            @pl.when(pl.program_id(0) == 0)
            def _():
                ex.start(*hosted_refs)

            @pl.when(pl.program_id(0) == nsteps - 1)
            def _():
                ex.wait(*hosted_refs)

        body(*own_in, *own_out, *own_scr)

    res = pl.pallas_call(
        full_body, name=name + "_hosting" if hosted else name, grid=(nsteps,),
        in_specs=list(in_specs) + (ex.in_specs if hosted else []),
        out_specs=list(out_specs) + (ex.out_specs if hosted else []),
        out_shape=list(out_shape) + (ex.out_shape if hosted else []),
        scratch_shapes=list(scratch) + (ex.scratch if hosted else []),
        compiler_params=_cparams(("arbitrary",)),
    )(*args, *(ex.arrays if hosted else []))
    return res[:n_out], res[n_out:]


def attn_fwd(pa, bias, sinks, ex=None):
    s = pa.shape[1]
    nb = s // BLK

    def body(*refs):
        ((pa_ref, kvp_ref, bias_ref, sink_ref), (y_ref,), _), hosted = _split_hosted(refs, 4, 1, 0, ex)
        n = pl.program_id(0)
        if ex is not None:
            @pl.when(n == 0)
            def _():
                ex.start(*hosted)

            @pl.when(n == nb - 1)
            def _():
                ex.wait(*hosted)

        kvc = pa_ref[2048:2304, :]
        kvp = kvp_ref[...]
        tri, no_prev = _tri_masks(n)
        for g in range(2):
            kt_cat, vt_cat = _kv_cat(kvp, kvc, g)
            row0 = g * GROUP_HEADS * HEAD_DIM
            _, _, _, o = _att_group(_heads_to_lanes(pa_ref, row0), kt_cat.astype(F32).T, vt_cat, bias_ref[g],
                                    sink_ref[g:g + 1, :], tri, no_prev)
            z = _heads_to_lanes(pa_ref, 1024 + row0).astype(F32)
            _lanes_to_heads(y_ref, row0, o * z * _sig(z))

    hosted = ex is not None
    res = pl.pallas_call(
        body, name="attn_fwd_gather" if hosted else "attn_fwd", grid=(nb,),
        in_specs=[pl.BlockSpec((ATT_COLS, BLK), lambda n: (0, n)),
                  pl.BlockSpec((256, BLK), lambda n: (8, jnp.maximum(n - 1, 0))),
                  _full((2, BLK, GROUP_LANES)), _full((2, GROUP_LANES))] + (ex.in_specs if hosted else []),
        out_specs=[pl.BlockSpec((1024, BLK), lambda n: (0, n))] + (ex.out_specs if hosted else []),
        out_shape=[jax.ShapeDtypeStruct((1024, s), ACT_DTYPE)] + (ex.out_shape if hosted else []),
        scratch_shapes=ex.scratch if hosted else [],
        compiler_params=_cparams(("arbitrary",)),
    )(pa, pa, bias, sinks, *(ex.arrays if hosted else []))
    return (res[0], res[1:]) if hosted else res[0]


def attn_bwd(pa, dy, bias, sinks, dbias_in, ex=None):
    s = pa.shape[1]
    nb = s // BLK

    def body(*refs):
        ((pa_ref, kvp_ref, dy_ref, bias_ref, sink_ref, dbin_ref), (dpa_ref, dbias_ref, dsink_ref),
         (carry, dsink_acc)), hosted = _split_hosted(refs, 6, 3, 2, ex)
        i = pl.program_id(0)
        n = nb - 1 - i

        @pl.when(i == 0)
        def _():
            dbias_ref[...] = dbin_ref[...]
            dsink_acc[...] = jnp.zeros_like(dsink_acc)
            carry[...] = jnp.zeros_like(carry)

        if ex is not None:
            @pl.when(i == 0)
            def _():
                ex.start(*hosted)

            @pl.when(i == nb - 1)
            def _():
                ex.wait(*hosted)

        kvc = pa_ref[2048:2304, :]
        kvp = kvp_ref[...]
        tri, no_prev = _tri_masks(n)
        scale = HEAD_DIM ** -0.5
        for g in range(2):
            kt_cat, vt_cat = _kv_cat(kvp, kvc, g)
            row0 = g * GROUP_HEADS * HEAD_DIM
            qg = _heads_to_lanes(pa_ref, row0)
            p, pcat, psink, o = _att_group(qg, kt_cat.astype(F32).T, vt_cat, bias_ref[g], sink_ref[g:g + 1, :], tri,
                                           no_prev)
            z = _heads_to_lanes(pa_ref, 1024 + row0).astype(F32)
            dyg = _heads_to_lanes(dy_ref, row0).astype(F32)
            sz = _sig(z)
            d_o = dyg * z * sz
            _lanes_to_heads(dpa_ref, 1024 + row0, dyg * _dsilu(z, sz) * o)
            delta = jnp.sum(d_o * o, axis=0, keepdims=True)
            dl = p * (_fold(_dot(vt_cat.astype(F32).T, d_o), tri) - delta)
            dsink_acc[g:g + 1, :] += psink * delta
            dbias_ref[g] += dl
            dlcat = _unfold(dl, tri)
            _lanes_to_heads(dpa_ref, row0, _dot(kt_cat, dlcat) * scale)
            for q, dkv in enumerate((_dot_nt(qg, dlcat) * scale, _dot_nt(d_o, pcat))):
                r0 = q * 128 + g * HEAD_DIM
                dpa_ref[2048 + r0:2048 + r0 + HEAD_DIM, :] = (
                    dkv[:, BLK:2 * BLK] + carry[r0:r0 + HEAD_DIM, :]).astype(dpa_ref.dtype)
                carry[r0:r0 + HEAD_DIM, :] = dkv[:, 0:BLK]

        @pl.when(i == nb - 1)
        def _():
            lane = lax.broadcasted_iota(jnp.int32, (1, 128), 1)
            dsink = jnp.zeros((1, 128), F32)
            for h in range(ATT_HEADS):
                g, j = divmod(h, GROUP_HEADS)
                tot = jnp.sum(dsink_acc[g:g + 1, j * BLK:(j + 1) * BLK], axis=1, keepdims=True)
                dsink = dsink + jnp.where(lane == h, -tot, 0.0)
            dsink_ref[...] = dsink

    hosted = ex is not None
    res = pl.pallas_call(
        body, name="attn_bwd_scatter" if hosted else "attn_bwd", grid=(nb,),
        in_specs=[pl.BlockSpec((ATT_COLS, BLK), lambda i: (0, nb - 1 - i)),
                  pl.BlockSpec((256, BLK), lambda i: (8, jnp.maximum(nb - 2 - i, 0))),
                  pl.BlockSpec((1024, BLK), lambda i: (0, nb - 1 - i)),
                  _full((2, BLK, GROUP_LANES)), _full((2, GROUP_LANES)), _full((2, BLK, GROUP_LANES))]
        + (ex.in_specs if hosted else []),
        out_specs=[pl.BlockSpec((ATT_COLS, BLK), lambda i: (0, nb - 1 - i)),
                   _full((2, BLK, GROUP_LANES)), _full((1, 128))] + (ex.out_specs if hosted else []),
        out_shape=[jax.ShapeDtypeStruct((ATT_COLS, s), ACT_DTYPE),
                   jax.ShapeDtypeStruct((2, BLK, GROUP_LANES), F32),
                   jax.ShapeDtypeStruct((1, 128), F32)] + (ex.out_shape if hosted else []),
        scratch_shapes=[pltpu.VMEM((256, BLK), F32), pltpu.VMEM((2, GROUP_LANES), F32)]
        + (ex.scratch if hosted else []),
        compiler_params=_cparams(("arbitrary",)),
    )(pa, pa, dy, bias, sinks, dbias_in, *(ex.arrays if hosted else []))
    return (res[0], res[1], res[2], res[3:]) if hosted else tuple(res)


def _layernorm(v, g, b):
    mu = jnp.mean(v, axis=-1, keepdims=True)
    vc = v - mu
    rstd = lax.rsqrt(jnp.mean(vc * vc, axis=-1, keepdims=True) + EPS)
    xhat = vc * rstd
    return xhat, rstd, xhat * g + b


def sgu_fwd(ps, ln_g, ln_b, w_tril, b_t):
    s = ps.shape[0]

    def body(ps_ref, g_ref, b_ref, w_ref, bt_ref, y_ref):
        u = ps_ref[:, 0:1024].astype(F32)
        v = ps_ref[:, 1024:2048].astype(F32)
        z = ps_ref[:, 2048:3072].astype(F32)
        _, _, vn = _layernorm(v, g_ref[...], b_ref[...])
        gate = u * z * _sig(z)
        for g in range(SG_GROUPS):
            sl = slice(g * 128, (g + 1) * 128)
            mixed = _dot(w_ref[g], vn[:, sl]) + bt_ref[:, g:g + 1]
            y_ref[:, sl] = (gate[:, sl] * mixed).astype(y_ref.dtype)

    return pl.pallas_call(
        body, name="sgu_fwd", grid=(s // BLK,),
        in_specs=[pl.BlockSpec((BLK, SG_COLS), lambda c: (c, 0)), _full((1, 1024)), _full((1, 1024)),
                  _full((SG_GROUPS, BLK, BLK)), _full((BLK, 128))],
        out_specs=pl.BlockSpec((BLK, 1024), lambda c: (c, 0)),
        out_shape=jax.ShapeDtypeStruct((s, 1024), ACT_DTYPE),
        compiler_params=_cparams(("parallel",)),
    )(ps, ln_g, ln_b, w_tril, b_t)


def sgu_bwd(ps, dy, ln_g, ln_b, w_tril, w_tril_t, b_t):
    s = ps.shape[0]

    def body(ps_ref, dy_ref, g_ref, b_ref, w_ref, wt_ref, bt_ref, dps_ref, dw_ref, dbt_ref, dg_ref, db_ref, dvn_scr):
        @pl.when(pl.program_id(0) == 0)
        def _():
            dw_ref[...] = jnp.zeros_like(dw_ref)
            dbt_ref[...] = jnp.zeros_like(dbt_ref)
            dg_ref[...] = jnp.zeros_like(dg_ref)
            db_ref[...] = jnp.zeros_like(db_ref)

        u = ps_ref[:, 0:1024].astype(F32)
        v = ps_ref[:, 1024:2048].astype(F32)
        z = ps_ref[:, 2048:3072].astype(F32)
        dy = dy_ref[...].astype(F32)
        xhat, rstd, vn = _layernorm(v, g_ref[...], b_ref[...])
        sz = _sig(z)
        silu = z * sz
        row = lax.broadcasted_iota(jnp.int32, (BLK, BLK), 0)
        colm = lax.broadcasted_iota(jnp.int32, (BLK, BLK), 1)
        tril = row >= colm
        dbt = jnp.zeros((BLK, 128), F32)
        for g in range(SG_GROUPS):
            sl = slice(g * 128, (g + 1) * 128)
            vng = vn[:, sl]
            mixed = _dot(w_ref[g], vng) + bt_ref[:, g:g + 1]
            dyg, ug = dy[:, sl], u[:, sl]
            dps_ref[:, sl] = (dyg * mixed * silu[:, sl]).astype(dps_ref.dtype)
            dps_ref[:, 2048 + g * 128:2048 + (g + 1) * 128] = (
                dyg * ug * mixed * _dsilu(z[:, sl], sz[:, sl])).astype(dps_ref.dtype)
            dm = dyg * ug * silu[:, sl]
            dw_ref[g] += jnp.where(tril, _dot_nt(dm, vng), 0.0)
            dbt = dbt + jnp.where(colm == g, jnp.sum(dm, axis=1, keepdims=True), 0.0)
            dvn_scr[:, sl] = _dot(wt_ref[g], dm)
        dbt_ref[...] += dbt
        dvn = dvn_scr[...]
        dg_ref[...] += jnp.sum(dvn * xhat, axis=0, keepdims=True)
        db_ref[...] += jnp.sum(dvn, axis=0, keepdims=True)
        dxh = dvn * g_ref[...]
        dv = rstd * (dxh - jnp.mean(dxh, axis=-1, keepdims=True)
                     - xhat * jnp.mean(dxh * xhat, axis=-1, keepdims=True))
        dps_ref[:, 1024:2048] = dv.astype(dps_ref.dtype)

    return pl.pallas_call(
        body, name="sgu_bwd", grid=(s // BLK,),
        in_specs=[pl.BlockSpec((BLK, SG_COLS), lambda c: (c, 0)), pl.BlockSpec((BLK, 1024), lambda c: (c, 0)),
                  _full((1, 1024)), _full((1, 1024)), _full((SG_GROUPS, BLK, BLK)), _full((SG_GROUPS, BLK, BLK)),
                  _full((BLK, 128))],
        out_specs=[pl.BlockSpec((BLK, SG_COLS), lambda c: (c, 0)), _full((SG_GROUPS, BLK, BLK)), _full((BLK, 128)),
                   _full((1, 1024)), _full((1, 1024))],
        out_shape=[jax.ShapeDtypeStruct((s, SG_COLS), ACT_DTYPE), jax.ShapeDtypeStruct((SG_GROUPS, BLK, BLK), F32),
                   jax.ShapeDtypeStruct((BLK, 128), F32), jax.ShapeDtypeStruct((1, 1024), F32),
                   jax.ShapeDtypeStruct((1, 1024), F32)],
        scratch_shapes=[pltpu.VMEM((BLK, 1024), F32)],
        compiler_params=_cparams(("arbitrary",)),
    )(ps, dy, ln_g, ln_b, w_tril, w_tril_t, b_t)


def _shift_down(cur, prev16, k):
    if k == 0:
        return cur
    r = pltpu.roll(cur, k, 0)
    rp = pltpu.roll(prev16, k, 0)
    row = lax.broadcasted_iota(jnp.int32, (8, cur.shape[1]), 0)
    return jnp.concatenate([jnp.where(row < k, rp[0:8], r[0:8]), r[8:]], axis=0)


def _shift_up(cur, next16, k):
    if k == 0:
        return cur
    n = cur.shape[0]
    r = pltpu.roll(cur, n - k, 0)
    rn = pltpu.roll(next16, 16 - k, 0)
    row = lax.broadcasted_iota(jnp.int32, (8, cur.shape[1]), 0)
    return jnp.concatenate([r[:n - 8], jnp.where(row >= 8 - k, rn[8:16], r[n - 8:])], axis=0)


def _bcast8(v):
    return jnp.broadcast_to(v, (16, v.shape[1]))


class _Ssd:
    def __init__(self, xbc, prev16, dtr, cw, cbias, dtb, alog, dsk, tri, e):
        pre = cbias + cw[3:4] * xbc
        self.shifted = [xbc]
        for k in (1, 2, 3):
            sh = _shift_down(xbc, prev16, k)
            self.shifted.append(sh)
            pre = pre + cw[3 - k:4 - k] * sh
        self.pre = pre
        self.sg = _sig(pre)
        act = pre * self.sg
        self.xs = act[:, 0:SSM_WIDTH]
        self.bm = act[:, SSM_WIDTH:SSM_WIDTH + 512]
        self.cm = act[:, SSM_WIDTH + 512:CONV_DIM]
        self.dtp = dtr + dtb
        self.dt = jnp.maximum(self.dtp, 0.0) + jnp.log(1.0 + jnp.exp(-jnp.abs(self.dtp)))
        self.a = -jnp.exp(alog)
        self.acs = _dot_hi(tri, self.dt * self.a)
        self.acs_t = self.acs.T
        tot = self.acs[BLK - 1:BLK]
        self.ecs = jnp.exp(self.acs)
        self.dte = jnp.exp(tot - self.acs)
        self.cd = jnp.exp(tot)
        self.dt_x = _dot_onehot(self.dt, e)
        self.ecs_x = _dot_onehot(self.ecs, e)
        self.dte_x = _dot_onehot(self.dte, e)
        self.cd_x = _dot_onehot(_bcast8(self.cd), e)[0:1]
        self.d_x = _dot_onehot(_bcast8(dsk), e)[0:1]
        self.xdt = self.xs * self.dt_x
        row = lax.broadcasted_iota(jnp.int32, (BLK, BLK), 0)
        col = lax.broadcasted_iota(jnp.int32, (BLK, BLK), 1)
        self.tril = row >= col

    def group(self, g):
        sl = slice(g * 128, (g + 1) * 128)
        bg, cg = self.bm[:, sl], self.cm[:, sl]
        return bg, cg, _dot_nt(cg, bg)

    def decay(self, h):
        seg = self.acs[:, h:h + 1] - self.acs_t[h:h + 1, :]
        return jnp.exp(jnp.where(self.tril, seg, NEG))

    def y_pre_gate(self, ht_of, yd_scr, yoff_scr):
        for g in range(SSM_GROUPS):
            bg, cg, cb = self.group(g)
            for j in range(8):
                h = g * 8 + j
                sl = slice(h * 64, (h + 1) * 64)
                yd_scr[:, sl] = _dot(cb * self.decay(h), self.xdt[:, sl])
            gs = slice(g * SSM_GW, (g + 1) * SSM_GW)
            yoff_scr[:, gs] = _dot(cg, ht_of(g)) * self.ecs_x[:, gs]
        return yd_scr[...] + yoff_scr[...] + self.d_x * self.xs


def _ssd_consts():
    hh = lax.broadcasted_iota(jnp.int32, (128, SSM_WIDTH), 0)
    ch = lax.broadcasted_iota(jnp.int32, (128, SSM_WIDTH), 1)
    e = (ch // 64 == hh).astype(jnp.bfloat16)
    row = lax.broadcasted_iota(jnp.int32, (BLK, BLK), 0)
    col = lax.broadcasted_iota(jnp.int32, (BLK, BLK), 1)
    tri = (row >= col).astype(F32)
    return tri, e


def _pad_lanes(v, n=128):
    return jnp.pad(v, ((0, 0), (0, n - v.shape[1])))


def ssd_fwd(pm, cw, cbias, dtb, alog, dsk, ng, ex=None):
    s = pm.shape[0]
    nc = s // BLK
    tri, e = _ssd_consts()

    def body(pm_ref, prev_ref, cw_ref, cb_ref, dtb_ref, al_ref, d_ref, ng_ref, tri_ref, e_ref,
             y_ref, st_ref, ht_ref, yd_scr, yoff_scr):
        c = pl.program_id(0)

        @pl.when(c == 0)
        def _():
            ht_ref[...] = jnp.zeros_like(ht_ref)

        xbc = pm_ref[:, 0:CONV_DIM].astype(F32)
        prev16 = jnp.where(c == 0, 0.0, prev_ref[...].astype(F32))
        f = _Ssd(xbc, prev16, pm_ref[:, DT_OFF:DT_OFF + 128].astype(F32), cw_ref[...], cb_ref[...], dtb_ref[...],
                 al_ref[...], d_ref[...], tri_ref[...], e_ref[...])
        st_ref[0] = ht_ref[...]
        y = f.y_pre_gate(lambda g: ht_ref[g], yd_scr, yoff_scr)
        for g in range(SSM_GROUPS):
            bg, _, _ = f.group(g)
            gs = slice(g * SSM_GW, (g + 1) * SSM_GW)
            ht_ref[g] = ht_ref[g] * f.cd_x[:, gs] + _dot_tn(bg, f.xdt[:, gs] * f.dte_x[:, gs])
        z = pm_ref[:, CONV_DIM:CONV_DIM + SSM_WIDTH].astype(F32)
        ypre = y * z * _sig(z)
        for g in range(SSM_GROUPS):
            gs = slice(g * SSM_GW, (g + 1) * SSM_GW)
            yg = ypre[:, gs]
            rr = lax.rsqrt(jnp.mean(yg * yg, axis=-1, keepdims=True) + EPS)
            y_ref[:, gs] = (yg * rr * ng_ref[:, gs]).astype(y_ref.dtype)

    own, hosted = _call_hosting(
        body, "ssd_fwd", nc,
        in_specs=[pl.BlockSpec((BLK, SSM_COLS), lambda c: (c, 0)),
                  pl.BlockSpec((16, CONV_DIM), lambda c: (jnp.maximum(8 * c - 1, 0), 0)),
                  _full((4, CONV_DIM)), _full((1, CONV_DIM)), _full((1, 128)), _full((1, 128)), _full((1, 128)),
                  _full((1, SSM_WIDTH)), _full((BLK, BLK)), _full((128, SSM_WIDTH))],
        out_specs=[pl.BlockSpec((BLK, SSM_WIDTH), lambda c: (c, 0)),
                   pl.BlockSpec((1, SSM_GROUPS, 128, SSM_GW), lambda c: (c, 0, 0, 0))],
        out_shape=[jax.ShapeDtypeStruct((s, SSM_WIDTH), ACT_DTYPE),
                   jax.ShapeDtypeStruct((nc, SSM_GROUPS, 128, SSM_GW), F32)],
        scratch=[pltpu.VMEM((SSM_GROUPS, 128, SSM_GW), F32), pltpu.VMEM((BLK, SSM_WIDTH), F32),
                 pltpu.VMEM((BLK, SSM_WIDTH), F32)],
        args=(pm, pm, cw, cbias, dtb, alog, dsk, ng, tri, e), ex=ex)
    return (*own, hosted) if ex is not None else tuple(own)


def ssd_bwd(pm, dy, states, cw, cbias, dtb, alog, dsk, ng, ex=None):
    s = pm.shape[0]
    nc = s // BLK
    tri, e = _ssd_consts()
    tri_t, e_t = tri.T, e.T

    def body(pm_ref, prev_ref, dy_ref, st_ref, cw_ref, cb_ref, dtb_ref, al_ref, d_ref, ng_ref,
             tri_ref, trit_ref, e_ref, et_ref,
             dpm_ref, dcw_ref, dcb_ref, dvec_ref, dng_ref,
             dht_ref, dcar_ref, yd_scr, yoff_scr, dx_scr, r2_scr, hs_scr, da_scr, dat_scr, dd_scr, dbc_scr):
        i = pl.program_id(0)
        n = nc - 1 - i

        @pl.when(i == 0)
        def _():
            dht_ref[...] = jnp.zeros_like(dht_ref)
            dcar_ref[...] = jnp.zeros_like(dcar_ref)
            dcw_ref[...] = jnp.zeros_like(dcw_ref)
            dcb_ref[...] = jnp.zeros_like(dcb_ref)
            dvec_ref[...] = jnp.zeros_like(dvec_ref)
            dng_ref[...] = jnp.zeros_like(dng_ref)
            dd_scr[...] = jnp.zeros_like(dd_scr)
            da_scr[...] = jnp.zeros_like(da_scr)
            dat_scr[...] = jnp.zeros_like(dat_scr)

        xbc = pm_ref[:, 0:CONV_DIM].astype(F32)
        prev16 = jnp.where(n == 0, 0.0, prev_ref[...].astype(F32))
        cw = cw_ref[...]
        f = _Ssd(xbc, prev16, pm_ref[:, DT_OFF:DT_OFF + 128].astype(F32), cw, cb_ref[...], dtb_ref[...],
                 al_ref[...], d_ref[...], tri_ref[...], e_ref[...])
        et = et_ref[...]
        y = f.y_pre_gate(lambda g: st_ref[0, g], yd_scr, yoff_scr)

        z = pm_ref[:, CONV_DIM:CONV_DIM + SSM_WIDTH].astype(F32)
        dyv = dy_ref[...].astype(F32)
        sz = _sig(z)
        silu = z * sz
        ypre = y * silu
        for g in range(SSM_GROUPS):
            gs = slice(g * SSM_GW, (g + 1) * SSM_GW)
            yg = ypre[:, gs]
            rr = lax.rsqrt(jnp.mean(yg * yg, axis=-1, keepdims=True) + EPS)
            nrm = yg * rr
            dng_ref[:, gs] += jnp.sum(dyv[:, gs] * nrm, axis=0, keepdims=True)
            dn = dyv[:, gs] * ng_ref[:, gs]
            dx_scr[:, gs] = rr * (dn - nrm * jnp.mean(dn * nrm, axis=-1, keepdims=True))
        dypre = dx_scr[...]
        d_y = dypre * silu
        dpm_ref[:, CONV_DIM:CONV_DIM + SSM_WIDTH] = (dypre * y * _dsilu(z, sz)).astype(dpm_ref.dtype)

        for g in range(SSM_GROUPS):
            bg, cg, cb = f.group(g)
            gs = slice(g * SSM_GW, (g + 1) * SSM_GW)
            htg = st_ref[0, g]
            dhn = dht_ref[g]
            dcb = jnp.zeros((BLK, BLK), F32)
            for j in range(8):
                h = g * 8 + j
                sl = slice(h * 64, (h + 1) * 64)
                dec = f.decay(h)
                dyh = d_y[:, sl]
                dmd = _dot_nt(dyh, f.xdt[:, sl]) * dec
                dcb = dcb + dmd
                gm = dmd * cb
                da_scr[:, h:h + 1] = jnp.sum(gm, axis=1, keepdims=True)
                dat_scr[h:h + 1, :] = jnp.sum(gm, axis=0, keepdims=True)
                dx_scr[:, sl] = _dot_tn(cb * dec, dyh)
            dz = f.ecs_x[:, gs] * d_y[:, gs]
            dbc_scr[:, 512 + g * 128:512 + (g + 1) * 128] = _dot(dcb, bg) + _dot_nt(dz, htg)
            dbc_scr[:, g * 128:(g + 1) * 128] = _dot_tn(dcb, cg) + _dot_nt(f.xdt[:, gs] * f.dte_x[:, gs], dhn)
            dws = _dot(bg, dhn)
            dx_scr[:, gs] += f.dte_x[:, gs] * dws
            r2_scr[:, gs] = dws * f.xdt[:, gs]
            hs_scr[:, gs] = _bcast8(jnp.sum(dhn * htg, axis=0, keepdims=True))
            dht_ref[g] = f.cd_x[:, gs] * dhn + _dot_tn(cg, dz)
        d_x = dx_scr[...]
        r1 = _dot_onehot(d_y * yoff_scr[...], et)
        r2 = _dot_onehot(r2_scr[...], et) * f.dte
        dcd = _dot_onehot(hs_scr[...], et)[0:1]
        d_tot = jnp.sum(r2, axis=0, keepdims=True) + dcd * f.cd
        row = lax.broadcasted_iota(jnp.int32, (BLK, 128), 0)
        d_a = da_scr[...] - dat_scr[...].T + r1 - r2 + jnp.where(row == BLK - 1, d_tot, 0.0)
        dadt = _dot_hi(trit_ref[...], d_a)
        ddt = dadt * f.a + _dot_onehot(d_x * f.xs, et)
        lane = lax.broadcasted_iota(jnp.int32, (BLK, 128), 1)
        dr = jnp.where(lane < SSM_HEADS, ddt * _sig(f.dtp), 0.0)
        dvec_ref[0:1, :] += jnp.sum(dr, axis=0, keepdims=True)
        dvec_ref[1:2, :] += jnp.sum(dadt * f.dt, axis=0, keepdims=True) * f.a
        dd_scr[...] += _bcast8(jnp.sum(d_y * f.xs, axis=0, keepdims=True))
        dpm_ref[:, DT_OFF:DT_OFF + 128] = dr.astype(dpm_ref.dtype)
        dpm_ref[:, DT_OFF + 128:SSM_COLS] = jnp.zeros((BLK, 128), dpm_ref.dtype)

        dxs = d_x * f.dt_x + f.d_x * d_y
        dact = jnp.concatenate([dxs, dbc_scr[...]], axis=1)
        dpre = dact * _dsilu(f.pre, f.sg)
        dcb_ref[...] += jnp.sum(dpre, axis=0, keepdims=True)
        dxraw = jnp.zeros((BLK, CONV_DIM), F32)
        nxt = dcar_ref[...]
        for k in range(4):
            dcw_ref[3 - k:4 - k, :] += jnp.sum(dpre * f.shifted[k], axis=0, keepdims=True)
            dxraw = dxraw + cw[3 - k:4 - k] * _shift_up(dpre, nxt, k)
        dcar_ref[...] = dpre[0:16]
        dpm_ref[:, 0:CONV_DIM] = dxraw.astype(dpm_ref.dtype)

        @pl.when(i == nc - 1)
        def _():
            dvec_ref[2:3, :] = _dot_onehot(dd_scr[...], et)[0:1]

    own, hosted = _call_hosting(
        body, "ssd_bwd", nc,
        in_specs=[pl.BlockSpec((BLK, SSM_COLS), lambda i: (nc - 1 - i, 0)),
                  pl.BlockSpec((16, CONV_DIM), lambda i: (jnp.maximum(8 * (nc - 1 - i) - 1, 0), 0)),
                  pl.BlockSpec((BLK, SSM_WIDTH), lambda i: (nc - 1 - i, 0)),
                  pl.BlockSpec((1, SSM_GROUPS, 128, SSM_GW), lambda i: (nc - 1 - i, 0, 0, 0)),
                  _full((4, CONV_DIM)), _full((1, CONV_DIM)), _full((1, 128)), _full((1, 128)), _full((1, 128)),
                  _full((1, SSM_WIDTH)), _full((BLK, BLK)), _full((BLK, BLK)), _full((128, SSM_WIDTH)),
                  _full((SSM_WIDTH, 128))],
        out_specs=[pl.BlockSpec((BLK, SSM_COLS), lambda i: (nc - 1 - i, 0)),
                   _full((8, CONV_DIM)), _full((1, CONV_DIM)), _full((8, 128)), _full((1, SSM_WIDTH))],
        out_shape=[jax.ShapeDtypeStruct((s, SSM_COLS), ACT_DTYPE), jax.ShapeDtypeStruct((8, CONV_DIM), F32),
                   jax.ShapeDtypeStruct((1, CONV_DIM), F32), jax.ShapeDtypeStruct((8, 128), F32),
                   jax.ShapeDtypeStruct((1, SSM_WIDTH), F32)],
        scratch=[pltpu.VMEM((SSM_GROUPS, 128, SSM_GW), F32), pltpu.VMEM((16, CONV_DIM), F32),
                 pltpu.VMEM((BLK, SSM_WIDTH), F32), pltpu.VMEM((BLK, SSM_WIDTH), F32),
                 pltpu.VMEM((BLK, SSM_WIDTH), F32), pltpu.VMEM((BLK, SSM_WIDTH), F32),
                 pltpu.VMEM((16, SSM_WIDTH), F32), pltpu.VMEM((BLK, 128), F32), pltpu.VMEM((128, BLK), F32),
                 pltpu.VMEM((16, SSM_WIDTH), F32), pltpu.VMEM((BLK, 1024), F32)],
        args=(pm, pm, dy, states, cw, cbias, dtb, alog, dsk, ng, tri, tri_t, e, e_t), ex=ex)
    return (*own, hosted) if ex is not None else tuple(own)


def merge_fwd(x, ya, ys, ym, pg, wa, ws, wm, wo, g_post):
    s, d = x.shape
    tm = min(256, s)

    def body(x_ref, ya_ref, ys_ref, ym_ref, pg_ref, wa_ref, ws_ref, wm_ref, wo_ref, g_ref,
             xo_ref, ba_ref, bs_ref, bm_ref, mg_ref, out_ref):
        ba = _dot(ya_ref[...], wa_ref[...])
        bs = _dot(ys_ref[...], ws_ref[...])
        bm = _dot(ym_ref[...], wm_ref[...])
        merged = (_sig(pg_ref[:, 0:d].astype(F32)) * ba + _sig(pg_ref[:, d:2 * d].astype(F32)) * bs
                  + _sig(pg_ref[:, 2 * d:3 * d].astype(F32)) * bm)
        out = _dot(merged, wo_ref[...])
        r = lax.rsqrt(jnp.mean(out * out, axis=-1, keepdims=True) + EPS)
        xo_ref[...] = x_ref[...] + out * r * g_ref[...]
        ba_ref[...] = ba.astype(ba_ref.dtype)
        bs_ref[...] = bs.astype(bs_ref.dtype)
        bm_ref[...] = bm.astype(bm_ref.dtype)
        mg_ref[...] = merged.astype(mg_ref.dtype)
        out_ref[...] = out.astype(out_ref.dtype)

    rows = lambda w: pl.BlockSpec((tm, w), lambda i: (i, 0))
    act = jax.ShapeDtypeStruct((s, d), ACT_DTYPE)
    return pl.pallas_call(
        body, name="merge_fwd", grid=(s // tm,),
        in_specs=[rows(d), rows(d), rows(d), rows(2 * d), rows(3 * d), _full((d, d)), _full((d, d)),
                  _full((2 * d, d)), _full((d, d)), _full((1, d))],
        out_specs=[rows(d)] * 6,
        out_shape=[jax.ShapeDtypeStruct((s, d), F32), act, act, act, act, act],
        compiler_params=_cparams(("parallel",)),
    )(x, ya, ys, ym, pg, wa, ws, wm, wo, g_post)


def merge_bwd(dx, out_s, pg, ba, bs, bm, wa, ws, wm, wo, g_post):
    s, d = dx.shape
    tm = min(256, s)

    def body(dx_ref, out_ref, pg_ref, ba_ref, bs_ref, bm_ref, wa_ref, ws_ref, wm_ref, wo_ref, g_ref,
             dout_ref, dba_ref, dbs_ref, dbm_ref, dpg_ref, dya_ref, dys_ref, dym_ref, dg_ref):
        @pl.when(pl.program_id(0) == 0)
        def _():
            dg_ref[...] = jnp.zeros_like(dg_ref)

        o = out_ref[...].astype(F32)
        dxv = dx_ref[...]
        r = lax.rsqrt(jnp.mean(o * o, axis=-1, keepdims=True) + EPS)
        nrm = o * r
        dg_ref[...] += jnp.sum(dxv * nrm, axis=0, keepdims=True)
        dn = dxv * g_ref[...]
        dout = r * (dn - nrm * jnp.mean(dn * nrm, axis=-1, keepdims=True))
        dout_ref[...] = dout.astype(dout_ref.dtype)
        dmerged = _dot_nt(dout, wo_ref[...])
        for q, (b_ref, db_ref, w_ref, dy_ref) in enumerate(((ba_ref, dba_ref, wa_ref, dya_ref),
                                                            (bs_ref, dbs_ref, ws_ref, dys_ref),
                                                            (bm_ref, dbm_ref, wm_ref, dym_ref))):
            gt = _sig(pg_ref[:, q * d:(q + 1) * d].astype(F32))
            db = dmerged * gt
            db_ref[...] = db.astype(db_ref.dtype)
            dpg_ref[:, q * d:(q + 1) * d] = (dmerged * b_ref[...].astype(F32) * gt * (1.0 - gt)).astype(dpg_ref.dtype)
            dy_ref[...] = _dot_nt(db, w_ref[...]).astype(dy_ref.dtype)

    rows = lambda w: pl.BlockSpec((tm, w), lambda i: (i, 0))
    act = lambda w: jax.ShapeDtypeStruct((s, w), ACT_DTYPE)
    return pl.pallas_call(
        body, name="merge_bwd", grid=(s // tm,),
        in_specs=[rows(d), rows(d), rows(3 * d), rows(d), rows(d), rows(d), _full((d, d)), _full((d, d)),
                  _full((2 * d, d)), _full((d, d)), _full((1, d))],
        out_specs=[rows(d), rows(d), rows(d), rows(d), rows(3 * d), rows(d), rows(d), rows(2 * d), _full((1, d))],
        out_shape=[act(d), act(d), act(d), act(d), act(3 * d), act(d), act(d), act(2 * d),
                   jax.ShapeDtypeStruct((1, d), F32)],
        compiler_params=_cparams(("arbitrary",)),
    )(dx, out_s, pg, ba, bs, bm, wa, ws, wm, wo, g_post)


def loss_grad(y, target):
    s, d = y.shape
    tm = min(512, s)

    def body(y_ref, t_ref, dy_ref, l_ref):
        @pl.when(pl.program_id(0) == 0)
        def _():
            l_ref[...] = jnp.zeros_like(l_ref)

        err = y_ref[...] - t_ref[...]
        dy_ref[...] = err * (1.0 / d)
        part = jnp.sum(jnp.sum(err * err, axis=-1, keepdims=True) * (1.0 / d), axis=0, keepdims=True)
        l_ref[...] += 0.5 * jnp.broadcast_to(part, l_ref.shape)

    return pl.pallas_call(
        body, name="loss_grad", grid=(s // tm,),
        in_specs=[pl.BlockSpec((tm, d), lambda i: (i, 0)), pl.BlockSpec((tm, d), lambda i: (i, 0))],
        out_specs=[pl.BlockSpec((tm, d), lambda i: (i, 0)), _full((8, 128))],
        out_shape=[jax.ShapeDtypeStruct((s, d), F32), jax.ShapeDtypeStruct((8, 128), F32)],
        compiler_params=_cparams(("arbitrary",)),
    )(y, target)


def _mesh_pos():
    x, y, c = lax.axis_index("x"), lax.axis_index("y"), lax.axis_index("c")
    return x, y, c, 4 * x + 2 * y + c


def _peer(x, y, c, k):
    px = 1 - x if k & 4 else x
    py = 1 - y if k & 2 else y
    pc = 1 - c if k & 1 else c
    return (px, py, pc), 4 * px + 2 * py + pc


class Exchange:
    def __init__(self, scattered, gathered):
        self.ns = len(scattered)
        self.arrays = list(scattered) + list(gathered)
        self.na = len(self.arrays)
        any_spec = pl.BlockSpec(memory_space=pl.ANY)
        self.in_specs = [any_spec] * self.na
        self.out_specs = [any_spec] * self.na
        self.out_shape = ([jax.ShapeDtypeStruct(a.shape, a.dtype) for a in scattered]
                          + [jax.ShapeDtypeStruct((N_DEV,) + a.shape, a.dtype) for a in gathered])
        self.scratch = [pltpu.SemaphoreType.DMA((self.na, N_DEV - 1)), pltpu.SemaphoreType.DMA((self.na, N_DEV - 1)),
                        pltpu.SemaphoreType.DMA((self.na,))]

    def _src(self, ins, q, slot):
        return ins[q].at[slot] if q < self.ns else ins[q]

    def _local(self, ins, outs, sems):
        me = _mesh_pos()[3]
        return [pltpu.make_async_copy(self._src(ins, q, me), outs[q].at[me], sems[2].at[q]) for q in range(self.na)]

    def _remote(self, ins, outs, sems, incoming):
        x, y, c, me = _mesh_pos()
        copies = []
        for k in range(1, N_DEV):
            peer, pidx = _peer(x, y, c, k)
            for q in range(self.na):
                copies.append(pltpu.make_async_remote_copy(
                    src_ref=self._src(ins, q, pidx), dst_ref=outs[q].at[pidx if incoming else me],
                    send_sem=sems[0].at[q, k - 1], recv_sem=sems[1].at[q, k - 1], device_id=peer,
                    device_id_type=MESH))
        return copies

    def start(self, ins, outs, sems):
        for cp in self._local(ins, outs, sems) + self._remote(ins, outs, sems, incoming=False):
            cp.start()

    def wait(self, ins, outs, sems):
        for cp in self._remote(ins, outs, sems, incoming=True):
            cp.wait_recv()
        for cp in self._remote(ins, outs, sems, incoming=False):
            cp.wait_send()
        for cp in self._local(ins, outs, sems):
            cp.wait()


def exchange(scattered, gathered, name):
    ex = Exchange(scattered, gathered)

    def body(*refs):
        ins, outs, sems = refs[:ex.na], refs[ex.na:2 * ex.na], refs[2 * ex.na:]
        ex.start(ins, outs, sems)
        ex.wait(ins, outs, sems)

    return pl.pallas_call(body, name=name, in_specs=ex.in_specs, out_specs=ex.out_specs, out_shape=ex.out_shape,
                          scratch_shapes=ex.scratch)(*ex.arrays)


def adamw(parts, w, m, v, tile, name):
    npart, r, dp = parts.shape
    d = w.shape[1]

    def body(p_ref, w_ref, m_ref, v_ref, g_ref, dw_ref, nm_ref, nv_ref):
        g = p_ref[0, :, 0:d].astype(F32)
        for q in range(1, npart):
            g = g + p_ref[q, :, 0:d].astype(F32)
        g_ref[...] = g
        nm = ADAM_B1 * m_ref[...] + (1.0 - ADAM_B1) * g
        nv = ADAM_B2 * v_ref[...] + (1.0 - ADAM_B2) * (g * g)
        nm_ref[...] = nm
        nv_ref[...] = nv
        m_hat = nm / (1.0 - ADAM_B1 ** ADAM_STEP)
        v_hat = nv / (1.0 - ADAM_B2 ** ADAM_STEP)
        dw_ref[...] = -ADAM_LR * (m_hat / (jnp.sqrt(v_hat) + ADAM_EPS) + ADAM_WD * w_ref[...])

    rows = pl.BlockSpec((tile, d), lambda i: (i, 0))
    out = jax.ShapeDtypeStruct((r, d), F32)
    return pl.pallas_call(
        body, name=name, grid=(r // tile,),
        in_specs=[pl.BlockSpec((npart, tile, dp), lambda i: (0, i, 0)), rows, rows, rows],
        out_specs=[rows] * 4, out_shape=[out] * 4,
        compiler_params=_cparams(("parallel",)),
    )(parts, w, m, v)


def _pad_rows(a, rows):
    return jnp.pad(a, ((0, rows - a.shape[0]), (0, 0)))


def _pack_rest(w_att, w_sg, w_ssm, w_out):
    parts = []
    for l in range(2):
        parts += [w_att[l], w_sg[l], w_ssm[l], w_out[l]]
    return jnp.concatenate(parts, axis=0)


def _unpack_rest(p):
    outs = [[], [], [], []]
    o = 0
    for l in range(2):
        for q, rws in enumerate(REST_PARTS):
            outs[q].append(p[o:o + rws])
            o += rws
    return [jnp.stack(t) for t in outs]


def _pack_win(w_in):
    return jnp.pad(w_in.reshape(2 * D_MODEL, WIN_SHARD), ((0, 0), (0, WIN_LANES - WIN_SHARD)))


W_IN_MAP = ((0, 1024, "att", 0), (1024, 1280, "att", 2048), (1280, 2304, "att", 1024), (2304, 5376, "sg", 0),
            (5376, 7424, "ssm", 3072), (7424, 10496, "ssm", 0), (10496, 10528, "ssm", 5120), (10528, 13600, "gate", 0))
SLAB_COLS = {"att": ATT_COLS, "sg": SG_COLS, "ssm": SSM_COLS, "gate": GATE_COLS}


def _slabs_from_shards(g):
    slabs = {}
    for name, width in SLAB_COLS.items():
        pieces, filled = [], 0
        for ga, gb, _, off in sorted((m for m in W_IN_MAP if m[2] == name), key=lambda m: m[3]):
            assert off == filled
            a = ga
            while a < gb:
                d = a // WIN_SHARD
                hi = min(gb, WIN_SHARD * (d + 1))
                pieces.append(g[d, :, a - WIN_SHARD * d:hi - WIN_SHARD * d])
                a = hi
            filled += gb - ga
        if filled < width:
            pieces.append(jnp.zeros((D_MODEL, width - filled), g.dtype))
        slabs[name] = jnp.concatenate(pieces, axis=1)
    return slabs


def _shards_from_slabs(dslabs):
    out = []
    for d in range(N_DEV):
        a, b = WIN_SHARD * d, WIN_SHARD * (d + 1)
        pieces = []
        for ga, gb, name, off in W_IN_MAP:
            lo, hi = max(a, ga), min(b, gb)
            if lo < hi:
                pieces.append(dslabs[name][:, off + lo - ga:off + hi - ga])
        pieces.append(jnp.zeros((D_MODEL, WIN_LANES - WIN_SHARD), pieces[0].dtype))
        out.append(jnp.concatenate(pieces, axis=1).astype(WIRE_DTYPE))
    return jnp.stack(out)


SMALL_SIZES = (("norm_pre", 2048), ("norm_post", 2048), ("rel_bias", 512), ("att_sinks", 32), ("sg_ln_g", 2048),
               ("sg_ln_b", 2048), ("sg_w", 262144), ("sg_b", 2048), ("ssm_conv_b", 6144), ("ssm_dt_bias", 64),
               ("ssm_a_log", 64), ("ssm_d", 64), ("ssm_norm_g", 4096), ("conv_w_full", 24576))


def _pack_small(d):
    parts = []
    for name, size in SMALL_SIZES:
        rows = 8 * (-(-size // (8 * D_MODEL)))
        flat = d[name].reshape(-1) if name in d else jnp.zeros((size,), F32)
        parts.append(jnp.pad(flat, (0, rows * D_MODEL - size)).reshape(rows, D_MODEL))
    return _pad_rows(jnp.concatenate(parts, axis=0), SMALL_ROWS)


def _unpack_small(p, shapes):
    out, o = {}, 0
    for name, size in SMALL_SIZES:
        rows = 8 * (-(-size // (8 * D_MODEL)))
        if name in shapes:
            out[name] = p[o:o + rows].reshape(-1)[:size].reshape(shapes[name])
        o += rows
    return out


def _bucket_onehot_t():
    qi = jnp.arange(BLK, dtype=jnp.int32)[None, :]
    kj = jnp.arange(BLK, dtype=jnp.int32)[:, None]
    dd = (qi - kj) & (BLK - 1)
    in_window = dd >= 0
    max_exact = REL_BUCKETS // 2
    dist_f = jnp.maximum(dd, 1).astype(F32)
    large = max_exact + (jnp.log(dist_f / max_exact) / math.log(128 / max_exact)
                         * (REL_BUCKETS - max_exact)).astype(jnp.int32)
    large = jnp.minimum(large, REL_BUCKETS - 1)
    bucket = jnp.where(dd < max_exact, dd, large).reshape(1, -1)
    onehot_t = (bucket == jnp.arange(REL_BUCKETS, dtype=jnp.int32)[:, None]).astype(F32)
    maskadd = jnp.where(in_window, 0.0, NEG).astype(F32).reshape(1, -1)
    return onehot_t, maskadd


WEIGHTS = ['w_in', 'norm_pre', 'norm_post', 'rel_bias', 'att_sinks', 'sg_ln_g', 'sg_ln_b', 'sg_w', 'sg_b',
           'ssm_conv_w', 'ssm_conv_b', 'ssm_dt_bias', 'ssm_a_log', 'ssm_d', 'ssm_norm_g',
           'w_br_att', 'w_br_sg', 'w_br_ssm', 'w_out']
REST = ('w_br_att', 'w_br_sg', 'w_br_ssm', 'w_out')


def kernel(x, w_in, norm_pre, norm_post, rel_bias, att_sinks, sg_ln_g, sg_ln_b, sg_w, sg_b, ssm_conv_w, ssm_conv_b, ssm_dt_bias, ssm_a_log, ssm_d, ssm_norm_g, w_br_att, w_br_sg, w_br_ssm, w_out, loss_target, m_w_in, m_norm_pre, m_norm_post, m_rel_bias, m_att_sinks, m_sg_ln_g, m_sg_ln_b, m_sg_w, m_sg_b, m_ssm_conv_w, m_ssm_conv_b, m_ssm_dt_bias, m_ssm_a_log, m_ssm_d, m_ssm_norm_g, m_w_br_att, m_w_br_sg, m_w_br_ssm, m_w_out, v_w_in, v_norm_pre, v_norm_post, v_rel_bias, v_att_sinks, v_sg_ln_g, v_sg_ln_b, v_sg_w, v_sg_b, v_ssm_conv_w, v_ssm_conv_b, v_ssm_dt_bias, v_ssm_a_log, v_ssm_d, v_ssm_norm_g, v_w_br_att, v_w_br_sg, v_w_br_ssm, v_w_out):
    w = dict(w_in=w_in, norm_pre=norm_pre, norm_post=norm_post, rel_bias=rel_bias, att_sinks=att_sinks,
             sg_ln_g=sg_ln_g, sg_ln_b=sg_ln_b, sg_w=sg_w, sg_b=sg_b, ssm_conv_w=ssm_conv_w, ssm_conv_b=ssm_conv_b,
             ssm_dt_bias=ssm_dt_bias, ssm_a_log=ssm_a_log, ssm_d=ssm_d, ssm_norm_g=ssm_norm_g,
             w_br_att=w_br_att, w_br_sg=w_br_sg, w_br_ssm=w_br_ssm, w_out=w_out)
    mom = dict(w_in=m_w_in, norm_pre=m_norm_pre, norm_post=m_norm_post, rel_bias=m_rel_bias, att_sinks=m_att_sinks,
               sg_ln_g=m_sg_ln_g, sg_ln_b=m_sg_ln_b, sg_w=m_sg_w, sg_b=m_sg_b, ssm_conv_w=m_ssm_conv_w,
               ssm_conv_b=m_ssm_conv_b, ssm_dt_bias=m_ssm_dt_bias, ssm_a_log=m_ssm_a_log, ssm_d=m_ssm_d,
               ssm_norm_g=m_ssm_norm_g, w_br_att=m_w_br_att, w_br_sg=m_w_br_sg, w_br_ssm=m_w_br_ssm, w_out=m_w_out)
    var = dict(w_in=v_w_in, norm_pre=v_norm_pre, norm_post=v_norm_post, rel_bias=v_rel_bias, att_sinks=v_att_sinks,
               sg_ln_g=v_sg_ln_g, sg_ln_b=v_sg_ln_b, sg_w=v_sg_w, sg_b=v_sg_b, ssm_conv_w=v_ssm_conv_w,
               ssm_conv_b=v_ssm_conv_b, ssm_dt_bias=v_ssm_dt_bias, ssm_a_log=v_ssm_a_log, ssm_d=v_ssm_d,
               ssm_norm_g=v_ssm_norm_g, w_br_att=v_w_br_att, w_br_sg=v_w_br_sg, w_br_ssm=v_w_br_ssm, w_out=v_w_out)
    xs0 = x[0]
    target = loss_target[0]
    my_dev = 4 * lax.axis_index("x") + 2 * lax.axis_index("y") + lax.axis_index("c")

    conv_shard = _pad_rows(ssm_conv_w.reshape(-1, D_MODEL), 8)
    win_shard = _pack_win(w_in).astype(WIRE_DTYPE)
    rest_shard = _pack_rest(*[w[n] for n in REST]).astype(WIRE_DTYPE)
    layer_shards = [[win_shard[l * D_MODEL:(l + 1) * D_MODEL], rest_shard[l * LAYER_REST:(l + 1) * LAYER_REST]]
                    for l in range(2)]
    g_win0, g_rest0, gathered_conv = exchange([], layer_shards[0] + [conv_shard], "all_gather")
    conv_full = gathered_conv[:, 0:3].reshape(N_DEV, 2, 4, 384).transpose(1, 2, 0, 3).reshape(2, 4, CONV_DIM)

    def layer_weights(l, g_win, g_rest):
        slabs = _slabs_from_shards(g_win)
        lw = {"in_" + name: slab.astype(MXU_DTYPE) for name, slab in slabs.items()}
        o = 0
        for name, rws in zip(("att", "sg", "ssm", "out"), REST_PARTS):
            lw[name] = g_rest[:, o:o + rws].reshape(N_DEV * rws, D_MODEL).astype(MXU_DTYPE)
            o += rws
        tril = jnp.tril(jnp.ones((BLK, BLK), bool))
        sgw = jnp.where(tril[None], sg_w[l], 0.0)
        lw.update(
            g_pre=norm_pre[l][None], g_post=norm_post[l][None], sinks=jnp.repeat(att_sinks[l], BLK).reshape(2, GROUP_LANES),
            ln_g=sg_ln_g[l][None], ln_b=sg_ln_b[l][None], sgw=sgw.astype(MXU_DTYPE),
            sgw_t=sgw.transpose(0, 2, 1).astype(MXU_DTYPE), sgb_t=_pad_lanes(sg_b[l].T),
            cw=conv_full[l], cb=ssm_conv_b[l][None], dtb=_pad_lanes(ssm_dt_bias[l][None]),
            alog=_pad_lanes(ssm_a_log[l][None]), dsk=_pad_lanes(ssm_d[l][None]), ng=ssm_norm_g[l][None])
        return lw

    onehot_t, maskadd = _bucket_onehot_t()
    bias = bias_table(rel_bias.T, onehot_t, maskadd).reshape(2, GROUP_HEADS, BLK, BLK).transpose(0, 2, 1, 3)
    bias = bias.reshape(2, BLK, GROUP_LANES)

    saved = []
    xl = xs0
    layers = [layer_weights(0, g_win0, g_rest0)]
    for l in range(2):
        lw = layers[l]
        h = rmsnorm_fwd(xl, lw["g_pre"])
        pa = mm_nn(h, lw["in_att"], 1152, "proj_att").T
        ps = mm_nn(h, lw["in_sg"], 1536, "proj_sg")
        pm = mm_nn(h, lw["in_ssm"], 1792, "proj_ssm")
        pg = mm_nn(h, lw["in_gate"], 1536, "proj_gate")
        ya = attn_fwd(pa, bias, lw["sinks"]).T
        ys = sgu_fwd(ps, lw["ln_g"], lw["ln_b"], lw["sgw"], lw["sgb_t"])
        ssd_args = (pm, lw["cw"], lw["cb"], lw["dtb"], lw["alog"], lw["dsk"], lw["ng"])
        if l == 0:
            ym, states, (g_win1, g_rest1) = ssd_fwd(*ssd_args, Exchange([], layer_shards[1]))
            layers.append(layer_weights(1, g_win1, g_rest1))
        else:
            ym, states = ssd_fwd(*ssd_args)
        x_next, ba, bs, bm, merged, out_s = merge_fwd(xl, ya, ys, ym, pg, lw["att"], lw["sg"], lw["ssm"], lw["out"],
                                                      lw["g_post"])
        saved.append(dict(x=xl, h=h, pa=pa, ps=ps, pm=pm, pg=pg, ya=ya, ys=ys, ym=ym, states=states, ba=ba, bs=bs,
                          bm=bm, merged=merged, out_s=out_s))
        xl = x_next

    dx, loss_part = loss_grad(xl, target)
    loss = lax.psum(loss_part[0, 0], ("x", "y", "c"))

    dbias = jnp.zeros((2, BLK, GROUP_LANES), F32)
    win_grads, rest_grads = [None, None], [None, None]
    small = {n: [None, None] for n in ("norm_pre", "norm_post", "att_sinks", "sg_ln_g", "sg_ln_b", "sg_w", "sg_b",
                                       "ssm_conv_b", "ssm_dt_bias", "ssm_a_log", "ssm_d", "ssm_norm_g",
                                       "conv_w_full")}
    for l in (1, 0):
        lw, sv = layers[l], saved[l]
        dout, dba, dbs, dbm, dpg, dya, dys, dym, dg_post = merge_bwd(
            dx, sv["out_s"], sv["pg"], sv["ba"], sv["bs"], sv["bm"], lw["att"], lw["sg"], lw["ssm"], lw["out"],
            lw["g_post"])
        dw_out = mm_tn(sv["merged"], dout, 1024, "dw_out")
        dw_att = mm_tn(sv["ya"], dba, 1024, "dw_br_att")
        dw_sg = mm_tn(sv["ys"], dbs, 1024, "dw_br_sg")
        dw_ssm = mm_tn(sv["ym"], dbm, 1024, "dw_br_ssm")
        rest_grads[l] = jnp.concatenate(
            [dw_att.reshape(N_DEV, 128, D_MODEL), dw_sg.reshape(N_DEV, 128, D_MODEL),
             dw_ssm.reshape(N_DEV, 256, D_MODEL), dw_out.reshape(N_DEV, 128, D_MODEL)], axis=1).astype(WIRE_DTYPE)
        dpa_t, dbias, dsinks = attn_bwd(sv["pa"], dya.T, bias, lw["sinks"], dbias)
        dpa = dpa_t.T
        dps, dsgw, dsgb_t, dln_g, dln_b = sgu_bwd(sv["ps"], dys, lw["ln_g"], lw["ln_b"], lw["sgw"], lw["sgw_t"],
                                                  lw["sgb_t"])
        ssd_args = (sv["pm"], dym, sv["states"], lw["cw"], lw["cb"], lw["dtb"], lw["alog"], lw["dsk"], lw["ng"])
        if l == 0:
            dpm, dcw, dcb, dvec, dng, (recv_win1, recv_rest1, recv_rest0) = ssd_bwd(
                *ssd_args, Exchange([win_grads[1], rest_grads[1], rest_grads[0]], []))
        else:
            dpm, dcw, dcb, dvec, dng = ssd_bwd(*ssd_args)
        dslabs = dict(att=mm_tn(sv["h"], dpa, 2304, "dw_in_att"), sg=mm_tn(sv["h"], dps, 3072, "dw_in_sg"),
                      ssm=mm_tn(sv["h"], dpm, 2688, "dw_in_ssm"), gate=mm_tn(sv["h"], dpg, 3072, "dw_in_gate"))
        win_grads[l] = _shards_from_slabs(dslabs)
        dh_args = ([dpa, dps, dpm, dpg], [lw["in_att"], lw["in_sg"], lw["in_ssm"], lw["in_gate"]], sv["x"],
                   lw["g_pre"], dx)
        if l == 0:
            dx, dg_pre, (recv_win0,) = dh_norm_bwd(*dh_args, Exchange([win_grads[0]], []))
        else:
            dx, dg_pre = dh_norm_bwd(*dh_args)
        small["norm_pre"][l] = dg_pre[0]
        small["norm_post"][l] = dg_post[0]
        small["att_sinks"][l] = dsinks[0, :ATT_HEADS]
        small["sg_ln_g"][l] = dln_g[0]
        small["sg_ln_b"][l] = dln_b[0]
        small["sg_w"][l] = dsgw
        small["sg_b"][l] = dsgb_t[:, :SG_GROUPS].T
        small["ssm_conv_b"][l] = dcb[0]
        small["ssm_dt_bias"][l] = dvec[0, :SSM_HEADS]
        small["ssm_a_log"][l] = dvec[1, :SSM_HEADS]
        small["ssm_d"][l] = dvec[2, :SSM_HEADS]
        small["ssm_norm_g"][l] = dng[0]
        small["conv_w_full"][l] = dcw[0:4]
    grad_x = dx
    dbias = dbias.reshape(2, BLK, GROUP_HEADS, BLK).transpose(0, 2, 1, 3).reshape(ATT_HEADS, BLK * BLK)
    d_rel_bias = bias_table_bwd(dbias, onehot_t).T

    small_d = {n: jnp.stack(v) for n, v in small.items()}
    small_d["rel_bias"] = d_rel_bias
    (recv_small,) = exchange([], [_pack_small(small_d)], "gather_small_grads")

    res_win, res_rest = [], []
    for l, (recv_win, recv_rest) in enumerate(((recv_win0, recv_rest0), (recv_win1, recv_rest1))):
        res_win.append(adamw(recv_win, w_in[l], m_w_in[l], v_w_in[l], WIN_TILE, "adamw_w_in"))
        layer_rest = lambda t: jnp.concatenate([t[n][l] for n in REST], axis=0)
        res_rest.append(adamw(recv_rest, layer_rest(w), layer_rest(mom), layer_rest(var), REST_TILE, "adamw_rest"))
    res_win = [jnp.stack([res_win[0][q], res_win[1][q]]) for q in range(4)]
    res_rest = [jnp.concatenate([res_rest[0][q], res_rest[1][q]], axis=0) for q in range(4)]
    small_names = [n for n, _ in SMALL_SIZES if n != "conv_w_full"]
    g_s, dw_s, nm_s, nv_s = adamw(recv_small, _pack_small({n: w[n] for n in small_names}),
                                  _pack_small({n: mom[n] for n in small_names}),
                                  _pack_small({n: var[n] for n in small_names}), SMALL_TILE, "adamw_small")
    shapes = {n: w[n].shape for n in small_names}
    shapes["conv_w_full"] = (2, 4, CONV_DIM)
    g_conv_full = _unpack_small(g_s, shapes)["conv_w_full"]
    g_conv = lax.dynamic_slice_in_dim(g_conv_full, my_dev * 384, 384, axis=2)
    pack_conv = lambda a: _pad_rows(a.reshape(-1, D_MODEL), 8)
    g_c, dw_c, nm_c, nv_c = adamw(pack_conv(g_conv)[None], pack_conv(ssm_conv_w), pack_conv(m_ssm_conv_w),
                                  pack_conv(v_ssm_conv_w), 8, "adamw_conv")

    results = {}
    for q, (tag, psm, pc) in enumerate((("grad", g_s, g_c), ("delta", dw_s, dw_c), ("new_m", nm_s, nm_c),
                                        ("new_v", nv_s, nv_c))):
        r = dict(zip(REST, _unpack_rest(res_rest[q])))
        r["w_in"] = res_win[q]
        r.update(_unpack_small(psm, {n: w[n].shape for n in small_names}))
        r["ssm_conv_w"] = pc[0:3].reshape(2, 4, 384)
        results[tag] = r
    outs = [loss, grad_x[None]]
    for tag in ("grad", "delta", "new_m", "new_v"):
        outs += [results[tag][n] for n in WEIGHTS]
    return tuple(outs)
```

```python
import math

import jax
import jax.numpy as jnp
from jax import lax
from jax.experimental import pallas as pl
from jax.experimental.pallas import tpu as pltpu

F32 = jnp.float32
MXU_DTYPE = jnp.bfloat16
ACT_DTYPE = jnp.bfloat16
WIRE_DTYPE = jnp.bfloat16
HI = lax.Precision.HIGHEST
MESH = pl.DeviceIdType.MESH

D_MODEL = 1024
N_DEV = 8
ATT_HEADS = 16
HEAD_DIM = 64
BLK = 128
SG_GROUPS = 8
SSM_WIDTH = 2048
SSM_HEADS = 32
SSM_GROUPS = 4
SSM_GW = SSM_WIDTH // SSM_GROUPS
CONV_DIM = 3072
REL_BUCKETS = 32
EPS = 1e-6
NEG = -1e30

ATT_COLS = 2304
SG_COLS = 3072
SSM_COLS = 5376
GATE_COLS = 3072
DT_OFF = 5120

VMEM_LIMIT_V7X = 56 * 2 ** 20

ADAM_LR, ADAM_B1, ADAM_B2, ADAM_EPS, ADAM_WD, ADAM_STEP = 0.001, 0.9, 0.999, 1e-08, 0.01, 10

WIN_SHARD = 1700
WIN_LANES = 1792
REST_PARTS = (128, 128, 256, 128)
LAYER_REST = sum(REST_PARTS)
REST_TILE = 128
WIN_TILE = 128
SMALL_ROWS = 384
SMALL_TILE = 128


def _cparams(sem=None):
    return pltpu.CompilerParams(dimension_semantics=sem, vmem_limit_bytes=VMEM_LIMIT_V7X)


def _dot(a, b):
    return jnp.dot(a.astype(MXU_DTYPE), b.astype(MXU_DTYPE), preferred_element_type=F32)


def _dot_nt(a, b):
    return lax.dot_general(a.astype(MXU_DTYPE), b.astype(MXU_DTYPE), (((1,), (1,)), ((), ())),
                           preferred_element_type=F32)


def _dot_tn(a, b):
    return lax.dot_general(a.astype(MXU_DTYPE), b.astype(MXU_DTYPE), (((0,), (0,)), ((), ())),
                           preferred_element_type=F32)


def _dot_hi(a, b):
    return jnp.dot(a, b, precision=HI, preferred_element_type=F32)


def _dot_onehot(a, onehot):
    hi = a.astype(jnp.bfloat16)
    lo = (a - hi.astype(F32)).astype(jnp.bfloat16)
    return (jnp.dot(hi, onehot, preferred_element_type=F32) + jnp.dot(lo, onehot, preferred_element_type=F32))


def _dot_hi_nt(a, b):
    return lax.dot_general(a, b, (((1,), (1,)), ((), ())), precision=HI, preferred_element_type=F32)


def _sig(x):
    return 1.0 / (1.0 + jnp.exp(-x))


def _dsilu(x, s):
    return s * (1.0 + x * (1.0 - s))


def _full(shape):
    nd = len(shape)
    return pl.BlockSpec(shape, lambda *_: (0,) * nd)


def rmsnorm_fwd(x, g):
    s, d = x.shape
    tm = min(512, s)

    def body(x_ref, g_ref, o_ref):
        xv = x_ref[...]
        r = lax.rsqrt(jnp.mean(xv * xv, axis=-1, keepdims=True) + EPS)
        o_ref[...] = (xv * r * g_ref[...]).astype(o_ref.dtype)

    return pl.pallas_call(
        body, name="rmsnorm_fwd", grid=(s // tm,),
        in_specs=[pl.BlockSpec((tm, d), lambda i: (i, 0)), _full((1, d))],
        out_specs=pl.BlockSpec((tm, d), lambda i: (i, 0)),
        out_shape=jax.ShapeDtypeStruct((s, d), ACT_DTYPE),
        compiler_params=_cparams(("parallel",)),
    )(x, g)


def mm_nn(a, b, tn, name):
    s, k = a.shape
    n = b.shape[1]
    tm = min(2048, s)

    def body(a_ref, b_ref, o_ref):
        o_ref[...] = _dot(a_ref[...], b_ref[...]).astype(o_ref.dtype)

    return pl.pallas_call(
        body, name=name, grid=(s // tm, n // tn),
        in_specs=[pl.BlockSpec((tm, k), lambda i, j: (i, 0)), pl.BlockSpec((k, tn), lambda i, j: (0, j))],
        out_specs=pl.BlockSpec((tm, tn), lambda i, j: (i, j)),
        out_shape=jax.ShapeDtypeStruct((s, n), ACT_DTYPE),
        compiler_params=_cparams(("parallel", "arbitrary")),
    )(a, b)


def mm_tn(a, b, tn, name):
    s, k = a.shape
    n = b.shape[1]
    ts = min(512, s)

    def body(a_ref, b_ref, o_ref):
        @pl.when(pl.program_id(1) == 0)
        def _():
            o_ref[...] = jnp.zeros_like(o_ref)

        o_ref[...] += _dot_tn(a_ref[...], b_ref[...])

    return pl.pallas_call(
        body, name=name, grid=(n // tn, s // ts),
        in_specs=[pl.BlockSpec((ts, k), lambda j, t: (t, 0)), pl.BlockSpec((ts, tn), lambda j, t: (t, j))],
        out_specs=pl.BlockSpec((k, tn), lambda j, t: (0, j)),
        out_shape=jax.ShapeDtypeStruct((k, n), F32),
        compiler_params=_cparams(("parallel", "arbitrary")),
    )(a, b)


def dh_norm_bwd(dslabs, wslabs, x, g, dres, ex=None):
    s, d = x.shape
    tm = min(1024, s)
    tk = 768
    counts = [ds.shape[1] // tk for ds in dslabs]
    starts = [sum(counts[:i]) for i in range(len(counts))]
    nk = sum(counts)
    ns = len(dslabs)

    hosted = ex is not None
    ni = s // tm

    def mm_body(*refs):
        (own_in, (dh_ref,), _), hosted_refs = _split_hosted(refs, 2 * ns, 1, 0, ex)
        d_refs, w_refs = own_in[:ns], own_in[ns:]
        i, k = pl.program_id(0), pl.program_id(1)
        if hosted:
            @pl.when((i == 0) & (k == 0))
            def _():
                ex.start(*hosted_refs)

            @pl.when((i == ni - 1) & (k == nk - 1))
            def _():
                ex.relay(*hosted_refs)
                ex.wait(*hosted_refs)

        @pl.when(k == 0)
        def _():
            dh_ref[...] = jnp.zeros_like(dh_ref)

        for q in range(ns):
            @pl.when((k >= starts[q]) & (k < starts[q] + counts[q]))
            def _(q=q):
                dh_ref[...] += _dot_nt(d_refs[q][...], w_refs[q][...])

    def clamp(q):
        return lambda i, k: (i, jnp.clip(k - starts[q], 0, counts[q] - 1))

    def clamp_w(q):
        return lambda i, k: (0, jnp.clip(k - starts[q], 0, counts[q] - 1))

    res = pl.pallas_call(
        mm_body, name="dh_matmul_scatter" if hosted else "dh_matmul", grid=(ni, nk),
        in_specs=([pl.BlockSpec((tm, tk), clamp(q)) for q in range(ns)]
                  + [pl.BlockSpec((d, tk), clamp_w(q)) for q in range(ns)] + (ex.in_specs if hosted else [])),
        out_specs=[pl.BlockSpec((tm, d), lambda i, k: (i, 0))] + (ex.out_specs if hosted else []),
        out_shape=[jax.ShapeDtypeStruct((s, d), F32)] + (ex.out_shape if hosted else []),
        scratch_shapes=ex.scratch if hosted else [],
        compiler_params=_cparams(("arbitrary" if hosted else "parallel", "arbitrary")),
    )(*dslabs, *wslabs, *(ex.arrays if hosted else []))
    dh, ex_results = res[0], res[1:]

    te = min(512, s)

    def norm_body(dh_ref, x_ref, g_ref, dres_ref, dx_ref, dg_ref):
        @pl.when(pl.program_id(0) == 0)
        def _():
            dg_ref[...] = jnp.zeros_like(dg_ref)

        xv = x_ref[...]
        r = lax.rsqrt(jnp.mean(xv * xv, axis=-1, keepdims=True) + EPS)
        xn = xv * r
        dhv = dh_ref[...]
        dg_ref[...] += jnp.sum(dhv * xn, axis=0, keepdims=True)
        dxn = dhv * g_ref[...]
        dx_ref[...] = dres_ref[...] + r * (dxn - xn * jnp.mean(dxn * xn, axis=-1, keepdims=True))

    rows = pl.BlockSpec((te, d), lambda i: (i, 0))
    dx, dg = pl.pallas_call(
        norm_body, name="norm_bwd", grid=(s // te,),
        in_specs=[rows, rows, _full((1, d)), rows],
        out_specs=[rows, _full((1, d))],
        out_shape=[jax.ShapeDtypeStruct((s, d), F32), jax.ShapeDtypeStruct((1, d), F32)],
        compiler_params=_cparams(("arbitrary",)),
    )(dh, x, g, dres)
    return (dx, dg, ex_results) if hosted else (dx, dg)


def bias_table(rel_bias_t, onehot_t, maskadd):
    n = onehot_t.shape[1]
    tn = 8192

    def body(r_ref, o_ref, m_ref, out_ref):
        out_ref[...] = _dot_hi(r_ref[...], o_ref[...]) + m_ref[...]

    return pl.pallas_call(
        body, name="bias_table", grid=(n // tn,),
        in_specs=[_full((ATT_HEADS, REL_BUCKETS)), pl.BlockSpec((REL_BUCKETS, tn), lambda j: (0, j)),
                  pl.BlockSpec((1, tn), lambda j: (0, j))],
        out_specs=pl.BlockSpec((ATT_HEADS, tn), lambda j: (0, j)),
        out_shape=jax.ShapeDtypeStruct((ATT_HEADS, n), F32),
        compiler_params=_cparams(("parallel",)),
    )(rel_bias_t, onehot_t, maskadd)


def bias_table_bwd(dbias, onehot_t):
    n = onehot_t.shape[1]
    tn = 8192

    def body(d_ref, o_ref, out_ref):
        @pl.when(pl.program_id(0) == 0)
        def _():
            out_ref[...] = jnp.zeros_like(out_ref)

        out_ref[...] += _dot_hi_nt(d_ref[...], o_ref[...])

    return pl.pallas_call(
        body, name="bias_table_bwd", grid=(n // tn,),
        in_specs=[pl.BlockSpec((ATT_HEADS, tn), lambda j: (0, j)), pl.BlockSpec((REL_BUCKETS, tn), lambda j: (0, j))],
        out_specs=_full((ATT_HEADS, REL_BUCKETS)),
        out_shape=jax.ShapeDtypeStruct((ATT_HEADS, REL_BUCKETS), F32),
        compiler_params=_cparams(("arbitrary",)),
    )(dbias, onehot_t)


def _fold(full, tri):
    return jnp.where(tri, full[BLK:2 * BLK], full[0:BLK])


def _unfold(folded, tri):
    return jnp.concatenate([jnp.where(tri, 0.0, folded), jnp.where(tri, folded, 0.0)], axis=0)


GROUP_HEADS = ATT_HEADS // 2
GROUP_LANES = GROUP_HEADS * BLK


def _att_group(qg, kcat, vt_cat, bias_g, sink_g, tri, no_prev):
    l = _fold(_dot(kcat, qg), tri) * (HEAD_DIM ** -0.5) + bias_g
    l = jnp.where(no_prev, NEG, l)
    m = jnp.maximum(jnp.max(l, axis=0, keepdims=True), sink_g)
    p = jnp.exp(l - m)
    es = jnp.exp(sink_g - m)
    inv = 1.0 / (jnp.sum(p, axis=0, keepdims=True) + es)
    p = p * inv
    pcat = _unfold(p, tri)
    return p, pcat, es * inv, _dot(vt_cat, pcat)


def _heads_to_lanes(ref, row0):
    return jnp.concatenate([ref[row0 + j * HEAD_DIM:row0 + (j + 1) * HEAD_DIM, :] for j in range(GROUP_HEADS)], axis=1)


def _lanes_to_heads(ref, row0, val):
    for j in range(GROUP_HEADS):
        ref[row0 + j * HEAD_DIM:row0 + (j + 1) * HEAD_DIM, :] = val[:, j * BLK:(j + 1) * BLK].astype(ref.dtype)


def _kv_cat(kvp, kvc, g):
    lo = g * HEAD_DIM
    kt_cat = jnp.concatenate([kvp[lo:lo + HEAD_DIM], kvc[lo:lo + HEAD_DIM]], axis=1)
    vt_cat = jnp.concatenate([kvp[128 + lo:128 + lo + HEAD_DIM], kvc[128 + lo:128 + lo + HEAD_DIM]], axis=1)
    return kt_cat, vt_cat


def _tri_masks(n):
    row = lax.broadcasted_iota(jnp.int32, (BLK, GROUP_LANES), 0)
    query = lax.broadcasted_iota(jnp.int32, (BLK, GROUP_LANES), 1) & (BLK - 1)
    tri = row <= query
    return tri, (n == 0) & jnp.logical_not(tri)


def _split_hosted(refs, n_in, n_out, n_scratch, ex):
    na = ex.na if ex is not None else 0
    o = 0
    parts = []
    for cnt in (n_in, na, n_out, na, n_scratch, 3 if ex is not None else 0):
        parts.append(refs[o:o + cnt])
        o += cnt
    own_in, ex_in, own_out, ex_out, own_scr, ex_sems = parts
    return (own_in, own_out, own_scr), (ex_in, ex_out, ex_sems)


def _call_hosting(body, name, nsteps, in_specs, out_specs, out_shape, scratch, args, ex):
    n_in, n_out, n_scr = len(in_specs), len(out_specs), len(scratch)
    hosted = ex is not None

    def full_body(*refs):
        (own_in, own_out, own_scr), hosted_refs = _split_hosted(refs, n_in, n_out, n_scr, ex)
        if hosted:
            @pl.when(pl.program_id(0) == 0)
            def _():
                ex.start(*hosted_refs)

            @pl.when(pl.program_id(0) == max(nsteps - 4, 0))
            def _():
                ex.relay(*hosted_refs)

            @pl.when(pl.program_id(0) == nsteps - 1)
            def _():
                ex.wait(*hosted_refs)

        body(*own_in, *own_out, *own_scr)

    res = pl.pallas_call(
        full_body, name=name + "_hosting" if hosted else name, grid=(nsteps,),
        in_specs=list(in_specs) + (ex.in_specs if hosted else []),
        out_specs=list(out_specs) + (ex.out_specs if hosted else []),
        out_shape=list(out_shape) + (ex.out_shape if hosted else []),
        scratch_shapes=list(scratch) + (ex.scratch if hosted else []),
        compiler_params=_cparams(("arbitrary",)),
    )(*args, *(ex.arrays if hosted else []))
    return res[:n_out], res[n_out:]


def attn_fwd(pa, bias, sinks):
    s = pa.shape[1]
    nb = s // BLK

    def body(pa_ref, kvp_ref, bias_ref, sink_ref, y_ref):
        n = pl.program_id(0)
        kvc = pa_ref[2048:2304, :]
        kvp = kvp_ref[...]
        tri, no_prev = _tri_masks(n)
        for g in range(2):
            kt_cat, vt_cat = _kv_cat(kvp, kvc, g)
            row0 = g * GROUP_HEADS * HEAD_DIM
            _, _, _, o = _att_group(_heads_to_lanes(pa_ref, row0), kt_cat.astype(F32).T, vt_cat, bias_ref[g],
                                    sink_ref[g:g + 1, :], tri, no_prev)
            z = _heads_to_lanes(pa_ref, 1024 + row0).astype(F32)
            _lanes_to_heads(y_ref, row0, o * z * _sig(z))

    return pl.pallas_call(
        body, name="attn_fwd", grid=(nb,),
        in_specs=[pl.BlockSpec((ATT_COLS, BLK), lambda n: (0, n)),
                  pl.BlockSpec((256, BLK), lambda n: (8, jnp.maximum(n - 1, 0))),
                  _full((2, BLK, GROUP_LANES)), _full((2, GROUP_LANES))],
        out_specs=pl.BlockSpec((1024, BLK), lambda n: (0, n)),
        out_shape=jax.ShapeDtypeStruct((1024, s), ACT_DTYPE),
        compiler_params=_cparams(("parallel",)),
    )(pa, pa, bias, sinks)


def attn_bwd(pa, dy, bias, sinks, dbias_in):
    s = pa.shape[1]
    nb = s // BLK

    def body(pa_ref, kvp_ref, dy_ref, bias_ref, sink_ref, dbin_ref, dpa_ref, dbias_ref, dsink_ref, carry, dsink_acc):
        i = pl.program_id(0)
        n = nb - 1 - i

        @pl.when(i == 0)
        def _():
            dbias_ref[...] = dbin_ref[...]
            dsink_acc[...] = jnp.zeros_like(dsink_acc)
            carry[...] = jnp.zeros_like(carry)

        kvc = pa_ref[2048:2304, :]
        kvp = kvp_ref[...]
        tri, no_prev = _tri_masks(n)
        scale = HEAD_DIM ** -0.5
        for g in range(2):
            kt_cat, vt_cat = _kv_cat(kvp, kvc, g)
            row0 = g * GROUP_HEADS * HEAD_DIM
            qg = _heads_to_lanes(pa_ref, row0)
            p, pcat, psink, o = _att_group(qg, kt_cat.astype(F32).T, vt_cat, bias_ref[g], sink_ref[g:g + 1, :], tri,
                                           no_prev)
            z = _heads_to_lanes(pa_ref, 1024 + row0).astype(F32)
            dyg = _heads_to_lanes(dy_ref, row0).astype(F32)
            sz = _sig(z)
            d_o = dyg * z * sz
            _lanes_to_heads(dpa_ref, 1024 + row0, dyg * _dsilu(z, sz) * o)
            delta = jnp.sum(d_o * o, axis=0, keepdims=True)
            dl = p * (_fold(_dot(vt_cat.astype(F32).T, d_o), tri) - delta)
            dsink_acc[g:g + 1, :] += psink * delta
            dbias_ref[g] += dl
            dlcat = _unfold(dl, tri)
            _lanes_to_heads(dpa_ref, row0, _dot(kt_cat, dlcat) * scale)
            for q, dkv in enumerate((_dot_nt(qg, dlcat) * scale, _dot_nt(d_o, pcat))):
                r0 = q * 128 + g * HEAD_DIM
                dpa_ref[2048 + r0:2048 + r0 + HEAD_DIM, :] = (
                    dkv[:, BLK:2 * BLK] + carry[r0:r0 + HEAD_DIM, :]).astype(dpa_ref.dtype)
                carry[r0:r0 + HEAD_DIM, :] = dkv[:, 0:BLK]

        @pl.when(i == nb - 1)
        def _():
            lane = lax.broadcasted_iota(jnp.int32, (1, 128), 1)
            dsink = jnp.zeros((1, 128), F32)
            for h in range(ATT_HEADS):
                g, j = divmod(h, GROUP_HEADS)
                tot = jnp.sum(dsink_acc[g:g + 1, j * BLK:(j + 1) * BLK], axis=1, keepdims=True)
                dsink = dsink + jnp.where(lane == h, -tot, 0.0)
            dsink_ref[...] = dsink

    return pl.pallas_call(
        body, name="attn_bwd", grid=(nb,),
        in_specs=[pl.BlockSpec((ATT_COLS, BLK), lambda i: (0, nb - 1 - i)),
                  pl.BlockSpec((256, BLK), lambda i: (8, jnp.maximum(nb - 2 - i, 0))),
                  pl.BlockSpec((1024, BLK), lambda i: (0, nb - 1 - i)),
                  _full((2, BLK, GROUP_LANES)), _full((2, GROUP_LANES)), _full((2, BLK, GROUP_LANES))],
        out_specs=[pl.BlockSpec((ATT_COLS, BLK), lambda i: (0, nb - 1 - i)),
                   _full((2, BLK, GROUP_LANES)), _full((1, 128))],
        out_shape=[jax.ShapeDtypeStruct((ATT_COLS, s), ACT_DTYPE),
                   jax.ShapeDtypeStruct((2, BLK, GROUP_LANES), F32),
                   jax.ShapeDtypeStruct((1, 128), F32)],
        scratch_shapes=[pltpu.VMEM((256, BLK), F32), pltpu.VMEM((2, GROUP_LANES), F32)],
        compiler_params=_cparams(("arbitrary",)),
    )(pa, pa, dy, bias, sinks, dbias_in)


def _layernorm(v, g, b):
    mu = jnp.mean(v, axis=-1, keepdims=True)
    vc = v - mu
    rstd = lax.rsqrt(jnp.mean(vc * vc, axis=-1, keepdims=True) + EPS)
    xhat = vc * rstd
    return xhat, rstd, xhat * g + b


def sgu_fwd(ps, ln_g, ln_b, w_tril, b_t):
    s = ps.shape[0]

    def body(ps_ref, g_ref, b_ref, w_ref, bt_ref, y_ref):
        u = ps_ref[:, 0:1024].astype(F32)
        v = ps_ref[:, 1024:2048].astype(F32)
        z = ps_ref[:, 2048:3072].astype(F32)
        _, _, vn = _layernorm(v, g_ref[...], b_ref[...])
        gate = u * z * _sig(z)
        for g in range(SG_GROUPS):
            sl = slice(g * 128, (g + 1) * 128)
            mixed = _dot(w_ref[g], vn[:, sl]) + bt_ref[:, g:g + 1]
            y_ref[:, sl] = (gate[:, sl] * mixed).astype(y_ref.dtype)

    return pl.pallas_call(
        body, name="sgu_fwd", grid=(s // BLK,),
        in_specs=[pl.BlockSpec((BLK, SG_COLS), lambda c: (c, 0)), _full((1, 1024)), _full((1, 1024)),
                  _full((SG_GROUPS, BLK, BLK)), _full((BLK, 128))],
        out_specs=pl.BlockSpec((BLK, 1024), lambda c: (c, 0)),
        out_shape=jax.ShapeDtypeStruct((s, 1024), ACT_DTYPE),
        compiler_params=_cparams(("parallel",)),
    )(ps, ln_g, ln_b, w_tril, b_t)


def sgu_bwd(ps, dy, ln_g, ln_b, w_tril, w_tril_t, b_t):
    s = ps.shape[0]

    def body(ps_ref, dy_ref, g_ref, b_ref, w_ref, wt_ref, bt_ref, dps_ref, dw_ref, dbt_ref, dg_ref, db_ref, dvn_scr):
        @pl.when(pl.program_id(0) == 0)
        def _():
            dw_ref[...] = jnp.zeros_like(dw_ref)
            dbt_ref[...] = jnp.zeros_like(dbt_ref)
            dg_ref[...] = jnp.zeros_like(dg_ref)
            db_ref[...] = jnp.zeros_like(db_ref)

        u = ps_ref[:, 0:1024].astype(F32)
        v = ps_ref[:, 1024:2048].astype(F32)
        z = ps_ref[:, 2048:3072].astype(F32)
        dy = dy_ref[...].astype(F32)
        xhat, rstd, vn = _layernorm(v, g_ref[...], b_ref[...])
        sz = _sig(z)
        silu = z * sz
        row = lax.broadcasted_iota(jnp.int32, (BLK, BLK), 0)
        colm = lax.broadcasted_iota(jnp.int32, (BLK, BLK), 1)
        tril = row >= colm
        dbt = jnp.zeros((BLK, 128), F32)
        for g in range(SG_GROUPS):
            sl = slice(g * 128, (g + 1) * 128)
            vng = vn[:, sl]
            mixed = _dot(w_ref[g], vng) + bt_ref[:, g:g + 1]
            dyg, ug = dy[:, sl], u[:, sl]
            dps_ref[:, sl] = (dyg * mixed * silu[:, sl]).astype(dps_ref.dtype)
            dps_ref[:, 2048 + g * 128:2048 + (g + 1) * 128] = (
                dyg * ug * mixed * _dsilu(z[:, sl], sz[:, sl])).astype(dps_ref.dtype)
            dm = dyg * ug * silu[:, sl]
            dw_ref[g] += jnp.where(tril, _dot_nt(dm, vng), 0.0)
            dbt = dbt + jnp.where(colm == g, jnp.sum(dm, axis=1, keepdims=True), 0.0)
            dvn_scr[:, sl] = _dot(wt_ref[g], dm)
        dbt_ref[...] += dbt
        dvn = dvn_scr[...]
        dg_ref[...] += jnp.sum(dvn * xhat, axis=0, keepdims=True)
        db_ref[...] += jnp.sum(dvn, axis=0, keepdims=True)
        dxh = dvn * g_ref[...]
        dv = rstd * (dxh - jnp.mean(dxh, axis=-1, keepdims=True)
                     - xhat * jnp.mean(dxh * xhat, axis=-1, keepdims=True))
        dps_ref[:, 1024:2048] = dv.astype(dps_ref.dtype)

    return pl.pallas_call(
        body, name="sgu_bwd", grid=(s // BLK,),
        in_specs=[pl.BlockSpec((BLK, SG_COLS), lambda c: (c, 0)), pl.BlockSpec((BLK, 1024), lambda c: (c, 0)),
                  _full((1, 1024)), _full((1, 1024)), _full((SG_GROUPS, BLK, BLK)), _full((SG_GROUPS, BLK, BLK)),
                  _full((BLK, 128))],
        out_specs=[pl.BlockSpec((BLK, SG_COLS), lambda c: (c, 0)), _full((SG_GROUPS, BLK, BLK)), _full((BLK, 128)),
                   _full((1, 1024)), _full((1, 1024))],
        out_shape=[jax.ShapeDtypeStruct((s, SG_COLS), ACT_DTYPE), jax.ShapeDtypeStruct((SG_GROUPS, BLK, BLK), F32),
                   jax.ShapeDtypeStruct((BLK, 128), F32), jax.ShapeDtypeStruct((1, 1024), F32),
                   jax.ShapeDtypeStruct((1, 1024), F32)],
        scratch_shapes=[pltpu.VMEM((BLK, 1024), F32)],
        compiler_params=_cparams(("arbitrary",)),
    )(ps, dy, ln_g, ln_b, w_tril, w_tril_t, b_t)


def _shift_down(cur, prev16, k):
    if k == 0:
        return cur
    r = pltpu.roll(cur, k, 0)
    rp = pltpu.roll(prev16, k, 0)
    row = lax.broadcasted_iota(jnp.int32, (8, cur.shape[1]), 0)
    return jnp.concatenate([jnp.where(row < k, rp[0:8], r[0:8]), r[8:]], axis=0)


def _shift_up(cur, next16, k):
    if k == 0:
        return cur
    n = cur.shape[0]
    r = pltpu.roll(cur, n - k, 0)
    rn = pltpu.roll(next16, 16 - k, 0)
    row = lax.broadcasted_iota(jnp.int32, (8, cur.shape[1]), 0)
    return jnp.concatenate([r[:n - 8], jnp.where(row >= 8 - k, rn[8:16], r[n - 8:])], axis=0)


def _bcast8(v):
    return jnp.broadcast_to(v, (16, v.shape[1]))


class _Ssd:
    def __init__(self, xbc, prev16, dtr, cw, cbias, dtb, alog, dsk, tri, e):
        pre = cbias + cw[3:4] * xbc
        self.shifted = [xbc]
        for k in (1, 2, 3):
            sh = _shift_down(xbc, prev16, k)
            self.shifted.append(sh)
            pre = pre + cw[3 - k:4 - k] * sh
        self.pre = pre
        self.sg = _sig(pre)
        act = pre * self.sg
        self.xs = act[:, 0:SSM_WIDTH]
        self.bm = act[:, SSM_WIDTH:SSM_WIDTH + 512]
        self.cm = act[:, SSM_WIDTH + 512:CONV_DIM]
        self.dtp = dtr + dtb
        self.dt = jnp.maximum(self.dtp, 0.0) + jnp.log(1.0 + jnp.exp(-jnp.abs(self.dtp)))
        self.a = -jnp.exp(alog)
        self.acs = _dot_hi(tri, self.dt * self.a)
        self.acs_t = self.acs.T
        tot = self.acs[BLK - 1:BLK]
        self.ecs = jnp.exp(self.acs)
        self.dte = jnp.exp(tot - self.acs)
        self.cd = jnp.exp(tot)
        self.dt_x = _dot_onehot(self.dt, e)
        self.ecs_x = _dot_onehot(self.ecs, e)
        self.dte_x = _dot_onehot(self.dte, e)
        self.cd_x = _dot_onehot(_bcast8(self.cd), e)[0:1]
        self.d_x = _dot_onehot(_bcast8(dsk), e)[0:1]
        self.xdt = self.xs * self.dt_x
        row = lax.broadcasted_iota(jnp.int32, (BLK, BLK), 0)
        col = lax.broadcasted_iota(jnp.int32, (BLK, BLK), 1)
        self.tril = row >= col

    def group(self, g):
        sl = slice(g * 128, (g + 1) * 128)
        bg, cg = self.bm[:, sl], self.cm[:, sl]
        return bg, cg, _dot_nt(cg, bg)

    def decay(self, h):
        seg = self.acs[:, h:h + 1] - self.acs_t[h:h + 1, :]
        return jnp.exp(jnp.where(self.tril, seg, NEG))

    def y_pre_gate(self, ht_of, yd_scr, yoff_scr):
        for g in range(SSM_GROUPS):
            bg, cg, cb = self.group(g)
            for j in range(8):
                h = g * 8 + j
                sl = slice(h * 64, (h + 1) * 64)
                yd_scr[:, sl] = _dot(cb * self.decay(h), self.xdt[:, sl])
            gs = slice(g * SSM_GW, (g + 1) * SSM_GW)
            yoff_scr[:, gs] = _dot(cg, ht_of(g)) * self.ecs_x[:, gs]
        return yd_scr[...] + yoff_scr[...] + self.d_x * self.xs


def _ssd_consts():
    hh = lax.broadcasted_iota(jnp.int32, (128, SSM_WIDTH), 0)
    ch = lax.broadcasted_iota(jnp.int32, (128, SSM_WIDTH), 1)
    e = (ch // 64 == hh).astype(jnp.bfloat16)
    row = lax.broadcasted_iota(jnp.int32, (BLK, BLK), 0)
    col = lax.broadcasted_iota(jnp.int32, (BLK, BLK), 1)
    tri = (row >= col).astype(F32)
    return tri, e


def _pad_lanes(v, n=128):
    return jnp.pad(v, ((0, 0), (0, n - v.shape[1])))


def ssd_fwd(pm, cw, cbias, dtb, alog, dsk, ng, ex=None):
    s = pm.shape[0]
    nc = s // BLK
    tri, e = _ssd_consts()

    def body(pm_ref, prev_ref, cw_ref, cb_ref, dtb_ref, al_ref, d_ref, ng_ref, tri_ref, e_ref,
             y_ref, st_ref, ht_ref, yd_scr, yoff_scr):
        c = pl.program_id(0)

        @pl.when(c == 0)
        def _():
            ht_ref[...] = jnp.zeros_like(ht_ref)

        xbc = pm_ref[:, 0:CONV_DIM].astype(F32)
        prev16 = jnp.where(c == 0, 0.0, prev_ref[...].astype(F32))
        f = _Ssd(xbc, prev16, pm_ref[:, DT_OFF:DT_OFF + 128].astype(F32), cw_ref[...], cb_ref[...], dtb_ref[...],
                 al_ref[...], d_ref[...], tri_ref[...], e_ref[...])
        st_ref[0] = ht_ref[...]
        y = f.y_pre_gate(lambda g: ht_ref[g], yd_scr, yoff_scr)
        for g in range(SSM_GROUPS):
            bg, _, _ = f.group(g)
            gs = slice(g * SSM_GW, (g + 1) * SSM_GW)
            ht_ref[g] = ht_ref[g] * f.cd_x[:, gs] + _dot_tn(bg, f.xdt[:, gs] * f.dte_x[:, gs])
        z = pm_ref[:, CONV_DIM:CONV_DIM + SSM_WIDTH].astype(F32)
        ypre = y * z * _sig(z)
        for g in range(SSM_GROUPS):
            gs = slice(g * SSM_GW, (g + 1) * SSM_GW)
            yg = ypre[:, gs]
            rr = lax.rsqrt(jnp.mean(yg * yg, axis=-1, keepdims=True) + EPS)
            y_ref[:, gs] = (yg * rr * ng_ref[:, gs]).astype(y_ref.dtype)

    own, hosted = _call_hosting(
        body, "ssd_fwd", nc,
        in_specs=[pl.BlockSpec((BLK, SSM_COLS), lambda c: (c, 0)),
                  pl.BlockSpec((16, CONV_DIM), lambda c: (jnp.maximum(8 * c - 1, 0), 0)),
                  _full((4, CONV_DIM)), _full((1, CONV_DIM)), _full((1, 128)), _full((1, 128)), _full((1, 128)),
                  _full((1, SSM_WIDTH)), _full((BLK, BLK)), _full((128, SSM_WIDTH))],
        out_specs=[pl.BlockSpec((BLK, SSM_WIDTH), lambda c: (c, 0)),
                   pl.BlockSpec((1, SSM_GROUPS, 128, SSM_GW), lambda c: (c, 0, 0, 0))],
        out_shape=[jax.ShapeDtypeStruct((s, SSM_WIDTH), ACT_DTYPE),
                   jax.ShapeDtypeStruct((nc, SSM_GROUPS, 128, SSM_GW), F32)],
        scratch=[pltpu.VMEM((SSM_GROUPS, 128, SSM_GW), F32), pltpu.VMEM((BLK, SSM_WIDTH), F32),
                 pltpu.VMEM((BLK, SSM_WIDTH), F32)],
        args=(pm, pm, cw, cbias, dtb, alog, dsk, ng, tri, e), ex=ex)
    return (*own, hosted) if ex is not None else tuple(own)


def ssd_bwd(pm, dy, states, cw, cbias, dtb, alog, dsk, ng, ex=None):
    s = pm.shape[0]
    nc = s // BLK
    tri, e = _ssd_consts()
    tri_t, e_t = tri.T, e.T

    def body(pm_ref, prev_ref, dy_ref, st_ref, cw_ref, cb_ref, dtb_ref, al_ref, d_ref, ng_ref,
             tri_ref, trit_ref, e_ref, et_ref,
             dpm_ref, dcw_ref, dcb_ref, dvec_ref, dng_ref,
             dht_ref, dcar_ref, yd_scr, yoff_scr, dx_scr, r2_scr, hs_scr, da_scr, dat_scr, dd_scr, dbc_scr):
        i = pl.program_id(0)
        n = nc - 1 - i

        @pl.when(i == 0)
        def _():
            dht_ref[...] = jnp.zeros_like(dht_ref)
            dcar_ref[...] = jnp.zeros_like(dcar_ref)
            dcw_ref[...] = jnp.zeros_like(dcw_ref)
            dcb_ref[...] = jnp.zeros_like(dcb_ref)
            dvec_ref[...] = jnp.zeros_like(dvec_ref)
            dng_ref[...] = jnp.zeros_like(dng_ref)
            dd_scr[...] = jnp.zeros_like(dd_scr)
            da_scr[...] = jnp.zeros_like(da_scr)
            dat_scr[...] = jnp.zeros_like(dat_scr)

        xbc = pm_ref[:, 0:CONV_DIM].astype(F32)
        prev16 = jnp.where(n == 0, 0.0, prev_ref[...].astype(F32))
        cw = cw_ref[...]
        f = _Ssd(xbc, prev16, pm_ref[:, DT_OFF:DT_OFF + 128].astype(F32), cw, cb_ref[...], dtb_ref[...],
                 al_ref[...], d_ref[...], tri_ref[...], e_ref[...])
        et = et_ref[...]
        y = f.y_pre_gate(lambda g: st_ref[0, g], yd_scr, yoff_scr)

        z = pm_ref[:, CONV_DIM:CONV_DIM + SSM_WIDTH].astype(F32)
        dyv = dy_ref[...].astype(F32)
        sz = _sig(z)
        silu = z * sz
        ypre = y * silu
        for g in range(SSM_GROUPS):
            gs = slice(g * SSM_GW, (g + 1) * SSM_GW)
            yg = ypre[:, gs]
            rr = lax.rsqrt(jnp.mean(yg * yg, axis=-1, keepdims=True) + EPS)
            nrm = yg * rr
            dng_ref[:, gs] += jnp.sum(dyv[:, gs] * nrm, axis=0, keepdims=True)
            dn = dyv[:, gs] * ng_ref[:, gs]
            dx_scr[:, gs] = rr * (dn - nrm * jnp.mean(dn * nrm, axis=-1, keepdims=True))
        dypre = dx_scr[...]
        d_y = dypre * silu
        dpm_ref[:, CONV_DIM:CONV_DIM + SSM_WIDTH] = (dypre * y * _dsilu(z, sz)).astype(dpm_ref.dtype)

        for g in range(SSM_GROUPS):
            bg, cg, cb = f.group(g)
            gs = slice(g * SSM_GW, (g + 1) * SSM_GW)
            htg = st_ref[0, g]
            dhn = dht_ref[g]
            dcb = jnp.zeros((BLK, BLK), F32)
            for j in range(8):
                h = g * 8 + j
                sl = slice(h * 64, (h + 1) * 64)
                dec = f.decay(h)
                dyh = d_y[:, sl]
                dmd = _dot_nt(dyh, f.xdt[:, sl]) * dec
                dcb = dcb + dmd
                gm = dmd * cb
                da_scr[:, h:h + 1] = jnp.sum(gm, axis=1, keepdims=True)
                dat_scr[h:h + 1, :] = jnp.sum(gm, axis=0, keepdims=True)
                dx_scr[:, sl] = _dot_tn(cb * dec, dyh)
            dz = f.ecs_x[:, gs] * d_y[:, gs]
            dbc_scr[:, 512 + g * 128:512 + (g + 1) * 128] = _dot(dcb, bg) + _dot_nt(dz, htg)
            dbc_scr[:, g * 128:(g + 1) * 128] = _dot_tn(dcb, cg) + _dot_nt(f.xdt[:, gs] * f.dte_x[:, gs], dhn)
            dws = _dot(bg, dhn)
            dx_scr[:, gs] += f.dte_x[:, gs] * dws
            r2_scr[:, gs] = dws * f.xdt[:, gs]
            hs_scr[:, gs] = _bcast8(jnp.sum(dhn * htg, axis=0, keepdims=True))
            dht_ref[g] = f.cd_x[:, gs] * dhn + _dot_tn(cg, dz)
        d_x = dx_scr[...]
        r1 = _dot_onehot(d_y * yoff_scr[...], et)
        r2 = _dot_onehot(r2_scr[...], et) * f.dte
        dcd = _dot_onehot(hs_scr[...], et)[0:1]
        d_tot = jnp.sum(r2, axis=0, keepdims=True) + dcd * f.cd
        row = lax.broadcasted_iota(jnp.int32, (BLK, 128), 0)
        d_a = da_scr[...] - dat_scr[...].T + r1 - r2 + jnp.where(row == BLK - 1, d_tot, 0.0)
        dadt = _dot_hi(trit_ref[...], d_a)
        ddt = dadt * f.a + _dot_onehot(d_x * f.xs, et)
        lane = lax.broadcasted_iota(jnp.int32, (BLK, 128), 1)
        dr = jnp.where(lane < SSM_HEADS, ddt * _sig(f.dtp), 0.0)
        dvec_ref[0:1, :] += jnp.sum(dr, axis=0, keepdims=True)
        dvec_ref[1:2, :] += jnp.sum(dadt * f.dt, axis=0, keepdims=True) * f.a
        dd_scr[...] += _bcast8(jnp.sum(d_y * f.xs, axis=0, keepdims=True))
        dpm_ref[:, DT_OFF:DT_OFF + 128] = dr.astype(dpm_ref.dtype)
        dpm_ref[:, DT_OFF + 128:SSM_COLS] = jnp.zeros((BLK, 128), dpm_ref.dtype)

        dxs = d_x * f.dt_x + f.d_x * d_y
        dact = jnp.concatenate([dxs, dbc_scr[...]], axis=1)
        dpre = dact * _dsilu(f.pre, f.sg)
        dcb_ref[...] += jnp.sum(dpre, axis=0, keepdims=True)
        dxraw = jnp.zeros((BLK, CONV_DIM), F32)
        nxt = dcar_ref[...]
        for k in range(4):
            dcw_ref[3 - k:4 - k, :] += jnp.sum(dpre * f.shifted[k], axis=0, keepdims=True)
            dxraw = dxraw + cw[3 - k:4 - k] * _shift_up(dpre, nxt, k)
        dcar_ref[...] = dpre[0:16]
        dpm_ref[:, 0:CONV_DIM] = dxraw.astype(dpm_ref.dtype)

        @pl.when(i == nc - 1)
        def _():
            dvec_ref[2:3, :] = _dot_onehot(dd_scr[...], et)[0:1]

    own, hosted = _call_hosting(
        body, "ssd_bwd", nc,
        in_specs=[pl.BlockSpec((BLK, SSM_COLS), lambda i: (nc - 1 - i, 0)),
                  pl.BlockSpec((16, CONV_DIM), lambda i: (jnp.maximum(8 * (nc - 1 - i) - 1, 0), 0)),
                  pl.BlockSpec((BLK, SSM_WIDTH), lambda i: (nc - 1 - i, 0)),
                  pl.BlockSpec((1, SSM_GROUPS, 128, SSM_GW), lambda i: (nc - 1 - i, 0, 0, 0)),
                  _full((4, CONV_DIM)), _full((1, CONV_DIM)), _full((1, 128)), _full((1, 128)), _full((1, 128)),
                  _full((1, SSM_WIDTH)), _full((BLK, BLK)), _full((BLK, BLK)), _full((128, SSM_WIDTH)),
                  _full((SSM_WIDTH, 128))],
        out_specs=[pl.BlockSpec((BLK, SSM_COLS), lambda i: (nc - 1 - i, 0)),
                   _full((8, CONV_DIM)), _full((1, CONV_DIM)), _full((8, 128)), _full((1, SSM_WIDTH))],
        out_shape=[jax.ShapeDtypeStruct((s, SSM_COLS), ACT_DTYPE), jax.ShapeDtypeStruct((8, CONV_DIM), F32),
                   jax.ShapeDtypeStruct((1, CONV_DIM), F32), jax.ShapeDtypeStruct((8, 128), F32),
                   jax.ShapeDtypeStruct((1, SSM_WIDTH), F32)],
        scratch=[pltpu.VMEM((SSM_GROUPS, 128, SSM_GW), F32), pltpu.VMEM((16, CONV_DIM), F32),
                 pltpu.VMEM((BLK, SSM_WIDTH), F32), pltpu.VMEM((BLK, SSM_WIDTH), F32),
                 pltpu.VMEM((BLK, SSM_WIDTH), F32), pltpu.VMEM((BLK, SSM_WIDTH), F32),
                 pltpu.VMEM((16, SSM_WIDTH), F32), pltpu.VMEM((BLK, 128), F32), pltpu.VMEM((128, BLK), F32),
                 pltpu.VMEM((16, SSM_WIDTH), F32), pltpu.VMEM((BLK, 1024), F32)],
        args=(pm, pm, dy, states, cw, cbias, dtb, alog, dsk, ng, tri, tri_t, e, e_t), ex=ex)
    return (*own, hosted) if ex is not None else tuple(own)


def merge_fwd(x, ya, ys, ym, pg, wa, ws, wm, wo, g_post):
    s, d = x.shape
    tm = min(256, s)

    def body(x_ref, ya_ref, ys_ref, ym_ref, pg_ref, wa_ref, ws_ref, wm_ref, wo_ref, g_ref,
             xo_ref, ba_ref, bs_ref, bm_ref, mg_ref, out_ref):
        ba = _dot(ya_ref[...], wa_ref[...])
        bs = _dot(ys_ref[...], ws_ref[...])
        bm = _dot(ym_ref[...], wm_ref[...])
        merged = (_sig(pg_ref[:, 0:d].astype(F32)) * ba + _sig(pg_ref[:, d:2 * d].astype(F32)) * bs
                  + _sig(pg_ref[:, 2 * d:3 * d].astype(F32)) * bm)
        out = _dot(merged, wo_ref[...])
        r = lax.rsqrt(jnp.mean(out * out, axis=-1, keepdims=True) + EPS)
        xo_ref[...] = x_ref[...] + out * r * g_ref[...]
        ba_ref[...] = ba.astype(ba_ref.dtype)
        bs_ref[...] = bs.astype(bs_ref.dtype)
        bm_ref[...] = bm.astype(bm_ref.dtype)
        mg_ref[...] = merged.astype(mg_ref.dtype)
        out_ref[...] = out.astype(out_ref.dtype)

    rows = lambda w: pl.BlockSpec((tm, w), lambda i: (i, 0))
    act = jax.ShapeDtypeStruct((s, d), ACT_DTYPE)
    return pl.pallas_call(
        body, name="merge_fwd", grid=(s // tm,),
        in_specs=[rows(d), rows(d), rows(d), rows(2 * d), rows(3 * d), _full((d, d)), _full((d, d)),
                  _full((2 * d, d)), _full((d, d)), _full((1, d))],
        out_specs=[rows(d)] * 6,
        out_shape=[jax.ShapeDtypeStruct((s, d), F32), act, act, act, act, act],
        compiler_params=_cparams(("parallel",)),
    )(x, ya, ys, ym, pg, wa, ws, wm, wo, g_post)


def merge_bwd(dx, out_s, pg, ba, bs, bm, wa, ws, wm, wo, g_post):
    s, d = dx.shape
    tm = min(256, s)

    def body(dx_ref, out_ref, pg_ref, ba_ref, bs_ref, bm_ref, wa_ref, ws_ref, wm_ref, wo_ref, g_ref,
             dout_ref, dba_ref, dbs_ref, dbm_ref, dpg_ref, dya_ref, dys_ref, dym_ref, dg_ref):
        @pl.when(pl.program_id(0) == 0)
        def _():
            dg_ref[...] = jnp.zeros_like(dg_ref)

        o = out_ref[...].astype(F32)
        dxv = dx_ref[...]
        r = lax.rsqrt(jnp.mean(o * o, axis=-1, keepdims=True) + EPS)
        nrm = o * r
        dg_ref[...] += jnp.sum(dxv * nrm, axis=0, keepdims=True)
        dn = dxv * g_ref[...]
        dout = r * (dn - nrm * jnp.mean(dn * nrm, axis=-1, keepdims=True))
        dout_ref[...] = dout.astype(dout_ref.dtype)
        dmerged = _dot_nt(dout, wo_ref[...])
        for q, (b_ref, db_ref, w_ref, dy_ref) in enumerate(((ba_ref, dba_ref, wa_ref, dya_ref),
                                                            (bs_ref, dbs_ref, ws_ref, dys_ref),
                                                            (bm_ref, dbm_ref, wm_ref, dym_ref))):
            gt = _sig(pg_ref[:, q * d:(q + 1) * d].astype(F32))
            db = dmerged * gt
            db_ref[...] = db.astype(db_ref.dtype)
            dpg_ref[:, q * d:(q + 1) * d] = (dmerged * b_ref[...].astype(F32) * gt * (1.0 - gt)).astype(dpg_ref.dtype)
            dy_ref[...] = _dot_nt(db, w_ref[...]).astype(dy_ref.dtype)

    rows = lambda w: pl.BlockSpec((tm, w), lambda i: (i, 0))
    act = lambda w: jax.ShapeDtypeStruct((s, w), ACT_DTYPE)
    return pl.pallas_call(
        body, name="merge_bwd", grid=(s // tm,),
        in_specs=[rows(d), rows(d), rows(3 * d), rows(d), rows(d), rows(d), _full((d, d)), _full((d, d)),
                  _full((2 * d, d)), _full((d, d)), _full((1, d))],
        out_specs=[rows(d), rows(d), rows(d), rows(d), rows(3 * d), rows(d), rows(d), rows(2 * d), _full((1, d))],
        out_shape=[act(d), act(d), act(d), act(d), act(3 * d), act(d), act(d), act(2 * d),
                   jax.ShapeDtypeStruct((1, d), F32)],
        compiler_params=_cparams(("arbitrary",)),
    )(dx, out_s, pg, ba, bs, bm, wa, ws, wm, wo, g_post)


def loss_grad(y, target):
    s, d = y.shape
    tm = min(512, s)

    def body(y_ref, t_ref, dy_ref, l_ref):
        @pl.when(pl.program_id(0) == 0)
        def _():
            l_ref[...] = jnp.zeros_like(l_ref)

        err = y_ref[...] - t_ref[...]
        dy_ref[...] = err * (1.0 / d)
        part = jnp.sum(jnp.sum(err * err, axis=-1, keepdims=True) * (1.0 / d), axis=0, keepdims=True)
        l_ref[...] += 0.5 * jnp.broadcast_to(part, l_ref.shape)

    return pl.pallas_call(
        body, name="loss_grad", grid=(s // tm,),
        in_specs=[pl.BlockSpec((tm, d), lambda i: (i, 0)), pl.BlockSpec((tm, d), lambda i: (i, 0))],
        out_specs=[pl.BlockSpec((tm, d), lambda i: (i, 0)), _full((8, 128))],
        out_shape=[jax.ShapeDtypeStruct((s, d), F32), jax.ShapeDtypeStruct((8, 128), F32)],
        compiler_params=_cparams(("arbitrary",)),
    )(y, target)


def _mesh_pos():
    x, y, c = lax.axis_index("x"), lax.axis_index("y"), lax.axis_index("c")
    return x, y, c, 4 * x + 2 * y + c


def _peer(x, y, c, k):
    px = 1 - x if k & 4 else x
    py = 1 - y if k & 2 else y
    pc = 1 - c if k & 1 else c
    return (px, py, pc), 4 * px + 2 * py + pc


class Exchange:
    SAME_CORE = (2, 4, 6)

    def __init__(self, scattered, gathered):
        self.ns = len(scattered)
        self.arrays = list(scattered) + list(gathered)
        self.na = len(self.arrays)
        any_spec = pl.BlockSpec(memory_space=pl.ANY)
        self.in_specs = [any_spec] * self.na
        self.out_specs = [any_spec] * self.na
        self.out_shape = ([jax.ShapeDtypeStruct(a.shape, a.dtype) for a in scattered]
                          + [jax.ShapeDtypeStruct((N_DEV,) + a.shape, a.dtype) for a in gathered])
        self.scratch = [pltpu.SemaphoreType.DMA((self.na, N_DEV - 1)), pltpu.SemaphoreType.DMA((self.na, N_DEV - 1)),
                        pltpu.SemaphoreType.DMA((self.na,))]

    def _src(self, ins, q, slot):
        return ins[q].at[slot] if q < self.ns else ins[q]

    def _local(self, ins, outs, sems):
        me = _mesh_pos()[3]
        return [pltpu.make_async_copy(self._src(ins, q, me), outs[q].at[me], sems[2].at[q]) for q in range(self.na)]

    def _direct(self, ins, outs, sems, relations, arrays):
        x, y, c, me = _mesh_pos()
        copies = []
        for k in relations:
            peer, pidx = _peer(x, y, c, k)
            for q in arrays:
                copies.append(pltpu.make_async_remote_copy(
                    src_ref=self._src(ins, q, pidx), dst_ref=outs[q].at[me], send_sem=sems[0].at[q, k - 1],
                    recv_sem=sems[1].at[q, k - 1], device_id=peer, device_id_type=MESH))
        return copies

    def _arrivals(self, ins, outs, sems, relations, arrays):
        x, y, c, _ = _mesh_pos()
        copies = []
        for k in relations:
            peer, pidx = _peer(x, y, c, k)
            for q in arrays:
                copies.append(pltpu.make_async_remote_copy(
                    src_ref=self._src(ins, q, pidx), dst_ref=outs[q].at[pidx], send_sem=sems[0].at[q, k - 1],
                    recv_sem=sems[1].at[q, k - 1], device_id=peer, device_id_type=MESH))
        return copies

    def _relays(self, outs, sems):
        x, y, c, _ = _mesh_pos()
        sibling, _ = _peer(x, y, c, 1)
        copies = []
        for k in self.SAME_CORE:
            _, pidx = _peer(x, y, c, k)
            for q in range(self.ns, self.na):
                copies.append(pltpu.make_async_remote_copy(
                    src_ref=outs[q].at[pidx], dst_ref=outs[q].at[pidx], send_sem=sems[0].at[q, k],
                    recv_sem=sems[1].at[q, k], device_id=sibling, device_id_type=MESH))
        return copies

    def _sends(self, ins, outs, sems):
        return (self._direct(ins, outs, sems, range(1, N_DEV), range(self.ns))
                + self._direct(ins, outs, sems, (1,) + self.SAME_CORE, range(self.ns, self.na)))

    def start(self, ins, outs, sems):
        for cp in self._local(ins, outs, sems) + self._sends(ins, outs, sems):
            cp.start()

    def relay(self, ins, outs, sems):
        for cp in self._arrivals(ins, outs, sems, self.SAME_CORE, range(self.ns, self.na)):
            cp.wait_recv()
        for cp in self._relays(outs, sems):
            cp.start()

    def wait(self, ins, outs, sems):
        for cp in (self._arrivals(ins, outs, sems, range(1, N_DEV), range(self.ns))
                   + self._arrivals(ins, outs, sems, (1, 3, 5, 7), range(self.ns, self.na))):
            cp.wait_recv()
        for cp in self._sends(ins, outs, sems) + self._relays(outs, sems):
            cp.wait_send()
        for cp in self._local(ins, outs, sems):
            cp.wait()


def exchange(scattered, gathered, name):
    ex = Exchange(scattered, gathered)

    def body(*refs):
        ins, outs, sems = refs[:ex.na], refs[ex.na:2 * ex.na], refs[2 * ex.na:]
        ex.start(ins, outs, sems)
        ex.relay(ins, outs, sems)
        ex.wait(ins, outs, sems)

    return pl.pallas_call(body, name=name, in_specs=ex.in_specs, out_specs=ex.out_specs, out_shape=ex.out_shape,
                          scratch_shapes=ex.scratch)(*ex.arrays)


def adamw(parts, w, m, v, tile, name):
    npart, r, dp = parts.shape
    d = w.shape[1]

    def body(p_ref, w_ref, m_ref, v_ref, g_ref, dw_ref, nm_ref, nv_ref):
        g = p_ref[0, :, 0:d].astype(F32)
        for q in range(1, npart):
            g = g + p_ref[q, :, 0:d].astype(F32)
        g_ref[...] = g
        nm = ADAM_B1 * m_ref[...] + (1.0 - ADAM_B1) * g
        nv = ADAM_B2 * v_ref[...] + (1.0 - ADAM_B2) * (g * g)
        nm_ref[...] = nm
        nv_ref[...] = nv
        m_hat = nm / (1.0 - ADAM_B1 ** ADAM_STEP)
        v_hat = nv / (1.0 - ADAM_B2 ** ADAM_STEP)
        dw_ref[...] = -ADAM_LR * (m_hat / (jnp.sqrt(v_hat) + ADAM_EPS) + ADAM_WD * w_ref[...])

    rows = pl.BlockSpec((tile, d), lambda i: (i, 0))
    out = jax.ShapeDtypeStruct((r, d), F32)
    return pl.pallas_call(
        body, name=name, grid=(r // tile,),
        in_specs=[pl.BlockSpec((npart, tile, dp), lambda i: (0, i, 0)), rows, rows, rows],
        out_specs=[rows] * 4, out_shape=[out] * 4,
        compiler_params=_cparams(("parallel",)),
    )(parts, w, m, v)


def _pad_rows(a, rows):
    return jnp.pad(a, ((0, rows - a.shape[0]), (0, 0)))


def _pack_rest(w_att, w_sg, w_ssm, w_out):
    parts = []
    for l in range(2):
        parts += [w_att[l], w_sg[l], w_ssm[l], w_out[l]]
    return jnp.concatenate(parts, axis=0)


def _unpack_rest(p):
    outs = [[], [], [], []]
    o = 0
    for l in range(2):
        for q, rws in enumerate(REST_PARTS):
            outs[q].append(p[o:o + rws])
            o += rws
    return [jnp.stack(t) for t in outs]


def _pack_win(w_in):
    return jnp.pad(w_in.reshape(2 * D_MODEL, WIN_SHARD), ((0, 0), (0, WIN_LANES - WIN_SHARD)))


W_IN_MAP = ((0, 1024, "att", 0), (1024, 1280, "att", 2048), (1280, 2304, "att", 1024), (2304, 5376, "sg", 0),
            (5376, 7424, "ssm", 3072), (7424, 10496, "ssm", 0), (10496, 10528, "ssm", 5120), (10528, 13600, "gate", 0))
SLAB_COLS = {"att": ATT_COLS, "sg": SG_COLS, "ssm": SSM_COLS, "gate": GATE_COLS}


def _slabs_from_shards(g):
    slabs = {}
    for name, width in SLAB_COLS.items():
        pieces, filled = [], 0
        for ga, gb, _, off in sorted((m for m in W_IN_MAP if m[2] == name), key=lambda m: m[3]):
            assert off == filled
            a = ga
            while a < gb:
                d = a // WIN_SHARD
                hi = min(gb, WIN_SHARD * (d + 1))
                pieces.append(g[d, :, a - WIN_SHARD * d:hi - WIN_SHARD * d])
                a = hi
            filled += gb - ga
        if filled < width:
            pieces.append(jnp.zeros((D_MODEL, width - filled), g.dtype))
        slabs[name] = jnp.concatenate(pieces, axis=1)
    return slabs


def _shards_from_slabs(dslabs):
    out = []
    for d in range(N_DEV):
        a, b = WIN_SHARD * d, WIN_SHARD * (d + 1)
        pieces = []
        for ga, gb, name, off in W_IN_MAP:
            lo, hi = max(a, ga), min(b, gb)
            if lo < hi:
                pieces.append(dslabs[name][:, off + lo - ga:off + hi - ga])
        pieces.append(jnp.zeros((D_MODEL, WIN_LANES - WIN_SHARD), pieces[0].dtype))
        out.append(jnp.concatenate(pieces, axis=1).astype(WIRE_DTYPE))
    return jnp.stack(out)


SMALL_SIZES = (("norm_pre", 2048), ("norm_post", 2048), ("rel_bias", 512), ("att_sinks", 32), ("sg_ln_g", 2048),
               ("sg_ln_b", 2048), ("sg_w", 262144), ("sg_b", 2048), ("ssm_conv_b", 6144), ("ssm_dt_bias", 64),
               ("ssm_a_log", 64), ("ssm_d", 64), ("ssm_norm_g", 4096), ("conv_w_full", 24576))


def _pack_small(d):
    parts = []
    for name, size in SMALL_SIZES:
        rows = 8 * (-(-size // (8 * D_MODEL)))
        flat = d[name].reshape(-1) if name in d else jnp.zeros((size,), F32)
        parts.append(jnp.pad(flat, (0, rows * D_MODEL - size)).reshape(rows, D_MODEL))
    return _pad_rows(jnp.concatenate(parts, axis=0), SMALL_ROWS)


def _unpack_small(p, shapes):
    out, o = {}, 0
    for name, size in SMALL_SIZES:
        rows = 8 * (-(-size // (8 * D_MODEL)))
        if name in shapes:
            out[name] = p[o:o + rows].reshape(-1)[:size].reshape(shapes[name])
        o += rows
    return out


def _bucket_onehot_t():
    qi = jnp.arange(BLK, dtype=jnp.int32)[None, :]
    kj = jnp.arange(BLK, dtype=jnp.int32)[:, None]
    dd = (qi - kj) & (BLK - 1)
    in_window = dd >= 0
    max_exact = REL_BUCKETS // 2
    dist_f = jnp.maximum(dd, 1).astype(F32)
    large = max_exact + (jnp.log(dist_f / max_exact) / math.log(128 / max_exact)
                         * (REL_BUCKETS - max_exact)).astype(jnp.int32)
    large = jnp.minimum(large, REL_BUCKETS - 1)
    bucket = jnp.where(dd < max_exact, dd, large).reshape(1, -1)
    onehot_t = (bucket == jnp.arange(REL_BUCKETS, dtype=jnp.int32)[:, None]).astype(F32)
    maskadd = jnp.where(in_window, 0.0, NEG).astype(F32).reshape(1, -1)
    return onehot_t, maskadd


WEIGHTS = ['w_in', 'norm_pre', 'norm_post', 'rel_bias', 'att_sinks', 'sg_ln_g', 'sg_ln_b', 'sg_w', 'sg_b',
           'ssm_conv_w', 'ssm_conv_b', 'ssm_dt_bias', 'ssm_a_log', 'ssm_d', 'ssm_norm_g',
           'w_br_att', 'w_br_sg', 'w_br_ssm', 'w_out']
REST = ('w_br_att', 'w_br_sg', 'w_br_ssm', 'w_out')


def kernel(x, w_in, norm_pre, norm_post, rel_bias, att_sinks, sg_ln_g, sg_ln_b, sg_w, sg_b, ssm_conv_w, ssm_conv_b, ssm_dt_bias, ssm_a_log, ssm_d, ssm_norm_g, w_br_att, w_br_sg, w_br_ssm, w_out, loss_target, m_w_in, m_norm_pre, m_norm_post, m_rel_bias, m_att_sinks, m_sg_ln_g, m_sg_ln_b, m_sg_w, m_sg_b, m_ssm_conv_w, m_ssm_conv_b, m_ssm_dt_bias, m_ssm_a_log, m_ssm_d, m_ssm_norm_g, m_w_br_att, m_w_br_sg, m_w_br_ssm, m_w_out, v_w_in, v_norm_pre, v_norm_post, v_rel_bias, v_att_sinks, v_sg_ln_g, v_sg_ln_b, v_sg_w, v_sg_b, v_ssm_conv_w, v_ssm_conv_b, v_ssm_dt_bias, v_ssm_a_log, v_ssm_d, v_ssm_norm_g, v_w_br_att, v_w_br_sg, v_w_br_ssm, v_w_out):
    w = dict(w_in=w_in, norm_pre=norm_pre, norm_post=norm_post, rel_bias=rel_bias, att_sinks=att_sinks,
             sg_ln_g=sg_ln_g, sg_ln_b=sg_ln_b, sg_w=sg_w, sg_b=sg_b, ssm_conv_w=ssm_conv_w, ssm_conv_b=ssm_conv_b,
             ssm_dt_bias=ssm_dt_bias, ssm_a_log=ssm_a_log, ssm_d=ssm_d, ssm_norm_g=ssm_norm_g,
             w_br_att=w_br_att, w_br_sg=w_br_sg, w_br_ssm=w_br_ssm, w_out=w_out)
    mom = dict(w_in=m_w_in, norm_pre=m_norm_pre, norm_post=m_norm_post, rel_bias=m_rel_bias, att_sinks=m_att_sinks,
               sg_ln_g=m_sg_ln_g, sg_ln_b=m_sg_ln_b, sg_w=m_sg_w, sg_b=m_sg_b, ssm_conv_w=m_ssm_conv_w,
               ssm_conv_b=m_ssm_conv_b, ssm_dt_bias=m_ssm_dt_bias, ssm_a_log=m_ssm_a_log, ssm_d=m_ssm_d,
               ssm_norm_g=m_ssm_norm_g, w_br_att=m_w_br_att, w_br_sg=m_w_br_sg, w_br_ssm=m_w_br_ssm, w_out=m_w_out)
    var = dict(w_in=v_w_in, norm_pre=v_norm_pre, norm_post=v_norm_post, rel_bias=v_rel_bias, att_sinks=v_att_sinks,
               sg_ln_g=v_sg_ln_g, sg_ln_b=v_sg_ln_b, sg_w=v_sg_w, sg_b=v_sg_b, ssm_conv_w=v_ssm_conv_w,
               ssm_conv_b=v_ssm_conv_b, ssm_dt_bias=v_ssm_dt_bias, ssm_a_log=v_ssm_a_log, ssm_d=v_ssm_d,
               ssm_norm_g=v_ssm_norm_g, w_br_att=v_w_br_att, w_br_sg=v_w_br_sg, w_br_ssm=v_w_br_ssm, w_out=v_w_out)
    xs0 = x[0]
    target = loss_target[0]
    my_dev = 4 * lax.axis_index("x") + 2 * lax.axis_index("y") + lax.axis_index("c")

    conv_shard = _pad_rows(ssm_conv_w.reshape(-1, D_MODEL), 8)
    win_shard = _pack_win(w_in).astype(WIRE_DTYPE)
    rest_shard = _pack_rest(*[w[n] for n in REST]).astype(WIRE_DTYPE)
    layer_shards = [[win_shard[l * D_MODEL:(l + 1) * D_MODEL], rest_shard[l * LAYER_REST:(l + 1) * LAYER_REST]]
                    for l in range(2)]
    g_win0, g_rest0, gathered_conv = exchange([], layer_shards[0] + [conv_shard], "all_gather")
    conv_full = gathered_conv[:, 0:3].reshape(N_DEV, 2, 4, 384).transpose(1, 2, 0, 3).reshape(2, 4, CONV_DIM)

    def layer_weights(l, g_win, g_rest):
        slabs = _slabs_from_shards(g_win)
        lw = {"in_" + name: slab.astype(MXU_DTYPE) for name, slab in slabs.items()}
        o = 0
        for name, rws in zip(("att", "sg", "ssm", "out"), REST_PARTS):
            lw[name] = g_rest[:, o:o + rws].reshape(N_DEV * rws, D_MODEL).astype(MXU_DTYPE)
            o += rws
        tril = jnp.tril(jnp.ones((BLK, BLK), bool))
        sgw = jnp.where(tril[None], sg_w[l], 0.0)
        lw.update(
            g_pre=norm_pre[l][None], g_post=norm_post[l][None], sinks=jnp.repeat(att_sinks[l], BLK).reshape(2, GROUP_LANES),
            ln_g=sg_ln_g[l][None], ln_b=sg_ln_b[l][None], sgw=sgw.astype(MXU_DTYPE),
            sgw_t=sgw.transpose(0, 2, 1).astype(MXU_DTYPE), sgb_t=_pad_lanes(sg_b[l].T),
            cw=conv_full[l], cb=ssm_conv_b[l][None], dtb=_pad_lanes(ssm_dt_bias[l][None]),
            alog=_pad_lanes(ssm_a_log[l][None]), dsk=_pad_lanes(ssm_d[l][None]), ng=ssm_norm_g[l][None])
        return lw

    onehot_t, maskadd = _bucket_onehot_t()
    bias = bias_table(rel_bias.T, onehot_t, maskadd).reshape(2, GROUP_HEADS, BLK, BLK).transpose(0, 2, 1, 3)
    bias = bias.reshape(2, BLK, GROUP_LANES)

    saved = []
    xl = xs0
    layers = [layer_weights(0, g_win0, g_rest0)]
    for l in range(2):
        lw = layers[l]
        h = rmsnorm_fwd(xl, lw["g_pre"])
        pa = mm_nn(h, lw["in_att"], 1152, "proj_att").T
        ps = mm_nn(h, lw["in_sg"], 1536, "proj_sg")
        pm = mm_nn(h, lw["in_ssm"], 1792, "proj_ssm")
        pg = mm_nn(h, lw["in_gate"], 1536, "proj_gate")
        ya = attn_fwd(pa, bias, lw["sinks"]).T
        ys = sgu_fwd(ps, lw["ln_g"], lw["ln_b"], lw["sgw"], lw["sgb_t"])
        ssd_args = (pm, lw["cw"], lw["cb"], lw["dtb"], lw["alog"], lw["dsk"], lw["ng"])
        if l == 0:
            ym, states, (g_win1, g_rest1) = ssd_fwd(*ssd_args, Exchange([], layer_shards[1]))
            layers.append(layer_weights(1, g_win1, g_rest1))
        else:
            ym, states = ssd_fwd(*ssd_args)
        x_next, ba, bs, bm, merged, out_s = merge_fwd(xl, ya, ys, ym, pg, lw["att"], lw["sg"], lw["ssm"], lw["out"],
                                                      lw["g_post"])
        saved.append(dict(x=xl, h=h, pa=pa, ps=ps, pm=pm, pg=pg, ya=ya, ys=ys, ym=ym, states=states, ba=ba, bs=bs,
                          bm=bm, merged=merged, out_s=out_s))
        xl = x_next

    dx, loss_part = loss_grad(xl, target)
    loss = lax.psum(loss_part[0, 0], ("x", "y", "c"))

    dbias = jnp.zeros((2, BLK, GROUP_LANES), F32)
    win_grads, rest_grads = [None, None], [None, None]
    small = {n: [None, None] for n in ("norm_pre", "norm_post", "att_sinks", "sg_ln_g", "sg_ln_b", "sg_w", "sg_b",
                                       "ssm_conv_b", "ssm_dt_bias", "ssm_a_log", "ssm_d", "ssm_norm_g",
                                       "conv_w_full")}
    for l in (1, 0):
        lw, sv = layers[l], saved[l]
        dout, dba, dbs, dbm, dpg, dya, dys, dym, dg_post = merge_bwd(
            dx, sv["out_s"], sv["pg"], sv["ba"], sv["bs"], sv["bm"], lw["att"], lw["sg"], lw["ssm"], lw["out"],
            lw["g_post"])
        dw_out = mm_tn(sv["merged"], dout, 1024, "dw_out")
        dw_att = mm_tn(sv["ya"], dba, 1024, "dw_br_att")
        dw_sg = mm_tn(sv["ys"], dbs, 1024, "dw_br_sg")
        dw_ssm = mm_tn(sv["ym"], dbm, 1024, "dw_br_ssm")
        rest_grads[l] = jnp.concatenate(
            [dw_att.reshape(N_DEV, 128, D_MODEL), dw_sg.reshape(N_DEV, 128, D_MODEL),
             dw_ssm.reshape(N_DEV, 256, D_MODEL), dw_out.reshape(N_DEV, 128, D_MODEL)], axis=1).astype(WIRE_DTYPE)
        dpa_t, dbias, dsinks = attn_bwd(sv["pa"], dya.T, bias, lw["sinks"], dbias)
        dpa = dpa_t.T
        dps, dsgw, dsgb_t, dln_g, dln_b = sgu_bwd(sv["ps"], dys, lw["ln_g"], lw["ln_b"], lw["sgw"], lw["sgw_t"],
                                                  lw["sgb_t"])
        ssd_args = (sv["pm"], dym, sv["states"], lw["cw"], lw["cb"], lw["dtb"], lw["alog"], lw["dsk"], lw["ng"])
        if l == 0:
            dpm, dcw, dcb, dvec, dng, (recv_win1, recv_rest1, recv_rest0) = ssd_bwd(
                *ssd_args, Exchange([win_grads[1], rest_grads[1], rest_grads[0]], []))
        else:
            dpm, dcw, dcb, dvec, dng = ssd_bwd(*ssd_args)
        dslabs = dict(att=mm_tn(sv["h"], dpa, 2304, "dw_in_att"), sg=mm_tn(sv["h"], dps, 3072, "dw_in_sg"),
                      ssm=mm_tn(sv["h"], dpm, 2688, "dw_in_ssm"), gate=mm_tn(sv["h"], dpg, 3072, "dw_in_gate"))
        win_grads[l] = _shards_from_slabs(dslabs)
        dh_args = ([dpa, dps, dpm, dpg], [lw["in_att"], lw["in_sg"], lw["in_ssm"], lw["in_gate"]], sv["x"],
                   lw["g_pre"], dx)
        if l == 0:
            dx, dg_pre, (recv_win0,) = dh_norm_bwd(*dh_args, Exchange([win_grads[0]], []))
        else:
            dx, dg_pre = dh_norm_bwd(*dh_args)
        small["norm_pre"][l] = dg_pre[0]
        small["norm_post"][l] = dg_post[0]
        small["att_sinks"][l] = dsinks[0, :ATT_HEADS]
        small["sg_ln_g"][l] = dln_g[0]
        small["sg_ln_b"][l] = dln_b[0]
        small["sg_w"][l] = dsgw
        small["sg_b"][l] = dsgb_t[:, :SG_GROUPS].T
        small["ssm_conv_b"][l] = dcb[0]
        small["ssm_dt_bias"][l] = dvec[0, :SSM_HEADS]
        small["ssm_a_log"][l] = dvec[1, :SSM_HEADS]
        small["ssm_d"][l] = dvec[2, :SSM_HEADS]
        small["ssm_norm_g"][l] = dng[0]
        small["conv_w_full"][l] = dcw[0:4]
    grad_x = dx
    dbias = dbias.reshape(2, BLK, GROUP_HEADS, BLK).transpose(0, 2, 1, 3).reshape(ATT_HEADS, BLK * BLK)
    d_rel_bias = bias_table_bwd(dbias, onehot_t).T

    small_d = {n: jnp.stack(v) for n, v in small.items()}
    small_d["rel_bias"] = d_rel_bias
    (recv_small,) = exchange([], [_pack_small(small_d)], "gather_small_grads")

    res_win, res_rest = [], []
    for l, (recv_win, recv_rest) in enumerate(((recv_win0, recv_rest0), (recv_win1, recv_rest1))):
        res_win.append(adamw(recv_win, w_in[l], m_w_in[l], v_w_in[l], WIN_TILE, "adamw_w_in"))
        layer_rest = lambda t: jnp.concatenate([t[n][l] for n in REST], axis=0)
        res_rest.append(adamw(recv_rest, layer_rest(w), layer_rest(mom), layer_rest(var), REST_TILE, "adamw_rest"))
    res_win = [jnp.stack([res_win[0][q], res_win[1][q]]) for q in range(4)]
    res_rest = [jnp.concatenate([res_rest[0][q], res_rest[1][q]], axis=0) for q in range(4)]
    small_names = [n for n, _ in SMALL_SIZES if n != "conv_w_full"]
    g_s, dw_s, nm_s, nv_s = adamw(recv_small, _pack_small({n: w[n] for n in small_names}),
                                  _pack_small({n: mom[n] for n in small_names}),
                                  _pack_small({n: var[n] for n in small_names}), SMALL_TILE, "adamw_small")
    shapes = {n: w[n].shape for n in small_names}
    shapes["conv_w_full"] = (2, 4, CONV_DIM)
    g_conv_full = _unpack_small(g_s, shapes)["conv_w_full"]
    g_conv = lax.dynamic_slice_in_dim(g_conv_full, my_dev * 384, 384, axis=2)
    pack_conv = lambda a: _pad_rows(a.reshape(-1, D_MODEL), 8)
    g_c, dw_c, nm_c, nv_c = adamw(pack_conv(g_conv)[None], pack_conv(ssm_conv_w), pack_conv(m_ssm_conv_w),
                                  pack_conv(v_ssm_conv_w), 8, "adamw_conv")

    results = {}
    for q, (tag, psm, pc) in enumerate((("grad", g_s, g_c), ("delta", dw_s, dw_c), ("new_m", nm_s, nm_c),
                                        ("new_v", nv_s, nv_c))):
        r = dict(zip(REST, _unpack_rest(res_rest[q])))
        r["w_in"] = res_win[q]
        r.update(_unpack_small(psm, {n: w[n].shape for n in small_names}))
        r["ssm_conv_w"] = pc[0:3].reshape(2, 4, 384)
        results[tag] = r
    outs = [loss, grad_x[None]]
    for tag in ("grad", "delta", "new_m", "new_v"):
        outs += [results[tag][n] for n in WEIGHTS]
    return tuple(outs)
```

```python
import math

import jax
import jax.numpy as jnp
from jax import lax
from jax.experimental import pallas as pl
from jax.experimental.pallas import tpu as pltpu

F32 = jnp.float32
MXU_DTYPE = jnp.bfloat16
ACT_DTYPE = jnp.bfloat16
WIRE_DTYPE = jnp.bfloat16
HI = lax.Precision.HIGHEST
MESH = pl.DeviceIdType.MESH

D_MODEL = 1024
N_DEV = 8
ATT_HEADS = 16
HEAD_DIM = 64
BLK = 128
SG_GROUPS = 8
SSM_WIDTH = 2048
SSM_HEADS = 32
SSM_GROUPS = 4
SSM_GW = SSM_WIDTH // SSM_GROUPS
CONV_DIM = 3072
REL_BUCKETS = 32
EPS = 1e-6
NEG = -1e30

ATT_COLS = 2304
SG_COLS = 3072
SSM_COLS = 5376
GATE_COLS = 3072
DT_OFF = 5120

VMEM_LIMIT_V7X = 56 * 2 ** 20

ADAM_LR, ADAM_B1, ADAM_B2, ADAM_EPS, ADAM_WD, ADAM_STEP = 0.001, 0.9, 0.999, 1e-08, 0.01, 10

WIN_SHARD = 1700
WIN_LANES = 1792
REST_PARTS = (128, 128, 256, 128)
LAYER_REST = sum(REST_PARTS)
REST_TILE = 128
WIN_TILE = 128
SMALL_ROWS = 384
SMALL_TILE = 128


def _cparams(sem=None):
    return pltpu.CompilerParams(dimension_semantics=sem, vmem_limit_bytes=VMEM_LIMIT_V7X)


def _dot(a, b):
    return jnp.dot(a.astype(MXU_DTYPE), b.astype(MXU_DTYPE), preferred_element_type=F32)


def _dot_nt(a, b):
    return lax.dot_general(a.astype(MXU_DTYPE), b.astype(MXU_DTYPE), (((1,), (1,)), ((), ())),
                           preferred_element_type=F32)


def _dot_tn(a, b):
    return lax.dot_general(a.astype(MXU_DTYPE), b.astype(MXU_DTYPE), (((0,), (0,)), ((), ())),
                           preferred_element_type=F32)


def _dot_hi(a, b):
    return jnp.dot(a, b, precision=HI, preferred_element_type=F32)


def _dot_onehot(a, onehot):
    hi = a.astype(jnp.bfloat16)
    lo = (a - hi.astype(F32)).astype(jnp.bfloat16)
    return (jnp.dot(hi, onehot, preferred_element_type=F32) + jnp.dot(lo, onehot, preferred_element_type=F32))


def _dot_hi_nt(a, b):
    return lax.dot_general(a, b, (((1,), (1,)), ((), ())), precision=HI, preferred_element_type=F32)


def _sig(x):
    return 1.0 / (1.0 + jnp.exp(-x))


def _dsilu(x, s):
    return s * (1.0 + x * (1.0 - s))


def _full(shape):
    nd = len(shape)
    return pl.BlockSpec(shape, lambda *_: (0,) * nd)


def rmsnorm_fwd(x, g):
    s, d = x.shape
    tm = min(512, s)

    def body(x_ref, g_ref, o_ref):
        xv = x_ref[...]
        r = lax.rsqrt(jnp.mean(xv * xv, axis=-1, keepdims=True) + EPS)
        o_ref[...] = (xv * r * g_ref[...]).astype(o_ref.dtype)

    return pl.pallas_call(
        body, name="rmsnorm_fwd", grid=(s // tm,),
        in_specs=[pl.BlockSpec((tm, d), lambda i: (i, 0)), _full((1, d))],
        out_specs=pl.BlockSpec((tm, d), lambda i: (i, 0)),
        out_shape=jax.ShapeDtypeStruct((s, d), ACT_DTYPE),
        compiler_params=_cparams(("parallel",)),
    )(x, g)


def mm_nn(a, b, tn, name):
    s, k = a.shape
    n = b.shape[1]
    tm = min(2048, s)

    def body(a_ref, b_ref, o_ref):
        o_ref[...] = _dot(a_ref[...], b_ref[...]).astype(o_ref.dtype)

    return pl.pallas_call(
        body, name=name, grid=(s // tm, n // tn),
        in_specs=[pl.BlockSpec((tm, k), lambda i, j: (i, 0)), pl.BlockSpec((k, tn), lambda i, j: (0, j))],
        out_specs=pl.BlockSpec((tm, tn), lambda i, j: (i, j)),
        out_shape=jax.ShapeDtypeStruct((s, n), ACT_DTYPE),
        compiler_params=_cparams(("parallel", "arbitrary")),
    )(a, b)


def mm_nt(a, b, tm, name):
    m, k = a.shape
    s = b.shape[0]
    ts = min(2048, s)

    def body(a_ref, b_ref, o_ref):
        o_ref[...] = _dot_nt(a_ref[...], b_ref[...]).astype(o_ref.dtype)

    return pl.pallas_call(
        body, name=name, grid=(s // ts, m // tm),
        in_specs=[pl.BlockSpec((tm, k), lambda i, j: (j, 0)), pl.BlockSpec((ts, k), lambda i, j: (i, 0))],
        out_specs=pl.BlockSpec((tm, ts), lambda i, j: (j, i)),
        out_shape=jax.ShapeDtypeStruct((m, s), ACT_DTYPE),
        compiler_params=_cparams(("parallel", "arbitrary")),
    )(a, b)


def mm_kn(a, b, tm, name):
    m, s = a.shape
    n = b.shape[1]
    ts = min(512, s)

    def body(a_ref, b_ref, o_ref):
        @pl.when(pl.program_id(1) == 0)
        def _():
            o_ref[...] = jnp.zeros_like(o_ref)

        o_ref[...] += _dot(a_ref[...], b_ref[...])

    return pl.pallas_call(
        body, name=name, grid=(m // tm, s // ts),
        in_specs=[pl.BlockSpec((tm, ts), lambda j, t: (j, t)), pl.BlockSpec((ts, n), lambda j, t: (t, 0))],
        out_specs=pl.BlockSpec((tm, n), lambda j, t: (j, 0)),
        out_shape=jax.ShapeDtypeStruct((m, n), F32),
        compiler_params=_cparams(("parallel", "arbitrary")),
    )(a, b)


def mm_tn(a, b, tn, name):
    s, k = a.shape
    n = b.shape[1]
    ts = min(512, s)

    def body(a_ref, b_ref, o_ref):
        @pl.when(pl.program_id(1) == 0)
        def _():
            o_ref[...] = jnp.zeros_like(o_ref)

        o_ref[...] += _dot_tn(a_ref[...], b_ref[...])

    return pl.pallas_call(
        body, name=name, grid=(n // tn, s // ts),
        in_specs=[pl.BlockSpec((ts, k), lambda j, t: (t, 0)), pl.BlockSpec((ts, tn), lambda j, t: (t, j))],
        out_specs=pl.BlockSpec((k, tn), lambda j, t: (0, j)),
        out_shape=jax.ShapeDtypeStruct((k, n), F32),
        compiler_params=_cparams(("parallel", "arbitrary")),
    )(a, b)


def dh_norm_bwd(dslabs, wslabs, x, g, dres, ex=None):
    s, d = x.shape
    tm = min(1024, s)
    tk = 768
    counts = [ds.shape[0 if q == 0 else 1] // tk for q, ds in enumerate(dslabs)]
    starts = [sum(counts[:i]) for i in range(len(counts))]
    nk = sum(counts)
    ns = len(dslabs)

    hosted = ex is not None
    ni = s // tm

    def mm_body(*refs):
        (own_in, (dh_ref,), _), hosted_refs = _split_hosted(refs, 2 * ns, 1, 0, ex)
        d_refs, w_refs = own_in[:ns], own_in[ns:]
        i, k = pl.program_id(0), pl.program_id(1)
        if hosted:
            @pl.when((i == 0) & (k == 0))
            def _():
                ex.start(*hosted_refs)

            @pl.when((i == ni - 1) & (k == nk - 1))
            def _():
                ex.relay(*hosted_refs)
                ex.wait(*hosted_refs)

        @pl.when(k == 0)
        def _():
            dh_ref[...] = jnp.zeros_like(dh_ref)

        for q in range(ns):
            @pl.when((k >= starts[q]) & (k < starts[q] + counts[q]))
            def _(q=q):
                if q == 0:
                    dh_ref[...] += _dot_tn(d_refs[q][...], w_refs[q][...])
                else:
                    dh_ref[...] += _dot_nt(d_refs[q][...], w_refs[q][...])

    def clamp(q):
        if q == 0:
            return pl.BlockSpec((tk, tm), lambda i, k: (jnp.clip(k - starts[q], 0, counts[q] - 1), i))
        return pl.BlockSpec((tm, tk), lambda i, k: (i, jnp.clip(k - starts[q], 0, counts[q] - 1)))

    def clamp_w(q):
        if q == 0:
            return pl.BlockSpec((tk, d), lambda i, k: (jnp.clip(k - starts[q], 0, counts[q] - 1), 0))
        return pl.BlockSpec((d, tk), lambda i, k: (0, jnp.clip(k - starts[q], 0, counts[q] - 1)))

    res = pl.pallas_call(
        mm_body, name="dh_matmul_scatter" if hosted else "dh_matmul", grid=(ni, nk),
        in_specs=([clamp(q) for q in range(ns)] + [clamp_w(q) for q in range(ns)]
                  + (ex.in_specs if hosted else [])),
        out_specs=[pl.BlockSpec((tm, d), lambda i, k: (i, 0))] + (ex.out_specs if hosted else []),
        out_shape=[jax.ShapeDtypeStruct((s, d), F32)] + (ex.out_shape if hosted else []),
        scratch_shapes=ex.scratch if hosted else [],
        compiler_params=_cparams(("arbitrary" if hosted else "parallel", "arbitrary")),
    )(*dslabs, *wslabs, *(ex.arrays if hosted else []))
    dh, ex_results = res[0], res[1:]

    te = min(512, s)

    def norm_body(dh_ref, x_ref, g_ref, dres_ref, dx_ref, dg_ref):
        @pl.when(pl.program_id(0) == 0)
        def _():
            dg_ref[...] = jnp.zeros_like(dg_ref)

        xv = x_ref[...]
        r = lax.rsqrt(jnp.mean(xv * xv, axis=-1, keepdims=True) + EPS)
        xn = xv * r
        dhv = dh_ref[...]
        dg_ref[...] += jnp.sum(dhv * xn, axis=0, keepdims=True)
        dxn = dhv * g_ref[...]
        dx_ref[...] = dres_ref[...] + r * (dxn - xn * jnp.mean(dxn * xn, axis=-1, keepdims=True))

    rows = pl.BlockSpec((te, d), lambda i: (i, 0))
    dx, dg = pl.pallas_call(
        norm_body, name="norm_bwd", grid=(s // te,),
        in_specs=[rows, rows, _full((1, d)), rows],
        out_specs=[rows, _full((1, d))],
        out_shape=[jax.ShapeDtypeStruct((s, d), F32), jax.ShapeDtypeStruct((1, d), F32)],
        compiler_params=_cparams(("arbitrary",)),
    )(dh, x, g, dres)
    return (dx, dg, ex_results) if hosted else (dx, dg)


def bias_table(rel_bias_t, onehot_t, maskadd):
    n = onehot_t.shape[1]
    tn = 8192

    def body(r_ref, o_ref, m_ref, out_ref):
        out_ref[...] = _dot_hi(r_ref[...], o_ref[...]) + m_ref[...]

    return pl.pallas_call(
        body, name="bias_table", grid=(n // tn,),
        in_specs=[_full((ATT_HEADS, REL_BUCKETS)), pl.BlockSpec((REL_BUCKETS, tn), lambda j: (0, j)),
                  pl.BlockSpec((1, tn), lambda j: (0, j))],
        out_specs=pl.BlockSpec((ATT_HEADS, tn), lambda j: (0, j)),
        out_shape=jax.ShapeDtypeStruct((ATT_HEADS, n), F32),
        compiler_params=_cparams(("parallel",)),
    )(rel_bias_t, onehot_t, maskadd)


def bias_table_bwd(dbias, onehot_t):
    n = onehot_t.shape[1]
    tn = 8192

    def body(d_ref, o_ref, out_ref):
        @pl.when(pl.program_id(0) == 0)
        def _():
            out_ref[...] = jnp.zeros_like(out_ref)

        out_ref[...] += _dot_hi_nt(d_ref[...], o_ref[...])

    return pl.pallas_call(
        body, name="bias_table_bwd", grid=(n // tn,),
        in_specs=[pl.BlockSpec((ATT_HEADS, tn), lambda j: (0, j)), pl.BlockSpec((REL_BUCKETS, tn), lambda j: (0, j))],
        out_specs=_full((ATT_HEADS, REL_BUCKETS)),
        out_shape=jax.ShapeDtypeStruct((ATT_HEADS, REL_BUCKETS), F32),
        compiler_params=_cparams(("arbitrary",)),
    )(dbias, onehot_t)


def _fold(full, tri):
    return jnp.where(tri, full[BLK:2 * BLK], full[0:BLK])


def _unfold(folded, tri):
    return jnp.concatenate([jnp.where(tri, 0.0, folded), jnp.where(tri, folded, 0.0)], axis=0)


GROUP_HEADS = ATT_HEADS // 2
GROUP_LANES = GROUP_HEADS * BLK


def _att_group(qg, kcat, vt_cat, bias_g, sink_g, tri, no_prev):
    l = _fold(_dot(kcat, qg), tri) * (HEAD_DIM ** -0.5) + bias_g
    l = jnp.where(no_prev, NEG, l)
    m = jnp.maximum(jnp.max(l, axis=0, keepdims=True), sink_g)
    p = jnp.exp(l - m)
    es = jnp.exp(sink_g - m)
    inv = 1.0 / (jnp.sum(p, axis=0, keepdims=True) + es)
    p = p * inv
    pcat = _unfold(p, tri)
    return p, pcat, es * inv, _dot(vt_cat, pcat)


def _heads_to_lanes(ref, row0):
    return jnp.concatenate([ref[row0 + j * HEAD_DIM:row0 + (j + 1) * HEAD_DIM, :] for j in range(GROUP_HEADS)], axis=1)


def _lanes_to_heads(ref, row0, val):
    for j in range(GROUP_HEADS):
        ref[row0 + j * HEAD_DIM:row0 + (j + 1) * HEAD_DIM, :] = val[:, j * BLK:(j + 1) * BLK].astype(ref.dtype)


def _kv_cat(kvp, kvc, g):
    lo = g * HEAD_DIM
    kt_cat = jnp.concatenate([kvp[lo:lo + HEAD_DIM], kvc[lo:lo + HEAD_DIM]], axis=1)
    vt_cat = jnp.concatenate([kvp[128 + lo:128 + lo + HEAD_DIM], kvc[128 + lo:128 + lo + HEAD_DIM]], axis=1)
    return kt_cat, vt_cat


def _tri_masks(n):
    row = lax.broadcasted_iota(jnp.int32, (BLK, GROUP_LANES), 0)
    query = lax.broadcasted_iota(jnp.int32, (BLK, GROUP_LANES), 1) & (BLK - 1)
    tri = row <= query
    return tri, (n == 0) & jnp.logical_not(tri)


def _split_hosted(refs, n_in, n_out, n_scratch, ex):
    na = ex.na if ex is not None else 0
    o = 0
    parts = []
    for cnt in (n_in, na, n_out, na, n_scratch, 3 if ex is not None else 0):
        parts.append(refs[o:o + cnt])
        o += cnt
    own_in, ex_in, own_out, ex_out, own_scr, ex_sems = parts
    return (own_in, own_out, own_scr), (ex_in, ex_out, ex_sems)


def _call_hosting(body, name, nsteps, in_specs, out_specs, out_shape, scratch, args, ex):
    n_in, n_out, n_scr = len(in_specs), len(out_specs), len(scratch)
    hosted = ex is not None

    def full_body(*refs):
        (own_in, own_out, own_scr), hosted_refs = _split_hosted(refs, n_in, n_out, n_scr, ex)
        if hosted:
            @pl.when(pl.program_id(0) == 0)
            def _():
                ex.start(*hosted_refs)

            @pl.when(pl.program_id(0) == max(nsteps - 4, 0))
            def _():
                ex.relay(*hosted_refs)

            @pl.when(pl.program_id(0) == nsteps - 1)
            def _():
                ex.wait(*hosted_refs)

        body(*own_in, *own_out, *own_scr)

    res = pl.pallas_call(
        full_body, name=name + "_hosting" if hosted else name, grid=(nsteps,),
        in_specs=list(in_specs) + (ex.in_specs if hosted else []),
        out_specs=list(out_specs) + (ex.out_specs if hosted else []),
        out_shape=list(out_shape) + (ex.out_shape if hosted else []),
        scratch_shapes=list(scratch) + (ex.scratch if hosted else []),
        compiler_params=_cparams(("arbitrary",)),
    )(*args, *(ex.arrays if hosted else []))
    return res[:n_out], res[n_out:]


def attn_fwd(pa, bias, sinks):
    s = pa.shape[1]
    nb = s // BLK

    def body(pa_ref, kvp_ref, bias_ref, sink_ref, y_ref):
        n = pl.program_id(0)
        kvc = pa_ref[2048:2304, :]
        kvp = kvp_ref[...]
        tri, no_prev = _tri_masks(n)
        for g in range(2):
            kt_cat, vt_cat = _kv_cat(kvp, kvc, g)
            row0 = g * GROUP_HEADS * HEAD_DIM
            _, _, _, o = _att_group(_heads_to_lanes(pa_ref, row0), kt_cat.astype(F32).T, vt_cat, bias_ref[g],
                                    sink_ref[g:g + 1, :], tri, no_prev)
            z = _heads_to_lanes(pa_ref, 1024 + row0).astype(F32)
            _lanes_to_heads(y_ref, row0, o * z * _sig(z))

    return pl.pallas_call(
        body, name="attn_fwd", grid=(nb,),
        in_specs=[pl.BlockSpec((ATT_COLS, BLK), lambda n: (0, n)),
                  pl.BlockSpec((256, BLK), lambda n: (8, jnp.maximum(n - 1, 0))),
                  _full((2, BLK, GROUP_LANES)), _full((2, GROUP_LANES))],
        out_specs=pl.BlockSpec((1024, BLK), lambda n: (0, n)),
        out_shape=jax.ShapeDtypeStruct((1024, s), ACT_DTYPE),
        compiler_params=_cparams(("parallel",)),
    )(pa, pa, bias, sinks)


def attn_bwd(pa, dy, bias, sinks, dbias_in):
    s = pa.shape[1]
    nb = s // BLK

    def body(pa_ref, kvp_ref, dy_ref, bias_ref, sink_ref, dbin_ref, dpa_ref, dbias_ref, dsink_ref, carry, dsink_acc):
        i = pl.program_id(0)
        n = nb - 1 - i

        @pl.when(i == 0)
        def _():
            dbias_ref[...] = dbin_ref[...]
            dsink_acc[...] = jnp.zeros_like(dsink_acc)
            carry[...] = jnp.zeros_like(carry)

        kvc = pa_ref[2048:2304, :]
        kvp = kvp_ref[...]
        tri, no_prev = _tri_masks(n)
        scale = HEAD_DIM ** -0.5
        for g in range(2):
            kt_cat, vt_cat = _kv_cat(kvp, kvc, g)
            row0 = g * GROUP_HEADS * HEAD_DIM
            qg = _heads_to_lanes(pa_ref, row0)
            p, pcat, psink, o = _att_group(qg, kt_cat.astype(F32).T, vt_cat, bias_ref[g], sink_ref[g:g + 1, :], tri,
                                           no_prev)
            z = _heads_to_lanes(pa_ref, 1024 + row0).astype(F32)
            dyg = _heads_to_lanes(dy_ref, row0).astype(F32)
            sz = _sig(z)
            d_o = dyg * z * sz
            _lanes_to_heads(dpa_ref, 1024 + row0, dyg * _dsilu(z, sz) * o)
            delta = jnp.sum(d_o * o, axis=0, keepdims=True)
            dl = p * (_fold(_dot(vt_cat.astype(F32).T, d_o), tri) - delta)
            dsink_acc[g:g + 1, :] += psink * delta
            dbias_ref[g] += dl
            dlcat = _unfold(dl, tri)
            _lanes_to_heads(dpa_ref, row0, _dot(kt_cat, dlcat) * scale)
            for q, dkv in enumerate((_dot_nt(qg, dlcat) * scale, _dot_nt(d_o, pcat))):
                r0 = q * 128 + g * HEAD_DIM
                dpa_ref[2048 + r0:2048 + r0 + HEAD_DIM, :] = (
                    dkv[:, BLK:2 * BLK] + carry[r0:r0 + HEAD_DIM, :]).astype(dpa_ref.dtype)
                carry[r0:r0 + HEAD_DIM, :] = dkv[:, 0:BLK]

        @pl.when(i == nb - 1)
        def _():
            lane = lax.broadcasted_iota(jnp.int32, (1, 128), 1)
            dsink = jnp.zeros((1, 128), F32)
            for h in range(ATT_HEADS):
                g, j = divmod(h, GROUP_HEADS)
                tot = jnp.sum(dsink_acc[g:g + 1, j * BLK:(j + 1) * BLK], axis=1, keepdims=True)
                dsink = dsink + jnp.where(lane == h, -tot, 0.0)
            dsink_ref[...] = dsink

    return pl.pallas_call(
        body, name="attn_bwd", grid=(nb,),
        in_specs=[pl.BlockSpec((ATT_COLS, BLK), lambda i: (0, nb - 1 - i)),
                  pl.BlockSpec((256, BLK), lambda i: (8, jnp.maximum(nb - 2 - i, 0))),
                  pl.BlockSpec((1024, BLK), lambda i: (0, nb - 1 - i)),
                  _full((2, BLK, GROUP_LANES)), _full((2, GROUP_LANES)), _full((2, BLK, GROUP_LANES))],
        out_specs=[pl.BlockSpec((ATT_COLS, BLK), lambda i: (0, nb - 1 - i)),
                   _full((2, BLK, GROUP_LANES)), _full((1, 128))],
        out_shape=[jax.ShapeDtypeStruct((ATT_COLS, s), ACT_DTYPE),
                   jax.ShapeDtypeStruct((2, BLK, GROUP_LANES), F32),
                   jax.ShapeDtypeStruct((1, 128), F32)],
        scratch_shapes=[pltpu.VMEM((256, BLK), F32), pltpu.VMEM((2, GROUP_LANES), F32)],
        compiler_params=_cparams(("arbitrary",)),
    )(pa, pa, dy, bias, sinks, dbias_in)


def _layernorm(v, g, b):
    mu = jnp.mean(v, axis=-1, keepdims=True)
    vc = v - mu
    rstd = lax.rsqrt(jnp.mean(vc * vc, axis=-1, keepdims=True) + EPS)
    xhat = vc * rstd
    return xhat, rstd, xhat * g + b


def sgu_fwd(ps, ln_g, ln_b, w_tril, b_t):
    s = ps.shape[0]

    def body(ps_ref, g_ref, b_ref, w_ref, bt_ref, y_ref):
        u = ps_ref[:, 0:1024].astype(F32)
        v = ps_ref[:, 1024:2048].astype(F32)
        z = ps_ref[:, 2048:3072].astype(F32)
        _, _, vn = _layernorm(v, g_ref[...], b_ref[...])
        gate = u * z * _sig(z)
        for g in range(SG_GROUPS):
            sl = slice(g * 128, (g + 1) * 128)
            mixed = _dot(w_ref[g], vn[:, sl]) + bt_ref[:, g:g + 1]
            y_ref[:, sl] = (gate[:, sl] * mixed).astype(y_ref.dtype)

    return pl.pallas_call(
        body, name="sgu_fwd", grid=(s // BLK,),
        in_specs=[pl.BlockSpec((BLK, SG_COLS), lambda c: (c, 0)), _full((1, 1024)), _full((1, 1024)),
                  _full((SG_GROUPS, BLK, BLK)), _full((BLK, 128))],
        out_specs=pl.BlockSpec((BLK, 1024), lambda c: (c, 0)),
        out_shape=jax.ShapeDtypeStruct((s, 1024), ACT_DTYPE),
        compiler_params=_cparams(("parallel",)),
    )(ps, ln_g, ln_b, w_tril, b_t)


def sgu_bwd(ps, dy, ln_g, ln_b, w_tril, w_tril_t, b_t):
    s = ps.shape[0]

    def body(ps_ref, dy_ref, g_ref, b_ref, w_ref, wt_ref, bt_ref, dps_ref, dw_ref, dbt_ref, dg_ref, db_ref, dvn_scr):
        @pl.when(pl.program_id(0) == 0)
        def _():
            dw_ref[...] = jnp.zeros_like(dw_ref)
            dbt_ref[...] = jnp.zeros_like(dbt_ref)
            dg_ref[...] = jnp.zeros_like(dg_ref)
            db_ref[...] = jnp.zeros_like(db_ref)

        u = ps_ref[:, 0:1024].astype(F32)
        v = ps_ref[:, 1024:2048].astype(F32)
        z = ps_ref[:, 2048:3072].astype(F32)
        dy = dy_ref[...].astype(F32)
        xhat, rstd, vn = _layernorm(v, g_ref[...], b_ref[...])
        sz = _sig(z)
        silu = z * sz
        row = lax.broadcasted_iota(jnp.int32, (BLK, BLK), 0)
        colm = lax.broadcasted_iota(jnp.int32, (BLK, BLK), 1)
        tril = row >= colm
        dbt = jnp.zeros((BLK, 128), F32)
        for g in range(SG_GROUPS):
            sl = slice(g * 128, (g + 1) * 128)
            vng = vn[:, sl]
            mixed = _dot(w_ref[g], vng) + bt_ref[:, g:g + 1]
            dyg, ug = dy[:, sl], u[:, sl]
            dps_ref[:, sl] = (dyg * mixed * silu[:, sl]).astype(dps_ref.dtype)
            dps_ref[:, 2048 + g * 128:2048 + (g + 1) * 128] = (
                dyg * ug * mixed * _dsilu(z[:, sl], sz[:, sl])).astype(dps_ref.dtype)
            dm = dyg * ug * silu[:, sl]
            dw_ref[g] += jnp.where(tril, _dot_nt(dm, vng), 0.0)
            dbt = dbt + jnp.where(colm == g, jnp.sum(dm, axis=1, keepdims=True), 0.0)
            dvn_scr[:, sl] = _dot(wt_ref[g], dm)
        dbt_ref[...] += dbt
        dvn = dvn_scr[...]
        dg_ref[...] += jnp.sum(dvn * xhat, axis=0, keepdims=True)
        db_ref[...] += jnp.sum(dvn, axis=0, keepdims=True)
        dxh = dvn * g_ref[...]
        dv = rstd * (dxh - jnp.mean(dxh, axis=-1, keepdims=True)
                     - xhat * jnp.mean(dxh * xhat, axis=-1, keepdims=True))
        dps_ref[:, 1024:2048] = dv.astype(dps_ref.dtype)

    return pl.pallas_call(
        body, name="sgu_bwd", grid=(s // BLK,),
        in_specs=[pl.BlockSpec((BLK, SG_COLS), lambda c: (c, 0)), pl.BlockSpec((BLK, 1024), lambda c: (c, 0)),
                  _full((1, 1024)), _full((1, 1024)), _full((SG_GROUPS, BLK, BLK)), _full((SG_GROUPS, BLK, BLK)),
                  _full((BLK, 128))],
        out_specs=[pl.BlockSpec((BLK, SG_COLS), lambda c: (c, 0)), _full((SG_GROUPS, BLK, BLK)), _full((BLK, 128)),
                   _full((1, 1024)), _full((1, 1024))],
        out_shape=[jax.ShapeDtypeStruct((s, SG_COLS), ACT_DTYPE), jax.ShapeDtypeStruct((SG_GROUPS, BLK, BLK), F32),
                   jax.ShapeDtypeStruct((BLK, 128), F32), jax.ShapeDtypeStruct((1, 1024), F32),
                   jax.ShapeDtypeStruct((1, 1024), F32)],
        scratch_shapes=[pltpu.VMEM((BLK, 1024), F32)],
        compiler_params=_cparams(("arbitrary",)),
    )(ps, dy, ln_g, ln_b, w_tril, w_tril_t, b_t)


def _shift_down(cur, prev16, k):
    if k == 0:
        return cur
    r = pltpu.roll(cur, k, 0)
    rp = pltpu.roll(prev16, k, 0)
    row = lax.broadcasted_iota(jnp.int32, (8, cur.shape[1]), 0)
    return jnp.concatenate([jnp.where(row < k, rp[0:8], r[0:8]), r[8:]], axis=0)


def _shift_up(cur, next16, k):
    if k == 0:
        return cur
    n = cur.shape[0]
    r = pltpu.roll(cur, n - k, 0)
    rn = pltpu.roll(next16, 16 - k, 0)
    row = lax.broadcasted_iota(jnp.int32, (8, cur.shape[1]), 0)
    return jnp.concatenate([r[:n - 8], jnp.where(row >= 8 - k, rn[8:16], r[n - 8:])], axis=0)


def _bcast8(v):
    return jnp.broadcast_to(v, (16, v.shape[1]))


class _Ssd:
    def __init__(self, xbc, prev16, dtr, cw, cbias, dtb, alog, dsk, tri, e):
        pre = cbias + cw[3:4] * xbc
        self.shifted = [xbc]
        for k in (1, 2, 3):
            sh = _shift_down(xbc, prev16, k)
            self.shifted.append(sh)
            pre = pre + cw[3 - k:4 - k] * sh
        self.pre = pre
        self.sg = _sig(pre)
        act = pre * self.sg
        self.xs = act[:, 0:SSM_WIDTH]
        self.bm = act[:, SSM_WIDTH:SSM_WIDTH + 512]
        self.cm = act[:, SSM_WIDTH + 512:CONV_DIM]
        self.dtp = dtr + dtb
        self.dt = jnp.maximum(self.dtp, 0.0) + jnp.log(1.0 + jnp.exp(-jnp.abs(self.dtp)))
        self.a = -jnp.exp(alog)
        self.acs = _dot_hi(tri, self.dt * self.a)
        self.acs_t = self.acs.T
        tot = self.acs[BLK - 1:BLK]
        self.ecs = jnp.exp(self.acs)
        self.dte = jnp.exp(tot - self.acs)
        self.cd = jnp.exp(tot)
        self.dt_x = _dot_onehot(self.dt, e)
        self.ecs_x = _dot_onehot(self.ecs, e)
        self.dte_x = _dot_onehot(self.dte, e)
        self.cd_x = _dot_onehot(_bcast8(self.cd), e)[0:1]
        self.d_x = _dot_onehot(_bcast8(dsk), e)[0:1]
        self.xdt = self.xs * self.dt_x
        row = lax.broadcasted_iota(jnp.int32, (BLK, BLK), 0)
        col = lax.broadcasted_iota(jnp.int32, (BLK, BLK), 1)
        self.tril = row >= col

    def group(self, g):
        sl = slice(g * 128, (g + 1) * 128)
        bg, cg = self.bm[:, sl], self.cm[:, sl]
        return bg, cg, _dot_nt(cg, bg)

    def decay(self, h):
        seg = self.acs[:, h:h + 1] - self.acs_t[h:h + 1, :]
        return jnp.exp(jnp.where(self.tril, seg, NEG))

    def y_pre_gate(self, ht_of, yd_scr, yoff_scr):
        for g in range(SSM_GROUPS):
            bg, cg, cb = self.group(g)
            for j in range(8):
                h = g * 8 + j
                sl = slice(h * 64, (h + 1) * 64)
                yd_scr[:, sl] = _dot(cb * self.decay(h), self.xdt[:, sl])
            gs = slice(g * SSM_GW, (g + 1) * SSM_GW)
            yoff_scr[:, gs] = _dot(cg, ht_of(g)) * self.ecs_x[:, gs]
        return yd_scr[...] + yoff_scr[...] + self.d_x * self.xs


def _ssd_consts():
    hh = lax.broadcasted_iota(jnp.int32, (128, SSM_WIDTH), 0)
    ch = lax.broadcasted_iota(jnp.int32, (128, SSM_WIDTH), 1)
    e = (ch // 64 == hh).astype(jnp.bfloat16)
    row = lax.broadcasted_iota(jnp.int32, (BLK, BLK), 0)
    col = lax.broadcasted_iota(jnp.int32, (BLK, BLK), 1)
    tri = (row >= col).astype(F32)
    return tri, e


def _pad_lanes(v, n=128):
    return jnp.pad(v, ((0, 0), (0, n - v.shape[1])))


def ssd_fwd(pm, cw, cbias, dtb, alog, dsk, ng, ex=None):
    s = pm.shape[0]
    nc = s // BLK
    tri, e = _ssd_consts()

    def body(pm_ref, prev_ref, cw_ref, cb_ref, dtb_ref, al_ref, d_ref, ng_ref, tri_ref, e_ref,
             y_ref, st_ref, ht_ref, yd_scr, yoff_scr):
        c = pl.program_id(0)

        @pl.when(c == 0)
        def _():
            ht_ref[...] = jnp.zeros_like(ht_ref)

        xbc = pm_ref[:, 0:CONV_DIM].astype(F32)
        prev16 = jnp.where(c == 0, 0.0, prev_ref[...].astype(F32))
        f = _Ssd(xbc, prev16, pm_ref[:, DT_OFF:DT_OFF + 128].astype(F32), cw_ref[...], cb_ref[...], dtb_ref[...],
                 al_ref[...], d_ref[...], tri_ref[...], e_ref[...])
        st_ref[0] = ht_ref[...]
        y = f.y_pre_gate(lambda g: ht_ref[g], yd_scr, yoff_scr)
        for g in range(SSM_GROUPS):
            bg, _, _ = f.group(g)
            gs = slice(g * SSM_GW, (g + 1) * SSM_GW)
            ht_ref[g] = ht_ref[g] * f.cd_x[:, gs] + _dot_tn(bg, f.xdt[:, gs] * f.dte_x[:, gs])
        z = pm_ref[:, CONV_DIM:CONV_DIM + SSM_WIDTH].astype(F32)
        ypre = y * z * _sig(z)
        for g in range(SSM_GROUPS):
            gs = slice(g * SSM_GW, (g + 1) * SSM_GW)
            yg = ypre[:, gs]
            rr = lax.rsqrt(jnp.mean(yg * yg, axis=-1, keepdims=True) + EPS)
            y_ref[:, gs] = (yg * rr * ng_ref[:, gs]).astype(y_ref.dtype)

    own, hosted = _call_hosting(
        body, "ssd_fwd", nc,
        in_specs=[pl.BlockSpec((BLK, SSM_COLS), lambda c: (c, 0)),
                  pl.BlockSpec((16, CONV_DIM), lambda c: (jnp.maximum(8 * c - 1, 0), 0)),
                  _full((4, CONV_DIM)), _full((1, CONV_DIM)), _full((1, 128)), _full((1, 128)), _full((1, 128)),
                  _full((1, SSM_WIDTH)), _full((BLK, BLK)), _full((128, SSM_WIDTH))],
        out_specs=[pl.BlockSpec((BLK, SSM_WIDTH), lambda c: (c, 0)),
                   pl.BlockSpec((1, SSM_GROUPS, 128, SSM_GW), lambda c: (c, 0, 0, 0))],
        out_shape=[jax.ShapeDtypeStruct((s, SSM_WIDTH), ACT_DTYPE),
                   jax.ShapeDtypeStruct((nc, SSM_GROUPS, 128, SSM_GW), F32)],
        scratch=[pltpu.VMEM((SSM_GROUPS, 128, SSM_GW), F32), pltpu.VMEM((BLK, SSM_WIDTH), F32),
                 pltpu.VMEM((BLK, SSM_WIDTH), F32)],
        args=(pm, pm, cw, cbias, dtb, alog, dsk, ng, tri, e), ex=ex)
    return (*own, hosted) if ex is not None else tuple(own)


def ssd_bwd(pm, dy, states, cw, cbias, dtb, alog, dsk, ng, ex=None):
    s = pm.shape[0]
    nc = s // BLK
    tri, e = _ssd_consts()
    tri_t, e_t = tri.T, e.T

    def body(pm_ref, prev_ref, dy_ref, st_ref, cw_ref, cb_ref, dtb_ref, al_ref, d_ref, ng_ref,
             tri_ref, trit_ref, e_ref, et_ref,
             dpm_ref, dcw_ref, dcb_ref, dvec_ref, dng_ref,
             dht_ref, dcar_ref, yd_scr, yoff_scr, dx_scr, r2_scr, hs_scr, da_scr, dat_scr, dd_scr, dbc_scr):
        i = pl.program_id(0)
        n = nc - 1 - i

        @pl.when(i == 0)
        def _():
            dht_ref[...] = jnp.zeros_like(dht_ref)
            dcar_ref[...] = jnp.zeros_like(dcar_ref)
            dcw_ref[...] = jnp.zeros_like(dcw_ref)
            dcb_ref[...] = jnp.zeros_like(dcb_ref)
            dvec_ref[...] = jnp.zeros_like(dvec_ref)
            dng_ref[...] = jnp.zeros_like(dng_ref)
            dd_scr[...] = jnp.zeros_like(dd_scr)
            da_scr[...] = jnp.zeros_like(da_scr)
            dat_scr[...] = jnp.zeros_like(dat_scr)

        xbc = pm_ref[:, 0:CONV_DIM].astype(F32)
        prev16 = jnp.where(n == 0, 0.0, prev_ref[...].astype(F32))
        cw = cw_ref[...]
        f = _Ssd(xbc, prev16, pm_ref[:, DT_OFF:DT_OFF + 128].astype(F32), cw, cb_ref[...], dtb_ref[...],
                 al_ref[...], d_ref[...], tri_ref[...], e_ref[...])
        et = et_ref[...]
        y = f.y_pre_gate(lambda g: st_ref[0, g], yd_scr, yoff_scr)

        z = pm_ref[:, CONV_DIM:CONV_DIM + SSM_WIDTH].astype(F32)
        dyv = dy_ref[...].astype(F32)
        sz = _sig(z)
        silu = z * sz
        ypre = y * silu
        for g in range(SSM_GROUPS):
            gs = slice(g * SSM_GW, (g + 1) * SSM_GW)
            yg = ypre[:, gs]
            rr = lax.rsqrt(jnp.mean(yg * yg, axis=-1, keepdims=True) + EPS)
            nrm = yg * rr
            dng_ref[:, gs] += jnp.sum(dyv[:, gs] * nrm, axis=0, keepdims=True)
            dn = dyv[:, gs] * ng_ref[:, gs]
            dx_scr[:, gs] = rr * (dn - nrm * jnp.mean(dn * nrm, axis=-1, keepdims=True))
        dypre = dx_scr[...]
        d_y = dypre * silu
        dpm_ref[:, CONV_DIM:CONV_DIM + SSM_WIDTH] = (dypre * y * _dsilu(z, sz)).astype(dpm_ref.dtype)

        for g in range(SSM_GROUPS):
            bg, cg, cb = f.group(g)
            gs = slice(g * SSM_GW, (g + 1) * SSM_GW)
            htg = st_ref[0, g]
            dhn = dht_ref[g]
            dcb = jnp.zeros((BLK, BLK), F32)
            for j in range(8):
                h = g * 8 + j
                sl = slice(h * 64, (h + 1) * 64)
                dec = f.decay(h)
                dyh = d_y[:, sl]
                dmd = _dot_nt(dyh, f.xdt[:, sl]) * dec
                dcb = dcb + dmd
                gm = dmd * cb
                da_scr[:, h:h + 1] = jnp.sum(gm, axis=1, keepdims=True)
                dat_scr[h:h + 1, :] = jnp.sum(gm, axis=0, keepdims=True)
                dx_scr[:, sl] = _dot_tn(cb * dec, dyh)
            dz = f.ecs_x[:, gs] * d_y[:, gs]
            dbc_scr[:, 512 + g * 128:512 + (g + 1) * 128] = _dot(dcb, bg) + _dot_nt(dz, htg)
            dbc_scr[:, g * 128:(g + 1) * 128] = _dot_tn(dcb, cg) + _dot_nt(f.xdt[:, gs] * f.dte_x[:, gs], dhn)
            dws = _dot(bg, dhn)
            dx_scr[:, gs] += f.dte_x[:, gs] * dws
            r2_scr[:, gs] = dws * f.xdt[:, gs]
            hs_scr[:, gs] = _bcast8(jnp.sum(dhn * htg, axis=0, keepdims=True))
            dht_ref[g] = f.cd_x[:, gs] * dhn + _dot_tn(cg, dz)
        d_x = dx_scr[...]
        r1 = _dot_onehot(d_y * yoff_scr[...], et)
        r2 = _dot_onehot(r2_scr[...], et) * f.dte
        dcd = _dot_onehot(hs_scr[...], et)[0:1]
        d_tot = jnp.sum(r2, axis=0, keepdims=True) + dcd * f.cd
        row = lax.broadcasted_iota(jnp.int32, (BLK, 128), 0)
        d_a = da_scr[...] - dat_scr[...].T + r1 - r2 + jnp.where(row == BLK - 1, d_tot, 0.0)
        dadt = _dot_hi(trit_ref[...], d_a)
        ddt = dadt * f.a + _dot_onehot(d_x * f.xs, et)
        lane = lax.broadcasted_iota(jnp.int32, (BLK, 128), 1)
        dr = jnp.where(lane < SSM_HEADS, ddt * _sig(f.dtp), 0.0)
        dvec_ref[0:1, :] += jnp.sum(dr, axis=0, keepdims=True)
        dvec_ref[1:2, :] += jnp.sum(dadt * f.dt, axis=0, keepdims=True) * f.a
        dd_scr[...] += _bcast8(jnp.sum(d_y * f.xs, axis=0, keepdims=True))
        dpm_ref[:, DT_OFF:DT_OFF + 128] = dr.astype(dpm_ref.dtype)
        dpm_ref[:, DT_OFF + 128:SSM_COLS] = jnp.zeros((BLK, 128), dpm_ref.dtype)

        dxs = d_x * f.dt_x + f.d_x * d_y
        dact = jnp.concatenate([dxs, dbc_scr[...]], axis=1)
        dpre = dact * _dsilu(f.pre, f.sg)
        dcb_ref[...] += jnp.sum(dpre, axis=0, keepdims=True)
        dxraw = jnp.zeros((BLK, CONV_DIM), F32)
        nxt = dcar_ref[...]
        for k in range(4):
            dcw_ref[3 - k:4 - k, :] += jnp.sum(dpre * f.shifted[k], axis=0, keepdims=True)
            dxraw = dxraw + cw[3 - k:4 - k] * _shift_up(dpre, nxt, k)
        dcar_ref[...] = dpre[0:16]
        dpm_ref[:, 0:CONV_DIM] = dxraw.astype(dpm_ref.dtype)

        @pl.when(i == nc - 1)
        def _():
            dvec_ref[2:3, :] = _dot_onehot(dd_scr[...], et)[0:1]

    own, hosted = _call_hosting(
        body, "ssd_bwd", nc,
        in_specs=[pl.BlockSpec((BLK, SSM_COLS), lambda i: (nc - 1 - i, 0)),
                  pl.BlockSpec((16, CONV_DIM), lambda i: (jnp.maximum(8 * (nc - 1 - i) - 1, 0), 0)),
                  pl.BlockSpec((BLK, SSM_WIDTH), lambda i: (nc - 1 - i, 0)),
                  pl.BlockSpec((1, SSM_GROUPS, 128, SSM_GW), lambda i: (nc - 1 - i, 0, 0, 0)),
                  _full((4, CONV_DIM)), _full((1, CONV_DIM)), _full((1, 128)), _full((1, 128)), _full((1, 128)),
                  _full((1, SSM_WIDTH)), _full((BLK, BLK)), _full((BLK, BLK)), _full((128, SSM_WIDTH)),
                  _full((SSM_WIDTH, 128))],
        out_specs=[pl.BlockSpec((BLK, SSM_COLS), lambda i: (nc - 1 - i, 0)),
                   _full((8, CONV_DIM)), _full((1, CONV_DIM)), _full((8, 128)), _full((1, SSM_WIDTH))],
        out_shape=[jax.ShapeDtypeStruct((s, SSM_COLS), ACT_DTYPE), jax.ShapeDtypeStruct((8, CONV_DIM), F32),
                   jax.ShapeDtypeStruct((1, CONV_DIM), F32), jax.ShapeDtypeStruct((8, 128), F32),
                   jax.ShapeDtypeStruct((1, SSM_WIDTH), F32)],
        scratch=[pltpu.VMEM((SSM_GROUPS, 128, SSM_GW), F32), pltpu.VMEM((16, CONV_DIM), F32),
                 pltpu.VMEM((BLK, SSM_WIDTH), F32), pltpu.VMEM((BLK, SSM_WIDTH), F32),
                 pltpu.VMEM((BLK, SSM_WIDTH), F32), pltpu.VMEM((BLK, SSM_WIDTH), F32),
                 pltpu.VMEM((16, SSM_WIDTH), F32), pltpu.VMEM((BLK, 128), F32), pltpu.VMEM((128, BLK), F32),
                 pltpu.VMEM((16, SSM_WIDTH), F32), pltpu.VMEM((BLK, 1024), F32)],
        args=(pm, pm, dy, states, cw, cbias, dtb, alog, dsk, ng, tri, tri_t, e, e_t), ex=ex)
    return (*own, hosted) if ex is not None else tuple(own)


def merge_fwd(x, ya, ys, ym, pg, wa, ws, wm, wo, g_post):
    s, d = x.shape
    tm = min(256, s)

    def body(x_ref, ya_ref, ys_ref, ym_ref, pg_ref, wa_ref, ws_ref, wm_ref, wo_ref, g_ref,
             xo_ref, ba_ref, bs_ref, bm_ref, mg_ref, out_ref):
        ba = _dot_tn(ya_ref[...], wa_ref[...])
        bs = _dot(ys_ref[...], ws_ref[...])
        bm = _dot(ym_ref[...], wm_ref[...])
        merged = (_sig(pg_ref[:, 0:d].astype(F32)) * ba + _sig(pg_ref[:, d:2 * d].astype(F32)) * bs
                  + _sig(pg_ref[:, 2 * d:3 * d].astype(F32)) * bm)
        out = _dot(merged, wo_ref[...])
        r = lax.rsqrt(jnp.mean(out * out, axis=-1, keepdims=True) + EPS)
        xo_ref[...] = x_ref[...] + out * r * g_ref[...]
        ba_ref[...] = ba.astype(ba_ref.dtype)
        bs_ref[...] = bs.astype(bs_ref.dtype)
        bm_ref[...] = bm.astype(bm_ref.dtype)
        mg_ref[...] = merged.astype(mg_ref.dtype)
        out_ref[...] = out.astype(out_ref.dtype)

    rows = lambda w: pl.BlockSpec((tm, w), lambda i: (i, 0))
    act = jax.ShapeDtypeStruct((s, d), ACT_DTYPE)
    return pl.pallas_call(
        body, name="merge_fwd", grid=(s // tm,),
        in_specs=[rows(d), pl.BlockSpec((d, tm), lambda i: (0, i)), rows(d), rows(2 * d), rows(3 * d), _full((d, d)),
                  _full((d, d)), _full((2 * d, d)), _full((d, d)), _full((1, d))],
        out_specs=[rows(d)] * 6,
        out_shape=[jax.ShapeDtypeStruct((s, d), F32), act, act, act, act, act],
        compiler_params=_cparams(("parallel",)),
    )(x, ya, ys, ym, pg, wa, ws, wm, wo, g_post)


def merge_bwd(dx, out_s, pg, ba, bs, bm, wa, ws, wm, wo, g_post):
    s, d = dx.shape
    tm = min(256, s)

    def body(dx_ref, out_ref, pg_ref, ba_ref, bs_ref, bm_ref, wa_ref, ws_ref, wm_ref, wo_ref, g_ref,
             dout_ref, dba_ref, dbs_ref, dbm_ref, dpg_ref, dya_ref, dys_ref, dym_ref, dg_ref):
        @pl.when(pl.program_id(0) == 0)
        def _():
            dg_ref[...] = jnp.zeros_like(dg_ref)

        o = out_ref[...].astype(F32)
        dxv = dx_ref[...]
        r = lax.rsqrt(jnp.mean(o * o, axis=-1, keepdims=True) + EPS)
        nrm = o * r
        dg_ref[...] += jnp.sum(dxv * nrm, axis=0, keepdims=True)
        dn = dxv * g_ref[...]
        dout = r * (dn - nrm * jnp.mean(dn * nrm, axis=-1, keepdims=True))
        dout_ref[...] = dout.astype(dout_ref.dtype)
        dmerged = _dot_nt(dout, wo_ref[...])
        for q, (b_ref, db_ref, w_ref, dy_ref) in enumerate(((ba_ref, dba_ref, wa_ref, dya_ref),
                                                            (bs_ref, dbs_ref, ws_ref, dys_ref),
                                                            (bm_ref, dbm_ref, wm_ref, dym_ref))):
            gt = _sig(pg_ref[:, q * d:(q + 1) * d].astype(F32))
            db = dmerged * gt
            db_ref[...] = db.astype(db_ref.dtype)
            dpg_ref[:, q * d:(q + 1) * d] = (dmerged * b_ref[...].astype(F32) * gt * (1.0 - gt)).astype(dpg_ref.dtype)
            if q == 0:
                dy_ref[...] = _dot_nt(w_ref[...], db).astype(dy_ref.dtype)
            else:
                dy_ref[...] = _dot_nt(db, w_ref[...]).astype(dy_ref.dtype)

    rows = lambda w: pl.BlockSpec((tm, w), lambda i: (i, 0))
    act = lambda w: jax.ShapeDtypeStruct((s, w), ACT_DTYPE)
    return pl.pallas_call(
        body, name="merge_bwd", grid=(s // tm,),
        in_specs=[rows(d), rows(d), rows(3 * d), rows(d), rows(d), rows(d), _full((d, d)), _full((d, d)),
                  _full((2 * d, d)), _full((d, d)), _full((1, d))],
        out_specs=[rows(d), rows(d), rows(d), rows(d), rows(3 * d), pl.BlockSpec((d, tm), lambda i: (0, i)), rows(d),
                   rows(2 * d), _full((1, d))],
        out_shape=[act(d), act(d), act(d), act(d), act(3 * d), jax.ShapeDtypeStruct((d, s), ACT_DTYPE), act(d),
                   act(2 * d), jax.ShapeDtypeStruct((1, d), F32)],
        compiler_params=_cparams(("arbitrary",)),
    )(dx, out_s, pg, ba, bs, bm, wa, ws, wm, wo, g_post)


def loss_grad(y, target):
    s, d = y.shape
    tm = min(512, s)

    def body(y_ref, t_ref, dy_ref, l_ref):
        @pl.when(pl.program_id(0) == 0)
        def _():
            l_ref[...] = jnp.zeros_like(l_ref)

        err = y_ref[...] - t_ref[...]
        dy_ref[...] = err * (1.0 / d)
        part = jnp.sum(jnp.sum(err * err, axis=-1, keepdims=True) * (1.0 / d), axis=0, keepdims=True)
        l_ref[...] += 0.5 * jnp.broadcast_to(part, l_ref.shape)

    return pl.pallas_call(
        body, name="loss_grad", grid=(s // tm,),
        in_specs=[pl.BlockSpec((tm, d), lambda i: (i, 0)), pl.BlockSpec((tm, d), lambda i: (i, 0))],
        out_specs=[pl.BlockSpec((tm, d), lambda i: (i, 0)), _full((8, 128))],
        out_shape=[jax.ShapeDtypeStruct((s, d), F32), jax.ShapeDtypeStruct((8, 128), F32)],
        compiler_params=_cparams(("arbitrary",)),
    )(y, target)


def _mesh_pos():
    x, y, c = lax.axis_index("x"), lax.axis_index("y"), lax.axis_index("c")
    return x, y, c, 4 * x + 2 * y + c


def _peer(x, y, c, k):
    px = 1 - x if k & 4 else x
    py = 1 - y if k & 2 else y
    pc = 1 - c if k & 1 else c
    return (px, py, pc), 4 * px + 2 * py + pc


class Exchange:
    SAME_CORE = (2, 4, 6)

    def __init__(self, scattered, gathered):
        self.ns = len(scattered)
        self.arrays = list(scattered) + list(gathered)
        self.na = len(self.arrays)
        any_spec = pl.BlockSpec(memory_space=pl.ANY)
        self.in_specs = [any_spec] * self.na
        self.out_specs = [any_spec] * self.na
        self.out_shape = ([jax.ShapeDtypeStruct(a.shape, a.dtype) for a in scattered]
                          + [jax.ShapeDtypeStruct((N_DEV,) + a.shape, a.dtype) for a in gathered])
        self.scratch = [pltpu.SemaphoreType.DMA((self.na, N_DEV - 1)), pltpu.SemaphoreType.DMA((self.na, N_DEV - 1)),
                        pltpu.SemaphoreType.DMA((self.na,))]

    def _src(self, ins, q, slot):
        return ins[q].at[slot] if q < self.ns else ins[q]

    def _local(self, ins, outs, sems):
        me = _mesh_pos()[3]
        return [pltpu.make_async_copy(self._src(ins, q, me), outs[q].at[me], sems[2].at[q]) for q in range(self.na)]

    def _direct(self, ins, outs, sems, relations, arrays):
        x, y, c, me = _mesh_pos()
        copies = []
        for k in relations:
            peer, pidx = _peer(x, y, c, k)
            for q in arrays:
                copies.append(pltpu.make_async_remote_copy(
                    src_ref=self._src(ins, q, pidx), dst_ref=outs[q].at[me], send_sem=sems[0].at[q, k - 1],
                    recv_sem=sems[1].at[q, k - 1], device_id=peer, device_id_type=MESH))
        return copies

    def _arrivals(self, ins, outs, sems, relations, arrays):
        x, y, c, _ = _mesh_pos()
        copies = []
        for k in relations:
            peer, pidx = _peer(x, y, c, k)
            for q in arrays:
                copies.append(pltpu.make_async_remote_copy(
                    src_ref=self._src(ins, q, pidx), dst_ref=outs[q].at[pidx], send_sem=sems[0].at[q, k - 1],
                    recv_sem=sems[1].at[q, k - 1], device_id=peer, device_id_type=MESH))
        return copies

    def _relays(self, outs, sems):
        x, y, c, _ = _mesh_pos()
        sibling, _ = _peer(x, y, c, 1)
        copies = []
        for k in self.SAME_CORE:
            _, pidx = _peer(x, y, c, k)
            for q in range(self.ns, self.na):
                copies.append(pltpu.make_async_remote_copy(
                    src_ref=outs[q].at[pidx], dst_ref=outs[q].at[pidx], send_sem=sems[0].at[q, k],
                    recv_sem=sems[1].at[q, k], device_id=sibling, device_id_type=MESH))
        return copies

    def _sends(self, ins, outs, sems):
        return (self._direct(ins, outs, sems, range(1, N_DEV), range(self.ns))
                + self._direct(ins, outs, sems, (1,) + self.SAME_CORE, range(self.ns, self.na)))

    def start(self, ins, outs, sems):
        for cp in self._local(ins, outs, sems) + self._sends(ins, outs, sems):
            cp.start()

    def relay(self, ins, outs, sems):
        for cp in self._arrivals(ins, outs, sems, self.SAME_CORE, range(self.ns, self.na)):
            cp.wait_recv()
        for cp in self._relays(outs, sems):
            cp.start()

    def wait(self, ins, outs, sems):
        for cp in (self._arrivals(ins, outs, sems, range(1, N_DEV), range(self.ns))
                   + self._arrivals(ins, outs, sems, (1, 3, 5, 7), range(self.ns, self.na))):
            cp.wait_recv()
        for cp in self._sends(ins, outs, sems) + self._relays(outs, sems):
            cp.wait_send()
        for cp in self._local(ins, outs, sems):
            cp.wait()


def exchange(scattered, gathered, name):
    ex = Exchange(scattered, gathered)

    def body(*refs):
        ins, outs, sems = refs[:ex.na], refs[ex.na:2 * ex.na], refs[2 * ex.na:]
        ex.start(ins, outs, sems)
        ex.relay(ins, outs, sems)
        ex.wait(ins, outs, sems)

    return pl.pallas_call(body, name=name, in_specs=ex.in_specs, out_specs=ex.out_specs, out_shape=ex.out_shape,
                          scratch_shapes=ex.scratch)(*ex.arrays)


def adamw(parts_list, w, m, v, tile, name):
    npart, _, dp = parts_list[0].shape
    r, d = w.shape
    counts = [p.shape[1] // tile for p in parts_list]
    starts = [sum(counts[:q]) for q in range(len(counts))]
    n_lists = len(parts_list)

    def body(*refs):
        p_refs = refs[:n_lists]
        w_ref, m_ref, v_ref, g_ref, dw_ref, nm_ref, nv_ref = refs[n_lists:]
        i = pl.program_id(0)
        for q, p_ref in enumerate(p_refs):
            @pl.when((i >= starts[q]) & (i < starts[q] + counts[q]))
            def _(p_ref=p_ref):
                acc = p_ref[0, :, 0:d].astype(F32)
                for k in range(1, npart):
                    acc = acc + p_ref[k, :, 0:d].astype(F32)
                g_ref[...] = acc

        g = g_ref[...]
        nm = ADAM_B1 * m_ref[...] + (1.0 - ADAM_B1) * g
        nv = ADAM_B2 * v_ref[...] + (1.0 - ADAM_B2) * (g * g)
        nm_ref[...] = nm
        nv_ref[...] = nv
        m_hat = nm / (1.0 - ADAM_B1 ** ADAM_STEP)
        v_hat = nv / (1.0 - ADAM_B2 ** ADAM_STEP)
        dw_ref[...] = -ADAM_LR * (m_hat / (jnp.sqrt(v_hat) + ADAM_EPS) + ADAM_WD * w_ref[...])

    def part_rows(q):
        return lambda i: (0, jnp.clip(i - starts[q], 0, counts[q] - 1), 0)

    rows = pl.BlockSpec((tile, d), lambda i: (i, 0))
    out = jax.ShapeDtypeStruct((r, d), F32)
    return pl.pallas_call(
        body, name=name, grid=(r // tile,),
        in_specs=[pl.BlockSpec((npart, tile, dp), part_rows(q)) for q in range(n_lists)] + [rows, rows, rows],
        out_specs=[rows] * 4, out_shape=[out] * 4,
        compiler_params=_cparams(("arbitrary",)),
    )(*parts_list, w, m, v)


def _pad_rows(a, rows):
    return jnp.pad(a, ((0, rows - a.shape[0]), (0, 0)))


def _pack_rest(w_att, w_sg, w_ssm, w_out):
    parts = []
    for l in range(2):
        parts += [w_att[l], w_sg[l], w_ssm[l], w_out[l]]
    return jnp.concatenate(parts, axis=0)


def _unpack_rest(p):
    outs = [[], [], [], []]
    o = 0
    for l in range(2):
        for q, rws in enumerate(REST_PARTS):
            outs[q].append(p[o:o + rws])
            o += rws
    return [jnp.stack(t) for t in outs]


def _pack_win(w_in):
    return jnp.pad(w_in.reshape(2 * D_MODEL, WIN_SHARD), ((0, 0), (0, WIN_LANES - WIN_SHARD)))


W_IN_MAP = ((0, 1024, "att", 0), (1024, 1280, "att", 2048), (1280, 2304, "att", 1024), (2304, 5376, "sg", 0),
            (5376, 7424, "ssm", 3072), (7424, 10496, "ssm", 0), (10496, 10528, "ssm", 5120), (10528, 13600, "gate", 0))
SLAB_COLS = {"att": ATT_COLS, "sg": SG_COLS, "ssm": SSM_COLS, "gate": GATE_COLS}


def _slabs_from_shards(g):
    slabs = {}
    for name, width in SLAB_COLS.items():
        pieces, filled = [], 0
        for ga, gb, _, off in sorted((m for m in W_IN_MAP if m[2] == name), key=lambda m: m[3]):
            assert off == filled
            a = ga
            while a < gb:
                d = a // WIN_SHARD
                hi = min(gb, WIN_SHARD * (d + 1))
                pieces.append(g[d, :, a - WIN_SHARD * d:hi - WIN_SHARD * d])
                a = hi
            filled += gb - ga
        if filled < width:
            pieces.append(jnp.zeros((D_MODEL, width - filled), g.dtype))
        slabs[name] = jnp.concatenate(pieces, axis=1)
    return slabs


def _shards_from_slabs(dslabs):
    out = []
    for d in range(N_DEV):
        a, b = WIN_SHARD * d, WIN_SHARD * (d + 1)
        pieces = []
        for ga, gb, name, off in W_IN_MAP:
            lo, hi = max(a, ga), min(b, gb)
            if lo < hi:
                pieces.append(dslabs[name][:, off + lo - ga:off + hi - ga])
        pieces.append(jnp.zeros((D_MODEL, WIN_LANES - WIN_SHARD), pieces[0].dtype))
        out.append(jnp.concatenate(pieces, axis=1).astype(WIRE_DTYPE))
    return jnp.stack(out)


SMALL_SIZES = (("norm_pre", 2048), ("norm_post", 2048), ("rel_bias", 512), ("att_sinks", 32), ("sg_ln_g", 2048),
               ("sg_ln_b", 2048), ("sg_w", 262144), ("sg_b", 2048), ("ssm_conv_b", 6144), ("ssm_dt_bias", 64),
               ("ssm_a_log", 64), ("ssm_d", 64), ("ssm_norm_g", 4096), ("conv_w_full", 24576))


def _pack_small(d):
    parts = []
    for name, size in SMALL_SIZES:
        rows = 8 * (-(-size // (8 * D_MODEL)))
        flat = d[name].reshape(-1) if name in d else jnp.zeros((size,), F32)
        parts.append(jnp.pad(flat, (0, rows * D_MODEL - size)).reshape(rows, D_MODEL))
    return _pad_rows(jnp.concatenate(parts, axis=0), SMALL_ROWS)


def _unpack_small(p, shapes):
    out, o = {}, 0
    for name, size in SMALL_SIZES:
        rows = 8 * (-(-size // (8 * D_MODEL)))
        if name in shapes:
            out[name] = p[o:o + rows].reshape(-1)[:size].reshape(shapes[name])
        o += rows
    return out


def _bucket_onehot_t():
    qi = jnp.arange(BLK, dtype=jnp.int32)[None, :]
    kj = jnp.arange(BLK, dtype=jnp.int32)[:, None]
    dd = (qi - kj) & (BLK - 1)
    in_window = dd >= 0
    max_exact = REL_BUCKETS // 2
    dist_f = jnp.maximum(dd, 1).astype(F32)
    large = max_exact + (jnp.log(dist_f / max_exact) / math.log(128 / max_exact)
                         * (REL_BUCKETS - max_exact)).astype(jnp.int32)
    large = jnp.minimum(large, REL_BUCKETS - 1)
    bucket = jnp.where(dd < max_exact, dd, large).reshape(1, -1)
    onehot_t = (bucket == jnp.arange(REL_BUCKETS, dtype=jnp.int32)[:, None]).astype(F32)
    maskadd = jnp.where(in_window, 0.0, NEG).astype(F32).reshape(1, -1)
    return onehot_t, maskadd


WEIGHTS = ['w_in', 'norm_pre', 'norm_post', 'rel_bias', 'att_sinks', 'sg_ln_g', 'sg_ln_b', 'sg_w', 'sg_b',
           'ssm_conv_w', 'ssm_conv_b', 'ssm_dt_bias', 'ssm_a_log', 'ssm_d', 'ssm_norm_g',
           'w_br_att', 'w_br_sg', 'w_br_ssm', 'w_out']
REST = ('w_br_att', 'w_br_sg', 'w_br_ssm', 'w_out')


def kernel(x, w_in, norm_pre, norm_post, rel_bias, att_sinks, sg_ln_g, sg_ln_b, sg_w, sg_b, ssm_conv_w, ssm_conv_b, ssm_dt_bias, ssm_a_log, ssm_d, ssm_norm_g, w_br_att, w_br_sg, w_br_ssm, w_out, loss_target, m_w_in, m_norm_pre, m_norm_post, m_rel_bias, m_att_sinks, m_sg_ln_g, m_sg_ln_b, m_sg_w, m_sg_b, m_ssm_conv_w, m_ssm_conv_b, m_ssm_dt_bias, m_ssm_a_log, m_ssm_d, m_ssm_norm_g, m_w_br_att, m_w_br_sg, m_w_br_ssm, m_w_out, v_w_in, v_norm_pre, v_norm_post, v_rel_bias, v_att_sinks, v_sg_ln_g, v_sg_ln_b, v_sg_w, v_sg_b, v_ssm_conv_w, v_ssm_conv_b, v_ssm_dt_bias, v_ssm_a_log, v_ssm_d, v_ssm_norm_g, v_w_br_att, v_w_br_sg, v_w_br_ssm, v_w_out):
    w = dict(w_in=w_in, norm_pre=norm_pre, norm_post=norm_post, rel_bias=rel_bias, att_sinks=att_sinks,
             sg_ln_g=sg_ln_g, sg_ln_b=sg_ln_b, sg_w=sg_w, sg_b=sg_b, ssm_conv_w=ssm_conv_w, ssm_conv_b=ssm_conv_b,
             ssm_dt_bias=ssm_dt_bias, ssm_a_log=ssm_a_log, ssm_d=ssm_d, ssm_norm_g=ssm_norm_g,
             w_br_att=w_br_att, w_br_sg=w_br_sg, w_br_ssm=w_br_ssm, w_out=w_out)
    mom = dict(w_in=m_w_in, norm_pre=m_norm_pre, norm_post=m_norm_post, rel_bias=m_rel_bias, att_sinks=m_att_sinks,
               sg_ln_g=m_sg_ln_g, sg_ln_b=m_sg_ln_b, sg_w=m_sg_w, sg_b=m_sg_b, ssm_conv_w=m_ssm_conv_w,
               ssm_conv_b=m_ssm_conv_b, ssm_dt_bias=m_ssm_dt_bias, ssm_a_log=m_ssm_a_log, ssm_d=m_ssm_d,
               ssm_norm_g=m_ssm_norm_g, w_br_att=m_w_br_att, w_br_sg=m_w_br_sg, w_br_ssm=m_w_br_ssm, w_out=m_w_out)
    var = dict(w_in=v_w_in, norm_pre=v_norm_pre, norm_post=v_norm_post, rel_bias=v_rel_bias, att_sinks=v_att_sinks,
               sg_ln_g=v_sg_ln_g, sg_ln_b=v_sg_ln_b, sg_w=v_sg_w, sg_b=v_sg_b, ssm_conv_w=v_ssm_conv_w,
               ssm_conv_b=v_ssm_conv_b, ssm_dt_bias=v_ssm_dt_bias, ssm_a_log=v_ssm_a_log, ssm_d=v_ssm_d,
               ssm_norm_g=v_ssm_norm_g, w_br_att=v_w_br_att, w_br_sg=v_w_br_sg, w_br_ssm=v_w_br_ssm, w_out=v_w_out)
    xs0 = x[0]
    target = loss_target[0]
    my_dev = 4 * lax.axis_index("x") + 2 * lax.axis_index("y") + lax.axis_index("c")

    conv_shard = _pad_rows(ssm_conv_w.reshape(-1, D_MODEL), 8)
    win_shard = _pack_win(w_in).astype(WIRE_DTYPE)
    rest_shard = _pack_rest(*[w[n] for n in REST]).astype(WIRE_DTYPE)
    layer_shards = [[win_shard[l * D_MODEL:(l + 1) * D_MODEL], rest_shard[l * LAYER_REST:(l + 1) * LAYER_REST]]
                    for l in range(2)]
    g_win0, g_rest0, gathered_conv = exchange([], layer_shards[0] + [conv_shard], "all_gather")
    conv_full = gathered_conv[:, 0:3].reshape(N_DEV, 2, 4, 384).transpose(1, 2, 0, 3).reshape(2, 4, CONV_DIM)

    def layer_weights(l, g_win, g_rest):
        slabs = _slabs_from_shards(g_win)
        lw = {"in_" + name: slab.astype(MXU_DTYPE) for name, slab in slabs.items()}
        lw["in_att"] = lw["in_att"].T
        o = 0
        for name, rws in zip(("att", "sg", "ssm", "out"), REST_PARTS):
            lw[name] = g_rest[:, o:o + rws].reshape(N_DEV * rws, D_MODEL).astype(MXU_DTYPE)
            o += rws
        tril = jnp.tril(jnp.ones((BLK, BLK), bool))
        sgw = jnp.where(tril[None], sg_w[l], 0.0)
        lw.update(
            g_pre=norm_pre[l][None], g_post=norm_post[l][None], sinks=jnp.repeat(att_sinks[l], BLK).reshape(2, GROUP_LANES),
            ln_g=sg_ln_g[l][None], ln_b=sg_ln_b[l][None], sgw=sgw.astype(MXU_DTYPE),
            sgw_t=sgw.transpose(0, 2, 1).astype(MXU_DTYPE), sgb_t=_pad_lanes(sg_b[l].T),
            cw=conv_full[l], cb=ssm_conv_b[l][None], dtb=_pad_lanes(ssm_dt_bias[l][None]),
            alog=_pad_lanes(ssm_a_log[l][None]), dsk=_pad_lanes(ssm_d[l][None]), ng=ssm_norm_g[l][None])
        return lw

    onehot_t, maskadd = _bucket_onehot_t()
    bias = bias_table(rel_bias.T, onehot_t, maskadd).reshape(2, GROUP_HEADS, BLK, BLK).transpose(0, 2, 1, 3)
    bias = bias.reshape(2, BLK, GROUP_LANES)

    saved = []
    xl = xs0
    layers = [layer_weights(0, g_win0, g_rest0)]
    for l in range(2):
        lw = layers[l]
        h = rmsnorm_fwd(xl, lw["g_pre"])
        pa = mm_nt(lw["in_att"], h, 1152, "proj_att")
        ps = mm_nn(h, lw["in_sg"], 1536, "proj_sg")
        pm = mm_nn(h, lw["in_ssm"], 1792, "proj_ssm")
        pg = mm_nn(h, lw["in_gate"], 1536, "proj_gate")
        ya = attn_fwd(pa, bias, lw["sinks"])
        ys = sgu_fwd(ps, lw["ln_g"], lw["ln_b"], lw["sgw"], lw["sgb_t"])
        ssd_args = (pm, lw["cw"], lw["cb"], lw["dtb"], lw["alog"], lw["dsk"], lw["ng"])
        if l == 0:
            ym, states, (g_win1, g_rest1) = ssd_fwd(*ssd_args, Exchange([], layer_shards[1]))
            layers.append(layer_weights(1, g_win1, g_rest1))
        else:
            ym, states = ssd_fwd(*ssd_args)
        x_next, ba, bs, bm, merged, out_s = merge_fwd(xl, ya, ys, ym, pg, lw["att"], lw["sg"], lw["ssm"], lw["out"],
                                                      lw["g_post"])
        saved.append(dict(x=xl, h=h, pa=pa, ps=ps, pm=pm, pg=pg, ya=ya, ys=ys, ym=ym, states=states, ba=ba, bs=bs,
                          bm=bm, merged=merged, out_s=out_s))
        xl = x_next

    dx, loss_part = loss_grad(xl, target)
    loss = lax.psum(loss_part[0, 0], ("x", "y", "c"))

    dbias = jnp.zeros((2, BLK, GROUP_LANES), F32)
    win_grads, rest_grads = [None, None], [None, None]
    small = {n: [None, None] for n in ("norm_pre", "norm_post", "att_sinks", "sg_ln_g", "sg_ln_b", "sg_w", "sg_b",
                                       "ssm_conv_b", "ssm_dt_bias", "ssm_a_log", "ssm_d", "ssm_norm_g",
                                       "conv_w_full")}
    for l in (1, 0):
        lw, sv = layers[l], saved[l]
        dout, dba, dbs, dbm, dpg, dya, dys, dym, dg_post = merge_bwd(
            dx, sv["out_s"], sv["pg"], sv["ba"], sv["bs"], sv["bm"], lw["att"], lw["sg"], lw["ssm"], lw["out"],
            lw["g_post"])
        dw_out = mm_tn(sv["merged"], dout, 1024, "dw_out")
        dw_att = mm_kn(sv["ya"], dba, 1024, "dw_br_att")
        dw_sg = mm_tn(sv["ys"], dbs, 1024, "dw_br_sg")
        dw_ssm = mm_tn(sv["ym"], dbm, 1024, "dw_br_ssm")
        rest_grads[l] = jnp.concatenate(
            [dw_att.reshape(N_DEV, 128, D_MODEL), dw_sg.reshape(N_DEV, 128, D_MODEL),
             dw_ssm.reshape(N_DEV, 256, D_MODEL), dw_out.reshape(N_DEV, 128, D_MODEL)], axis=1).astype(WIRE_DTYPE)
        dpa, dbias, dsinks = attn_bwd(sv["pa"], dya, bias, lw["sinks"], dbias)
        dps, dsgw, dsgb_t, dln_g, dln_b = sgu_bwd(sv["ps"], dys, lw["ln_g"], lw["ln_b"], lw["sgw"], lw["sgw_t"],
                                                  lw["sgb_t"])
        ssd_args = (sv["pm"], dym, sv["states"], lw["cw"], lw["cb"], lw["dtb"], lw["alog"], lw["dsk"], lw["ng"])
        if l == 0:
            dpm, dcw, dcb, dvec, dng, (recv_win1, recv_rest1, recv_rest0) = ssd_bwd(
                *ssd_args, Exchange([win_grads[1], rest_grads[1], rest_grads[0]], []))
        else:
            dpm, dcw, dcb, dvec, dng = ssd_bwd(*ssd_args)
        dslabs = dict(att=mm_kn(dpa, sv["h"], 1152, "dw_in_att").T, sg=mm_tn(sv["h"], dps, 3072, "dw_in_sg"),
                      ssm=mm_tn(sv["h"], dpm, 2688, "dw_in_ssm"), gate=mm_tn(sv["h"], dpg, 3072, "dw_in_gate"))
        win_grads[l] = _shards_from_slabs(dslabs)
        dh_args = ([dpa, dps, dpm, dpg], [lw["in_att"], lw["in_sg"], lw["in_ssm"], lw["in_gate"]], sv["x"],
                   lw["g_pre"], dx)
        if l == 0:
            dx, dg_pre, (recv_win0,) = dh_norm_bwd(*dh_args, Exchange([win_grads[0]], []))
        else:
            dx, dg_pre = dh_norm_bwd(*dh_args)
        small["norm_pre"][l] = dg_pre[0]
        small["norm_post"][l] = dg_post[0]
        small["att_sinks"][l] = dsinks[0, :ATT_HEADS]
        small["sg_ln_g"][l] = dln_g[0]
        small["sg_ln_b"][l] = dln_b[0]
        small["sg_w"][l] = dsgw
        small["sg_b"][l] = dsgb_t[:, :SG_GROUPS].T
        small["ssm_conv_b"][l] = dcb[0]
        small["ssm_dt_bias"][l] = dvec[0, :SSM_HEADS]
        small["ssm_a_log"][l] = dvec[1, :SSM_HEADS]
        small["ssm_d"][l] = dvec[2, :SSM_HEADS]
        small["ssm_norm_g"][l] = dng[0]
        small["conv_w_full"][l] = dcw[0:4]
    grad_x = dx
    dbias = dbias.reshape(2, BLK, GROUP_HEADS, BLK).transpose(0, 2, 1, 3).reshape(ATT_HEADS, BLK * BLK)
    d_rel_bias = bias_table_bwd(dbias, onehot_t).T

    small_d = {n: jnp.stack(v) for n, v in small.items()}
    small_d["rel_bias"] = d_rel_bias
    (recv_small,) = exchange([], [_pack_small(small_d)], "gather_small_grads")

    flat_win = lambda a: a.reshape(2 * D_MODEL, WIN_SHARD)
    res_win = [a.reshape(2, D_MODEL, WIN_SHARD) for a in
               adamw([recv_win0, recv_win1], flat_win(w_in), flat_win(m_w_in), flat_win(v_w_in), WIN_TILE,
                     "adamw_w_in")]
    res_rest = adamw([recv_rest0, recv_rest1], _pack_rest(*[w[n] for n in REST]), _pack_rest(*[mom[n] for n in REST]),
                     _pack_rest(*[var[n] for n in REST]), REST_TILE, "adamw_rest")
    small_names = [n for n, _ in SMALL_SIZES if n != "conv_w_full"]
    g_s, dw_s, nm_s, nv_s = adamw([recv_small], _pack_small({n: w[n] for n in small_names}),
                                  _pack_small({n: mom[n] for n in small_names}),
                                  _pack_small({n: var[n] for n in small_names}), SMALL_TILE, "adamw_small")
    shapes = {n: w[n].shape for n in small_names}
    shapes["conv_w_full"] = (2, 4, CONV_DIM)
    g_conv_full = _unpack_small(g_s, shapes)["conv_w_full"]
    g_conv = lax.dynamic_slice_in_dim(g_conv_full, my_dev * 384, 384, axis=2)
    pack_conv = lambda a: _pad_rows(a.reshape(-1, D_MODEL), 8)
    g_c, dw_c, nm_c, nv_c = adamw([pack_conv(g_conv)[None]], pack_conv(ssm_conv_w), pack_conv(m_ssm_conv_w),
                                  pack_conv(v_ssm_conv_w), 8, "adamw_conv")

    results = {}
    for q, (tag, psm, pc) in enumerate((("grad", g_s, g_c), ("delta", dw_s, dw_c), ("new_m", nm_s, nm_c),
                                        ("new_v", nv_s, nv_c))):
        r = dict(zip(REST, _unpack_rest(res_rest[q])))
        r["w_in"] = res_win[q]
        r.update(_unpack_small(psm, {n: w[n].shape for n in small_names}))
        r["ssm_conv_w"] = pc[0:3].reshape(2, 4, 384)
        results[tag] = r
    outs = [loss, grad_x[None]]
    for tag in ("grad", "delta", "new_m", "new_v"):
        outs += [results[tag][n] for n in WEIGHTS]
    return tuple(outs)
```

```python
import math

import jax
import jax.numpy as jnp
from jax import lax
from jax.experimental import pallas as pl
from jax.experimental.pallas import tpu as pltpu

F32 = jnp.float32
MXU_DTYPE = jnp.bfloat16
ACT_DTYPE = jnp.bfloat16
WIRE_DTYPE = jnp.bfloat16
HI = lax.Precision.HIGHEST
MESH = pl.DeviceIdType.MESH

D_MODEL = 1024
N_DEV = 8
ATT_HEADS = 16
HEAD_DIM = 64
BLK = 128
SG_GROUPS = 8
SSM_WIDTH = 2048
SSM_HEADS = 32
SSM_GROUPS = 4
SSM_GW = SSM_WIDTH // SSM_GROUPS
CONV_DIM = 3072
REL_BUCKETS = 32
EPS = 1e-6
NEG = -1e30

ATT_COLS = 2304
SG_COLS = 3072
SSM_COLS = 5376
GATE_COLS = 3072
DT_OFF = 5120

VMEM_LIMIT_V7X = 56 * 2 ** 20

ADAM_LR, ADAM_B1, ADAM_B2, ADAM_EPS, ADAM_WD, ADAM_STEP = 0.001, 0.9, 0.999, 1e-08, 0.01, 10

WIN_SHARD = 1700
WIN_LANES = 1792
REST_PARTS = (128, 128, 256, 128)
LAYER_REST = sum(REST_PARTS)
REST_TILE = 128
WIN_TILE = 128
SMALL_ROWS = 384
SMALL_TILE = 128


def _cparams(sem=None):
    return pltpu.CompilerParams(dimension_semantics=sem, vmem_limit_bytes=VMEM_LIMIT_V7X)


def _dot(a, b):
    return jnp.dot(a.astype(MXU_DTYPE), b.astype(MXU_DTYPE), preferred_element_type=F32)


def _dot_nt(a, b):
    return lax.dot_general(a.astype(MXU_DTYPE), b.astype(MXU_DTYPE), (((1,), (1,)), ((), ())),
                           preferred_element_type=F32)


def _dot_tn(a, b):
    return lax.dot_general(a.astype(MXU_DTYPE), b.astype(MXU_DTYPE), (((0,), (0,)), ((), ())),
                           preferred_element_type=F32)


def _dot_hi(a, b):
    return jnp.dot(a, b, precision=HI, preferred_element_type=F32)


def _dot_onehot(a, onehot):
    hi = a.astype(jnp.bfloat16)
    lo = (a - hi.astype(F32)).astype(jnp.bfloat16)
    return (jnp.dot(hi, onehot, preferred_element_type=F32) + jnp.dot(lo, onehot, preferred_element_type=F32))


def _dot_hi_nt(a, b):
    return lax.dot_general(a, b, (((1,), (1,)), ((), ())), precision=HI, preferred_element_type=F32)


def _sig(x):
    return 1.0 / (1.0 + jnp.exp(-x))


def _dsilu(x, s):
    return s * (1.0 + x * (1.0 - s))


def _full(shape):
    nd = len(shape)
    return pl.BlockSpec(shape, lambda *_: (0,) * nd)


def rmsnorm_fwd(x, g):
    s, d = x.shape
    tm = min(512, s)

    def body(x_ref, g_ref, o_ref):
        xv = x_ref[...]
        r = lax.rsqrt(jnp.mean(xv * xv, axis=-1, keepdims=True) + EPS)
        o_ref[...] = (xv * r * g_ref[...]).astype(o_ref.dtype)

    return pl.pallas_call(
        body, name="rmsnorm_fwd", grid=(s // tm,),
        in_specs=[pl.BlockSpec((tm, d), lambda i: (i, 0)), _full((1, d))],
        out_specs=pl.BlockSpec((tm, d), lambda i: (i, 0)),
        out_shape=jax.ShapeDtypeStruct((s, d), ACT_DTYPE),
        compiler_params=_cparams(("parallel",)),
    )(x, g)


def mm_nn(a, b, tn, name):
    s, k = a.shape
    n = b.shape[1]
    tm = min(2048, s)

    def body(a_ref, b_ref, o_ref):
        o_ref[...] = _dot(a_ref[...], b_ref[...]).astype(o_ref.dtype)

    return pl.pallas_call(
        body, name=name, grid=(s // tm, n // tn),
        in_specs=[pl.BlockSpec((tm, k), lambda i, j: (i, 0)), pl.BlockSpec((k, tn), lambda i, j: (0, j))],
        out_specs=pl.BlockSpec((tm, tn), lambda i, j: (i, j)),
        out_shape=jax.ShapeDtypeStruct((s, n), ACT_DTYPE),
        compiler_params=_cparams(("parallel", "arbitrary")),
    )(a, b)


def mm_nt(a, b, tm, name):
    m, k = a.shape
    s = b.shape[0]
    ts = min(2048, s)

    def body(a_ref, b_ref, o_ref):
        o_ref[...] = _dot_nt(a_ref[...], b_ref[...]).astype(o_ref.dtype)

    return pl.pallas_call(
        body, name=name, grid=(s // ts, m // tm),
        in_specs=[pl.BlockSpec((tm, k), lambda i, j: (j, 0)), pl.BlockSpec((ts, k), lambda i, j: (i, 0))],
        out_specs=pl.BlockSpec((tm, ts), lambda i, j: (j, i)),
        out_shape=jax.ShapeDtypeStruct((m, s), ACT_DTYPE),
        compiler_params=_cparams(("parallel", "arbitrary")),
    )(a, b)


def mm_kn(a, b, tm, name):
    m, s = a.shape
    n = b.shape[1]
    ts = min(512, s)

    def body(a_ref, b_ref, o_ref):
        @pl.when(pl.program_id(1) == 0)
        def _():
            o_ref[...] = jnp.zeros_like(o_ref)

        o_ref[...] += _dot(a_ref[...], b_ref[...])

    return pl.pallas_call(
        body, name=name, grid=(m // tm, s // ts),
        in_specs=[pl.BlockSpec((tm, ts), lambda j, t: (j, t)), pl.BlockSpec((ts, n), lambda j, t: (t, 0))],
        out_specs=pl.BlockSpec((tm, n), lambda j, t: (j, 0)),
        out_shape=jax.ShapeDtypeStruct((m, n), F32),
        compiler_params=_cparams(("parallel", "arbitrary")),
    )(a, b)


def mm_tn(a, b, tn, name):
    s, k = a.shape
    n = b.shape[1]
    ts = min(512, s)

    def body(a_ref, b_ref, o_ref):
        @pl.when(pl.program_id(1) == 0)
        def _():
            o_ref[...] = jnp.zeros_like(o_ref)

        o_ref[...] += _dot_tn(a_ref[...], b_ref[...])

    return pl.pallas_call(
        body, name=name, grid=(n // tn, s // ts),
        in_specs=[pl.BlockSpec((ts, k), lambda j, t: (t, 0)), pl.BlockSpec((ts, tn), lambda j, t: (t, j))],
        out_specs=pl.BlockSpec((k, tn), lambda j, t: (0, j)),
        out_shape=jax.ShapeDtypeStruct((k, n), F32),
        compiler_params=_cparams(("parallel", "arbitrary")),
    )(a, b)


def dh_norm_bwd(dslabs, wslabs, x, g, dres, ex=None):
    s, d = x.shape
    tm = min(1024, s)
    tk = 768
    counts = [ds.shape[0 if q == 0 else 1] // tk for q, ds in enumerate(dslabs)]
    starts = [sum(counts[:i]) for i in range(len(counts))]
    nk = sum(counts)
    ns = len(dslabs)

    hosted = ex is not None
    ni = s // tm

    def mm_body(*refs):
        (own_in, (dh_ref,), _), hosted_refs = _split_hosted(refs, 2 * ns, 1, 0, ex)
        d_refs, w_refs = own_in[:ns], own_in[ns:]
        i, k = pl.program_id(0), pl.program_id(1)
        if hosted:
            @pl.when((i == 0) & (k == 0))
            def _():
                ex.start(*hosted_refs)

            @pl.when((i == ni - 1) & (k == nk - 1))
            def _():
                ex.relay(*hosted_refs)
                ex.wait(*hosted_refs)

        @pl.when(k == 0)
        def _():
            dh_ref[...] = jnp.zeros_like(dh_ref)

        for q in range(ns):
            @pl.when((k >= starts[q]) & (k < starts[q] + counts[q]))
            def _(q=q):
                if q == 0:
                    dh_ref[...] += _dot_tn(d_refs[q][...], w_refs[q][...])
                else:
                    dh_ref[...] += _dot_nt(d_refs[q][...], w_refs[q][...])

    def clamp(q):
        if q == 0:
            return pl.BlockSpec((tk, tm), lambda i, k: (jnp.clip(k - starts[q], 0, counts[q] - 1), i))
        return pl.BlockSpec((tm, tk), lambda i, k: (i, jnp.clip(k - starts[q], 0, counts[q] - 1)))

    def clamp_w(q):
        if q == 0:
            return pl.BlockSpec((tk, d), lambda i, k: (jnp.clip(k - starts[q], 0, counts[q] - 1), 0))
        return pl.BlockSpec((d, tk), lambda i, k: (0, jnp.clip(k - starts[q], 0, counts[q] - 1)))

    res = pl.pallas_call(
        mm_body, name="dh_matmul_scatter" if hosted else "dh_matmul", grid=(ni, nk),
        in_specs=([clamp(q) for q in range(ns)] + [clamp_w(q) for q in range(ns)]
                  + (ex.in_specs if hosted else [])),
        out_specs=[pl.BlockSpec((tm, d), lambda i, k: (i, 0))] + (ex.out_specs if hosted else []),
        out_shape=[jax.ShapeDtypeStruct((s, d), F32)] + (ex.out_shape if hosted else []),
        scratch_shapes=ex.scratch if hosted else [],
        compiler_params=_cparams(("arbitrary" if hosted else "parallel", "arbitrary")),
    )(*dslabs, *wslabs, *(ex.arrays if hosted else []))
    dh, ex_results = res[0], res[1:]

    te = min(512, s)

    def norm_body(dh_ref, x_ref, g_ref, dres_ref, dx_ref, dg_ref):
        @pl.when(pl.program_id(0) == 0)
        def _():
            dg_ref[...] = jnp.zeros_like(dg_ref)

        xv = x_ref[...]
        r = lax.rsqrt(jnp.mean(xv * xv, axis=-1, keepdims=True) + EPS)
        xn = xv * r
        dhv = dh_ref[...]
        dg_ref[...] += jnp.sum(dhv * xn, axis=0, keepdims=True)
        dxn = dhv * g_ref[...]
        dx_ref[...] = dres_ref[...] + r * (dxn - xn * jnp.mean(dxn * xn, axis=-1, keepdims=True))

    rows = pl.BlockSpec((te, d), lambda i: (i, 0))
    dx, dg = pl.pallas_call(
        norm_body, name="norm_bwd", grid=(s // te,),
        in_specs=[rows, rows, _full((1, d)), rows],
        out_specs=[rows, _full((1, d))],
        out_shape=[jax.ShapeDtypeStruct((s, d), F32), jax.ShapeDtypeStruct((1, d), F32)],
        compiler_params=_cparams(("arbitrary",)),
    )(dh, x, g, dres)
    return (dx, dg, ex_results) if hosted else (dx, dg)


def bias_table(rel_bias_t, onehot_t, maskadd):
    n = onehot_t.shape[1]
    tn = 8192

    def body(r_ref, o_ref, m_ref, out_ref):
        out_ref[...] = _dot_hi(r_ref[...], o_ref[...]) + m_ref[...]

    return pl.pallas_call(
        body, name="bias_table", grid=(n // tn,),
        in_specs=[_full((ATT_HEADS, REL_BUCKETS)), pl.BlockSpec((REL_BUCKETS, tn), lambda j: (0, j)),
                  pl.BlockSpec((1, tn), lambda j: (0, j))],
        out_specs=pl.BlockSpec((ATT_HEADS, tn), lambda j: (0, j)),
        out_shape=jax.ShapeDtypeStruct((ATT_HEADS, n), F32),
        compiler_params=_cparams(("parallel",)),
    )(rel_bias_t, onehot_t, maskadd)


def bias_table_bwd(dbias, onehot_t):
    n = onehot_t.shape[1]
    tn = 8192

    def body(d_ref, o_ref, out_ref):
        @pl.when(pl.program_id(0) == 0)
        def _():
            out_ref[...] = jnp.zeros_like(out_ref)

        out_ref[...] += _dot_hi_nt(d_ref[...], o_ref[...])

    return pl.pallas_call(
        body, name="bias_table_bwd", grid=(n // tn,),
        in_specs=[pl.BlockSpec((ATT_HEADS, tn), lambda j: (0, j)), pl.BlockSpec((REL_BUCKETS, tn), lambda j: (0, j))],
        out_specs=_full((ATT_HEADS, REL_BUCKETS)),
        out_shape=jax.ShapeDtypeStruct((ATT_HEADS, REL_BUCKETS), F32),
        compiler_params=_cparams(("arbitrary",)),
    )(dbias, onehot_t)


def _fold(full, tri):
    return jnp.where(tri, full[BLK:2 * BLK], full[0:BLK])


def _unfold(folded, tri):
    return jnp.concatenate([jnp.where(tri, 0.0, folded), jnp.where(tri, folded, 0.0)], axis=0)


GROUP_HEADS = ATT_HEADS // 2
GROUP_LANES = GROUP_HEADS * BLK


def _att_group(qg, kcat, vt_cat, bias_g, sink_g, tri, no_prev):
    l = _fold(_dot(kcat, qg), tri) * (HEAD_DIM ** -0.5) + bias_g
    l = jnp.where(no_prev, NEG, l)
    m = jnp.maximum(jnp.max(l, axis=0, keepdims=True), sink_g)
    p = jnp.exp(l - m)
    es = jnp.exp(sink_g - m)
    inv = 1.0 / (jnp.sum(p, axis=0, keepdims=True) + es)
    p = p * inv
    pcat = _unfold(p, tri)
    return p, pcat, es * inv, _dot(vt_cat, pcat)


def _heads_to_lanes(ref, row0):
    return jnp.concatenate([ref[row0 + j * HEAD_DIM:row0 + (j + 1) * HEAD_DIM, :] for j in range(GROUP_HEADS)], axis=1)


def _lanes_to_heads(ref, row0, val):
    for j in range(GROUP_HEADS):
        ref[row0 + j * HEAD_DIM:row0 + (j + 1) * HEAD_DIM, :] = val[:, j * BLK:(j + 1) * BLK].astype(ref.dtype)


def _kv_cat(kvp, kvc, g):
    lo = g * HEAD_DIM
    kt_cat = jnp.concatenate([kvp[lo:lo + HEAD_DIM], kvc[lo:lo + HEAD_DIM]], axis=1)
    vt_cat = jnp.concatenate([kvp[128 + lo:128 + lo + HEAD_DIM], kvc[128 + lo:128 + lo + HEAD_DIM]], axis=1)
    return kt_cat, vt_cat


def _tri_masks(n):
    row = lax.broadcasted_iota(jnp.int32, (BLK, GROUP_LANES), 0)
    query = lax.broadcasted_iota(jnp.int32, (BLK, GROUP_LANES), 1) & (BLK - 1)
    tri = row <= query
    return tri, (n == 0) & jnp.logical_not(tri)


def _split_hosted(refs, n_in, n_out, n_scratch, ex):
    na = ex.na if ex is not None else 0
    o = 0
    parts = []
    for cnt in (n_in, na, n_out, na, n_scratch, 3 if ex is not None else 0):
        parts.append(refs[o:o + cnt])
        o += cnt
    own_in, ex_in, own_out, ex_out, own_scr, ex_sems = parts
    return (own_in, own_out, own_scr), (ex_in, ex_out, ex_sems)


def _call_hosting(body, name, nsteps, in_specs, out_specs, out_shape, scratch, args, ex):
    n_in, n_out, n_scr = len(in_specs), len(out_specs), len(scratch)
    hosted = ex is not None

    def full_body(*refs):
        (own_in, own_out, own_scr), hosted_refs = _split_hosted(refs, n_in, n_out, n_scr, ex)
        if hosted:
            @pl.when(pl.program_id(0) == 0)
            def _():
                ex.start(*hosted_refs)

            @pl.when(pl.program_id(0) == max(nsteps - 4, 0))
            def _():
                ex.relay(*hosted_refs)

            @pl.when(pl.program_id(0) == nsteps - 1)
            def _():
                ex.wait(*hosted_refs)

        body(*own_in, *own_out, *own_scr)

    res = pl.pallas_call(
        full_body, name=name + "_hosting" if hosted else name, grid=(nsteps,),
        in_specs=list(in_specs) + (ex.in_specs if hosted else []),
        out_specs=list(out_specs) + (ex.out_specs if hosted else []),
        out_shape=list(out_shape) + (ex.out_shape if hosted else []),
        scratch_shapes=list(scratch) + (ex.scratch if hosted else []),
        compiler_params=_cparams(("arbitrary",)),
    )(*args, *(ex.arrays if hosted else []))
    return res[:n_out], res[n_out:]


def attn_fwd(pa, bias, sinks, ex=None):
    s = pa.shape[1]
    nb = s // BLK

    def body(pa_ref, kvp_ref, bias_ref, sink_ref, y_ref):
        n = pl.program_id(0)
        kvc = pa_ref[2048:2304, :]
        kvp = kvp_ref[...]
        tri, no_prev = _tri_masks(n)
        for g in range(2):
            kt_cat, vt_cat = _kv_cat(kvp, kvc, g)
            row0 = g * GROUP_HEADS * HEAD_DIM
            _, _, _, o = _att_group(_heads_to_lanes(pa_ref, row0), kt_cat.astype(F32).T, vt_cat, bias_ref[g],
                                    sink_ref[g:g + 1, :], tri, no_prev)
            z = _heads_to_lanes(pa_ref, 1024 + row0).astype(F32)
            _lanes_to_heads(y_ref, row0, o * z * _sig(z))

    (y,), hosted = _call_hosting(
        body, "attn_fwd", nb,
        in_specs=[pl.BlockSpec((ATT_COLS, BLK), lambda n: (0, n)),
                  pl.BlockSpec((256, BLK), lambda n: (8, jnp.maximum(n - 1, 0))),
                  _full((2, BLK, GROUP_LANES)), _full((2, GROUP_LANES))],
        out_specs=[pl.BlockSpec((1024, BLK), lambda n: (0, n))],
        out_shape=[jax.ShapeDtypeStruct((1024, s), ACT_DTYPE)], scratch=[],
        args=(pa, pa, bias, sinks), ex=ex)
    return (y, hosted) if ex is not None else y


def attn_bwd(pa, dy, bias, sinks, dbias_in):
    s = pa.shape[1]
    nb = s // BLK

    def body(pa_ref, kvp_ref, dy_ref, bias_ref, sink_ref, dbin_ref, dpa_ref, dbias_ref, dsink_ref, carry, dsink_acc):
        i = pl.program_id(0)
        n = nb - 1 - i

        @pl.when(i == 0)
        def _():
            dbias_ref[...] = dbin_ref[...]
            dsink_acc[...] = jnp.zeros_like(dsink_acc)
            carry[...] = jnp.zeros_like(carry)

        kvc = pa_ref[2048:2304, :]
        kvp = kvp_ref[...]
        tri, no_prev = _tri_masks(n)
        scale = HEAD_DIM ** -0.5
        for g in range(2):
            kt_cat, vt_cat = _kv_cat(kvp, kvc, g)
            row0 = g * GROUP_HEADS * HEAD_DIM
            qg = _heads_to_lanes(pa_ref, row0)
            p, pcat, psink, o = _att_group(qg, kt_cat.astype(F32).T, vt_cat, bias_ref[g], sink_ref[g:g + 1, :], tri,
                                           no_prev)
            z = _heads_to_lanes(pa_ref, 1024 + row0).astype(F32)
            dyg = _heads_to_lanes(dy_ref, row0).astype(F32)
            sz = _sig(z)
            d_o = dyg * z * sz
            _lanes_to_heads(dpa_ref, 1024 + row0, dyg * _dsilu(z, sz) * o)
            delta = jnp.sum(d_o * o, axis=0, keepdims=True)
            dl = p * (_fold(_dot(vt_cat.astype(F32).T, d_o), tri) - delta)
            dsink_acc[g:g + 1, :] += psink * delta
            dbias_ref[g] += dl
            dlcat = _unfold(dl, tri)
            _lanes_to_heads(dpa_ref, row0, _dot(kt_cat, dlcat) * scale)
            for q, dkv in enumerate((_dot_nt(qg, dlcat) * scale, _dot_nt(d_o, pcat))):
                r0 = q * 128 + g * HEAD_DIM
                dpa_ref[2048 + r0:2048 + r0 + HEAD_DIM, :] = (
                    dkv[:, BLK:2 * BLK] + carry[r0:r0 + HEAD_DIM, :]).astype(dpa_ref.dtype)
                carry[r0:r0 + HEAD_DIM, :] = dkv[:, 0:BLK]

        @pl.when(i == nb - 1)
        def _():
            lane = lax.broadcasted_iota(jnp.int32, (1, 128), 1)
            dsink = jnp.zeros((1, 128), F32)
            for h in range(ATT_HEADS):
                g, j = divmod(h, GROUP_HEADS)
                tot = jnp.sum(dsink_acc[g:g + 1, j * BLK:(j + 1) * BLK], axis=1, keepdims=True)
                dsink = dsink + jnp.where(lane == h, -tot, 0.0)
            dsink_ref[...] = dsink

    return pl.pallas_call(
        body, name="attn_bwd", grid=(nb,),
        in_specs=[pl.BlockSpec((ATT_COLS, BLK), lambda i: (0, nb - 1 - i)),
                  pl.BlockSpec((256, BLK), lambda i: (8, jnp.maximum(nb - 2 - i, 0))),
                  pl.BlockSpec((1024, BLK), lambda i: (0, nb - 1 - i)),
                  _full((2, BLK, GROUP_LANES)), _full((2, GROUP_LANES)), _full((2, BLK, GROUP_LANES))],
        out_specs=[pl.BlockSpec((ATT_COLS, BLK), lambda i: (0, nb - 1 - i)),
                   _full((2, BLK, GROUP_LANES)), _full((1, 128))],
        out_shape=[jax.ShapeDtypeStruct((ATT_COLS, s), ACT_DTYPE),
                   jax.ShapeDtypeStruct((2, BLK, GROUP_LANES), F32),
                   jax.ShapeDtypeStruct((1, 128), F32)],
        scratch_shapes=[pltpu.VMEM((256, BLK), F32), pltpu.VMEM((2, GROUP_LANES), F32)],
        compiler_params=_cparams(("arbitrary",)),
    )(pa, pa, dy, bias, sinks, dbias_in)


def _layernorm(v, g, b):
    mu = jnp.mean(v, axis=-1, keepdims=True)
    vc = v - mu
    rstd = lax.rsqrt(jnp.mean(vc * vc, axis=-1, keepdims=True) + EPS)
    xhat = vc * rstd
    return xhat, rstd, xhat * g + b


def sgu_fwd(ps, ln_g, ln_b, w_tril, b_t):
    s = ps.shape[0]

    def body(ps_ref, g_ref, b_ref, w_ref, bt_ref, y_ref):
        u = ps_ref[:, 0:1024].astype(F32)
        v = ps_ref[:, 1024:2048].astype(F32)
        z = ps_ref[:, 2048:3072].astype(F32)
        _, _, vn = _layernorm(v, g_ref[...], b_ref[...])
        gate = u * z * _sig(z)
        for g in range(SG_GROUPS):
            sl = slice(g * 128, (g + 1) * 128)
            mixed = _dot(w_ref[g], vn[:, sl]) + bt_ref[:, g:g + 1]
            y_ref[:, sl] = (gate[:, sl] * mixed).astype(y_ref.dtype)

    return pl.pallas_call(
        body, name="sgu_fwd", grid=(s // BLK,),
        in_specs=[pl.BlockSpec((BLK, SG_COLS), lambda c: (c, 0)), _full((1, 1024)), _full((1, 1024)),
                  _full((SG_GROUPS, BLK, BLK)), _full((BLK, 128))],
        out_specs=pl.BlockSpec((BLK, 1024), lambda c: (c, 0)),
        out_shape=jax.ShapeDtypeStruct((s, 1024), ACT_DTYPE),
        compiler_params=_cparams(("parallel",)),
    )(ps, ln_g, ln_b, w_tril, b_t)


def sgu_bwd(ps, dy, ln_g, ln_b, w_tril, w_tril_t, b_t):
    s = ps.shape[0]

    def body(ps_ref, dy_ref, g_ref, b_ref, w_ref, wt_ref, bt_ref, dps_ref, dw_ref, dbt_ref, dg_ref, db_ref, dvn_scr):
        @pl.when(pl.program_id(0) == 0)
        def _():
            dw_ref[...] = jnp.zeros_like(dw_ref)
            dbt_ref[...] = jnp.zeros_like(dbt_ref)
            dg_ref[...] = jnp.zeros_like(dg_ref)
            db_ref[...] = jnp.zeros_like(db_ref)

        u = ps_ref[:, 0:1024].astype(F32)
        v = ps_ref[:, 1024:2048].astype(F32)
        z = ps_ref[:, 2048:3072].astype(F32)
        dy = dy_ref[...].astype(F32)
        xhat, rstd, vn = _layernorm(v, g_ref[...], b_ref[...])
        sz = _sig(z)
        silu = z * sz
        row = lax.broadcasted_iota(jnp.int32, (BLK, BLK), 0)
        colm = lax.broadcasted_iota(jnp.int32, (BLK, BLK), 1)
        tril = row >= colm
        dbt = jnp.zeros((BLK, 128), F32)
        for g in range(SG_GROUPS):
            sl = slice(g * 128, (g + 1) * 128)
            vng = vn[:, sl]
            mixed = _dot(w_ref[g], vng) + bt_ref[:, g:g + 1]
            dyg, ug = dy[:, sl], u[:, sl]
            dps_ref[:, sl] = (dyg * mixed * silu[:, sl]).astype(dps_ref.dtype)
            dps_ref[:, 2048 + g * 128:2048 + (g + 1) * 128] = (
                dyg * ug * mixed * _dsilu(z[:, sl], sz[:, sl])).astype(dps_ref.dtype)
            dm = dyg * ug * silu[:, sl]
            dw_ref[g] += jnp.where(tril, _dot_nt(dm, vng), 0.0)
            dbt = dbt + jnp.where(colm == g, jnp.sum(dm, axis=1, keepdims=True), 0.0)
            dvn_scr[:, sl] = _dot(wt_ref[g], dm)
        dbt_ref[...] += dbt
        dvn = dvn_scr[...]
        dg_ref[...] += jnp.sum(dvn * xhat, axis=0, keepdims=True)
        db_ref[...] += jnp.sum(dvn, axis=0, keepdims=True)
        dxh = dvn * g_ref[...]
        dv = rstd * (dxh - jnp.mean(dxh, axis=-1, keepdims=True)
                     - xhat * jnp.mean(dxh * xhat, axis=-1, keepdims=True))
        dps_ref[:, 1024:2048] = dv.astype(dps_ref.dtype)

    return pl.pallas_call(
        body, name="sgu_bwd", grid=(s // BLK,),
        in_specs=[pl.BlockSpec((BLK, SG_COLS), lambda c: (c, 0)), pl.BlockSpec((BLK, 1024), lambda c: (c, 0)),
                  _full((1, 1024)), _full((1, 1024)), _full((SG_GROUPS, BLK, BLK)), _full((SG_GROUPS, BLK, BLK)),
                  _full((BLK, 128))],
        out_specs=[pl.BlockSpec((BLK, SG_COLS), lambda c: (c, 0)), _full((SG_GROUPS, BLK, BLK)), _full((BLK, 128)),
                   _full((1, 1024)), _full((1, 1024))],
        out_shape=[jax.ShapeDtypeStruct((s, SG_COLS), ACT_DTYPE), jax.ShapeDtypeStruct((SG_GROUPS, BLK, BLK), F32),
                   jax.ShapeDtypeStruct((BLK, 128), F32), jax.ShapeDtypeStruct((1, 1024), F32),
                   jax.ShapeDtypeStruct((1, 1024), F32)],
        scratch_shapes=[pltpu.VMEM((BLK, 1024), F32)],
        compiler_params=_cparams(("arbitrary",)),
    )(ps, dy, ln_g, ln_b, w_tril, w_tril_t, b_t)


def _shift_down(cur, prev16, k):
    if k == 0:
        return cur
    r = pltpu.roll(cur, k, 0)
    rp = pltpu.roll(prev16, k, 0)
    row = lax.broadcasted_iota(jnp.int32, (8, cur.shape[1]), 0)
    return jnp.concatenate([jnp.where(row < k, rp[0:8], r[0:8]), r[8:]], axis=0)


def _shift_up(cur, next16, k):
    if k == 0:
        return cur
    n = cur.shape[0]
    r = pltpu.roll(cur, n - k, 0)
    rn = pltpu.roll(next16, 16 - k, 0)
    row = lax.broadcasted_iota(jnp.int32, (8, cur.shape[1]), 0)
    return jnp.concatenate([r[:n - 8], jnp.where(row >= 8 - k, rn[8:16], r[n - 8:])], axis=0)


def _bcast8(v):
    return jnp.broadcast_to(v, (16, v.shape[1]))


class _Ssd:
    def __init__(self, xbc, prev16, dtr, cw, cbias, dtb, alog, dsk, tri, e):
        pre = cbias + cw[3:4] * xbc
        self.shifted = [xbc]
        for k in (1, 2, 3):
            sh = _shift_down(xbc, prev16, k)
            self.shifted.append(sh)
            pre = pre + cw[3 - k:4 - k] * sh
        self.pre = pre
        self.sg = _sig(pre)
        act = pre * self.sg
        self.xs = act[:, 0:SSM_WIDTH]
        self.bm = act[:, SSM_WIDTH:SSM_WIDTH + 512]
        self.cm = act[:, SSM_WIDTH + 512:CONV_DIM]
        self.dtp = dtr + dtb
        self.dt = jnp.maximum(self.dtp, 0.0) + jnp.log(1.0 + jnp.exp(-jnp.abs(self.dtp)))
        self.a = -jnp.exp(alog)
        self.acs = _dot_hi(tri, self.dt * self.a)
        self.acs_t = self.acs.T
        tot = self.acs[BLK - 1:BLK]
        self.ecs = jnp.exp(self.acs)
        self.dte = jnp.exp(tot - self.acs)
        self.cd = jnp.exp(tot)
        self.dt_x = _dot_onehot(self.dt, e)
        self.ecs_x = _dot_onehot(self.ecs, e)
        self.dte_x = _dot_onehot(self.dte, e)
        self.cd_x = _dot_onehot(_bcast8(self.cd), e)[0:1]
        self.d_x = _dot_onehot(_bcast8(dsk), e)[0:1]
        self.xdt = self.xs * self.dt_x
        row = lax.broadcasted_iota(jnp.int32, (BLK, BLK), 0)
        col = lax.broadcasted_iota(jnp.int32, (BLK, BLK), 1)
        self.tril = row >= col

    def group(self, g):
        sl = slice(g * 128, (g + 1) * 128)
        bg, cg = self.bm[:, sl], self.cm[:, sl]
        return bg, cg, _dot_nt(cg, bg)

    def decay(self, h):
        seg = self.acs[:, h:h + 1] - self.acs_t[h:h + 1, :]
        return jnp.exp(jnp.where(self.tril, seg, NEG))

    def y_pre_gate(self, ht_of, yd_scr, yoff_scr):
        for g in range(SSM_GROUPS):
            bg, cg, cb = self.group(g)
            for j in range(8):
                h = g * 8 + j
                sl = slice(h * 64, (h + 1) * 64)
                yd_scr[:, sl] = _dot(cb * self.decay(h), self.xdt[:, sl])
            gs = slice(g * SSM_GW, (g + 1) * SSM_GW)
            yoff_scr[:, gs] = _dot(cg, ht_of(g)) * self.ecs_x[:, gs]
        return yd_scr[...] + yoff_scr[...] + self.d_x * self.xs


def _ssd_consts():
    hh = lax.broadcasted_iota(jnp.int32, (128, SSM_WIDTH), 0)
    ch = lax.broadcasted_iota(jnp.int32, (128, SSM_WIDTH), 1)
    e = (ch // 64 == hh).astype(jnp.bfloat16)
    row = lax.broadcasted_iota(jnp.int32, (BLK, BLK), 0)
    col = lax.broadcasted_iota(jnp.int32, (BLK, BLK), 1)
    tri = (row >= col).astype(F32)
    return tri, e


def _pad_lanes(v, n=128):
    return jnp.pad(v, ((0, 0), (0, n - v.shape[1])))


def ssd_fwd(pm, cw, cbias, dtb, alog, dsk, ng, ex=None):
    s = pm.shape[0]
    nc = s // BLK
    tri, e = _ssd_consts()

    def body(pm_ref, prev_ref, cw_ref, cb_ref, dtb_ref, al_ref, d_ref, ng_ref, tri_ref, e_ref,
             y_ref, st_ref, ht_ref, yd_scr, yoff_scr):
        c = pl.program_id(0)

        @pl.when(c == 0)
        def _():
            ht_ref[...] = jnp.zeros_like(ht_ref)

        xbc = pm_ref[:, 0:CONV_DIM].astype(F32)
        prev16 = jnp.where(c == 0, 0.0, prev_ref[...].astype(F32))
        f = _Ssd(xbc, prev16, pm_ref[:, DT_OFF:DT_OFF + 128].astype(F32), cw_ref[...], cb_ref[...], dtb_ref[...],
                 al_ref[...], d_ref[...], tri_ref[...], e_ref[...])
        st_ref[0] = ht_ref[...]
        y = f.y_pre_gate(lambda g: ht_ref[g], yd_scr, yoff_scr)
        for g in range(SSM_GROUPS):
            bg, _, _ = f.group(g)
            gs = slice(g * SSM_GW, (g + 1) * SSM_GW)
            ht_ref[g] = ht_ref[g] * f.cd_x[:, gs] + _dot_tn(bg, f.xdt[:, gs] * f.dte_x[:, gs])
        z = pm_ref[:, CONV_DIM:CONV_DIM + SSM_WIDTH].astype(F32)
        ypre = y * z * _sig(z)
        for g in range(SSM_GROUPS):
            gs = slice(g * SSM_GW, (g + 1) * SSM_GW)
            yg = ypre[:, gs]
            rr = lax.rsqrt(jnp.mean(yg * yg, axis=-1, keepdims=True) + EPS)
            y_ref[:, gs] = (yg * rr * ng_ref[:, gs]).astype(y_ref.dtype)

    own, hosted = _call_hosting(
        body, "ssd_fwd", nc,
        in_specs=[pl.BlockSpec((BLK, SSM_COLS), lambda c: (c, 0)),
                  pl.BlockSpec((16, CONV_DIM), lambda c: (jnp.maximum(8 * c - 1, 0), 0)),
                  _full((4, CONV_DIM)), _full((1, CONV_DIM)), _full((1, 128)), _full((1, 128)), _full((1, 128)),
                  _full((1, SSM_WIDTH)), _full((BLK, BLK)), _full((128, SSM_WIDTH))],
        out_specs=[pl.BlockSpec((BLK, SSM_WIDTH), lambda c: (c, 0)),
                   pl.BlockSpec((1, SSM_GROUPS, 128, SSM_GW), lambda c: (c, 0, 0, 0))],
        out_shape=[jax.ShapeDtypeStruct((s, SSM_WIDTH), ACT_DTYPE),
                   jax.ShapeDtypeStruct((nc, SSM_GROUPS, 128, SSM_GW), F32)],
        scratch=[pltpu.VMEM((SSM_GROUPS, 128, SSM_GW), F32), pltpu.VMEM((BLK, SSM_WIDTH), F32),
                 pltpu.VMEM((BLK, SSM_WIDTH), F32)],
        args=(pm, pm, cw, cbias, dtb, alog, dsk, ng, tri, e), ex=ex)
    return (*own, hosted) if ex is not None else tuple(own)


def ssd_bwd(pm, dy, states, cw, cbias, dtb, alog, dsk, ng, ex=None):
    s = pm.shape[0]
    nc = s // BLK
    tri, e = _ssd_consts()
    tri_t, e_t = tri.T, e.T

    def body(pm_ref, prev_ref, dy_ref, st_ref, cw_ref, cb_ref, dtb_ref, al_ref, d_ref, ng_ref,
             tri_ref, trit_ref, e_ref, et_ref,
             dpm_ref, dcw_ref, dcb_ref, dvec_ref, dng_ref,
             dht_ref, dcar_ref, yd_scr, yoff_scr, dx_scr, r2_scr, hs_scr, da_scr, dat_scr, dd_scr, dbc_scr):
        i = pl.program_id(0)
        n = nc - 1 - i

        @pl.when(i == 0)
        def _():
            dht_ref[...] = jnp.zeros_like(dht_ref)
            dcar_ref[...] = jnp.zeros_like(dcar_ref)
            dcw_ref[...] = jnp.zeros_like(dcw_ref)
            dcb_ref[...] = jnp.zeros_like(dcb_ref)
            dvec_ref[...] = jnp.zeros_like(dvec_ref)
            dng_ref[...] = jnp.zeros_like(dng_ref)
            dd_scr[...] = jnp.zeros_like(dd_scr)
            da_scr[...] = jnp.zeros_like(da_scr)
            dat_scr[...] = jnp.zeros_like(dat_scr)

        xbc = pm_ref[:, 0:CONV_DIM].astype(F32)
        prev16 = jnp.where(n == 0, 0.0, prev_ref[...].astype(F32))
        cw = cw_ref[...]
        f = _Ssd(xbc, prev16, pm_ref[:, DT_OFF:DT_OFF + 128].astype(F32), cw, cb_ref[...], dtb_ref[...],
                 al_ref[...], d_ref[...], tri_ref[...], e_ref[...])
        et = et_ref[...]
        y = f.y_pre_gate(lambda g: st_ref[0, g], yd_scr, yoff_scr)

        z = pm_ref[:, CONV_DIM:CONV_DIM + SSM_WIDTH].astype(F32)
        dyv = dy_ref[...].astype(F32)
        sz = _sig(z)
        silu = z * sz
        ypre = y * silu
        for g in range(SSM_GROUPS):
            gs = slice(g * SSM_GW, (g + 1) * SSM_GW)
            yg = ypre[:, gs]
            rr = lax.rsqrt(jnp.mean(yg * yg, axis=-1, keepdims=True) + EPS)
            nrm = yg * rr
            dng_ref[:, gs] += jnp.sum(dyv[:, gs] * nrm, axis=0, keepdims=True)
            dn = dyv[:, gs] * ng_ref[:, gs]
            dx_scr[:, gs] = rr * (dn - nrm * jnp.mean(dn * nrm, axis=-1, keepdims=True))
        dypre = dx_scr[...]
        d_y = dypre * silu
        dpm_ref[:, CONV_DIM:CONV_DIM + SSM_WIDTH] = (dypre * y * _dsilu(z, sz)).astype(dpm_ref.dtype)

        for g in range(SSM_GROUPS):
            bg, cg, cb = f.group(g)
            gs = slice(g * SSM_GW, (g + 1) * SSM_GW)
            htg = st_ref[0, g]
            dhn = dht_ref[g]
            dcb = jnp.zeros((BLK, BLK), F32)
            for j in range(8):
                h = g * 8 + j
                sl = slice(h * 64, (h + 1) * 64)
                dec = f.decay(h)
                dyh = d_y[:, sl]
                dmd = _dot_nt(dyh, f.xdt[:, sl]) * dec
                dcb = dcb + dmd
                gm = dmd * cb
                da_scr[:, h:h + 1] = jnp.sum(gm, axis=1, keepdims=True)
                dat_scr[h:h + 1, :] = jnp.sum(gm, axis=0, keepdims=True)
                dx_scr[:, sl] = _dot_tn(cb * dec, dyh)
            dz = f.ecs_x[:, gs] * d_y[:, gs]
            dbc_scr[:, 512 + g * 128:512 + (g + 1) * 128] = _dot(dcb, bg) + _dot_nt(dz, htg)
            dbc_scr[:, g * 128:(g + 1) * 128] = _dot_tn(dcb, cg) + _dot_nt(f.xdt[:, gs] * f.dte_x[:, gs], dhn)
            dws = _dot(bg, dhn)
            dx_scr[:, gs] += f.dte_x[:, gs] * dws
            r2_scr[:, gs] = dws * f.xdt[:, gs]
            hs_scr[:, gs] = _bcast8(jnp.sum(dhn * htg, axis=0, keepdims=True))
            dht_ref[g] = f.cd_x[:, gs] * dhn + _dot_tn(cg, dz)
        d_x = dx_scr[...]
        r1 = _dot_onehot(d_y * yoff_scr[...], et)
        r2 = _dot_onehot(r2_scr[...], et) * f.dte
        dcd = _dot_onehot(hs_scr[...], et)[0:1]
        d_tot = jnp.sum(r2, axis=0, keepdims=True) + dcd * f.cd
        row = lax.broadcasted_iota(jnp.int32, (BLK, 128), 0)
        d_a = da_scr[...] - dat_scr[...].T + r1 - r2 + jnp.where(row == BLK - 1, d_tot, 0.0)
        dadt = _dot_hi(trit_ref[...], d_a)
        ddt = dadt * f.a + _dot_onehot(d_x * f.xs, et)
        lane = lax.broadcasted_iota(jnp.int32, (BLK, 128), 1)
        dr = jnp.where(lane < SSM_HEADS, ddt * _sig(f.dtp), 0.0)
        dvec_ref[0:1, :] += jnp.sum(dr, axis=0, keepdims=True)
        dvec_ref[1:2, :] += jnp.sum(dadt * f.dt, axis=0, keepdims=True) * f.a
        dd_scr[...] += _bcast8(jnp.sum(d_y * f.xs, axis=0, keepdims=True))
        dpm_ref[:, DT_OFF:DT_OFF + 128] = dr.astype(dpm_ref.dtype)
        dpm_ref[:, DT_OFF + 128:SSM_COLS] = jnp.zeros((BLK, 128), dpm_ref.dtype)

        dxs = d_x * f.dt_x + f.d_x * d_y
        dact = jnp.concatenate([dxs, dbc_scr[...]], axis=1)
        dpre = dact * _dsilu(f.pre, f.sg)
        dcb_ref[...] += jnp.sum(dpre, axis=0, keepdims=True)
        dxraw = jnp.zeros((BLK, CONV_DIM), F32)
        nxt = dcar_ref[...]
        for k in range(4):
            dcw_ref[3 - k:4 - k, :] += jnp.sum(dpre * f.shifted[k], axis=0, keepdims=True)
            dxraw = dxraw + cw[3 - k:4 - k] * _shift_up(dpre, nxt, k)
        dcar_ref[...] = dpre[0:16]
        dpm_ref[:, 0:CONV_DIM] = dxraw.astype(dpm_ref.dtype)

        @pl.when(i == nc - 1)
        def _():
            dvec_ref[2:3, :] = _dot_onehot(dd_scr[...], et)[0:1]

    own, hosted = _call_hosting(
        body, "ssd_bwd", nc,
        in_specs=[pl.BlockSpec((BLK, SSM_COLS), lambda i: (nc - 1 - i, 0)),
                  pl.BlockSpec((16, CONV_DIM), lambda i: (jnp.maximum(8 * (nc - 1 - i) - 1, 0), 0)),
                  pl.BlockSpec((BLK, SSM_WIDTH), lambda i: (nc - 1 - i, 0)),
                  pl.BlockSpec((1, SSM_GROUPS, 128, SSM_GW), lambda i: (nc - 1 - i, 0, 0, 0)),
                  _full((4, CONV_DIM)), _full((1, CONV_DIM)), _full((1, 128)), _full((1, 128)), _full((1, 128)),
                  _full((1, SSM_WIDTH)), _full((BLK, BLK)), _full((BLK, BLK)), _full((128, SSM_WIDTH)),
                  _full((SSM_WIDTH, 128))],
        out_specs=[pl.BlockSpec((BLK, SSM_COLS), lambda i: (nc - 1 - i, 0)),
                   _full((8, CONV_DIM)), _full((1, CONV_DIM)), _full((8, 128)), _full((1, SSM_WIDTH))],
        out_shape=[jax.ShapeDtypeStruct((s, SSM_COLS), ACT_DTYPE), jax.ShapeDtypeStruct((8, CONV_DIM), F32),
                   jax.ShapeDtypeStruct((1, CONV_DIM), F32), jax.ShapeDtypeStruct((8, 128), F32),
                   jax.ShapeDtypeStruct((1, SSM_WIDTH), F32)],
        scratch=[pltpu.VMEM((SSM_GROUPS, 128, SSM_GW), F32), pltpu.VMEM((16, CONV_DIM), F32),
                 pltpu.VMEM((BLK, SSM_WIDTH), F32), pltpu.VMEM((BLK, SSM_WIDTH), F32),
                 pltpu.VMEM((BLK, SSM_WIDTH), F32), pltpu.VMEM((BLK, SSM_WIDTH), F32),
                 pltpu.VMEM((16, SSM_WIDTH), F32), pltpu.VMEM((BLK, 128), F32), pltpu.VMEM((128, BLK), F32),
                 pltpu.VMEM((16, SSM_WIDTH), F32), pltpu.VMEM((BLK, 1024), F32)],
        args=(pm, pm, dy, states, cw, cbias, dtb, alog, dsk, ng, tri, tri_t, e, e_t), ex=ex)
    return (*own, hosted) if ex is not None else tuple(own)


def merge_fwd(x, ya, ys, ym, pg, wa, ws, wm, wo, g_post):
    s, d = x.shape
    tm = min(256, s)

    def body(x_ref, ya_ref, ys_ref, ym_ref, pg_ref, wa_ref, ws_ref, wm_ref, wo_ref, g_ref,
             xo_ref, ba_ref, bs_ref, bm_ref, mg_ref, out_ref):
        ba = _dot_tn(ya_ref[...], wa_ref[...])
        bs = _dot(ys_ref[...], ws_ref[...])
        bm = _dot(ym_ref[...], wm_ref[...])
        merged = (_sig(pg_ref[:, 0:d].astype(F32)) * ba + _sig(pg_ref[:, d:2 * d].astype(F32)) * bs
                  + _sig(pg_ref[:, 2 * d:3 * d].astype(F32)) * bm)
        out = _dot(merged, wo_ref[...])
        r = lax.rsqrt(jnp.mean(out * out, axis=-1, keepdims=True) + EPS)
        xo_ref[...] = x_ref[...] + out * r * g_ref[...]
        ba_ref[...] = ba.astype(ba_ref.dtype)
        bs_ref[...] = bs.astype(bs_ref.dtype)
        bm_ref[...] = bm.astype(bm_ref.dtype)
        mg_ref[...] = merged.astype(mg_ref.dtype)
        out_ref[...] = out.astype(out_ref.dtype)

    rows = lambda w: pl.BlockSpec((tm, w), lambda i: (i, 0))
    act = jax.ShapeDtypeStruct((s, d), ACT_DTYPE)
    return pl.pallas_call(
        body, name="merge_fwd", grid=(s // tm,),
        in_specs=[rows(d), pl.BlockSpec((d, tm), lambda i: (0, i)), rows(d), rows(2 * d), rows(3 * d), _full((d, d)),
                  _full((d, d)), _full((2 * d, d)), _full((d, d)), _full((1, d))],
        out_specs=[rows(d)] * 6,
        out_shape=[jax.ShapeDtypeStruct((s, d), F32), act, act, act, act, act],
        compiler_params=_cparams(("parallel",)),
    )(x, ya, ys, ym, pg, wa, ws, wm, wo, g_post)


def merge_bwd(dx, out_s, pg, ba, bs, bm, wa, ws, wm, wo, g_post):
    s, d = dx.shape
    tm = min(256, s)

    def body(dx_ref, out_ref, pg_ref, ba_ref, bs_ref, bm_ref, wa_ref, ws_ref, wm_ref, wo_ref, g_ref,
             dout_ref, dba_ref, dbs_ref, dbm_ref, dpg_ref, dya_ref, dys_ref, dym_ref, dg_ref):
        @pl.when(pl.program_id(0) == 0)
        def _():
            dg_ref[...] = jnp.zeros_like(dg_ref)

        o = out_ref[...].astype(F32)
        dxv = dx_ref[...]
        r = lax.rsqrt(jnp.mean(o * o, axis=-1, keepdims=True) + EPS)
        nrm = o * r
        dg_ref[...] += jnp.sum(dxv * nrm, axis=0, keepdims=True)
        dn = dxv * g_ref[...]
        dout = r * (dn - nrm * jnp.mean(dn * nrm, axis=-1, keepdims=True))
        dout_ref[...] = dout.astype(dout_ref.dtype)
        dmerged = _dot_nt(dout, wo_ref[...])
        for q, (b_ref, db_ref, w_ref, dy_ref) in enumerate(((ba_ref, dba_ref, wa_ref, dya_ref),
                                                            (bs_ref, dbs_ref, ws_ref, dys_ref),
                                                            (bm_ref, dbm_ref, wm_ref, dym_ref))):
            gt = _sig(pg_ref[:, q * d:(q + 1) * d].astype(F32))
            db = dmerged * gt
            db_ref[...] = db.astype(db_ref.dtype)
            dpg_ref[:, q * d:(q + 1) * d] = (dmerged * b_ref[...].astype(F32) * gt * (1.0 - gt)).astype(dpg_ref.dtype)
            if q == 0:
                dy_ref[...] = _dot_nt(w_ref[...], db).astype(dy_ref.dtype)
            else:
                dy_ref[...] = _dot_nt(db, w_ref[...]).astype(dy_ref.dtype)

    rows = lambda w: pl.BlockSpec((tm, w), lambda i: (i, 0))
    act = lambda w: jax.ShapeDtypeStruct((s, w), ACT_DTYPE)
    return pl.pallas_call(
        body, name="merge_bwd", grid=(s // tm,),
        in_specs=[rows(d), rows(d), rows(3 * d), rows(d), rows(d), rows(d), _full((d, d)), _full((d, d)),
                  _full((2 * d, d)), _full((d, d)), _full((1, d))],
        out_specs=[rows(d), rows(d), rows(d), rows(d), rows(3 * d), pl.BlockSpec((d, tm), lambda i: (0, i)), rows(d),
                   rows(2 * d), _full((1, d))],
        out_shape=[act(d), act(d), act(d), act(d), act(3 * d), jax.ShapeDtypeStruct((d, s), ACT_DTYPE), act(d),
                   act(2 * d), jax.ShapeDtypeStruct((1, d), F32)],
        compiler_params=_cparams(("arbitrary",)),
    )(dx, out_s, pg, ba, bs, bm, wa, ws, wm, wo, g_post)


def loss_grad(y, target):
    s, d = y.shape
    tm = min(512, s)

    def body(y_ref, t_ref, dy_ref, l_ref):
        @pl.when(pl.program_id(0) == 0)
        def _():
            l_ref[...] = jnp.zeros_like(l_ref)

        err = y_ref[...] - t_ref[...]
        dy_ref[...] = err * (1.0 / d)
        part = jnp.sum(jnp.sum(err * err, axis=-1, keepdims=True) * (1.0 / d), axis=0, keepdims=True)
        l_ref[...] += 0.5 * jnp.broadcast_to(part, l_ref.shape)

    return pl.pallas_call(
        body, name="loss_grad", grid=(s // tm,),
        in_specs=[pl.BlockSpec((tm, d), lambda i: (i, 0)), pl.BlockSpec((tm, d), lambda i: (i, 0))],
        out_specs=[pl.BlockSpec((tm, d), lambda i: (i, 0)), _full((8, 128))],
        out_shape=[jax.ShapeDtypeStruct((s, d), F32), jax.ShapeDtypeStruct((8, 128), F32)],
        compiler_params=_cparams(("arbitrary",)),
    )(y, target)


def _mesh_pos():
    x, y, c = lax.axis_index("x"), lax.axis_index("y"), lax.axis_index("c")
    return x, y, c, 4 * x + 2 * y + c


def _peer(x, y, c, k):
    px = 1 - x if k & 4 else x
    py = 1 - y if k & 2 else y
    pc = 1 - c if k & 1 else c
    return (px, py, pc), 4 * px + 2 * py + pc


class Exchange:
    SAME_CORE = (2, 4, 6)

    def __init__(self, scattered, gathered):
        self.ns = len(scattered)
        self.arrays = list(scattered) + list(gathered)
        self.na = len(self.arrays)
        any_spec = pl.BlockSpec(memory_space=pl.ANY)
        self.in_specs = [any_spec] * self.na
        self.out_specs = [any_spec] * self.na
        self.out_shape = ([jax.ShapeDtypeStruct(a.shape, a.dtype) for a in scattered]
                          + [jax.ShapeDtypeStruct((N_DEV,) + a.shape, a.dtype) for a in gathered])
        self.scratch = [pltpu.SemaphoreType.DMA((self.na, N_DEV - 1)), pltpu.SemaphoreType.DMA((self.na, N_DEV - 1)),
                        pltpu.SemaphoreType.DMA((self.na,))]

    def _src(self, ins, q, slot):
        return ins[q].at[slot] if q < self.ns else ins[q]

    def _local(self, ins, outs, sems):
        me = _mesh_pos()[3]
        return [pltpu.make_async_copy(self._src(ins, q, me), outs[q].at[me], sems[2].at[q]) for q in range(self.na)]

    def _direct(self, ins, outs, sems, relations, arrays):
        x, y, c, me = _mesh_pos()
        copies = []
        for k in relations:
            peer, pidx = _peer(x, y, c, k)
            for q in arrays:
                copies.append(pltpu.make_async_remote_copy(
                    src_ref=self._src(ins, q, pidx), dst_ref=outs[q].at[me], send_sem=sems[0].at[q, k - 1],
                    recv_sem=sems[1].at[q, k - 1], device_id=peer, device_id_type=MESH))
        return copies

    def _arrivals(self, ins, outs, sems, relations, arrays):
        x, y, c, _ = _mesh_pos()
        copies = []
        for k in relations:
            peer, pidx = _peer(x, y, c, k)
            for q in arrays:
                copies.append(pltpu.make_async_remote_copy(
                    src_ref=self._src(ins, q, pidx), dst_ref=outs[q].at[pidx], send_sem=sems[0].at[q, k - 1],
                    recv_sem=sems[1].at[q, k - 1], device_id=peer, device_id_type=MESH))
        return copies

    def _relays(self, outs, sems):
        x, y, c, _ = _mesh_pos()
        sibling, _ = _peer(x, y, c, 1)
        copies = []
        for k in self.SAME_CORE:
            _, pidx = _peer(x, y, c, k)
            for q in range(self.ns, self.na):
                copies.append(pltpu.make_async_remote_copy(
                    src_ref=outs[q].at[pidx], dst_ref=outs[q].at[pidx], send_sem=sems[0].at[q, k],
                    recv_sem=sems[1].at[q, k], device_id=sibling, device_id_type=MESH))
        return copies

    def _sends(self, ins, outs, sems):
        return (self._direct(ins, outs, sems, range(1, N_DEV), range(self.ns))
                + self._direct(ins, outs, sems, (1,) + self.SAME_CORE, range(self.ns, self.na)))

    def start(self, ins, outs, sems):
        for cp in self._local(ins, outs, sems) + self._sends(ins, outs, sems):
            cp.start()

    def relay(self, ins, outs, sems):
        for cp in self._arrivals(ins, outs, sems, self.SAME_CORE, range(self.ns, self.na)):
            cp.wait_recv()
        for cp in self._relays(outs, sems):
            cp.start()

    def wait(self, ins, outs, sems):
        for cp in (self._arrivals(ins, outs, sems, range(1, N_DEV), range(self.ns))
                   + self._arrivals(ins, outs, sems, (1, 3, 5, 7), range(self.ns, self.na))):
            cp.wait_recv()
        for cp in self._sends(ins, outs, sems) + self._relays(outs, sems):
            cp.wait_send()
        for cp in self._local(ins, outs, sems):
            cp.wait()


def exchange(scattered, gathered, name):
    ex = Exchange(scattered, gathered)

    def body(*refs):
        ins, outs, sems = refs[:ex.na], refs[ex.na:2 * ex.na], refs[2 * ex.na:]
        ex.start(ins, outs, sems)
        ex.relay(ins, outs, sems)
        ex.wait(ins, outs, sems)

    return pl.pallas_call(body, name=name, in_specs=ex.in_specs, out_specs=ex.out_specs, out_shape=ex.out_shape,
                          scratch_shapes=ex.scratch)(*ex.arrays)


def adamw(parts_list, w, m, v, tile, name, ex=None):
    npart, _, dp = parts_list[0].shape
    d = w.shape[-1]
    counts = [p.shape[1] // tile for p in parts_list]
    starts = [sum(counts[:q]) for q in range(len(counts))]
    n_lists = len(parts_list)

    def body(*refs):
        p_refs = refs[:n_lists]
        w_ref, m_ref, v_ref, g_ref, dw_ref, nm_ref, nv_ref = refs[n_lists:]
        i = pl.program_id(0)
        for q, p_ref in enumerate(p_refs):
            @pl.when((i >= starts[q]) & (i < starts[q] + counts[q]))
            def _(p_ref=p_ref):
                acc = p_ref[0, :, 0:d].astype(F32)
                for k in range(1, npart):
                    acc = acc + p_ref[k, :, 0:d].astype(F32)
                g_ref[...] = acc

        g = g_ref[...]
        nm = ADAM_B1 * m_ref[...] + (1.0 - ADAM_B1) * g
        nv = ADAM_B2 * v_ref[...] + (1.0 - ADAM_B2) * (g * g)
        nm_ref[...] = nm
        nv_ref[...] = nv
        m_hat = nm / (1.0 - ADAM_B1 ** ADAM_STEP)
        v_hat = nv / (1.0 - ADAM_B2 ** ADAM_STEP)
        dw_ref[...] = -ADAM_LR * (m_hat / (jnp.sqrt(v_hat) + ADAM_EPS) + ADAM_WD * w_ref[...])

    def part_rows(q):
        return lambda i: (0, jnp.clip(i - starts[q], 0, counts[q] - 1), 0)

    if w.ndim == 3:
        rows = pl.BlockSpec((None, tile, d), lambda i: (i // counts[0], i % counts[0], 0))
    else:
        rows = pl.BlockSpec((tile, d), lambda i: (i, 0))
    own, hosted = _call_hosting(
        body, name, sum(counts),
        in_specs=[pl.BlockSpec((npart, tile, dp), part_rows(q)) for q in range(n_lists)] + [rows, rows, rows],
        out_specs=[rows] * 4, out_shape=[jax.ShapeDtypeStruct(w.shape, F32)] * 4, scratch=[],
        args=(*parts_list, w, m, v), ex=ex)
    return (*own, hosted) if ex is not None else tuple(own)


def _pad_rows(a, rows):
    return jnp.pad(a, ((0, rows - a.shape[0]), (0, 0)))


def _pack_rest(w_att, w_sg, w_ssm, w_out):
    parts = []
    for l in range(2):
        parts += [w_att[l], w_sg[l], w_ssm[l], w_out[l]]
    return jnp.concatenate(parts, axis=0)


def _unpack_rest(p):
    outs = [[], [], [], []]
    o = 0
    for l in range(2):
        for q, rws in enumerate(REST_PARTS):
            outs[q].append(p[o:o + rws])
            o += rws
    return [jnp.stack(t) for t in outs]


def _pack_win(w_in):
    return jnp.pad(w_in.reshape(2 * D_MODEL, WIN_SHARD), ((0, 0), (0, WIN_LANES - WIN_SHARD)))


W_IN_MAP = ((0, 1024, "att", 0), (1024, 1280, "att", 2048), (1280, 2304, "att", 1024), (2304, 5376, "sg", 0),
            (5376, 7424, "ssm", 3072), (7424, 10496, "ssm", 0), (10496, 10528, "ssm", 5120), (10528, 13600, "gate", 0))
SLAB_COLS = {"att": ATT_COLS, "sg": SG_COLS, "ssm": SSM_COLS, "gate": GATE_COLS}


def _slabs_from_shards(g):
    slabs = {}
    for name, width in SLAB_COLS.items():
        pieces, filled = [], 0
        for ga, gb, _, off in sorted((m for m in W_IN_MAP if m[2] == name), key=lambda m: m[3]):
            assert off == filled
            a = ga
            while a < gb:
                d = a // WIN_SHARD
                hi = min(gb, WIN_SHARD * (d + 1))
                pieces.append(g[d, :, a - WIN_SHARD * d:hi - WIN_SHARD * d])
                a = hi
            filled += gb - ga
        if filled < width:
            pieces.append(jnp.zeros((D_MODEL, width - filled), g.dtype))
        slabs[name] = jnp.concatenate(pieces, axis=1)
    return slabs


def _shards_from_slabs(dslabs):
    out = []
    for d in range(N_DEV):
        a, b = WIN_SHARD * d, WIN_SHARD * (d + 1)
        pieces = []
        for ga, gb, name, off in W_IN_MAP:
            lo, hi = max(a, ga), min(b, gb)
            if lo < hi:
                pieces.append(dslabs[name][:, off + lo - ga:off + hi - ga])
        pieces.append(jnp.zeros((D_MODEL, WIN_LANES - WIN_SHARD), pieces[0].dtype))
        out.append(jnp.concatenate(pieces, axis=1).astype(WIRE_DTYPE))
    return jnp.stack(out)


SMALL_SIZES = (("norm_pre", 2048), ("norm_post", 2048), ("rel_bias", 512), ("att_sinks", 32), ("sg_ln_g", 2048),
               ("sg_ln_b", 2048), ("sg_w", 262144), ("sg_b", 2048), ("ssm_conv_b", 6144), ("ssm_dt_bias", 64),
               ("ssm_a_log", 64), ("ssm_d", 64), ("ssm_norm_g", 4096), ("conv_w_full", 24576))


def _pack_small(d):
    parts = []
    for name, size in SMALL_SIZES:
        rows = 8 * (-(-size // (8 * D_MODEL)))
        flat = d[name].reshape(-1) if name in d else jnp.zeros((size,), F32)
        parts.append(jnp.pad(flat, (0, rows * D_MODEL - size)).reshape(rows, D_MODEL))
    return _pad_rows(jnp.concatenate(parts, axis=0), SMALL_ROWS)


def _unpack_small(p, shapes):
    out, o = {}, 0
    for name, size in SMALL_SIZES:
        rows = 8 * (-(-size // (8 * D_MODEL)))
        if name in shapes:
            out[name] = p[o:o + rows].reshape(-1)[:size].reshape(shapes[name])
        o += rows
    return out


def _bucket_onehot_t():
    qi = jnp.arange(BLK, dtype=jnp.int32)[None, :]
    kj = jnp.arange(BLK, dtype=jnp.int32)[:, None]
    dd = (qi - kj) & (BLK - 1)
    in_window = dd >= 0
    max_exact = REL_BUCKETS // 2
    dist_f = jnp.maximum(dd, 1).astype(F32)
    large = max_exact + (jnp.log(dist_f / max_exact) / math.log(128 / max_exact)
                         * (REL_BUCKETS - max_exact)).astype(jnp.int32)
    large = jnp.minimum(large, REL_BUCKETS - 1)
    bucket = jnp.where(dd < max_exact, dd, large).reshape(1, -1)
    onehot_t = (bucket == jnp.arange(REL_BUCKETS, dtype=jnp.int32)[:, None]).astype(F32)
    maskadd = jnp.where(in_window, 0.0, NEG).astype(F32).reshape(1, -1)
    return onehot_t, maskadd


WEIGHTS = ['w_in', 'norm_pre', 'norm_post', 'rel_bias', 'att_sinks', 'sg_ln_g', 'sg_ln_b', 'sg_w', 'sg_b',
           'ssm_conv_w', 'ssm_conv_b', 'ssm_dt_bias', 'ssm_a_log', 'ssm_d', 'ssm_norm_g',
           'w_br_att', 'w_br_sg', 'w_br_ssm', 'w_out']
REST = ('w_br_att', 'w_br_sg', 'w_br_ssm', 'w_out')


def kernel(x, w_in, norm_pre, norm_post, rel_bias, att_sinks, sg_ln_g, sg_ln_b, sg_w, sg_b, ssm_conv_w, ssm_conv_b, ssm_dt_bias, ssm_a_log, ssm_d, ssm_norm_g, w_br_att, w_br_sg, w_br_ssm, w_out, loss_target, m_w_in, m_norm_pre, m_norm_post, m_rel_bias, m_att_sinks, m_sg_ln_g, m_sg_ln_b, m_sg_w, m_sg_b, m_ssm_conv_w, m_ssm_conv_b, m_ssm_dt_bias, m_ssm_a_log, m_ssm_d, m_ssm_norm_g, m_w_br_att, m_w_br_sg, m_w_br_ssm, m_w_out, v_w_in, v_norm_pre, v_norm_post, v_rel_bias, v_att_sinks, v_sg_ln_g, v_sg_ln_b, v_sg_w, v_sg_b, v_ssm_conv_w, v_ssm_conv_b, v_ssm_dt_bias, v_ssm_a_log, v_ssm_d, v_ssm_norm_g, v_w_br_att, v_w_br_sg, v_w_br_ssm, v_w_out):
    w = dict(w_in=w_in, norm_pre=norm_pre, norm_post=norm_post, rel_bias=rel_bias, att_sinks=att_sinks,
             sg_ln_g=sg_ln_g, sg_ln_b=sg_ln_b, sg_w=sg_w, sg_b=sg_b, ssm_conv_w=ssm_conv_w, ssm_conv_b=ssm_conv_b,
             ssm_dt_bias=ssm_dt_bias, ssm_a_log=ssm_a_log, ssm_d=ssm_d, ssm_norm_g=ssm_norm_g,
             w_br_att=w_br_att, w_br_sg=w_br_sg, w_br_ssm=w_br_ssm, w_out=w_out)
    mom = dict(w_in=m_w_in, norm_pre=m_norm_pre, norm_post=m_norm_post, rel_bias=m_rel_bias, att_sinks=m_att_sinks,
               sg_ln_g=m_sg_ln_g, sg_ln_b=m_sg_ln_b, sg_w=m_sg_w, sg_b=m_sg_b, ssm_conv_w=m_ssm_conv_w,
               ssm_conv_b=m_ssm_conv_b, ssm_dt_bias=m_ssm_dt_bias, ssm_a_log=m_ssm_a_log, ssm_d=m_ssm_d,
               ssm_norm_g=m_ssm_norm_g, w_br_att=m_w_br_att, w_br_sg=m_w_br_sg, w_br_ssm=m_w_br_ssm, w_out=m_w_out)
    var = dict(w_in=v_w_in, norm_pre=v_norm_pre, norm_post=v_norm_post, rel_bias=v_rel_bias, att_sinks=v_att_sinks,
               sg_ln_g=v_sg_ln_g, sg_ln_b=v_sg_ln_b, sg_w=v_sg_w, sg_b=v_sg_b, ssm_conv_w=v_ssm_conv_w,
               ssm_conv_b=v_ssm_conv_b, ssm_dt_bias=v_ssm_dt_bias, ssm_a_log=v_ssm_a_log, ssm_d=v_ssm_d,
               ssm_norm_g=v_ssm_norm_g, w_br_att=v_w_br_att, w_br_sg=v_w_br_sg, w_br_ssm=v_w_br_ssm, w_out=v_w_out)
    xs0 = x[0]
    target = loss_target[0]
    my_dev = 4 * lax.axis_index("x") + 2 * lax.axis_index("y") + lax.axis_index("c")

    conv_shard = _pad_rows(ssm_conv_w.reshape(-1, D_MODEL), 8)
    win_shard = _pack_win(w_in).astype(WIRE_DTYPE)
    rest_shard = _pack_rest(*[w[n] for n in REST]).astype(WIRE_DTYPE)
    layer_shards = [[win_shard[l * D_MODEL:(l + 1) * D_MODEL], rest_shard[l * LAYER_REST:(l + 1) * LAYER_REST]]
                    for l in range(2)]
    g_win0, gathered_conv = exchange([], [layer_shards[0][0], conv_shard], "all_gather")
    conv_full = gathered_conv[:, 0:3].reshape(N_DEV, 2, 4, 384).transpose(1, 2, 0, 3).reshape(2, 4, CONV_DIM)

    def set_rest(lw, g_rest):
        o = 0
        for name, rws in zip(("att", "sg", "ssm", "out"), REST_PARTS):
            lw[name] = g_rest[:, o:o + rws].reshape(N_DEV * rws, D_MODEL).astype(MXU_DTYPE)
            o += rws

    def layer_weights(l, g_win):
        slabs = _slabs_from_shards(g_win)
        lw = {"in_" + name: slab.astype(MXU_DTYPE) for name, slab in slabs.items()}
        lw["in_att"] = lw["in_att"].T
        tril = jnp.tril(jnp.ones((BLK, BLK), bool))
        sgw = jnp.where(tril[None], sg_w[l], 0.0)
        lw.update(
            g_pre=norm_pre[l][None], g_post=norm_post[l][None], sinks=jnp.repeat(att_sinks[l], BLK).reshape(2, GROUP_LANES),
            ln_g=sg_ln_g[l][None], ln_b=sg_ln_b[l][None], sgw=sgw.astype(MXU_DTYPE),
            sgw_t=sgw.transpose(0, 2, 1).astype(MXU_DTYPE), sgb_t=_pad_lanes(sg_b[l].T),
            cw=conv_full[l], cb=ssm_conv_b[l][None], dtb=_pad_lanes(ssm_dt_bias[l][None]),
            alog=_pad_lanes(ssm_a_log[l][None]), dsk=_pad_lanes(ssm_d[l][None]), ng=ssm_norm_g[l][None])
        return lw

    onehot_t, maskadd = _bucket_onehot_t()
    bias = bias_table(rel_bias.T, onehot_t, maskadd).reshape(2, GROUP_HEADS, BLK, BLK).transpose(0, 2, 1, 3)
    bias = bias.reshape(2, BLK, GROUP_LANES)

    saved = []
    xl = xs0
    layers = [layer_weights(0, g_win0)]
    for l in range(2):
        lw = layers[l]
        h = rmsnorm_fwd(xl, lw["g_pre"])
        pa = mm_nt(lw["in_att"], h, 1152, "proj_att")
        ps = mm_nn(h, lw["in_sg"], 1536, "proj_sg")
        pm = mm_nn(h, lw["in_ssm"], 1792, "proj_ssm")
        pg = mm_nn(h, lw["in_gate"], 1536, "proj_gate")
        if l == 0:
            ya, (g_rest0,) = attn_fwd(pa, bias, lw["sinks"], Exchange([], [layer_shards[0][1]]))
            set_rest(lw, g_rest0)
        else:
            ya = attn_fwd(pa, bias, lw["sinks"])
        ys = sgu_fwd(ps, lw["ln_g"], lw["ln_b"], lw["sgw"], lw["sgb_t"])
        ssd_args = (pm, lw["cw"], lw["cb"], lw["dtb"], lw["alog"], lw["dsk"], lw["ng"])
        if l == 0:
            ym, states, (g_win1, g_rest1) = ssd_fwd(*ssd_args, Exchange([], layer_shards[1]))
            layers.append(layer_weights(1, g_win1))
            set_rest(layers[1], g_rest1)
        else:
            ym, states = ssd_fwd(*ssd_args)
        x_next, ba, bs, bm, merged, out_s = merge_fwd(xl, ya, ys, ym, pg, lw["att"], lw["sg"], lw["ssm"], lw["out"],
                                                      lw["g_post"])
        saved.append(dict(x=xl, h=h, pa=pa, ps=ps, pm=pm, pg=pg, ya=ya, ys=ys, ym=ym, states=states, ba=ba, bs=bs,
                          bm=bm, merged=merged, out_s=out_s))
        xl = x_next

    dx, loss_part = loss_grad(xl, target)
    loss = lax.psum(loss_part[0, 0], ("x", "y", "c"))

    dbias = jnp.zeros((2, BLK, GROUP_LANES), F32)
    win_grads, rest_grads = [None, None], [None, None]
    small = {n: [None, None] for n in ("norm_pre", "norm_post", "att_sinks", "sg_ln_g", "sg_ln_b", "sg_w", "sg_b",
                                       "ssm_conv_b", "ssm_dt_bias", "ssm_a_log", "ssm_d", "ssm_norm_g",
                                       "conv_w_full")}
    for l in (1, 0):
        lw, sv = layers[l], saved[l]
        dout, dba, dbs, dbm, dpg, dya, dys, dym, dg_post = merge_bwd(
            dx, sv["out_s"], sv["pg"], sv["ba"], sv["bs"], sv["bm"], lw["att"], lw["sg"], lw["ssm"], lw["out"],
            lw["g_post"])
        dw_out = mm_tn(sv["merged"], dout, 1024, "dw_out")
        dw_att = mm_kn(sv["ya"], dba, 1024, "dw_br_att")
        dw_sg = mm_tn(sv["ys"], dbs, 1024, "dw_br_sg")
        dw_ssm = mm_tn(sv["ym"], dbm, 1024, "dw_br_ssm")
        rest_grads[l] = jnp.concatenate(
            [dw_att.reshape(N_DEV, 128, D_MODEL), dw_sg.reshape(N_DEV, 128, D_MODEL),
             dw_ssm.reshape(N_DEV, 256, D_MODEL), dw_out.reshape(N_DEV, 128, D_MODEL)], axis=1).astype(WIRE_DTYPE)
        dpa, dbias, dsinks = attn_bwd(sv["pa"], dya, bias, lw["sinks"], dbias)
        dps, dsgw, dsgb_t, dln_g, dln_b = sgu_bwd(sv["ps"], dys, lw["ln_g"], lw["ln_b"], lw["sgw"], lw["sgw_t"],
                                                  lw["sgb_t"])
        ssd_args = (sv["pm"], dym, sv["states"], lw["cw"], lw["cb"], lw["dtb"], lw["alog"], lw["dsk"], lw["ng"])
        if l == 0:
            dpm, dcw, dcb, dvec, dng, (recv_win1, recv_rest1, recv_rest0) = ssd_bwd(
                *ssd_args, Exchange([win_grads[1], rest_grads[1], rest_grads[0]], []))
        else:
            dpm, dcw, dcb, dvec, dng = ssd_bwd(*ssd_args)
        dslabs = dict(att=mm_kn(dpa, sv["h"], 1152, "dw_in_att").T, sg=mm_tn(sv["h"], dps, 3072, "dw_in_sg"),
                      ssm=mm_tn(sv["h"], dpm, 2688, "dw_in_ssm"), gate=mm_tn(sv["h"], dpg, 3072, "dw_in_gate"))
        win_grads[l] = _shards_from_slabs(dslabs)
        dh_args = ([dpa, dps, dpm, dpg], [lw["in_att"], lw["in_sg"], lw["in_ssm"], lw["in_gate"]], sv["x"],
                   lw["g_pre"], dx)
        if l == 0:
            dx, dg_pre, (recv_win0,) = dh_norm_bwd(*dh_args, Exchange([win_grads[0]], []))
        else:
            dx, dg_pre = dh_norm_bwd(*dh_args)
        small["norm_pre"][l] = dg_pre[0]
        small["norm_post"][l] = dg_post[0]
        small["att_sinks"][l] = dsinks[0, :ATT_HEADS]
        small["sg_ln_g"][l] = dln_g[0]
        small["sg_ln_b"][l] = dln_b[0]
        small["sg_w"][l] = dsgw
        small["sg_b"][l] = dsgb_t[:, :SG_GROUPS].T
        small["ssm_conv_b"][l] = dcb[0]
        small["ssm_dt_bias"][l] = dvec[0, :SSM_HEADS]
        small["ssm_a_log"][l] = dvec[1, :SSM_HEADS]
        small["ssm_d"][l] = dvec[2, :SSM_HEADS]
        small["ssm_norm_g"][l] = dng[0]
        small["conv_w_full"][l] = dcw[0:4]
    grad_x = dx
    dbias = dbias.reshape(2, BLK, GROUP_HEADS, BLK).transpose(0, 2, 1, 3).reshape(ATT_HEADS, BLK * BLK)
    d_rel_bias = bias_table_bwd(dbias, onehot_t).T

    small_d = {n: jnp.stack(v) for n, v in small.items()}
    small_d["rel_bias"] = d_rel_bias
    *res_win, (recv_small,) = adamw([recv_win0, recv_win1], w_in, m_w_in, v_w_in, WIN_TILE, "adamw_w_in",
                                    Exchange([], [_pack_small(small_d)]))
    res_rest = adamw([recv_rest0, recv_rest1], _pack_rest(*[w[n] for n in REST]), _pack_rest(*[mom[n] for n in REST]),
                     _pack_rest(*[var[n] for n in REST]), REST_TILE, "adamw_rest")
    small_names = [n for n, _ in SMALL_SIZES if n != "conv_w_full"]
    g_s, dw_s, nm_s, nv_s = adamw([recv_small], _pack_small({n: w[n] for n in small_names}),
                                  _pack_small({n: mom[n] for n in small_names}),
                                  _pack_small({n: var[n] for n in small_names}), SMALL_TILE, "adamw_small")
    shapes = {n: w[n].shape for n in small_names}
    shapes["conv_w_full"] = (2, 4, CONV_DIM)
    g_conv_full = _unpack_small(g_s, shapes)["conv_w_full"]
    g_conv = lax.dynamic_slice_in_dim(g_conv_full, my_dev * 384, 384, axis=2)
    pack_conv = lambda a: _pad_rows(a.reshape(-1, D_MODEL), 8)
    g_c, dw_c, nm_c, nv_c = adamw([pack_conv(g_conv)[None]], pack_conv(ssm_conv_w), pack_conv(m_ssm_conv_w),
                                  pack_conv(v_ssm_conv_w), 8, "adamw_conv")

    results = {}
    for q, (tag, psm, pc) in enumerate((("grad", g_s, g_c), ("delta", dw_s, dw_c), ("new_m", nm_s, nm_c),
                                        ("new_v", nv_s, nv_c))):
        r = dict(zip(REST, _unpack_rest(res_rest[q])))
        r["w_in"] = res_win[q]
        r.update(_unpack_small(psm, {n: w[n].shape for n in small_names}))
        r["ssm_conv_w"] = pc[0:3].reshape(2, 4, 384)
        results[tag] = r
    outs = [loss, grad_x[None]]
    for tag in ("grad", "delta", "new_m", "new_v"):
        outs += [results[tag][n] for n in WEIGHTS]
    return tuple(outs)
```

```python
import math

import jax
import jax.numpy as jnp
from jax import lax
from jax.experimental import pallas as pl
from jax.experimental.pallas import tpu as pltpu

F32 = jnp.float32
MXU_DTYPE = jnp.bfloat16
ACT_DTYPE = jnp.bfloat16
WIRE_DTYPE = jnp.bfloat16
HI = lax.Precision.HIGHEST
MESH = pl.DeviceIdType.MESH

D_MODEL = 1024
N_DEV = 8
ATT_HEADS = 16
HEAD_DIM = 64
BLK = 128
SG_GROUPS = 8
SSM_WIDTH = 2048
SSM_HEADS = 32
SSM_GROUPS = 4
SSM_GW = SSM_WIDTH // SSM_GROUPS
CONV_DIM = 3072
REL_BUCKETS = 32
EPS = 1e-6
NEG = -1e30

ATT_COLS = 2304
SG_COLS = 3072
SSM_COLS = 5376
GATE_COLS = 3072
DT_OFF = 5120

VMEM_LIMIT_V7X = 56 * 2 ** 20

ADAM_LR, ADAM_B1, ADAM_B2, ADAM_EPS, ADAM_WD, ADAM_STEP = 0.001, 0.9, 0.999, 1e-08, 0.01, 10

WIN_SHARD = 1700
WIN_LANES = 1792
REST_PARTS = (128, 128, 256, 128)
LAYER_REST = sum(REST_PARTS)
REST_TILE = 128
WIN_TILE = 128
SMALL_ROWS = 384
SMALL_TILE = 128


def _cparams(sem=None):
    return pltpu.CompilerParams(dimension_semantics=sem, vmem_limit_bytes=VMEM_LIMIT_V7X)


def _dot(a, b):
    return jnp.dot(a.astype(MXU_DTYPE), b.astype(MXU_DTYPE), preferred_element_type=F32)


def _dot_nt(a, b):
    return lax.dot_general(a.astype(MXU_DTYPE), b.astype(MXU_DTYPE), (((1,), (1,)), ((), ())),
                           preferred_element_type=F32)


def _dot_tn(a, b):
    return lax.dot_general(a.astype(MXU_DTYPE), b.astype(MXU_DTYPE), (((0,), (0,)), ((), ())),
                           preferred_element_type=F32)


def _dot_hi(a, b):
    return jnp.dot(a, b, precision=HI, preferred_element_type=F32)


def _dot_onehot(a, onehot):
    hi = a.astype(jnp.bfloat16)
    lo = (a - hi.astype(F32)).astype(jnp.bfloat16)
    return (jnp.dot(hi, onehot, preferred_element_type=F32) + jnp.dot(lo, onehot, preferred_element_type=F32))


def _dot_hi_nt(a, b):
    return lax.dot_general(a, b, (((1,), (1,)), ((), ())), precision=HI, preferred_element_type=F32)


def _sig(x):
    return 1.0 / (1.0 + jnp.exp(-x))


def _dsilu(x, s):
    return s * (1.0 + x * (1.0 - s))


def _full(shape):
    nd = len(shape)
    return pl.BlockSpec(shape, lambda *_: (0,) * nd)


def rmsnorm_fwd(x, g):
    s, d = x.shape
    tm = min(512, s)

    def body(x_ref, g_ref, o_ref):
        xv = x_ref[...]
        r = lax.rsqrt(jnp.mean(xv * xv, axis=-1, keepdims=True) + EPS)
        o_ref[...] = (xv * r * g_ref[...]).astype(o_ref.dtype)

    return pl.pallas_call(
        body, name="rmsnorm_fwd", grid=(s // tm,),
        in_specs=[pl.BlockSpec((tm, d), lambda i: (i, 0)), _full((1, d))],
        out_specs=pl.BlockSpec((tm, d), lambda i: (i, 0)),
        out_shape=jax.ShapeDtypeStruct((s, d), ACT_DTYPE),
        compiler_params=_cparams(("parallel",)),
    )(x, g)


def mm_nn(a, b, tn, name):
    s, k = a.shape
    n = b.shape[1]
    tm = min(2048, s)

    def body(a_ref, b_ref, o_ref):
        o_ref[...] = _dot(a_ref[...], b_ref[...]).astype(o_ref.dtype)

    return pl.pallas_call(
        body, name=name, grid=(s // tm, n // tn),
        in_specs=[pl.BlockSpec((tm, k), lambda i, j: (i, 0)), pl.BlockSpec((k, tn), lambda i, j: (0, j))],
        out_specs=pl.BlockSpec((tm, tn), lambda i, j: (i, j)),
        out_shape=jax.ShapeDtypeStruct((s, n), ACT_DTYPE),
        compiler_params=_cparams(("parallel", "arbitrary")),
    )(a, b)


def mm_nt(a, b, tm, name):
    m, k = a.shape
    s = b.shape[0]
    ts = min(2048, s)

    def body(a_ref, b_ref, o_ref):
        o_ref[...] = _dot_nt(a_ref[...], b_ref[...]).astype(o_ref.dtype)

    return pl.pallas_call(
        body, name=name, grid=(s // ts, m // tm),
        in_specs=[pl.BlockSpec((tm, k), lambda i, j: (j, 0)), pl.BlockSpec((ts, k), lambda i, j: (i, 0))],
        out_specs=pl.BlockSpec((tm, ts), lambda i, j: (j, i)),
        out_shape=jax.ShapeDtypeStruct((m, s), ACT_DTYPE),
        compiler_params=_cparams(("parallel", "arbitrary")),
    )(a, b)


def mm_kn(a, b, tm, name):
    m, s = a.shape
    n = b.shape[1]
    ts = min(512, s)
    nt = s // ts

    def body(a_ref, b_ref, o_ref, acc_ref):
        @pl.when(pl.program_id(1) == 0)
        def _():
            acc_ref[...] = jnp.zeros_like(acc_ref)

        acc_ref[...] += _dot(a_ref[...], b_ref[...])

        @pl.when(pl.program_id(1) == nt - 1)
        def _():
            o_ref[...] = acc_ref[...].astype(o_ref.dtype)

    return pl.pallas_call(
        body, name=name, grid=(m // tm, nt),
        in_specs=[pl.BlockSpec((tm, ts), lambda j, t: (j, t)), pl.BlockSpec((ts, n), lambda j, t: (t, 0))],
        out_specs=pl.BlockSpec((tm, n), lambda j, t: (j, 0)),
        out_shape=jax.ShapeDtypeStruct((m, n), WIRE_DTYPE),
        scratch_shapes=[pltpu.VMEM((tm, n), F32)],
        compiler_params=_cparams(("parallel", "arbitrary")),
    )(a, b)


def mm_tn(a, b, tn, name):
    s, k = a.shape
    n = b.shape[1]
    ts = min(512, s)
    nt = s // ts

    def body(a_ref, b_ref, o_ref, acc_ref):
        @pl.when(pl.program_id(1) == 0)
        def _():
            acc_ref[...] = jnp.zeros_like(acc_ref)

        acc_ref[...] += _dot_tn(a_ref[...], b_ref[...])

        @pl.when(pl.program_id(1) == nt - 1)
        def _():
            o_ref[...] = acc_ref[...].astype(o_ref.dtype)

    return pl.pallas_call(
        body, name=name, grid=(n // tn, nt),
        in_specs=[pl.BlockSpec((ts, k), lambda j, t: (t, 0)), pl.BlockSpec((ts, tn), lambda j, t: (t, j))],
        out_specs=pl.BlockSpec((k, tn), lambda j, t: (0, j)),
        out_shape=jax.ShapeDtypeStruct((k, n), WIRE_DTYPE),
        scratch_shapes=[pltpu.VMEM((k, tn), F32)],
        compiler_params=_cparams(("parallel", "arbitrary")),
    )(a, b)


def dh_norm_bwd(dslabs, wslabs, x, g, dres, ex=None):
    s, d = x.shape
    tm = min(1024, s)
    tk = 768
    counts = [ds.shape[0 if q == 0 else 1] // tk for q, ds in enumerate(dslabs)]
    starts = [sum(counts[:i]) for i in range(len(counts))]
    nk = sum(counts)
    ns = len(dslabs)

    hosted = ex is not None
    ni = s // tm

    def mm_body(*refs):
        (own_in, (dh_ref,), _), hosted_refs = _split_hosted(refs, 2 * ns, 1, 0, ex)
        d_refs, w_refs = own_in[:ns], own_in[ns:]
        i, k = pl.program_id(0), pl.program_id(1)
        if hosted:
            @pl.when((i == 0) & (k == 0))
            def _():
                ex.start(*hosted_refs)

            @pl.when((i == ni - 1) & (k == nk - 1))
            def _():
                ex.relay(*hosted_refs)
                ex.wait(*hosted_refs)

        @pl.when(k == 0)
        def _():
            dh_ref[...] = jnp.zeros_like(dh_ref)

        for q in range(ns):
            @pl.when((k >= starts[q]) & (k < starts[q] + counts[q]))
            def _(q=q):
                if q == 0:
                    dh_ref[...] += _dot_tn(d_refs[q][...], w_refs[q][...])
                else:
                    dh_ref[...] += _dot_nt(d_refs[q][...], w_refs[q][...])

    def clamp(q):
        if q == 0:
            return pl.BlockSpec((tk, tm), lambda i, k: (jnp.clip(k - starts[q], 0, counts[q] - 1), i))
        return pl.BlockSpec((tm, tk), lambda i, k: (i, jnp.clip(k - starts[q], 0, counts[q] - 1)))

    def clamp_w(q):
        if q == 0:
            return pl.BlockSpec((tk, d), lambda i, k: (jnp.clip(k - starts[q], 0, counts[q] - 1), 0))
        return pl.BlockSpec((d, tk), lambda i, k: (0, jnp.clip(k - starts[q], 0, counts[q] - 1)))

    res = pl.pallas_call(
        mm_body, name="dh_matmul_scatter" if hosted else "dh_matmul", grid=(ni, nk),
        in_specs=([clamp(q) for q in range(ns)] + [clamp_w(q) for q in range(ns)]
                  + (ex.in_specs if hosted else [])),
        out_specs=[pl.BlockSpec((tm, d), lambda i, k: (i, 0))] + (ex.out_specs if hosted else []),
        out_shape=[jax.ShapeDtypeStruct((s, d), F32)] + (ex.out_shape if hosted else []),
        scratch_shapes=ex.scratch if hosted else [],
        compiler_params=_cparams(("arbitrary" if hosted else "parallel", "arbitrary")),
    )(*dslabs, *wslabs, *(ex.arrays if hosted else []))
    dh, ex_results = res[0], res[1:]

    te = min(512, s)

    def norm_body(dh_ref, x_ref, g_ref, dres_ref, dx_ref, dg_ref):
        @pl.when(pl.program_id(0) == 0)
        def _():
            dg_ref[...] = jnp.zeros_like(dg_ref)

        xv = x_ref[...]
        r = lax.rsqrt(jnp.mean(xv * xv, axis=-1, keepdims=True) + EPS)
        xn = xv * r
        dhv = dh_ref[...]
        dg_ref[...] += jnp.sum(dhv * xn, axis=0, keepdims=True)
        dxn = dhv * g_ref[...]
        dx_ref[...] = dres_ref[...] + r * (dxn - xn * jnp.mean(dxn * xn, axis=-1, keepdims=True))

    rows = pl.BlockSpec((te, d), lambda i: (i, 0))
    dx, dg = pl.pallas_call(
        norm_body, name="norm_bwd", grid=(s // te,),
        in_specs=[rows, rows, _full((1, d)), rows],
        out_specs=[rows, _full((1, d))],
        out_shape=[jax.ShapeDtypeStruct((s, d), F32), jax.ShapeDtypeStruct((1, d), F32)],
        compiler_params=_cparams(("arbitrary",)),
    )(dh, x, g, dres)
    return (dx, dg, ex_results) if hosted else (dx, dg)


def bias_table(rel_bias_t, onehot_t, maskadd):
    n = onehot_t.shape[1]
    tn = 8192

    def body(r_ref, o_ref, m_ref, out_ref):
        out_ref[...] = _dot_hi(r_ref[...], o_ref[...]) + m_ref[...]

    return pl.pallas_call(
        body, name="bias_table", grid=(n // tn,),
        in_specs=[_full((ATT_HEADS, REL_BUCKETS)), pl.BlockSpec((REL_BUCKETS, tn), lambda j: (0, j)),
                  pl.BlockSpec((1, tn), lambda j: (0, j))],
        out_specs=pl.BlockSpec((ATT_HEADS, tn), lambda j: (0, j)),
        out_shape=jax.ShapeDtypeStruct((ATT_HEADS, n), F32),
        compiler_params=_cparams(("parallel",)),
    )(rel_bias_t, onehot_t, maskadd)


def bias_table_bwd(dbias, onehot_t):
    n = onehot_t.shape[1]
    tn = 8192

    def body(d_ref, o_ref, out_ref):
        @pl.when(pl.program_id(0) == 0)
        def _():
            out_ref[...] = jnp.zeros_like(out_ref)

        out_ref[...] += _dot_hi_nt(d_ref[...], o_ref[...])

    return pl.pallas_call(
        body, name="bias_table_bwd", grid=(n // tn,),
        in_specs=[pl.BlockSpec((ATT_HEADS, tn), lambda j: (0, j)), pl.BlockSpec((REL_BUCKETS, tn), lambda j: (0, j))],
        out_specs=_full((ATT_HEADS, REL_BUCKETS)),
        out_shape=jax.ShapeDtypeStruct((ATT_HEADS, REL_BUCKETS), F32),
        compiler_params=_cparams(("arbitrary",)),
    )(dbias, onehot_t)


def _fold(full, tri):
    return jnp.where(tri, full[BLK:2 * BLK], full[0:BLK])


def _unfold(folded, tri):
    return jnp.concatenate([jnp.where(tri, 0.0, folded), jnp.where(tri, folded, 0.0)], axis=0)


GROUP_HEADS = ATT_HEADS // 2
GROUP_LANES = GROUP_HEADS * BLK


def _att_group(qg, kcat, vt_cat, bias_g, sink_g, tri, no_prev):
    l = _fold(_dot(kcat, qg), tri) * (HEAD_DIM ** -0.5) + bias_g
    l = jnp.where(no_prev, NEG, l)
    m = jnp.maximum(jnp.max(l, axis=0, keepdims=True), sink_g)
    p = jnp.exp(l - m)
    es = jnp.exp(sink_g - m)
    inv = 1.0 / (jnp.sum(p, axis=0, keepdims=True) + es)
    p = p * inv
    pcat = _unfold(p, tri)
    return p, pcat, es * inv, _dot(vt_cat, pcat)


def _heads_to_lanes(ref, row0):
    return jnp.concatenate([ref[row0 + j * HEAD_DIM:row0 + (j + 1) * HEAD_DIM, :] for j in range(GROUP_HEADS)], axis=1)


def _lanes_to_heads(ref, row0, val):
    for j in range(GROUP_HEADS):
        ref[row0 + j * HEAD_DIM:row0 + (j + 1) * HEAD_DIM, :] = val[:, j * BLK:(j + 1) * BLK].astype(ref.dtype)


def _kv_cat(kvp, kvc, g):
    lo = g * HEAD_DIM
    kt_cat = jnp.concatenate([kvp[lo:lo + HEAD_DIM], kvc[lo:lo + HEAD_DIM]], axis=1)
    vt_cat = jnp.concatenate([kvp[128 + lo:128 + lo + HEAD_DIM], kvc[128 + lo:128 + lo + HEAD_DIM]], axis=1)
    return kt_cat, vt_cat


def _tri_masks(n):
    row = lax.broadcasted_iota(jnp.int32, (BLK, GROUP_LANES), 0)
    query = lax.broadcasted_iota(jnp.int32, (BLK, GROUP_LANES), 1) & (BLK - 1)
    tri = row <= query
    return tri, (n == 0) & jnp.logical_not(tri)


def _split_hosted(refs, n_in, n_out, n_scratch, ex):
    na = ex.na if ex is not None else 0
    o = 0
    parts = []
    for cnt in (n_in, na, n_out, na, n_scratch, 3 if ex is not None else 0):
        parts.append(refs[o:o + cnt])
        o += cnt
    own_in, ex_in, own_out, ex_out, own_scr, ex_sems = parts
    return (own_in, own_out, own_scr), (ex_in, ex_out, ex_sems)


def _call_hosting(body, name, nsteps, in_specs, out_specs, out_shape, scratch, args, ex):
    n_in, n_out, n_scr = len(in_specs), len(out_specs), len(scratch)
    hosted = ex is not None

    def full_body(*refs):
        (own_in, own_out, own_scr), hosted_refs = _split_hosted(refs, n_in, n_out, n_scr, ex)
        if hosted:
            @pl.when(pl.program_id(0) == 0)
            def _():
                ex.start(*hosted_refs)

            @pl.when(pl.program_id(0) == max(nsteps - 4, 0))
            def _():
                ex.relay(*hosted_refs)

            @pl.when(pl.program_id(0) == nsteps - 1)
            def _():
                ex.wait(*hosted_refs)

        body(*own_in, *own_out, *own_scr)

    res = pl.pallas_call(
        full_body, name=name + "_hosting" if hosted else name, grid=(nsteps,),
        in_specs=list(in_specs) + (ex.in_specs if hosted else []),
        out_specs=list(out_specs) + (ex.out_specs if hosted else []),
        out_shape=list(out_shape) + (ex.out_shape if hosted else []),
        scratch_shapes=list(scratch) + (ex.scratch if hosted else []),
        compiler_params=_cparams(("arbitrary",)),
    )(*args, *(ex.arrays if hosted else []))
    return res[:n_out], res[n_out:]


def attn_fwd(pa, bias, sinks, ex=None):
    s = pa.shape[1]
    nb = s // BLK

    def body(pa_ref, kvp_ref, bias_ref, sink_ref, y_ref):
        n = pl.program_id(0)
        kvc = pa_ref[2048:2304, :]
        kvp = kvp_ref[...]
        tri, no_prev = _tri_masks(n)
        for g in range(2):
            kt_cat, vt_cat = _kv_cat(kvp, kvc, g)
            row0 = g * GROUP_HEADS * HEAD_DIM
            _, _, _, o = _att_group(_heads_to_lanes(pa_ref, row0), kt_cat.astype(F32).T, vt_cat, bias_ref[g],
                                    sink_ref[g:g + 1, :], tri, no_prev)
            z = _heads_to_lanes(pa_ref, 1024 + row0).astype(F32)
            _lanes_to_heads(y_ref, row0, o * z * _sig(z))

    (y,), hosted = _call_hosting(
        body, "attn_fwd", nb,
        in_specs=[pl.BlockSpec((ATT_COLS, BLK), lambda n: (0, n)),
                  pl.BlockSpec((256, BLK), lambda n: (8, jnp.maximum(n - 1, 0))),
                  _full((2, BLK, GROUP_LANES)), _full((2, GROUP_LANES))],
        out_specs=[pl.BlockSpec((1024, BLK), lambda n: (0, n))],
        out_shape=[jax.ShapeDtypeStruct((1024, s), ACT_DTYPE)], scratch=[],
        args=(pa, pa, bias, sinks), ex=ex)
    return (y, hosted) if ex is not None else y


def attn_bwd(pa, dy, bias, sinks, dbias_in):
    s = pa.shape[1]
    nb = s // BLK

    def body(pa_ref, kvp_ref, dy_ref, bias_ref, sink_ref, dbin_ref, dpa_ref, dbias_ref, dsink_ref, carry, dsink_acc):
        i = pl.program_id(0)
        n = nb - 1 - i

        @pl.when(i == 0)
        def _():
            dbias_ref[...] = dbin_ref[...]
            dsink_acc[...] = jnp.zeros_like(dsink_acc)
            carry[...] = jnp.zeros_like(carry)

        kvc = pa_ref[2048:2304, :]
        kvp = kvp_ref[...]
        tri, no_prev = _tri_masks(n)
        scale = HEAD_DIM ** -0.5
        for g in range(2):
            kt_cat, vt_cat = _kv_cat(kvp, kvc, g)
            row0 = g * GROUP_HEADS * HEAD_DIM
            qg = _heads_to_lanes(pa_ref, row0)
            p, pcat, psink, o = _att_group(qg, kt_cat.astype(F32).T, vt_cat, bias_ref[g], sink_ref[g:g + 1, :], tri,
                                           no_prev)
            z = _heads_to_lanes(pa_ref, 1024 + row0).astype(F32)
            dyg = _heads_to_lanes(dy_ref, row0).astype(F32)
            sz = _sig(z)
            d_o = dyg * z * sz
            _lanes_to_heads(dpa_ref, 1024 + row0, dyg * _dsilu(z, sz) * o)
            delta = jnp.sum(d_o * o, axis=0, keepdims=True)
            dl = p * (_fold(_dot(vt_cat.astype(F32).T, d_o), tri) - delta)
            dsink_acc[g:g + 1, :] += psink * delta
            dbias_ref[g] += dl
            dlcat = _unfold(dl, tri)
            _lanes_to_heads(dpa_ref, row0, _dot(kt_cat, dlcat) * scale)
            for q, dkv in enumerate((_dot_nt(qg, dlcat) * scale, _dot_nt(d_o, pcat))):
                r0 = q * 128 + g * HEAD_DIM
                dpa_ref[2048 + r0:2048 + r0 + HEAD_DIM, :] = (
                    dkv[:, BLK:2 * BLK] + carry[r0:r0 + HEAD_DIM, :]).astype(dpa_ref.dtype)
                carry[r0:r0 + HEAD_DIM, :] = dkv[:, 0:BLK]

        @pl.when(i == nb - 1)
        def _():
            lane = lax.broadcasted_iota(jnp.int32, (1, 128), 1)
            dsink = jnp.zeros((1, 128), F32)
            for h in range(ATT_HEADS):
                g, j = divmod(h, GROUP_HEADS)
                tot = jnp.sum(dsink_acc[g:g + 1, j * BLK:(j + 1) * BLK], axis=1, keepdims=True)
                dsink = dsink + jnp.where(lane == h, -tot, 0.0)
            dsink_ref[...] = dsink

    return pl.pallas_call(
        body, name="attn_bwd", grid=(nb,),
        in_specs=[pl.BlockSpec((ATT_COLS, BLK), lambda i: (0, nb - 1 - i)),
                  pl.BlockSpec((256, BLK), lambda i: (8, jnp.maximum(nb - 2 - i, 0))),
                  pl.BlockSpec((1024, BLK), lambda i: (0, nb - 1 - i)),
                  _full((2, BLK, GROUP_LANES)), _full((2, GROUP_LANES)), _full((2, BLK, GROUP_LANES))],
        out_specs=[pl.BlockSpec((ATT_COLS, BLK), lambda i: (0, nb - 1 - i)),
                   _full((2, BLK, GROUP_LANES)), _full((1, 128))],
        out_shape=[jax.ShapeDtypeStruct((ATT_COLS, s), ACT_DTYPE),
                   jax.ShapeDtypeStruct((2, BLK, GROUP_LANES), F32),
                   jax.ShapeDtypeStruct((1, 128), F32)],
        scratch_shapes=[pltpu.VMEM((256, BLK), F32), pltpu.VMEM((2, GROUP_LANES), F32)],
        compiler_params=_cparams(("arbitrary",)),
    )(pa, pa, dy, bias, sinks, dbias_in)


def _layernorm(v, g, b):
    mu = jnp.mean(v, axis=-1, keepdims=True)
    vc = v - mu
    rstd = lax.rsqrt(jnp.mean(vc * vc, axis=-1, keepdims=True) + EPS)
    xhat = vc * rstd
    return xhat, rstd, xhat * g + b


def sgu_fwd(ps, ln_g, ln_b, w_tril, b_t):
    s = ps.shape[0]

    def body(ps_ref, g_ref, b_ref, w_ref, bt_ref, y_ref):
        u = ps_ref[:, 0:1024].astype(F32)
        v = ps_ref[:, 1024:2048].astype(F32)
        z = ps_ref[:, 2048:3072].astype(F32)
        _, _, vn = _layernorm(v, g_ref[...], b_ref[...])
        gate = u * z * _sig(z)
        for g in range(SG_GROUPS):
            sl = slice(g * 128, (g + 1) * 128)
            mixed = _dot(w_ref[g], vn[:, sl]) + bt_ref[:, g:g + 1]
            y_ref[:, sl] = (gate[:, sl] * mixed).astype(y_ref.dtype)

    return pl.pallas_call(
        body, name="sgu_fwd", grid=(s // BLK,),
        in_specs=[pl.BlockSpec((BLK, SG_COLS), lambda c: (c, 0)), _full((1, 1024)), _full((1, 1024)),
                  _full((SG_GROUPS, BLK, BLK)), _full((BLK, 128))],
        out_specs=pl.BlockSpec((BLK, 1024), lambda c: (c, 0)),
        out_shape=jax.ShapeDtypeStruct((s, 1024), ACT_DTYPE),
        compiler_params=_cparams(("parallel",)),
    )(ps, ln_g, ln_b, w_tril, b_t)


def sgu_bwd(ps, dy, ln_g, ln_b, w_tril, w_tril_t, b_t):
    s = ps.shape[0]

    def body(ps_ref, dy_ref, g_ref, b_ref, w_ref, wt_ref, bt_ref, dps_ref, dw_ref, dbt_ref, dg_ref, db_ref, dvn_scr):
        @pl.when(pl.program_id(0) == 0)
        def _():
            dw_ref[...] = jnp.zeros_like(dw_ref)
            dbt_ref[...] = jnp.zeros_like(dbt_ref)
            dg_ref[...] = jnp.zeros_like(dg_ref)
            db_ref[...] = jnp.zeros_like(db_ref)

        u = ps_ref[:, 0:1024].astype(F32)
        v = ps_ref[:, 1024:2048].astype(F32)
        z = ps_ref[:, 2048:3072].astype(F32)
        dy = dy_ref[...].astype(F32)
        xhat, rstd, vn = _layernorm(v, g_ref[...], b_ref[...])
        sz = _sig(z)
        silu = z * sz
        row = lax.broadcasted_iota(jnp.int32, (BLK, BLK), 0)
        colm = lax.broadcasted_iota(jnp.int32, (BLK, BLK), 1)
        tril = row >= colm
        dbt = jnp.zeros((BLK, 128), F32)
        for g in range(SG_GROUPS):
            sl = slice(g * 128, (g + 1) * 128)
            vng = vn[:, sl]
            mixed = _dot(w_ref[g], vng) + bt_ref[:, g:g + 1]
            dyg, ug = dy[:, sl], u[:, sl]
            dps_ref[:, sl] = (dyg * mixed * silu[:, sl]).astype(dps_ref.dtype)
            dps_ref[:, 2048 + g * 128:2048 + (g + 1) * 128] = (
                dyg * ug * mixed * _dsilu(z[:, sl], sz[:, sl])).astype(dps_ref.dtype)
            dm = dyg * ug * silu[:, sl]
            dw_ref[g] += jnp.where(tril, _dot_nt(dm, vng), 0.0)
            dbt = dbt + jnp.where(colm == g, jnp.sum(dm, axis=1, keepdims=True), 0.0)
            dvn_scr[:, sl] = _dot(wt_ref[g], dm)
        dbt_ref[...] += dbt
        dvn = dvn_scr[...]
        dg_ref[...] += jnp.sum(dvn * xhat, axis=0, keepdims=True)
        db_ref[...] += jnp.sum(dvn, axis=0, keepdims=True)
        dxh = dvn * g_ref[...]
        dv = rstd * (dxh - jnp.mean(dxh, axis=-1, keepdims=True)
                     - xhat * jnp.mean(dxh * xhat, axis=-1, keepdims=True))
        dps_ref[:, 1024:2048] = dv.astype(dps_ref.dtype)

    return pl.pallas_call(
        body, name="sgu_bwd", grid=(s // BLK,),
        in_specs=[pl.BlockSpec((BLK, SG_COLS), lambda c: (c, 0)), pl.BlockSpec((BLK, 1024), lambda c: (c, 0)),
                  _full((1, 1024)), _full((1, 1024)), _full((SG_GROUPS, BLK, BLK)), _full((SG_GROUPS, BLK, BLK)),
                  _full((BLK, 128))],
        out_specs=[pl.BlockSpec((BLK, SG_COLS), lambda c: (c, 0)), _full((SG_GROUPS, BLK, BLK)), _full((BLK, 128)),
                   _full((1, 1024)), _full((1, 1024))],
        out_shape=[jax.ShapeDtypeStruct((s, SG_COLS), ACT_DTYPE), jax.ShapeDtypeStruct((SG_GROUPS, BLK, BLK), F32),
                   jax.ShapeDtypeStruct((BLK, 128), F32), jax.ShapeDtypeStruct((1, 1024), F32),
                   jax.ShapeDtypeStruct((1, 1024), F32)],
        scratch_shapes=[pltpu.VMEM((BLK, 1024), F32)],
        compiler_params=_cparams(("arbitrary",)),
    )(ps, dy, ln_g, ln_b, w_tril, w_tril_t, b_t)


def _shift_down(cur, prev16, k):
    if k == 0:
        return cur
    r = pltpu.roll(cur, k, 0)
    rp = pltpu.roll(prev16, k, 0)
    row = lax.broadcasted_iota(jnp.int32, (8, cur.shape[1]), 0)
    return jnp.concatenate([jnp.where(row < k, rp[0:8], r[0:8]), r[8:]], axis=0)


def _shift_up(cur, next16, k):
    if k == 0:
        return cur
    n = cur.shape[0]
    r = pltpu.roll(cur, n - k, 0)
    rn = pltpu.roll(next16, 16 - k, 0)
    row = lax.broadcasted_iota(jnp.int32, (8, cur.shape[1]), 0)
    return jnp.concatenate([r[:n - 8], jnp.where(row >= 8 - k, rn[8:16], r[n - 8:])], axis=0)


def _bcast8(v):
    return jnp.broadcast_to(v, (16, v.shape[1]))


def _causal_conv(xbc, prev16, cw, cbias):
    pre = cbias + cw[3:4] * xbc
    for k in (1, 2, 3):
        pre = pre + cw[3 - k:4 - k] * _shift_down(xbc, prev16, k)
    return pre


class _Ssd:
    def __init__(self, pre, dtr, dtb, alog, dsk, tri, e):
        self.pre = pre
        self.sg = _sig(pre)
        act = pre * self.sg
        self.xs = act[:, 0:SSM_WIDTH]
        self.bm = act[:, SSM_WIDTH:SSM_WIDTH + 512]
        self.cm = act[:, SSM_WIDTH + 512:CONV_DIM]
        self.dtp = dtr + dtb
        self.dt = jnp.maximum(self.dtp, 0.0) + jnp.log(1.0 + jnp.exp(-jnp.abs(self.dtp)))
        self.a = -jnp.exp(alog)
        self.acs = _dot_hi(tri, self.dt * self.a)
        self.acs_t = self.acs.T
        tot = self.acs[BLK - 1:BLK]
        self.ecs = jnp.exp(self.acs)
        self.dte = jnp.exp(tot - self.acs)
        self.cd = jnp.exp(tot)
        self.dt_x = _dot_onehot(self.dt, e)
        self.ecs_x = _dot_onehot(self.ecs, e)
        self.dte_x = _dot_onehot(self.dte, e)
        self.cd_x = _dot_onehot(_bcast8(self.cd), e)[0:1]
        self.d_x = _dot_onehot(_bcast8(dsk), e)[0:1]
        self.xdt = self.xs * self.dt_x
        row = lax.broadcasted_iota(jnp.int32, (BLK, BLK), 0)
        col = lax.broadcasted_iota(jnp.int32, (BLK, BLK), 1)
        self.tril = row >= col

    def group(self, g):
        sl = slice(g * 128, (g + 1) * 128)
        bg, cg = self.bm[:, sl], self.cm[:, sl]
        return bg, cg, _dot_nt(cg, bg)

    def decay(self, h):
        seg = self.acs[:, h:h + 1] - self.acs_t[h:h + 1, :]
        return jnp.exp(jnp.where(self.tril, seg, NEG))

    def y_pre_gate(self, ht_of, yd_scr, yoff_scr):
        for g in range(SSM_GROUPS):
            bg, cg, cb = self.group(g)
            for j in range(8):
                h = g * 8 + j
                sl = slice(h * 64, (h + 1) * 64)
                yd_scr[:, sl] = _dot(cb * self.decay(h), self.xdt[:, sl])
            gs = slice(g * SSM_GW, (g + 1) * SSM_GW)
            yoff_scr[:, gs] = _dot(cg, ht_of(g)) * self.ecs_x[:, gs]
        return yd_scr[...] + yoff_scr[...] + self.d_x * self.xs


def _ssd_consts():
    hh = lax.broadcasted_iota(jnp.int32, (128, SSM_WIDTH), 0)
    ch = lax.broadcasted_iota(jnp.int32, (128, SSM_WIDTH), 1)
    e = (ch // 64 == hh).astype(jnp.bfloat16)
    row = lax.broadcasted_iota(jnp.int32, (BLK, BLK), 0)
    col = lax.broadcasted_iota(jnp.int32, (BLK, BLK), 1)
    tri = (row >= col).astype(F32)
    return tri, e


def _pad_lanes(v, n=128):
    return jnp.pad(v, ((0, 0), (0, n - v.shape[1])))


def ssd_fwd(pm, cw, cbias, dtb, alog, dsk, ng, ex=None):
    s = pm.shape[0]
    nc = s // BLK
    tri, e = _ssd_consts()

    def body(pm_ref, prev_ref, cw_ref, cb_ref, dtb_ref, al_ref, d_ref, ng_ref, tri_ref, e_ref,
             y_ref, st_ref, pre_ref, ht_ref, yd_scr, yoff_scr):
        c = pl.program_id(0)

        @pl.when(c == 0)
        def _():
            ht_ref[...] = jnp.zeros_like(ht_ref)

        xbc = pm_ref[:, 0:CONV_DIM].astype(F32)
        prev16 = jnp.where(c == 0, 0.0, prev_ref[...].astype(F32))
        pre = _causal_conv(xbc, prev16, cw_ref[...], cb_ref[...])
        pre_ref[...] = pre.astype(pre_ref.dtype)
        f = _Ssd(pre, pm_ref[:, DT_OFF:DT_OFF + 128].astype(F32), dtb_ref[...], al_ref[...], d_ref[...],
                 tri_ref[...], e_ref[...])
        st_ref[0] = ht_ref[...]
        y = f.y_pre_gate(lambda g: ht_ref[g], yd_scr, yoff_scr)
        for g in range(SSM_GROUPS):
            bg, _, _ = f.group(g)
            gs = slice(g * SSM_GW, (g + 1) * SSM_GW)
            ht_ref[g] = ht_ref[g] * f.cd_x[:, gs] + _dot_tn(bg, f.xdt[:, gs] * f.dte_x[:, gs])
        z = pm_ref[:, CONV_DIM:CONV_DIM + SSM_WIDTH].astype(F32)
        ypre = y * z * _sig(z)
        for g in range(SSM_GROUPS):
            gs = slice(g * SSM_GW, (g + 1) * SSM_GW)
            yg = ypre[:, gs]
            rr = lax.rsqrt(jnp.mean(yg * yg, axis=-1, keepdims=True) + EPS)
            y_ref[:, gs] = (yg * rr * ng_ref[:, gs]).astype(y_ref.dtype)

    own, hosted = _call_hosting(
        body, "ssd_fwd", nc,
        in_specs=[pl.BlockSpec((BLK, SSM_COLS), lambda c: (c, 0)),
                  pl.BlockSpec((16, CONV_DIM), lambda c: (jnp.maximum(8 * c - 1, 0), 0)),
                  _full((4, CONV_DIM)), _full((1, CONV_DIM)), _full((1, 128)), _full((1, 128)), _full((1, 128)),
                  _full((1, SSM_WIDTH)), _full((BLK, BLK)), _full((128, SSM_WIDTH))],
        out_specs=[pl.BlockSpec((BLK, SSM_WIDTH), lambda c: (c, 0)),
                   pl.BlockSpec((1, SSM_GROUPS, 128, SSM_GW), lambda c: (c, 0, 0, 0)),
                   pl.BlockSpec((BLK, CONV_DIM), lambda c: (c, 0))],
        out_shape=[jax.ShapeDtypeStruct((s, SSM_WIDTH), ACT_DTYPE),
                   jax.ShapeDtypeStruct((nc, SSM_GROUPS, 128, SSM_GW), F32),
                   jax.ShapeDtypeStruct((s, CONV_DIM), ACT_DTYPE)],
        scratch=[pltpu.VMEM((SSM_GROUPS, 128, SSM_GW), F32), pltpu.VMEM((BLK, SSM_WIDTH), F32),
                 pltpu.VMEM((BLK, SSM_WIDTH), F32)],
        args=(pm, pm, cw, cbias, dtb, alog, dsk, ng, tri, e), ex=ex)
    return (*own, hosted) if ex is not None else tuple(own)


def ssd_bwd(pm, pre, dy, states, cw, dtb, alog, dsk, ng, ex=None):
    s = pm.shape[0]
    nc = s // BLK
    tri, e = _ssd_consts()
    tri_t, e_t = tri.T, e.T

    def body(pm_ref, pre_ref, dy_ref, st_ref, cw_ref, dtb_ref, al_ref, d_ref, ng_ref,
             tri_ref, trit_ref, e_ref, et_ref,
             dpm_ref, dcw_ref, dcb_ref, dvec_ref, dng_ref,
             dht_ref, dcar_ref, yd_scr, yoff_scr, dx_scr, r2_scr, hs_scr, da_scr, dat_scr, dd_scr, dbc_scr):
        i = pl.program_id(0)
        n = nc - 1 - i

        @pl.when(i == 0)
        def _():
            dht_ref[...] = jnp.zeros_like(dht_ref)
            dcar_ref[...] = jnp.zeros_like(dcar_ref)
            dcw_ref[...] = jnp.zeros_like(dcw_ref)
            dcb_ref[...] = jnp.zeros_like(dcb_ref)
            dvec_ref[...] = jnp.zeros_like(dvec_ref)
            dng_ref[...] = jnp.zeros_like(dng_ref)
            dd_scr[...] = jnp.zeros_like(dd_scr)
            da_scr[...] = jnp.zeros_like(da_scr)
            dat_scr[...] = jnp.zeros_like(dat_scr)

        cw = cw_ref[...]
        f = _Ssd(pre_ref[...].astype(F32), pm_ref[:, DT_OFF:DT_OFF + 128].astype(F32), dtb_ref[...], al_ref[...],
                 d_ref[...], tri_ref[...], e_ref[...])
        et = et_ref[...]
        y = f.y_pre_gate(lambda g: st_ref[0, g], yd_scr, yoff_scr)

        z = pm_ref[:, CONV_DIM:CONV_DIM + SSM_WIDTH].astype(F32)
        dyv = dy_ref[...].astype(F32)
        sz = _sig(z)
        silu = z * sz
        ypre = y * silu
        for g in range(SSM_GROUPS):
            gs = slice(g * SSM_GW, (g + 1) * SSM_GW)
            yg = ypre[:, gs]
            rr = lax.rsqrt(jnp.mean(yg * yg, axis=-1, keepdims=True) + EPS)
            nrm = yg * rr
            dng_ref[:, gs] += jnp.sum(dyv[:, gs] * nrm, axis=0, keepdims=True)
            dn = dyv[:, gs] * ng_ref[:, gs]
            dx_scr[:, gs] = rr * (dn - nrm * jnp.mean(dn * nrm, axis=-1, keepdims=True))
        dypre = dx_scr[...]
        d_y = dypre * silu
        dpm_ref[:, CONV_DIM:CONV_DIM + SSM_WIDTH] = (dypre * y * _dsilu(z, sz)).astype(dpm_ref.dtype)

        for g in range(SSM_GROUPS):
            bg, cg, cb = f.group(g)
            gs = slice(g * SSM_GW, (g + 1) * SSM_GW)
            htg = st_ref[0, g]
            dhn = dht_ref[g]
            dcb = jnp.zeros((BLK, BLK), F32)
            for j in range(8):
                h = g * 8 + j
                sl = slice(h * 64, (h + 1) * 64)
                dec = f.decay(h)
                dyh = d_y[:, sl]
                dmd = _dot_nt(dyh, f.xdt[:, sl]) * dec
                dcb = dcb + dmd
                gm = dmd * cb
                da_scr[:, h:h + 1] = jnp.sum(gm, axis=1, keepdims=True)
                dat_scr[h:h + 1, :] = jnp.sum(gm, axis=0, keepdims=True)
                dx_scr[:, sl] = _dot_tn(cb * dec, dyh)
            dz = f.ecs_x[:, gs] * d_y[:, gs]
            dbc_scr[:, 512 + g * 128:512 + (g + 1) * 128] = _dot(dcb, bg) + _dot_nt(dz, htg)
            dbc_scr[:, g * 128:(g + 1) * 128] = _dot_tn(dcb, cg) + _dot_nt(f.xdt[:, gs] * f.dte_x[:, gs], dhn)
            dws = _dot(bg, dhn)
            dx_scr[:, gs] += f.dte_x[:, gs] * dws
            r2_scr[:, gs] = dws * f.xdt[:, gs]
            hs_scr[:, gs] = _bcast8(jnp.sum(dhn * htg, axis=0, keepdims=True))
            dht_ref[g] = f.cd_x[:, gs] * dhn + _dot_tn(cg, dz)
        d_x = dx_scr[...]
        r1 = _dot(d_y * yoff_scr[...], et)
        r2 = _dot(r2_scr[...], et) * f.dte
        dcd = _dot_onehot(hs_scr[...], et)[0:1]
        d_tot = jnp.sum(r2, axis=0, keepdims=True) + dcd * f.cd
        row = lax.broadcasted_iota(jnp.int32, (BLK, 128), 0)
        d_a = da_scr[...] - dat_scr[...].T + r1 - r2 + jnp.where(row == BLK - 1, d_tot, 0.0)
        dadt = _dot_hi(trit_ref[...], d_a)
        ddt = dadt * f.a + _dot(d_x * f.xs, et)
        lane = lax.broadcasted_iota(jnp.int32, (BLK, 128), 1)
        dr = jnp.where(lane < SSM_HEADS, ddt * _sig(f.dtp), 0.0)
        dvec_ref[0:1, :] += jnp.sum(dr, axis=0, keepdims=True)
        dvec_ref[1:2, :] += jnp.sum(dadt * f.dt, axis=0, keepdims=True) * f.a
        dd_scr[...] += _bcast8(jnp.sum(d_y * f.xs, axis=0, keepdims=True))
        dpm_ref[:, DT_OFF:DT_OFF + 128] = dr.astype(dpm_ref.dtype)
        dpm_ref[:, DT_OFF + 128:SSM_COLS] = jnp.zeros((BLK, 128), dpm_ref.dtype)

        dxs = d_x * f.dt_x + f.d_x * d_y
        dact = jnp.concatenate([dxs, dbc_scr[...]], axis=1)
        dpre = dact * _dsilu(f.pre, f.sg)
        dcb_ref[...] += jnp.sum(dpre, axis=0, keepdims=True)
        xbc = pm_ref[:, 0:CONV_DIM].astype(F32)
        dxraw = jnp.zeros((BLK, CONV_DIM), F32)
        nxt = dcar_ref[...]
        for k in range(4):
            ahead = _shift_up(dpre, nxt, k)
            dcw_ref[3 - k:4 - k, :] += jnp.sum(ahead * xbc, axis=0, keepdims=True)
            dxraw = dxraw + cw[3 - k:4 - k] * ahead
        dcar_ref[...] = dpre[0:16]
        dpm_ref[:, 0:CONV_DIM] = dxraw.astype(dpm_ref.dtype)

        @pl.when(i == nc - 1)
        def _():
            dvec_ref[2:3, :] = _dot_onehot(dd_scr[...], et)[0:1]

    own, hosted = _call_hosting(
        body, "ssd_bwd", nc,
        in_specs=[pl.BlockSpec((BLK, SSM_COLS), lambda i: (nc - 1 - i, 0)),
                  pl.BlockSpec((BLK, CONV_DIM), lambda i: (nc - 1 - i, 0)),
                  pl.BlockSpec((BLK, SSM_WIDTH), lambda i: (nc - 1 - i, 0)),
                  pl.BlockSpec((1, SSM_GROUPS, 128, SSM_GW), lambda i: (nc - 1 - i, 0, 0, 0)),
                  _full((4, CONV_DIM)), _full((1, 128)), _full((1, 128)), _full((1, 128)),
                  _full((1, SSM_WIDTH)), _full((BLK, BLK)), _full((BLK, BLK)), _full((128, SSM_WIDTH)),
                  _full((SSM_WIDTH, 128))],
        out_specs=[pl.BlockSpec((BLK, SSM_COLS), lambda i: (nc - 1 - i, 0)),
                   _full((8, CONV_DIM)), _full((1, CONV_DIM)), _full((8, 128)), _full((1, SSM_WIDTH))],
        out_shape=[jax.ShapeDtypeStruct((s, SSM_COLS), ACT_DTYPE), jax.ShapeDtypeStruct((8, CONV_DIM), F32),
                   jax.ShapeDtypeStruct((1, CONV_DIM), F32), jax.ShapeDtypeStruct((8, 128), F32),
                   jax.ShapeDtypeStruct((1, SSM_WIDTH), F32)],
        scratch=[pltpu.VMEM((SSM_GROUPS, 128, SSM_GW), F32), pltpu.VMEM((16, CONV_DIM), F32),
                 pltpu.VMEM((BLK, SSM_WIDTH), F32), pltpu.VMEM((BLK, SSM_WIDTH), F32),
                 pltpu.VMEM((BLK, SSM_WIDTH), F32), pltpu.VMEM((BLK, SSM_WIDTH), F32),
                 pltpu.VMEM((16, SSM_WIDTH), F32), pltpu.VMEM((BLK, 128), F32), pltpu.VMEM((128, BLK), F32),
                 pltpu.VMEM((16, SSM_WIDTH), F32), pltpu.VMEM((BLK, 1024), F32)],
        args=(pm, pre, dy, states, cw, dtb, alog, dsk, ng, tri, tri_t, e, e_t), ex=ex)
    return (*own, hosted) if ex is not None else tuple(own)


def merge_fwd(x, ya, ys, ym, pg, wa, ws, wm, wo, g_post):
    s, d = x.shape
    tm = min(256, s)

    def body(x_ref, ya_ref, ys_ref, ym_ref, pg_ref, wa_ref, ws_ref, wm_ref, wo_ref, g_ref,
             xo_ref, ba_ref, bs_ref, bm_ref, mg_ref, out_ref):
        ba = _dot_tn(ya_ref[...], wa_ref[...])
        bs = _dot(ys_ref[...], ws_ref[...])
        bm = _dot(ym_ref[...], wm_ref[...])
        merged = (_sig(pg_ref[:, 0:d].astype(F32)) * ba + _sig(pg_ref[:, d:2 * d].astype(F32)) * bs
                  + _sig(pg_ref[:, 2 * d:3 * d].astype(F32)) * bm)
        out = _dot(merged, wo_ref[...])
        r = lax.rsqrt(jnp.mean(out * out, axis=-1, keepdims=True) + EPS)
        xo_ref[...] = x_ref[...] + out * r * g_ref[...]
        ba_ref[...] = ba.astype(ba_ref.dtype)
        bs_ref[...] = bs.astype(bs_ref.dtype)
        bm_ref[...] = bm.astype(bm_ref.dtype)
        mg_ref[...] = merged.astype(mg_ref.dtype)
        out_ref[...] = out.astype(out_ref.dtype)

    rows = lambda w: pl.BlockSpec((tm, w), lambda i: (i, 0))
    act = jax.ShapeDtypeStruct((s, d), ACT_DTYPE)
    return pl.pallas_call(
        body, name="merge_fwd", grid=(s // tm,),
        in_specs=[rows(d), pl.BlockSpec((d, tm), lambda i: (0, i)), rows(d), rows(2 * d), rows(3 * d), _full((d, d)),
                  _full((d, d)), _full((2 * d, d)), _full((d, d)), _full((1, d))],
        out_specs=[rows(d)] * 6,
        out_shape=[jax.ShapeDtypeStruct((s, d), F32), act, act, act, act, act],
        compiler_params=_cparams(("parallel",)),
    )(x, ya, ys, ym, pg, wa, ws, wm, wo, g_post)


def merge_bwd(dx, out_s, pg, ba, bs, bm, wa, ws, wm, wo, g_post):
    s, d = dx.shape
    tm = min(256, s)

    def body(dx_ref, out_ref, pg_ref, ba_ref, bs_ref, bm_ref, wa_ref, ws_ref, wm_ref, wo_ref, g_ref,
             dout_ref, dba_ref, dbs_ref, dbm_ref, dpg_ref, dya_ref, dys_ref, dym_ref, dg_ref):
        @pl.when(pl.program_id(0) == 0)
        def _():
            dg_ref[...] = jnp.zeros_like(dg_ref)

        o = out_ref[...].astype(F32)
        dxv = dx_ref[...]
        r = lax.rsqrt(jnp.mean(o * o, axis=-1, keepdims=True) + EPS)
        nrm = o * r
        dg_ref[...] += jnp.sum(dxv * nrm, axis=0, keepdims=True)
        dn = dxv * g_ref[...]
        dout = r * (dn - nrm * jnp.mean(dn * nrm, axis=-1, keepdims=True))
        dout_ref[...] = dout.astype(dout_ref.dtype)
        dmerged = _dot_nt(dout, wo_ref[...])
        for q, (b_ref, db_ref, w_ref, dy_ref) in enumerate(((ba_ref, dba_ref, wa_ref, dya_ref),
                                                            (bs_ref, dbs_ref, ws_ref, dys_ref),
                                                            (bm_ref, dbm_ref, wm_ref, dym_ref))):
            gt = _sig(pg_ref[:, q * d:(q + 1) * d].astype(F32))
            db = dmerged * gt
            db_ref[...] = db.astype(db_ref.dtype)
            dpg_ref[:, q * d:(q + 1) * d] = (dmerged * b_ref[...].astype(F32) * gt * (1.0 - gt)).astype(dpg_ref.dtype)
            if q == 0:
                dy_ref[...] = _dot_nt(w_ref[...], db).astype(dy_ref.dtype)
            else:
                dy_ref[...] = _dot_nt(db, w_ref[...]).astype(dy_ref.dtype)

    rows = lambda w: pl.BlockSpec((tm, w), lambda i: (i, 0))
    act = lambda w: jax.ShapeDtypeStruct((s, w), ACT_DTYPE)
    return pl.pallas_call(
        body, name="merge_bwd", grid=(s // tm,),
        in_specs=[rows(d), rows(d), rows(3 * d), rows(d), rows(d), rows(d), _full((d, d)), _full((d, d)),
                  _full((2 * d, d)), _full((d, d)), _full((1, d))],
        out_specs=[rows(d), rows(d), rows(d), rows(d), rows(3 * d), pl.BlockSpec((d, tm), lambda i: (0, i)), rows(d),
                   rows(2 * d), _full((1, d))],
        out_shape=[act(d), act(d), act(d), act(d), act(3 * d), jax.ShapeDtypeStruct((d, s), ACT_DTYPE), act(d),
                   act(2 * d), jax.ShapeDtypeStruct((1, d), F32)],
        compiler_params=_cparams(("arbitrary",)),
    )(dx, out_s, pg, ba, bs, bm, wa, ws, wm, wo, g_post)


def loss_grad(y, target):
    s, d = y.shape
    tm = min(512, s)

    def body(y_ref, t_ref, dy_ref, l_ref):
        @pl.when(pl.program_id(0) == 0)
        def _():
            l_ref[...] = jnp.zeros_like(l_ref)

        err = y_ref[...] - t_ref[...]
        dy_ref[...] = err * (1.0 / d)
        part = jnp.sum(jnp.sum(err * err, axis=-1, keepdims=True) * (1.0 / d), axis=0, keepdims=True)
        l_ref[...] += 0.5 * jnp.broadcast_to(part, l_ref.shape)

    return pl.pallas_call(
        body, name="loss_grad", grid=(s // tm,),
        in_specs=[pl.BlockSpec((tm, d), lambda i: (i, 0)), pl.BlockSpec((tm, d), lambda i: (i, 0))],
        out_specs=[pl.BlockSpec((tm, d), lambda i: (i, 0)), _full((8, 128))],
        out_shape=[jax.ShapeDtypeStruct((s, d), F32), jax.ShapeDtypeStruct((8, 128), F32)],
        compiler_params=_cparams(("arbitrary",)),
    )(y, target)


def _mesh_pos():
    x, y, c = lax.axis_index("x"), lax.axis_index("y"), lax.axis_index("c")
    return x, y, c, 4 * x + 2 * y + c


def _peer(x, y, c, k):
    px = 1 - x if k & 4 else x
    py = 1 - y if k & 2 else y
    pc = 1 - c if k & 1 else c
    return (px, py, pc), 4 * px + 2 * py + pc


class Exchange:
    SAME_CORE = (2, 4, 6)

    def __init__(self, scattered, gathered):
        self.ns = len(scattered)
        self.arrays = list(scattered) + list(gathered)
        self.na = len(self.arrays)
        any_spec = pl.BlockSpec(memory_space=pl.ANY)
        self.in_specs = [any_spec] * self.na
        self.out_specs = [any_spec] * self.na
        self.out_shape = ([jax.ShapeDtypeStruct(a.shape, a.dtype) for a in scattered]
                          + [jax.ShapeDtypeStruct((N_DEV,) + a.shape, a.dtype) for a in gathered])
        self.scratch = [pltpu.SemaphoreType.DMA((self.na, N_DEV - 1)), pltpu.SemaphoreType.DMA((self.na, N_DEV - 1)),
                        pltpu.SemaphoreType.DMA((self.na,))]

    def _src(self, ins, q, slot):
        return ins[q].at[slot] if q < self.ns else ins[q]

    def _local(self, ins, outs, sems):
        me = _mesh_pos()[3]
        return [pltpu.make_async_copy(self._src(ins, q, me), outs[q].at[me], sems[2].at[q]) for q in range(self.na)]

    def _direct(self, ins, outs, sems, relations, arrays):
        x, y, c, me = _mesh_pos()
        copies = []
        for k in relations:
            peer, pidx = _peer(x, y, c, k)
            for q in arrays:
                copies.append(pltpu.make_async_remote_copy(
                    src_ref=self._src(ins, q, pidx), dst_ref=outs[q].at[me], send_sem=sems[0].at[q, k - 1],
                    recv_sem=sems[1].at[q, k - 1], device_id=peer, device_id_type=MESH))
        return copies

    def _arrivals(self, ins, outs, sems, relations, arrays):
        x, y, c, _ = _mesh_pos()
        copies = []
        for k in relations:
            peer, pidx = _peer(x, y, c, k)
            for q in arrays:
                copies.append(pltpu.make_async_remote_copy(
                    src_ref=self._src(ins, q, pidx), dst_ref=outs[q].at[pidx], send_sem=sems[0].at[q, k - 1],
                    recv_sem=sems[1].at[q, k - 1], device_id=peer, device_id_type=MESH))
        return copies

    def _relays(self, outs, sems):
        x, y, c, _ = _mesh_pos()
        sibling, _ = _peer(x, y, c, 1)
        copies = []
        for k in self.SAME_CORE:
            _, pidx = _peer(x, y, c, k)
            for q in range(self.ns, self.na):
                copies.append(pltpu.make_async_remote_copy(
                    src_ref=outs[q].at[pidx], dst_ref=outs[q].at[pidx], send_sem=sems[0].at[q, k],
                    recv_sem=sems[1].at[q, k], device_id=sibling, device_id_type=MESH))
        return copies

    def _sends(self, ins, outs, sems):
        return (self._direct(ins, outs, sems, range(1, N_DEV), range(self.ns))
                + self._direct(ins, outs, sems, (1,) + self.SAME_CORE, range(self.ns, self.na)))

    def start(self, ins, outs, sems):
        for cp in self._local(ins, outs, sems) + self._sends(ins, outs, sems):
            cp.start()

    def relay(self, ins, outs, sems):
        for cp in self._arrivals(ins, outs, sems, self.SAME_CORE, range(self.ns, self.na)):
            cp.wait_recv()
        for cp in self._relays(outs, sems):
            cp.start()

    def wait(self, ins, outs, sems):
        for cp in (self._arrivals(ins, outs, sems, range(1, N_DEV), range(self.ns))
                   + self._arrivals(ins, outs, sems, (1, 3, 5, 7), range(self.ns, self.na))):
            cp.wait_recv()
        for cp in self._sends(ins, outs, sems) + self._relays(outs, sems):
            cp.wait_send()
        for cp in self._local(ins, outs, sems):
            cp.wait()


def exchange(scattered, gathered, name):
    ex = Exchange(scattered, gathered)

    def body(*refs):
        ins, outs, sems = refs[:ex.na], refs[ex.na:2 * ex.na], refs[2 * ex.na:]
        ex.start(ins, outs, sems)
        ex.relay(ins, outs, sems)
        ex.wait(ins, outs, sems)

    return pl.pallas_call(body, name=name, in_specs=ex.in_specs, out_specs=ex.out_specs, out_shape=ex.out_shape,
                          scratch_shapes=ex.scratch)(*ex.arrays)


def adamw(parts_list, w, m, v, tile, name, ex=None):
    npart, _, dp = parts_list[0].shape
    d = w.shape[-1]
    counts = [p.shape[1] // tile for p in parts_list]
    starts = [sum(counts[:q]) for q in range(len(counts))]
    n_lists = len(parts_list)

    def body(*refs):
        p_refs = refs[:n_lists]
        w_ref, m_ref, v_ref, g_ref, dw_ref, nm_ref, nv_ref = refs[n_lists:]
        i = pl.program_id(0)
        for q, p_ref in enumerate(p_refs):
            @pl.when((i >= starts[q]) & (i < starts[q] + counts[q]))
            def _(p_ref=p_ref):
                acc = p_ref[0, :, 0:d].astype(F32)
                for k in range(1, npart):
                    acc = acc + p_ref[k, :, 0:d].astype(F32)
                g_ref[...] = acc

        g = g_ref[...]
        nm = ADAM_B1 * m_ref[...] + (1.0 - ADAM_B1) * g
        nv = ADAM_B2 * v_ref[...] + (1.0 - ADAM_B2) * (g * g)
        nm_ref[...] = nm
        nv_ref[...] = nv
        m_hat = nm / (1.0 - ADAM_B1 ** ADAM_STEP)
        v_hat = nv / (1.0 - ADAM_B2 ** ADAM_STEP)
        dw_ref[...] = -ADAM_LR * (m_hat / (jnp.sqrt(v_hat) + ADAM_EPS) + ADAM_WD * w_ref[...])

    def part_rows(q):
        return lambda i: (0, jnp.clip(i - starts[q], 0, counts[q] - 1), 0)

    if w.ndim == 3:
        rows = pl.BlockSpec((None, tile, d), lambda i: (i // counts[0], i % counts[0], 0))
    else:
        rows = pl.BlockSpec((tile, d), lambda i: (i, 0))
    own, hosted = _call_hosting(
        body, name, sum(counts),
        in_specs=[pl.BlockSpec((npart, tile, dp), part_rows(q)) for q in range(n_lists)] + [rows, rows, rows],
        out_specs=[rows] * 4, out_shape=[jax.ShapeDtypeStruct(w.shape, F32)] * 4, scratch=[],
        args=(*parts_list, w, m, v), ex=ex)
    return (*own, hosted) if ex is not None else tuple(own)


def _pad_rows(a, rows):
    return jnp.pad(a, ((0, rows - a.shape[0]), (0, 0)))


def _pack_rest(w_att, w_sg, w_ssm, w_out):
    parts = []
    for l in range(2):
        parts += [w_att[l], w_sg[l], w_ssm[l], w_out[l]]
    return jnp.concatenate(parts, axis=0)


def _unpack_rest(p):
    outs = [[], [], [], []]
    o = 0
    for l in range(2):
        for q, rws in enumerate(REST_PARTS):
            outs[q].append(p[o:o + rws])
            o += rws
    return [jnp.stack(t) for t in outs]


def _pack_win(w_in):
    return jnp.pad(w_in.reshape(2 * D_MODEL, WIN_SHARD), ((0, 0), (0, WIN_LANES - WIN_SHARD)))


W_IN_MAP = ((0, 1024, "att", 0), (1024, 1280, "att", 2048), (1280, 2304, "att", 1024), (2304, 5376, "sg", 0),
            (5376, 7424, "ssm", 3072), (7424, 10496, "ssm", 0), (10496, 10528, "ssm", 5120), (10528, 13600, "gate", 0))
SLAB_COLS = {"att": ATT_COLS, "sg": SG_COLS, "ssm": SSM_COLS, "gate": GATE_COLS}


def _slabs_from_shards(g):
    slabs = {}
    for name, width in SLAB_COLS.items():
        pieces, filled = [], 0
        for ga, gb, _, off in sorted((m for m in W_IN_MAP if m[2] == name), key=lambda m: m[3]):
            assert off == filled
            a = ga
            while a < gb:
                d = a // WIN_SHARD
                hi = min(gb, WIN_SHARD * (d + 1))
                pieces.append(g[d, :, a - WIN_SHARD * d:hi - WIN_SHARD * d])
                a = hi
            filled += gb - ga
        if filled < width:
            pieces.append(jnp.zeros((D_MODEL, width - filled), g.dtype))
        slabs[name] = jnp.concatenate(pieces, axis=1)
    return slabs


def _shards_from_slabs(dslabs):
    out = []
    for d in range(N_DEV):
        a, b = WIN_SHARD * d, WIN_SHARD * (d + 1)
        pieces = []
        for ga, gb, name, off in W_IN_MAP:
            lo, hi = max(a, ga), min(b, gb)
            if lo < hi:
                pieces.append(dslabs[name][:, off + lo - ga:off + hi - ga])
        pieces.append(jnp.zeros((D_MODEL, WIN_LANES - WIN_SHARD), pieces[0].dtype))
        out.append(jnp.concatenate(pieces, axis=1).astype(WIRE_DTYPE))
    return jnp.stack(out)


SMALL_SIZES = (("norm_pre", 2048), ("norm_post", 2048), ("rel_bias", 512), ("att_sinks", 32), ("sg_ln_g", 2048),
               ("sg_ln_b", 2048), ("sg_w", 262144), ("sg_b", 2048), ("ssm_conv_b", 6144), ("ssm_dt_bias", 64),
               ("ssm_a_log", 64), ("ssm_d", 64), ("ssm_norm_g", 4096), ("conv_w_full", 24576))


def _pack_small(d):
    parts = []
    for name, size in SMALL_SIZES:
        rows = 8 * (-(-size // (8 * D_MODEL)))
        flat = d[name].reshape(-1) if name in d else jnp.zeros((size,), F32)
        parts.append(jnp.pad(flat, (0, rows * D_MODEL - size)).reshape(rows, D_MODEL))
    return _pad_rows(jnp.concatenate(parts, axis=0), SMALL_ROWS)


def _unpack_small(p, shapes):
    out, o = {}, 0
    for name, size in SMALL_SIZES:
        rows = 8 * (-(-size // (8 * D_MODEL)))
        if name in shapes:
            out[name] = p[o:o + rows].reshape(-1)[:size].reshape(shapes[name])
        o += rows
    return out


def _bucket_onehot_t():
    qi = jnp.arange(BLK, dtype=jnp.int32)[None, :]
    kj = jnp.arange(BLK, dtype=jnp.int32)[:, None]
    dd = (qi - kj) & (BLK - 1)
    in_window = dd >= 0
    max_exact = REL_BUCKETS // 2
    dist_f = jnp.maximum(dd, 1).astype(F32)
    large = max_exact + (jnp.log(dist_f / max_exact) / math.log(128 / max_exact)
                         * (REL_BUCKETS - max_exact)).astype(jnp.int32)
    large = jnp.minimum(large, REL_BUCKETS - 1)
    bucket = jnp.where(dd < max_exact, dd, large).reshape(1, -1)
    onehot_t = (bucket == jnp.arange(REL_BUCKETS, dtype=jnp.int32)[:, None]).astype(F32)
    maskadd = jnp.where(in_window, 0.0, NEG).astype(F32).reshape(1, -1)
    return onehot_t, maskadd


WEIGHTS = ['w_in', 'norm_pre', 'norm_post', 'rel_bias', 'att_sinks', 'sg_ln_g', 'sg_ln_b', 'sg_w', 'sg_b',
           'ssm_conv_w', 'ssm_conv_b', 'ssm_dt_bias', 'ssm_a_log', 'ssm_d', 'ssm_norm_g',
           'w_br_att', 'w_br_sg', 'w_br_ssm', 'w_out']
REST = ('w_br_att', 'w_br_sg', 'w_br_ssm', 'w_out')


def kernel(x, w_in, norm_pre, norm_post, rel_bias, att_sinks, sg_ln_g, sg_ln_b, sg_w, sg_b, ssm_conv_w, ssm_conv_b, ssm_dt_bias, ssm_a_log, ssm_d, ssm_norm_g, w_br_att, w_br_sg, w_br_ssm, w_out, loss_target, m_w_in, m_norm_pre, m_norm_post, m_rel_bias, m_att_sinks, m_sg_ln_g, m_sg_ln_b, m_sg_w, m_sg_b, m_ssm_conv_w, m_ssm_conv_b, m_ssm_dt_bias, m_ssm_a_log, m_ssm_d, m_ssm_norm_g, m_w_br_att, m_w_br_sg, m_w_br_ssm, m_w_out, v_w_in, v_norm_pre, v_norm_post, v_rel_bias, v_att_sinks, v_sg_ln_g, v_sg_ln_b, v_sg_w, v_sg_b, v_ssm_conv_w, v_ssm_conv_b, v_ssm_dt_bias, v_ssm_a_log, v_ssm_d, v_ssm_norm_g, v_w_br_att, v_w_br_sg, v_w_br_ssm, v_w_out):
    w = dict(w_in=w_in, norm_pre=norm_pre, norm_post=norm_post, rel_bias=rel_bias, att_sinks=att_sinks,
             sg_ln_g=sg_ln_g, sg_ln_b=sg_ln_b, sg_w=sg_w, sg_b=sg_b, ssm_conv_w=ssm_conv_w, ssm_conv_b=ssm_conv_b,
             ssm_dt_bias=ssm_dt_bias, ssm_a_log=ssm_a_log, ssm_d=ssm_d, ssm_norm_g=ssm_norm_g,
             w_br_att=w_br_att, w_br_sg=w_br_sg, w_br_ssm=w_br_ssm, w_out=w_out)
    mom = dict(w_in=m_w_in, norm_pre=m_norm_pre, norm_post=m_norm_post, rel_bias=m_rel_bias, att_sinks=m_att_sinks,
               sg_ln_g=m_sg_ln_g, sg_ln_b=m_sg_ln_b, sg_w=m_sg_w, sg_b=m_sg_b, ssm_conv_w=m_ssm_conv_w,
               ssm_conv_b=m_ssm_conv_b, ssm_dt_bias=m_ssm_dt_bias, ssm_a_log=m_ssm_a_log, ssm_d=m_ssm_d,
               ssm_norm_g=m_ssm_norm_g, w_br_att=m_w_br_att, w_br_sg=m_w_br_sg, w_br_ssm=m_w_br_ssm, w_out=m_w_out)
    var = dict(w_in=v_w_in, norm_pre=v_norm_pre, norm_post=v_norm_post, rel_bias=v_rel_bias, att_sinks=v_att_sinks,
               sg_ln_g=v_sg_ln_g, sg_ln_b=v_sg_ln_b, sg_w=v_sg_w, sg_b=v_sg_b, ssm_conv_w=v_ssm_conv_w,
               ssm_conv_b=v_ssm_conv_b, ssm_dt_bias=v_ssm_dt_bias, ssm_a_log=v_ssm_a_log, ssm_d=v_ssm_d,
               ssm_norm_g=v_ssm_norm_g, w_br_att=v_w_br_att, w_br_sg=v_w_br_sg, w_br_ssm=v_w_br_ssm, w_out=v_w_out)
    xs0 = x[0]
    target = loss_target[0]
    my_dev = 4 * lax.axis_index("x") + 2 * lax.axis_index("y") + lax.axis_index("c")

    conv_shard = _pad_rows(ssm_conv_w.reshape(-1, D_MODEL), 8)
    win_shard = _pack_win(w_in).astype(WIRE_DTYPE)
    rest_shard = _pack_rest(*[w[n] for n in REST]).astype(WIRE_DTYPE)
    layer_shards = [[win_shard[l * D_MODEL:(l + 1) * D_MODEL], rest_shard[l * LAYER_REST:(l + 1) * LAYER_REST]]
                    for l in range(2)]
    g_win0, gathered_conv = exchange([], [layer_shards[0][0], conv_shard], "all_gather")
    conv_full = gathered_conv[:, 0:3].reshape(N_DEV, 2, 4, 384).transpose(1, 2, 0, 3).reshape(2, 4, CONV_DIM)

    def set_rest(lw, g_rest):
        o = 0
        for name, rws in zip(("att", "sg", "ssm", "out"), REST_PARTS):
            lw[name] = g_rest[:, o:o + rws].reshape(N_DEV * rws, D_MODEL).astype(MXU_DTYPE)
            o += rws

    def layer_weights(l, g_win):
        slabs = _slabs_from_shards(g_win)
        lw = {"in_" + name: slab.astype(MXU_DTYPE) for name, slab in slabs.items()}
        lw["in_att"] = lw["in_att"].T
        tril = jnp.tril(jnp.ones((BLK, BLK), bool))
        sgw = jnp.where(tril[None], sg_w[l], 0.0)
        lw.update(
            g_pre=norm_pre[l][None], g_post=norm_post[l][None], sinks=jnp.repeat(att_sinks[l], BLK).reshape(2, GROUP_LANES),
            ln_g=sg_ln_g[l][None], ln_b=sg_ln_b[l][None], sgw=sgw.astype(MXU_DTYPE),
            sgw_t=sgw.transpose(0, 2, 1).astype(MXU_DTYPE), sgb_t=_pad_lanes(sg_b[l].T),
            cw=conv_full[l], cb=ssm_conv_b[l][None], dtb=_pad_lanes(ssm_dt_bias[l][None]),
            alog=_pad_lanes(ssm_a_log[l][None]), dsk=_pad_lanes(ssm_d[l][None]), ng=ssm_norm_g[l][None])
        return lw

    onehot_t, maskadd = _bucket_onehot_t()
    bias = bias_table(rel_bias.T, onehot_t, maskadd).reshape(2, GROUP_HEADS, BLK, BLK).transpose(0, 2, 1, 3)
    bias = bias.reshape(2, BLK, GROUP_LANES)

    saved = []
    xl = xs0
    layers = [layer_weights(0, g_win0)]
    for l in range(2):
        lw = layers[l]
        h = rmsnorm_fwd(xl, lw["g_pre"])
        pa = mm_nt(lw["in_att"], h, 1152, "proj_att")
        ps = mm_nn(h, lw["in_sg"], 1536, "proj_sg")
        pm = mm_nn(h, lw["in_ssm"], 1792, "proj_ssm")
        pg = mm_nn(h, lw["in_gate"], 1536, "proj_gate")
        if l == 0:
            ya, (g_rest0,) = attn_fwd(pa, bias, lw["sinks"], Exchange([], [layer_shards[0][1]]))
            set_rest(lw, g_rest0)
        else:
            ya = attn_fwd(pa, bias, lw["sinks"])
        ys = sgu_fwd(ps, lw["ln_g"], lw["ln_b"], lw["sgw"], lw["sgb_t"])
        ssd_args = (pm, lw["cw"], lw["cb"], lw["dtb"], lw["alog"], lw["dsk"], lw["ng"])
        if l == 0:
            ym, states, conv_pre, (g_win1, g_rest1) = ssd_fwd(*ssd_args, Exchange([], layer_shards[1]))
            layers.append(layer_weights(1, g_win1))
            set_rest(layers[1], g_rest1)
        else:
            ym, states, conv_pre = ssd_fwd(*ssd_args)
        x_next, ba, bs, bm, merged, out_s = merge_fwd(xl, ya, ys, ym, pg, lw["att"], lw["sg"], lw["ssm"], lw["out"],
                                                      lw["g_post"])
        saved.append(dict(x=xl, h=h, pa=pa, ps=ps, pm=pm, pg=pg, ya=ya, ys=ys, ym=ym, states=states, conv_pre=conv_pre, ba=ba, bs=bs,
                          bm=bm, merged=merged, out_s=out_s))
        xl = x_next

    dx, loss_part = loss_grad(xl, target)
    loss = lax.psum(loss_part[0, 0], ("x", "y", "c"))

    dbias = jnp.zeros((2, BLK, GROUP_LANES), F32)
    win_grads, rest_grads = [None, None], [None, None]
    small = {n: [None, None] for n in ("norm_pre", "norm_post", "att_sinks", "sg_ln_g", "sg_ln_b", "sg_w", "sg_b",
                                       "ssm_conv_b", "ssm_dt_bias", "ssm_a_log", "ssm_d", "ssm_norm_g",
                                       "conv_w_full")}
    for l in (1, 0):
        lw, sv = layers[l], saved[l]
        dout, dba, dbs, dbm, dpg, dya, dys, dym, dg_post = merge_bwd(
            dx, sv["out_s"], sv["pg"], sv["ba"], sv["bs"], sv["bm"], lw["att"], lw["sg"], lw["ssm"], lw["out"],
            lw["g_post"])
        dw_out = mm_tn(sv["merged"], dout, 1024, "dw_out")
        dw_att = mm_kn(sv["ya"], dba, 1024, "dw_br_att")
        dw_sg = mm_tn(sv["ys"], dbs, 1024, "dw_br_sg")
        dw_ssm = mm_tn(sv["ym"], dbm, 1024, "dw_br_ssm")
        rest_grads[l] = jnp.concatenate(
            [dw_att.reshape(N_DEV, 128, D_MODEL), dw_sg.reshape(N_DEV, 128, D_MODEL),
             dw_ssm.reshape(N_DEV, 256, D_MODEL), dw_out.reshape(N_DEV, 128, D_MODEL)], axis=1).astype(WIRE_DTYPE)
        dpa, dbias, dsinks = attn_bwd(sv["pa"], dya, bias, lw["sinks"], dbias)
        dps, dsgw, dsgb_t, dln_g, dln_b = sgu_bwd(sv["ps"], dys, lw["ln_g"], lw["ln_b"], lw["sgw"], lw["sgw_t"],
                                                  lw["sgb_t"])
        ssd_args = (sv["pm"], sv["conv_pre"], dym, sv["states"], lw["cw"], lw["dtb"], lw["alog"], lw["dsk"], lw["ng"])
        if l == 0:
            dpm, dcw, dcb, dvec, dng, (recv_win1, recv_rest1, recv_rest0) = ssd_bwd(
                *ssd_args, Exchange([win_grads[1], rest_grads[1], rest_grads[0]], []))
        else:
            dpm, dcw, dcb, dvec, dng = ssd_bwd(*ssd_args)
        dslabs = dict(att=mm_kn(dpa, sv["h"], 1152, "dw_in_att").T, sg=mm_tn(sv["h"], dps, 3072, "dw_in_sg"),
                      ssm=mm_tn(sv["h"], dpm, 2688, "dw_in_ssm"), gate=mm_tn(sv["h"], dpg, 3072, "dw_in_gate"))
        win_grads[l] = _shards_from_slabs(dslabs)
        dh_args = ([dpa, dps, dpm, dpg], [lw["in_att"], lw["in_sg"], lw["in_ssm"], lw["in_gate"]], sv["x"],
                   lw["g_pre"], dx)
        if l == 0:
            dx, dg_pre, (recv_win0,) = dh_norm_bwd(*dh_args, Exchange([win_grads[0]], []))
        else:
            dx, dg_pre = dh_norm_bwd(*dh_args)
        small["norm_pre"][l] = dg_pre[0]
        small["norm_post"][l] = dg_post[0]
        small["att_sinks"][l] = dsinks[0, :ATT_HEADS]
        small["sg_ln_g"][l] = dln_g[0]
        small["sg_ln_b"][l] = dln_b[0]
        small["sg_w"][l] = dsgw
        small["sg_b"][l] = dsgb_t[:, :SG_GROUPS].T
        small["ssm_conv_b"][l] = dcb[0]
        small["ssm_dt_bias"][l] = dvec[0, :SSM_HEADS]
        small["ssm_a_log"][l] = dvec[1, :SSM_HEADS]
        small["ssm_d"][l] = dvec[2, :SSM_HEADS]
        small["ssm_norm_g"][l] = dng[0]
        small["conv_w_full"][l] = dcw[0:4]
    grad_x = dx
    dbias = dbias.reshape(2, BLK, GROUP_HEADS, BLK).transpose(0, 2, 1, 3).reshape(ATT_HEADS, BLK * BLK)
    d_rel_bias = bias_table_bwd(dbias, onehot_t).T

    small_d = {n: jnp.stack(v) for n, v in small.items()}
    small_d["rel_bias"] = d_rel_bias
    *res_win, (recv_small,) = adamw([recv_win0, recv_win1], w_in, m_w_in, v_w_in, WIN_TILE, "adamw_w_in",
                                    Exchange([], [_pack_small(small_d)]))
    res_rest = adamw([recv_rest0, recv_rest1], _pack_rest(*[w[n] for n in REST]), _pack_rest(*[mom[n] for n in REST]),
                     _pack_rest(*[var[n] for n in REST]), REST_TILE, "adamw_rest")
    small_names = [n for n, _ in SMALL_SIZES if n != "conv_w_full"]
    g_s, dw_s, nm_s, nv_s = adamw([recv_small], _pack_small({n: w[n] for n in small_names}),
                                  _pack_small({n: mom[n] for n in small_names}),
                                  _pack_small({n: var[n] for n in small_names}), SMALL_TILE, "adamw_small")
    shapes = {n: w[n].shape for n in small_names}
    shapes["conv_w_full"] = (2, 4, CONV_DIM)
    g_conv_full = _unpack_small(g_s, shapes)["conv_w_full"]
    g_conv = lax.dynamic_slice_in_dim(g_conv_full, my_dev * 384, 384, axis=2)
    pack_conv = lambda a: _pad_rows(a.reshape(-1, D_MODEL), 8)
    g_c, dw_c, nm_c, nv_c = adamw([pack_conv(g_conv)[None]], pack_conv(ssm_conv_w), pack_conv(m_ssm_conv_w),
                                  pack_conv(v_ssm_conv_w), 8, "adamw_conv")

    results = {}
    for q, (tag, psm, pc) in enumerate((("grad", g_s, g_c), ("delta", dw_s, dw_c), ("new_m", nm_s, nm_c),
                                        ("new_v", nv_s, nv_c))):
        r = dict(zip(REST, _unpack_rest(res_rest[q])))
        r["w_in"] = res_win[q]
        r.update(_unpack_small(psm, {n: w[n].shape for n in small_names}))
        r["ssm_conv_w"] = pc[0:3].reshape(2, 4, 384)
        results[tag] = r
    outs = [loss, grad_x[None]]
    for tag in ("grad", "delta", "new_m", "new_v"):
        outs += [results[tag][n] for n in WEIGHTS]
    return tuple(outs)
```

```python
import math

import jax
import jax.numpy as jnp
from jax import lax
from jax.experimental import pallas as pl
from jax.experimental.pallas import tpu as pltpu

F32 = jnp.float32
MXU_DTYPE = jnp.bfloat16
ACT_DTYPE = jnp.bfloat16
WIRE_DTYPE = jnp.bfloat16
HI = lax.Precision.HIGHEST
MESH = pl.DeviceIdType.MESH

D_MODEL = 1024
N_DEV = 8
ATT_HEADS = 16
HEAD_DIM = 64
BLK = 128
SG_GROUPS = 8
SSM_WIDTH = 2048
SSM_HEADS = 32
SSM_GROUPS = 4
SSM_GW = SSM_WIDTH // SSM_GROUPS
CONV_DIM = 3072
REL_BUCKETS = 32
EPS = 1e-6
NEG = -1e30

ATT_COLS = 2304
SG_COLS = 3072
SSM_COLS = 5376
GATE_COLS = 3072
DT_OFF = 5120

VMEM_LIMIT_V7X = 56 * 2 ** 20
LANES = 128
DH_MAX_TK = 1792

ADAM_LR, ADAM_B1, ADAM_B2, ADAM_EPS, ADAM_WD, ADAM_STEP = 0.001, 0.9, 0.999, 1e-08, 0.01, 10

WIN_SHARD = 1700
WIN_LANES = 1792
REST_PARTS = (128, 128, 256, 128)
LAYER_REST = sum(REST_PARTS)
REST_TILE = 128
WIN_TILE = 128
SMALL_ROWS = 384
SMALL_TILE = 128


def _cparams(sem=None):
    return pltpu.CompilerParams(dimension_semantics=sem, vmem_limit_bytes=VMEM_LIMIT_V7X)


def _dot(a, b):
    return jnp.dot(a.astype(MXU_DTYPE), b.astype(MXU_DTYPE), preferred_element_type=F32)


def _dot_nt(a, b):
    return lax.dot_general(a.astype(MXU_DTYPE), b.astype(MXU_DTYPE), (((1,), (1,)), ((), ())),
                           preferred_element_type=F32)


def _dot_tn(a, b):
    return lax.dot_general(a.astype(MXU_DTYPE), b.astype(MXU_DTYPE), (((0,), (0,)), ((), ())),
                           preferred_element_type=F32)


def _dot_hi(a, b):
    return jnp.dot(a, b, precision=HI, preferred_element_type=F32)


def _dot_onehot(a, onehot):
    hi = a.astype(jnp.bfloat16)
    lo = (a - hi.astype(F32)).astype(jnp.bfloat16)
    return (jnp.dot(hi, onehot, preferred_element_type=F32) + jnp.dot(lo, onehot, preferred_element_type=F32))


def _dot_hi_nt(a, b):
    return lax.dot_general(a, b, (((1,), (1,)), ((), ())), precision=HI, preferred_element_type=F32)


def _sig(x):
    return 1.0 / (1.0 + jnp.exp(-x))


def _dsilu(x, s):
    return s * (1.0 + x * (1.0 - s))


def _full(shape):
    nd = len(shape)
    return pl.BlockSpec(shape, lambda *_: (0,) * nd)


def rmsnorm_fwd(x, g):
    s, d = x.shape
    tm = min(512, s)

    def body(x_ref, g_ref, o_ref):
        xv = x_ref[...]
        r = lax.rsqrt(jnp.mean(xv * xv, axis=-1, keepdims=True) + EPS)
        o_ref[...] = (xv * r * g_ref[...]).astype(o_ref.dtype)

    return pl.pallas_call(
        body, name="rmsnorm_fwd", grid=(s // tm,),
        in_specs=[pl.BlockSpec((tm, d), lambda i: (i, 0)), _full((1, d))],
        out_specs=pl.BlockSpec((tm, d), lambda i: (i, 0)),
        out_shape=jax.ShapeDtypeStruct((s, d), ACT_DTYPE),
        compiler_params=_cparams(("parallel",)),
    )(x, g)


def mm_nn(a, b, tn, name):
    s, k = a.shape
    n = b.shape[1]
    tm = min(2048, s)

    def body(a_ref, b_ref, o_ref):
        o_ref[...] = _dot(a_ref[...], b_ref[...]).astype(o_ref.dtype)

    return pl.pallas_call(
        body, name=name, grid=(s // tm, n // tn),
        in_specs=[pl.BlockSpec((tm, k), lambda i, j: (i, 0)), pl.BlockSpec((k, tn), lambda i, j: (0, j))],
        out_specs=pl.BlockSpec((tm, tn), lambda i, j: (i, j)),
        out_shape=jax.ShapeDtypeStruct((s, n), ACT_DTYPE),
        compiler_params=_cparams(("parallel", "arbitrary")),
    )(a, b)


def mm_nt(a, b, tm, name):
    m, k = a.shape
    s = b.shape[0]
    ts = min(2048, s)

    def body(a_ref, b_ref, o_ref):
        o_ref[...] = _dot_nt(a_ref[...], b_ref[...]).astype(o_ref.dtype)

    return pl.pallas_call(
        body, name=name, grid=(s // ts, m // tm),
        in_specs=[pl.BlockSpec((tm, k), lambda i, j: (j, 0)), pl.BlockSpec((ts, k), lambda i, j: (i, 0))],
        out_specs=pl.BlockSpec((tm, ts), lambda i, j: (j, i)),
        out_shape=jax.ShapeDtypeStruct((m, s), ACT_DTYPE),
        compiler_params=_cparams(("parallel", "arbitrary")),
    )(a, b)


def mm_kn(a, b, tm, name):
    m, s = a.shape
    n = b.shape[1]
    ts = min(512, s)
    nt = s // ts

    def body(a_ref, b_ref, o_ref, acc_ref):
        @pl.when(pl.program_id(1) == 0)
        def _():
            acc_ref[...] = jnp.zeros_like(acc_ref)

        acc_ref[...] += _dot(a_ref[...], b_ref[...])

        @pl.when(pl.program_id(1) == nt - 1)
        def _():
            o_ref[...] = acc_ref[...].astype(o_ref.dtype)

    return pl.pallas_call(
        body, name=name, grid=(m // tm, nt),
        in_specs=[pl.BlockSpec((tm, ts), lambda j, t: (j, t)), pl.BlockSpec((ts, n), lambda j, t: (t, 0))],
        out_specs=pl.BlockSpec((tm, n), lambda j, t: (j, 0)),
        out_shape=jax.ShapeDtypeStruct((m, n), WIRE_DTYPE),
        scratch_shapes=[pltpu.VMEM((tm, n), F32)],
        compiler_params=_cparams(("parallel", "arbitrary")),
    )(a, b)


def mm_tn(a, b, tn, name):
    s, k = a.shape
    n = b.shape[1]
    ts = min(512, s)
    nt = s // ts

    def body(a_ref, b_ref, o_ref, acc_ref):
        @pl.when(pl.program_id(1) == 0)
        def _():
            acc_ref[...] = jnp.zeros_like(acc_ref)

        acc_ref[...] += _dot_tn(a_ref[...], b_ref[...])

        @pl.when(pl.program_id(1) == nt - 1)
        def _():
            o_ref[...] = acc_ref[...].astype(o_ref.dtype)

    return pl.pallas_call(
        body, name=name, grid=(n // tn, nt),
        in_specs=[pl.BlockSpec((ts, k), lambda j, t: (t, 0)), pl.BlockSpec((ts, tn), lambda j, t: (t, j))],
        out_specs=pl.BlockSpec((k, tn), lambda j, t: (0, j)),
        out_shape=jax.ShapeDtypeStruct((k, n), WIRE_DTYPE),
        scratch_shapes=[pltpu.VMEM((k, tn), F32)],
        compiler_params=_cparams(("parallel", "arbitrary")),
    )(a, b)


def dh_norm_bwd(dslabs, wslabs, x, g, dres, ex=None):
    s, d = x.shape
    tm = min(512, s)
    widths = [ds.shape[0 if q == 0 else 1] for q, ds in enumerate(dslabs)]
    tks = [max(t for t in range(LANES, DH_MAX_TK + 1, LANES) if wd % t == 0) for wd in widths]
    counts = [wd // t for wd, t in zip(widths, tks)]
    starts = [sum(counts[:i]) for i in range(len(counts))]
    nk = sum(counts)
    ns = len(dslabs)

    hosted = ex is not None
    ni = s // tm

    def mm_body(*refs):
        (own_in, (dh_ref,), _), hosted_refs = _split_hosted(refs, 2 * ns, 1, 0, ex)
        d_refs, w_refs = own_in[:ns], own_in[ns:]
        i, k = pl.program_id(0), pl.program_id(1)
        if hosted:
            @pl.when((i == 0) & (k == 0))
            def _():
                ex.start(*hosted_refs)

            @pl.when((i == ni - 1) & (k == nk - 1))
            def _():
                ex.relay(*hosted_refs)
                ex.wait(*hosted_refs)

        @pl.when(k == 0)
        def _():
            dh_ref[...] = jnp.zeros_like(dh_ref)

        for q in range(ns):
            @pl.when((k >= starts[q]) & (k < starts[q] + counts[q]))
            def _(q=q):
                if q == 0:
                    dh_ref[...] += _dot_tn(d_refs[q][...], w_refs[q][...])
                else:
                    dh_ref[...] += _dot_nt(d_refs[q][...], w_refs[q][...])

    def clamp(q):
        if q == 0:
            return pl.BlockSpec((tks[q], tm), lambda i, k: (jnp.clip(k - starts[q], 0, counts[q] - 1), i))
        return pl.BlockSpec((tm, tks[q]), lambda i, k: (i, jnp.clip(k - starts[q], 0, counts[q] - 1)))

    def clamp_w(q):
        if q == 0:
            return pl.BlockSpec((tks[q], d), lambda i, k: (jnp.clip(k - starts[q], 0, counts[q] - 1), 0))
        return pl.BlockSpec((d, tks[q]), lambda i, k: (0, jnp.clip(k - starts[q], 0, counts[q] - 1)))

    res = pl.pallas_call(
        mm_body, name="dh_matmul_scatter" if hosted else "dh_matmul", grid=(ni, nk),
        in_specs=([clamp(q) for q in range(ns)] + [clamp_w(q) for q in range(ns)]
                  + (ex.in_specs if hosted else [])),
        out_specs=[pl.BlockSpec((tm, d), lambda i, k: (i, 0))] + (ex.out_specs if hosted else []),
        out_shape=[jax.ShapeDtypeStruct((s, d), F32)] + (ex.out_shape if hosted else []),
        scratch_shapes=ex.scratch if hosted else [],
        compiler_params=_cparams(("arbitrary" if hosted else "parallel", "arbitrary")),
    )(*dslabs, *wslabs, *(ex.arrays if hosted else []))
    dh, ex_results = res[0], res[1:]

    te = min(512, s)

    def norm_body(dh_ref, x_ref, g_ref, dres_ref, dx_ref, dg_ref):
        @pl.when(pl.program_id(0) == 0)
        def _():
            dg_ref[...] = jnp.zeros_like(dg_ref)

        xv = x_ref[...]
        r = lax.rsqrt(jnp.mean(xv * xv, axis=-1, keepdims=True) + EPS)
        xn = xv * r
        dhv = dh_ref[...]
        dg_ref[...] += jnp.sum(dhv * xn, axis=0, keepdims=True)
        dxn = dhv * g_ref[...]
        dx_ref[...] = dres_ref[...] + r * (dxn - xn * jnp.mean(dxn * xn, axis=-1, keepdims=True))

    rows = pl.BlockSpec((te, d), lambda i: (i, 0))
    dx, dg = pl.pallas_call(
        norm_body, name="norm_bwd", grid=(s // te,),
        in_specs=[rows, rows, _full((1, d)), rows],
        out_specs=[rows, _full((1, d))],
        out_shape=[jax.ShapeDtypeStruct((s, d), F32), jax.ShapeDtypeStruct((1, d), F32)],
        compiler_params=_cparams(("arbitrary",)),
    )(dh, x, g, dres)
    return (dx, dg, ex_results) if hosted else (dx, dg)


def bias_table(rel_bias_t, onehot_t, maskadd):
    n = onehot_t.shape[1]
    tn = 8192

    def body(r_ref, o_ref, m_ref, out_ref):
        out_ref[...] = _dot_hi(r_ref[...], o_ref[...]) + m_ref[...]

    return pl.pallas_call(
        body, name="bias_table", grid=(n // tn,),
        in_specs=[_full((ATT_HEADS, REL_BUCKETS)), pl.BlockSpec((REL_BUCKETS, tn), lambda j: (0, j)),
                  pl.BlockSpec((1, tn), lambda j: (0, j))],
        out_specs=pl.BlockSpec((ATT_HEADS, tn), lambda j: (0, j)),
        out_shape=jax.ShapeDtypeStruct((ATT_HEADS, n), F32),
        compiler_params=_cparams(("parallel",)),
    )(rel_bias_t, onehot_t, maskadd)


def bias_table_bwd(dbias, onehot_t):
    n = onehot_t.shape[1]
    tn = 8192

    def body(d_ref, o_ref, out_ref):
        @pl.when(pl.program_id(0) == 0)
        def _():
            out_ref[...] = jnp.zeros_like(out_ref)

        out_ref[...] += _dot_hi_nt(d_ref[...], o_ref[...])

    return pl.pallas_call(
        body, name="bias_table_bwd", grid=(n // tn,),
        in_specs=[pl.BlockSpec((ATT_HEADS, tn), lambda j: (0, j)), pl.BlockSpec((REL_BUCKETS, tn), lambda j: (0, j))],
        out_specs=_full((ATT_HEADS, REL_BUCKETS)),
        out_shape=jax.ShapeDtypeStruct((ATT_HEADS, REL_BUCKETS), F32),
        compiler_params=_cparams(("arbitrary",)),
    )(dbias, onehot_t)


def _fold(full, tri):
    return jnp.where(tri, full[BLK:2 * BLK], full[0:BLK])


def _unfold(folded, tri):
    return jnp.concatenate([jnp.where(tri, 0.0, folded), jnp.where(tri, folded, 0.0)], axis=0)


GROUP_HEADS = ATT_HEADS // 2
GROUP_LANES = GROUP_HEADS * BLK


def _att_group(qg, kcat, vt_cat, bias_g, sink_g, tri, no_prev):
    l = _fold(_dot(kcat, qg), tri) * (HEAD_DIM ** -0.5) + bias_g
    l = jnp.where(no_prev, NEG, l)
    m = jnp.maximum(jnp.max(l, axis=0, keepdims=True), sink_g)
    p = jnp.exp(l - m)
    es = jnp.exp(sink_g - m)
    inv = 1.0 / (jnp.sum(p, axis=0, keepdims=True) + es)
    p = p * inv
    pcat = _unfold(p, tri)
    return p, pcat, es * inv, _dot(vt_cat, pcat)


def _heads_to_lanes(ref, row0):
    return jnp.concatenate([ref[row0 + j * HEAD_DIM:row0 + (j + 1) * HEAD_DIM, :] for j in range(GROUP_HEADS)], axis=1)


def _lanes_to_heads(ref, row0, val):
    for j in range(GROUP_HEADS):
        ref[row0 + j * HEAD_DIM:row0 + (j + 1) * HEAD_DIM, :] = val[:, j * BLK:(j + 1) * BLK].astype(ref.dtype)


def _kv_cat(kvp, kvc, g):
    lo = g * HEAD_DIM
    kt_cat = jnp.concatenate([kvp[lo:lo + HEAD_DIM], kvc[lo:lo + HEAD_DIM]], axis=1)
    vt_cat = jnp.concatenate([kvp[128 + lo:128 + lo + HEAD_DIM], kvc[128 + lo:128 + lo + HEAD_DIM]], axis=1)
    return kt_cat, vt_cat


def _tri_masks(n):
    row = lax.broadcasted_iota(jnp.int32, (BLK, GROUP_LANES), 0)
    query = lax.broadcasted_iota(jnp.int32, (BLK, GROUP_LANES), 1) & (BLK - 1)
    tri = row <= query
    return tri, (n == 0) & jnp.logical_not(tri)


def _split_hosted(refs, n_in, n_out, n_scratch, ex):
    na = ex.na if ex is not None else 0
    o = 0
    parts = []
    for cnt in (n_in, na, n_out, na, n_scratch, 3 if ex is not None else 0):
        parts.append(refs[o:o + cnt])
        o += cnt
    own_in, ex_in, own_out, ex_out, own_scr, ex_sems = parts
    return (own_in, own_out, own_scr), (ex_in, ex_out, ex_sems)


def _call_hosting(body, name, nsteps, in_specs, out_specs, out_shape, scratch, args, ex):
    n_in, n_out, n_scr = len(in_specs), len(out_specs), len(scratch)
    hosted = ex is not None

    def full_body(*refs):
        (own_in, own_out, own_scr), hosted_refs = _split_hosted(refs, n_in, n_out, n_scr, ex)
        if hosted:
            @pl.when(pl.program_id(0) == 0)
            def _():
                ex.start(*hosted_refs)

            @pl.when(pl.program_id(0) == max(nsteps - 4, 0))
            def _():
                ex.relay(*hosted_refs)

            @pl.when(pl.program_id(0) == nsteps - 1)
            def _():
                ex.wait(*hosted_refs)

        body(*own_in, *own_out, *own_scr)

    res = pl.pallas_call(
        full_body, name=name + "_hosting" if hosted else name, grid=(nsteps,),
        in_specs=list(in_specs) + (ex.in_specs if hosted else []),
        out_specs=list(out_specs) + (ex.out_specs if hosted else []),
        out_shape=list(out_shape) + (ex.out_shape if hosted else []),
        scratch_shapes=list(scratch) + (ex.scratch if hosted else []),
        compiler_params=_cparams(("arbitrary",)),
    )(*args, *(ex.arrays if hosted else []))
    return res[:n_out], res[n_out:]


def attn_fwd(pa, bias, sinks, ex=None):
    s = pa.shape[1]
    nb = s // BLK

    def body(pa_ref, kvp_ref, bias_ref, sink_ref, y_ref):
        n = pl.program_id(0)
        kvc = pa_ref[2048:2304, :]
        kvp = kvp_ref[...]
        tri, no_prev = _tri_masks(n)
        for g in range(2):
            kt_cat, vt_cat = _kv_cat(kvp, kvc, g)
            row0 = g * GROUP_HEADS * HEAD_DIM
            _, _, _, o = _att_group(_heads_to_lanes(pa_ref, row0), kt_cat.astype(F32).T, vt_cat, bias_ref[g],
                                    sink_ref[g:g + 1, :], tri, no_prev)
            z = _heads_to_lanes(pa_ref, 1024 + row0).astype(F32)
            _lanes_to_heads(y_ref, row0, o * z * _sig(z))

    (y,), hosted = _call_hosting(
        body, "attn_fwd", nb,
        in_specs=[pl.BlockSpec((ATT_COLS, BLK), lambda n: (0, n)),
                  pl.BlockSpec((256, BLK), lambda n: (8, jnp.maximum(n - 1, 0))),
                  _full((2, BLK, GROUP_LANES)), _full((2, GROUP_LANES))],
        out_specs=[pl.BlockSpec((1024, BLK), lambda n: (0, n))],
        out_shape=[jax.ShapeDtypeStruct((1024, s), ACT_DTYPE)], scratch=[],
        args=(pa, pa, bias, sinks), ex=ex)
    return (y, hosted) if ex is not None else y


def attn_bwd(pa, dy, bias, sinks, dbias_in):
    s = pa.shape[1]
    nb = s // BLK

    def body(pa_ref, kvp_ref, dy_ref, bias_ref, sink_ref, dbin_ref, dpa_ref, dbias_ref, dsink_ref, carry, dsink_acc):
        i = pl.program_id(0)
        n = nb - 1 - i

        @pl.when(i == 0)
        def _():
            dbias_ref[...] = dbin_ref[...]
            dsink_acc[...] = jnp.zeros_like(dsink_acc)
            carry[...] = jnp.zeros_like(carry)

        kvc = pa_ref[2048:2304, :]
        kvp = kvp_ref[...]
        tri, no_prev = _tri_masks(n)
        scale = HEAD_DIM ** -0.5
        for g in range(2):
            kt_cat, vt_cat = _kv_cat(kvp, kvc, g)
            row0 = g * GROUP_HEADS * HEAD_DIM
            qg = _heads_to_lanes(pa_ref, row0)
            p, pcat, psink, o = _att_group(qg, kt_cat.astype(F32).T, vt_cat, bias_ref[g], sink_ref[g:g + 1, :], tri,
                                           no_prev)
            z = _heads_to_lanes(pa_ref, 1024 + row0).astype(F32)
            dyg = _heads_to_lanes(dy_ref, row0).astype(F32)
            sz = _sig(z)
            d_o = dyg * z * sz
            _lanes_to_heads(dpa_ref, 1024 + row0, dyg * _dsilu(z, sz) * o)
            delta = jnp.sum(d_o * o, axis=0, keepdims=True)
            dl = p * (_fold(_dot(vt_cat.astype(F32).T, d_o), tri) - delta)
            dsink_acc[g:g + 1, :] += psink * delta
            dbias_ref[g] += dl
            dlcat = _unfold(dl, tri)
            _lanes_to_heads(dpa_ref, row0, _dot(kt_cat, dlcat) * scale)
            for q, dkv in enumerate((_dot_nt(qg, dlcat) * scale, _dot_nt(d_o, pcat))):
                r0 = q * 128 + g * HEAD_DIM
                dpa_ref[2048 + r0:2048 + r0 + HEAD_DIM, :] = (
                    dkv[:, BLK:2 * BLK] + carry[r0:r0 + HEAD_DIM, :]).astype(dpa_ref.dtype)
                carry[r0:r0 + HEAD_DIM, :] = dkv[:, 0:BLK]

        @pl.when(i == nb - 1)
        def _():
            lane = lax.broadcasted_iota(jnp.int32, (1, 128), 1)
            dsink = jnp.zeros((1, 128), F32)
            for h in range(ATT_HEADS):
                g, j = divmod(h, GROUP_HEADS)
                tot = jnp.sum(dsink_acc[g:g + 1, j * BLK:(j + 1) * BLK], axis=1, keepdims=True)
                dsink = dsink + jnp.where(lane == h, -tot, 0.0)
            dsink_ref[...] = dsink

    return pl.pallas_call(
        body, name="attn_bwd", grid=(nb,),
        in_specs=[pl.BlockSpec((ATT_COLS, BLK), lambda i: (0, nb - 1 - i)),
                  pl.BlockSpec((256, BLK), lambda i: (8, jnp.maximum(nb - 2 - i, 0))),
                  pl.BlockSpec((1024, BLK), lambda i: (0, nb - 1 - i)),
                  _full((2, BLK, GROUP_LANES)), _full((2, GROUP_LANES)), _full((2, BLK, GROUP_LANES))],
        out_specs=[pl.BlockSpec((ATT_COLS, BLK), lambda i: (0, nb - 1 - i)),
                   _full((2, BLK, GROUP_LANES)), _full((1, 128))],
        out_shape=[jax.ShapeDtypeStruct((ATT_COLS, s), ACT_DTYPE),
                   jax.ShapeDtypeStruct((2, BLK, GROUP_LANES), F32),
                   jax.ShapeDtypeStruct((1, 128), F32)],
        scratch_shapes=[pltpu.VMEM((256, BLK), F32), pltpu.VMEM((2, GROUP_LANES), F32)],
        compiler_params=_cparams(("arbitrary",)),
    )(pa, pa, dy, bias, sinks, dbias_in)


def _layernorm(v, g, b):
    mu = jnp.mean(v, axis=-1, keepdims=True)
    vc = v - mu
    rstd = lax.rsqrt(jnp.mean(vc * vc, axis=-1, keepdims=True) + EPS)
    xhat = vc * rstd
    return xhat, rstd, xhat * g + b


def sgu_fwd(ps, ln_g, ln_b, w_tril, b_t, ex=None):
    s = ps.shape[0]

    def body(ps_ref, g_ref, b_ref, w_ref, bt_ref, y_ref):
        u = ps_ref[:, 0:1024].astype(F32)
        v = ps_ref[:, 1024:2048].astype(F32)
        z = ps_ref[:, 2048:3072].astype(F32)
        _, _, vn = _layernorm(v, g_ref[...], b_ref[...])
        gate = u * z * _sig(z)
        for g in range(SG_GROUPS):
            sl = slice(g * 128, (g + 1) * 128)
            mixed = _dot(w_ref[g], vn[:, sl]) + bt_ref[:, g:g + 1]
            y_ref[:, sl] = (gate[:, sl] * mixed).astype(y_ref.dtype)

    (y,), hosted = _call_hosting(
        body, "sgu_fwd", s // BLK,
        in_specs=[pl.BlockSpec((BLK, SG_COLS), lambda c: (c, 0)), _full((1, 1024)), _full((1, 1024)),
                  _full((SG_GROUPS, BLK, BLK)), _full((BLK, 128))],
        out_specs=[pl.BlockSpec((BLK, 1024), lambda c: (c, 0))],
        out_shape=[jax.ShapeDtypeStruct((s, 1024), ACT_DTYPE)], scratch=[],
        args=(ps, ln_g, ln_b, w_tril, b_t), ex=ex)
    return (y, hosted) if ex is not None else y


def sgu_bwd(ps, dy, ln_g, ln_b, w_tril, w_tril_t, b_t):
    s = ps.shape[0]

    def body(ps_ref, dy_ref, g_ref, b_ref, w_ref, wt_ref, bt_ref, dps_ref, dw_ref, dbt_ref, dg_ref, db_ref, dvn_scr):
        @pl.when(pl.program_id(0) == 0)
        def _():
            dw_ref[...] = jnp.zeros_like(dw_ref)
            dbt_ref[...] = jnp.zeros_like(dbt_ref)
            dg_ref[...] = jnp.zeros_like(dg_ref)
            db_ref[...] = jnp.zeros_like(db_ref)

        u = ps_ref[:, 0:1024].astype(F32)
        v = ps_ref[:, 1024:2048].astype(F32)
        z = ps_ref[:, 2048:3072].astype(F32)
        dy = dy_ref[...].astype(F32)
        xhat, rstd, vn = _layernorm(v, g_ref[...], b_ref[...])
        sz = _sig(z)
        silu = z * sz
        row = lax.broadcasted_iota(jnp.int32, (BLK, BLK), 0)
        colm = lax.broadcasted_iota(jnp.int32, (BLK, BLK), 1)
        tril = row >= colm
        dbt = jnp.zeros((BLK, 128), F32)
        for g in range(SG_GROUPS):
            sl = slice(g * 128, (g + 1) * 128)
            vng = vn[:, sl]
            mixed = _dot(w_ref[g], vng) + bt_ref[:, g:g + 1]
            dyg, ug = dy[:, sl], u[:, sl]
            dps_ref[:, sl] = (dyg * mixed * silu[:, sl]).astype(dps_ref.dtype)
            dps_ref[:, 2048 + g * 128:2048 + (g + 1) * 128] = (
                dyg * ug * mixed * _dsilu(z[:, sl], sz[:, sl])).astype(dps_ref.dtype)
            dm = dyg * ug * silu[:, sl]
            dw_ref[g] += jnp.where(tril, _dot_nt(dm, vng), 0.0)
            dbt = dbt + jnp.where(colm == g, jnp.sum(dm, axis=1, keepdims=True), 0.0)
            dvn_scr[:, sl] = _dot(wt_ref[g], dm)
        dbt_ref[...] += dbt
        dvn = dvn_scr[...]
        dg_ref[...] += jnp.sum(dvn * xhat, axis=0, keepdims=True)
        db_ref[...] += jnp.sum(dvn, axis=0, keepdims=True)
        dxh = dvn * g_ref[...]
        dv = rstd * (dxh - jnp.mean(dxh, axis=-1, keepdims=True)
                     - xhat * jnp.mean(dxh * xhat, axis=-1, keepdims=True))
        dps_ref[:, 1024:2048] = dv.astype(dps_ref.dtype)

    return pl.pallas_call(
        body, name="sgu_bwd", grid=(s // BLK,),
        in_specs=[pl.BlockSpec((BLK, SG_COLS), lambda c: (c, 0)), pl.BlockSpec((BLK, 1024), lambda c: (c, 0)),
                  _full((1, 1024)), _full((1, 1024)), _full((SG_GROUPS, BLK, BLK)), _full((SG_GROUPS, BLK, BLK)),
                  _full((BLK, 128))],
        out_specs=[pl.BlockSpec((BLK, SG_COLS), lambda c: (c, 0)), _full((SG_GROUPS, BLK, BLK)), _full((BLK, 128)),
                   _full((1, 1024)), _full((1, 1024))],
        out_shape=[jax.ShapeDtypeStruct((s, SG_COLS), ACT_DTYPE), jax.ShapeDtypeStruct((SG_GROUPS, BLK, BLK), F32),
                   jax.ShapeDtypeStruct((BLK, 128), F32), jax.ShapeDtypeStruct((1, 1024), F32),
                   jax.ShapeDtypeStruct((1, 1024), F32)],
        scratch_shapes=[pltpu.VMEM((BLK, 1024), F32)],
        compiler_params=_cparams(("arbitrary",)),
    )(ps, dy, ln_g, ln_b, w_tril, w_tril_t, b_t)


def _shift_down(cur, prev16, k):
    if k == 0:
        return cur
    r = pltpu.roll(cur, k, 0)
    rp = pltpu.roll(prev16, k, 0)
    row = lax.broadcasted_iota(jnp.int32, (8, cur.shape[1]), 0)
    return jnp.concatenate([jnp.where(row < k, rp[0:8], r[0:8]), r[8:]], axis=0)


def _shift_up(cur, next16, k):
    if k == 0:
        return cur
    n = cur.shape[0]
    r = pltpu.roll(cur, n - k, 0)
    rn = pltpu.roll(next16, 16 - k, 0)
    row = lax.broadcasted_iota(jnp.int32, (8, cur.shape[1]), 0)
    return jnp.concatenate([r[:n - 8], jnp.where(row >= 8 - k, rn[8:16], r[n - 8:])], axis=0)


def _bcast8(v):
    return jnp.broadcast_to(v, (16, v.shape[1]))


def _causal_conv(xbc, prev16, cw, cbias):
    pre = cbias + cw[3:4] * xbc
    for k in (1, 2, 3):
        pre = pre + cw[3 - k:4 - k] * _shift_down(xbc, prev16, k)
    return pre


class _Ssd:
    def __init__(self, pre, dtr, dtb, alog, dsk, tri, e):
        self.pre = pre
        self.sg = _sig(pre)
        act = pre * self.sg
        self.xs = act[:, 0:SSM_WIDTH]
        self.bm = act[:, SSM_WIDTH:SSM_WIDTH + 512]
        self.cm = act[:, SSM_WIDTH + 512:CONV_DIM]
        self.dtp = dtr + dtb
        self.dt = jnp.maximum(self.dtp, 0.0) + jnp.log(1.0 + jnp.exp(-jnp.abs(self.dtp)))
        self.a = -jnp.exp(alog)
        self.acs = _dot_hi(tri, self.dt * self.a)
        self.acs_t = self.acs.T
        tot = self.acs[BLK - 1:BLK]
        self.ecs = jnp.exp(self.acs)
        self.dte = jnp.exp(tot - self.acs)
        self.cd = jnp.exp(tot)
        self.dt_x = _dot_onehot(self.dt, e)
        self.ecs_x = _dot_onehot(self.ecs, e)
        self.dte_x = _dot_onehot(self.dte, e)
        self.cd_x = _dot_onehot(_bcast8(self.cd), e)[0:1]
        self.d_x = _dot_onehot(_bcast8(dsk), e)[0:1]
        self.xdt = self.xs * self.dt_x
        row = lax.broadcasted_iota(jnp.int32, (BLK, BLK), 0)
        col = lax.broadcasted_iota(jnp.int32, (BLK, BLK), 1)
        self.tril = row >= col

    def group(self, g):
        sl = slice(g * 128, (g + 1) * 128)
        bg, cg = self.bm[:, sl], self.cm[:, sl]
        return bg, cg, _dot_nt(cg, bg)

    def decay(self, h):
        seg = self.acs[:, h:h + 1] - self.acs_t[h:h + 1, :]
        return jnp.exp(jnp.where(self.tril, seg, NEG))

    def y_pre_gate(self, ht_of, yd_scr, yoff_scr):
        for g in range(SSM_GROUPS):
            bg, cg, cb = self.group(g)
            for j in range(8):
                h = g * 8 + j
                sl = slice(h * 64, (h + 1) * 64)
                yd_scr[:, sl] = _dot(cb * self.decay(h), self.xdt[:, sl])
            gs = slice(g * SSM_GW, (g + 1) * SSM_GW)
            yoff_scr[:, gs] = _dot(cg, ht_of(g)) * self.ecs_x[:, gs]
        return yd_scr[...] + yoff_scr[...] + self.d_x * self.xs


def _ssd_consts():
    hh = lax.broadcasted_iota(jnp.int32, (128, SSM_WIDTH), 0)
    ch = lax.broadcasted_iota(jnp.int32, (128, SSM_WIDTH), 1)
    e = (ch // 64 == hh).astype(jnp.bfloat16)
    row = lax.broadcasted_iota(jnp.int32, (BLK, BLK), 0)
    col = lax.broadcasted_iota(jnp.int32, (BLK, BLK), 1)
    tri = (row >= col).astype(F32)
    return tri, e


def _pad_lanes(v, n=128):
    return jnp.pad(v, ((0, 0), (0, n - v.shape[1])))


def ssd_fwd(pm, cw, cbias, dtb, alog, dsk, ng, ex=None):
    s = pm.shape[0]
    nc = s // BLK
    tri, e = _ssd_consts()

    def body(pm_ref, prev_ref, cw_ref, cb_ref, dtb_ref, al_ref, d_ref, ng_ref, tri_ref, e_ref,
             y_ref, st_ref, pre_ref, ht_ref, yd_scr, yoff_scr):
        c = pl.program_id(0)

        @pl.when(c == 0)
        def _():
            ht_ref[...] = jnp.zeros_like(ht_ref)

        xbc = pm_ref[:, 0:CONV_DIM].astype(F32)
        prev16 = jnp.where(c == 0, 0.0, prev_ref[...].astype(F32))
        pre = _causal_conv(xbc, prev16, cw_ref[...], cb_ref[...])
        pre_ref[...] = pre.astype(pre_ref.dtype)
        f = _Ssd(pre, pm_ref[:, DT_OFF:DT_OFF + 128].astype(F32), dtb_ref[...], al_ref[...], d_ref[...],
                 tri_ref[...], e_ref[...])
        st_ref[0] = ht_ref[...]
        y = f.y_pre_gate(lambda g: ht_ref[g], yd_scr, yoff_scr)
        for g in range(SSM_GROUPS):
            bg, _, _ = f.group(g)
            gs = slice(g * SSM_GW, (g + 1) * SSM_GW)
            ht_ref[g] = ht_ref[g] * f.cd_x[:, gs] + _dot_tn(bg, f.xdt[:, gs] * f.dte_x[:, gs])
        z = pm_ref[:, CONV_DIM:CONV_DIM + SSM_WIDTH].astype(F32)
        ypre = y * z * _sig(z)
        for g in range(SSM_GROUPS):
            gs = slice(g * SSM_GW, (g + 1) * SSM_GW)
            yg = ypre[:, gs]
            rr = lax.rsqrt(jnp.mean(yg * yg, axis=-1, keepdims=True) + EPS)
            y_ref[:, gs] = (yg * rr * ng_ref[:, gs]).astype(y_ref.dtype)

    own, hosted = _call_hosting(
        body, "ssd_fwd", nc,
        in_specs=[pl.BlockSpec((BLK, SSM_COLS), lambda c: (c, 0)),
                  pl.BlockSpec((16, CONV_DIM), lambda c: (jnp.maximum(8 * c - 1, 0), 0)),
                  _full((4, CONV_DIM)), _full((1, CONV_DIM)), _full((1, 128)), _full((1, 128)), _full((1, 128)),
                  _full((1, SSM_WIDTH)), _full((BLK, BLK)), _full((128, SSM_WIDTH))],
        out_specs=[pl.BlockSpec((BLK, SSM_WIDTH), lambda c: (c, 0)),
                   pl.BlockSpec((1, SSM_GROUPS, 128, SSM_GW), lambda c: (c, 0, 0, 0)),
                   pl.BlockSpec((BLK, CONV_DIM), lambda c: (c, 0))],
        out_shape=[jax.ShapeDtypeStruct((s, SSM_WIDTH), ACT_DTYPE),
                   jax.ShapeDtypeStruct((nc, SSM_GROUPS, 128, SSM_GW), F32),
                   jax.ShapeDtypeStruct((s, CONV_DIM), ACT_DTYPE)],
        scratch=[pltpu.VMEM((SSM_GROUPS, 128, SSM_GW), F32), pltpu.VMEM((BLK, SSM_WIDTH), F32),
                 pltpu.VMEM((BLK, SSM_WIDTH), F32)],
        args=(pm, pm, cw, cbias, dtb, alog, dsk, ng, tri, e), ex=ex)
    return (*own, hosted) if ex is not None else tuple(own)


def ssd_bwd(pm, pre, dy, states, cw, dtb, alog, dsk, ng, ex=None):
    s = pm.shape[0]
    nc = s // BLK
    tri, e = _ssd_consts()
    tri_t, e_t = tri.T, e.T

    def body(pm_ref, pre_ref, dy_ref, st_ref, cw_ref, dtb_ref, al_ref, d_ref, ng_ref,
             tri_ref, trit_ref, e_ref, et_ref,
             dpm_ref, dcw_ref, dcb_ref, dvec_ref, dng_ref,
             dht_ref, dcar_ref, yd_scr, yoff_scr, dx_scr, r2_scr, hs_scr, da_scr, dat_scr, dd_scr, dbc_scr):
        i = pl.program_id(0)
        n = nc - 1 - i

        @pl.when(i == 0)
        def _():
            dht_ref[...] = jnp.zeros_like(dht_ref)
            dcar_ref[...] = jnp.zeros_like(dcar_ref)
            dcw_ref[...] = jnp.zeros_like(dcw_ref)
            dcb_ref[...] = jnp.zeros_like(dcb_ref)
            dvec_ref[...] = jnp.zeros_like(dvec_ref)
            dng_ref[...] = jnp.zeros_like(dng_ref)
            dd_scr[...] = jnp.zeros_like(dd_scr)
            da_scr[...] = jnp.zeros_like(da_scr)
            dat_scr[...] = jnp.zeros_like(dat_scr)

        cw = cw_ref[...]
        f = _Ssd(pre_ref[...].astype(F32), pm_ref[:, DT_OFF:DT_OFF + 128].astype(F32), dtb_ref[...], al_ref[...],
                 d_ref[...], tri_ref[...], e_ref[...])
        et = et_ref[...]
        y = f.y_pre_gate(lambda g: st_ref[0, g], yd_scr, yoff_scr)

        z = pm_ref[:, CONV_DIM:CONV_DIM + SSM_WIDTH].astype(F32)
        dyv = dy_ref[...].astype(F32)
        sz = _sig(z)
        silu = z * sz
        ypre = y * silu
        for g in range(SSM_GROUPS):
            gs = slice(g * SSM_GW, (g + 1) * SSM_GW)
            yg = ypre[:, gs]
            rr = lax.rsqrt(jnp.mean(yg * yg, axis=-1, keepdims=True) + EPS)
            nrm = yg * rr
            dng_ref[:, gs] += jnp.sum(dyv[:, gs] * nrm, axis=0, keepdims=True)
            dn = dyv[:, gs] * ng_ref[:, gs]
            dx_scr[:, gs] = rr * (dn - nrm * jnp.mean(dn * nrm, axis=-1, keepdims=True))
        dypre = dx_scr[...]
        d_y = dypre * silu
        dpm_ref[:, CONV_DIM:CONV_DIM + SSM_WIDTH] = (dypre * y * _dsilu(z, sz)).astype(dpm_ref.dtype)

        for g in range(SSM_GROUPS):
            bg, cg, cb = f.group(g)
            gs = slice(g * SSM_GW, (g + 1) * SSM_GW)
            htg = st_ref[0, g]
            dhn = dht_ref[g]
            dcb = jnp.zeros((BLK, BLK), F32)
            for j in range(8):
                h = g * 8 + j
                sl = slice(h * 64, (h + 1) * 64)
                dec = f.decay(h)
                dyh = d_y[:, sl]
                dmd = _dot_nt(dyh, f.xdt[:, sl]) * dec
                dcb = dcb + dmd
                gm = dmd * cb
                da_scr[:, h:h + 1] = jnp.sum(gm, axis=1, keepdims=True)
                dat_scr[h:h + 1, :] = jnp.sum(gm, axis=0, keepdims=True)
                dx_scr[:, sl] = _dot_tn(cb * dec, dyh)
            dz = f.ecs_x[:, gs] * d_y[:, gs]
            dbc_scr[:, 512 + g * 128:512 + (g + 1) * 128] = _dot(dcb, bg) + _dot_nt(dz, htg)
            dbc_scr[:, g * 128:(g + 1) * 128] = _dot_tn(dcb, cg) + _dot_nt(f.xdt[:, gs] * f.dte_x[:, gs], dhn)
            dws = _dot(bg, dhn)
            dx_scr[:, gs] += f.dte_x[:, gs] * dws
            r2_scr[:, gs] = dws * f.xdt[:, gs]
            hs_scr[:, gs] = _bcast8(jnp.sum(dhn * htg, axis=0, keepdims=True))
            dht_ref[g] = f.cd_x[:, gs] * dhn + _dot_tn(cg, dz)
        d_x = dx_scr[...]
        r1 = _dot(d_y * yoff_scr[...], et)
        r2 = _dot(r2_scr[...], et) * f.dte
        dcd = _dot_onehot(hs_scr[...], et)[0:1]
        d_tot = jnp.sum(r2, axis=0, keepdims=True) + dcd * f.cd
        row = lax.broadcasted_iota(jnp.int32, (BLK, 128), 0)
        d_a = da_scr[...] - dat_scr[...].T + r1 - r2 + jnp.where(row == BLK - 1, d_tot, 0.0)
        dadt = _dot_hi(trit_ref[...], d_a)
        ddt = dadt * f.a + _dot(d_x * f.xs, et)
        lane = lax.broadcasted_iota(jnp.int32, (BLK, 128), 1)
        dr = jnp.where(lane < SSM_HEADS, ddt * _sig(f.dtp), 0.0)
        dvec_ref[0:1, :] += jnp.sum(dr, axis=0, keepdims=True)
        dvec_ref[1:2, :] += jnp.sum(dadt * f.dt, axis=0, keepdims=True) * f.a
        dd_scr[...] += _bcast8(jnp.sum(d_y * f.xs, axis=0, keepdims=True))
        dpm_ref[:, DT_OFF:DT_OFF + 128] = dr.astype(dpm_ref.dtype)
        dpm_ref[:, DT_OFF + 128:SSM_COLS] = jnp.zeros((BLK, 128), dpm_ref.dtype)

        dxs = d_x * f.dt_x + f.d_x * d_y
        dact = jnp.concatenate([dxs, dbc_scr[...]], axis=1)
        dpre = dact * _dsilu(f.pre, f.sg)
        dcb_ref[...] += jnp.sum(dpre, axis=0, keepdims=True)
        xbc = pm_ref[:, 0:CONV_DIM].astype(F32)
        dxraw = jnp.zeros((BLK, CONV_DIM), F32)
        nxt = dcar_ref[...]
        for k in range(4):
            ahead = _shift_up(dpre, nxt, k)
            dcw_ref[3 - k:4 - k, :] += jnp.sum(ahead * xbc, axis=0, keepdims=True)
            dxraw = dxraw + cw[3 - k:4 - k] * ahead
        dcar_ref[...] = dpre[0:16]
        dpm_ref[:, 0:CONV_DIM] = dxraw.astype(dpm_ref.dtype)

        @pl.when(i == nc - 1)
        def _():
            dvec_ref[2:3, :] = _dot_onehot(dd_scr[...], et)[0:1]

    own, hosted = _call_hosting(
        body, "ssd_bwd", nc,
        in_specs=[pl.BlockSpec((BLK, SSM_COLS), lambda i: (nc - 1 - i, 0)),
                  pl.BlockSpec((BLK, CONV_DIM), lambda i: (nc - 1 - i, 0)),
                  pl.BlockSpec((BLK, SSM_WIDTH), lambda i: (nc - 1 - i, 0)),
                  pl.BlockSpec((1, SSM_GROUPS, 128, SSM_GW), lambda i: (nc - 1 - i, 0, 0, 0)),
                  _full((4, CONV_DIM)), _full((1, 128)), _full((1, 128)), _full((1, 128)),
                  _full((1, SSM_WIDTH)), _full((BLK, BLK)), _full((BLK, BLK)), _full((128, SSM_WIDTH)),
                  _full((SSM_WIDTH, 128))],
        out_specs=[pl.BlockSpec((BLK, SSM_COLS), lambda i: (nc - 1 - i, 0)),
                   _full((8, CONV_DIM)), _full((1, CONV_DIM)), _full((8, 128)), _full((1, SSM_WIDTH))],
        out_shape=[jax.ShapeDtypeStruct((s, SSM_COLS), ACT_DTYPE), jax.ShapeDtypeStruct((8, CONV_DIM), F32),
                   jax.ShapeDtypeStruct((1, CONV_DIM), F32), jax.ShapeDtypeStruct((8, 128), F32),
                   jax.ShapeDtypeStruct((1, SSM_WIDTH), F32)],
        scratch=[pltpu.VMEM((SSM_GROUPS, 128, SSM_GW), F32), pltpu.VMEM((16, CONV_DIM), F32),
                 pltpu.VMEM((BLK, SSM_WIDTH), F32), pltpu.VMEM((BLK, SSM_WIDTH), F32),
                 pltpu.VMEM((BLK, SSM_WIDTH), F32), pltpu.VMEM((BLK, SSM_WIDTH), F32),
                 pltpu.VMEM((16, SSM_WIDTH), F32), pltpu.VMEM((BLK, 128), F32), pltpu.VMEM((128, BLK), F32),
                 pltpu.VMEM((16, SSM_WIDTH), F32), pltpu.VMEM((BLK, 1024), F32)],
        args=(pm, pre, dy, states, cw, dtb, alog, dsk, ng, tri, tri_t, e, e_t), ex=ex)
    return (*own, hosted) if ex is not None else tuple(own)


def merge_fwd(x, ya, ys, ym, pg, wa, ws, wm, wo, g_post):
    s, d = x.shape
    tm = min(256, s)

    def body(x_ref, ya_ref, ys_ref, ym_ref, pg_ref, wa_ref, ws_ref, wm_ref, wo_ref, g_ref,
             xo_ref, ba_ref, bs_ref, bm_ref, mg_ref, out_ref):
        ba = _dot_tn(ya_ref[...], wa_ref[...])
        bs = _dot(ys_ref[...], ws_ref[...])
        bm = _dot(ym_ref[...], wm_ref[...])
        merged = (_sig(pg_ref[:, 0:d].astype(F32)) * ba + _sig(pg_ref[:, d:2 * d].astype(F32)) * bs
                  + _sig(pg_ref[:, 2 * d:3 * d].astype(F32)) * bm)
        out = _dot(merged, wo_ref[...])
        r = lax.rsqrt(jnp.mean(out * out, axis=-1, keepdims=True) + EPS)
        xo_ref[...] = x_ref[...] + out * r * g_ref[...]
        ba_ref[...] = ba.astype(ba_ref.dtype)
        bs_ref[...] = bs.astype(bs_ref.dtype)
        bm_ref[...] = bm.astype(bm_ref.dtype)
        mg_ref[...] = merged.astype(mg_ref.dtype)
        out_ref[...] = out.astype(out_ref.dtype)

    rows = lambda w: pl.BlockSpec((tm, w), lambda i: (i, 0))
    act = jax.ShapeDtypeStruct((s, d), ACT_DTYPE)
    return pl.pallas_call(
        body, name="merge_fwd", grid=(s // tm,),
        in_specs=[rows(d), pl.BlockSpec((d, tm), lambda i: (0, i)), rows(d), rows(2 * d), rows(3 * d), _full((d, d)),
                  _full((d, d)), _full((2 * d, d)), _full((d, d)), _full((1, d))],
        out_specs=[rows(d)] * 6,
        out_shape=[jax.ShapeDtypeStruct((s, d), F32), act, act, act, act, act],
        compiler_params=_cparams(("parallel",)),
    )(x, ya, ys, ym, pg, wa, ws, wm, wo, g_post)


def merge_bwd(dx, out_s, pg, ba, bs, bm, wa, ws, wm, wo, g_post):
    s, d = dx.shape
    tm = min(256, s)

    def body(dx_ref, out_ref, pg_ref, ba_ref, bs_ref, bm_ref, wa_ref, ws_ref, wm_ref, wo_ref, g_ref,
             dout_ref, dba_ref, dbs_ref, dbm_ref, dpg_ref, dya_ref, dys_ref, dym_ref, dg_ref):
        @pl.when(pl.program_id(0) == 0)
        def _():
            dg_ref[...] = jnp.zeros_like(dg_ref)

        o = out_ref[...].astype(F32)
        dxv = dx_ref[...]
        r = lax.rsqrt(jnp.mean(o * o, axis=-1, keepdims=True) + EPS)
        nrm = o * r
        dg_ref[...] += jnp.sum(dxv * nrm, axis=0, keepdims=True)
        dn = dxv * g_ref[...]
        dout = r * (dn - nrm * jnp.mean(dn * nrm, axis=-1, keepdims=True))
        dout_ref[...] = dout.astype(dout_ref.dtype)
        dmerged = _dot_nt(dout, wo_ref[...])
        for q, (b_ref, db_ref, w_ref, dy_ref) in enumerate(((ba_ref, dba_ref, wa_ref, dya_ref),
                                                            (bs_ref, dbs_ref, ws_ref, dys_ref),
                                                            (bm_ref, dbm_ref, wm_ref, dym_ref))):
            gt = _sig(pg_ref[:, q * d:(q + 1) * d].astype(F32))
            db = dmerged * gt
            db_ref[...] = db.astype(db_ref.dtype)
            dpg_ref[:, q * d:(q + 1) * d] = (dmerged * b_ref[...].astype(F32) * gt * (1.0 - gt)).astype(dpg_ref.dtype)
            if q == 0:
                dy_ref[...] = _dot_nt(w_ref[...], db).astype(dy_ref.dtype)
            else:
                dy_ref[...] = _dot_nt(db, w_ref[...]).astype(dy_ref.dtype)

    rows = lambda w: pl.BlockSpec((tm, w), lambda i: (i, 0))
    act = lambda w: jax.ShapeDtypeStruct((s, w), ACT_DTYPE)
    return pl.pallas_call(
        body, name="merge_bwd", grid=(s // tm,),
        in_specs=[rows(d), rows(d), rows(3 * d), rows(d), rows(d), rows(d), _full((d, d)), _full((d, d)),
                  _full((2 * d, d)), _full((d, d)), _full((1, d))],
        out_specs=[rows(d), rows(d), rows(d), rows(d), rows(3 * d), pl.BlockSpec((d, tm), lambda i: (0, i)), rows(d),
                   rows(2 * d), _full((1, d))],
        out_shape=[act(d), act(d), act(d), act(d), act(3 * d), jax.ShapeDtypeStruct((d, s), ACT_DTYPE), act(d),
                   act(2 * d), jax.ShapeDtypeStruct((1, d), F32)],
        compiler_params=_cparams(("arbitrary",)),
    )(dx, out_s, pg, ba, bs, bm, wa, ws, wm, wo, g_post)


def loss_grad(y, target):
    s, d = y.shape
    tm = min(512, s)

    def body(y_ref, t_ref, dy_ref, l_ref):
        @pl.when(pl.program_id(0) == 0)
        def _():
            l_ref[...] = jnp.zeros_like(l_ref)

        err = y_ref[...] - t_ref[...]
        dy_ref[...] = err * (1.0 / d)
        part = jnp.sum(jnp.sum(err * err, axis=-1, keepdims=True) * (1.0 / d), axis=0, keepdims=True)
        l_ref[...] += 0.5 * jnp.broadcast_to(part, l_ref.shape)

    return pl.pallas_call(
        body, name="loss_grad", grid=(s // tm,),
        in_specs=[pl.BlockSpec((tm, d), lambda i: (i, 0)), pl.BlockSpec((tm, d), lambda i: (i, 0))],
        out_specs=[pl.BlockSpec((tm, d), lambda i: (i, 0)), _full((8, 128))],
        out_shape=[jax.ShapeDtypeStruct((s, d), F32), jax.ShapeDtypeStruct((8, 128), F32)],
        compiler_params=_cparams(("arbitrary",)),
    )(y, target)


def _mesh_pos():
    x, y, c = lax.axis_index("x"), lax.axis_index("y"), lax.axis_index("c")
    return x, y, c, 4 * x + 2 * y + c


def _peer(x, y, c, k):
    px = 1 - x if k & 4 else x
    py = 1 - y if k & 2 else y
    pc = 1 - c if k & 1 else c
    return (px, py, pc), 4 * px + 2 * py + pc


class Exchange:
    SAME_CORE = (2, 4, 6)

    def __init__(self, scattered, gathered):
        self.ns = len(scattered)
        self.arrays = list(scattered) + list(gathered)
        self.na = len(self.arrays)
        any_spec = pl.BlockSpec(memory_space=pl.ANY)
        self.in_specs = [any_spec] * self.na
        self.out_specs = [any_spec] * self.na
        self.out_shape = ([jax.ShapeDtypeStruct(a.shape, a.dtype) for a in scattered]
                          + [jax.ShapeDtypeStruct((N_DEV,) + a.shape, a.dtype) for a in gathered])
        self.scratch = [pltpu.SemaphoreType.DMA((self.na, N_DEV - 1)), pltpu.SemaphoreType.DMA((self.na, N_DEV - 1)),
                        pltpu.SemaphoreType.DMA((self.na,))]

    def _src(self, ins, q, slot):
        return ins[q].at[slot] if q < self.ns else ins[q]

    def _local(self, ins, outs, sems):
        me = _mesh_pos()[3]
        return [pltpu.make_async_copy(self._src(ins, q, me), outs[q].at[me], sems[2].at[q]) for q in range(self.na)]

    def _direct(self, ins, outs, sems, relations, arrays):
        x, y, c, me = _mesh_pos()
        copies = []
        for k in relations:
            peer, pidx = _peer(x, y, c, k)
            for q in arrays:
                copies.append(pltpu.make_async_remote_copy(
                    src_ref=self._src(ins, q, pidx), dst_ref=outs[q].at[me], send_sem=sems[0].at[q, k - 1],
                    recv_sem=sems[1].at[q, k - 1], device_id=peer, device_id_type=MESH))
        return copies

    def _arrivals(self, ins, outs, sems, relations, arrays):
        x, y, c, _ = _mesh_pos()
        copies = []
        for k in relations:
            peer, pidx = _peer(x, y, c, k)
            for q in arrays:
                copies.append(pltpu.make_async_remote_copy(
                    src_ref=self._src(ins, q, pidx), dst_ref=outs[q].at[pidx], send_sem=sems[0].at[q, k - 1],
                    recv_sem=sems[1].at[q, k - 1], device_id=peer, device_id_type=MESH))
        return copies

    def _relays(self, outs, sems):
        x, y, c, _ = _mesh_pos()
        sibling, _ = _peer(x, y, c, 1)
        copies = []
        for k in self.SAME_CORE:
            _, pidx = _peer(x, y, c, k)
            for q in range(self.ns, self.na):
                copies.append(pltpu.make_async_remote_copy(
                    src_ref=outs[q].at[pidx], dst_ref=outs[q].at[pidx], send_sem=sems[0].at[q, k],
                    recv_sem=sems[1].at[q, k], device_id=sibling, device_id_type=MESH))
        return copies

    def _sends(self, ins, outs, sems):
        return (self._direct(ins, outs, sems, range(1, N_DEV), range(self.ns))
                + self._direct(ins, outs, sems, (1,) + self.SAME_CORE, range(self.ns, self.na)))

    def start(self, ins, outs, sems):
        for cp in self._local(ins, outs, sems) + self._sends(ins, outs, sems):
            cp.start()

    def relay(self, ins, outs, sems):
        for cp in self._arrivals(ins, outs, sems, self.SAME_CORE, range(self.ns, self.na)):
            cp.wait_recv()
        for cp in self._relays(outs, sems):
            cp.start()

    def wait(self, ins, outs, sems):
        for cp in (self._arrivals(ins, outs, sems, range(1, N_DEV), range(self.ns))
                   + self._arrivals(ins, outs, sems, (1, 3, 5, 7), range(self.ns, self.na))):
            cp.wait_recv()
        for cp in self._sends(ins, outs, sems) + self._relays(outs, sems):
            cp.wait_send()
        for cp in self._local(ins, outs, sems):
            cp.wait()


def exchange(scattered, gathered, name):
    ex = Exchange(scattered, gathered)

    def body(*refs):
        ins, outs, sems = refs[:ex.na], refs[ex.na:2 * ex.na], refs[2 * ex.na:]
        ex.start(ins, outs, sems)
        ex.relay(ins, outs, sems)
        ex.wait(ins, outs, sems)

    return pl.pallas_call(body, name=name, in_specs=ex.in_specs, out_specs=ex.out_specs, out_shape=ex.out_shape,
                          scratch_shapes=ex.scratch)(*ex.arrays)


def adamw(parts_list, w, m, v, tile, name, ex=None):
    npart, _, dp = parts_list[0].shape
    d = w.shape[-1]
    counts = [p.shape[1] // tile for p in parts_list]
    starts = [sum(counts[:q]) for q in range(len(counts))]
    n_lists = len(parts_list)

    def body(*refs):
        p_refs = refs[:n_lists]
        w_ref, m_ref, v_ref, g_ref, dw_ref, nm_ref, nv_ref = refs[n_lists:]
        i = pl.program_id(0)
        for q, p_ref in enumerate(p_refs):
            @pl.when((i >= starts[q]) & (i < starts[q] + counts[q]))
            def _(p_ref=p_ref):
                acc = p_ref[0, :, 0:d].astype(F32)
                for k in range(1, npart):
                    acc = acc + p_ref[k, :, 0:d].astype(F32)
                g_ref[...] = acc

        g = g_ref[...]
        nm = ADAM_B1 * m_ref[...] + (1.0 - ADAM_B1) * g
        nv = ADAM_B2 * v_ref[...] + (1.0 - ADAM_B2) * (g * g)
        nm_ref[...] = nm
        nv_ref[...] = nv
        m_hat = nm / (1.0 - ADAM_B1 ** ADAM_STEP)
        v_hat = nv / (1.0 - ADAM_B2 ** ADAM_STEP)
        dw_ref[...] = -ADAM_LR * (m_hat / (jnp.sqrt(v_hat) + ADAM_EPS) + ADAM_WD * w_ref[...])

    def part_rows(q):
        return lambda i: (0, jnp.clip(i - starts[q], 0, counts[q] - 1), 0)

    if w.ndim == 3:
        rows = pl.BlockSpec((None, tile, d), lambda i: (i // counts[0], i % counts[0], 0))
    else:
        rows = pl.BlockSpec((tile, d), lambda i: (i, 0))
    own, hosted = _call_hosting(
        body, name, sum(counts),
        in_specs=[pl.BlockSpec((npart, tile, dp), part_rows(q)) for q in range(n_lists)] + [rows, rows, rows],
        out_specs=[rows] * 4, out_shape=[jax.ShapeDtypeStruct(w.shape, F32)] * 4, scratch=[],
        args=(*parts_list, w, m, v), ex=ex)
    return (*own, hosted) if ex is not None else tuple(own)


def _pad_rows(a, rows):
    return jnp.pad(a, ((0, rows - a.shape[0]), (0, 0)))


def _pack_rest(w_att, w_sg, w_ssm, w_out):
    parts = []
    for l in range(2):
        parts += [w_att[l], w_sg[l], w_ssm[l], w_out[l]]
    return jnp.concatenate(parts, axis=0)


def _unpack_rest(p):
    outs = [[], [], [], []]
    o = 0
    for l in range(2):
        for q, rws in enumerate(REST_PARTS):
            outs[q].append(p[o:o + rws])
            o += rws
    return [jnp.stack(t) for t in outs]


def _pack_win(w_in):
    return jnp.pad(w_in.reshape(2 * D_MODEL, WIN_SHARD), ((0, 0), (0, WIN_LANES - WIN_SHARD)))


W_IN_MAP = ((0, 1024, "att", 0), (1024, 1280, "att", 2048), (1280, 2304, "att", 1024), (2304, 5376, "sg", 0),
            (5376, 7424, "ssm", 3072), (7424, 10496, "ssm", 0), (10496, 10528, "ssm", 5120), (10528, 13600, "gate", 0))
SLAB_COLS = {"att": ATT_COLS, "sg": SG_COLS, "ssm": SSM_COLS, "gate": GATE_COLS}


def _slabs_from_shards(g):
    slabs = {}
    for name, width in SLAB_COLS.items():
        pieces, filled = [], 0
        for ga, gb, _, off in sorted((m for m in W_IN_MAP if m[2] == name), key=lambda m: m[3]):
            assert off == filled
            a = ga
            while a < gb:
                d = a // WIN_SHARD
                hi = min(gb, WIN_SHARD * (d + 1))
                pieces.append(g[d, :, a - WIN_SHARD * d:hi - WIN_SHARD * d])
                a = hi
            filled += gb - ga
        if filled < width:
            pieces.append(jnp.zeros((D_MODEL, width - filled), g.dtype))
        slabs[name] = jnp.concatenate(pieces, axis=1)
    return slabs


def _shards_from_slabs(dslabs):
    out = []
    for d in range(N_DEV):
        a, b = WIN_SHARD * d, WIN_SHARD * (d + 1)
        pieces = []
        for ga, gb, name, off in W_IN_MAP:
            lo, hi = max(a, ga), min(b, gb)
            if lo < hi:
                pieces.append(dslabs[name][:, off + lo - ga:off + hi - ga])
        pieces.append(jnp.zeros((D_MODEL, WIN_LANES - WIN_SHARD), pieces[0].dtype))
        out.append(jnp.concatenate(pieces, axis=1).astype(WIRE_DTYPE))
    return jnp.stack(out)


SMALL_SIZES = (("norm_pre", 2048), ("norm_post", 2048), ("rel_bias", 512), ("att_sinks", 32), ("sg_ln_g", 2048),
               ("sg_ln_b", 2048), ("sg_w", 262144), ("sg_b", 2048), ("ssm_conv_b", 6144), ("ssm_dt_bias", 64),
               ("ssm_a_log", 64), ("ssm_d", 64), ("ssm_norm_g", 4096), ("conv_w_full", 24576))


def _pack_small(d):
    parts = []
    for name, size in SMALL_SIZES:
        rows = 8 * (-(-size // (8 * D_MODEL)))
        flat = d[name].reshape(-1) if name in d else jnp.zeros((size,), F32)
        parts.append(jnp.pad(flat, (0, rows * D_MODEL - size)).reshape(rows, D_MODEL))
    return _pad_rows(jnp.concatenate(parts, axis=0), SMALL_ROWS)


def _unpack_small(p, shapes):
    out, o = {}, 0
    for name, size in SMALL_SIZES:
        rows = 8 * (-(-size // (8 * D_MODEL)))
        if name in shapes:
            out[name] = p[o:o + rows].reshape(-1)[:size].reshape(shapes[name])
        o += rows
    return out


def _bucket_onehot_t():
    qi = jnp.arange(BLK, dtype=jnp.int32)[None, :]
    kj = jnp.arange(BLK, dtype=jnp.int32)[:, None]
    dd = (qi - kj) & (BLK - 1)
    in_window = dd >= 0
    max_exact = REL_BUCKETS // 2
    dist_f = jnp.maximum(dd, 1).astype(F32)
    large = max_exact + (jnp.log(dist_f / max_exact) / math.log(128 / max_exact)
                         * (REL_BUCKETS - max_exact)).astype(jnp.int32)
    large = jnp.minimum(large, REL_BUCKETS - 1)
    bucket = jnp.where(dd < max_exact, dd, large).reshape(1, -1)
    onehot_t = (bucket == jnp.arange(REL_BUCKETS, dtype=jnp.int32)[:, None]).astype(F32)
    maskadd = jnp.where(in_window, 0.0, NEG).astype(F32).reshape(1, -1)
    return onehot_t, maskadd


WEIGHTS = ['w_in', 'norm_pre', 'norm_post', 'rel_bias', 'att_sinks', 'sg_ln_g', 'sg_ln_b', 'sg_w', 'sg_b',
           'ssm_conv_w', 'ssm_conv_b', 'ssm_dt_bias', 'ssm_a_log', 'ssm_d', 'ssm_norm_g',
           'w_br_att', 'w_br_sg', 'w_br_ssm', 'w_out']
REST = ('w_br_att', 'w_br_sg', 'w_br_ssm', 'w_out')


def kernel(x, w_in, norm_pre, norm_post, rel_bias, att_sinks, sg_ln_g, sg_ln_b, sg_w, sg_b, ssm_conv_w, ssm_conv_b, ssm_dt_bias, ssm_a_log, ssm_d, ssm_norm_g, w_br_att, w_br_sg, w_br_ssm, w_out, loss_target, m_w_in, m_norm_pre, m_norm_post, m_rel_bias, m_att_sinks, m_sg_ln_g, m_sg_ln_b, m_sg_w, m_sg_b, m_ssm_conv_w, m_ssm_conv_b, m_ssm_dt_bias, m_ssm_a_log, m_ssm_d, m_ssm_norm_g, m_w_br_att, m_w_br_sg, m_w_br_ssm, m_w_out, v_w_in, v_norm_pre, v_norm_post, v_rel_bias, v_att_sinks, v_sg_ln_g, v_sg_ln_b, v_sg_w, v_sg_b, v_ssm_conv_w, v_ssm_conv_b, v_ssm_dt_bias, v_ssm_a_log, v_ssm_d, v_ssm_norm_g, v_w_br_att, v_w_br_sg, v_w_br_ssm, v_w_out):
    w = dict(w_in=w_in, norm_pre=norm_pre, norm_post=norm_post, rel_bias=rel_bias, att_sinks=att_sinks,
             sg_ln_g=sg_ln_g, sg_ln_b=sg_ln_b, sg_w=sg_w, sg_b=sg_b, ssm_conv_w=ssm_conv_w, ssm_conv_b=ssm_conv_b,
             ssm_dt_bias=ssm_dt_bias, ssm_a_log=ssm_a_log, ssm_d=ssm_d, ssm_norm_g=ssm_norm_g,
             w_br_att=w_br_att, w_br_sg=w_br_sg, w_br_ssm=w_br_ssm, w_out=w_out)
    mom = dict(w_in=m_w_in, norm_pre=m_norm_pre, norm_post=m_norm_post, rel_bias=m_rel_bias, att_sinks=m_att_sinks,
               sg_ln_g=m_sg_ln_g, sg_ln_b=m_sg_ln_b, sg_w=m_sg_w, sg_b=m_sg_b, ssm_conv_w=m_ssm_conv_w,
               ssm_conv_b=m_ssm_conv_b, ssm_dt_bias=m_ssm_dt_bias, ssm_a_log=m_ssm_a_log, ssm_d=m_ssm_d,
               ssm_norm_g=m_ssm_norm_g, w_br_att=m_w_br_att, w_br_sg=m_w_br_sg, w_br_ssm=m_w_br_ssm, w_out=m_w_out)
    var = dict(w_in=v_w_in, norm_pre=v_norm_pre, norm_post=v_norm_post, rel_bias=v_rel_bias, att_sinks=v_att_sinks,
               sg_ln_g=v_sg_ln_g, sg_ln_b=v_sg_ln_b, sg_w=v_sg_w, sg_b=v_sg_b, ssm_conv_w=v_ssm_conv_w,
               ssm_conv_b=v_ssm_conv_b, ssm_dt_bias=v_ssm_dt_bias, ssm_a_log=v_ssm_a_log, ssm_d=v_ssm_d,
               ssm_norm_g=v_ssm_norm_g, w_br_att=v_w_br_att, w_br_sg=v_w_br_sg, w_br_ssm=v_w_br_ssm, w_out=v_w_out)
    xs0 = x[0]
    target = loss_target[0]
    my_dev = 4 * lax.axis_index("x") + 2 * lax.axis_index("y") + lax.axis_index("c")

    conv_shard = _pad_rows(ssm_conv_w.reshape(-1, D_MODEL), 8)
    win_shard = _pack_win(w_in).astype(WIRE_DTYPE)
    rest_shard = _pack_rest(*[w[n] for n in REST]).astype(WIRE_DTYPE)
    layer_shards = [[win_shard[l * D_MODEL:(l + 1) * D_MODEL], rest_shard[l * LAYER_REST:(l + 1) * LAYER_REST]]
                    for l in range(2)]
    g_win0, gathered_conv = exchange([], [layer_shards[0][0], conv_shard], "all_gather")
    conv_full = gathered_conv[:, 0:3].reshape(N_DEV, 2, 4, 384).transpose(1, 2, 0, 3).reshape(2, 4, CONV_DIM)

    def set_rest(lw, g_rest):
        o = 0
        for name, rws in zip(("att", "sg", "ssm", "out"), REST_PARTS):
            lw[name] = g_rest[:, o:o + rws].reshape(N_DEV * rws, D_MODEL).astype(MXU_DTYPE)
            o += rws

    def layer_weights(l, g_win):
        slabs = _slabs_from_shards(g_win)
        lw = {"in_" + name: slab.astype(MXU_DTYPE) for name, slab in slabs.items()}
        lw["in_att"] = lw["in_att"].T
        tril = jnp.tril(jnp.ones((BLK, BLK), bool))
        sgw = jnp.where(tril[None], sg_w[l], 0.0)
        lw.update(
            g_pre=norm_pre[l][None], g_post=norm_post[l][None], sinks=jnp.repeat(att_sinks[l], BLK).reshape(2, GROUP_LANES),
            ln_g=sg_ln_g[l][None], ln_b=sg_ln_b[l][None], sgw=sgw.astype(MXU_DTYPE),
            sgw_t=sgw.transpose(0, 2, 1).astype(MXU_DTYPE), sgb_t=_pad_lanes(sg_b[l].T),
            cw=conv_full[l], cb=ssm_conv_b[l][None], dtb=_pad_lanes(ssm_dt_bias[l][None]),
            alog=_pad_lanes(ssm_a_log[l][None]), dsk=_pad_lanes(ssm_d[l][None]), ng=ssm_norm_g[l][None])
        return lw

    onehot_t, maskadd = _bucket_onehot_t()
    bias = bias_table(rel_bias.T, onehot_t, maskadd).reshape(2, GROUP_HEADS, BLK, BLK).transpose(0, 2, 1, 3)
    bias = bias.reshape(2, BLK, GROUP_LANES)

    saved = []
    xl = xs0
    layers = [layer_weights(0, g_win0)]
    for l in range(2):
        lw = layers[l]
        h = rmsnorm_fwd(xl, lw["g_pre"])
        pa = mm_nt(lw["in_att"], h, 1152, "proj_att")
        ps = mm_nn(h, lw["in_sg"], 1536, "proj_sg")
        pm = mm_nn(h, lw["in_ssm"], 1792, "proj_ssm")
        pg = mm_nn(h, lw["in_gate"], 1536, "proj_gate")
        if l == 0:
            ya, (g_rest0,) = attn_fwd(pa, bias, lw["sinks"], Exchange([], [layer_shards[0][1]]))
            set_rest(lw, g_rest0)
        else:
            ya = attn_fwd(pa, bias, lw["sinks"])
        sgu_args = (ps, lw["ln_g"], lw["ln_b"], lw["sgw"], lw["sgb_t"])
        ssd_args = (pm, lw["cw"], lw["cb"], lw["dtb"], lw["alog"], lw["dsk"], lw["ng"])
        if l == 0:
            ys, (g_rest1,) = sgu_fwd(*sgu_args, Exchange([], [layer_shards[1][1]]))
            ym, states, conv_pre, (g_win1,) = ssd_fwd(*ssd_args, Exchange([], [layer_shards[1][0]]))
            layers.append(layer_weights(1, g_win1))
            set_rest(layers[1], g_rest1)
        else:
            ys = sgu_fwd(*sgu_args)
            ym, states, conv_pre = ssd_fwd(*ssd_args)
        x_next, ba, bs, bm, merged, out_s = merge_fwd(xl, ya, ys, ym, pg, lw["att"], lw["sg"], lw["ssm"], lw["out"],
                                                      lw["g_post"])
        saved.append(dict(x=xl, h=h, pa=pa, ps=ps, pm=pm, pg=pg, ya=ya, ys=ys, ym=ym, states=states, conv_pre=conv_pre, ba=ba, bs=bs,
                          bm=bm, merged=merged, out_s=out_s))
        xl = x_next

    dx, loss_part = loss_grad(xl, target)
    loss = lax.psum(loss_part[0, 0], ("x", "y", "c"))

    dbias = jnp.zeros((2, BLK, GROUP_LANES), F32)
    win_grads, rest_grads = [None, None], [None, None]
    small = {n: [None, None] for n in ("norm_pre", "norm_post", "att_sinks", "sg_ln_g", "sg_ln_b", "sg_w", "sg_b",
                                       "ssm_conv_b", "ssm_dt_bias", "ssm_a_log", "ssm_d", "ssm_norm_g",
                                       "conv_w_full")}
    for l in (1, 0):
        lw, sv = layers[l], saved[l]
        dout, dba, dbs, dbm, dpg, dya, dys, dym, dg_post = merge_bwd(
            dx, sv["out_s"], sv["pg"], sv["ba"], sv["bs"], sv["bm"], lw["att"], lw["sg"], lw["ssm"], lw["out"],
            lw["g_post"])
        dw_out = mm_tn(sv["merged"], dout, 1024, "dw_out")
        dw_att = mm_kn(sv["ya"], dba, 1024, "dw_br_att")
        dw_sg = mm_tn(sv["ys"], dbs, 1024, "dw_br_sg")
        dw_ssm = mm_tn(sv["ym"], dbm, 1024, "dw_br_ssm")
        rest_grads[l] = jnp.concatenate(
            [dw_att.reshape(N_DEV, 128, D_MODEL), dw_sg.reshape(N_DEV, 128, D_MODEL),
             dw_ssm.reshape(N_DEV, 256, D_MODEL), dw_out.reshape(N_DEV, 128, D_MODEL)], axis=1).astype(WIRE_DTYPE)
        dpa, dbias, dsinks = attn_bwd(sv["pa"], dya, bias, lw["sinks"], dbias)
        dps, dsgw, dsgb_t, dln_g, dln_b = sgu_bwd(sv["ps"], dys, lw["ln_g"], lw["ln_b"], lw["sgw"], lw["sgw_t"],
                                                  lw["sgb_t"])
        ssd_args = (sv["pm"], sv["conv_pre"], dym, sv["states"], lw["cw"], lw["dtb"], lw["alog"], lw["dsk"], lw["ng"])
        if l == 0:
            dpm, dcw, dcb, dvec, dng, (recv_win1, recv_rest0) = ssd_bwd(
                *ssd_args, Exchange([win_grads[1], rest_grads[0]], []))
        else:
            dpm, dcw, dcb, dvec, dng, (recv_rest1,) = ssd_bwd(*ssd_args, Exchange([rest_grads[1]], []))
        dslabs = dict(att=mm_kn(dpa, sv["h"], 1152, "dw_in_att").T, sg=mm_tn(sv["h"], dps, 3072, "dw_in_sg"),
                      ssm=mm_tn(sv["h"], dpm, 2688, "dw_in_ssm"), gate=mm_tn(sv["h"], dpg, 3072, "dw_in_gate"))
        win_grads[l] = _shards_from_slabs(dslabs)
        dh_args = ([dpa, dps, dpm, dpg], [lw["in_att"], lw["in_sg"], lw["in_ssm"], lw["in_gate"]], sv["x"],
                   lw["g_pre"], dx)
        if l == 0:
            dx, dg_pre, (recv_win0,) = dh_norm_bwd(*dh_args, Exchange([win_grads[0]], []))
        else:
            dx, dg_pre = dh_norm_bwd(*dh_args)
        small["norm_pre"][l] = dg_pre[0]
        small["norm_post"][l] = dg_post[0]
        small["att_sinks"][l] = dsinks[0, :ATT_HEADS]
        small["sg_ln_g"][l] = dln_g[0]
        small["sg_ln_b"][l] = dln_b[0]
        small["sg_w"][l] = dsgw
        small["sg_b"][l] = dsgb_t[:, :SG_GROUPS].T
        small["ssm_conv_b"][l] = dcb[0]
        small["ssm_dt_bias"][l] = dvec[0, :SSM_HEADS]
        small["ssm_a_log"][l] = dvec[1, :SSM_HEADS]
        small["ssm_d"][l] = dvec[2, :SSM_HEADS]
        small["ssm_norm_g"][l] = dng[0]
        small["conv_w_full"][l] = dcw[0:4]
    grad_x = dx
    dbias = dbias.reshape(2, BLK, GROUP_HEADS, BLK).transpose(0, 2, 1, 3).reshape(ATT_HEADS, BLK * BLK)
    d_rel_bias = bias_table_bwd(dbias, onehot_t).T

    small_d = {n: jnp.stack(v) for n, v in small.items()}
    small_d["rel_bias"] = d_rel_bias
    *res_win, (recv_small,) = adamw([recv_win0, recv_win1], w_in, m_w_in, v_w_in, WIN_TILE, "adamw_w_in",
                                    Exchange([], [_pack_small(small_d)]))
    res_rest = adamw([recv_rest0, recv_rest1], _pack_rest(*[w[n] for n in REST]), _pack_rest(*[mom[n] for n in REST]),
                     _pack_rest(*[var[n] for n in REST]), REST_TILE, "adamw_rest")
    small_names = [n for n, _ in SMALL_SIZES if n != "conv_w_full"]
    g_s, dw_s, nm_s, nv_s = adamw([recv_small], _pack_small({n: w[n] for n in small_names}),
                                  _pack_small({n: mom[n] for n in small_names}),
                                  _pack_small({n: var[n] for n in small_names}), SMALL_TILE, "adamw_small")
    shapes = {n: w[n].shape for n in small_names}
    shapes["conv_w_full"] = (2, 4, CONV_DIM)
    g_conv_full = _unpack_small(g_s, shapes)["conv_w_full"]
    g_conv = lax.dynamic_slice_in_dim(g_conv_full, my_dev * 384, 384, axis=2)
    pack_conv = lambda a: _pad_rows(a.reshape(-1, D_MODEL), 8)
    g_c, dw_c, nm_c, nv_c = adamw([pack_conv(g_conv)[None]], pack_conv(ssm_conv_w), pack_conv(m_ssm_conv_w),
                                  pack_conv(v_ssm_conv_w), 8, "adamw_conv")

    results = {}
    for q, (tag, psm, pc) in enumerate((("grad", g_s, g_c), ("delta", dw_s, dw_c), ("new_m", nm_s, nm_c),
                                        ("new_v", nv_s, nv_c))):
        r = dict(zip(REST, _unpack_rest(res_rest[q])))
        r["w_in"] = res_win[q]
        r.update(_unpack_small(psm, {n: w[n].shape for n in small_names}))
        r["ssm_conv_w"] = pc[0:3].reshape(2, 4, 384)
        results[tag] = r
    outs = [loss, grad_x[None]]
    for tag in ("grad", "delta", "new_m", "new_v"):
        outs += [results[tag][n] for n in WEIGHTS]
    return tuple(outs)
```

```python
import math

import jax
import jax.numpy as jnp
from jax import lax
from jax.experimental import pallas as pl
from jax.experimental.pallas import tpu as pltpu

F32 = jnp.float32
MXU_DTYPE = jnp.bfloat16
ACT_DTYPE = jnp.bfloat16
WIRE_DTYPE = jnp.bfloat16
HI = lax.Precision.HIGHEST
MESH = pl.DeviceIdType.MESH

D_MODEL = 1024
N_DEV = 8
ATT_HEADS = 16
HEAD_DIM = 64
BLK = 128
SG_GROUPS = 8
SSM_WIDTH = 2048
SSM_HEADS = 32
SSM_GROUPS = 4
SSM_GW = SSM_WIDTH // SSM_GROUPS
CONV_DIM = 3072
REL_BUCKETS = 32
EPS = 1e-6
NEG = -1e30

ATT_COLS = 2304
SG_COLS = 3072
SSM_COLS = 5376
GATE_COLS = 3072
DT_OFF = 5120

VMEM_LIMIT_V7X = 56 * 2 ** 20
DH_TK = 768

ADAM_LR, ADAM_B1, ADAM_B2, ADAM_EPS, ADAM_WD, ADAM_STEP = 0.001, 0.9, 0.999, 1e-08, 0.01, 10

WIN_SHARD = 1700
WIN_LANES = 1792
REST_PARTS = (128, 128, 256, 128)
LAYER_REST = sum(REST_PARTS)
REST_TILE = 128
WIN_TILE = 128
SMALL_ROWS = 384
SMALL_TILE = 128


def _cparams(sem=None):
    return pltpu.CompilerParams(dimension_semantics=sem, vmem_limit_bytes=VMEM_LIMIT_V7X)


def _dot(a, b):
    return jnp.dot(a.astype(MXU_DTYPE), b.astype(MXU_DTYPE), preferred_element_type=F32)


def _dot_nt(a, b):
    return lax.dot_general(a.astype(MXU_DTYPE), b.astype(MXU_DTYPE), (((1,), (1,)), ((), ())),
                           preferred_element_type=F32)


def _dot_tn(a, b):
    return lax.dot_general(a.astype(MXU_DTYPE), b.astype(MXU_DTYPE), (((0,), (0,)), ((), ())),
                           preferred_element_type=F32)


def _dot_hi(a, b):
    return jnp.dot(a, b, precision=HI, preferred_element_type=F32)


def _dot_onehot(a, onehot):
    hi = a.astype(jnp.bfloat16)
    lo = (a - hi.astype(F32)).astype(jnp.bfloat16)
    return (jnp.dot(hi, onehot, preferred_element_type=F32) + jnp.dot(lo, onehot, preferred_element_type=F32))


def _dot_hi_nt(a, b):
    return lax.dot_general(a, b, (((1,), (1,)), ((), ())), precision=HI, preferred_element_type=F32)


def _sig(x):
    return 1.0 / (1.0 + jnp.exp(-x))


def _dsilu(x, s):
    return s * (1.0 + x * (1.0 - s))


def _full(shape):
    nd = len(shape)
    return pl.BlockSpec(shape, lambda *_: (0,) * nd)


def rmsnorm_fwd(x, g):
    s, d = x.shape
    tm = min(512, s)

    def body(x_ref, g_ref, o_ref):
        xv = x_ref[...]
        r = lax.rsqrt(jnp.mean(xv * xv, axis=-1, keepdims=True) + EPS)
        o_ref[...] = (xv * r * g_ref[...]).astype(o_ref.dtype)

    return pl.pallas_call(
        body, name="rmsnorm_fwd", grid=(s // tm,),
        in_specs=[pl.BlockSpec((tm, d), lambda i: (i, 0)), _full((1, d))],
        out_specs=pl.BlockSpec((tm, d), lambda i: (i, 0)),
        out_shape=jax.ShapeDtypeStruct((s, d), ACT_DTYPE),
        compiler_params=_cparams(("parallel",)),
    )(x, g)


def mm_nn(a, b, tn, name):
    s, k = a.shape
    n = b.shape[1]
    tm = min(2048, s)

    def body(a_ref, b_ref, o_ref):
        o_ref[...] = _dot(a_ref[...], b_ref[...]).astype(o_ref.dtype)

    return pl.pallas_call(
        body, name=name, grid=(s // tm, n // tn),
        in_specs=[pl.BlockSpec((tm, k), lambda i, j: (i, 0)), pl.BlockSpec((k, tn), lambda i, j: (0, j))],
        out_specs=pl.BlockSpec((tm, tn), lambda i, j: (i, j)),
        out_shape=jax.ShapeDtypeStruct((s, n), ACT_DTYPE),
        compiler_params=_cparams(("parallel", "arbitrary")),
    )(a, b)


def mm_nt(a, b, tm, name):
    m, k = a.shape
    s = b.shape[0]
    ts = min(2048, s)

    def body(a_ref, b_ref, o_ref):
        o_ref[...] = _dot_nt(a_ref[...], b_ref[...]).astype(o_ref.dtype)

    return pl.pallas_call(
        body, name=name, grid=(s // ts, m // tm),
        in_specs=[pl.BlockSpec((tm, k), lambda i, j: (j, 0)), pl.BlockSpec((ts, k), lambda i, j: (i, 0))],
        out_specs=pl.BlockSpec((tm, ts), lambda i, j: (j, i)),
        out_shape=jax.ShapeDtypeStruct((m, s), ACT_DTYPE),
        compiler_params=_cparams(("parallel", "arbitrary")),
    )(a, b)


def mm_kn(a, b, tm, name):
    m, s = a.shape
    n = b.shape[1]
    ts = min(512, s)
    nt = s // ts

    def body(a_ref, b_ref, o_ref, acc_ref):
        @pl.when(pl.program_id(1) == 0)
        def _():
            acc_ref[...] = jnp.zeros_like(acc_ref)

        acc_ref[...] += _dot(a_ref[...], b_ref[...])

        @pl.when(pl.program_id(1) == nt - 1)
        def _():
            o_ref[...] = acc_ref[...].astype(o_ref.dtype)

    return pl.pallas_call(
        body, name=name, grid=(m // tm, nt),
        in_specs=[pl.BlockSpec((tm, ts), lambda j, t: (j, t)), pl.BlockSpec((ts, n), lambda j, t: (t, 0))],
        out_specs=pl.BlockSpec((tm, n), lambda j, t: (j, 0)),
        out_shape=jax.ShapeDtypeStruct((m, n), WIRE_DTYPE),
        scratch_shapes=[pltpu.VMEM((tm, n), F32)],
        compiler_params=_cparams(("parallel", "arbitrary")),
    )(a, b)


def mm_tn(a, b, tn, name):
    s, k = a.shape
    n = b.shape[1]
    ts = min(512, s)
    nt = s // ts

    def body(a_ref, b_ref, o_ref, acc_ref):
        @pl.when(pl.program_id(1) == 0)
        def _():
            acc_ref[...] = jnp.zeros_like(acc_ref)

        acc_ref[...] += _dot_tn(a_ref[...], b_ref[...])

        @pl.when(pl.program_id(1) == nt - 1)
        def _():
            o_ref[...] = acc_ref[...].astype(o_ref.dtype)

    return pl.pallas_call(
        body, name=name, grid=(n // tn, nt),
        in_specs=[pl.BlockSpec((ts, k), lambda j, t: (t, 0)), pl.BlockSpec((ts, tn), lambda j, t: (t, j))],
        out_specs=pl.BlockSpec((k, tn), lambda j, t: (0, j)),
        out_shape=jax.ShapeDtypeStruct((k, n), WIRE_DTYPE),
        scratch_shapes=[pltpu.VMEM((k, tn), F32)],
        compiler_params=_cparams(("parallel", "arbitrary")),
    )(a, b)


def dh_norm_bwd(dslabs, wslabs, x, g, dres, ex=None):
    s, d = x.shape
    tm = min(1024, s)
    widths = [ds.shape[0 if q == 0 else 1] for q, ds in enumerate(dslabs)]
    tks = [DH_TK] * len(widths)
    counts = [wd // t for wd, t in zip(widths, tks)]
    starts = [sum(counts[:i]) for i in range(len(counts))]
    nk = sum(counts)
    ns = len(dslabs)

    hosted = ex is not None
    ni = s // tm

    def mm_body(*refs):
        (own_in, (dh_ref,), _), hosted_refs = _split_hosted(refs, 2 * ns, 1, 0, ex)
        d_refs, w_refs = own_in[:ns], own_in[ns:]
        i, k = pl.program_id(0), pl.program_id(1)
        if hosted:
            @pl.when((i == 0) & (k == 0))
            def _():
                ex.start(*hosted_refs)

            @pl.when((i == ni - 1) & (k == nk - 1))
            def _():
                ex.relay(*hosted_refs)
                ex.wait(*hosted_refs)

        @pl.when(k == 0)
        def _():
            dh_ref[...] = jnp.zeros_like(dh_ref)

        for q in range(ns):
            @pl.when((k >= starts[q]) & (k < starts[q] + counts[q]))
            def _(q=q):
                if q == 0:
                    dh_ref[...] += _dot_tn(d_refs[q][...], w_refs[q][...])
                else:
                    dh_ref[...] += _dot_nt(d_refs[q][...], w_refs[q][...])

    def clamp(q):
        if q == 0:
            return pl.BlockSpec((tks[q], tm), lambda i, k: (jnp.clip(k - starts[q], 0, counts[q] - 1), i))
        return pl.BlockSpec((tm, tks[q]), lambda i, k: (i, jnp.clip(k - starts[q], 0, counts[q] - 1)))

    def clamp_w(q):
        if q == 0:
            return pl.BlockSpec((tks[q], d), lambda i, k: (jnp.clip(k - starts[q], 0, counts[q] - 1), 0))
        return pl.BlockSpec((d, tks[q]), lambda i, k: (0, jnp.clip(k - starts[q], 0, counts[q] - 1)))

    res = pl.pallas_call(
        mm_body, name="dh_matmul_scatter" if hosted else "dh_matmul", grid=(ni, nk),
        in_specs=([clamp(q) for q in range(ns)] + [clamp_w(q) for q in range(ns)]
                  + (ex.in_specs if hosted else [])),
        out_specs=[pl.BlockSpec((tm, d), lambda i, k: (i, 0))] + (ex.out_specs if hosted else []),
        out_shape=[jax.ShapeDtypeStruct((s, d), F32)] + (ex.out_shape if hosted else []),
        scratch_shapes=ex.scratch if hosted else [],
        compiler_params=_cparams(("arbitrary" if hosted else "parallel", "arbitrary")),
    )(*dslabs, *wslabs, *(ex.arrays if hosted else []))
    dh, ex_results = res[0], res[1:]

    te = min(512, s)

    def norm_body(dh_ref, x_ref, g_ref, dres_ref, dx_ref, dg_ref):
        @pl.when(pl.program_id(0) == 0)
        def _():
            dg_ref[...] = jnp.zeros_like(dg_ref)

        xv = x_ref[...]
        r = lax.rsqrt(jnp.mean(xv * xv, axis=-1, keepdims=True) + EPS)
        xn = xv * r
        dhv = dh_ref[...]
        dg_ref[...] += jnp.sum(dhv * xn, axis=0, keepdims=True)
        dxn = dhv * g_ref[...]
        dx_ref[...] = dres_ref[...] + r * (dxn - xn * jnp.mean(dxn * xn, axis=-1, keepdims=True))

    rows = pl.BlockSpec((te, d), lambda i: (i, 0))
    dx, dg = pl.pallas_call(
        norm_body, name="norm_bwd", grid=(s // te,),
        in_specs=[rows, rows, _full((1, d)), rows],
        out_specs=[rows, _full((1, d))],
        out_shape=[jax.ShapeDtypeStruct((s, d), F32), jax.ShapeDtypeStruct((1, d), F32)],
        compiler_params=_cparams(("arbitrary",)),
    )(dh, x, g, dres)
    return (dx, dg, ex_results) if hosted else (dx, dg)


def bias_table(rel_bias_t, onehot_t, maskadd):
    n = onehot_t.shape[1]
    tn = 8192

    def body(r_ref, o_ref, m_ref, out_ref):
        out_ref[...] = _dot_hi(r_ref[...], o_ref[...]) + m_ref[...]

    return pl.pallas_call(
        body, name="bias_table", grid=(n // tn,),
        in_specs=[_full((ATT_HEADS, REL_BUCKETS)), pl.BlockSpec((REL_BUCKETS, tn), lambda j: (0, j)),
                  pl.BlockSpec((1, tn), lambda j: (0, j))],
        out_specs=pl.BlockSpec((ATT_HEADS, tn), lambda j: (0, j)),
        out_shape=jax.ShapeDtypeStruct((ATT_HEADS, n), F32),
        compiler_params=_cparams(("parallel",)),
    )(rel_bias_t, onehot_t, maskadd)


def bias_table_bwd(dbias, onehot_t):
    n = onehot_t.shape[1]
    tn = 8192

    def body(d_ref, o_ref, out_ref):
        @pl.when(pl.program_id(0) == 0)
        def _():
            out_ref[...] = jnp.zeros_like(out_ref)

        out_ref[...] += _dot_hi_nt(d_ref[...], o_ref[...])

    return pl.pallas_call(
        body, name="bias_table_bwd", grid=(n // tn,),
        in_specs=[pl.BlockSpec((ATT_HEADS, tn), lambda j: (0, j)), pl.BlockSpec((REL_BUCKETS, tn), lambda j: (0, j))],
        out_specs=_full((ATT_HEADS, REL_BUCKETS)),
        out_shape=jax.ShapeDtypeStruct((ATT_HEADS, REL_BUCKETS), F32),
        compiler_params=_cparams(("arbitrary",)),
    )(dbias, onehot_t)


def _fold(full, tri):
    return jnp.where(tri, full[BLK:2 * BLK], full[0:BLK])


def _unfold(folded, tri):
    return jnp.concatenate([jnp.where(tri, 0.0, folded), jnp.where(tri, folded, 0.0)], axis=0)


GROUP_HEADS = ATT_HEADS // 2
GROUP_LANES = GROUP_HEADS * BLK


def _att_group(qg, kcat, vt_cat, bias_g, sink_g, tri, no_prev):
    l = _fold(_dot(kcat, qg), tri) * (HEAD_DIM ** -0.5) + bias_g
    l = jnp.where(no_prev, NEG, l)
    m = jnp.maximum(jnp.max(l, axis=0, keepdims=True), sink_g)
    p = jnp.exp(l - m)
    es = jnp.exp(sink_g - m)
    inv = 1.0 / (jnp.sum(p, axis=0, keepdims=True) + es)
    p = p * inv
    pcat = _unfold(p, tri)
    return p, pcat, es * inv, _dot(vt_cat, pcat)


ATT_SUB = 2


def _heads_to_lanes(ref, row0, ln):
    return jnp.concatenate([ref[row0 + j * HEAD_DIM:row0 + (j + 1) * HEAD_DIM, ln] for j in range(GROUP_HEADS)], axis=1)


def _lanes_to_heads(ref, row0, ln, val):
    for j in range(GROUP_HEADS):
        ref[row0 + j * HEAD_DIM:row0 + (j + 1) * HEAD_DIM, ln] = val[:, j * BLK:(j + 1) * BLK].astype(ref.dtype)


def _kv_cat(kvp, kvc, g):
    lo = g * HEAD_DIM
    kt_cat = jnp.concatenate([kvp[lo:lo + HEAD_DIM], kvc[lo:lo + HEAD_DIM]], axis=1)
    vt_cat = jnp.concatenate([kvp[128 + lo:128 + lo + HEAD_DIM], kvc[128 + lo:128 + lo + HEAD_DIM]], axis=1)
    return kt_cat, vt_cat


def _tri_masks(n):
    row = lax.broadcasted_iota(jnp.int32, (BLK, GROUP_LANES), 0)
    query = lax.broadcasted_iota(jnp.int32, (BLK, GROUP_LANES), 1) & (BLK - 1)
    tri = row <= query
    return tri, (n == 0) & jnp.logical_not(tri)


def _split_hosted(refs, n_in, n_out, n_scratch, ex):
    na = ex.na if ex is not None else 0
    o = 0
    parts = []
    for cnt in (n_in, na, n_out, na, n_scratch, 3 if ex is not None else 0):
        parts.append(refs[o:o + cnt])
        o += cnt
    own_in, ex_in, own_out, ex_out, own_scr, ex_sems = parts
    return (own_in, own_out, own_scr), (ex_in, ex_out, ex_sems)


def _call_hosting(body, name, nsteps, in_specs, out_specs, out_shape, scratch, args, ex):
    n_in, n_out, n_scr = len(in_specs), len(out_specs), len(scratch)
    hosted = ex is not None

    def full_body(*refs):
        (own_in, own_out, own_scr), hosted_refs = _split_hosted(refs, n_in, n_out, n_scr, ex)
        if hosted:
            @pl.when(pl.program_id(0) == 0)
            def _():
                ex.start(*hosted_refs)

            @pl.when(pl.program_id(0) == max(nsteps - 4, 0))
            def _():
                ex.relay(*hosted_refs)

            @pl.when(pl.program_id(0) == nsteps - 1)
            def _():
                ex.wait(*hosted_refs)

        body(*own_in, *own_out, *own_scr)

    res = pl.pallas_call(
        full_body, name=name + "_hosting" if hosted else name, grid=(nsteps,),
        in_specs=list(in_specs) + (ex.in_specs if hosted else []),
        out_specs=list(out_specs) + (ex.out_specs if hosted else []),
        out_shape=list(out_shape) + (ex.out_shape if hosted else []),
        scratch_shapes=list(scratch) + (ex.scratch if hosted else []),
        compiler_params=_cparams(("arbitrary",)),
    )(*args, *(ex.arrays if hosted else []))
    return res[:n_out], res[n_out:]


def attn_fwd(pa, bias, sinks, ex=None):
    s = pa.shape[1]
    nsteps = s // (ATT_SUB * BLK)

    def body(pa_ref, kvp_ref, bias_ref, sink_ref, y_ref):
        for sub in range(ATT_SUB):
            n = pl.program_id(0) * ATT_SUB + sub
            ln = slice(sub * BLK, (sub + 1) * BLK)
            kvc = pa_ref[2048:2304, ln]
            kvp = kvp_ref[...] if sub == 0 else pa_ref[2048:2304, (sub - 1) * BLK:sub * BLK]
            tri, no_prev = _tri_masks(n)
            for g in range(2):
                kt_cat, vt_cat = _kv_cat(kvp, kvc, g)
                row0 = g * GROUP_HEADS * HEAD_DIM
                _, _, _, o = _att_group(_heads_to_lanes(pa_ref, row0, ln), kt_cat.astype(F32).T, vt_cat, bias_ref[g],
                                        sink_ref[g:g + 1, :], tri, no_prev)
                z = _heads_to_lanes(pa_ref, 1024 + row0, ln).astype(F32)
                _lanes_to_heads(y_ref, row0, ln, o * z * _sig(z))

    (y,), hosted = _call_hosting(
        body, "attn_fwd", nsteps,
        in_specs=[pl.BlockSpec((ATT_COLS, ATT_SUB * BLK), lambda n: (0, n)),
                  pl.BlockSpec((256, BLK), lambda n: (8, jnp.maximum(ATT_SUB * n - 1, 0))),
                  _full((2, BLK, GROUP_LANES)), _full((2, GROUP_LANES))],
        out_specs=[pl.BlockSpec((1024, ATT_SUB * BLK), lambda n: (0, n))],
        out_shape=[jax.ShapeDtypeStruct((1024, s), ACT_DTYPE)], scratch=[],
        args=(pa, pa, bias, sinks), ex=ex)
    return (y, hosted) if ex is not None else y


def attn_bwd(pa, dy, bias, sinks, dbias_in):
    s = pa.shape[1]
    nsteps = s // (ATT_SUB * BLK)

    def body(pa_ref, kvp_ref, dy_ref, bias_ref, sink_ref, dbin_ref, dpa_ref, dbias_ref, dsink_ref, carry, dsink_acc):
        i = pl.program_id(0)

        @pl.when(i == 0)
        def _():
            dbias_ref[...] = dbin_ref[...]
            dsink_acc[...] = jnp.zeros_like(dsink_acc)
            carry[...] = jnp.zeros_like(carry)

        scale = HEAD_DIM ** -0.5
        for sub in reversed(range(ATT_SUB)):
            n = (nsteps - 1 - i) * ATT_SUB + sub
            ln = slice(sub * BLK, (sub + 1) * BLK)
            kvc = pa_ref[2048:2304, ln]
            kvp = kvp_ref[...] if sub == 0 else pa_ref[2048:2304, (sub - 1) * BLK:sub * BLK]
            tri, no_prev = _tri_masks(n)
            for g in range(2):
                kt_cat, vt_cat = _kv_cat(kvp, kvc, g)
                row0 = g * GROUP_HEADS * HEAD_DIM
                qg = _heads_to_lanes(pa_ref, row0, ln)
                p, pcat, psink, o = _att_group(qg, kt_cat.astype(F32).T, vt_cat, bias_ref[g], sink_ref[g:g + 1, :],
                                               tri, no_prev)
                z = _heads_to_lanes(pa_ref, 1024 + row0, ln).astype(F32)
                dyg = _heads_to_lanes(dy_ref, row0, ln).astype(F32)
                sz = _sig(z)
                d_o = dyg * z * sz
                _lanes_to_heads(dpa_ref, 1024 + row0, ln, dyg * _dsilu(z, sz) * o)
                delta = jnp.sum(d_o * o, axis=0, keepdims=True)
                dl = p * (_fold(_dot(vt_cat.astype(F32).T, d_o), tri) - delta)
                dsink_acc[g:g + 1, :] += psink * delta
                dbias_ref[g] += dl
                dlcat = _unfold(dl, tri)
                _lanes_to_heads(dpa_ref, row0, ln, _dot(kt_cat, dlcat) * scale)
                for q, dkv in enumerate((_dot_nt(qg, dlcat) * scale, _dot_nt(d_o, pcat))):
                    r0 = q * 128 + g * HEAD_DIM
                    dpa_ref[2048 + r0:2048 + r0 + HEAD_DIM, ln] = (
                        dkv[:, BLK:2 * BLK] + carry[r0:r0 + HEAD_DIM, :]).astype(dpa_ref.dtype)
                    carry[r0:r0 + HEAD_DIM, :] = dkv[:, 0:BLK]

        @pl.when(i == nsteps - 1)
        def _():
            lane = lax.broadcasted_iota(jnp.int32, (1, 128), 1)
            dsink = jnp.zeros((1, 128), F32)
            for h in range(ATT_HEADS):
                g, j = divmod(h, GROUP_HEADS)
                tot = jnp.sum(dsink_acc[g:g + 1, j * BLK:(j + 1) * BLK], axis=1, keepdims=True)
                dsink = dsink + jnp.where(lane == h, -tot, 0.0)
            dsink_ref[...] = dsink

    return pl.pallas_call(
        body, name="attn_bwd", grid=(nsteps,),
        in_specs=[pl.BlockSpec((ATT_COLS, ATT_SUB * BLK), lambda i: (0, nsteps - 1 - i)),
                  pl.BlockSpec((256, BLK), lambda i: (8, jnp.maximum(ATT_SUB * (nsteps - 1 - i) - 1, 0))),
                  pl.BlockSpec((1024, ATT_SUB * BLK), lambda i: (0, nsteps - 1 - i)),
                  _full((2, BLK, GROUP_LANES)), _full((2, GROUP_LANES)), _full((2, BLK, GROUP_LANES))],
        out_specs=[pl.BlockSpec((ATT_COLS, ATT_SUB * BLK), lambda i: (0, nsteps - 1 - i)),
                   _full((2, BLK, GROUP_LANES)), _full((1, 128))],
        out_shape=[jax.ShapeDtypeStruct((ATT_COLS, s), ACT_DTYPE),
                   jax.ShapeDtypeStruct((2, BLK, GROUP_LANES), F32),
                   jax.ShapeDtypeStruct((1, 128), F32)],
        scratch_shapes=[pltpu.VMEM((256, BLK), F32), pltpu.VMEM((2, GROUP_LANES), F32)],
        compiler_params=_cparams(("arbitrary",)),
    )(pa, pa, dy, bias, sinks, dbias_in)


def _layernorm(v, g, b):
    mu = jnp.mean(v, axis=-1, keepdims=True)
    vc = v - mu
    rstd = lax.rsqrt(jnp.mean(vc * vc, axis=-1, keepdims=True) + EPS)
    xhat = vc * rstd
    return xhat, rstd, xhat * g + b


def sgu_fwd(ps, ln_g, ln_b, w_tril, b_t, ex=None):
    s = ps.shape[0]
    rows = min(4 * BLK, s)

    def body(ps_ref, g_ref, b_ref, w_ref, bt_ref, y_ref):
        u = ps_ref[:, 0:1024].astype(F32)
        v = ps_ref[:, 1024:2048].astype(F32)
        z = ps_ref[:, 2048:3072].astype(F32)
        _, _, vn = _layernorm(v, g_ref[...], b_ref[...])
        gate = u * z * _sig(z)
        for c in range(rows // BLK):
            ch = slice(c * BLK, (c + 1) * BLK)
            for g in range(SG_GROUPS):
                sl = slice(g * 128, (g + 1) * 128)
                mixed = _dot(w_ref[g], vn[ch, sl]) + bt_ref[:, g:g + 1]
                y_ref[ch, sl] = (gate[ch, sl] * mixed).astype(y_ref.dtype)

    (y,), hosted = _call_hosting(
        body, "sgu_fwd", s // rows,
        in_specs=[pl.BlockSpec((rows, SG_COLS), lambda c: (c, 0)), _full((1, 1024)), _full((1, 1024)),
                  _full((SG_GROUPS, BLK, BLK)), _full((BLK, 128))],
        out_specs=[pl.BlockSpec((rows, 1024), lambda c: (c, 0))],
        out_shape=[jax.ShapeDtypeStruct((s, 1024), ACT_DTYPE)], scratch=[],
        args=(ps, ln_g, ln_b, w_tril, b_t), ex=ex)
    return (y, hosted) if ex is not None else y


def sgu_bwd(ps, dy, ln_g, ln_b, w_tril, w_tril_t, b_t):
    s = ps.shape[0]
    rows = min(2 * BLK, s)

    def body(ps_ref, dy_ref, g_ref, b_ref, w_ref, wt_ref, bt_ref, dps_ref, dw_ref, dbt_ref, dg_ref, db_ref, dvn_scr):
        @pl.when(pl.program_id(0) == 0)
        def _():
            dw_ref[...] = jnp.zeros_like(dw_ref)
            dbt_ref[...] = jnp.zeros_like(dbt_ref)
            dg_ref[...] = jnp.zeros_like(dg_ref)
            db_ref[...] = jnp.zeros_like(db_ref)

        u = ps_ref[:, 0:1024].astype(F32)
        v = ps_ref[:, 1024:2048].astype(F32)
        z = ps_ref[:, 2048:3072].astype(F32)
        dy = dy_ref[...].astype(F32)
        xhat, rstd, vn = _layernorm(v, g_ref[...], b_ref[...])
        sz = _sig(z)
        silu = z * sz
        row = lax.broadcasted_iota(jnp.int32, (BLK, BLK), 0)
        colm = lax.broadcasted_iota(jnp.int32, (BLK, BLK), 1)
        tril = row >= colm
        dbt = jnp.zeros((BLK, 128), F32)
        dsilu_z = _dsilu(z, sz)
        for c in range(rows // BLK):
            ch = slice(c * BLK, (c + 1) * BLK)
            for g in range(SG_GROUPS):
                sl = slice(g * 128, (g + 1) * 128)
                vng = vn[ch, sl]
                mixed = _dot(w_ref[g], vng) + bt_ref[:, g:g + 1]
                dyg, ug = dy[ch, sl], u[ch, sl]
                dps_ref[ch, sl] = (dyg * mixed * silu[ch, sl]).astype(dps_ref.dtype)
                dps_ref[ch, 2048 + g * 128:2048 + (g + 1) * 128] = (
                    dyg * ug * mixed * dsilu_z[ch, sl]).astype(dps_ref.dtype)
                dm = dyg * ug * silu[ch, sl]
                dw_ref[g] += jnp.where(tril, _dot_nt(dm, vng), 0.0)
                dbt = dbt + jnp.where(colm == g, jnp.sum(dm, axis=1, keepdims=True), 0.0)
                dvn_scr[ch, sl] = _dot(wt_ref[g], dm)
        dbt_ref[...] += dbt
        dvn = dvn_scr[...]
        dg_ref[...] += jnp.sum(dvn * xhat, axis=0, keepdims=True)
        db_ref[...] += jnp.sum(dvn, axis=0, keepdims=True)
        dxh = dvn * g_ref[...]
        dv = rstd * (dxh - jnp.mean(dxh, axis=-1, keepdims=True)
                     - xhat * jnp.mean(dxh * xhat, axis=-1, keepdims=True))
        dps_ref[:, 1024:2048] = dv.astype(dps_ref.dtype)

    return pl.pallas_call(
        body, name="sgu_bwd", grid=(s // rows,),
        in_specs=[pl.BlockSpec((rows, SG_COLS), lambda c: (c, 0)), pl.BlockSpec((rows, 1024), lambda c: (c, 0)),
                  _full((1, 1024)), _full((1, 1024)), _full((SG_GROUPS, BLK, BLK)), _full((SG_GROUPS, BLK, BLK)),
                  _full((BLK, 128))],
        out_specs=[pl.BlockSpec((rows, SG_COLS), lambda c: (c, 0)), _full((SG_GROUPS, BLK, BLK)), _full((BLK, 128)),
                   _full((1, 1024)), _full((1, 1024))],
        out_shape=[jax.ShapeDtypeStruct((s, SG_COLS), ACT_DTYPE), jax.ShapeDtypeStruct((SG_GROUPS, BLK, BLK), F32),
                   jax.ShapeDtypeStruct((BLK, 128), F32), jax.ShapeDtypeStruct((1, 1024), F32),
                   jax.ShapeDtypeStruct((1, 1024), F32)],
        scratch_shapes=[pltpu.VMEM((rows, 1024), F32)],
        compiler_params=_cparams(("arbitrary",)),
    )(ps, dy, ln_g, ln_b, w_tril, w_tril_t, b_t)


def _shift_down(cur, prev16, k):
    if k == 0:
        return cur
    r = pltpu.roll(cur, k, 0)
    rp = pltpu.roll(prev16, k, 0)
    row = lax.broadcasted_iota(jnp.int32, (8, cur.shape[1]), 0)
    return jnp.concatenate([jnp.where(row < k, rp[0:8], r[0:8]), r[8:]], axis=0)


def _shift_up(cur, next16, k):
    if k == 0:
        return cur
    n = cur.shape[0]
    r = pltpu.roll(cur, n - k, 0)
    rn = pltpu.roll(next16, 16 - k, 0)
    row = lax.broadcasted_iota(jnp.int32, (8, cur.shape[1]), 0)
    return jnp.concatenate([r[:n - 8], jnp.where(row >= 8 - k, rn[8:16], r[n - 8:])], axis=0)


def _bcast8(v):
    return jnp.broadcast_to(v, (16, v.shape[1]))


def _causal_conv(xbc, prev16, cw, cbias):
    pre = cbias + cw[3:4] * xbc
    for k in (1, 2, 3):
        pre = pre + cw[3 - k:4 - k] * _shift_down(xbc, prev16, k)
    return pre


class _Ssd:
    def __init__(self, pre, dtr, dtb, alog, dsk, tri, e):
        self.pre = pre
        self.sg = _sig(pre)
        act = pre * self.sg
        self.xs = act[:, 0:SSM_WIDTH]
        self.bm = act[:, SSM_WIDTH:SSM_WIDTH + 512]
        self.cm = act[:, SSM_WIDTH + 512:CONV_DIM]
        self.dtp = dtr + dtb
        self.dt = jnp.maximum(self.dtp, 0.0) + jnp.log(1.0 + jnp.exp(-jnp.abs(self.dtp)))
        self.a = -jnp.exp(alog)
        self.acs = _dot_hi(tri, self.dt * self.a)
        self.acs_t = self.acs.T
        tot = self.acs[BLK - 1:BLK]
        self.ecs = jnp.exp(self.acs)
        self.dte = jnp.exp(tot - self.acs)
        self.cd = jnp.exp(tot)
        self.dt_x = _dot_onehot(self.dt, e)
        self.ecs_x = _dot_onehot(self.ecs, e)
        self.dte_x = _dot_onehot(self.dte, e)
        self.cd_x = _dot_onehot(_bcast8(self.cd), e)[0:1]
        self.d_x = _dot_onehot(_bcast8(dsk), e)[0:1]
        self.xdt = self.xs * self.dt_x
        row = lax.broadcasted_iota(jnp.int32, (BLK, BLK), 0)
        col = lax.broadcasted_iota(jnp.int32, (BLK, BLK), 1)
        self.tril = row >= col

    def group(self, g):
        sl = slice(g * 128, (g + 1) * 128)
        bg, cg = self.bm[:, sl], self.cm[:, sl]
        return bg, cg, _dot_nt(cg, bg)

    def decay(self, h):
        seg = self.acs[:, h:h + 1] - self.acs_t[h:h + 1, :]
        return jnp.exp(jnp.where(self.tril, seg, NEG))

    def y_pre_gate(self, ht_of, yd_scr, yoff_scr):
        for g in range(SSM_GROUPS):
            bg, cg, cb = self.group(g)
            for j in range(8):
                h = g * 8 + j
                sl = slice(h * 64, (h + 1) * 64)
                yd_scr[:, sl] = _dot(cb * self.decay(h), self.xdt[:, sl])
            gs = slice(g * SSM_GW, (g + 1) * SSM_GW)
            yoff_scr[:, gs] = _dot(cg, ht_of(g)) * self.ecs_x[:, gs]
        return yd_scr[...] + yoff_scr[...] + self.d_x * self.xs


def _ssd_consts():
    hh = lax.broadcasted_iota(jnp.int32, (128, SSM_WIDTH), 0)
    ch = lax.broadcasted_iota(jnp.int32, (128, SSM_WIDTH), 1)
    e = (ch // 64 == hh).astype(jnp.bfloat16)
    row = lax.broadcasted_iota(jnp.int32, (BLK, BLK), 0)
    col = lax.broadcasted_iota(jnp.int32, (BLK, BLK), 1)
    tri = (row >= col).astype(F32)
    return tri, e


def _pad_lanes(v, n=128):
    return jnp.pad(v, ((0, 0), (0, n - v.shape[1])))


def ssd_fwd(pm, cw, cbias, dtb, alog, dsk, ng, ex=None):
    s = pm.shape[0]
    nc = s // BLK
    tri, e = _ssd_consts()

    def body(pm_ref, prev_ref, cw_ref, cb_ref, dtb_ref, al_ref, d_ref, ng_ref, tri_ref, e_ref,
             y_ref, st_ref, pre_ref, ht_ref, yd_scr, yoff_scr):
        c = pl.program_id(0)

        @pl.when(c == 0)
        def _():
            ht_ref[...] = jnp.zeros_like(ht_ref)

        xbc = pm_ref[:, 0:CONV_DIM].astype(F32)
        prev16 = jnp.where(c == 0, 0.0, prev_ref[...].astype(F32))
        pre = _causal_conv(xbc, prev16, cw_ref[...], cb_ref[...])
        pre_ref[...] = pre.astype(pre_ref.dtype)
        f = _Ssd(pre, pm_ref[:, DT_OFF:DT_OFF + 128].astype(F32), dtb_ref[...], al_ref[...], d_ref[...],
                 tri_ref[...], e_ref[...])
        st_ref[0] = ht_ref[...]
        y = f.y_pre_gate(lambda g: ht_ref[g], yd_scr, yoff_scr)
        for g in range(SSM_GROUPS):
            bg, _, _ = f.group(g)
            gs = slice(g * SSM_GW, (g + 1) * SSM_GW)
            ht_ref[g] = ht_ref[g] * f.cd_x[:, gs] + _dot_tn(bg, f.xdt[:, gs] * f.dte_x[:, gs])
        z = pm_ref[:, CONV_DIM:CONV_DIM + SSM_WIDTH].astype(F32)
        ypre = y * z * _sig(z)
        for g in range(SSM_GROUPS):
            gs = slice(g * SSM_GW, (g + 1) * SSM_GW)
            yg = ypre[:, gs]
            rr = lax.rsqrt(jnp.mean(yg * yg, axis=-1, keepdims=True) + EPS)
            y_ref[:, gs] = (yg * rr * ng_ref[:, gs]).astype(y_ref.dtype)

    own, hosted = _call_hosting(
        body, "ssd_fwd", nc,
        in_specs=[pl.BlockSpec((BLK, SSM_COLS), lambda c: (c, 0)),
                  pl.BlockSpec((16, CONV_DIM), lambda c: (jnp.maximum(8 * c - 1, 0), 0)),
                  _full((4, CONV_DIM)), _full((1, CONV_DIM)), _full((1, 128)), _full((1, 128)), _full((1, 128)),
                  _full((1, SSM_WIDTH)), _full((BLK, BLK)), _full((128, SSM_WIDTH))],
        out_specs=[pl.BlockSpec((BLK, SSM_WIDTH), lambda c: (c, 0)),
                   pl.BlockSpec((1, SSM_GROUPS, 128, SSM_GW), lambda c: (c, 0, 0, 0)),
                   pl.BlockSpec((BLK, CONV_DIM), lambda c: (c, 0))],
        out_shape=[jax.ShapeDtypeStruct((s, SSM_WIDTH), ACT_DTYPE),
                   jax.ShapeDtypeStruct((nc, SSM_GROUPS, 128, SSM_GW), F32),
                   jax.ShapeDtypeStruct((s, CONV_DIM), ACT_DTYPE)],
        scratch=[pltpu.VMEM((SSM_GROUPS, 128, SSM_GW), F32), pltpu.VMEM((BLK, SSM_WIDTH), F32),
                 pltpu.VMEM((BLK, SSM_WIDTH), F32)],
        args=(pm, pm, cw, cbias, dtb, alog, dsk, ng, tri, e), ex=ex)
    return (*own, hosted) if ex is not None else tuple(own)


def ssd_bwd(pm, pre, dy, states, cw, dtb, alog, dsk, ng, ex=None):
    s = pm.shape[0]
    nc = s // BLK
    tri, e = _ssd_consts()
    tri_t, e_t = tri.T, e.T

    def body(pm_ref, pre_ref, dy_ref, st_ref, cw_ref, dtb_ref, al_ref, d_ref, ng_ref,
             tri_ref, trit_ref, e_ref, et_ref,
             dpm_ref, dcw_ref, dcb_ref, dvec_ref, dng_ref,
             dht_ref, dcar_ref, yd_scr, yoff_scr, dx_scr, r2_scr, hs_scr, da_scr, dat_scr, dd_scr, dbc_scr):
        i = pl.program_id(0)
        n = nc - 1 - i

        @pl.when(i == 0)
        def _():
            dht_ref[...] = jnp.zeros_like(dht_ref)
            dcar_ref[...] = jnp.zeros_like(dcar_ref)
            dcw_ref[...] = jnp.zeros_like(dcw_ref)
            dcb_ref[...] = jnp.zeros_like(dcb_ref)
            dvec_ref[...] = jnp.zeros_like(dvec_ref)
            dng_ref[...] = jnp.zeros_like(dng_ref)
            dd_scr[...] = jnp.zeros_like(dd_scr)
            da_scr[...] = jnp.zeros_like(da_scr)
            dat_scr[...] = jnp.zeros_like(dat_scr)

        cw = cw_ref[...]
        f = _Ssd(pre_ref[...].astype(F32), pm_ref[:, DT_OFF:DT_OFF + 128].astype(F32), dtb_ref[...], al_ref[...],
                 d_ref[...], tri_ref[...], e_ref[...])
        et = et_ref[...]
        y = f.y_pre_gate(lambda g: st_ref[0, g], yd_scr, yoff_scr)

        z = pm_ref[:, CONV_DIM:CONV_DIM + SSM_WIDTH].astype(F32)
        dyv = dy_ref[...].astype(F32)
        sz = _sig(z)
        silu = z * sz
        ypre = y * silu
        for g in range(SSM_GROUPS):
            gs = slice(g * SSM_GW, (g + 1) * SSM_GW)
            yg = ypre[:, gs]
            rr = lax.rsqrt(jnp.mean(yg * yg, axis=-1, keepdims=True) + EPS)
            nrm = yg * rr
            dng_ref[:, gs] += jnp.sum(dyv[:, gs] * nrm, axis=0, keepdims=True)
            dn = dyv[:, gs] * ng_ref[:, gs]
            dx_scr[:, gs] = rr * (dn - nrm * jnp.mean(dn * nrm, axis=-1, keepdims=True))
        dypre = dx_scr[...]
        d_y = dypre * silu
        dpm_ref[:, CONV_DIM:CONV_DIM + SSM_WIDTH] = (dypre * y * _dsilu(z, sz)).astype(dpm_ref.dtype)

        for g in range(SSM_GROUPS):
            bg, cg, cb = f.group(g)
            gs = slice(g * SSM_GW, (g + 1) * SSM_GW)
            htg = st_ref[0, g]
            dhn = dht_ref[g]
            dcb = jnp.zeros((BLK, BLK), F32)
            for j in range(8):
                h = g * 8 + j
                sl = slice(h * 64, (h + 1) * 64)
                dec = f.decay(h)
                dyh = d_y[:, sl]
                dmd = _dot_nt(dyh, f.xdt[:, sl]) * dec
                dcb = dcb + dmd
                gm = dmd * cb
                da_scr[:, h:h + 1] = jnp.sum(gm, axis=1, keepdims=True)
                dat_scr[h:h + 1, :] = jnp.sum(gm, axis=0, keepdims=True)
                dx_scr[:, sl] = _dot_tn(cb * dec, dyh)
            dz = f.ecs_x[:, gs] * d_y[:, gs]
            dbc_scr[:, 512 + g * 128:512 + (g + 1) * 128] = _dot(dcb, bg) + _dot_nt(dz, htg)
            dbc_scr[:, g * 128:(g + 1) * 128] = _dot_tn(dcb, cg) + _dot_nt(f.xdt[:, gs] * f.dte_x[:, gs], dhn)
            dws = _dot(bg, dhn)
            dx_scr[:, gs] += f.dte_x[:, gs] * dws
            r2_scr[:, gs] = dws * f.xdt[:, gs]
            hs_scr[:, gs] = _bcast8(jnp.sum(dhn * htg, axis=0, keepdims=True))
            dht_ref[g] = f.cd_x[:, gs] * dhn + _dot_tn(cg, dz)
        d_x = dx_scr[...]
        r1 = _dot(d_y * yoff_scr[...], et)
        r2 = _dot(r2_scr[...], et) * f.dte
        dcd = _dot_onehot(hs_scr[...], et)[0:1]
        d_tot = jnp.sum(r2, axis=0, keepdims=True) + dcd * f.cd
        row = lax.broadcasted_iota(jnp.int32, (BLK, 128), 0)
        d_a = da_scr[...] - dat_scr[...].T + r1 - r2 + jnp.where(row == BLK - 1, d_tot, 0.0)
        dadt = _dot_hi(trit_ref[...], d_a)
        ddt = dadt * f.a + _dot(d_x * f.xs, et)
        lane = lax.broadcasted_iota(jnp.int32, (BLK, 128), 1)
        dr = jnp.where(lane < SSM_HEADS, ddt * _sig(f.dtp), 0.0)
        dvec_ref[0:1, :] += jnp.sum(dr, axis=0, keepdims=True)
        dvec_ref[1:2, :] += jnp.sum(dadt * f.dt, axis=0, keepdims=True) * f.a
        dd_scr[...] += _bcast8(jnp.sum(d_y * f.xs, axis=0, keepdims=True))
        dpm_ref[:, DT_OFF:DT_OFF + 128] = dr.astype(dpm_ref.dtype)
        dpm_ref[:, DT_OFF + 128:SSM_COLS] = jnp.zeros((BLK, 128), dpm_ref.dtype)

        dxs = d_x * f.dt_x + f.d_x * d_y
        dact = jnp.concatenate([dxs, dbc_scr[...]], axis=1)
        dpre = dact * _dsilu(f.pre, f.sg)
        dcb_ref[...] += jnp.sum(dpre, axis=0, keepdims=True)
        xbc = pm_ref[:, 0:CONV_DIM].astype(F32)
        dxraw = jnp.zeros((BLK, CONV_DIM), F32)
        nxt = dcar_ref[...]
        for k in range(4):
            ahead = _shift_up(dpre, nxt, k)
            dcw_ref[3 - k:4 - k, :] += jnp.sum(ahead * xbc, axis=0, keepdims=True)
            dxraw = dxraw + cw[3 - k:4 - k] * ahead
        dcar_ref[...] = dpre[0:16]
        dpm_ref[:, 0:CONV_DIM] = dxraw.astype(dpm_ref.dtype)

        @pl.when(i == nc - 1)
        def _():
            dvec_ref[2:3, :] = _dot_onehot(dd_scr[...], et)[0:1]

    own, hosted = _call_hosting(
        body, "ssd_bwd", nc,
        in_specs=[pl.BlockSpec((BLK, SSM_COLS), lambda i: (nc - 1 - i, 0)),
                  pl.BlockSpec((BLK, CONV_DIM), lambda i: (nc - 1 - i, 0)),
                  pl.BlockSpec((BLK, SSM_WIDTH), lambda i: (nc - 1 - i, 0)),
                  pl.BlockSpec((1, SSM_GROUPS, 128, SSM_GW), lambda i: (nc - 1 - i, 0, 0, 0)),
                  _full((4, CONV_DIM)), _full((1, 128)), _full((1, 128)), _full((1, 128)),
                  _full((1, SSM_WIDTH)), _full((BLK, BLK)), _full((BLK, BLK)), _full((128, SSM_WIDTH)),
                  _full((SSM_WIDTH, 128))],
        out_specs=[pl.BlockSpec((BLK, SSM_COLS), lambda i: (nc - 1 - i, 0)),
                   _full((8, CONV_DIM)), _full((1, CONV_DIM)), _full((8, 128)), _full((1, SSM_WIDTH))],
        out_shape=[jax.ShapeDtypeStruct((s, SSM_COLS), ACT_DTYPE), jax.ShapeDtypeStruct((8, CONV_DIM), F32),
                   jax.ShapeDtypeStruct((1, CONV_DIM), F32), jax.ShapeDtypeStruct((8, 128), F32),
                   jax.ShapeDtypeStruct((1, SSM_WIDTH), F32)],
        scratch=[pltpu.VMEM((SSM_GROUPS, 128, SSM_GW), F32), pltpu.VMEM((16, CONV_DIM), F32),
                 pltpu.VMEM((BLK, SSM_WIDTH), F32), pltpu.VMEM((BLK, SSM_WIDTH), F32),
                 pltpu.VMEM((BLK, SSM_WIDTH), F32), pltpu.VMEM((BLK, SSM_WIDTH), F32),
                 pltpu.VMEM((16, SSM_WIDTH), F32), pltpu.VMEM((BLK, 128), F32), pltpu.VMEM((128, BLK), F32),
                 pltpu.VMEM((16, SSM_WIDTH), F32), pltpu.VMEM((BLK, 1024), F32)],
        args=(pm, pre, dy, states, cw, dtb, alog, dsk, ng, tri, tri_t, e, e_t), ex=ex)
    return (*own, hosted) if ex is not None else tuple(own)


def merge_fwd(x, ya, ys, ym, pg, wa, ws, wm, wo, g_post):
    s, d = x.shape
    tm = min(256, s)

    def body(x_ref, ya_ref, ys_ref, ym_ref, pg_ref, wa_ref, ws_ref, wm_ref, wo_ref, g_ref,
             xo_ref, ba_ref, bs_ref, bm_ref, mg_ref, out_ref):
        ba = _dot_tn(ya_ref[...], wa_ref[...])
        bs = _dot(ys_ref[...], ws_ref[...])
        bm = _dot(ym_ref[...], wm_ref[...])
        merged = (_sig(pg_ref[:, 0:d].astype(F32)) * ba + _sig(pg_ref[:, d:2 * d].astype(F32)) * bs
                  + _sig(pg_ref[:, 2 * d:3 * d].astype(F32)) * bm)
        out = _dot(merged, wo_ref[...])
        r = lax.rsqrt(jnp.mean(out * out, axis=-1, keepdims=True) + EPS)
        xo_ref[...] = x_ref[...] + out * r * g_ref[...]
        ba_ref[...] = ba.astype(ba_ref.dtype)
        bs_ref[...] = bs.astype(bs_ref.dtype)
        bm_ref[...] = bm.astype(bm_ref.dtype)
        mg_ref[...] = merged.astype(mg_ref.dtype)
        out_ref[...] = out.astype(out_ref.dtype)

    rows = lambda w: pl.BlockSpec((tm, w), lambda i: (i, 0))
    act = jax.ShapeDtypeStruct((s, d), ACT_DTYPE)
    return pl.pallas_call(
        body, name="merge_fwd", grid=(s // tm,),
        in_specs=[rows(d), pl.BlockSpec((d, tm), lambda i: (0, i)), rows(d), rows(2 * d), rows(3 * d), _full((d, d)),
                  _full((d, d)), _full((2 * d, d)), _full((d, d)), _full((1, d))],
        out_specs=[rows(d)] * 6,
        out_shape=[jax.ShapeDtypeStruct((s, d), F32), act, act, act, act, act],
        compiler_params=_cparams(("parallel",)),
    )(x, ya, ys, ym, pg, wa, ws, wm, wo, g_post)


def merge_bwd(dx, out_s, pg, ba, bs, bm, wa, ws, wm, wo, g_post):
    s, d = dx.shape
    tm = min(256, s)

    def body(dx_ref, out_ref, pg_ref, ba_ref, bs_ref, bm_ref, wa_ref, ws_ref, wm_ref, wo_ref, g_ref,
             dout_ref, dba_ref, dbs_ref, dbm_ref, dpg_ref, dya_ref, dys_ref, dym_ref, dg_ref):
        @pl.when(pl.program_id(0) == 0)
        def _():
            dg_ref[...] = jnp.zeros_like(dg_ref)

        o = out_ref[...].astype(F32)
        dxv = dx_ref[...]
        r = lax.rsqrt(jnp.mean(o * o, axis=-1, keepdims=True) + EPS)
        nrm = o * r
        dg_ref[...] += jnp.sum(dxv * nrm, axis=0, keepdims=True)
        dn = dxv * g_ref[...]
        dout = r * (dn - nrm * jnp.mean(dn * nrm, axis=-1, keepdims=True))
        dout_ref[...] = dout.astype(dout_ref.dtype)
        dmerged = _dot_nt(dout, wo_ref[...])
        for q, (b_ref, db_ref, w_ref, dy_ref) in enumerate(((ba_ref, dba_ref, wa_ref, dya_ref),
                                                            (bs_ref, dbs_ref, ws_ref, dys_ref),
                                                            (bm_ref, dbm_ref, wm_ref, dym_ref))):
            gt = _sig(pg_ref[:, q * d:(q + 1) * d].astype(F32))
            db = dmerged * gt
            db_ref[...] = db.astype(db_ref.dtype)
            dpg_ref[:, q * d:(q + 1) * d] = (dmerged * b_ref[...].astype(F32) * gt * (1.0 - gt)).astype(dpg_ref.dtype)
            if q == 0:
                dy_ref[...] = _dot_nt(w_ref[...], db).astype(dy_ref.dtype)
            else:
                dy_ref[...] = _dot_nt(db, w_ref[...]).astype(dy_ref.dtype)

    rows = lambda w: pl.BlockSpec((tm, w), lambda i: (i, 0))
    act = lambda w: jax.ShapeDtypeStruct((s, w), ACT_DTYPE)
    return pl.pallas_call(
        body, name="merge_bwd", grid=(s // tm,),
        in_specs=[rows(d), rows(d), rows(3 * d), rows(d), rows(d), rows(d), _full((d, d)), _full((d, d)),
                  _full((2 * d, d)), _full((d, d)), _full((1, d))],
        out_specs=[rows(d), rows(d), rows(d), rows(d), rows(3 * d), pl.BlockSpec((d, tm), lambda i: (0, i)), rows(d),
                   rows(2 * d), _full((1, d))],
        out_shape=[act(d), act(d), act(d), act(d), act(3 * d), jax.ShapeDtypeStruct((d, s), ACT_DTYPE), act(d),
                   act(2 * d), jax.ShapeDtypeStruct((1, d), F32)],
        compiler_params=_cparams(("arbitrary",)),
    )(dx, out_s, pg, ba, bs, bm, wa, ws, wm, wo, g_post)


def loss_grad(y, target):
    s, d = y.shape
    tm = min(512, s)

    def body(y_ref, t_ref, dy_ref, l_ref):
        @pl.when(pl.program_id(0) == 0)
        def _():
            l_ref[...] = jnp.zeros_like(l_ref)

        err = y_ref[...] - t_ref[...]
        dy_ref[...] = err * (1.0 / d)
        part = jnp.sum(jnp.sum(err * err, axis=-1, keepdims=True) * (1.0 / d), axis=0, keepdims=True)
        l_ref[...] += 0.5 * jnp.broadcast_to(part, l_ref.shape)

    return pl.pallas_call(
        body, name="loss_grad", grid=(s // tm,),
        in_specs=[pl.BlockSpec((tm, d), lambda i: (i, 0)), pl.BlockSpec((tm, d), lambda i: (i, 0))],
        out_specs=[pl.BlockSpec((tm, d), lambda i: (i, 0)), _full((8, 128))],
        out_shape=[jax.ShapeDtypeStruct((s, d), F32), jax.ShapeDtypeStruct((8, 128), F32)],
        compiler_params=_cparams(("arbitrary",)),
    )(y, target)


def _mesh_pos():
    x, y, c = lax.axis_index("x"), lax.axis_index("y"), lax.axis_index("c")
    return x, y, c, 4 * x + 2 * y + c


def _peer(x, y, c, k):
    px = 1 - x if k & 4 else x
    py = 1 - y if k & 2 else y
    pc = 1 - c if k & 1 else c
    return (px, py, pc), 4 * px + 2 * py + pc


class Exchange:
    SAME_CORE = (2, 4, 6)

    def __init__(self, scattered, gathered):
        self.ns = len(scattered)
        self.arrays = list(scattered) + list(gathered)
        self.na = len(self.arrays)
        any_spec = pl.BlockSpec(memory_space=pl.ANY)
        self.in_specs = [any_spec] * self.na
        self.out_specs = [any_spec] * self.na
        self.out_shape = ([jax.ShapeDtypeStruct(a.shape, a.dtype) for a in scattered]
                          + [jax.ShapeDtypeStruct((N_DEV,) + a.shape, a.dtype) for a in gathered])
        self.scratch = [pltpu.SemaphoreType.DMA((self.na, N_DEV - 1)), pltpu.SemaphoreType.DMA((self.na, N_DEV - 1)),
                        pltpu.SemaphoreType.DMA((self.na,))]

    def _src(self, ins, q, slot):
        return ins[q].at[slot] if q < self.ns else ins[q]

    def _local(self, ins, outs, sems):
        me = _mesh_pos()[3]
        return [pltpu.make_async_copy(self._src(ins, q, me), outs[q].at[me], sems[2].at[q]) for q in range(self.na)]

    def _direct(self, ins, outs, sems, relations, arrays):
        x, y, c, me = _mesh_pos()
        copies = []
        for k in relations:
            peer, pidx = _peer(x, y, c, k)
            for q in arrays:
                copies.append(pltpu.make_async_remote_copy(
                    src_ref=self._src(ins, q, pidx), dst_ref=outs[q].at[me], send_sem=sems[0].at[q, k - 1],
                    recv_sem=sems[1].at[q, k - 1], device_id=peer, device_id_type=MESH))
        return copies

    def _arrivals(self, ins, outs, sems, relations, arrays):
        x, y, c, _ = _mesh_pos()
        copies = []
        for k in relations:
            peer, pidx = _peer(x, y, c, k)
            for q in arrays:
                copies.append(pltpu.make_async_remote_copy(
                    src_ref=self._src(ins, q, pidx), dst_ref=outs[q].at[pidx], send_sem=sems[0].at[q, k - 1],
                    recv_sem=sems[1].at[q, k - 1], device_id=peer, device_id_type=MESH))
        return copies

    def _relays(self, outs, sems):
        x, y, c, _ = _mesh_pos()
        sibling, _ = _peer(x, y, c, 1)
        copies = []
        for k in self.SAME_CORE:
            _, pidx = _peer(x, y, c, k)
            for q in range(self.ns, self.na):
                copies.append(pltpu.make_async_remote_copy(
                    src_ref=outs[q].at[pidx], dst_ref=outs[q].at[pidx], send_sem=sems[0].at[q, k],
                    recv_sem=sems[1].at[q, k], device_id=sibling, device_id_type=MESH))
        return copies

    def _sends(self, ins, outs, sems):
        return (self._direct(ins, outs, sems, range(1, N_DEV), range(self.ns))
                + self._direct(ins, outs, sems, (1,) + self.SAME_CORE, range(self.ns, self.na)))

    def start(self, ins, outs, sems):
        for cp in self._local(ins, outs, sems) + self._sends(ins, outs, sems):
            cp.start()

    def relay(self, ins, outs, sems):
        for cp in self._arrivals(ins, outs, sems, self.SAME_CORE, range(self.ns, self.na)):
            cp.wait_recv()
        for cp in self._relays(outs, sems):
            cp.start()

    def wait(self, ins, outs, sems):
        for cp in (self._arrivals(ins, outs, sems, range(1, N_DEV), range(self.ns))
                   + self._arrivals(ins, outs, sems, (1, 3, 5, 7), range(self.ns, self.na))):
            cp.wait_recv()
        for cp in self._sends(ins, outs, sems) + self._relays(outs, sems):
            cp.wait_send()
        for cp in self._local(ins, outs, sems):
            cp.wait()


def exchange(scattered, gathered, name):
    ex = Exchange(scattered, gathered)

    def body(*refs):
        ins, outs, sems = refs[:ex.na], refs[ex.na:2 * ex.na], refs[2 * ex.na:]
        ex.start(ins, outs, sems)
        ex.relay(ins, outs, sems)
        ex.wait(ins, outs, sems)

    return pl.pallas_call(body, name=name, in_specs=ex.in_specs, out_specs=ex.out_specs, out_shape=ex.out_shape,
                          scratch_shapes=ex.scratch)(*ex.arrays)


def adamw(parts_list, w, m, v, tile, name, ex=None):
    npart, _, dp = parts_list[0].shape
    d = w.shape[-1]
    counts = [p.shape[1] // tile for p in parts_list]
    starts = [sum(counts[:q]) for q in range(len(counts))]
    n_lists = len(parts_list)

    def body(*refs):
        p_refs = refs[:n_lists]
        w_ref, m_ref, v_ref, g_ref, dw_ref, nm_ref, nv_ref = refs[n_lists:]
        i = pl.program_id(0)
        for q, p_ref in enumerate(p_refs):
            @pl.when((i >= starts[q]) & (i < starts[q] + counts[q]))
            def _(p_ref=p_ref):
                acc = p_ref[0, :, 0:d].astype(F32)
                for k in range(1, npart):
                    acc = acc + p_ref[k, :, 0:d].astype(F32)
                g_ref[...] = acc

        g = g_ref[...]
        nm = ADAM_B1 * m_ref[...] + (1.0 - ADAM_B1) * g
        nv = ADAM_B2 * v_ref[...] + (1.0 - ADAM_B2) * (g * g)
        nm_ref[...] = nm
        nv_ref[...] = nv
        m_hat = nm / (1.0 - ADAM_B1 ** ADAM_STEP)
        v_hat = nv / (1.0 - ADAM_B2 ** ADAM_STEP)
        dw_ref[...] = -ADAM_LR * (m_hat / (jnp.sqrt(v_hat) + ADAM_EPS) + ADAM_WD * w_ref[...])

    def part_rows(q):
        return lambda i: (0, jnp.clip(i - starts[q], 0, counts[q] - 1), 0)

    if w.ndim == 3:
        rows = pl.BlockSpec((None, tile, d), lambda i: (i // counts[0], i % counts[0], 0))
    else:
        rows = pl.BlockSpec((tile, d), lambda i: (i, 0))
    own, hosted = _call_hosting(
        body, name, sum(counts),
        in_specs=[pl.BlockSpec((npart, tile, dp), part_rows(q)) for q in range(n_lists)] + [rows, rows, rows],
        out_specs=[rows] * 4, out_shape=[jax.ShapeDtypeStruct(w.shape, F32)] * 4, scratch=[],
        args=(*parts_list, w, m, v), ex=ex)
    return (*own, hosted) if ex is not None else tuple(own)


def _pad_rows(a, rows):
    return jnp.pad(a, ((0, rows - a.shape[0]), (0, 0)))


def _pack_rest(w_att, w_sg, w_ssm, w_out):
    parts = []
    for l in range(2):
        parts += [w_att[l], w_sg[l], w_ssm[l], w_out[l]]
    return jnp.concatenate(parts, axis=0)


def _unpack_rest(p):
    outs = [[], [], [], []]
    o = 0
    for l in range(2):
        for q, rws in enumerate(REST_PARTS):
            outs[q].append(p[o:o + rws])
            o += rws
    return [jnp.stack(t) for t in outs]


def _pack_win(w_in):
    return jnp.pad(w_in.reshape(2 * D_MODEL, WIN_SHARD), ((0, 0), (0, WIN_LANES - WIN_SHARD)))


W_IN_MAP = ((0, 1024, "att", 0), (1024, 1280, "att", 2048), (1280, 2304, "att", 1024), (2304, 5376, "sg", 0),
            (5376, 7424, "ssm", 3072), (7424, 10496, "ssm", 0), (10496, 10528, "ssm", 5120), (10528, 13600, "gate", 0))
SLAB_COLS = {"att": ATT_COLS, "sg": SG_COLS, "ssm": SSM_COLS, "gate": GATE_COLS}


def _slabs_from_shards(g):
    slabs = {}
    for name, width in SLAB_COLS.items():
        pieces, filled = [], 0
        for ga, gb, _, off in sorted((m for m in W_IN_MAP if m[2] == name), key=lambda m: m[3]):
            assert off == filled
            a = ga
            while a < gb:
                d = a // WIN_SHARD
                hi = min(gb, WIN_SHARD * (d + 1))
                pieces.append(g[d, :, a - WIN_SHARD * d:hi - WIN_SHARD * d])
                a = hi
            filled += gb - ga
        if filled < width:
            pieces.append(jnp.zeros((D_MODEL, width - filled), g.dtype))
        slabs[name] = jnp.concatenate(pieces, axis=1)
    return slabs


def _shards_from_slabs(dslabs):
    out = []
    for d in range(N_DEV):
        a, b = WIN_SHARD * d, WIN_SHARD * (d + 1)
        pieces = []
        for ga, gb, name, off in W_IN_MAP:
            lo, hi = max(a, ga), min(b, gb)
            if lo < hi:
                pieces.append(dslabs[name][:, off + lo - ga:off + hi - ga])
        pieces.append(jnp.zeros((D_MODEL, WIN_LANES - WIN_SHARD), pieces[0].dtype))
        out.append(jnp.concatenate(pieces, axis=1).astype(WIRE_DTYPE))
    return jnp.stack(out)


SMALL_SIZES = (("norm_pre", 2048), ("norm_post", 2048), ("rel_bias", 512), ("att_sinks", 32), ("sg_ln_g", 2048),
               ("sg_ln_b", 2048), ("sg_w", 262144), ("sg_b", 2048), ("ssm_conv_b", 6144), ("ssm_dt_bias", 64),
               ("ssm_a_log", 64), ("ssm_d", 64), ("ssm_norm_g", 4096), ("conv_w_full", 24576))


def _pack_small(d):
    parts = []
    for name, size in SMALL_SIZES:
        rows = 8 * (-(-size // (8 * D_MODEL)))
        flat = d[name].reshape(-1) if name in d else jnp.zeros((size,), F32)
        parts.append(jnp.pad(flat, (0, rows * D_MODEL - size)).reshape(rows, D_MODEL))
    return _pad_rows(jnp.concatenate(parts, axis=0), SMALL_ROWS)


def _unpack_small(p, shapes):
    out, o = {}, 0
    for name, size in SMALL_SIZES:
        rows = 8 * (-(-size // (8 * D_MODEL)))
        if name in shapes:
            out[name] = p[o:o + rows].reshape(-1)[:size].reshape(shapes[name])
        o += rows
    return out


def _bucket_onehot_t():
    qi = jnp.arange(BLK, dtype=jnp.int32)[None, :]
    kj = jnp.arange(BLK, dtype=jnp.int32)[:, None]
    dd = (qi - kj) & (BLK - 1)
    in_window = dd >= 0
    max_exact = REL_BUCKETS // 2
    dist_f = jnp.maximum(dd, 1).astype(F32)
    large = max_exact + (jnp.log(dist_f / max_exact) / math.log(128 / max_exact)
                         * (REL_BUCKETS - max_exact)).astype(jnp.int32)
    large = jnp.minimum(large, REL_BUCKETS - 1)
    bucket = jnp.where(dd < max_exact, dd, large).reshape(1, -1)
    onehot_t = (bucket == jnp.arange(REL_BUCKETS, dtype=jnp.int32)[:, None]).astype(F32)
    maskadd = jnp.where(in_window, 0.0, NEG).astype(F32).reshape(1, -1)
    return onehot_t, maskadd


WEIGHTS = ['w_in', 'norm_pre', 'norm_post', 'rel_bias', 'att_sinks', 'sg_ln_g', 'sg_ln_b', 'sg_w', 'sg_b',
           'ssm_conv_w', 'ssm_conv_b', 'ssm_dt_bias', 'ssm_a_log', 'ssm_d', 'ssm_norm_g',
           'w_br_att', 'w_br_sg', 'w_br_ssm', 'w_out']
REST = ('w_br_att', 'w_br_sg', 'w_br_ssm', 'w_out')


def kernel(x, w_in, norm_pre, norm_post, rel_bias, att_sinks, sg_ln_g, sg_ln_b, sg_w, sg_b, ssm_conv_w, ssm_conv_b, ssm_dt_bias, ssm_a_log, ssm_d, ssm_norm_g, w_br_att, w_br_sg, w_br_ssm, w_out, loss_target, m_w_in, m_norm_pre, m_norm_post, m_rel_bias, m_att_sinks, m_sg_ln_g, m_sg_ln_b, m_sg_w, m_sg_b, m_ssm_conv_w, m_ssm_conv_b, m_ssm_dt_bias, m_ssm_a_log, m_ssm_d, m_ssm_norm_g, m_w_br_att, m_w_br_sg, m_w_br_ssm, m_w_out, v_w_in, v_norm_pre, v_norm_post, v_rel_bias, v_att_sinks, v_sg_ln_g, v_sg_ln_b, v_sg_w, v_sg_b, v_ssm_conv_w, v_ssm_conv_b, v_ssm_dt_bias, v_ssm_a_log, v_ssm_d, v_ssm_norm_g, v_w_br_att, v_w_br_sg, v_w_br_ssm, v_w_out):
    w = dict(w_in=w_in, norm_pre=norm_pre, norm_post=norm_post, rel_bias=rel_bias, att_sinks=att_sinks,
             sg_ln_g=sg_ln_g, sg_ln_b=sg_ln_b, sg_w=sg_w, sg_b=sg_b, ssm_conv_w=ssm_conv_w, ssm_conv_b=ssm_conv_b,
             ssm_dt_bias=ssm_dt_bias, ssm_a_log=ssm_a_log, ssm_d=ssm_d, ssm_norm_g=ssm_norm_g,
             w_br_att=w_br_att, w_br_sg=w_br_sg, w_br_ssm=w_br_ssm, w_out=w_out)
    mom = dict(w_in=m_w_in, norm_pre=m_norm_pre, norm_post=m_norm_post, rel_bias=m_rel_bias, att_sinks=m_att_sinks,
               sg_ln_g=m_sg_ln_g, sg_ln_b=m_sg_ln_b, sg_w=m_sg_w, sg_b=m_sg_b, ssm_conv_w=m_ssm_conv_w,
               ssm_conv_b=m_ssm_conv_b, ssm_dt_bias=m_ssm_dt_bias, ssm_a_log=m_ssm_a_log, ssm_d=m_ssm_d,
               ssm_norm_g=m_ssm_norm_g, w_br_att=m_w_br_att, w_br_sg=m_w_br_sg, w_br_ssm=m_w_br_ssm, w_out=m_w_out)
    var = dict(w_in=v_w_in, norm_pre=v_norm_pre, norm_post=v_norm_post, rel_bias=v_rel_bias, att_sinks=v_att_sinks,
               sg_ln_g=v_sg_ln_g, sg_ln_b=v_sg_ln_b, sg_w=v_sg_w, sg_b=v_sg_b, ssm_conv_w=v_ssm_conv_w,
               ssm_conv_b=v_ssm_conv_b, ssm_dt_bias=v_ssm_dt_bias, ssm_a_log=v_ssm_a_log, ssm_d=v_ssm_d,
               ssm_norm_g=v_ssm_norm_g, w_br_att=v_w_br_att, w_br_sg=v_w_br_sg, w_br_ssm=v_w_br_ssm, w_out=v_w_out)
    xs0 = x[0]
    target = loss_target[0]
    my_dev = 4 * lax.axis_index("x") + 2 * lax.axis_index("y") + lax.axis_index("c")

    conv_shard = _pad_rows(ssm_conv_w.reshape(-1, D_MODEL), 8)
    win_shard = _pack_win(w_in).astype(WIRE_DTYPE)
    rest_shard = _pack_rest(*[w[n] for n in REST]).astype(WIRE_DTYPE)
    layer_shards = [[win_shard[l * D_MODEL:(l + 1) * D_MODEL], rest_shard[l * LAYER_REST:(l + 1) * LAYER_REST]]
                    for l in range(2)]
    g_win0, gathered_conv = exchange([], [layer_shards[0][0], conv_shard], "all_gather")
    conv_full = gathered_conv[:, 0:3].reshape(N_DEV, 2, 4, 384).transpose(1, 2, 0, 3).reshape(2, 4, CONV_DIM)

    def set_rest(lw, g_rest):
        o = 0
        for name, rws in zip(("att", "sg", "ssm", "out"), REST_PARTS):
            lw[name] = g_rest[:, o:o + rws].reshape(N_DEV * rws, D_MODEL).astype(MXU_DTYPE)
            o += rws

    def layer_weights(l, g_win):
        slabs = _slabs_from_shards(g_win)
        lw = {"in_" + name: slab.astype(MXU_DTYPE) for name, slab in slabs.items()}
        lw["in_att"] = lw["in_att"].T
        tril = jnp.tril(jnp.ones((BLK, BLK), bool))
        sgw = jnp.where(tril[None], sg_w[l], 0.0)
        lw.update(
            g_pre=norm_pre[l][None], g_post=norm_post[l][None], sinks=jnp.repeat(att_sinks[l], BLK).reshape(2, GROUP_LANES),
            ln_g=sg_ln_g[l][None], ln_b=sg_ln_b[l][None], sgw=sgw.astype(MXU_DTYPE),
            sgw_t=sgw.transpose(0, 2, 1).astype(MXU_DTYPE), sgb_t=_pad_lanes(sg_b[l].T),
            cw=conv_full[l], cb=ssm_conv_b[l][None], dtb=_pad_lanes(ssm_dt_bias[l][None]),
            alog=_pad_lanes(ssm_a_log[l][None]), dsk=_pad_lanes(ssm_d[l][None]), ng=ssm_norm_g[l][None])
        return lw

    onehot_t, maskadd = _bucket_onehot_t()
    bias = bias_table(rel_bias.T, onehot_t, maskadd).reshape(2, GROUP_HEADS, BLK, BLK).transpose(0, 2, 1, 3)
    bias = bias.reshape(2, BLK, GROUP_LANES)

    saved = []
    xl = xs0
    layers = [layer_weights(0, g_win0)]
    for l in range(2):
        lw = layers[l]
        h = rmsnorm_fwd(xl, lw["g_pre"])
        pa = mm_nt(lw["in_att"], h, 1152, "proj_att")
        ps = mm_nn(h, lw["in_sg"], 1536, "proj_sg")
        pm = mm_nn(h, lw["in_ssm"], 1792, "proj_ssm")
        pg = mm_nn(h, lw["in_gate"], 1536, "proj_gate")
        if l == 0:
            ya, (g_rest0,) = attn_fwd(pa, bias, lw["sinks"], Exchange([], [layer_shards[0][1]]))
            set_rest(lw, g_rest0)
        else:
            ya = attn_fwd(pa, bias, lw["sinks"])
        sgu_args = (ps, lw["ln_g"], lw["ln_b"], lw["sgw"], lw["sgb_t"])
        ssd_args = (pm, lw["cw"], lw["cb"], lw["dtb"], lw["alog"], lw["dsk"], lw["ng"])
        ys = sgu_fwd(*sgu_args)
        if l == 0:
            ym, states, conv_pre, (g_win1, g_rest1) = ssd_fwd(*ssd_args, Exchange([], layer_shards[1]))
            layers.append(layer_weights(1, g_win1))
            set_rest(layers[1], g_rest1)
        else:
            ym, states, conv_pre = ssd_fwd(*ssd_args)
        x_next, ba, bs, bm, merged, out_s = merge_fwd(xl, ya, ys, ym, pg, lw["att"], lw["sg"], lw["ssm"], lw["out"],
                                                      lw["g_post"])
        saved.append(dict(x=xl, h=h, pa=pa, ps=ps, pm=pm, pg=pg, ya=ya, ys=ys, ym=ym, states=states, conv_pre=conv_pre, ba=ba, bs=bs,
                          bm=bm, merged=merged, out_s=out_s))
        xl = x_next

    dx, loss_part = loss_grad(xl, target)
    loss = lax.psum(loss_part[0, 0], ("x", "y", "c"))

    dbias = jnp.zeros((2, BLK, GROUP_LANES), F32)
    win_grads, rest_grads = [None, None], [None, None]
    small = {n: [None, None] for n in ("norm_pre", "norm_post", "att_sinks", "sg_ln_g", "sg_ln_b", "sg_w", "sg_b",
                                       "ssm_conv_b", "ssm_dt_bias", "ssm_a_log", "ssm_d", "ssm_norm_g",
                                       "conv_w_full")}
    for l in (1, 0):
        lw, sv = layers[l], saved[l]
        dout, dba, dbs, dbm, dpg, dya, dys, dym, dg_post = merge_bwd(
            dx, sv["out_s"], sv["pg"], sv["ba"], sv["bs"], sv["bm"], lw["att"], lw["sg"], lw["ssm"], lw["out"],
            lw["g_post"])
        dw_out = mm_tn(sv["merged"], dout, 1024, "dw_out")
        dw_att = mm_kn(sv["ya"], dba, 1024, "dw_br_att")
        dw_sg = mm_tn(sv["ys"], dbs, 1024, "dw_br_sg")
        dw_ssm = mm_tn(sv["ym"], dbm, 1024, "dw_br_ssm")
        rest_grads[l] = jnp.concatenate(
            [dw_att.reshape(N_DEV, 128, D_MODEL), dw_sg.reshape(N_DEV, 128, D_MODEL),
             dw_ssm.reshape(N_DEV, 256, D_MODEL), dw_out.reshape(N_DEV, 128, D_MODEL)], axis=1).astype(WIRE_DTYPE)
        dpa, dbias, dsinks = attn_bwd(sv["pa"], dya, bias, lw["sinks"], dbias)
        dps, dsgw, dsgb_t, dln_g, dln_b = sgu_bwd(sv["ps"], dys, lw["ln_g"], lw["ln_b"], lw["sgw"], lw["sgw_t"],
                                                  lw["sgb_t"])
        ssd_args = (sv["pm"], sv["conv_pre"], dym, sv["states"], lw["cw"], lw["dtb"], lw["alog"], lw["dsk"], lw["ng"])
        if l == 0:
            dpm, dcw, dcb, dvec, dng, (recv_win1, recv_rest1, recv_rest0) = ssd_bwd(
                *ssd_args, Exchange([win_grads[1], rest_grads[1], rest_grads[0]], []))
        else:
            dpm, dcw, dcb, dvec, dng = ssd_bwd(*ssd_args)
        dslabs = dict(att=mm_kn(dpa, sv["h"], 1152, "dw_in_att").T, sg=mm_tn(sv["h"], dps, 3072, "dw_in_sg"),
                      ssm=mm_tn(sv["h"], dpm, 2688, "dw_in_ssm"), gate=mm_tn(sv["h"], dpg, 3072, "dw_in_gate"))
        win_grads[l] = _shards_from_slabs(dslabs)
        dh_args = ([dpa, dps, dpm, dpg], [lw["in_att"], lw["in_sg"], lw["in_ssm"], lw["in_gate"]], sv["x"],
                   lw["g_pre"], dx)
        if l == 0:
            dx, dg_pre, (recv_win0,) = dh_norm_bwd(*dh_args, Exchange([win_grads[0]], []))
        else:
            dx, dg_pre = dh_norm_bwd(*dh_args)
        small["norm_pre"][l] = dg_pre[0]
        small["norm_post"][l] = dg_post[0]
        small["att_sinks"][l] = dsinks[0, :ATT_HEADS]
        small["sg_ln_g"][l] = dln_g[0]
        small["sg_ln_b"][l] = dln_b[0]
        small["sg_w"][l] = dsgw
        small["sg_b"][l] = dsgb_t[:, :SG_GROUPS].T
        small["ssm_conv_b"][l] = dcb[0]
        small["ssm_dt_bias"][l] = dvec[0, :SSM_HEADS]
        small["ssm_a_log"][l] = dvec[1, :SSM_HEADS]
        small["ssm_d"][l] = dvec[2, :SSM_HEADS]
        small["ssm_norm_g"][l] = dng[0]
        small["conv_w_full"][l] = dcw[0:4]
    grad_x = dx
    dbias = dbias.reshape(2, BLK, GROUP_HEADS, BLK).transpose(0, 2, 1, 3).reshape(ATT_HEADS, BLK * BLK)
    d_rel_bias = bias_table_bwd(dbias, onehot_t).T

    small_d = {n: jnp.stack(v) for n, v in small.items()}
    small_d["rel_bias"] = d_rel_bias
    *res_win, (recv_small,) = adamw([recv_win0, recv_win1], w_in, m_w_in, v_w_in, WIN_TILE, "adamw_w_in",
                                    Exchange([], [_pack_small(small_d)]))
    res_rest = adamw([recv_rest0, recv_rest1], _pack_rest(*[w[n] for n in REST]), _pack_rest(*[mom[n] for n in REST]),
                     _pack_rest(*[var[n] for n in REST]), REST_TILE, "adamw_rest")
    small_names = [n for n, _ in SMALL_SIZES if n != "conv_w_full"]
    g_s, dw_s, nm_s, nv_s = adamw([recv_small], _pack_small({n: w[n] for n in small_names}),
                                  _pack_small({n: mom[n] for n in small_names}),
                                  _pack_small({n: var[n] for n in small_names}), SMALL_TILE, "adamw_small")
    shapes = {n: w[n].shape for n in small_names}
    shapes["conv_w_full"] = (2, 4, CONV_DIM)
    g_conv_full = _unpack_small(g_s, shapes)["conv_w_full"]
    g_conv = lax.dynamic_slice_in_dim(g_conv_full, my_dev * 384, 384, axis=2)
    pack_conv = lambda a: _pad_rows(a.reshape(-1, D_MODEL), 8)
    g_c, dw_c, nm_c, nv_c = adamw([pack_conv(g_conv)[None]], pack_conv(ssm_conv_w), pack_conv(m_ssm_conv_w),
                                  pack_conv(v_ssm_conv_w), 8, "adamw_conv")

    results = {}
    for q, (tag, psm, pc) in enumerate((("grad", g_s, g_c), ("delta", dw_s, dw_c), ("new_m", nm_s, nm_c),
                                        ("new_v", nv_s, nv_c))):
        r = dict(zip(REST, _unpack_rest(res_rest[q])))
        r["w_in"] = res_win[q]
        r.update(_unpack_small(psm, {n: w[n].shape for n in small_names}))
        r["ssm_conv_w"] = pc[0:3].reshape(2, 4, 384)
        results[tag] = r
    outs = [loss, grad_x[None]]
    for tag in ("grad", "delta", "new_m", "new_v"):
        outs += [results[tag][n] for n in WEIGHTS]
    return tuple(outs)
```

```python
import math

import jax
import jax.numpy as jnp
from jax import lax
from jax.experimental import pallas as pl
from jax.experimental.pallas import tpu as pltpu

F32 = jnp.float32
MXU_DTYPE = jnp.bfloat16
ACT_DTYPE = jnp.bfloat16
WIRE_DTYPE = jnp.bfloat16
HI = lax.Precision.HIGHEST
MESH = pl.DeviceIdType.MESH

D_MODEL = 1024
N_DEV = 8
ATT_HEADS = 16
HEAD_DIM = 64
BLK = 128
SG_GROUPS = 8
SSM_WIDTH = 2048
SSM_HEADS = 32
SSM_GROUPS = 4
SSM_GW = SSM_WIDTH // SSM_GROUPS
CONV_DIM = 3072
REL_BUCKETS = 32
EPS = 1e-6
NEG = -1e30

ATT_COLS = 2304
SG_COLS = 3072
SSM_COLS = 5376
GATE_COLS = 3072
DT_OFF = 5120

VMEM_LIMIT_V7X = 56 * 2 ** 20
DH_TK = 768

ADAM_LR, ADAM_B1, ADAM_B2, ADAM_EPS, ADAM_WD, ADAM_STEP = 0.001, 0.9, 0.999, 1e-08, 0.01, 10

WIN_SHARD = 1700
WIN_LANES = 1792
REST_PARTS = (128, 128, 256, 128)
LAYER_REST = sum(REST_PARTS)
REST_TILE = 128
WIN_TILE = 128
SMALL_ROWS = 384
SMALL_TILE = 128


def _cparams(sem=None):
    return pltpu.CompilerParams(dimension_semantics=sem, vmem_limit_bytes=VMEM_LIMIT_V7X)


def _dot(a, b):
    return jnp.dot(a.astype(MXU_DTYPE), b.astype(MXU_DTYPE), preferred_element_type=F32)


def _dot_nt(a, b):
    return lax.dot_general(a.astype(MXU_DTYPE), b.astype(MXU_DTYPE), (((1,), (1,)), ((), ())),
                           preferred_element_type=F32)


def _dot_tn(a, b):
    return lax.dot_general(a.astype(MXU_DTYPE), b.astype(MXU_DTYPE), (((0,), (0,)), ((), ())),
                           preferred_element_type=F32)


def _dot_hi(a, b):
    return jnp.dot(a, b, precision=HI, preferred_element_type=F32)


def _dot_onehot(a, onehot):
    hi = a.astype(jnp.bfloat16)
    lo = (a - hi.astype(F32)).astype(jnp.bfloat16)
    return (jnp.dot(hi, onehot, preferred_element_type=F32) + jnp.dot(lo, onehot, preferred_element_type=F32))


def _dot_hi_nt(a, b):
    return lax.dot_general(a, b, (((1,), (1,)), ((), ())), precision=HI, preferred_element_type=F32)


def _sig(x):
    return 1.0 / (1.0 + jnp.exp(-x))


def _dsilu(x, s):
    return s * (1.0 + x * (1.0 - s))


def _full(shape):
    nd = len(shape)
    return pl.BlockSpec(shape, lambda *_: (0,) * nd)


def rmsnorm_fwd(x, g):
    s, d = x.shape
    tm = min(512, s)

    def body(x_ref, g_ref, o_ref):
        xv = x_ref[...]
        r = lax.rsqrt(jnp.mean(xv * xv, axis=-1, keepdims=True) + EPS)
        o_ref[...] = (xv * r * g_ref[...]).astype(o_ref.dtype)

    return pl.pallas_call(
        body, name="rmsnorm_fwd", grid=(s // tm,),
        in_specs=[pl.BlockSpec((tm, d), lambda i: (i, 0)), _full((1, d))],
        out_specs=pl.BlockSpec((tm, d), lambda i: (i, 0)),
        out_shape=jax.ShapeDtypeStruct((s, d), ACT_DTYPE),
        compiler_params=_cparams(("parallel",)),
    )(x, g)


def mm_nn(a, b, tn, name):
    s, k = a.shape
    n = b.shape[1]
    tm = min(2048, s)

    def body(a_ref, b_ref, o_ref):
        o_ref[...] = _dot(a_ref[...], b_ref[...]).astype(o_ref.dtype)

    return pl.pallas_call(
        body, name=name, grid=(s // tm, n // tn),
        in_specs=[pl.BlockSpec((tm, k), lambda i, j: (i, 0)), pl.BlockSpec((k, tn), lambda i, j: (0, j))],
        out_specs=pl.BlockSpec((tm, tn), lambda i, j: (i, j)),
        out_shape=jax.ShapeDtypeStruct((s, n), ACT_DTYPE),
        compiler_params=_cparams(("parallel", "arbitrary")),
    )(a, b)


def mm_nt(a, b, tm, name):
    m, k = a.shape
    s = b.shape[0]
    ts = min(2048, s)

    def body(a_ref, b_ref, o_ref):
        o_ref[...] = _dot_nt(a_ref[...], b_ref[...]).astype(o_ref.dtype)

    return pl.pallas_call(
        body, name=name, grid=(s // ts, m // tm),
        in_specs=[pl.BlockSpec((tm, k), lambda i, j: (j, 0)), pl.BlockSpec((ts, k), lambda i, j: (i, 0))],
        out_specs=pl.BlockSpec((tm, ts), lambda i, j: (j, i)),
        out_shape=jax.ShapeDtypeStruct((m, s), ACT_DTYPE),
        compiler_params=_cparams(("parallel", "arbitrary")),
    )(a, b)


def mm_kn(a, b, tm, name):
    m, s = a.shape
    n = b.shape[1]
    ts = min(512, s)
    nt = s // ts

    def body(a_ref, b_ref, o_ref, acc_ref):
        @pl.when(pl.program_id(1) == 0)
        def _():
            acc_ref[...] = jnp.zeros_like(acc_ref)

        acc_ref[...] += _dot(a_ref[...], b_ref[...])

        @pl.when(pl.program_id(1) == nt - 1)
        def _():
            o_ref[...] = acc_ref[...].astype(o_ref.dtype)

    return pl.pallas_call(
        body, name=name, grid=(m // tm, nt),
        in_specs=[pl.BlockSpec((tm, ts), lambda j, t: (j, t)), pl.BlockSpec((ts, n), lambda j, t: (t, 0))],
        out_specs=pl.BlockSpec((tm, n), lambda j, t: (j, 0)),
        out_shape=jax.ShapeDtypeStruct((m, n), WIRE_DTYPE),
        scratch_shapes=[pltpu.VMEM((tm, n), F32)],
        compiler_params=_cparams(("parallel", "arbitrary")),
    )(a, b)


def mm_tn(a, b, tn, name):
    s, k = a.shape
    n = b.shape[1]
    ts = min(512, s)
    nt = s // ts

    def body(a_ref, b_ref, o_ref, acc_ref):
        @pl.when(pl.program_id(1) == 0)
        def _():
            acc_ref[...] = jnp.zeros_like(acc_ref)

        acc_ref[...] += _dot_tn(a_ref[...], b_ref[...])

        @pl.when(pl.program_id(1) == nt - 1)
        def _():
            o_ref[...] = acc_ref[...].astype(o_ref.dtype)

    return pl.pallas_call(
        body, name=name, grid=(n // tn, nt),
        in_specs=[pl.BlockSpec((ts, k), lambda j, t: (t, 0)), pl.BlockSpec((ts, tn), lambda j, t: (t, j))],
        out_specs=pl.BlockSpec((k, tn), lambda j, t: (0, j)),
        out_shape=jax.ShapeDtypeStruct((k, n), WIRE_DTYPE),
        scratch_shapes=[pltpu.VMEM((k, tn), F32)],
        compiler_params=_cparams(("parallel", "arbitrary")),
    )(a, b)


def dh_norm_bwd(dslabs, wslabs, x, g, dres, ex=None):
    s, d = x.shape
    tm = min(1024, s)
    widths = [ds.shape[0 if q == 0 else 1] for q, ds in enumerate(dslabs)]
    tks = [DH_TK] * len(widths)
    counts = [wd // t for wd, t in zip(widths, tks)]
    starts = [sum(counts[:i]) for i in range(len(counts))]
    nk = sum(counts)
    ns = len(dslabs)

    hosted = ex is not None
    ni = s // tm

    def mm_body(*refs):
        (own_in, (dh_ref,), _), hosted_refs = _split_hosted(refs, 2 * ns, 1, 0, ex)
        d_refs, w_refs = own_in[:ns], own_in[ns:]
        i, k = pl.program_id(0), pl.program_id(1)
        if hosted:
            @pl.when((i == 0) & (k == 0))
            def _():
                ex.start(*hosted_refs)

            @pl.when((i == ni - 1) & (k == nk - 1))
            def _():
                ex.relay(*hosted_refs)
                ex.wait(*hosted_refs)

        @pl.when(k == 0)
        def _():
            dh_ref[...] = jnp.zeros_like(dh_ref)

        for q in range(ns):
            @pl.when((k >= starts[q]) & (k < starts[q] + counts[q]))
            def _(q=q):
                if q == 0:
                    dh_ref[...] += _dot_tn(d_refs[q][...], w_refs[q][...])
                else:
                    dh_ref[...] += _dot_nt(d_refs[q][...], w_refs[q][...])

    def clamp(q):
        if q == 0:
            return pl.BlockSpec((tks[q], tm), lambda i, k: (jnp.clip(k - starts[q], 0, counts[q] - 1), i))
        return pl.BlockSpec((tm, tks[q]), lambda i, k: (i, jnp.clip(k - starts[q], 0, counts[q] - 1)))

    def clamp_w(q):
        if q == 0:
            return pl.BlockSpec((tks[q], d), lambda i, k: (jnp.clip(k - starts[q], 0, counts[q] - 1), 0))
        return pl.BlockSpec((d, tks[q]), lambda i, k: (0, jnp.clip(k - starts[q], 0, counts[q] - 1)))

    res = pl.pallas_call(
        mm_body, name="dh_matmul_scatter" if hosted else "dh_matmul", grid=(ni, nk),
        in_specs=([clamp(q) for q in range(ns)] + [clamp_w(q) for q in range(ns)]
                  + (ex.in_specs if hosted else [])),
        out_specs=[pl.BlockSpec((tm, d), lambda i, k: (i, 0))] + (ex.out_specs if hosted else []),
        out_shape=[jax.ShapeDtypeStruct((s, d), F32)] + (ex.out_shape if hosted else []),
        scratch_shapes=ex.scratch if hosted else [],
        compiler_params=_cparams(("arbitrary" if hosted else "parallel", "arbitrary")),
    )(*dslabs, *wslabs, *(ex.arrays if hosted else []))
    dh, ex_results = res[0], res[1:]

    te = min(512, s)

    def norm_body(dh_ref, x_ref, g_ref, dres_ref, dx_ref, dg_ref):
        @pl.when(pl.program_id(0) == 0)
        def _():
            dg_ref[...] = jnp.zeros_like(dg_ref)

        xv = x_ref[...]
        r = lax.rsqrt(jnp.mean(xv * xv, axis=-1, keepdims=True) + EPS)
        xn = xv * r
        dhv = dh_ref[...]
        dg_ref[...] += jnp.sum(dhv * xn, axis=0, keepdims=True)
        dxn = dhv * g_ref[...]
        dx_ref[...] = dres_ref[...] + r * (dxn - xn * jnp.mean(dxn * xn, axis=-1, keepdims=True))

    rows = pl.BlockSpec((te, d), lambda i: (i, 0))
    dx, dg = pl.pallas_call(
        norm_body, name="norm_bwd", grid=(s // te,),
        in_specs=[rows, rows, _full((1, d)), rows],
        out_specs=[rows, _full((1, d))],
        out_shape=[jax.ShapeDtypeStruct((s, d), F32), jax.ShapeDtypeStruct((1, d), F32)],
        compiler_params=_cparams(("arbitrary",)),
    )(dh, x, g, dres)
    return (dx, dg, ex_results) if hosted else (dx, dg)


def bias_table(rel_bias_t, onehot_t, maskadd):
    n = onehot_t.shape[1]
    tn = 8192

    def body(r_ref, o_ref, m_ref, out_ref):
        out_ref[...] = _dot_hi(r_ref[...], o_ref[...]) + m_ref[...]

    return pl.pallas_call(
        body, name="bias_table", grid=(n // tn,),
        in_specs=[_full((ATT_HEADS, REL_BUCKETS)), pl.BlockSpec((REL_BUCKETS, tn), lambda j: (0, j)),
                  pl.BlockSpec((1, tn), lambda j: (0, j))],
        out_specs=pl.BlockSpec((ATT_HEADS, tn), lambda j: (0, j)),
        out_shape=jax.ShapeDtypeStruct((ATT_HEADS, n), F32),
        compiler_params=_cparams(("parallel",)),
    )(rel_bias_t, onehot_t, maskadd)


def bias_table_bwd(dbias, onehot_t):
    n = onehot_t.shape[1]
    tn = 8192

    def body(d_ref, o_ref, out_ref):
        @pl.when(pl.program_id(0) == 0)
        def _():
            out_ref[...] = jnp.zeros_like(out_ref)

        out_ref[...] += _dot_hi_nt(d_ref[...], o_ref[...])

    return pl.pallas_call(
        body, name="bias_table_bwd", grid=(n // tn,),
        in_specs=[pl.BlockSpec((ATT_HEADS, tn), lambda j: (0, j)), pl.BlockSpec((REL_BUCKETS, tn), lambda j: (0, j))],
        out_specs=_full((ATT_HEADS, REL_BUCKETS)),
        out_shape=jax.ShapeDtypeStruct((ATT_HEADS, REL_BUCKETS), F32),
        compiler_params=_cparams(("arbitrary",)),
    )(dbias, onehot_t)


def _fold(full, tri):
    return jnp.where(tri, full[BLK:2 * BLK], full[0:BLK])


def _unfold(folded, tri):
    return jnp.concatenate([jnp.where(tri, 0.0, folded), jnp.where(tri, folded, 0.0)], axis=0)


GROUP_HEADS = ATT_HEADS // 2
GROUP_LANES = GROUP_HEADS * BLK


def _att_group(qg, kcat, vt_cat, bias_g, sink_g, tri, no_prev):
    l = _fold(_dot(kcat, qg), tri) * (HEAD_DIM ** -0.5) + bias_g
    l = jnp.where(no_prev, NEG, l)
    m = jnp.maximum(jnp.max(l, axis=0, keepdims=True), sink_g)
    p = jnp.exp(l - m)
    es = jnp.exp(sink_g - m)
    inv = 1.0 / (jnp.sum(p, axis=0, keepdims=True) + es)
    p = p * inv
    pcat = _unfold(p, tri)
    return p, pcat, es * inv, _dot(vt_cat, pcat)


ATT_SUB = 4


def _heads_to_lanes(ref, row0, ln):
    return jnp.concatenate([ref[row0 + j * HEAD_DIM:row0 + (j + 1) * HEAD_DIM, ln] for j in range(GROUP_HEADS)], axis=1)


def _lanes_to_heads(ref, row0, ln, val):
    for j in range(GROUP_HEADS):
        ref[row0 + j * HEAD_DIM:row0 + (j + 1) * HEAD_DIM, ln] = val[:, j * BLK:(j + 1) * BLK].astype(ref.dtype)


def _kv_cat(kvp, kvc, g):
    lo = g * HEAD_DIM
    kt_cat = jnp.concatenate([kvp[lo:lo + HEAD_DIM], kvc[lo:lo + HEAD_DIM]], axis=1)
    vt_cat = jnp.concatenate([kvp[128 + lo:128 + lo + HEAD_DIM], kvc[128 + lo:128 + lo + HEAD_DIM]], axis=1)
    return kt_cat, vt_cat


def _tri_masks(n):
    row = lax.broadcasted_iota(jnp.int32, (BLK, GROUP_LANES), 0)
    query = lax.broadcasted_iota(jnp.int32, (BLK, GROUP_LANES), 1) & (BLK - 1)
    tri = row <= query
    return tri, (n == 0) & jnp.logical_not(tri)


def _split_hosted(refs, n_in, n_out, n_scratch, ex):
    na = ex.na if ex is not None else 0
    o = 0
    parts = []
    for cnt in (n_in, na, n_out, na, n_scratch, 3 if ex is not None else 0):
        parts.append(refs[o:o + cnt])
        o += cnt
    own_in, ex_in, own_out, ex_out, own_scr, ex_sems = parts
    return (own_in, own_out, own_scr), (ex_in, ex_out, ex_sems)


def _call_hosting(body, name, nsteps, in_specs, out_specs, out_shape, scratch, args, ex):
    n_in, n_out, n_scr = len(in_specs), len(out_specs), len(scratch)
    hosted = ex is not None

    def full_body(*refs):
        (own_in, own_out, own_scr), hosted_refs = _split_hosted(refs, n_in, n_out, n_scr, ex)
        if hosted:
            @pl.when(pl.program_id(0) == 0)
            def _():
                ex.start(*hosted_refs)

            @pl.when(pl.program_id(0) == max(nsteps - 4, 0))
            def _():
                ex.relay(*hosted_refs)

            @pl.when(pl.program_id(0) == nsteps - 1)
            def _():
                ex.wait(*hosted_refs)

        body(*own_in, *own_out, *own_scr)

    res = pl.pallas_call(
        full_body, name=name + "_hosting" if hosted else name, grid=(nsteps,),
        in_specs=list(in_specs) + (ex.in_specs if hosted else []),
        out_specs=list(out_specs) + (ex.out_specs if hosted else []),
        out_shape=list(out_shape) + (ex.out_shape if hosted else []),
        scratch_shapes=list(scratch) + (ex.scratch if hosted else []),
        compiler_params=_cparams(("arbitrary",)),
    )(*args, *(ex.arrays if hosted else []))
    return res[:n_out], res[n_out:]


def attn_fwd(pa, bias, sinks, ex=None):
    s = pa.shape[1]
    nsteps = s // (ATT_SUB * BLK)

    def body(pa_ref, kvp_ref, bias_ref, sink_ref, y_ref):
        for sub in range(ATT_SUB):
            n = pl.program_id(0) * ATT_SUB + sub
            ln = slice(sub * BLK, (sub + 1) * BLK)
            kvc = pa_ref[2048:2304, ln]
            kvp = kvp_ref[...] if sub == 0 else pa_ref[2048:2304, (sub - 1) * BLK:sub * BLK]
            tri, no_prev = _tri_masks(n)
            for g in range(2):
                kt_cat, vt_cat = _kv_cat(kvp, kvc, g)
                row0 = g * GROUP_HEADS * HEAD_DIM
                _, _, _, o = _att_group(_heads_to_lanes(pa_ref, row0, ln), kt_cat.astype(F32).T, vt_cat, bias_ref[g],
                                        sink_ref[g:g + 1, :], tri, no_prev)
                z = _heads_to_lanes(pa_ref, 1024 + row0, ln).astype(F32)
                _lanes_to_heads(y_ref, row0, ln, o * z * _sig(z))

    (y,), hosted = _call_hosting(
        body, "attn_fwd", nsteps,
        in_specs=[pl.BlockSpec((ATT_COLS, ATT_SUB * BLK), lambda n: (0, n)),
                  pl.BlockSpec((256, BLK), lambda n: (8, jnp.maximum(ATT_SUB * n - 1, 0))),
                  _full((2, BLK, GROUP_LANES)), _full((2, GROUP_LANES))],
        out_specs=[pl.BlockSpec((1024, ATT_SUB * BLK), lambda n: (0, n))],
        out_shape=[jax.ShapeDtypeStruct((1024, s), ACT_DTYPE)], scratch=[],
        args=(pa, pa, bias, sinks), ex=ex)
    return (y, hosted) if ex is not None else y


def attn_bwd(pa, dy, bias, sinks, dbias_in):
    s = pa.shape[1]
    nsteps = s // (ATT_SUB * BLK)

    def body(pa_ref, kvp_ref, dy_ref, bias_ref, sink_ref, dbin_ref, dpa_ref, dbias_ref, dsink_ref, carry, dsink_acc):
        i = pl.program_id(0)

        @pl.when(i == 0)
        def _():
            dbias_ref[...] = dbin_ref[...]
            dsink_acc[...] = jnp.zeros_like(dsink_acc)
            carry[...] = jnp.zeros_like(carry)

        scale = HEAD_DIM ** -0.5
        for sub in reversed(range(ATT_SUB)):
            n = (nsteps - 1 - i) * ATT_SUB + sub
            ln = slice(sub * BLK, (sub + 1) * BLK)
            kvc = pa_ref[2048:2304, ln]
            kvp = kvp_ref[...] if sub == 0 else pa_ref[2048:2304, (sub - 1) * BLK:sub * BLK]
            tri, no_prev = _tri_masks(n)
            for g in range(2):
                kt_cat, vt_cat = _kv_cat(kvp, kvc, g)
                row0 = g * GROUP_HEADS * HEAD_DIM
                qg = _heads_to_lanes(pa_ref, row0, ln)
                p, pcat, psink, o = _att_group(qg, kt_cat.astype(F32).T, vt_cat, bias_ref[g], sink_ref[g:g + 1, :],
                                               tri, no_prev)
                z = _heads_to_lanes(pa_ref, 1024 + row0, ln).astype(F32)
                dyg = _heads_to_lanes(dy_ref, row0, ln).astype(F32)
                sz = _sig(z)
                d_o = dyg * z * sz
                _lanes_to_heads(dpa_ref, 1024 + row0, ln, dyg * _dsilu(z, sz) * o)
                delta = jnp.sum(d_o * o, axis=0, keepdims=True)
                dl = p * (_fold(_dot(vt_cat.astype(F32).T, d_o), tri) - delta)
                dsink_acc[g:g + 1, :] += psink * delta
                dbias_ref[g] += dl
                dlcat = _unfold(dl, tri)
                _lanes_to_heads(dpa_ref, row0, ln, _dot(kt_cat, dlcat) * scale)
                for q, dkv in enumerate((_dot_nt(qg, dlcat) * scale, _dot_nt(d_o, pcat))):
                    r0 = q * 128 + g * HEAD_DIM
                    dpa_ref[2048 + r0:2048 + r0 + HEAD_DIM, ln] = (
                        dkv[:, BLK:2 * BLK] + carry[r0:r0 + HEAD_DIM, :]).astype(dpa_ref.dtype)
                    carry[r0:r0 + HEAD_DIM, :] = dkv[:, 0:BLK]

        @pl.when(i == nsteps - 1)
        def _():
            lane = lax.broadcasted_iota(jnp.int32, (1, 128), 1)
            dsink = jnp.zeros((1, 128), F32)
            for h in range(ATT_HEADS):
                g, j = divmod(h, GROUP_HEADS)
                tot = jnp.sum(dsink_acc[g:g + 1, j * BLK:(j + 1) * BLK], axis=1, keepdims=True)
                dsink = dsink + jnp.where(lane == h, -tot, 0.0)
            dsink_ref[...] = dsink

    return pl.pallas_call(
        body, name="attn_bwd", grid=(nsteps,),
        in_specs=[pl.BlockSpec((ATT_COLS, ATT_SUB * BLK), lambda i: (0, nsteps - 1 - i)),
                  pl.BlockSpec((256, BLK), lambda i: (8, jnp.maximum(ATT_SUB * (nsteps - 1 - i) - 1, 0))),
                  pl.BlockSpec((1024, ATT_SUB * BLK), lambda i: (0, nsteps - 1 - i)),
                  _full((2, BLK, GROUP_LANES)), _full((2, GROUP_LANES)), _full((2, BLK, GROUP_LANES))],
        out_specs=[pl.BlockSpec((ATT_COLS, ATT_SUB * BLK), lambda i: (0, nsteps - 1 - i)),
                   _full((2, BLK, GROUP_LANES)), _full((1, 128))],
        out_shape=[jax.ShapeDtypeStruct((ATT_COLS, s), ACT_DTYPE),
                   jax.ShapeDtypeStruct((2, BLK, GROUP_LANES), F32),
                   jax.ShapeDtypeStruct((1, 128), F32)],
        scratch_shapes=[pltpu.VMEM((256, BLK), F32), pltpu.VMEM((2, GROUP_LANES), F32)],
        compiler_params=_cparams(("arbitrary",)),
    )(pa, pa, dy, bias, sinks, dbias_in)


def _layernorm(v, g, b):
    mu = jnp.mean(v, axis=-1, keepdims=True)
    vc = v - mu
    rstd = lax.rsqrt(jnp.mean(vc * vc, axis=-1, keepdims=True) + EPS)
    xhat = vc * rstd
    return xhat, rstd, xhat * g + b


def sgu_fwd(ps, ln_g, ln_b, w_tril, b_t, ex=None):
    s = ps.shape[0]
    rows = min(4 * BLK, s)

    def body(ps_ref, g_ref, b_ref, w_ref, bt_ref, y_ref):
        u = ps_ref[:, 0:1024].astype(F32)
        v = ps_ref[:, 1024:2048].astype(F32)
        z = ps_ref[:, 2048:3072].astype(F32)
        _, _, vn = _layernorm(v, g_ref[...], b_ref[...])
        gate = u * z * _sig(z)
        for c in range(rows // BLK):
            ch = slice(c * BLK, (c + 1) * BLK)
            for g in range(SG_GROUPS):
                sl = slice(g * 128, (g + 1) * 128)
                mixed = _dot(w_ref[g], vn[ch, sl]) + bt_ref[:, g:g + 1]
                y_ref[ch, sl] = (gate[ch, sl] * mixed).astype(y_ref.dtype)

    (y,), hosted = _call_hosting(
        body, "sgu_fwd", s // rows,
        in_specs=[pl.BlockSpec((rows, SG_COLS), lambda c: (c, 0)), _full((1, 1024)), _full((1, 1024)),
                  _full((SG_GROUPS, BLK, BLK)), _full((BLK, 128))],
        out_specs=[pl.BlockSpec((rows, 1024), lambda c: (c, 0))],
        out_shape=[jax.ShapeDtypeStruct((s, 1024), ACT_DTYPE)], scratch=[],
        args=(ps, ln_g, ln_b, w_tril, b_t), ex=ex)
    return (y, hosted) if ex is not None else y


def sgu_bwd(ps, dy, ln_g, ln_b, w_tril, w_tril_t, b_t):
    s = ps.shape[0]
    rows = min(2 * BLK, s)

    def body(ps_ref, dy_ref, g_ref, b_ref, w_ref, wt_ref, bt_ref, dps_ref, dw_ref, dbt_ref, dg_ref, db_ref, dvn_scr):
        @pl.when(pl.program_id(0) == 0)
        def _():
            dw_ref[...] = jnp.zeros_like(dw_ref)
            dbt_ref[...] = jnp.zeros_like(dbt_ref)
            dg_ref[...] = jnp.zeros_like(dg_ref)
            db_ref[...] = jnp.zeros_like(db_ref)

        u = ps_ref[:, 0:1024].astype(F32)
        v = ps_ref[:, 1024:2048].astype(F32)
        z = ps_ref[:, 2048:3072].astype(F32)
        dy = dy_ref[...].astype(F32)
        xhat, rstd, vn = _layernorm(v, g_ref[...], b_ref[...])
        sz = _sig(z)
        silu = z * sz
        row = lax.broadcasted_iota(jnp.int32, (BLK, BLK), 0)
        colm = lax.broadcasted_iota(jnp.int32, (BLK, BLK), 1)
        tril = row >= colm
        dbt = jnp.zeros((BLK, 128), F32)
        dsilu_z = _dsilu(z, sz)
        for c in range(rows // BLK):
            ch = slice(c * BLK, (c + 1) * BLK)
            for g in range(SG_GROUPS):
                sl = slice(g * 128, (g + 1) * 128)
                vng = vn[ch, sl]
                mixed = _dot(w_ref[g], vng) + bt_ref[:, g:g + 1]
                dyg, ug = dy[ch, sl], u[ch, sl]
                dps_ref[ch, sl] = (dyg * mixed * silu[ch, sl]).astype(dps_ref.dtype)
                dps_ref[ch, 2048 + g * 128:2048 + (g + 1) * 128] = (
                    dyg * ug * mixed * dsilu_z[ch, sl]).astype(dps_ref.dtype)
                dm = dyg * ug * silu[ch, sl]
                dw_ref[g] += jnp.where(tril, _dot_nt(dm, vng), 0.0)
                dbt = dbt + jnp.where(colm == g, jnp.sum(dm, axis=1, keepdims=True), 0.0)
                dvn_scr[ch, sl] = _dot(wt_ref[g], dm)
        dbt_ref[...] += dbt
        dvn = dvn_scr[...]
        dg_ref[...] += jnp.sum(dvn * xhat, axis=0, keepdims=True)
        db_ref[...] += jnp.sum(dvn, axis=0, keepdims=True)
        dxh = dvn * g_ref[...]
        dv = rstd * (dxh - jnp.mean(dxh, axis=-1, keepdims=True)
                     - xhat * jnp.mean(dxh * xhat, axis=-1, keepdims=True))
        dps_ref[:, 1024:2048] = dv.astype(dps_ref.dtype)

    return pl.pallas_call(
        body, name="sgu_bwd", grid=(s // rows,),
        in_specs=[pl.BlockSpec((rows, SG_COLS), lambda c: (c, 0)), pl.BlockSpec((rows, 1024), lambda c: (c, 0)),
                  _full((1, 1024)), _full((1, 1024)), _full((SG_GROUPS, BLK, BLK)), _full((SG_GROUPS, BLK, BLK)),
                  _full((BLK, 128))],
        out_specs=[pl.BlockSpec((rows, SG_COLS), lambda c: (c, 0)), _full((SG_GROUPS, BLK, BLK)), _full((BLK, 128)),
                   _full((1, 1024)), _full((1, 1024))],
        out_shape=[jax.ShapeDtypeStruct((s, SG_COLS), ACT_DTYPE), jax.ShapeDtypeStruct((SG_GROUPS, BLK, BLK), F32),
                   jax.ShapeDtypeStruct((BLK, 128), F32), jax.ShapeDtypeStruct((1, 1024), F32),
                   jax.ShapeDtypeStruct((1, 1024), F32)],
        scratch_shapes=[pltpu.VMEM((rows, 1024), F32)],
        compiler_params=_cparams(("arbitrary",)),
    )(ps, dy, ln_g, ln_b, w_tril, w_tril_t, b_t)


def _shift_down(cur, prev16, k):
    if k == 0:
        return cur
    r = pltpu.roll(cur, k, 0)
    rp = pltpu.roll(prev16, k, 0)
    row = lax.broadcasted_iota(jnp.int32, (8, cur.shape[1]), 0)
    return jnp.concatenate([jnp.where(row < k, rp[0:8], r[0:8]), r[8:]], axis=0)


def _shift_up(cur, next16, k):
    if k == 0:
        return cur
    n = cur.shape[0]
    r = pltpu.roll(cur, n - k, 0)
    rn = pltpu.roll(next16, 16 - k, 0)
    row = lax.broadcasted_iota(jnp.int32, (8, cur.shape[1]), 0)
    return jnp.concatenate([r[:n - 8], jnp.where(row >= 8 - k, rn[8:16], r[n - 8:])], axis=0)


def _bcast8(v):
    return jnp.broadcast_to(v, (16, v.shape[1]))


def _causal_conv(xbc, prev16, cw, cbias):
    pre = cbias + cw[3:4] * xbc
    for k in (1, 2, 3):
        pre = pre + cw[3 - k:4 - k] * _shift_down(xbc, prev16, k)
    return pre


class _Ssd:
    def __init__(self, pre, dtr, dtb, alog, dsk, tri, e):
        self.pre = pre
        self.sg = _sig(pre)
        act = pre * self.sg
        self.xs = act[:, 0:SSM_WIDTH]
        self.bm = act[:, SSM_WIDTH:SSM_WIDTH + 512]
        self.cm = act[:, SSM_WIDTH + 512:CONV_DIM]
        self.dtp = dtr + dtb
        self.dt = jnp.maximum(self.dtp, 0.0) + jnp.log(1.0 + jnp.exp(-jnp.abs(self.dtp)))
        self.a = -jnp.exp(alog)
        self.acs = _dot_hi(tri, self.dt * self.a)
        self.acs_t = self.acs.T
        tot = self.acs[BLK - 1:BLK]
        self.ecs = jnp.exp(self.acs)
        self.dte = jnp.exp(tot - self.acs)
        self.cd = jnp.exp(tot)
        self.dt_x = _dot_onehot(self.dt, e)
        self.ecs_x = _dot_onehot(self.ecs, e)
        self.dte_x = _dot_onehot(self.dte, e)
        self.cd_x = _dot_onehot(_bcast8(self.cd), e)[0:1]
        self.d_x = _dot_onehot(_bcast8(dsk), e)[0:1]
        self.xdt = self.xs * self.dt_x
        row = lax.broadcasted_iota(jnp.int32, (BLK, BLK), 0)
        col = lax.broadcasted_iota(jnp.int32, (BLK, BLK), 1)
        self.tril = row >= col

    def group(self, g):
        sl = slice(g * 128, (g + 1) * 128)
        bg, cg = self.bm[:, sl], self.cm[:, sl]
        return bg, cg, _dot_nt(cg, bg)

    def decay(self, h):
        seg = self.acs[:, h:h + 1] - self.acs_t[h:h + 1, :]
        return jnp.exp(jnp.where(self.tril, seg, NEG))

    def y_pre_gate(self, ht_of, yd_scr, yoff_scr):
        for g in range(SSM_GROUPS):
            bg, cg, cb = self.group(g)
            for j in range(8):
                h = g * 8 + j
                sl = slice(h * 64, (h + 1) * 64)
                yd_scr[:, sl] = _dot(cb * self.decay(h), self.xdt[:, sl])
            gs = slice(g * SSM_GW, (g + 1) * SSM_GW)
            yoff_scr[:, gs] = _dot(cg, ht_of(g)) * self.ecs_x[:, gs]
        return yd_scr[...] + yoff_scr[...] + self.d_x * self.xs


def _ssd_consts():
    hh = lax.broadcasted_iota(jnp.int32, (128, SSM_WIDTH), 0)
    ch = lax.broadcasted_iota(jnp.int32, (128, SSM_WIDTH), 1)
    e = (ch // 64 == hh).astype(jnp.bfloat16)
    row = lax.broadcasted_iota(jnp.int32, (BLK, BLK), 0)
    col = lax.broadcasted_iota(jnp.int32, (BLK, BLK), 1)
    tri = (row >= col).astype(F32)
    return tri, e


def _pad_lanes(v, n=128):
    return jnp.pad(v, ((0, 0), (0, n - v.shape[1])))


def ssd_fwd(pm, cw, cbias, dtb, alog, dsk, ng, ex=None):
    s = pm.shape[0]
    nc = s // BLK
    tri, e = _ssd_consts()

    def body(pm_ref, prev_ref, cw_ref, cb_ref, dtb_ref, al_ref, d_ref, ng_ref, tri_ref, e_ref,
             y_ref, st_ref, pre_ref, ht_ref, yd_scr, yoff_scr):
        c = pl.program_id(0)

        @pl.when(c == 0)
        def _():
            ht_ref[...] = jnp.zeros_like(ht_ref)

        xbc = pm_ref[:, 0:CONV_DIM].astype(F32)
        prev16 = jnp.where(c == 0, 0.0, prev_ref[...].astype(F32))
        pre = _causal_conv(xbc, prev16, cw_ref[...], cb_ref[...])
        pre_ref[...] = pre.astype(pre_ref.dtype)
        f = _Ssd(pre, pm_ref[:, DT_OFF:DT_OFF + 128].astype(F32), dtb_ref[...], al_ref[...], d_ref[...],
                 tri_ref[...], e_ref[...])
        st_ref[0] = ht_ref[...]
        y = f.y_pre_gate(lambda g: ht_ref[g], yd_scr, yoff_scr)
        for g in range(SSM_GROUPS):
            bg, _, _ = f.group(g)
            gs = slice(g * SSM_GW, (g + 1) * SSM_GW)
            ht_ref[g] = ht_ref[g] * f.cd_x[:, gs] + _dot_tn(bg, f.xdt[:, gs] * f.dte_x[:, gs])
        z = pm_ref[:, CONV_DIM:CONV_DIM + SSM_WIDTH].astype(F32)
        ypre = y * z * _sig(z)
        for g in range(SSM_GROUPS):
            gs = slice(g * SSM_GW, (g + 1) * SSM_GW)
            yg = ypre[:, gs]
            rr = lax.rsqrt(jnp.mean(yg * yg, axis=-1, keepdims=True) + EPS)
            y_ref[:, gs] = (yg * rr * ng_ref[:, gs]).astype(y_ref.dtype)

    own, hosted = _call_hosting(
        body, "ssd_fwd", nc,
        in_specs=[pl.BlockSpec((BLK, SSM_COLS), lambda c: (c, 0)),
                  pl.BlockSpec((16, CONV_DIM), lambda c: (jnp.maximum(8 * c - 1, 0), 0)),
                  _full((4, CONV_DIM)), _full((1, CONV_DIM)), _full((1, 128)), _full((1, 128)), _full((1, 128)),
                  _full((1, SSM_WIDTH)), _full((BLK, BLK)), _full((128, SSM_WIDTH))],
        out_specs=[pl.BlockSpec((BLK, SSM_WIDTH), lambda c: (c, 0)),
                   pl.BlockSpec((1, SSM_GROUPS, 128, SSM_GW), lambda c: (c, 0, 0, 0)),
                   pl.BlockSpec((BLK, CONV_DIM), lambda c: (c, 0))],
        out_shape=[jax.ShapeDtypeStruct((s, SSM_WIDTH), ACT_DTYPE),
                   jax.ShapeDtypeStruct((nc, SSM_GROUPS, 128, SSM_GW), F32),
                   jax.ShapeDtypeStruct((s, CONV_DIM), ACT_DTYPE)],
        scratch=[pltpu.VMEM((SSM_GROUPS, 128, SSM_GW), F32), pltpu.VMEM((BLK, SSM_WIDTH), F32),
                 pltpu.VMEM((BLK, SSM_WIDTH), F32)],
        args=(pm, pm, cw, cbias, dtb, alog, dsk, ng, tri, e), ex=ex)
    return (*own, hosted) if ex is not None else tuple(own)


def ssd_bwd(pm, pre, dy, states, cw, dtb, alog, dsk, ng, ex=None):
    s = pm.shape[0]
    nc = s // BLK
    tri, e = _ssd_consts()
    tri_t, e_t = tri.T, e.T

    def body(pm_ref, pre_ref, dy_ref, st_ref, cw_ref, dtb_ref, al_ref, d_ref, ng_ref,
             tri_ref, trit_ref, e_ref, et_ref,
             dpm_ref, dcw_ref, dcb_ref, dvec_ref, dng_ref,
             dht_ref, dcar_ref, yd_scr, yoff_scr, dx_scr, r2_scr, hs_scr, da_scr, dat_scr, dd_scr, dbc_scr):
        i = pl.program_id(0)
        n = nc - 1 - i

        @pl.when(i == 0)
        def _():
            dht_ref[...] = jnp.zeros_like(dht_ref)
            dcar_ref[...] = jnp.zeros_like(dcar_ref)
            dcw_ref[...] = jnp.zeros_like(dcw_ref)
            dcb_ref[...] = jnp.zeros_like(dcb_ref)
            dvec_ref[...] = jnp.zeros_like(dvec_ref)
            dng_ref[...] = jnp.zeros_like(dng_ref)
            dd_scr[...] = jnp.zeros_like(dd_scr)
            da_scr[...] = jnp.zeros_like(da_scr)
            dat_scr[...] = jnp.zeros_like(dat_scr)

        cw = cw_ref[...]
        f = _Ssd(pre_ref[...].astype(F32), pm_ref[:, DT_OFF:DT_OFF + 128].astype(F32), dtb_ref[...], al_ref[...],
                 d_ref[...], tri_ref[...], e_ref[...])
        et = et_ref[...]
        y = f.y_pre_gate(lambda g: st_ref[0, g], yd_scr, yoff_scr)

        z = pm_ref[:, CONV_DIM:CONV_DIM + SSM_WIDTH].astype(F32)
        dyv = dy_ref[...].astype(F32)
        sz = _sig(z)
        silu = z * sz
        ypre = y * silu
        for g in range(SSM_GROUPS):
            gs = slice(g * SSM_GW, (g + 1) * SSM_GW)
            yg = ypre[:, gs]
            rr = lax.rsqrt(jnp.mean(yg * yg, axis=-1, keepdims=True) + EPS)
            nrm = yg * rr
            dng_ref[:, gs] += jnp.sum(dyv[:, gs] * nrm, axis=0, keepdims=True)
            dn = dyv[:, gs] * ng_ref[:, gs]
            dx_scr[:, gs] = rr * (dn - nrm * jnp.mean(dn * nrm, axis=-1, keepdims=True))
        dypre = dx_scr[...]
        d_y = dypre * silu
        dpm_ref[:, CONV_DIM:CONV_DIM + SSM_WIDTH] = (dypre * y * _dsilu(z, sz)).astype(dpm_ref.dtype)

        for g in range(SSM_GROUPS):
            bg, cg, cb = f.group(g)
            gs = slice(g * SSM_GW, (g + 1) * SSM_GW)
            htg = st_ref[0, g]
            dhn = dht_ref[g]
            dcb = jnp.zeros((BLK, BLK), F32)
            for j in range(8):
                h = g * 8 + j
                sl = slice(h * 64, (h + 1) * 64)
                dec = f.decay(h)
                dyh = d_y[:, sl]
                dmd = _dot_nt(dyh, f.xdt[:, sl]) * dec
                dcb = dcb + dmd
                gm = dmd * cb
                da_scr[:, h:h + 1] = jnp.sum(gm, axis=1, keepdims=True)
                dat_scr[h:h + 1, :] = jnp.sum(gm, axis=0, keepdims=True)
                dx_scr[:, sl] = _dot_tn(cb * dec, dyh)
            dz = f.ecs_x[:, gs] * d_y[:, gs]
            dbc_scr[:, 512 + g * 128:512 + (g + 1) * 128] = _dot(dcb, bg) + _dot_nt(dz, htg)
            dbc_scr[:, g * 128:(g + 1) * 128] = _dot_tn(dcb, cg) + _dot_nt(f.xdt[:, gs] * f.dte_x[:, gs], dhn)
            dws = _dot(bg, dhn)
            dx_scr[:, gs] += f.dte_x[:, gs] * dws
            r2_scr[:, gs] = dws * f.xdt[:, gs]
            hs_scr[:, gs] = _bcast8(jnp.sum(dhn * htg, axis=0, keepdims=True))
            dht_ref[g] = f.cd_x[:, gs] * dhn + _dot_tn(cg, dz)
        d_x = dx_scr[...]
        r1 = _dot(d_y * yoff_scr[...], et)
        r2 = _dot(r2_scr[...], et) * f.dte
        dcd = _dot_onehot(hs_scr[...], et)[0:1]
        d_tot = jnp.sum(r2, axis=0, keepdims=True) + dcd * f.cd
        row = lax.broadcasted_iota(jnp.int32, (BLK, 128), 0)
        d_a = da_scr[...] - dat_scr[...].T + r1 - r2 + jnp.where(row == BLK - 1, d_tot, 0.0)
        dadt = _dot_hi(trit_ref[...], d_a)
        ddt = dadt * f.a + _dot(d_x * f.xs, et)
        lane = lax.broadcasted_iota(jnp.int32, (BLK, 128), 1)
        dr = jnp.where(lane < SSM_HEADS, ddt * _sig(f.dtp), 0.0)
        dvec_ref[0:1, :] += jnp.sum(dr, axis=0, keepdims=True)
        dvec_ref[1:2, :] += jnp.sum(dadt * f.dt, axis=0, keepdims=True) * f.a
        dd_scr[...] += _bcast8(jnp.sum(d_y * f.xs, axis=0, keepdims=True))
        dpm_ref[:, DT_OFF:DT_OFF + 128] = dr.astype(dpm_ref.dtype)
        dpm_ref[:, DT_OFF + 128:SSM_COLS] = jnp.zeros((BLK, 128), dpm_ref.dtype)

        dxs = d_x * f.dt_x + f.d_x * d_y
        dact = jnp.concatenate([dxs, dbc_scr[...]], axis=1)
        dpre = dact * _dsilu(f.pre, f.sg)
        dcb_ref[...] += jnp.sum(dpre, axis=0, keepdims=True)
        xbc = pm_ref[:, 0:CONV_DIM].astype(F32)
        dxraw = jnp.zeros((BLK, CONV_DIM), F32)
        nxt = dcar_ref[...]
        for k in range(4):
            ahead = _shift_up(dpre, nxt, k)
            dcw_ref[3 - k:4 - k, :] += jnp.sum(ahead * xbc, axis=0, keepdims=True)
            dxraw = dxraw + cw[3 - k:4 - k] * ahead
        dcar_ref[...] = dpre[0:16]
        dpm_ref[:, 0:CONV_DIM] = dxraw.astype(dpm_ref.dtype)

        @pl.when(i == nc - 1)
        def _():
            dvec_ref[2:3, :] = _dot_onehot(dd_scr[...], et)[0:1]

    own, hosted = _call_hosting(
        body, "ssd_bwd", nc,
        in_specs=[pl.BlockSpec((BLK, SSM_COLS), lambda i: (nc - 1 - i, 0)),
                  pl.BlockSpec((BLK, CONV_DIM), lambda i: (nc - 1 - i, 0)),
                  pl.BlockSpec((BLK, SSM_WIDTH), lambda i: (nc - 1 - i, 0)),
                  pl.BlockSpec((1, SSM_GROUPS, 128, SSM_GW), lambda i: (nc - 1 - i, 0, 0, 0)),
                  _full((4, CONV_DIM)), _full((1, 128)), _full((1, 128)), _full((1, 128)),
                  _full((1, SSM_WIDTH)), _full((BLK, BLK)), _full((BLK, BLK)), _full((128, SSM_WIDTH)),
                  _full((SSM_WIDTH, 128))],
        out_specs=[pl.BlockSpec((BLK, SSM_COLS), lambda i: (nc - 1 - i, 0)),
                   _full((8, CONV_DIM)), _full((1, CONV_DIM)), _full((8, 128)), _full((1, SSM_WIDTH))],
        out_shape=[jax.ShapeDtypeStruct((s, SSM_COLS), ACT_DTYPE), jax.ShapeDtypeStruct((8, CONV_DIM), F32),
                   jax.ShapeDtypeStruct((1, CONV_DIM), F32), jax.ShapeDtypeStruct((8, 128), F32),
                   jax.ShapeDtypeStruct((1, SSM_WIDTH), F32)],
        scratch=[pltpu.VMEM((SSM_GROUPS, 128, SSM_GW), F32), pltpu.VMEM((16, CONV_DIM), F32),
                 pltpu.VMEM((BLK, SSM_WIDTH), F32), pltpu.VMEM((BLK, SSM_WIDTH), F32),
                 pltpu.VMEM((BLK, SSM_WIDTH), F32), pltpu.VMEM((BLK, SSM_WIDTH), F32),
                 pltpu.VMEM((16, SSM_WIDTH), F32), pltpu.VMEM((BLK, 128), F32), pltpu.VMEM((128, BLK), F32),
                 pltpu.VMEM((16, SSM_WIDTH), F32), pltpu.VMEM((BLK, 1024), F32)],
        args=(pm, pre, dy, states, cw, dtb, alog, dsk, ng, tri, tri_t, e, e_t), ex=ex)
    return (*own, hosted) if ex is not None else tuple(own)


def merge_fwd(x, ya, ys, ym, pg, wa, ws, wm, wo, g_post, target=None):
    s, d = x.shape
    tm = min(256, s)
    with_loss = target is not None

    def body(*refs):
        x_ref, ya_ref, ys_ref, ym_ref, pg_ref, wa_ref, ws_ref, wm_ref, wo_ref, g_ref = refs[:10]
        if with_loss:
            t_ref, xo_ref, l_ref, ba_ref, bs_ref, bm_ref, mg_ref, out_ref = refs[10:]
        else:
            xo_ref, ba_ref, bs_ref, bm_ref, mg_ref, out_ref = refs[10:]
        ba = _dot_tn(ya_ref[...], wa_ref[...])
        bs = _dot(ys_ref[...], ws_ref[...])
        bm = _dot(ym_ref[...], wm_ref[...])
        merged = (_sig(pg_ref[:, 0:d].astype(F32)) * ba + _sig(pg_ref[:, d:2 * d].astype(F32)) * bs
                  + _sig(pg_ref[:, 2 * d:3 * d].astype(F32)) * bm)
        out = _dot(merged, wo_ref[...])
        r = lax.rsqrt(jnp.mean(out * out, axis=-1, keepdims=True) + EPS)
        y = x_ref[...] + out * r * g_ref[...]
        if with_loss:
            @pl.when(pl.program_id(0) == 0)
            def _():
                l_ref[...] = jnp.zeros_like(l_ref)

            err = y - t_ref[...]
            xo_ref[...] = err * (1.0 / d)
            part = jnp.sum(jnp.sum(err * err, axis=-1, keepdims=True) * (1.0 / d), axis=0, keepdims=True)
            l_ref[...] += 0.5 * jnp.broadcast_to(part, l_ref.shape)
        else:
            xo_ref[...] = y
        ba_ref[...] = ba.astype(ba_ref.dtype)
        bs_ref[...] = bs.astype(bs_ref.dtype)
        bm_ref[...] = bm.astype(bm_ref.dtype)
        mg_ref[...] = merged.astype(mg_ref.dtype)
        out_ref[...] = out.astype(out_ref.dtype)

    rows = lambda w: pl.BlockSpec((tm, w), lambda i: (i, 0))
    act = jax.ShapeDtypeStruct((s, d), ACT_DTYPE)
    loss_spec = [_full((8, 128))] if with_loss else []
    loss_shape = [jax.ShapeDtypeStruct((8, 128), F32)] if with_loss else []
    return pl.pallas_call(
        body, name="merge_fwd_loss" if with_loss else "merge_fwd", grid=(s // tm,),
        in_specs=[rows(d), pl.BlockSpec((d, tm), lambda i: (0, i)), rows(d), rows(2 * d), rows(3 * d), _full((d, d)),
                  _full((d, d)), _full((2 * d, d)), _full((d, d)), _full((1, d))] + ([rows(d)] if with_loss else []),
        out_specs=[rows(d)] + loss_spec + [rows(d)] * 5,
        out_shape=[jax.ShapeDtypeStruct((s, d), F32)] + loss_shape + [act] * 5,
        compiler_params=_cparams(("arbitrary" if with_loss else "parallel",)),
    )(x, ya, ys, ym, pg, wa, ws, wm, wo, g_post, *([target] if with_loss else []))


def merge_bwd(dx, out_s, pg, ba, bs, bm, wa, ws, wm, wo, g_post):
    s, d = dx.shape
    tm = min(256, s)

    def body(dx_ref, out_ref, pg_ref, ba_ref, bs_ref, bm_ref, wa_ref, ws_ref, wm_ref, wo_ref, g_ref,
             dout_ref, dba_ref, dbs_ref, dbm_ref, dpg_ref, dya_ref, dys_ref, dym_ref, dg_ref):
        @pl.when(pl.program_id(0) == 0)
        def _():
            dg_ref[...] = jnp.zeros_like(dg_ref)

        o = out_ref[...].astype(F32)
        dxv = dx_ref[...]
        r = lax.rsqrt(jnp.mean(o * o, axis=-1, keepdims=True) + EPS)
        nrm = o * r
        dg_ref[...] += jnp.sum(dxv * nrm, axis=0, keepdims=True)
        dn = dxv * g_ref[...]
        dout = r * (dn - nrm * jnp.mean(dn * nrm, axis=-1, keepdims=True))
        dout_ref[...] = dout.astype(dout_ref.dtype)
        dmerged = _dot_nt(dout, wo_ref[...])
        for q, (b_ref, db_ref, w_ref, dy_ref) in enumerate(((ba_ref, dba_ref, wa_ref, dya_ref),
                                                            (bs_ref, dbs_ref, ws_ref, dys_ref),
                                                            (bm_ref, dbm_ref, wm_ref, dym_ref))):
            gt = _sig(pg_ref[:, q * d:(q + 1) * d].astype(F32))
            db = dmerged * gt
            db_ref[...] = db.astype(db_ref.dtype)
            dpg_ref[:, q * d:(q + 1) * d] = (dmerged * b_ref[...].astype(F32) * gt * (1.0 - gt)).astype(dpg_ref.dtype)
            if q == 0:
                dy_ref[...] = _dot_nt(w_ref[...], db).astype(dy_ref.dtype)
            else:
                dy_ref[...] = _dot_nt(db, w_ref[...]).astype(dy_ref.dtype)

    rows = lambda w: pl.BlockSpec((tm, w), lambda i: (i, 0))
    act = lambda w: jax.ShapeDtypeStruct((s, w), ACT_DTYPE)
    return pl.pallas_call(
        body, name="merge_bwd", grid=(s // tm,),
        in_specs=[rows(d), rows(d), rows(3 * d), rows(d), rows(d), rows(d), _full((d, d)), _full((d, d)),
                  _full((2 * d, d)), _full((d, d)), _full((1, d))],
        out_specs=[rows(d), rows(d), rows(d), rows(d), rows(3 * d), pl.BlockSpec((d, tm), lambda i: (0, i)), rows(d),
                   rows(2 * d), _full((1, d))],
        out_shape=[act(d), act(d), act(d), act(d), act(3 * d), jax.ShapeDtypeStruct((d, s), ACT_DTYPE), act(d),
                   act(2 * d), jax.ShapeDtypeStruct((1, d), F32)],
        compiler_params=_cparams(("arbitrary",)),
    )(dx, out_s, pg, ba, bs, bm, wa, ws, wm, wo, g_post)


def _mesh_pos():
    x, y, c = lax.axis_index("x"), lax.axis_index("y"), lax.axis_index("c")
    return x, y, c, 4 * x + 2 * y + c


def _peer(x, y, c, k):
    px = 1 - x if k & 4 else x
    py = 1 - y if k & 2 else y
    pc = 1 - c if k & 1 else c
    return (px, py, pc), 4 * px + 2 * py + pc


class Exchange:
    SAME_CORE = (2, 4, 6)

    def __init__(self, scattered, gathered):
        self.ns = len(scattered)
        self.arrays = list(scattered) + list(gathered)
        self.na = len(self.arrays)
        any_spec = pl.BlockSpec(memory_space=pl.ANY)
        self.in_specs = [any_spec] * self.na
        self.out_specs = [any_spec] * self.na
        self.out_shape = ([jax.ShapeDtypeStruct(a.shape, a.dtype) for a in scattered]
                          + [jax.ShapeDtypeStruct((N_DEV,) + a.shape, a.dtype) for a in gathered])
        self.scratch = [pltpu.SemaphoreType.DMA((self.na, N_DEV - 1)), pltpu.SemaphoreType.DMA((self.na, N_DEV - 1)),
                        pltpu.SemaphoreType.DMA((self.na,))]

    def _src(self, ins, q, slot):
        return ins[q].at[slot] if q < self.ns else ins[q]

    def _local(self, ins, outs, sems):
        me = _mesh_pos()[3]
        return [pltpu.make_async_copy(self._src(ins, q, me), outs[q].at[me], sems[2].at[q]) for q in range(self.na)]

    def _direct(self, ins, outs, sems, relations, arrays):
        x, y, c, me = _mesh_pos()
        copies = []
        for k in relations:
            peer, pidx = _peer(x, y, c, k)
            for q in arrays:
                copies.append(pltpu.make_async_remote_copy(
                    src_ref=self._src(ins, q, pidx), dst_ref=outs[q].at[me], send_sem=sems[0].at[q, k - 1],
                    recv_sem=sems[1].at[q, k - 1], device_id=peer, device_id_type=MESH))
        return copies

    def _arrivals(self, ins, outs, sems, relations, arrays):
        x, y, c, _ = _mesh_pos()
        copies = []
        for k in relations:
            peer, pidx = _peer(x, y, c, k)
            for q in arrays:
                copies.append(pltpu.make_async_remote_copy(
                    src_ref=self._src(ins, q, pidx), dst_ref=outs[q].at[pidx], send_sem=sems[0].at[q, k - 1],
                    recv_sem=sems[1].at[q, k - 1], device_id=peer, device_id_type=MESH))
        return copies

    def _relays(self, outs, sems):
        x, y, c, _ = _mesh_pos()
        sibling, _ = _peer(x, y, c, 1)
        copies = []
        for k in self.SAME_CORE:
            _, pidx = _peer(x, y, c, k)
            for q in range(self.ns, self.na):
                copies.append(pltpu.make_async_remote_copy(
                    src_ref=outs[q].at[pidx], dst_ref=outs[q].at[pidx], send_sem=sems[0].at[q, k],
                    recv_sem=sems[1].at[q, k], device_id=sibling, device_id_type=MESH))
        return copies

    def _sends(self, ins, outs, sems):
        return (self._direct(ins, outs, sems, range(1, N_DEV), range(self.ns))
                + self._direct(ins, outs, sems, (1,) + self.SAME_CORE, range(self.ns, self.na)))

    def start(self, ins, outs, sems):
        for cp in self._local(ins, outs, sems) + self._sends(ins, outs, sems):
            cp.start()

    def relay(self, ins, outs, sems):
        for cp in self._arrivals(ins, outs, sems, self.SAME_CORE, range(self.ns, self.na)):
            cp.wait_recv()
        for cp in self._relays(outs, sems):
            cp.start()

    def wait(self, ins, outs, sems):
        for cp in (self._arrivals(ins, outs, sems, range(1, N_DEV), range(self.ns))
                   + self._arrivals(ins, outs, sems, (1, 3, 5, 7), range(self.ns, self.na))):
            cp.wait_recv()
        for cp in self._sends(ins, outs, sems) + self._relays(outs, sems):
            cp.wait_send()
        for cp in self._local(ins, outs, sems):
            cp.wait()


def exchange(scattered, gathered, name):
    ex = Exchange(scattered, gathered)

    def body(*refs):
        ins, outs, sems = refs[:ex.na], refs[ex.na:2 * ex.na], refs[2 * ex.na:]
        ex.start(ins, outs, sems)
        ex.relay(ins, outs, sems)
        ex.wait(ins, outs, sems)

    return pl.pallas_call(body, name=name, in_specs=ex.in_specs, out_specs=ex.out_specs, out_shape=ex.out_shape,
                          scratch_shapes=ex.scratch)(*ex.arrays)


def adamw(parts_list, w, m, v, tile, name, ex=None):
    npart, _, dp = parts_list[0].shape
    d = w.shape[-1]
    counts = [p.shape[1] // tile for p in parts_list]
    starts = [sum(counts[:q]) for q in range(len(counts))]
    n_lists = len(parts_list)

    def body(*refs):
        p_refs = refs[:n_lists]
        w_ref, m_ref, v_ref, g_ref, dw_ref, nm_ref, nv_ref = refs[n_lists:]
        i = pl.program_id(0)
        for q, p_ref in enumerate(p_refs):
            @pl.when((i >= starts[q]) & (i < starts[q] + counts[q]))
            def _(p_ref=p_ref):
                acc = p_ref[0, :, 0:d].astype(F32)
                for k in range(1, npart):
                    acc = acc + p_ref[k, :, 0:d].astype(F32)
                g_ref[...] = acc

        g = g_ref[...]
        nm = ADAM_B1 * m_ref[...] + (1.0 - ADAM_B1) * g
        nv = ADAM_B2 * v_ref[...] + (1.0 - ADAM_B2) * (g * g)
        nm_ref[...] = nm
        nv_ref[...] = nv
        m_hat = nm / (1.0 - ADAM_B1 ** ADAM_STEP)
        v_hat = nv / (1.0 - ADAM_B2 ** ADAM_STEP)
        dw_ref[...] = -ADAM_LR * (m_hat / (jnp.sqrt(v_hat) + ADAM_EPS) + ADAM_WD * w_ref[...])

    def part_rows(q):
        return lambda i: (0, jnp.clip(i - starts[q], 0, counts[q] - 1), 0)

    if w.ndim == 3:
        rows = pl.BlockSpec((None, tile, d), lambda i: (i // counts[0], i % counts[0], 0))
    else:
        rows = pl.BlockSpec((tile, d), lambda i: (i, 0))
    own, hosted = _call_hosting(
        body, name, sum(counts),
        in_specs=[pl.BlockSpec((npart, tile, dp), part_rows(q)) for q in range(n_lists)] + [rows, rows, rows],
        out_specs=[rows] * 4, out_shape=[jax.ShapeDtypeStruct(w.shape, F32)] * 4, scratch=[],
        args=(*parts_list, w, m, v), ex=ex)
    return (*own, hosted) if ex is not None else tuple(own)


def _pad_rows(a, rows):
    return jnp.pad(a, ((0, rows - a.shape[0]), (0, 0)))


def _pack_rest(w_att, w_sg, w_ssm, w_out):
    parts = []
    for l in range(2):
        parts += [w_att[l], w_sg[l], w_ssm[l], w_out[l]]
    return jnp.concatenate(parts, axis=0)


def _unpack_rest(p):
    outs = [[], [], [], []]
    o = 0
    for l in range(2):
        for q, rws in enumerate(REST_PARTS):
            outs[q].append(p[o:o + rws])
            o += rws
    return [jnp.stack(t) for t in outs]


def _pack_win(w_in):
    return jnp.pad(w_in.reshape(2 * D_MODEL, WIN_SHARD), ((0, 0), (0, WIN_LANES - WIN_SHARD)))


W_IN_MAP = ((0, 1024, "att", 0), (1024, 1280, "att", 2048), (1280, 2304, "att", 1024), (2304, 5376, "sg", 0),
            (5376, 7424, "ssm", 3072), (7424, 10496, "ssm", 0), (10496, 10528, "ssm", 5120), (10528, 13600, "gate", 0))
SLAB_COLS = {"att": ATT_COLS, "sg": SG_COLS, "ssm": SSM_COLS, "gate": GATE_COLS}


def _slabs_from_shards(g):
    slabs = {}
    for name, width in SLAB_COLS.items():
        pieces, filled = [], 0
        for ga, gb, _, off in sorted((m for m in W_IN_MAP if m[2] == name), key=lambda m: m[3]):
            assert off == filled
            a = ga
            while a < gb:
                d = a // WIN_SHARD
                hi = min(gb, WIN_SHARD * (d + 1))
                pieces.append(g[d, :, a - WIN_SHARD * d:hi - WIN_SHARD * d])
                a = hi
            filled += gb - ga
        if filled < width:
            pieces.append(jnp.zeros((D_MODEL, width - filled), g.dtype))
        slabs[name] = jnp.concatenate(pieces, axis=1)
    return slabs


def _shards_from_slabs(dslabs):
    out = []
    for d in range(N_DEV):
        a, b = WIN_SHARD * d, WIN_SHARD * (d + 1)
        pieces = []
        for ga, gb, name, off in W_IN_MAP:
            lo, hi = max(a, ga), min(b, gb)
            if lo < hi:
                pieces.append(dslabs[name][:, off + lo - ga:off + hi - ga])
        pieces.append(jnp.zeros((D_MODEL, WIN_LANES - WIN_SHARD), pieces[0].dtype))
        out.append(jnp.concatenate(pieces, axis=1).astype(WIRE_DTYPE))
    return jnp.stack(out)


SMALL_SIZES = (("norm_pre", 2048), ("norm_post", 2048), ("rel_bias", 512), ("att_sinks", 32), ("sg_ln_g", 2048),
               ("sg_ln_b", 2048), ("sg_w", 262144), ("sg_b", 2048), ("ssm_conv_b", 6144), ("ssm_dt_bias", 64),
               ("ssm_a_log", 64), ("ssm_d", 64), ("ssm_norm_g", 4096), ("conv_w_full", 24576))


def _pack_small(d):
    parts = []
    for name, size in SMALL_SIZES:
        rows = 8 * (-(-size // (8 * D_MODEL)))
        flat = d[name].reshape(-1) if name in d else jnp.zeros((size,), F32)
        parts.append(jnp.pad(flat, (0, rows * D_MODEL - size)).reshape(rows, D_MODEL))
    return _pad_rows(jnp.concatenate(parts, axis=0), SMALL_ROWS)


def _unpack_small(p, shapes):
    out, o = {}, 0
    for name, size in SMALL_SIZES:
        rows = 8 * (-(-size // (8 * D_MODEL)))
        if name in shapes:
            out[name] = p[o:o + rows].reshape(-1)[:size].reshape(shapes[name])
        o += rows
    return out


def _bucket_onehot_t():
    qi = jnp.arange(BLK, dtype=jnp.int32)[None, :]
    kj = jnp.arange(BLK, dtype=jnp.int32)[:, None]
    dd = (qi - kj) & (BLK - 1)
    in_window = dd >= 0
    max_exact = REL_BUCKETS // 2
    dist_f = jnp.maximum(dd, 1).astype(F32)
    large = max_exact + (jnp.log(dist_f / max_exact) / math.log(128 / max_exact)
                         * (REL_BUCKETS - max_exact)).astype(jnp.int32)
    large = jnp.minimum(large, REL_BUCKETS - 1)
    bucket = jnp.where(dd < max_exact, dd, large).reshape(1, -1)
    onehot_t = (bucket == jnp.arange(REL_BUCKETS, dtype=jnp.int32)[:, None]).astype(F32)
    maskadd = jnp.where(in_window, 0.0, NEG).astype(F32).reshape(1, -1)
    return onehot_t, maskadd


WEIGHTS = ['w_in', 'norm_pre', 'norm_post', 'rel_bias', 'att_sinks', 'sg_ln_g', 'sg_ln_b', 'sg_w', 'sg_b',
           'ssm_conv_w', 'ssm_conv_b', 'ssm_dt_bias', 'ssm_a_log', 'ssm_d', 'ssm_norm_g',
           'w_br_att', 'w_br_sg', 'w_br_ssm', 'w_out']
REST = ('w_br_att', 'w_br_sg', 'w_br_ssm', 'w_out')


def kernel(x, w_in, norm_pre, norm_post, rel_bias, att_sinks, sg_ln_g, sg_ln_b, sg_w, sg_b, ssm_conv_w, ssm_conv_b, ssm_dt_bias, ssm_a_log, ssm_d, ssm_norm_g, w_br_att, w_br_sg, w_br_ssm, w_out, loss_target, m_w_in, m_norm_pre, m_norm_post, m_rel_bias, m_att_sinks, m_sg_ln_g, m_sg_ln_b, m_sg_w, m_sg_b, m_ssm_conv_w, m_ssm_conv_b, m_ssm_dt_bias, m_ssm_a_log, m_ssm_d, m_ssm_norm_g, m_w_br_att, m_w_br_sg, m_w_br_ssm, m_w_out, v_w_in, v_norm_pre, v_norm_post, v_rel_bias, v_att_sinks, v_sg_ln_g, v_sg_ln_b, v_sg_w, v_sg_b, v_ssm_conv_w, v_ssm_conv_b, v_ssm_dt_bias, v_ssm_a_log, v_ssm_d, v_ssm_norm_g, v_w_br_att, v_w_br_sg, v_w_br_ssm, v_w_out):
    w = dict(w_in=w_in, norm_pre=norm_pre, norm_post=norm_post, rel_bias=rel_bias, att_sinks=att_sinks,
             sg_ln_g=sg_ln_g, sg_ln_b=sg_ln_b, sg_w=sg_w, sg_b=sg_b, ssm_conv_w=ssm_conv_w, ssm_conv_b=ssm_conv_b,
             ssm_dt_bias=ssm_dt_bias, ssm_a_log=ssm_a_log, ssm_d=ssm_d, ssm_norm_g=ssm_norm_g,
             w_br_att=w_br_att, w_br_sg=w_br_sg, w_br_ssm=w_br_ssm, w_out=w_out)
    mom = dict(w_in=m_w_in, norm_pre=m_norm_pre, norm_post=m_norm_post, rel_bias=m_rel_bias, att_sinks=m_att_sinks,
               sg_ln_g=m_sg_ln_g, sg_ln_b=m_sg_ln_b, sg_w=m_sg_w, sg_b=m_sg_b, ssm_conv_w=m_ssm_conv_w,
               ssm_conv_b=m_ssm_conv_b, ssm_dt_bias=m_ssm_dt_bias, ssm_a_log=m_ssm_a_log, ssm_d=m_ssm_d,
               ssm_norm_g=m_ssm_norm_g, w_br_att=m_w_br_att, w_br_sg=m_w_br_sg, w_br_ssm=m_w_br_ssm, w_out=m_w_out)
    var = dict(w_in=v_w_in, norm_pre=v_norm_pre, norm_post=v_norm_post, rel_bias=v_rel_bias, att_sinks=v_att_sinks,
               sg_ln_g=v_sg_ln_g, sg_ln_b=v_sg_ln_b, sg_w=v_sg_w, sg_b=v_sg_b, ssm_conv_w=v_ssm_conv_w,
               ssm_conv_b=v_ssm_conv_b, ssm_dt_bias=v_ssm_dt_bias, ssm_a_log=v_ssm_a_log, ssm_d=v_ssm_d,
               ssm_norm_g=v_ssm_norm_g, w_br_att=v_w_br_att, w_br_sg=v_w_br_sg, w_br_ssm=v_w_br_ssm, w_out=v_w_out)
    xs0 = x[0]
    target = loss_target[0]
    my_dev = 4 * lax.axis_index("x") + 2 * lax.axis_index("y") + lax.axis_index("c")

    conv_shard = _pad_rows(ssm_conv_w.reshape(-1, D_MODEL), 8)
    win_shard = _pack_win(w_in).astype(WIRE_DTYPE)
    rest_shard = _pack_rest(*[w[n] for n in REST]).astype(WIRE_DTYPE)
    layer_shards = [[win_shard[l * D_MODEL:(l + 1) * D_MODEL], rest_shard[l * LAYER_REST:(l + 1) * LAYER_REST]]
                    for l in range(2)]
    g_win0, gathered_conv = exchange([], [layer_shards[0][0], conv_shard], "all_gather")
    conv_full = gathered_conv[:, 0:3].reshape(N_DEV, 2, 4, 384).transpose(1, 2, 0, 3).reshape(2, 4, CONV_DIM)

    def set_rest(lw, g_rest):
        o = 0
        for name, rws in zip(("att", "sg", "ssm", "out"), REST_PARTS):
            lw[name] = g_rest[:, o:o + rws].reshape(N_DEV * rws, D_MODEL).astype(MXU_DTYPE)
            o += rws

    def layer_weights(l, g_win):
        slabs = _slabs_from_shards(g_win)
        lw = {"in_" + name: slab.astype(MXU_DTYPE) for name, slab in slabs.items()}
        lw["in_att"] = lw["in_att"].T
        tril = jnp.tril(jnp.ones((BLK, BLK), bool))
        sgw = jnp.where(tril[None], sg_w[l], 0.0)
        lw.update(
            g_pre=norm_pre[l][None], g_post=norm_post[l][None], sinks=jnp.repeat(att_sinks[l], BLK).reshape(2, GROUP_LANES),
            ln_g=sg_ln_g[l][None], ln_b=sg_ln_b[l][None], sgw=sgw.astype(MXU_DTYPE),
            sgw_t=sgw.transpose(0, 2, 1).astype(MXU_DTYPE), sgb_t=_pad_lanes(sg_b[l].T),
            cw=conv_full[l], cb=ssm_conv_b[l][None], dtb=_pad_lanes(ssm_dt_bias[l][None]),
            alog=_pad_lanes(ssm_a_log[l][None]), dsk=_pad_lanes(ssm_d[l][None]), ng=ssm_norm_g[l][None])
        return lw

    onehot_t, maskadd = _bucket_onehot_t()
    bias = bias_table(rel_bias.T, onehot_t, maskadd).reshape(2, GROUP_HEADS, BLK, BLK).transpose(0, 2, 1, 3)
    bias = bias.reshape(2, BLK, GROUP_LANES)

    saved = []
    xl = xs0
    layers = [layer_weights(0, g_win0)]
    for l in range(2):
        lw = layers[l]
        h = rmsnorm_fwd(xl, lw["g_pre"])
        pa = mm_nt(lw["in_att"], h, 1152, "proj_att")
        ps = mm_nn(h, lw["in_sg"], 1536, "proj_sg")
        pm = mm_nn(h, lw["in_ssm"], 1792, "proj_ssm")
        pg = mm_nn(h, lw["in_gate"], 1536, "proj_gate")
        if l == 0:
            ya, (g_rest0,) = attn_fwd(pa, bias, lw["sinks"], Exchange([], [layer_shards[0][1]]))
            set_rest(lw, g_rest0)
        else:
            ya = attn_fwd(pa, bias, lw["sinks"])
        sgu_args = (ps, lw["ln_g"], lw["ln_b"], lw["sgw"], lw["sgb_t"])
        ssd_args = (pm, lw["cw"], lw["cb"], lw["dtb"], lw["alog"], lw["dsk"], lw["ng"])
        ys = sgu_fwd(*sgu_args)
        if l == 0:
            ym, states, conv_pre, (g_win1, g_rest1) = ssd_fwd(*ssd_args, Exchange([], layer_shards[1]))
            layers.append(layer_weights(1, g_win1))
            set_rest(layers[1], g_rest1)
        else:
            ym, states, conv_pre = ssd_fwd(*ssd_args)
        merge_args = (xl, ya, ys, ym, pg, lw["att"], lw["sg"], lw["ssm"], lw["out"], lw["g_post"])
        if l == 0:
            x_next, ba, bs, bm, merged, out_s = merge_fwd(*merge_args)
        else:
            dx, loss_part, ba, bs, bm, merged, out_s = merge_fwd(*merge_args, target)
        saved.append(dict(x=xl, h=h, pa=pa, ps=ps, pm=pm, pg=pg, ya=ya, ys=ys, ym=ym, states=states,
                          conv_pre=conv_pre, ba=ba, bs=bs, bm=bm, merged=merged, out_s=out_s))
        xl = x_next

    loss = lax.psum(loss_part[0, 0], ("x", "y", "c"))

    dbias = jnp.zeros((2, BLK, GROUP_LANES), F32)
    win_grads, rest_grads = [None, None], [None, None]
    small = {n: [None, None] for n in ("norm_pre", "norm_post", "att_sinks", "sg_ln_g", "sg_ln_b", "sg_w", "sg_b",
                                       "ssm_conv_b", "ssm_dt_bias", "ssm_a_log", "ssm_d", "ssm_norm_g",
                                       "conv_w_full")}
    for l in (1, 0):
        lw, sv = layers[l], saved[l]
        dout, dba, dbs, dbm, dpg, dya, dys, dym, dg_post = merge_bwd(
            dx, sv["out_s"], sv["pg"], sv["ba"], sv["bs"], sv["bm"], lw["att"], lw["sg"], lw["ssm"], lw["out"],
            lw["g_post"])
        dw_out = mm_tn(sv["merged"], dout, 1024, "dw_out")
        dw_att = mm_kn(sv["ya"], dba, 1024, "dw_br_att")
        dw_sg = mm_tn(sv["ys"], dbs, 1024, "dw_br_sg")
        dw_ssm = mm_tn(sv["ym"], dbm, 1024, "dw_br_ssm")
        rest_grads[l] = jnp.concatenate(
            [dw_att.reshape(N_DEV, 128, D_MODEL), dw_sg.reshape(N_DEV, 128, D_MODEL),
             dw_ssm.reshape(N_DEV, 256, D_MODEL), dw_out.reshape(N_DEV, 128, D_MODEL)], axis=1).astype(WIRE_DTYPE)
        dpa, dbias, dsinks = attn_bwd(sv["pa"], dya, bias, lw["sinks"], dbias)
        dps, dsgw, dsgb_t, dln_g, dln_b = sgu_bwd(sv["ps"], dys, lw["ln_g"], lw["ln_b"], lw["sgw"], lw["sgw_t"],
                                                  lw["sgb_t"])
        ssd_args = (sv["pm"], sv["conv_pre"], dym, sv["states"], lw["cw"], lw["dtb"], lw["alog"], lw["dsk"], lw["ng"])
        if l == 0:
            dpm, dcw, dcb, dvec, dng, (recv_win1, recv_rest0) = ssd_bwd(
                *ssd_args, Exchange([win_grads[1], rest_grads[0]], []))
        else:
            dpm, dcw, dcb, dvec, dng, (recv_rest1,) = ssd_bwd(*ssd_args, Exchange([rest_grads[1]], []))
        dslabs = dict(att=mm_kn(dpa, sv["h"], 1152, "dw_in_att").T, sg=mm_tn(sv["h"], dps, 3072, "dw_in_sg"),
                      ssm=mm_tn(sv["h"], dpm, 2688, "dw_in_ssm"), gate=mm_tn(sv["h"], dpg, 3072, "dw_in_gate"))
        win_grads[l] = _shards_from_slabs(dslabs)
        dh_args = ([dpa, dps, dpm, dpg], [lw["in_att"], lw["in_sg"], lw["in_ssm"], lw["in_gate"]], sv["x"],
                   lw["g_pre"], dx)
        if l == 0:
            dx, dg_pre, (recv_win0,) = dh_norm_bwd(*dh_args, Exchange([win_grads[0]], []))
        else:
            dx, dg_pre = dh_norm_bwd(*dh_args)
        small["norm_pre"][l] = dg_pre[0]
        small["norm_post"][l] = dg_post[0]
        small["att_sinks"][l] = dsinks[0, :ATT_HEADS]
        small["sg_ln_g"][l] = dln_g[0]
        small["sg_ln_b"][l] = dln_b[0]
        small["sg_w"][l] = dsgw
        small["sg_b"][l] = dsgb_t[:, :SG_GROUPS].T
        small["ssm_conv_b"][l] = dcb[0]
        small["ssm_dt_bias"][l] = dvec[0, :SSM_HEADS]
        small["ssm_a_log"][l] = dvec[1, :SSM_HEADS]
        small["ssm_d"][l] = dvec[2, :SSM_HEADS]
        small["ssm_norm_g"][l] = dng[0]
        small["conv_w_full"][l] = dcw[0:4]
    grad_x = dx
    dbias = dbias.reshape(2, BLK, GROUP_HEADS, BLK).transpose(0, 2, 1, 3).reshape(ATT_HEADS, BLK * BLK)
    d_rel_bias = bias_table_bwd(dbias, onehot_t).T

    small_d = {n: jnp.stack(v) for n, v in small.items()}
    small_d["rel_bias"] = d_rel_bias
    *res_win, (recv_small,) = adamw([recv_win0, recv_win1], w_in, m_w_in, v_w_in, WIN_TILE, "adamw_w_in",
                                    Exchange([], [_pack_small(small_d)]))
    res_rest = adamw([recv_rest0, recv_rest1], _pack_rest(*[w[n] for n in REST]), _pack_rest(*[mom[n] for n in REST]),
                     _pack_rest(*[var[n] for n in REST]), REST_TILE, "adamw_rest")
    small_names = [n for n, _ in SMALL_SIZES if n != "conv_w_full"]
    g_s, dw_s, nm_s, nv_s = adamw([recv_small], _pack_small({n: w[n] for n in small_names}),
                                  _pack_small({n: mom[n] for n in small_names}),
                                  _pack_small({n: var[n] for n in small_names}), SMALL_TILE, "adamw_small")
    shapes = {n: w[n].shape for n in small_names}
    shapes["conv_w_full"] = (2, 4, CONV_DIM)
    g_conv_full = _unpack_small(g_s, shapes)["conv_w_full"]
    g_conv = lax.dynamic_slice_in_dim(g_conv_full, my_dev * 384, 384, axis=2)
    pack_conv = lambda a: _pad_rows(a.reshape(-1, D_MODEL), 8)
    g_c, dw_c, nm_c, nv_c = adamw([pack_conv(g_conv)[None]], pack_conv(ssm_conv_w), pack_conv(m_ssm_conv_w),
                                  pack_conv(v_ssm_conv_w), 8, "adamw_conv")

    results = {}
    for q, (tag, psm, pc) in enumerate((("grad", g_s, g_c), ("delta", dw_s, dw_c), ("new_m", nm_s, nm_c),
                                        ("new_v", nv_s, nv_c))):
        r = dict(zip(REST, _unpack_rest(res_rest[q])))
        r["w_in"] = res_win[q]
        r.update(_unpack_small(psm, {n: w[n].shape for n in small_names}))
        r["ssm_conv_w"] = pc[0:3].reshape(2, 4, 384)
        results[tag] = r
    outs = [loss, grad_x[None]]
    for tag in ("grad", "delta", "new_m", "new_v"):
        outs += [results[tag][n] for n in WEIGHTS]
    return tuple(outs)
```

```python
import math

import jax
import jax.numpy as jnp
from jax import lax
from jax.experimental import pallas as pl
from jax.experimental.pallas import tpu as pltpu

F32 = jnp.float32
MXU_DTYPE = jnp.bfloat16
ACT_DTYPE = jnp.bfloat16
WIRE_DTYPE = jnp.bfloat16
HI = lax.Precision.HIGHEST
MESH = pl.DeviceIdType.MESH

D_MODEL = 1024
N_DEV = 8
ATT_HEADS = 16
HEAD_DIM = 64
BLK = 128
SG_GROUPS = 8
SSM_WIDTH = 2048
SSM_HEADS = 32
SSM_GROUPS = 4
SSM_GW = SSM_WIDTH // SSM_GROUPS
CONV_DIM = 3072
REL_BUCKETS = 32
EPS = 1e-6
NEG = -1e30

ATT_COLS = 2304
SG_COLS = 3072
SSM_COLS = 5376
GATE_COLS = 3072
DT_OFF = 5120

VMEM_LIMIT_V7X = 56 * 2 ** 20
DH_TK = 768

ADAM_LR, ADAM_B1, ADAM_B2, ADAM_EPS, ADAM_WD, ADAM_STEP = 0.001, 0.9, 0.999, 1e-08, 0.01, 10

WIN_SHARD = 1700
WIN_LANES = 1792
REST_PARTS = (128, 128, 256, 128)
LAYER_REST = sum(REST_PARTS)
REST_TILE = 128
WIN_TILE = 128
SMALL_ROWS = 384
SMALL_TILE = 128


def _cparams(sem=None):
    return pltpu.CompilerParams(dimension_semantics=sem, vmem_limit_bytes=VMEM_LIMIT_V7X)


def _dot(a, b):
    return jnp.dot(a.astype(MXU_DTYPE), b.astype(MXU_DTYPE), preferred_element_type=F32)


def _dot_nt(a, b):
    return lax.dot_general(a.astype(MXU_DTYPE), b.astype(MXU_DTYPE), (((1,), (1,)), ((), ())),
                           preferred_element_type=F32)


def _dot_tn(a, b):
    return lax.dot_general(a.astype(MXU_DTYPE), b.astype(MXU_DTYPE), (((0,), (0,)), ((), ())),
                           preferred_element_type=F32)


def _dot_hi(a, b):
    return jnp.dot(a, b, precision=HI, preferred_element_type=F32)


def _dot_onehot(a, onehot):
    hi = a.astype(jnp.bfloat16)
    lo = (a - hi.astype(F32)).astype(jnp.bfloat16)
    return (jnp.dot(hi, onehot, preferred_element_type=F32) + jnp.dot(lo, onehot, preferred_element_type=F32))


def _dot_hi_nt(a, b):
    return lax.dot_general(a, b, (((1,), (1,)), ((), ())), precision=HI, preferred_element_type=F32)


def _sig(x):
    return 1.0 / (1.0 + jnp.exp(-x))


def _dsilu(x, s):
    return s * (1.0 + x * (1.0 - s))


def _full(shape):
    nd = len(shape)
    return pl.BlockSpec(shape, lambda *_: (0,) * nd)


def rmsnorm_fwd(x, g, ex=None):
    s, d = x.shape
    tm = min(512, s)

    def body(x_ref, g_ref, o_ref):
        xv = x_ref[...]
        r = lax.rsqrt(jnp.mean(xv * xv, axis=-1, keepdims=True) + EPS)
        o_ref[...] = (xv * r * g_ref[...]).astype(o_ref.dtype)

    (h,), hosted = _call_hosting(
        body, "rmsnorm_fwd", s // tm,
        in_specs=[pl.BlockSpec((tm, d), lambda i: (i, 0)), _full((1, d))],
        out_specs=[pl.BlockSpec((tm, d), lambda i: (i, 0))],
        out_shape=[jax.ShapeDtypeStruct((s, d), ACT_DTYPE)], scratch=[], args=(x, g), ex=ex)
    return (h, hosted) if ex is not None else h


def mm_nn(a, b, tn, name):
    s, k = a.shape
    n = b.shape[1]
    tm = min(2048, s)

    def body(a_ref, b_ref, o_ref):
        o_ref[...] = _dot(a_ref[...], b_ref[...]).astype(o_ref.dtype)

    return pl.pallas_call(
        body, name=name, grid=(s // tm, n // tn),
        in_specs=[pl.BlockSpec((tm, k), lambda i, j: (i, 0)), pl.BlockSpec((k, tn), lambda i, j: (0, j))],
        out_specs=pl.BlockSpec((tm, tn), lambda i, j: (i, j)),
        out_shape=jax.ShapeDtypeStruct((s, n), ACT_DTYPE),
        compiler_params=_cparams(("parallel", "arbitrary")),
    )(a, b)


def mm_nt(a, b, tm, name):
    m, k = a.shape
    s = b.shape[0]
    ts = min(2048, s)

    def body(a_ref, b_ref, o_ref):
        o_ref[...] = _dot_nt(a_ref[...], b_ref[...]).astype(o_ref.dtype)

    return pl.pallas_call(
        body, name=name, grid=(s // ts, m // tm),
        in_specs=[pl.BlockSpec((tm, k), lambda i, j: (j, 0)), pl.BlockSpec((ts, k), lambda i, j: (i, 0))],
        out_specs=pl.BlockSpec((tm, ts), lambda i, j: (j, i)),
        out_shape=jax.ShapeDtypeStruct((m, s), ACT_DTYPE),
        compiler_params=_cparams(("parallel", "arbitrary")),
    )(a, b)


def mm_kn(a, b, tm, name):
    m, s = a.shape
    n = b.shape[1]
    ts = min(512, s)
    nt = s // ts

    def body(a_ref, b_ref, o_ref, acc_ref):
        @pl.when(pl.program_id(1) == 0)
        def _():
            acc_ref[...] = jnp.zeros_like(acc_ref)

        acc_ref[...] += _dot(a_ref[...], b_ref[...])

        @pl.when(pl.program_id(1) == nt - 1)
        def _():
            o_ref[...] = acc_ref[...].astype(o_ref.dtype)

    return pl.pallas_call(
        body, name=name, grid=(m // tm, nt),
        in_specs=[pl.BlockSpec((tm, ts), lambda j, t: (j, t)), pl.BlockSpec((ts, n), lambda j, t: (t, 0))],
        out_specs=pl.BlockSpec((tm, n), lambda j, t: (j, 0)),
        out_shape=jax.ShapeDtypeStruct((m, n), WIRE_DTYPE),
        scratch_shapes=[pltpu.VMEM((tm, n), F32)],
        compiler_params=_cparams(("parallel", "arbitrary")),
    )(a, b)


def mm_tn(a, b, tn, name):
    s, k = a.shape
    n = b.shape[1]
    ts = min(512, s)
    nt = s // ts

    def body(a_ref, b_ref, o_ref, acc_ref):
        @pl.when(pl.program_id(1) == 0)
        def _():
            acc_ref[...] = jnp.zeros_like(acc_ref)

        acc_ref[...] += _dot_tn(a_ref[...], b_ref[...])

        @pl.when(pl.program_id(1) == nt - 1)
        def _():
            o_ref[...] = acc_ref[...].astype(o_ref.dtype)

    return pl.pallas_call(
        body, name=name, grid=(n // tn, nt),
        in_specs=[pl.BlockSpec((ts, k), lambda j, t: (t, 0)), pl.BlockSpec((ts, tn), lambda j, t: (t, j))],
        out_specs=pl.BlockSpec((k, tn), lambda j, t: (0, j)),
        out_shape=jax.ShapeDtypeStruct((k, n), WIRE_DTYPE),
        scratch_shapes=[pltpu.VMEM((k, tn), F32)],
        compiler_params=_cparams(("parallel", "arbitrary")),
    )(a, b)


def dh_norm_bwd(dslabs, wslabs, x, g, dres, ex=None):
    s, d = x.shape
    tm = min(1024, s)
    widths = [ds.shape[0 if q == 0 else 1] for q, ds in enumerate(dslabs)]
    tks = [DH_TK] * len(widths)
    counts = [wd // t for wd, t in zip(widths, tks)]
    starts = [sum(counts[:i]) for i in range(len(counts))]
    nk = sum(counts)
    ns = len(dslabs)

    hosted = ex is not None
    ni = s // tm

    def mm_body(*refs):
        (own_in, (dh_ref,), _), hosted_refs = _split_hosted(refs, 2 * ns, 1, 0, ex)
        d_refs, w_refs = own_in[:ns], own_in[ns:]
        i, k = pl.program_id(0), pl.program_id(1)
        if hosted:
            @pl.when((i == 0) & (k == 0))
            def _():
                ex.start(*hosted_refs)

            @pl.when((i == ni - 1) & (k == nk - 1))
            def _():
                ex.relay(*hosted_refs)
                ex.wait(*hosted_refs)

        @pl.when(k == 0)
        def _():
            dh_ref[...] = jnp.zeros_like(dh_ref)

        for q in range(ns):
            @pl.when((k >= starts[q]) & (k < starts[q] + counts[q]))
            def _(q=q):
                if q == 0:
                    dh_ref[...] += _dot_tn(d_refs[q][...], w_refs[q][...])
                else:
                    dh_ref[...] += _dot_nt(d_refs[q][...], w_refs[q][...])

    def clamp(q):
        if q == 0:
            return pl.BlockSpec((tks[q], tm), lambda i, k: (jnp.clip(k - starts[q], 0, counts[q] - 1), i))
        return pl.BlockSpec((tm, tks[q]), lambda i, k: (i, jnp.clip(k - starts[q], 0, counts[q] - 1)))

    def clamp_w(q):
        if q == 0:
            return pl.BlockSpec((tks[q], d), lambda i, k: (jnp.clip(k - starts[q], 0, counts[q] - 1), 0))
        return pl.BlockSpec((d, tks[q]), lambda i, k: (0, jnp.clip(k - starts[q], 0, counts[q] - 1)))

    res = pl.pallas_call(
        mm_body, name="dh_matmul_scatter" if hosted else "dh_matmul", grid=(ni, nk),
        in_specs=([clamp(q) for q in range(ns)] + [clamp_w(q) for q in range(ns)]
                  + (ex.in_specs if hosted else [])),
        out_specs=[pl.BlockSpec((tm, d), lambda i, k: (i, 0))] + (ex.out_specs if hosted else []),
        out_shape=[jax.ShapeDtypeStruct((s, d), F32)] + (ex.out_shape if hosted else []),
        scratch_shapes=ex.scratch if hosted else [],
        compiler_params=_cparams(("arbitrary" if hosted else "parallel", "arbitrary")),
    )(*dslabs, *wslabs, *(ex.arrays if hosted else []))
    dh, ex_results = res[0], res[1:]

    te = min(512, s)

    def norm_body(dh_ref, x_ref, g_ref, dres_ref, dx_ref, dg_ref):
        @pl.when(pl.program_id(0) == 0)
        def _():
            dg_ref[...] = jnp.zeros_like(dg_ref)

        xv = x_ref[...]
        r = lax.rsqrt(jnp.mean(xv * xv, axis=-1, keepdims=True) + EPS)
        xn = xv * r
        dhv = dh_ref[...]
        dg_ref[...] += jnp.sum(dhv * xn, axis=0, keepdims=True)
        dxn = dhv * g_ref[...]
        dx_ref[...] = dres_ref[...] + r * (dxn - xn * jnp.mean(dxn * xn, axis=-1, keepdims=True))

    rows = pl.BlockSpec((te, d), lambda i: (i, 0))
    dx, dg = pl.pallas_call(
        norm_body, name="norm_bwd", grid=(s // te,),
        in_specs=[rows, rows, _full((1, d)), rows],
        out_specs=[rows, _full((1, d))],
        out_shape=[jax.ShapeDtypeStruct((s, d), F32), jax.ShapeDtypeStruct((1, d), F32)],
        compiler_params=_cparams(("arbitrary",)),
    )(dh, x, g, dres)
    return (dx, dg, ex_results) if hosted else (dx, dg)


def bias_table(rel_bias_t, onehot_t, maskadd):
    n = onehot_t.shape[1]
    tn = 8192

    def body(r_ref, o_ref, m_ref, out_ref):
        out_ref[...] = _dot_hi(r_ref[...], o_ref[...]) + m_ref[...]

    return pl.pallas_call(
        body, name="bias_table", grid=(n // tn,),
        in_specs=[_full((ATT_HEADS, REL_BUCKETS)), pl.BlockSpec((REL_BUCKETS, tn), lambda j: (0, j)),
                  pl.BlockSpec((1, tn), lambda j: (0, j))],
        out_specs=pl.BlockSpec((ATT_HEADS, tn), lambda j: (0, j)),
        out_shape=jax.ShapeDtypeStruct((ATT_HEADS, n), F32),
        compiler_params=_cparams(("parallel",)),
    )(rel_bias_t, onehot_t, maskadd)


def bias_table_bwd(dbias, onehot_t):
    n = onehot_t.shape[1]
    tn = 8192

    def body(d_ref, o_ref, out_ref):
        @pl.when(pl.program_id(0) == 0)
        def _():
            out_ref[...] = jnp.zeros_like(out_ref)

        out_ref[...] += _dot_hi_nt(d_ref[...], o_ref[...])

    return pl.pallas_call(
        body, name="bias_table_bwd", grid=(n // tn,),
        in_specs=[pl.BlockSpec((ATT_HEADS, tn), lambda j: (0, j)), pl.BlockSpec((REL_BUCKETS, tn), lambda j: (0, j))],
        out_specs=_full((ATT_HEADS, REL_BUCKETS)),
        out_shape=jax.ShapeDtypeStruct((ATT_HEADS, REL_BUCKETS), F32),
        compiler_params=_cparams(("arbitrary",)),
    )(dbias, onehot_t)


def _fold(full, tri):
    return jnp.where(tri, full[BLK:2 * BLK], full[0:BLK])


def _unfold(folded, tri):
    return jnp.concatenate([jnp.where(tri, 0.0, folded), jnp.where(tri, folded, 0.0)], axis=0)


GROUP_HEADS = ATT_HEADS // 2
GROUP_LANES = GROUP_HEADS * BLK


def _att_group(qg, kcat, vt_cat, bias_g, sink_g, tri, no_prev):
    l = _fold(_dot(kcat, qg), tri) * (HEAD_DIM ** -0.5) + bias_g
    l = jnp.where(no_prev, NEG, l)
    m = jnp.maximum(jnp.max(l, axis=0, keepdims=True), sink_g)
    p = jnp.exp(l - m)
    es = jnp.exp(sink_g - m)
    inv = 1.0 / (jnp.sum(p, axis=0, keepdims=True) + es)
    p = p * inv
    pcat = _unfold(p, tri)
    return p, pcat, es * inv, _dot(vt_cat, pcat)


ATT_SUB = 4


def _heads_to_lanes(ref, row0, ln):
    return jnp.concatenate([ref[row0 + j * HEAD_DIM:row0 + (j + 1) * HEAD_DIM, ln] for j in range(GROUP_HEADS)], axis=1)


def _lanes_to_heads(ref, row0, ln, val):
    for j in range(GROUP_HEADS):
        ref[row0 + j * HEAD_DIM:row0 + (j + 1) * HEAD_DIM, ln] = val[:, j * BLK:(j + 1) * BLK].astype(ref.dtype)


def _kv_cat(kvp, kvc, g):
    lo = g * HEAD_DIM
    kt_cat = jnp.concatenate([kvp[lo:lo + HEAD_DIM], kvc[lo:lo + HEAD_DIM]], axis=1)
    vt_cat = jnp.concatenate([kvp[128 + lo:128 + lo + HEAD_DIM], kvc[128 + lo:128 + lo + HEAD_DIM]], axis=1)
    return kt_cat, vt_cat


def _tri_masks(n):
    row = lax.broadcasted_iota(jnp.int32, (BLK, GROUP_LANES), 0)
    query = lax.broadcasted_iota(jnp.int32, (BLK, GROUP_LANES), 1) & (BLK - 1)
    tri = row <= query
    return tri, (n == 0) & jnp.logical_not(tri)


def _split_hosted(refs, n_in, n_out, n_scratch, ex):
    na = ex.na if ex is not None else 0
    o = 0
    parts = []
    for cnt in (n_in, na, n_out, na, n_scratch, 3 if ex is not None else 0):
        parts.append(refs[o:o + cnt])
        o += cnt
    own_in, ex_in, own_out, ex_out, own_scr, ex_sems = parts
    return (own_in, own_out, own_scr), (ex_in, ex_out, ex_sems)


def _call_hosting(body, name, nsteps, in_specs, out_specs, out_shape, scratch, args, ex):
    n_in, n_out, n_scr = len(in_specs), len(out_specs), len(scratch)
    hosted = ex is not None

    def full_body(*refs):
        (own_in, own_out, own_scr), hosted_refs = _split_hosted(refs, n_in, n_out, n_scr, ex)
        if hosted:
            @pl.when(pl.program_id(0) == 0)
            def _():
                ex.start(*hosted_refs)

            @pl.when(pl.program_id(0) == max(nsteps - 4, 0))
            def _():
                ex.relay(*hosted_refs)

            @pl.when(pl.program_id(0) == nsteps - 1)
            def _():
                ex.wait(*hosted_refs)

        body(*own_in, *own_out, *own_scr)

    res = pl.pallas_call(
        full_body, name=name + "_hosting" if hosted else name, grid=(nsteps,),
        in_specs=list(in_specs) + (ex.in_specs if hosted else []),
        out_specs=list(out_specs) + (ex.out_specs if hosted else []),
        out_shape=list(out_shape) + (ex.out_shape if hosted else []),
        scratch_shapes=list(scratch) + (ex.scratch if hosted else []),
        compiler_params=_cparams(("arbitrary",)),
    )(*args, *(ex.arrays if hosted else []))
    return res[:n_out], res[n_out:]


def attn_fwd(pa, bias, sinks, ex=None):
    s = pa.shape[1]
    nsteps = s // (ATT_SUB * BLK)

    def body(pa_ref, kvp_ref, bias_ref, sink_ref, y_ref):
        for sub in range(ATT_SUB):
            n = pl.program_id(0) * ATT_SUB + sub
            ln = slice(sub * BLK, (sub + 1) * BLK)
            kvc = pa_ref[2048:2304, ln]
            kvp = kvp_ref[...] if sub == 0 else pa_ref[2048:2304, (sub - 1) * BLK:sub * BLK]
            tri, no_prev = _tri_masks(n)
            for g in range(2):
                kt_cat, vt_cat = _kv_cat(kvp, kvc, g)
                row0 = g * GROUP_HEADS * HEAD_DIM
                _, _, _, o = _att_group(_heads_to_lanes(pa_ref, row0, ln), kt_cat.astype(F32).T, vt_cat, bias_ref[g],
                                        sink_ref[g:g + 1, :], tri, no_prev)
                z = _heads_to_lanes(pa_ref, 1024 + row0, ln).astype(F32)
                _lanes_to_heads(y_ref, row0, ln, o * z * _sig(z))

    (y,), hosted = _call_hosting(
        body, "attn_fwd", nsteps,
        in_specs=[pl.BlockSpec((ATT_COLS, ATT_SUB * BLK), lambda n: (0, n)),
                  pl.BlockSpec((256, BLK), lambda n: (8, jnp.maximum(ATT_SUB * n - 1, 0))),
                  _full((2, BLK, GROUP_LANES)), _full((2, GROUP_LANES))],
        out_specs=[pl.BlockSpec((1024, ATT_SUB * BLK), lambda n: (0, n))],
        out_shape=[jax.ShapeDtypeStruct((1024, s), ACT_DTYPE)], scratch=[],
        args=(pa, pa, bias, sinks), ex=ex)
    return (y, hosted) if ex is not None else y


def attn_bwd(pa, dy, bias, sinks, dbias_in):
    s = pa.shape[1]
    nsteps = s // (ATT_SUB * BLK)

    def body(pa_ref, kvp_ref, dy_ref, bias_ref, sink_ref, dbin_ref, dpa_ref, dbias_ref, dsink_ref, carry, dsink_acc):
        i = pl.program_id(0)

        @pl.when(i == 0)
        def _():
            dbias_ref[...] = dbin_ref[...]
            dsink_acc[...] = jnp.zeros_like(dsink_acc)
            carry[...] = jnp.zeros_like(carry)

        scale = HEAD_DIM ** -0.5
        for sub in reversed(range(ATT_SUB)):
            n = (nsteps - 1 - i) * ATT_SUB + sub
            ln = slice(sub * BLK, (sub + 1) * BLK)
            kvc = pa_ref[2048:2304, ln]
            kvp = kvp_ref[...] if sub == 0 else pa_ref[2048:2304, (sub - 1) * BLK:sub * BLK]
            tri, no_prev = _tri_masks(n)
            for g in range(2):
                kt_cat, vt_cat = _kv_cat(kvp, kvc, g)
                row0 = g * GROUP_HEADS * HEAD_DIM
                qg = _heads_to_lanes(pa_ref, row0, ln)
                p, pcat, psink, o = _att_group(qg, kt_cat.astype(F32).T, vt_cat, bias_ref[g], sink_ref[g:g + 1, :],
                                               tri, no_prev)
                z = _heads_to_lanes(pa_ref, 1024 + row0, ln).astype(F32)
                dyg = _heads_to_lanes(dy_ref, row0, ln).astype(F32)
                sz = _sig(z)
                d_o = dyg * z * sz
                _lanes_to_heads(dpa_ref, 1024 + row0, ln, dyg * _dsilu(z, sz) * o)
                delta = jnp.sum(d_o * o, axis=0, keepdims=True)
                dl = p * (_fold(_dot(vt_cat.astype(F32).T, d_o), tri) - delta)
                dsink_acc[g:g + 1, :] += psink * delta
                dbias_ref[g] += dl
                dlcat = _unfold(dl, tri)
                _lanes_to_heads(dpa_ref, row0, ln, _dot(kt_cat, dlcat) * scale)
                for q, dkv in enumerate((_dot_nt(qg, dlcat) * scale, _dot_nt(d_o, pcat))):
                    r0 = q * 128 + g * HEAD_DIM
                    dpa_ref[2048 + r0:2048 + r0 + HEAD_DIM, ln] = (
                        dkv[:, BLK:2 * BLK] + carry[r0:r0 + HEAD_DIM, :]).astype(dpa_ref.dtype)
                    carry[r0:r0 + HEAD_DIM, :] = dkv[:, 0:BLK]

        @pl.when(i == nsteps - 1)
        def _():
            lane = lax.broadcasted_iota(jnp.int32, (1, 128), 1)
            dsink = jnp.zeros((1, 128), F32)
            for h in range(ATT_HEADS):
                g, j = divmod(h, GROUP_HEADS)
                tot = jnp.sum(dsink_acc[g:g + 1, j * BLK:(j + 1) * BLK], axis=1, keepdims=True)
                dsink = dsink + jnp.where(lane == h, -tot, 0.0)
            dsink_ref[...] = dsink

    return pl.pallas_call(
        body, name="attn_bwd", grid=(nsteps,),
        in_specs=[pl.BlockSpec((ATT_COLS, ATT_SUB * BLK), lambda i: (0, nsteps - 1 - i)),
                  pl.BlockSpec((256, BLK), lambda i: (8, jnp.maximum(ATT_SUB * (nsteps - 1 - i) - 1, 0))),
                  pl.BlockSpec((1024, ATT_SUB * BLK), lambda i: (0, nsteps - 1 - i)),
                  _full((2, BLK, GROUP_LANES)), _full((2, GROUP_LANES)), _full((2, BLK, GROUP_LANES))],
        out_specs=[pl.BlockSpec((ATT_COLS, ATT_SUB * BLK), lambda i: (0, nsteps - 1 - i)),
                   _full((2, BLK, GROUP_LANES)), _full((1, 128))],
        out_shape=[jax.ShapeDtypeStruct((ATT_COLS, s), ACT_DTYPE),
                   jax.ShapeDtypeStruct((2, BLK, GROUP_LANES), F32),
                   jax.ShapeDtypeStruct((1, 128), F32)],
        scratch_shapes=[pltpu.VMEM((256, BLK), F32), pltpu.VMEM((2, GROUP_LANES), F32)],
        compiler_params=_cparams(("arbitrary",)),
    )(pa, pa, dy, bias, sinks, dbias_in)


def _layernorm(v, g, b):
    mu = jnp.mean(v, axis=-1, keepdims=True)
    vc = v - mu
    rstd = lax.rsqrt(jnp.mean(vc * vc, axis=-1, keepdims=True) + EPS)
    xhat = vc * rstd
    return xhat, rstd, xhat * g + b


def sgu_fwd(ps, ln_g, ln_b, w_tril, b_t):
    s = ps.shape[0]
    rows = min(4 * BLK, s)

    def body(ps_ref, g_ref, b_ref, w_ref, bt_ref, y_ref):
        u = ps_ref[:, 0:1024].astype(F32)
        v = ps_ref[:, 1024:2048].astype(F32)
        z = ps_ref[:, 2048:3072].astype(F32)
        _, _, vn = _layernorm(v, g_ref[...], b_ref[...])
        gate = u * z * _sig(z)
        for c in range(rows // BLK):
            ch = slice(c * BLK, (c + 1) * BLK)
            for g in range(SG_GROUPS):
                sl = slice(g * 128, (g + 1) * 128)
                mixed = _dot(w_ref[g], vn[ch, sl]) + bt_ref[:, g:g + 1]
                y_ref[ch, sl] = (gate[ch, sl] * mixed).astype(y_ref.dtype)

    return pl.pallas_call(
        body, name="sgu_fwd", grid=(s // rows,),
        in_specs=[pl.BlockSpec((rows, SG_COLS), lambda c: (c, 0)), _full((1, 1024)), _full((1, 1024)),
                  _full((SG_GROUPS, BLK, BLK)), _full((BLK, 128))],
        out_specs=pl.BlockSpec((rows, 1024), lambda c: (c, 0)),
        out_shape=jax.ShapeDtypeStruct((s, 1024), ACT_DTYPE),
        compiler_params=_cparams(("parallel",)),
    )(ps, ln_g, ln_b, w_tril, b_t)


def sgu_bwd(ps, dy, ln_g, ln_b, w_tril, w_tril_t, b_t):
    s = ps.shape[0]
    rows = min(2 * BLK, s)

    def body(ps_ref, dy_ref, g_ref, b_ref, w_ref, wt_ref, bt_ref, dps_ref, dw_ref, dbt_ref, dg_ref, db_ref, dvn_scr):
        @pl.when(pl.program_id(0) == 0)
        def _():
            dw_ref[...] = jnp.zeros_like(dw_ref)
            dbt_ref[...] = jnp.zeros_like(dbt_ref)
            dg_ref[...] = jnp.zeros_like(dg_ref)
            db_ref[...] = jnp.zeros_like(db_ref)

        u = ps_ref[:, 0:1024].astype(F32)
        v = ps_ref[:, 1024:2048].astype(F32)
        z = ps_ref[:, 2048:3072].astype(F32)
        dy = dy_ref[...].astype(F32)
        xhat, rstd, vn = _layernorm(v, g_ref[...], b_ref[...])
        sz = _sig(z)
        silu = z * sz
        row = lax.broadcasted_iota(jnp.int32, (BLK, BLK), 0)
        colm = lax.broadcasted_iota(jnp.int32, (BLK, BLK), 1)
        tril = row >= colm
        dbt = jnp.zeros((BLK, 128), F32)
        dsilu_z = _dsilu(z, sz)
        for c in range(rows // BLK):
            ch = slice(c * BLK, (c + 1) * BLK)
            for g in range(SG_GROUPS):
                sl = slice(g * 128, (g + 1) * 128)
                vng = vn[ch, sl]
                mixed = _dot(w_ref[g], vng) + bt_ref[:, g:g + 1]
                dyg, ug = dy[ch, sl], u[ch, sl]
                dps_ref[ch, sl] = (dyg * mixed * silu[ch, sl]).astype(dps_ref.dtype)
                dps_ref[ch, 2048 + g * 128:2048 + (g + 1) * 128] = (
                    dyg * ug * mixed * dsilu_z[ch, sl]).astype(dps_ref.dtype)
                dm = dyg * ug * silu[ch, sl]
                dw_ref[g] += jnp.where(tril, _dot_nt(dm, vng), 0.0)
                dbt = dbt + jnp.where(colm == g, jnp.sum(dm, axis=1, keepdims=True), 0.0)
                dvn_scr[ch, sl] = _dot(wt_ref[g], dm)
        dbt_ref[...] += dbt
        dvn = dvn_scr[...]
        dg_ref[...] += jnp.sum(dvn * xhat, axis=0, keepdims=True)
        db_ref[...] += jnp.sum(dvn, axis=0, keepdims=True)
        dxh = dvn * g_ref[...]
        dv = rstd * (dxh - jnp.mean(dxh, axis=-1, keepdims=True)
                     - xhat * jnp.mean(dxh * xhat, axis=-1, keepdims=True))
        dps_ref[:, 1024:2048] = dv.astype(dps_ref.dtype)

    return pl.pallas_call(
        body, name="sgu_bwd", grid=(s // rows,),
        in_specs=[pl.BlockSpec((rows, SG_COLS), lambda c: (c, 0)), pl.BlockSpec((rows, 1024), lambda c: (c, 0)),
                  _full((1, 1024)), _full((1, 1024)), _full((SG_GROUPS, BLK, BLK)), _full((SG_GROUPS, BLK, BLK)),
                  _full((BLK, 128))],
        out_specs=[pl.BlockSpec((rows, SG_COLS), lambda c: (c, 0)), _full((SG_GROUPS, BLK, BLK)), _full((BLK, 128)),
                   _full((1, 1024)), _full((1, 1024))],
        out_shape=[jax.ShapeDtypeStruct((s, SG_COLS), ACT_DTYPE), jax.ShapeDtypeStruct((SG_GROUPS, BLK, BLK), F32),
                   jax.ShapeDtypeStruct((BLK, 128), F32), jax.ShapeDtypeStruct((1, 1024), F32),
                   jax.ShapeDtypeStruct((1, 1024), F32)],
        scratch_shapes=[pltpu.VMEM((rows, 1024), F32)],
        compiler_params=_cparams(("arbitrary",)),
    )(ps, dy, ln_g, ln_b, w_tril, w_tril_t, b_t)


def _shift_down(cur, prev16, k):
    if k == 0:
        return cur
    r = pltpu.roll(cur, k, 0)
    rp = pltpu.roll(prev16, k, 0)
    row = lax.broadcasted_iota(jnp.int32, (8, cur.shape[1]), 0)
    return jnp.concatenate([jnp.where(row < k, rp[0:8], r[0:8]), r[8:]], axis=0)


def _shift_up(cur, next16, k):
    if k == 0:
        return cur
    n = cur.shape[0]
    r = pltpu.roll(cur, n - k, 0)
    rn = pltpu.roll(next16, 16 - k, 0)
    row = lax.broadcasted_iota(jnp.int32, (8, cur.shape[1]), 0)
    return jnp.concatenate([r[:n - 8], jnp.where(row >= 8 - k, rn[8:16], r[n - 8:])], axis=0)


def _bcast8(v):
    return jnp.broadcast_to(v, (16, v.shape[1]))


def _causal_conv(xbc, prev16, cw, cbias):
    pre = cbias + cw[3:4] * xbc
    for k in (1, 2, 3):
        pre = pre + cw[3 - k:4 - k] * _shift_down(xbc, prev16, k)
    return pre


class _Ssd:
    def __init__(self, pre, dtr, dtb, alog, dsk, tri, e):
        self.pre = pre
        self.sg = _sig(pre)
        act = pre * self.sg
        self.xs = act[:, 0:SSM_WIDTH]
        self.bm = act[:, SSM_WIDTH:SSM_WIDTH + 512]
        self.cm = act[:, SSM_WIDTH + 512:CONV_DIM]
        self.dtp = dtr + dtb
        self.dt = jnp.maximum(self.dtp, 0.0) + jnp.log(1.0 + jnp.exp(-jnp.abs(self.dtp)))
        self.a = -jnp.exp(alog)
        self.acs = _dot_hi(tri, self.dt * self.a)
        self.acs_t = self.acs.T
        tot = self.acs[BLK - 1:BLK]
        self.ecs = jnp.exp(self.acs)
        self.dte = jnp.exp(tot - self.acs)
        self.cd = jnp.exp(tot)
        self.dt_x = _dot_onehot(self.dt, e)
        self.ecs_x = _dot_onehot(self.ecs, e)
        self.dte_x = _dot_onehot(self.dte, e)
        self.cd_x = _dot_onehot(_bcast8(self.cd), e)[0:1]
        self.d_x = _dot_onehot(_bcast8(dsk), e)[0:1]
        self.xdt = self.xs * self.dt_x
        row = lax.broadcasted_iota(jnp.int32, (BLK, BLK), 0)
        col = lax.broadcasted_iota(jnp.int32, (BLK, BLK), 1)
        self.tril = row >= col

    def group(self, g):
        sl = slice(g * 128, (g + 1) * 128)
        bg, cg = self.bm[:, sl], self.cm[:, sl]
        return bg, cg, _dot_nt(cg, bg)

    def decay(self, h):
        seg = self.acs[:, h:h + 1] - self.acs_t[h:h + 1, :]
        return jnp.exp(jnp.where(self.tril, seg, NEG))

    def y_pre_gate(self, ht_of, yd_scr, yoff_scr):
        for g in range(SSM_GROUPS):
            bg, cg, cb = self.group(g)
            for j in range(8):
                h = g * 8 + j
                sl = slice(h * 64, (h + 1) * 64)
                yd_scr[:, sl] = _dot(cb * self.decay(h), self.xdt[:, sl])
            gs = slice(g * SSM_GW, (g + 1) * SSM_GW)
            yoff_scr[:, gs] = _dot(cg, ht_of(g)) * self.ecs_x[:, gs]
        return yd_scr[...] + yoff_scr[...] + self.d_x * self.xs


def _ssd_consts():
    hh = lax.broadcasted_iota(jnp.int32, (128, SSM_WIDTH), 0)
    ch = lax.broadcasted_iota(jnp.int32, (128, SSM_WIDTH), 1)
    e = (ch // 64 == hh).astype(jnp.bfloat16)
    row = lax.broadcasted_iota(jnp.int32, (BLK, BLK), 0)
    col = lax.broadcasted_iota(jnp.int32, (BLK, BLK), 1)
    tri = (row >= col).astype(F32)
    return tri, e


def _pad_lanes(v, n=128):
    return jnp.pad(v, ((0, 0), (0, n - v.shape[1])))


def ssd_fwd(pm, cw, cbias, dtb, alog, dsk, ng, ex=None):
    s = pm.shape[0]
    nc = s // BLK
    tri, e = _ssd_consts()

    def body(pm_ref, prev_ref, cw_ref, cb_ref, dtb_ref, al_ref, d_ref, ng_ref, tri_ref, e_ref,
             y_ref, st_ref, pre_ref, ht_ref, yd_scr, yoff_scr):
        c = pl.program_id(0)

        @pl.when(c == 0)
        def _():
            ht_ref[...] = jnp.zeros_like(ht_ref)

        xbc = pm_ref[:, 0:CONV_DIM].astype(F32)
        prev16 = jnp.where(c == 0, 0.0, prev_ref[...].astype(F32))
        pre = _causal_conv(xbc, prev16, cw_ref[...], cb_ref[...])
        pre_ref[...] = pre.astype(pre_ref.dtype)
        f = _Ssd(pre, pm_ref[:, DT_OFF:DT_OFF + 128].astype(F32), dtb_ref[...], al_ref[...], d_ref[...],
                 tri_ref[...], e_ref[...])
        st_ref[0] = ht_ref[...]
        y = f.y_pre_gate(lambda g: ht_ref[g], yd_scr, yoff_scr)
        for g in range(SSM_GROUPS):
            bg, _, _ = f.group(g)
            gs = slice(g * SSM_GW, (g + 1) * SSM_GW)
            ht_ref[g] = ht_ref[g] * f.cd_x[:, gs] + _dot_tn(bg, f.xdt[:, gs] * f.dte_x[:, gs])
        z = pm_ref[:, CONV_DIM:CONV_DIM + SSM_WIDTH].astype(F32)
        ypre = y * z * _sig(z)
        for g in range(SSM_GROUPS):
            gs = slice(g * SSM_GW, (g + 1) * SSM_GW)
            yg = ypre[:, gs]
            rr = lax.rsqrt(jnp.mean(yg * yg, axis=-1, keepdims=True) + EPS)
            y_ref[:, gs] = (yg * rr * ng_ref[:, gs]).astype(y_ref.dtype)

    own, hosted = _call_hosting(
        body, "ssd_fwd", nc,
        in_specs=[pl.BlockSpec((BLK, SSM_COLS), lambda c: (c, 0)),
                  pl.BlockSpec((16, CONV_DIM), lambda c: (jnp.maximum(8 * c - 1, 0), 0)),
                  _full((4, CONV_DIM)), _full((1, CONV_DIM)), _full((1, 128)), _full((1, 128)), _full((1, 128)),
                  _full((1, SSM_WIDTH)), _full((BLK, BLK)), _full((128, SSM_WIDTH))],
        out_specs=[pl.BlockSpec((BLK, SSM_WIDTH), lambda c: (c, 0)),
                   pl.BlockSpec((1, SSM_GROUPS, 128, SSM_GW), lambda c: (c, 0, 0, 0)),
                   pl.BlockSpec((BLK, CONV_DIM), lambda c: (c, 0))],
        out_shape=[jax.ShapeDtypeStruct((s, SSM_WIDTH), ACT_DTYPE),
                   jax.ShapeDtypeStruct((nc, SSM_GROUPS, 128, SSM_GW), F32),
                   jax.ShapeDtypeStruct((s, CONV_DIM), ACT_DTYPE)],
        scratch=[pltpu.VMEM((SSM_GROUPS, 128, SSM_GW), F32), pltpu.VMEM((BLK, SSM_WIDTH), F32),
                 pltpu.VMEM((BLK, SSM_WIDTH), F32)],
        args=(pm, pm, cw, cbias, dtb, alog, dsk, ng, tri, e), ex=ex)
    return (*own, hosted) if ex is not None else tuple(own)


def ssd_bwd(pm, pre, dy, states, cw, dtb, alog, dsk, ng, ex=None):
    s = pm.shape[0]
    nc = s // BLK
    tri, e = _ssd_consts()
    tri_t, e_t = tri.T, e.T

    def body(pm_ref, pre_ref, dy_ref, st_ref, cw_ref, dtb_ref, al_ref, d_ref, ng_ref,
             tri_ref, trit_ref, e_ref, et_ref,
             dpm_ref, dcw_ref, dcb_ref, dvec_ref, dng_ref,
             dht_ref, dcar_ref, yd_scr, yoff_scr, dx_scr, r2_scr, hs_scr, da_scr, dat_scr, dd_scr, dbc_scr):
        i = pl.program_id(0)
        n = nc - 1 - i

        @pl.when(i == 0)
        def _():
            dht_ref[...] = jnp.zeros_like(dht_ref)
            dcar_ref[...] = jnp.zeros_like(dcar_ref)
            dcw_ref[...] = jnp.zeros_like(dcw_ref)
            dcb_ref[...] = jnp.zeros_like(dcb_ref)
            dvec_ref[...] = jnp.zeros_like(dvec_ref)
            dng_ref[...] = jnp.zeros_like(dng_ref)
            dd_scr[...] = jnp.zeros_like(dd_scr)
            da_scr[...] = jnp.zeros_like(da_scr)
            dat_scr[...] = jnp.zeros_like(dat_scr)

        cw = cw_ref[...]
        f = _Ssd(pre_ref[...].astype(F32), pm_ref[:, DT_OFF:DT_OFF + 128].astype(F32), dtb_ref[...], al_ref[...],
                 d_ref[...], tri_ref[...], e_ref[...])
        et = et_ref[...]
        y = f.y_pre_gate(lambda g: st_ref[0, g], yd_scr, yoff_scr)

        z = pm_ref[:, CONV_DIM:CONV_DIM + SSM_WIDTH].astype(F32)
        dyv = dy_ref[...].astype(F32)
        sz = _sig(z)
        silu = z * sz
        ypre = y * silu
        for g in range(SSM_GROUPS):
            gs = slice(g * SSM_GW, (g + 1) * SSM_GW)
            yg = ypre[:, gs]
            rr = lax.rsqrt(jnp.mean(yg * yg, axis=-1, keepdims=True) + EPS)
            nrm = yg * rr
            dng_ref[:, gs] += jnp.sum(dyv[:, gs] * nrm, axis=0, keepdims=True)
            dn = dyv[:, gs] * ng_ref[:, gs]
            dx_scr[:, gs] = rr * (dn - nrm * jnp.mean(dn * nrm, axis=-1, keepdims=True))
        dypre = dx_scr[...]
        d_y = dypre * silu
        dpm_ref[:, CONV_DIM:CONV_DIM + SSM_WIDTH] = (dypre * y * _dsilu(z, sz)).astype(dpm_ref.dtype)

        for g in range(SSM_GROUPS):
            bg, cg, cb = f.group(g)
            gs = slice(g * SSM_GW, (g + 1) * SSM_GW)
            htg = st_ref[0, g]
            dhn = dht_ref[g]
            dcb = jnp.zeros((BLK, BLK), F32)
            for j in range(8):
                h = g * 8 + j
                sl = slice(h * 64, (h + 1) * 64)
                dec = f.decay(h)
                dyh = d_y[:, sl]
                dmd = _dot_nt(dyh, f.xdt[:, sl]) * dec
                dcb = dcb + dmd
                gm = dmd * cb
                da_scr[:, h:h + 1] = jnp.sum(gm, axis=1, keepdims=True)
                dat_scr[h:h + 1, :] = jnp.sum(gm, axis=0, keepdims=True)
                dx_scr[:, sl] = _dot_tn(cb * dec, dyh)
            dz = f.ecs_x[:, gs] * d_y[:, gs]
            dbc_scr[:, 512 + g * 128:512 + (g + 1) * 128] = _dot(dcb, bg) + _dot_nt(dz, htg)
            dbc_scr[:, g * 128:(g + 1) * 128] = _dot_tn(dcb, cg) + _dot_nt(f.xdt[:, gs] * f.dte_x[:, gs], dhn)
            dws = _dot(bg, dhn)
            dx_scr[:, gs] += f.dte_x[:, gs] * dws
            r2_scr[:, gs] = dws * f.xdt[:, gs]
            hs_scr[:, gs] = _bcast8(jnp.sum(dhn * htg, axis=0, keepdims=True))
            dht_ref[g] = f.cd_x[:, gs] * dhn + _dot_tn(cg, dz)
        d_x = dx_scr[...]
        r1 = _dot(d_y * yoff_scr[...], et)
        r2 = _dot(r2_scr[...], et) * f.dte
        dcd = _dot_onehot(hs_scr[...], et)[0:1]
        d_tot = jnp.sum(r2, axis=0, keepdims=True) + dcd * f.cd
        row = lax.broadcasted_iota(jnp.int32, (BLK, 128), 0)
        d_a = da_scr[...] - dat_scr[...].T + r1 - r2 + jnp.where(row == BLK - 1, d_tot, 0.0)
        dadt = _dot_hi(trit_ref[...], d_a)
        ddt = dadt * f.a + _dot(d_x * f.xs, et)
        lane = lax.broadcasted_iota(jnp.int32, (BLK, 128), 1)
        dr = jnp.where(lane < SSM_HEADS, ddt * _sig(f.dtp), 0.0)
        dvec_ref[0:1, :] += jnp.sum(dr, axis=0, keepdims=True)
        dvec_ref[1:2, :] += jnp.sum(dadt * f.dt, axis=0, keepdims=True) * f.a
        dd_scr[...] += _bcast8(jnp.sum(d_y * f.xs, axis=0, keepdims=True))
        dpm_ref[:, DT_OFF:DT_OFF + 128] = dr.astype(dpm_ref.dtype)
        dpm_ref[:, DT_OFF + 128:SSM_COLS] = jnp.zeros((BLK, 128), dpm_ref.dtype)

        dxs = d_x * f.dt_x + f.d_x * d_y
        dact = jnp.concatenate([dxs, dbc_scr[...]], axis=1)
        dpre = dact * _dsilu(f.pre, f.sg)
        dcb_ref[...] += jnp.sum(dpre, axis=0, keepdims=True)
        xbc = pm_ref[:, 0:CONV_DIM].astype(F32)
        dxraw = jnp.zeros((BLK, CONV_DIM), F32)
        nxt = dcar_ref[...]
        for k in range(4):
            ahead = _shift_up(dpre, nxt, k)
            dcw_ref[3 - k:4 - k, :] += jnp.sum(ahead * xbc, axis=0, keepdims=True)
            dxraw = dxraw + cw[3 - k:4 - k] * ahead
        dcar_ref[...] = dpre[0:16]
        dpm_ref[:, 0:CONV_DIM] = dxraw.astype(dpm_ref.dtype)

        @pl.when(i == nc - 1)
        def _():
            dvec_ref[2:3, :] = _dot_onehot(dd_scr[...], et)[0:1]

    own, hosted = _call_hosting(
        body, "ssd_bwd", nc,
        in_specs=[pl.BlockSpec((BLK, SSM_COLS), lambda i: (nc - 1 - i, 0)),
                  pl.BlockSpec((BLK, CONV_DIM), lambda i: (nc - 1 - i, 0)),
                  pl.BlockSpec((BLK, SSM_WIDTH), lambda i: (nc - 1 - i, 0)),
                  pl.BlockSpec((1, SSM_GROUPS, 128, SSM_GW), lambda i: (nc - 1 - i, 0, 0, 0)),
                  _full((4, CONV_DIM)), _full((1, 128)), _full((1, 128)), _full((1, 128)),
                  _full((1, SSM_WIDTH)), _full((BLK, BLK)), _full((BLK, BLK)), _full((128, SSM_WIDTH)),
                  _full((SSM_WIDTH, 128))],
        out_specs=[pl.BlockSpec((BLK, SSM_COLS), lambda i: (nc - 1 - i, 0)),
                   _full((8, CONV_DIM)), _full((1, CONV_DIM)), _full((8, 128)), _full((1, SSM_WIDTH))],
        out_shape=[jax.ShapeDtypeStruct((s, SSM_COLS), ACT_DTYPE), jax.ShapeDtypeStruct((8, CONV_DIM), F32),
                   jax.ShapeDtypeStruct((1, CONV_DIM), F32), jax.ShapeDtypeStruct((8, 128), F32),
                   jax.ShapeDtypeStruct((1, SSM_WIDTH), F32)],
        scratch=[pltpu.VMEM((SSM_GROUPS, 128, SSM_GW), F32), pltpu.VMEM((16, CONV_DIM), F32),
                 pltpu.VMEM((BLK, SSM_WIDTH), F32), pltpu.VMEM((BLK, SSM_WIDTH), F32),
                 pltpu.VMEM((BLK, SSM_WIDTH), F32), pltpu.VMEM((BLK, SSM_WIDTH), F32),
                 pltpu.VMEM((16, SSM_WIDTH), F32), pltpu.VMEM((BLK, 128), F32), pltpu.VMEM((128, BLK), F32),
                 pltpu.VMEM((16, SSM_WIDTH), F32), pltpu.VMEM((BLK, 1024), F32)],
        args=(pm, pre, dy, states, cw, dtb, alog, dsk, ng, tri, tri_t, e, e_t), ex=ex)
    return (*own, hosted) if ex is not None else tuple(own)


def merge_fwd(x, ya, ys, ym, pg, wa, ws, wm, wo, g_post, target=None):
    s, d = x.shape
    tm = min(256, s)
    with_loss = target is not None

    def body(*refs):
        x_ref, ya_ref, ys_ref, ym_ref, pg_ref, wa_ref, ws_ref, wm_ref, wo_ref, g_ref = refs[:10]
        if with_loss:
            t_ref, xo_ref, l_ref, ba_ref, bs_ref, bm_ref, mg_ref, out_ref = refs[10:]
        else:
            xo_ref, ba_ref, bs_ref, bm_ref, mg_ref, out_ref = refs[10:]
        ba = _dot_tn(ya_ref[...], wa_ref[...])
        bs = _dot(ys_ref[...], ws_ref[...])
        bm = _dot(ym_ref[...], wm_ref[...])
        merged = (_sig(pg_ref[:, 0:d].astype(F32)) * ba + _sig(pg_ref[:, d:2 * d].astype(F32)) * bs
                  + _sig(pg_ref[:, 2 * d:3 * d].astype(F32)) * bm)
        out = _dot(merged, wo_ref[...])
        r = lax.rsqrt(jnp.mean(out * out, axis=-1, keepdims=True) + EPS)
        y = x_ref[...] + out * r * g_ref[...]
        if with_loss:
            @pl.when(pl.program_id(0) == 0)
            def _():
                l_ref[...] = jnp.zeros_like(l_ref)

            err = y - t_ref[...]
            xo_ref[...] = err * (1.0 / d)
            part = jnp.sum(jnp.sum(err * err, axis=-1, keepdims=True) * (1.0 / d), axis=0, keepdims=True)
            l_ref[...] += 0.5 * jnp.broadcast_to(part, l_ref.shape)
        else:
            xo_ref[...] = y
        ba_ref[...] = ba.astype(ba_ref.dtype)
        bs_ref[...] = bs.astype(bs_ref.dtype)
        bm_ref[...] = bm.astype(bm_ref.dtype)
        mg_ref[...] = merged.astype(mg_ref.dtype)
        out_ref[...] = out.astype(out_ref.dtype)

    rows = lambda w: pl.BlockSpec((tm, w), lambda i: (i, 0))
    act = jax.ShapeDtypeStruct((s, d), ACT_DTYPE)
    loss_spec = [_full((8, 128))] if with_loss else []
    loss_shape = [jax.ShapeDtypeStruct((8, 128), F32)] if with_loss else []
    return pl.pallas_call(
        body, name="merge_fwd_loss" if with_loss else "merge_fwd", grid=(s // tm,),
        in_specs=[rows(d), pl.BlockSpec((d, tm), lambda i: (0, i)), rows(d), rows(2 * d), rows(3 * d), _full((d, d)),
                  _full((d, d)), _full((2 * d, d)), _full((d, d)), _full((1, d))] + ([rows(d)] if with_loss else []),
        out_specs=[rows(d)] + loss_spec + [rows(d)] * 5,
        out_shape=[jax.ShapeDtypeStruct((s, d), F32)] + loss_shape + [act] * 5,
        compiler_params=_cparams(("arbitrary" if with_loss else "parallel",)),
    )(x, ya, ys, ym, pg, wa, ws, wm, wo, g_post, *([target] if with_loss else []))


def merge_bwd(dx, out_s, pg, ba, bs, bm, wa, ws, wm, wo, g_post):
    s, d = dx.shape
    tm = min(256, s)

    def body(dx_ref, out_ref, pg_ref, ba_ref, bs_ref, bm_ref, wa_ref, ws_ref, wm_ref, wo_ref, g_ref,
             dout_ref, dba_ref, dbs_ref, dbm_ref, dpg_ref, dya_ref, dys_ref, dym_ref, dg_ref):
        @pl.when(pl.program_id(0) == 0)
        def _():
            dg_ref[...] = jnp.zeros_like(dg_ref)

        o = out_ref[...].astype(F32)
        dxv = dx_ref[...]
        r = lax.rsqrt(jnp.mean(o * o, axis=-1, keepdims=True) + EPS)
        nrm = o * r
        dg_ref[...] += jnp.sum(dxv * nrm, axis=0, keepdims=True)
        dn = dxv * g_ref[...]
        dout = r * (dn - nrm * jnp.mean(dn * nrm, axis=-1, keepdims=True))
        dout_ref[...] = dout.astype(dout_ref.dtype)
        dmerged = _dot_nt(dout, wo_ref[...])
        for q, (b_ref, db_ref, w_ref, dy_ref) in enumerate(((ba_ref, dba_ref, wa_ref, dya_ref),
                                                            (bs_ref, dbs_ref, ws_ref, dys_ref),
                                                            (bm_ref, dbm_ref, wm_ref, dym_ref))):
            gt = _sig(pg_ref[:, q * d:(q + 1) * d].astype(F32))
            db = dmerged * gt
            db_ref[...] = db.astype(db_ref.dtype)
            dpg_ref[:, q * d:(q + 1) * d] = (dmerged * b_ref[...].astype(F32) * gt * (1.0 - gt)).astype(dpg_ref.dtype)
            if q == 0:
                dy_ref[...] = _dot_nt(w_ref[...], db).astype(dy_ref.dtype)
            else:
                dy_ref[...] = _dot_nt(db, w_ref[...]).astype(dy_ref.dtype)

    rows = lambda w: pl.BlockSpec((tm, w), lambda i: (i, 0))
    act = lambda w: jax.ShapeDtypeStruct((s, w), ACT_DTYPE)
    return pl.pallas_call(
        body, name="merge_bwd", grid=(s // tm,),
        in_specs=[rows(d), rows(d), rows(3 * d), rows(d), rows(d), rows(d), _full((d, d)), _full((d, d)),
                  _full((2 * d, d)), _full((d, d)), _full((1, d))],
        out_specs=[rows(d), rows(d), rows(d), rows(d), rows(3 * d), pl.BlockSpec((d, tm), lambda i: (0, i)), rows(d),
                   rows(2 * d), _full((1, d))],
        out_shape=[act(d), act(d), act(d), act(d), act(3 * d), jax.ShapeDtypeStruct((d, s), ACT_DTYPE), act(d),
                   act(2 * d), jax.ShapeDtypeStruct((1, d), F32)],
        compiler_params=_cparams(("arbitrary",)),
    )(dx, out_s, pg, ba, bs, bm, wa, ws, wm, wo, g_post)


def _mesh_pos():
    x, y, c = lax.axis_index("x"), lax.axis_index("y"), lax.axis_index("c")
    return x, y, c, 4 * x + 2 * y + c


def _peer(x, y, c, k):
    px = 1 - x if k & 4 else x
    py = 1 - y if k & 2 else y
    pc = 1 - c if k & 1 else c
    return (px, py, pc), 4 * px + 2 * py + pc


class Exchange:
    SAME_CORE = (2, 4, 6)

    def __init__(self, scattered, gathered):
        self.ns = len(scattered)
        self.arrays = list(scattered) + list(gathered)
        self.na = len(self.arrays)
        any_spec = pl.BlockSpec(memory_space=pl.ANY)
        self.in_specs = [any_spec] * self.na
        self.out_specs = [any_spec] * self.na
        self.out_shape = ([jax.ShapeDtypeStruct(a.shape, a.dtype) for a in scattered]
                          + [jax.ShapeDtypeStruct((N_DEV,) + a.shape, a.dtype) for a in gathered])
        self.scratch = [pltpu.SemaphoreType.DMA((self.na, N_DEV - 1)), pltpu.SemaphoreType.DMA((self.na, N_DEV - 1)),
                        pltpu.SemaphoreType.DMA((self.na,))]

    def _src(self, ins, q, slot):
        return ins[q].at[slot] if q < self.ns else ins[q]

    def _local(self, ins, outs, sems):
        me = _mesh_pos()[3]
        return [pltpu.make_async_copy(self._src(ins, q, me), outs[q].at[me], sems[2].at[q]) for q in range(self.na)]

    def _direct(self, ins, outs, sems, relations, arrays):
        x, y, c, me = _mesh_pos()
        copies = []
        for k in relations:
            peer, pidx = _peer(x, y, c, k)
            for q in arrays:
                copies.append(pltpu.make_async_remote_copy(
                    src_ref=self._src(ins, q, pidx), dst_ref=outs[q].at[me], send_sem=sems[0].at[q, k - 1],
                    recv_sem=sems[1].at[q, k - 1], device_id=peer, device_id_type=MESH))
        return copies

    def _arrivals(self, ins, outs, sems, relations, arrays):
        x, y, c, _ = _mesh_pos()
        copies = []
        for k in relations:
            peer, pidx = _peer(x, y, c, k)
            for q in arrays:
                copies.append(pltpu.make_async_remote_copy(
                    src_ref=self._src(ins, q, pidx), dst_ref=outs[q].at[pidx], send_sem=sems[0].at[q, k - 1],
                    recv_sem=sems[1].at[q, k - 1], device_id=peer, device_id_type=MESH))
        return copies

    def _relays(self, outs, sems):
        x, y, c, _ = _mesh_pos()
        sibling, _ = _peer(x, y, c, 1)
        copies = []
        for k in self.SAME_CORE:
            _, pidx = _peer(x, y, c, k)
            for q in range(self.ns, self.na):
                copies.append(pltpu.make_async_remote_copy(
                    src_ref=outs[q].at[pidx], dst_ref=outs[q].at[pidx], send_sem=sems[0].at[q, k],
                    recv_sem=sems[1].at[q, k], device_id=sibling, device_id_type=MESH))
        return copies

    def _sends(self, ins, outs, sems):
        return (self._direct(ins, outs, sems, range(1, N_DEV), range(self.ns))
                + self._direct(ins, outs, sems, (1,) + self.SAME_CORE, range(self.ns, self.na)))

    def start(self, ins, outs, sems):
        for cp in self._local(ins, outs, sems) + self._sends(ins, outs, sems):
            cp.start()

    def relay(self, ins, outs, sems):
        for cp in self._arrivals(ins, outs, sems, self.SAME_CORE, range(self.ns, self.na)):
            cp.wait_recv()
        for cp in self._relays(outs, sems):
            cp.start()

    def wait(self, ins, outs, sems):
        for cp in (self._arrivals(ins, outs, sems, range(1, N_DEV), range(self.ns))
                   + self._arrivals(ins, outs, sems, (1, 3, 5, 7), range(self.ns, self.na))):
            cp.wait_recv()
        for cp in self._sends(ins, outs, sems) + self._relays(outs, sems):
            cp.wait_send()
        for cp in self._local(ins, outs, sems):
            cp.wait()


def adamw(parts_list, w, m, v, tile, name, ex=None):
    npart, _, dp = parts_list[0].shape
    d = w.shape[-1]
    counts = [p.shape[1] // tile for p in parts_list]
    starts = [sum(counts[:q]) for q in range(len(counts))]
    n_lists = len(parts_list)

    def body(*refs):
        p_refs = refs[:n_lists]
        w_ref, m_ref, v_ref, g_ref, dw_ref, nm_ref, nv_ref = refs[n_lists:]
        i = pl.program_id(0)
        for q, p_ref in enumerate(p_refs):
            @pl.when((i >= starts[q]) & (i < starts[q] + counts[q]))
            def _(p_ref=p_ref):
                acc = p_ref[0, :, 0:d].astype(F32)
                for k in range(1, npart):
                    acc = acc + p_ref[k, :, 0:d].astype(F32)
                g_ref[...] = acc

        g = g_ref[...]
        nm = ADAM_B1 * m_ref[...] + (1.0 - ADAM_B1) * g
        nv = ADAM_B2 * v_ref[...] + (1.0 - ADAM_B2) * (g * g)
        nm_ref[...] = nm
        nv_ref[...] = nv
        m_hat = nm / (1.0 - ADAM_B1 ** ADAM_STEP)
        v_hat = nv / (1.0 - ADAM_B2 ** ADAM_STEP)
        dw_ref[...] = -ADAM_LR * (m_hat / (jnp.sqrt(v_hat) + ADAM_EPS) + ADAM_WD * w_ref[...])

    def part_rows(q):
        return lambda i: (0, jnp.clip(i - starts[q], 0, counts[q] - 1), 0)

    if w.ndim == 3:
        rows = pl.BlockSpec((None, tile, d), lambda i: (i // counts[0], i % counts[0], 0))
    else:
        rows = pl.BlockSpec((tile, d), lambda i: (i, 0))
    own, hosted = _call_hosting(
        body, name, sum(counts),
        in_specs=[pl.BlockSpec((npart, tile, dp), part_rows(q)) for q in range(n_lists)] + [rows, rows, rows],
        out_specs=[rows] * 4, out_shape=[jax.ShapeDtypeStruct(w.shape, F32)] * 4, scratch=[],
        args=(*parts_list, w, m, v), ex=ex)
    return (*own, hosted) if ex is not None else tuple(own)


def _pad_rows(a, rows):
    return jnp.pad(a, ((0, rows - a.shape[0]), (0, 0)))


def _pack_rest(w_att, w_sg, w_ssm, w_out):
    parts = []
    for l in range(2):
        parts += [w_att[l], w_sg[l], w_ssm[l], w_out[l]]
    return jnp.concatenate(parts, axis=0)


def _unpack_rest(p):
    outs = [[], [], [], []]
    o = 0
    for l in range(2):
        for q, rws in enumerate(REST_PARTS):
            outs[q].append(p[o:o + rws])
            o += rws
    return [jnp.stack(t) for t in outs]


def _pack_win(w_in):
    return jnp.pad(w_in.reshape(2 * D_MODEL, WIN_SHARD), ((0, 0), (0, WIN_LANES - WIN_SHARD)))


W_IN_MAP = ((0, 1024, "att", 0), (1024, 1280, "att", 2048), (1280, 2304, "att", 1024), (2304, 5376, "sg", 0),
            (5376, 7424, "ssm", 3072), (7424, 10496, "ssm", 0), (10496, 10528, "ssm", 5120), (10528, 13600, "gate", 0))
SLAB_COLS = {"att": ATT_COLS, "sg": SG_COLS, "ssm": SSM_COLS, "gate": GATE_COLS}


def _slabs_from_shards(g):
    slabs = {}
    for name, width in SLAB_COLS.items():
        pieces, filled = [], 0
        for ga, gb, _, off in sorted((m for m in W_IN_MAP if m[2] == name), key=lambda m: m[3]):
            assert off == filled
            a = ga
            while a < gb:
                d = a // WIN_SHARD
                hi = min(gb, WIN_SHARD * (d + 1))
                pieces.append(g[d, :, a - WIN_SHARD * d:hi - WIN_SHARD * d])
                a = hi
            filled += gb - ga
        if filled < width:
            pieces.append(jnp.zeros((D_MODEL, width - filled), g.dtype))
        slabs[name] = jnp.concatenate(pieces, axis=1)
    return slabs


def _shards_from_slabs(dslabs):
    out = []
    for d in range(N_DEV):
        a, b = WIN_SHARD * d, WIN_SHARD * (d + 1)
        pieces = []
        for ga, gb, name, off in W_IN_MAP:
            lo, hi = max(a, ga), min(b, gb)
            if lo < hi:
                pieces.append(dslabs[name][:, off + lo - ga:off + hi - ga])
        pieces.append(jnp.zeros((D_MODEL, WIN_LANES - WIN_SHARD), pieces[0].dtype))
        out.append(jnp.concatenate(pieces, axis=1).astype(WIRE_DTYPE))
    return jnp.stack(out)


SMALL_SIZES = (("norm_pre", 2048), ("norm_post", 2048), ("rel_bias", 512), ("att_sinks", 32), ("sg_ln_g", 2048),
               ("sg_ln_b", 2048), ("sg_w", 262144), ("sg_b", 2048), ("ssm_conv_b", 6144), ("ssm_dt_bias", 64),
               ("ssm_a_log", 64), ("ssm_d", 64), ("ssm_norm_g", 4096), ("conv_w_full", 24576))


def _pack_small(d):
    parts = []
    for name, size in SMALL_SIZES:
        rows = 8 * (-(-size // (8 * D_MODEL)))
        flat = d[name].reshape(-1) if name in d else jnp.zeros((size,), F32)
        parts.append(jnp.pad(flat, (0, rows * D_MODEL - size)).reshape(rows, D_MODEL))
    return _pad_rows(jnp.concatenate(parts, axis=0), SMALL_ROWS)


def _unpack_small(p, shapes):
    out, o = {}, 0
    for name, size in SMALL_SIZES:
        rows = 8 * (-(-size // (8 * D_MODEL)))
        if name in shapes:
            out[name] = p[o:o + rows].reshape(-1)[:size].reshape(shapes[name])
        o += rows
    return out


def _bucket_onehot_t():
    qi = jnp.arange(BLK, dtype=jnp.int32)[None, :]
    kj = jnp.arange(BLK, dtype=jnp.int32)[:, None]
    dd = (qi - kj) & (BLK - 1)
    in_window = dd >= 0
    max_exact = REL_BUCKETS // 2
    dist_f = jnp.maximum(dd, 1).astype(F32)
    large = max_exact + (jnp.log(dist_f / max_exact) / math.log(128 / max_exact)
                         * (REL_BUCKETS - max_exact)).astype(jnp.int32)
    large = jnp.minimum(large, REL_BUCKETS - 1)
    bucket = jnp.where(dd < max_exact, dd, large).reshape(1, -1)
    onehot_t = (bucket == jnp.arange(REL_BUCKETS, dtype=jnp.int32)[:, None]).astype(F32)
    maskadd = jnp.where(in_window, 0.0, NEG).astype(F32).reshape(1, -1)
    return onehot_t, maskadd


WEIGHTS = ['w_in', 'norm_pre', 'norm_post', 'rel_bias', 'att_sinks', 'sg_ln_g', 'sg_ln_b', 'sg_w', 'sg_b',
           'ssm_conv_w', 'ssm_conv_b', 'ssm_dt_bias', 'ssm_a_log', 'ssm_d', 'ssm_norm_g',
           'w_br_att', 'w_br_sg', 'w_br_ssm', 'w_out']
REST = ('w_br_att', 'w_br_sg', 'w_br_ssm', 'w_out')


def kernel(x, w_in, norm_pre, norm_post, rel_bias, att_sinks, sg_ln_g, sg_ln_b, sg_w, sg_b, ssm_conv_w, ssm_conv_b, ssm_dt_bias, ssm_a_log, ssm_d, ssm_norm_g, w_br_att, w_br_sg, w_br_ssm, w_out, loss_target, m_w_in, m_norm_pre, m_norm_post, m_rel_bias, m_att_sinks, m_sg_ln_g, m_sg_ln_b, m_sg_w, m_sg_b, m_ssm_conv_w, m_ssm_conv_b, m_ssm_dt_bias, m_ssm_a_log, m_ssm_d, m_ssm_norm_g, m_w_br_att, m_w_br_sg, m_w_br_ssm, m_w_out, v_w_in, v_norm_pre, v_norm_post, v_rel_bias, v_att_sinks, v_sg_ln_g, v_sg_ln_b, v_sg_w, v_sg_b, v_ssm_conv_w, v_ssm_conv_b, v_ssm_dt_bias, v_ssm_a_log, v_ssm_d, v_ssm_norm_g, v_w_br_att, v_w_br_sg, v_w_br_ssm, v_w_out):
    w = dict(w_in=w_in, norm_pre=norm_pre, norm_post=norm_post, rel_bias=rel_bias, att_sinks=att_sinks,
             sg_ln_g=sg_ln_g, sg_ln_b=sg_ln_b, sg_w=sg_w, sg_b=sg_b, ssm_conv_w=ssm_conv_w, ssm_conv_b=ssm_conv_b,
             ssm_dt_bias=ssm_dt_bias, ssm_a_log=ssm_a_log, ssm_d=ssm_d, ssm_norm_g=ssm_norm_g,
             w_br_att=w_br_att, w_br_sg=w_br_sg, w_br_ssm=w_br_ssm, w_out=w_out)
    mom = dict(w_in=m_w_in, norm_pre=m_norm_pre, norm_post=m_norm_post, rel_bias=m_rel_bias, att_sinks=m_att_sinks,
               sg_ln_g=m_sg_ln_g, sg_ln_b=m_sg_ln_b, sg_w=m_sg_w, sg_b=m_sg_b, ssm_conv_w=m_ssm_conv_w,
               ssm_conv_b=m_ssm_conv_b, ssm_dt_bias=m_ssm_dt_bias, ssm_a_log=m_ssm_a_log, ssm_d=m_ssm_d,
               ssm_norm_g=m_ssm_norm_g, w_br_att=m_w_br_att, w_br_sg=m_w_br_sg, w_br_ssm=m_w_br_ssm, w_out=m_w_out)
    var = dict(w_in=v_w_in, norm_pre=v_norm_pre, norm_post=v_norm_post, rel_bias=v_rel_bias, att_sinks=v_att_sinks,
               sg_ln_g=v_sg_ln_g, sg_ln_b=v_sg_ln_b, sg_w=v_sg_w, sg_b=v_sg_b, ssm_conv_w=v_ssm_conv_w,
               ssm_conv_b=v_ssm_conv_b, ssm_dt_bias=v_ssm_dt_bias, ssm_a_log=v_ssm_a_log, ssm_d=v_ssm_d,
               ssm_norm_g=v_ssm_norm_g, w_br_att=v_w_br_att, w_br_sg=v_w_br_sg, w_br_ssm=v_w_br_ssm, w_out=v_w_out)
    xs0 = x[0]
    target = loss_target[0]
    my_dev = 4 * lax.axis_index("x") + 2 * lax.axis_index("y") + lax.axis_index("c")

    conv_shard = _pad_rows(ssm_conv_w.reshape(-1, D_MODEL), 8)
    win_shard = _pack_win(w_in).astype(WIRE_DTYPE)
    rest_shard = _pack_rest(*[w[n] for n in REST]).astype(WIRE_DTYPE)
    layer_shards = [[win_shard[l * D_MODEL:(l + 1) * D_MODEL], rest_shard[l * LAYER_REST:(l + 1) * LAYER_REST]]
                    for l in range(2)]
    h0, (g_win0, gathered_conv) = rmsnorm_fwd(xs0, norm_pre[0][None], Exchange([], [layer_shards[0][0], conv_shard]))
    conv_full = gathered_conv[:, 0:3].reshape(N_DEV, 2, 4, 384).transpose(1, 2, 0, 3).reshape(2, 4, CONV_DIM)

    def set_rest(lw, g_rest):
        o = 0
        for name, rws in zip(("att", "sg", "ssm", "out"), REST_PARTS):
            lw[name] = g_rest[:, o:o + rws].reshape(N_DEV * rws, D_MODEL).astype(MXU_DTYPE)
            o += rws

    def layer_weights(l, g_win):
        slabs = _slabs_from_shards(g_win)
        lw = {"in_" + name: slab.astype(MXU_DTYPE) for name, slab in slabs.items()}
        lw["in_att"] = lw["in_att"].T
        tril = jnp.tril(jnp.ones((BLK, BLK), bool))
        sgw = jnp.where(tril[None], sg_w[l], 0.0)
        lw.update(
            g_pre=norm_pre[l][None], g_post=norm_post[l][None], sinks=jnp.repeat(att_sinks[l], BLK).reshape(2, GROUP_LANES),
            ln_g=sg_ln_g[l][None], ln_b=sg_ln_b[l][None], sgw=sgw.astype(MXU_DTYPE),
            sgw_t=sgw.transpose(0, 2, 1).astype(MXU_DTYPE), sgb_t=_pad_lanes(sg_b[l].T),
            cw=conv_full[l], cb=ssm_conv_b[l][None], dtb=_pad_lanes(ssm_dt_bias[l][None]),
            alog=_pad_lanes(ssm_a_log[l][None]), dsk=_pad_lanes(ssm_d[l][None]), ng=ssm_norm_g[l][None])
        return lw

    onehot_t, maskadd = _bucket_onehot_t()
    bias = bias_table(rel_bias.T, onehot_t, maskadd).reshape(2, GROUP_HEADS, BLK, BLK).transpose(0, 2, 1, 3)
    bias = bias.reshape(2, BLK, GROUP_LANES)

    saved = []
    xl = xs0
    layers = [layer_weights(0, g_win0)]
    for l in range(2):
        lw = layers[l]
        h = h0 if l == 0 else rmsnorm_fwd(xl, lw["g_pre"])
        pa = mm_nt(lw["in_att"], h, 1152, "proj_att")
        ps = mm_nn(h, lw["in_sg"], 1536, "proj_sg")
        pm = mm_nn(h, lw["in_ssm"], 1792, "proj_ssm")
        pg = mm_nn(h, lw["in_gate"], 1536, "proj_gate")
        if l == 0:
            ya, (g_rest0,) = attn_fwd(pa, bias, lw["sinks"], Exchange([], [layer_shards[0][1]]))
            set_rest(lw, g_rest0)
        else:
            ya = attn_fwd(pa, bias, lw["sinks"])
        sgu_args = (ps, lw["ln_g"], lw["ln_b"], lw["sgw"], lw["sgb_t"])
        ssd_args = (pm, lw["cw"], lw["cb"], lw["dtb"], lw["alog"], lw["dsk"], lw["ng"])
        ys = sgu_fwd(*sgu_args)
        if l == 0:
            ym, states, conv_pre, (g_win1, g_rest1) = ssd_fwd(*ssd_args, Exchange([], layer_shards[1]))
            layers.append(layer_weights(1, g_win1))
            set_rest(layers[1], g_rest1)
        else:
            ym, states, conv_pre = ssd_fwd(*ssd_args)
        merge_args = (xl, ya, ys, ym, pg, lw["att"], lw["sg"], lw["ssm"], lw["out"], lw["g_post"])
        if l == 0:
            x_next, ba, bs, bm, merged, out_s = merge_fwd(*merge_args)
        else:
            dx, loss_part, ba, bs, bm, merged, out_s = merge_fwd(*merge_args, target)
        saved.append(dict(x=xl, h=h, pa=pa, ps=ps, pm=pm, pg=pg, ya=ya, ys=ys, ym=ym, states=states,
                          conv_pre=conv_pre, ba=ba, bs=bs, bm=bm, merged=merged, out_s=out_s))
        xl = x_next

    loss = lax.psum(loss_part[0, 0], ("x", "y", "c"))

    dbias = jnp.zeros((2, BLK, GROUP_LANES), F32)
    win_grads, rest_grads = [None, None], [None, None]
    small = {n: [None, None] for n in ("norm_pre", "norm_post", "att_sinks", "sg_ln_g", "sg_ln_b", "sg_w", "sg_b",
                                       "ssm_conv_b", "ssm_dt_bias", "ssm_a_log", "ssm_d", "ssm_norm_g",
                                       "conv_w_full")}
    for l in (1, 0):
        lw, sv = layers[l], saved[l]
        dout, dba, dbs, dbm, dpg, dya, dys, dym, dg_post = merge_bwd(
            dx, sv["out_s"], sv["pg"], sv["ba"], sv["bs"], sv["bm"], lw["att"], lw["sg"], lw["ssm"], lw["out"],
            lw["g_post"])
        dw_out = mm_tn(sv["merged"], dout, 1024, "dw_out")
        dw_att = mm_kn(sv["ya"], dba, 1024, "dw_br_att")
        dw_sg = mm_tn(sv["ys"], dbs, 1024, "dw_br_sg")
        dw_ssm = mm_tn(sv["ym"], dbm, 1024, "dw_br_ssm")
        rest_grads[l] = jnp.concatenate(
            [dw_att.reshape(N_DEV, 128, D_MODEL), dw_sg.reshape(N_DEV, 128, D_MODEL),
             dw_ssm.reshape(N_DEV, 256, D_MODEL), dw_out.reshape(N_DEV, 128, D_MODEL)], axis=1).astype(WIRE_DTYPE)
        dpa, dbias, dsinks = attn_bwd(sv["pa"], dya, bias, lw["sinks"], dbias)
        dps, dsgw, dsgb_t, dln_g, dln_b = sgu_bwd(sv["ps"], dys, lw["ln_g"], lw["ln_b"], lw["sgw"], lw["sgw_t"],
                                                  lw["sgb_t"])
        ssd_args = (sv["pm"], sv["conv_pre"], dym, sv["states"], lw["cw"], lw["dtb"], lw["alog"], lw["dsk"], lw["ng"])
        if l == 0:
            dpm, dcw, dcb, dvec, dng, (recv_win1, recv_rest0) = ssd_bwd(
                *ssd_args, Exchange([win_grads[1], rest_grads[0]], []))
        else:
            dpm, dcw, dcb, dvec, dng, (recv_rest1,) = ssd_bwd(*ssd_args, Exchange([rest_grads[1]], []))
        dslabs = dict(att=mm_kn(dpa, sv["h"], 1152, "dw_in_att").T, sg=mm_tn(sv["h"], dps, 3072, "dw_in_sg"),
                      ssm=mm_tn(sv["h"], dpm, 2688, "dw_in_ssm"), gate=mm_tn(sv["h"], dpg, 3072, "dw_in_gate"))
        win_grads[l] = _shards_from_slabs(dslabs)
        dh_args = ([dpa, dps, dpm, dpg], [lw["in_att"], lw["in_sg"], lw["in_ssm"], lw["in_gate"]], sv["x"],
                   lw["g_pre"], dx)
        if l == 0:
            dx, dg_pre, (recv_win0,) = dh_norm_bwd(*dh_args, Exchange([win_grads[0]], []))
        else:
            dx, dg_pre = dh_norm_bwd(*dh_args)
        small["norm_pre"][l] = dg_pre[0]
        small["norm_post"][l] = dg_post[0]
        small["att_sinks"][l] = dsinks[0, :ATT_HEADS]
        small["sg_ln_g"][l] = dln_g[0]
        small["sg_ln_b"][l] = dln_b[0]
        small["sg_w"][l] = dsgw
        small["sg_b"][l] = dsgb_t[:, :SG_GROUPS].T
        small["ssm_conv_b"][l] = dcb[0]
        small["ssm_dt_bias"][l] = dvec[0, :SSM_HEADS]
        small["ssm_a_log"][l] = dvec[1, :SSM_HEADS]
        small["ssm_d"][l] = dvec[2, :SSM_HEADS]
        small["ssm_norm_g"][l] = dng[0]
        small["conv_w_full"][l] = dcw[0:4]
    grad_x = dx
    dbias = dbias.reshape(2, BLK, GROUP_HEADS, BLK).transpose(0, 2, 1, 3).reshape(ATT_HEADS, BLK * BLK)
    d_rel_bias = bias_table_bwd(dbias, onehot_t).T

    small_d = {n: jnp.stack(v) for n, v in small.items()}
    small_d["rel_bias"] = d_rel_bias
    *res_win, (recv_small,) = adamw([recv_win0, recv_win1], w_in, m_w_in, v_w_in, WIN_TILE, "adamw_w_in",
                                    Exchange([], [_pack_small(small_d)]))
    res_rest = adamw([recv_rest0, recv_rest1], _pack_rest(*[w[n] for n in REST]), _pack_rest(*[mom[n] for n in REST]),
                     _pack_rest(*[var[n] for n in REST]), REST_TILE, "adamw_rest")
    small_names = [n for n, _ in SMALL_SIZES if n != "conv_w_full"]
    g_s, dw_s, nm_s, nv_s = adamw([recv_small], _pack_small({n: w[n] for n in small_names}),
                                  _pack_small({n: mom[n] for n in small_names}),
                                  _pack_small({n: var[n] for n in small_names}), SMALL_TILE, "adamw_small")
    shapes = {n: w[n].shape for n in small_names}
    shapes["conv_w_full"] = (2, 4, CONV_DIM)
    g_conv_full = _unpack_small(g_s, shapes)["conv_w_full"]
    g_conv = lax.dynamic_slice_in_dim(g_conv_full, my_dev * 384, 384, axis=2)
    pack_conv = lambda a: _pad_rows(a.reshape(-1, D_MODEL), 8)
    g_c, dw_c, nm_c, nv_c = adamw([pack_conv(g_conv)[None]], pack_conv(ssm_conv_w), pack_conv(m_ssm_conv_w),
                                  pack_conv(v_ssm_conv_w), 8, "adamw_conv")

    results = {}
    for q, (tag, psm, pc) in enumerate((("grad", g_s, g_c), ("delta", dw_s, dw_c), ("new_m", nm_s, nm_c),
                                        ("new_v", nv_s, nv_c))):
        r = dict(zip(REST, _unpack_rest(res_rest[q])))
        r["w_in"] = res_win[q]
        r.update(_unpack_small(psm, {n: w[n].shape for n in small_names}))
        r["ssm_conv_w"] = pc[0:3].reshape(2, 4, 384)
        results[tag] = r
    outs = [loss, grad_x[None]]
    for tag in ("grad", "delta", "new_m", "new_v"):
        outs += [results[tag][n] for n in WEIGHTS]
    return tuple(outs)
```

```python
import math

import jax
import jax.numpy as jnp
from jax import lax
from jax.experimental import pallas as pl
from jax.experimental.pallas import tpu as pltpu

F32 = jnp.float32
MXU_DTYPE = jnp.bfloat16
ACT_DTYPE = jnp.bfloat16
WIRE_DTYPE = jnp.bfloat16
HI = lax.Precision.HIGHEST
MESH = pl.DeviceIdType.MESH

D_MODEL = 1024
N_DEV = 8
ATT_HEADS = 16
HEAD_DIM = 64
BLK = 128
SG_GROUPS = 8
SSM_WIDTH = 2048
SSM_HEADS = 32
SSM_GROUPS = 4
SSM_GW = SSM_WIDTH // SSM_GROUPS
CONV_DIM = 3072
REL_BUCKETS = 32
EPS = 1e-6
NEG = -1e30

ATT_COLS = 2304
SG_COLS = 3072
SSM_COLS = 5376
GATE_COLS = 3072
DT_OFF = 5120

VMEM_LIMIT_V7X = 56 * 2 ** 20
DH_TK = 768

ADAM_LR, ADAM_B1, ADAM_B2, ADAM_EPS, ADAM_WD, ADAM_STEP = 0.001, 0.9, 0.999, 1e-08, 0.01, 10

WIN_SHARD = 1700
WIN_LANES = 1792
REST_PARTS = (128, 128, 256, 128)
LAYER_REST = sum(REST_PARTS)
REST_TILE = 128
WIN_TILE = 128
SMALL_ROWS = 384
SMALL_TILE = 128


def _cparams(sem=None):
    return pltpu.CompilerParams(dimension_semantics=sem, vmem_limit_bytes=VMEM_LIMIT_V7X)


def _dot(a, b):
    return jnp.dot(a.astype(MXU_DTYPE), b.astype(MXU_DTYPE), preferred_element_type=F32)


def _dot_nt(a, b):
    return lax.dot_general(a.astype(MXU_DTYPE), b.astype(MXU_DTYPE), (((1,), (1,)), ((), ())),
                           preferred_element_type=F32)


def _dot_tn(a, b):
    return lax.dot_general(a.astype(MXU_DTYPE), b.astype(MXU_DTYPE), (((0,), (0,)), ((), ())),
                           preferred_element_type=F32)


def _dot_hi(a, b):
    return jnp.dot(a, b, precision=HI, preferred_element_type=F32)


def _dot_onehot(a, onehot):
    hi = a.astype(jnp.bfloat16)
    lo = (a - hi.astype(F32)).astype(jnp.bfloat16)
    return (jnp.dot(hi, onehot, preferred_element_type=F32) + jnp.dot(lo, onehot, preferred_element_type=F32))


def _dot_hi_nt(a, b):
    return lax.dot_general(a, b, (((1,), (1,)), ((), ())), precision=HI, preferred_element_type=F32)


def _sig(x):
    return 1.0 / (1.0 + jnp.exp(-x))


def _dsilu(x, s):
    return s * (1.0 + x * (1.0 - s))


def _full(shape):
    nd = len(shape)
    return pl.BlockSpec(shape, lambda *_: (0,) * nd)


def rmsnorm_fwd(x, g, ex=None):
    s, d = x.shape
    tm = min(512, s)

    def body(x_ref, g_ref, o_ref):
        xv = x_ref[...]
        r = lax.rsqrt(jnp.mean(xv * xv, axis=-1, keepdims=True) + EPS)
        o_ref[...] = (xv * r * g_ref[...]).astype(o_ref.dtype)

    (h,), hosted = _call_hosting(
        body, "rmsnorm_fwd", s // tm,
        in_specs=[pl.BlockSpec((tm, d), lambda i: (i, 0)), _full((1, d))],
        out_specs=[pl.BlockSpec((tm, d), lambda i: (i, 0))],
        out_shape=[jax.ShapeDtypeStruct((s, d), ACT_DTYPE)], scratch=[], args=(x, g), ex=ex)
    return (h, hosted) if ex is not None else h


def mm_nn(a, b, tn, name):
    s, k = a.shape
    n = b.shape[1]
    tm = min(2048, s)

    def body(a_ref, b_ref, o_ref):
        o_ref[...] = _dot(a_ref[...], b_ref[...]).astype(o_ref.dtype)

    return pl.pallas_call(
        body, name=name, grid=(s // tm, n // tn),
        in_specs=[pl.BlockSpec((tm, k), lambda i, j: (i, 0)), pl.BlockSpec((k, tn), lambda i, j: (0, j))],
        out_specs=pl.BlockSpec((tm, tn), lambda i, j: (i, j)),
        out_shape=jax.ShapeDtypeStruct((s, n), ACT_DTYPE),
        compiler_params=_cparams(("parallel", "arbitrary")),
    )(a, b)


def mm_nt(a, b, tm, name):
    m, k = a.shape
    s = b.shape[0]
    ts = min(2048, s)

    def body(a_ref, b_ref, o_ref):
        o_ref[...] = _dot_nt(a_ref[...], b_ref[...]).astype(o_ref.dtype)

    return pl.pallas_call(
        body, name=name, grid=(s // ts, m // tm),
        in_specs=[pl.BlockSpec((tm, k), lambda i, j: (j, 0)), pl.BlockSpec((ts, k), lambda i, j: (i, 0))],
        out_specs=pl.BlockSpec((tm, ts), lambda i, j: (j, i)),
        out_shape=jax.ShapeDtypeStruct((m, s), ACT_DTYPE),
        compiler_params=_cparams(("parallel", "arbitrary")),
    )(a, b)


def mm_kn(a, b, tm, name):
    m, s = a.shape
    n = b.shape[1]
    ts = min(512, s)
    nt = s // ts

    def body(a_ref, b_ref, o_ref, acc_ref):
        @pl.when(pl.program_id(1) == 0)
        def _():
            acc_ref[...] = jnp.zeros_like(acc_ref)

        acc_ref[...] += _dot(a_ref[...], b_ref[...])

        @pl.when(pl.program_id(1) == nt - 1)
        def _():
            o_ref[...] = acc_ref[...].astype(o_ref.dtype)

    return pl.pallas_call(
        body, name=name, grid=(m // tm, nt),
        in_specs=[pl.BlockSpec((tm, ts), lambda j, t: (j, t)), pl.BlockSpec((ts, n), lambda j, t: (t, 0))],
        out_specs=pl.BlockSpec((tm, n), lambda j, t: (j, 0)),
        out_shape=jax.ShapeDtypeStruct((m, n), WIRE_DTYPE),
        scratch_shapes=[pltpu.VMEM((tm, n), F32)],
        compiler_params=_cparams(("parallel", "arbitrary")),
    )(a, b)


def mm_tn(a, b, tn, name):
    s, k = a.shape
    n = b.shape[1]
    ts = min(512, s)
    nt = s // ts

    def body(a_ref, b_ref, o_ref, acc_ref):
        @pl.when(pl.program_id(1) == 0)
        def _():
            acc_ref[...] = jnp.zeros_like(acc_ref)

        acc_ref[...] += _dot_tn(a_ref[...], b_ref[...])

        @pl.when(pl.program_id(1) == nt - 1)
        def _():
            o_ref[...] = acc_ref[...].astype(o_ref.dtype)

    return pl.pallas_call(
        body, name=name, grid=(n // tn, nt),
        in_specs=[pl.BlockSpec((ts, k), lambda j, t: (t, 0)), pl.BlockSpec((ts, tn), lambda j, t: (t, j))],
        out_specs=pl.BlockSpec((k, tn), lambda j, t: (0, j)),
        out_shape=jax.ShapeDtypeStruct((k, n), WIRE_DTYPE),
        scratch_shapes=[pltpu.VMEM((k, tn), F32)],
        compiler_params=_cparams(("parallel", "arbitrary")),
    )(a, b)


def dh_norm_bwd(dslabs, wslabs, x, g, dres, ex=None):
    s, d = x.shape
    tm = min(1024, s)
    widths = [ds.shape[0 if q == 0 else 1] for q, ds in enumerate(dslabs)]
    tks = [DH_TK] * len(widths)
    counts = [wd // t for wd, t in zip(widths, tks)]
    starts = [sum(counts[:i]) for i in range(len(counts))]
    nk = sum(counts)
    ns = len(dslabs)

    hosted = ex is not None
    ni = s // tm

    def mm_body(*refs):
        (own_in, (dh_ref,), _), hosted_refs = _split_hosted(refs, 2 * ns, 1, 0, ex)
        d_refs, w_refs = own_in[:ns], own_in[ns:]
        i, k = pl.program_id(0), pl.program_id(1)
        if hosted:
            @pl.when((i == 0) & (k == 0))
            def _():
                ex.start(*hosted_refs)

            @pl.when((i == ni - 1) & (k == nk - 1))
            def _():
                ex.relay(*hosted_refs)
                ex.wait(*hosted_refs)

        @pl.when(k == 0)
        def _():
            dh_ref[...] = jnp.zeros_like(dh_ref)

        for q in range(ns):
            @pl.when((k >= starts[q]) & (k < starts[q] + counts[q]))
            def _(q=q):
                if q == 0:
                    dh_ref[...] += _dot_tn(d_refs[q][...], w_refs[q][...])
                else:
                    dh_ref[...] += _dot_nt(d_refs[q][...], w_refs[q][...])

    def clamp(q):
        if q == 0:
            return pl.BlockSpec((tks[q], tm), lambda i, k: (jnp.clip(k - starts[q], 0, counts[q] - 1), i))
        return pl.BlockSpec((tm, tks[q]), lambda i, k: (i, jnp.clip(k - starts[q], 0, counts[q] - 1)))

    def clamp_w(q):
        if q == 0:
            return pl.BlockSpec((tks[q], d), lambda i, k: (jnp.clip(k - starts[q], 0, counts[q] - 1), 0))
        return pl.BlockSpec((d, tks[q]), lambda i, k: (0, jnp.clip(k - starts[q], 0, counts[q] - 1)))

    res = pl.pallas_call(
        mm_body, name="dh_matmul_scatter" if hosted else "dh_matmul", grid=(ni, nk),
        in_specs=([clamp(q) for q in range(ns)] + [clamp_w(q) for q in range(ns)]
                  + (ex.in_specs if hosted else [])),
        out_specs=[pl.BlockSpec((tm, d), lambda i, k: (i, 0))] + (ex.out_specs if hosted else []),
        out_shape=[jax.ShapeDtypeStruct((s, d), F32)] + (ex.out_shape if hosted else []),
        scratch_shapes=ex.scratch if hosted else [],
        compiler_params=_cparams(("arbitrary" if hosted else "parallel", "arbitrary")),
    )(*dslabs, *wslabs, *(ex.arrays if hosted else []))
    dh, ex_results = res[0], res[1:]

    te = min(512, s)

    def norm_body(dh_ref, x_ref, g_ref, dres_ref, dx_ref, dg_ref):
        @pl.when(pl.program_id(0) == 0)
        def _():
            dg_ref[...] = jnp.zeros_like(dg_ref)

        xv = x_ref[...]
        r = lax.rsqrt(jnp.mean(xv * xv, axis=-1, keepdims=True) + EPS)
        xn = xv * r
        dhv = dh_ref[...]
        dg_ref[...] += jnp.sum(dhv * xn, axis=0, keepdims=True)
        dxn = dhv * g_ref[...]
        dx_ref[...] = dres_ref[...] + r * (dxn - xn * jnp.mean(dxn * xn, axis=-1, keepdims=True))

    rows = pl.BlockSpec((te, d), lambda i: (i, 0))
    dx, dg = pl.pallas_call(
        norm_body, name="norm_bwd", grid=(s // te,),
        in_specs=[rows, rows, _full((1, d)), rows],
        out_specs=[rows, _full((1, d))],
        out_shape=[jax.ShapeDtypeStruct((s, d), F32), jax.ShapeDtypeStruct((1, d), F32)],
        compiler_params=_cparams(("arbitrary",)),
    )(dh, x, g, dres)
    return (dx, dg, ex_results) if hosted else (dx, dg)


def bias_table(rel_bias_t, onehot_t, maskadd):
    n = onehot_t.shape[1]
    tn = 8192

    def body(r_ref, o_ref, m_ref, out_ref):
        out_ref[...] = _dot_hi(r_ref[...], o_ref[...]) + m_ref[...]

    return pl.pallas_call(
        body, name="bias_table", grid=(n // tn,),
        in_specs=[_full((ATT_HEADS, REL_BUCKETS)), pl.BlockSpec((REL_BUCKETS, tn), lambda j: (0, j)),
                  pl.BlockSpec((1, tn), lambda j: (0, j))],
        out_specs=pl.BlockSpec((ATT_HEADS, tn), lambda j: (0, j)),
        out_shape=jax.ShapeDtypeStruct((ATT_HEADS, n), F32),
        compiler_params=_cparams(("parallel",)),
    )(rel_bias_t, onehot_t, maskadd)


def bias_table_bwd(dbias, onehot_t):
    n = onehot_t.shape[1]
    tn = 8192

    def body(d_ref, o_ref, out_ref):
        @pl.when(pl.program_id(0) == 0)
        def _():
            out_ref[...] = jnp.zeros_like(out_ref)

        out_ref[...] += _dot_hi_nt(d_ref[...], o_ref[...])

    return pl.pallas_call(
        body, name="bias_table_bwd", grid=(n // tn,),
        in_specs=[pl.BlockSpec((ATT_HEADS, tn), lambda j: (0, j)), pl.BlockSpec((REL_BUCKETS, tn), lambda j: (0, j))],
        out_specs=_full((ATT_HEADS, REL_BUCKETS)),
        out_shape=jax.ShapeDtypeStruct((ATT_HEADS, REL_BUCKETS), F32),
        compiler_params=_cparams(("arbitrary",)),
    )(dbias, onehot_t)


def _fold(full, tri):
    return jnp.where(tri, full[BLK:2 * BLK], full[0:BLK])


def _unfold(folded, tri):
    return jnp.concatenate([jnp.where(tri, 0.0, folded), jnp.where(tri, folded, 0.0)], axis=0)


GROUP_HEADS = ATT_HEADS // 2
GROUP_LANES = GROUP_HEADS * BLK


def _att_group(qg, kcat, vt_cat, bias_g, sink_g, tri, no_prev):
    l = _fold(_dot(kcat, qg), tri) * (HEAD_DIM ** -0.5) + bias_g
    l = jnp.where(no_prev, NEG, l)
    m = jnp.maximum(jnp.max(l, axis=0, keepdims=True), sink_g)
    p = jnp.exp(l - m)
    es = jnp.exp(sink_g - m)
    inv = 1.0 / (jnp.sum(p, axis=0, keepdims=True) + es)
    p = p * inv
    pcat = _unfold(p, tri)
    return p, pcat, es * inv, _dot(vt_cat, pcat)


ATT_SUB = 4


def _heads_to_lanes(ref, row0, ln):
    return jnp.concatenate([ref[row0 + j * HEAD_DIM:row0 + (j + 1) * HEAD_DIM, ln] for j in range(GROUP_HEADS)], axis=1)


def _lanes_to_heads(ref, row0, ln, val):
    for j in range(GROUP_HEADS):
        ref[row0 + j * HEAD_DIM:row0 + (j + 1) * HEAD_DIM, ln] = val[:, j * BLK:(j + 1) * BLK].astype(ref.dtype)


def _kv_cat(kvp, kvc, g):
    lo = g * HEAD_DIM
    kt_cat = jnp.concatenate([kvp[lo:lo + HEAD_DIM], kvc[lo:lo + HEAD_DIM]], axis=1)
    vt_cat = jnp.concatenate([kvp[128 + lo:128 + lo + HEAD_DIM], kvc[128 + lo:128 + lo + HEAD_DIM]], axis=1)
    return kt_cat, vt_cat


def _tri_masks(n):
    row = lax.broadcasted_iota(jnp.int32, (BLK, GROUP_LANES), 0)
    query = lax.broadcasted_iota(jnp.int32, (BLK, GROUP_LANES), 1) & (BLK - 1)
    tri = row <= query
    return tri, (n == 0) & jnp.logical_not(tri)


def _split_hosted(refs, n_in, n_out, n_scratch, ex):
    na = ex.na if ex is not None else 0
    o = 0
    parts = []
    for cnt in (n_in, na, n_out, na, n_scratch, 3 if ex is not None else 0):
        parts.append(refs[o:o + cnt])
        o += cnt
    own_in, ex_in, own_out, ex_out, own_scr, ex_sems = parts
    return (own_in, own_out, own_scr), (ex_in, ex_out, ex_sems)


def _call_hosting(body, name, nsteps, in_specs, out_specs, out_shape, scratch, args, ex):
    n_in, n_out, n_scr = len(in_specs), len(out_specs), len(scratch)
    hosted = ex is not None

    def full_body(*refs):
        (own_in, own_out, own_scr), hosted_refs = _split_hosted(refs, n_in, n_out, n_scr, ex)
        if hosted:
            @pl.when(pl.program_id(0) == 0)
            def _():
                ex.start(*hosted_refs)

            @pl.when(pl.program_id(0) == max(nsteps - max(nsteps // 8, 4), 0))
            def _():
                ex.relay(*hosted_refs)

            @pl.when(pl.program_id(0) == nsteps - 1)
            def _():
                ex.wait(*hosted_refs)

        body(*own_in, *own_out, *own_scr)

    res = pl.pallas_call(
        full_body, name=name + "_hosting" if hosted else name, grid=(nsteps,),
        in_specs=list(in_specs) + (ex.in_specs if hosted else []),
        out_specs=list(out_specs) + (ex.out_specs if hosted else []),
        out_shape=list(out_shape) + (ex.out_shape if hosted else []),
        scratch_shapes=list(scratch) + (ex.scratch if hosted else []),
        compiler_params=_cparams(("arbitrary",)),
    )(*args, *(ex.arrays if hosted else []))
    return res[:n_out], res[n_out:]


def attn_fwd(pa, bias, sinks, ex=None):
    s = pa.shape[1]
    nsteps = s // (ATT_SUB * BLK)

    def body(pa_ref, kvp_ref, bias_ref, sink_ref, y_ref):
        for sub in range(ATT_SUB):
            n = pl.program_id(0) * ATT_SUB + sub
            ln = slice(sub * BLK, (sub + 1) * BLK)
            kvc = pa_ref[2048:2304, ln]
            kvp = kvp_ref[...] if sub == 0 else pa_ref[2048:2304, (sub - 1) * BLK:sub * BLK]
            tri, no_prev = _tri_masks(n)
            for g in range(2):
                kt_cat, vt_cat = _kv_cat(kvp, kvc, g)
                row0 = g * GROUP_HEADS * HEAD_DIM
                _, _, _, o = _att_group(_heads_to_lanes(pa_ref, row0, ln), kt_cat.astype(F32).T, vt_cat, bias_ref[g],
                                        sink_ref[g:g + 1, :], tri, no_prev)
                z = _heads_to_lanes(pa_ref, 1024 + row0, ln).astype(F32)
                _lanes_to_heads(y_ref, row0, ln, o * z * _sig(z))

    (y,), hosted = _call_hosting(
        body, "attn_fwd", nsteps,
        in_specs=[pl.BlockSpec((ATT_COLS, ATT_SUB * BLK), lambda n: (0, n)),
                  pl.BlockSpec((256, BLK), lambda n: (8, jnp.maximum(ATT_SUB * n - 1, 0))),
                  _full((2, BLK, GROUP_LANES)), _full((2, GROUP_LANES))],
        out_specs=[pl.BlockSpec((1024, ATT_SUB * BLK), lambda n: (0, n))],
        out_shape=[jax.ShapeDtypeStruct((1024, s), ACT_DTYPE)], scratch=[],
        args=(pa, pa, bias, sinks), ex=ex)
    return (y, hosted) if ex is not None else y


def attn_bwd(pa, dy, bias, sinks, dbias_in):
    s = pa.shape[1]
    nsteps = s // (ATT_SUB * BLK)

    def body(pa_ref, kvp_ref, dy_ref, bias_ref, sink_ref, dbin_ref, dpa_ref, dbias_ref, dsink_ref, carry, dsink_acc):
        i = pl.program_id(0)

        @pl.when(i == 0)
        def _():
            dbias_ref[...] = dbin_ref[...]
            dsink_acc[...] = jnp.zeros_like(dsink_acc)
            carry[...] = jnp.zeros_like(carry)

        scale = HEAD_DIM ** -0.5
        for sub in reversed(range(ATT_SUB)):
            n = (nsteps - 1 - i) * ATT_SUB + sub
            ln = slice(sub * BLK, (sub + 1) * BLK)
            kvc = pa_ref[2048:2304, ln]
            kvp = kvp_ref[...] if sub == 0 else pa_ref[2048:2304, (sub - 1) * BLK:sub * BLK]
            tri, no_prev = _tri_masks(n)
            for g in range(2):
                kt_cat, vt_cat = _kv_cat(kvp, kvc, g)
                row0 = g * GROUP_HEADS * HEAD_DIM
                qg = _heads_to_lanes(pa_ref, row0, ln)
                p, pcat, psink, o = _att_group(qg, kt_cat.astype(F32).T, vt_cat, bias_ref[g], sink_ref[g:g + 1, :],
                                               tri, no_prev)
                z = _heads_to_lanes(pa_ref, 1024 + row0, ln).astype(F32)
                dyg = _heads_to_lanes(dy_ref, row0, ln).astype(F32)
                sz = _sig(z)
                d_o = dyg * z * sz
                _lanes_to_heads(dpa_ref, 1024 + row0, ln, dyg * _dsilu(z, sz) * o)
                delta = jnp.sum(d_o * o, axis=0, keepdims=True)
                dl = p * (_fold(_dot(vt_cat.astype(F32).T, d_o), tri) - delta)
                dsink_acc[g:g + 1, :] += psink * delta
                dbias_ref[g] += dl
                dlcat = _unfold(dl, tri)
                _lanes_to_heads(dpa_ref, row0, ln, _dot(kt_cat, dlcat) * scale)
                for q, dkv in enumerate((_dot_nt(qg, dlcat) * scale, _dot_nt(d_o, pcat))):
                    r0 = q * 128 + g * HEAD_DIM
                    dpa_ref[2048 + r0:2048 + r0 + HEAD_DIM, ln] = (
                        dkv[:, BLK:2 * BLK] + carry[r0:r0 + HEAD_DIM, :]).astype(dpa_ref.dtype)
                    carry[r0:r0 + HEAD_DIM, :] = dkv[:, 0:BLK]

        @pl.when(i == nsteps - 1)
        def _():
            lane = lax.broadcasted_iota(jnp.int32, (1, 128), 1)
            dsink = jnp.zeros((1, 128), F32)
            for h in range(ATT_HEADS):
                g, j = divmod(h, GROUP_HEADS)
                tot = jnp.sum(dsink_acc[g:g + 1, j * BLK:(j + 1) * BLK], axis=1, keepdims=True)
                dsink = dsink + jnp.where(lane == h, -tot, 0.0)
            dsink_ref[...] = dsink

    return pl.pallas_call(
        body, name="attn_bwd", grid=(nsteps,),
        in_specs=[pl.BlockSpec((ATT_COLS, ATT_SUB * BLK), lambda i: (0, nsteps - 1 - i)),
                  pl.BlockSpec((256, BLK), lambda i: (8, jnp.maximum(ATT_SUB * (nsteps - 1 - i) - 1, 0))),
                  pl.BlockSpec((1024, ATT_SUB * BLK), lambda i: (0, nsteps - 1 - i)),
                  _full((2, BLK, GROUP_LANES)), _full((2, GROUP_LANES)), _full((2, BLK, GROUP_LANES))],
        out_specs=[pl.BlockSpec((ATT_COLS, ATT_SUB * BLK), lambda i: (0, nsteps - 1 - i)),
                   _full((2, BLK, GROUP_LANES)), _full((1, 128))],
        out_shape=[jax.ShapeDtypeStruct((ATT_COLS, s), ACT_DTYPE),
                   jax.ShapeDtypeStruct((2, BLK, GROUP_LANES), F32),
                   jax.ShapeDtypeStruct((1, 128), F32)],
        scratch_shapes=[pltpu.VMEM((256, BLK), F32), pltpu.VMEM((2, GROUP_LANES), F32)],
        compiler_params=_cparams(("arbitrary",)),
    )(pa, pa, dy, bias, sinks, dbias_in)


def _layernorm(v, g, b):
    mu = jnp.mean(v, axis=-1, keepdims=True)
    vc = v - mu
    rstd = lax.rsqrt(jnp.mean(vc * vc, axis=-1, keepdims=True) + EPS)
    xhat = vc * rstd
    return xhat, rstd, xhat * g + b


def sgu_fwd(ps, ln_g, ln_b, w_tril, b_t):
    s = ps.shape[0]
    rows = min(4 * BLK, s)

    def body(ps_ref, g_ref, b_ref, w_ref, bt_ref, y_ref):
        u = ps_ref[:, 0:1024].astype(F32)
        v = ps_ref[:, 1024:2048].astype(F32)
        z = ps_ref[:, 2048:3072].astype(F32)
        _, _, vn = _layernorm(v, g_ref[...], b_ref[...])
        gate = u * z * _sig(z)
        for c in range(rows // BLK):
            ch = slice(c * BLK, (c + 1) * BLK)
            for g in range(SG_GROUPS):
                sl = slice(g * 128, (g + 1) * 128)
                mixed = _dot(w_ref[g], vn[ch, sl]) + bt_ref[:, g:g + 1]
                y_ref[ch, sl] = (gate[ch, sl] * mixed).astype(y_ref.dtype)

    return pl.pallas_call(
        body, name="sgu_fwd", grid=(s // rows,),
        in_specs=[pl.BlockSpec((rows, SG_COLS), lambda c: (c, 0)), _full((1, 1024)), _full((1, 1024)),
                  _full((SG_GROUPS, BLK, BLK)), _full((BLK, 128))],
        out_specs=pl.BlockSpec((rows, 1024), lambda c: (c, 0)),
        out_shape=jax.ShapeDtypeStruct((s, 1024), ACT_DTYPE),
        compiler_params=_cparams(("parallel",)),
    )(ps, ln_g, ln_b, w_tril, b_t)


def sgu_bwd(ps, dy, ln_g, ln_b, w_tril, w_tril_t, b_t):
    s = ps.shape[0]
    rows = min(4 * BLK, s)

    def body(ps_ref, dy_ref, g_ref, b_ref, w_ref, wt_ref, bt_ref, dps_ref, dw_ref, dbt_ref, dg_ref, db_ref, dvn_scr):
        @pl.when(pl.program_id(0) == 0)
        def _():
            dw_ref[...] = jnp.zeros_like(dw_ref)
            dbt_ref[...] = jnp.zeros_like(dbt_ref)
            dg_ref[...] = jnp.zeros_like(dg_ref)
            db_ref[...] = jnp.zeros_like(db_ref)

        u = ps_ref[:, 0:1024].astype(F32)
        v = ps_ref[:, 1024:2048].astype(F32)
        z = ps_ref[:, 2048:3072].astype(F32)
        dy = dy_ref[...].astype(F32)
        xhat, rstd, vn = _layernorm(v, g_ref[...], b_ref[...])
        sz = _sig(z)
        silu = z * sz
        row = lax.broadcasted_iota(jnp.int32, (BLK, BLK), 0)
        colm = lax.broadcasted_iota(jnp.int32, (BLK, BLK), 1)
        tril = row >= colm
        dbt = jnp.zeros((BLK, 128), F32)
        dsilu_z = _dsilu(z, sz)
        for c in range(rows // BLK):
            ch = slice(c * BLK, (c + 1) * BLK)
            for g in range(SG_GROUPS):
                sl = slice(g * 128, (g + 1) * 128)
                vng = vn[ch, sl]
                mixed = _dot(w_ref[g], vng) + bt_ref[:, g:g + 1]
                dyg, ug = dy[ch, sl], u[ch, sl]
                dps_ref[ch, sl] = (dyg * mixed * silu[ch, sl]).astype(dps_ref.dtype)
                dps_ref[ch, 2048 + g * 128:2048 + (g + 1) * 128] = (
                    dyg * ug * mixed * dsilu_z[ch, sl]).astype(dps_ref.dtype)
                dm = dyg * ug * silu[ch, sl]
                dw_ref[g] += jnp.where(tril, _dot_nt(dm, vng), 0.0)
                dbt = dbt + jnp.where(colm == g, jnp.sum(dm, axis=1, keepdims=True), 0.0)
                dvn_scr[ch, sl] = _dot(wt_ref[g], dm)
        dbt_ref[...] += dbt
        dvn = dvn_scr[...]
        dg_ref[...] += jnp.sum(dvn * xhat, axis=0, keepdims=True)
        db_ref[...] += jnp.sum(dvn, axis=0, keepdims=True)
        dxh = dvn * g_ref[...]
        dv = rstd * (dxh - jnp.mean(dxh, axis=-1, keepdims=True)
                     - xhat * jnp.mean(dxh * xhat, axis=-1, keepdims=True))
        dps_ref[:, 1024:2048] = dv.astype(dps_ref.dtype)

    return pl.pallas_call(
        body, name="sgu_bwd", grid=(s // rows,),
        in_specs=[pl.BlockSpec((rows, SG_COLS), lambda c: (c, 0)), pl.BlockSpec((rows, 1024), lambda c: (c, 0)),
                  _full((1, 1024)), _full((1, 1024)), _full((SG_GROUPS, BLK, BLK)), _full((SG_GROUPS, BLK, BLK)),
                  _full((BLK, 128))],
        out_specs=[pl.BlockSpec((rows, SG_COLS), lambda c: (c, 0)), _full((SG_GROUPS, BLK, BLK)), _full((BLK, 128)),
                   _full((1, 1024)), _full((1, 1024))],
        out_shape=[jax.ShapeDtypeStruct((s, SG_COLS), ACT_DTYPE), jax.ShapeDtypeStruct((SG_GROUPS, BLK, BLK), F32),
                   jax.ShapeDtypeStruct((BLK, 128), F32), jax.ShapeDtypeStruct((1, 1024), F32),
                   jax.ShapeDtypeStruct((1, 1024), F32)],
        scratch_shapes=[pltpu.VMEM((rows, 1024), F32)],
        compiler_params=_cparams(("arbitrary",)),
    )(ps, dy, ln_g, ln_b, w_tril, w_tril_t, b_t)


def _shift_down(cur, prev16, k):
    if k == 0:
        return cur
    r = pltpu.roll(cur, k, 0)
    rp = pltpu.roll(prev16, k, 0)
    row = lax.broadcasted_iota(jnp.int32, (8, cur.shape[1]), 0)
    return jnp.concatenate([jnp.where(row < k, rp[0:8], r[0:8]), r[8:]], axis=0)


def _shift_up(cur, next16, k):
    if k == 0:
        return cur
    n = cur.shape[0]
    r = pltpu.roll(cur, n - k, 0)
    rn = pltpu.roll(next16, 16 - k, 0)
    row = lax.broadcasted_iota(jnp.int32, (8, cur.shape[1]), 0)
    return jnp.concatenate([r[:n - 8], jnp.where(row >= 8 - k, rn[8:16], r[n - 8:])], axis=0)


def _bcast8(v):
    return jnp.broadcast_to(v, (16, v.shape[1]))


def _causal_conv(xbc, prev16, cw, cbias):
    pre = cbias + cw[3:4] * xbc
    for k in (1, 2, 3):
        pre = pre + cw[3 - k:4 - k] * _shift_down(xbc, prev16, k)
    return pre


class _Ssd:
    def __init__(self, pre, dtr, dtb, alog, dsk, tri, e):
        self.pre = pre
        self.sg = _sig(pre)
        act = pre * self.sg
        self.xs = act[:, 0:SSM_WIDTH]
        self.bm = act[:, SSM_WIDTH:SSM_WIDTH + 512]
        self.cm = act[:, SSM_WIDTH + 512:CONV_DIM]
        self.dtp = dtr + dtb
        self.dt = jnp.maximum(self.dtp, 0.0) + jnp.log(1.0 + jnp.exp(-jnp.abs(self.dtp)))
        self.a = -jnp.exp(alog)
        self.acs = _dot_hi(tri, self.dt * self.a)
        self.acs_t = self.acs.T
        tot = self.acs[BLK - 1:BLK]
        self.ecs = jnp.exp(self.acs)
        self.dte = jnp.exp(tot - self.acs)
        self.cd = jnp.exp(tot)
        self.dt_x = _dot_onehot(self.dt, e)
        self.ecs_x = _dot_onehot(self.ecs, e)
        self.dte_x = _dot_onehot(self.dte, e)
        self.cd_x = _dot_onehot(_bcast8(self.cd), e)[0:1]
        self.d_x = _dot_onehot(_bcast8(dsk), e)[0:1]
        self.xdt = self.xs * self.dt_x
        row = lax.broadcasted_iota(jnp.int32, (BLK, BLK), 0)
        col = lax.broadcasted_iota(jnp.int32, (BLK, BLK), 1)
        self.tril = row >= col

    def group(self, g):
        sl = slice(g * 128, (g + 1) * 128)
        bg, cg = self.bm[:, sl], self.cm[:, sl]
        return bg, cg, _dot_nt(cg, bg)

    def decay(self, h):
        seg = self.acs[:, h:h + 1] - self.acs_t[h:h + 1, :]
        return jnp.exp(jnp.where(self.tril, seg, NEG))

    def y_pre_gate(self, ht_of, yd_scr, yoff_scr):
        for g in range(SSM_GROUPS):
            bg, cg, cb = self.group(g)
            for j in range(8):
                h = g * 8 + j
                sl = slice(h * 64, (h + 1) * 64)
                yd_scr[:, sl] = _dot(cb * self.decay(h), self.xdt[:, sl])
            gs = slice(g * SSM_GW, (g + 1) * SSM_GW)
            yoff_scr[:, gs] = _dot(cg, ht_of(g)) * self.ecs_x[:, gs]
        return yd_scr[...] + yoff_scr[...] + self.d_x * self.xs


def _ssd_consts():
    hh = lax.broadcasted_iota(jnp.int32, (128, SSM_WIDTH), 0)
    ch = lax.broadcasted_iota(jnp.int32, (128, SSM_WIDTH), 1)
    e = (ch // 64 == hh).astype(jnp.bfloat16)
    row = lax.broadcasted_iota(jnp.int32, (BLK, BLK), 0)
    col = lax.broadcasted_iota(jnp.int32, (BLK, BLK), 1)
    tri = (row >= col).astype(F32)
    return tri, e


def _pad_lanes(v, n=128):
    return jnp.pad(v, ((0, 0), (0, n - v.shape[1])))


def ssd_fwd(pm, cw, cbias, dtb, alog, dsk, ng, ex=None):
    s = pm.shape[0]
    nc = s // BLK
    tri, e = _ssd_consts()

    def body(pm_ref, prev_ref, cw_ref, cb_ref, dtb_ref, al_ref, d_ref, ng_ref, tri_ref, e_ref,
             y_ref, st_ref, pre_ref, ht_ref, yd_scr, yoff_scr):
        c = pl.program_id(0)

        @pl.when(c == 0)
        def _():
            ht_ref[...] = jnp.zeros_like(ht_ref)

        xbc = pm_ref[:, 0:CONV_DIM].astype(F32)
        prev16 = jnp.where(c == 0, 0.0, prev_ref[...].astype(F32))
        pre = _causal_conv(xbc, prev16, cw_ref[...], cb_ref[...])
        pre_ref[...] = pre.astype(pre_ref.dtype)
        f = _Ssd(pre, pm_ref[:, DT_OFF:DT_OFF + 128].astype(F32), dtb_ref[...], al_ref[...], d_ref[...],
                 tri_ref[...], e_ref[...])
        st_ref[0] = ht_ref[...]
        y = f.y_pre_gate(lambda g: ht_ref[g], yd_scr, yoff_scr)
        for g in range(SSM_GROUPS):
            bg, _, _ = f.group(g)
            gs = slice(g * SSM_GW, (g + 1) * SSM_GW)
            ht_ref[g] = ht_ref[g] * f.cd_x[:, gs] + _dot_tn(bg, f.xdt[:, gs] * f.dte_x[:, gs])
        z = pm_ref[:, CONV_DIM:CONV_DIM + SSM_WIDTH].astype(F32)
        ypre = y * z * _sig(z)
        for g in range(SSM_GROUPS):
            gs = slice(g * SSM_GW, (g + 1) * SSM_GW)
            yg = ypre[:, gs]
            rr = lax.rsqrt(jnp.mean(yg * yg, axis=-1, keepdims=True) + EPS)
            y_ref[:, gs] = (yg * rr * ng_ref[:, gs]).astype(y_ref.dtype)

    own, hosted = _call_hosting(
        body, "ssd_fwd", nc,
        in_specs=[pl.BlockSpec((BLK, SSM_COLS), lambda c: (c, 0)),
                  pl.BlockSpec((16, CONV_DIM), lambda c: (jnp.maximum(8 * c - 1, 0), 0)),
                  _full((4, CONV_DIM)), _full((1, CONV_DIM)), _full((1, 128)), _full((1, 128)), _full((1, 128)),
                  _full((1, SSM_WIDTH)), _full((BLK, BLK)), _full((128, SSM_WIDTH))],
        out_specs=[pl.BlockSpec((BLK, SSM_WIDTH), lambda c: (c, 0)),
                   pl.BlockSpec((1, SSM_GROUPS, 128, SSM_GW), lambda c: (c, 0, 0, 0)),
                   pl.BlockSpec((BLK, CONV_DIM), lambda c: (c, 0))],
        out_shape=[jax.ShapeDtypeStruct((s, SSM_WIDTH), ACT_DTYPE),
                   jax.ShapeDtypeStruct((nc, SSM_GROUPS, 128, SSM_GW), F32),
                   jax.ShapeDtypeStruct((s, CONV_DIM), ACT_DTYPE)],
        scratch=[pltpu.VMEM((SSM_GROUPS, 128, SSM_GW), F32), pltpu.VMEM((BLK, SSM_WIDTH), F32),
                 pltpu.VMEM((BLK, SSM_WIDTH), F32)],
        args=(pm, pm, cw, cbias, dtb, alog, dsk, ng, tri, e), ex=ex)
    return (*own, hosted) if ex is not None else tuple(own)


def ssd_bwd(pm, pre, dy, states, cw, dtb, alog, dsk, ng, ex=None):
    s = pm.shape[0]
    nc = s // BLK
    tri, e = _ssd_consts()
    tri_t, e_t = tri.T, e.T

    def body(pm_ref, pre_ref, dy_ref, st_ref, cw_ref, dtb_ref, al_ref, d_ref, ng_ref,
             tri_ref, trit_ref, e_ref, et_ref,
             dpm_ref, dcw_ref, dcb_ref, dvec_ref, dng_ref,
             dht_ref, dcar_ref, yd_scr, yoff_scr, dx_scr, r2_scr, hs_scr, da_scr, dat_scr, dd_scr, dbc_scr):
        i = pl.program_id(0)
        n = nc - 1 - i

        @pl.when(i == 0)
        def _():
            dht_ref[...] = jnp.zeros_like(dht_ref)
            dcar_ref[...] = jnp.zeros_like(dcar_ref)
            dcw_ref[...] = jnp.zeros_like(dcw_ref)
            dcb_ref[...] = jnp.zeros_like(dcb_ref)
            dvec_ref[...] = jnp.zeros_like(dvec_ref)
            dng_ref[...] = jnp.zeros_like(dng_ref)
            dd_scr[...] = jnp.zeros_like(dd_scr)
            da_scr[...] = jnp.zeros_like(da_scr)
            dat_scr[...] = jnp.zeros_like(dat_scr)

        cw = cw_ref[...]
        f = _Ssd(pre_ref[...].astype(F32), pm_ref[:, DT_OFF:DT_OFF + 128].astype(F32), dtb_ref[...], al_ref[...],
                 d_ref[...], tri_ref[...], e_ref[...])
        et = et_ref[...]
        y = f.y_pre_gate(lambda g: st_ref[0, g], yd_scr, yoff_scr)

        z = pm_ref[:, CONV_DIM:CONV_DIM + SSM_WIDTH].astype(F32)
        dyv = dy_ref[...].astype(F32)
        sz = _sig(z)
        silu = z * sz
        ypre = y * silu
        for g in range(SSM_GROUPS):
            gs = slice(g * SSM_GW, (g + 1) * SSM_GW)
            yg = ypre[:, gs]
            rr = lax.rsqrt(jnp.mean(yg * yg, axis=-1, keepdims=True) + EPS)
            nrm = yg * rr
            dng_ref[:, gs] += jnp.sum(dyv[:, gs] * nrm, axis=0, keepdims=True)
            dn = dyv[:, gs] * ng_ref[:, gs]
            dx_scr[:, gs] = rr * (dn - nrm * jnp.mean(dn * nrm, axis=-1, keepdims=True))
        dypre = dx_scr[...]
        d_y = dypre * silu
        dpm_ref[:, CONV_DIM:CONV_DIM + SSM_WIDTH] = (dypre * y * _dsilu(z, sz)).astype(dpm_ref.dtype)

        for g in range(SSM_GROUPS):
            bg, cg, cb = f.group(g)
            gs = slice(g * SSM_GW, (g + 1) * SSM_GW)
            htg = st_ref[0, g]
            dhn = dht_ref[g]
            dcb = jnp.zeros((BLK, BLK), F32)
            for j in range(8):
                h = g * 8 + j
                sl = slice(h * 64, (h + 1) * 64)
                dec = f.decay(h)
                dyh = d_y[:, sl]
                dmd = _dot_nt(dyh, f.xdt[:, sl]) * dec
                dcb = dcb + dmd
                gm = dmd * cb
                da_scr[:, h:h + 1] = jnp.sum(gm, axis=1, keepdims=True)
                dat_scr[h:h + 1, :] = jnp.sum(gm, axis=0, keepdims=True)
                dx_scr[:, sl] = _dot_tn(cb * dec, dyh)
            dz = f.ecs_x[:, gs] * d_y[:, gs]
            dbc_scr[:, 512 + g * 128:512 + (g + 1) * 128] = _dot(dcb, bg) + _dot_nt(dz, htg)
            dbc_scr[:, g * 128:(g + 1) * 128] = _dot_tn(dcb, cg) + _dot_nt(f.xdt[:, gs] * f.dte_x[:, gs], dhn)
            dws = _dot(bg, dhn)
            dx_scr[:, gs] += f.dte_x[:, gs] * dws
            r2_scr[:, gs] = dws * f.xdt[:, gs]
            hs_scr[:, gs] = _bcast8(jnp.sum(dhn * htg, axis=0, keepdims=True))
            dht_ref[g] = f.cd_x[:, gs] * dhn + _dot_tn(cg, dz)
        d_x = dx_scr[...]
        r1 = _dot(d_y * yoff_scr[...], et)
        r2 = _dot(r2_scr[...], et) * f.dte
        dcd = _dot_onehot(hs_scr[...], et)[0:1]
        d_tot = jnp.sum(r2, axis=0, keepdims=True) + dcd * f.cd
        row = lax.broadcasted_iota(jnp.int32, (BLK, 128), 0)
        d_a = da_scr[...] - dat_scr[...].T + r1 - r2 + jnp.where(row == BLK - 1, d_tot, 0.0)
        dadt = _dot_hi(trit_ref[...], d_a)
        ddt = dadt * f.a + _dot(d_x * f.xs, et)
        lane = lax.broadcasted_iota(jnp.int32, (BLK, 128), 1)
        dr = jnp.where(lane < SSM_HEADS, ddt * _sig(f.dtp), 0.0)
        dvec_ref[0:1, :] += jnp.sum(dr, axis=0, keepdims=True)
        dvec_ref[1:2, :] += jnp.sum(dadt * f.dt, axis=0, keepdims=True) * f.a
        dd_scr[...] += _bcast8(jnp.sum(d_y * f.xs, axis=0, keepdims=True))
        dpm_ref[:, DT_OFF:DT_OFF + 128] = dr.astype(dpm_ref.dtype)
        dpm_ref[:, DT_OFF + 128:SSM_COLS] = jnp.zeros((BLK, 128), dpm_ref.dtype)

        dxs = d_x * f.dt_x + f.d_x * d_y
        dact = jnp.concatenate([dxs, dbc_scr[...]], axis=1)
        dpre = dact * _dsilu(f.pre, f.sg)
        dcb_ref[...] += jnp.sum(dpre, axis=0, keepdims=True)
        xbc = pm_ref[:, 0:CONV_DIM].astype(F32)
        dxraw = jnp.zeros((BLK, CONV_DIM), F32)
        nxt = dcar_ref[...]
        for k in range(4):
            ahead = _shift_up(dpre, nxt, k)
            dcw_ref[3 - k:4 - k, :] += jnp.sum(ahead * xbc, axis=0, keepdims=True)
            dxraw = dxraw + cw[3 - k:4 - k] * ahead
        dcar_ref[...] = dpre[0:16]
        dpm_ref[:, 0:CONV_DIM] = dxraw.astype(dpm_ref.dtype)

        @pl.when(i == nc - 1)
        def _():
            dvec_ref[2:3, :] = _dot_onehot(dd_scr[...], et)[0:1]

    own, hosted = _call_hosting(
        body, "ssd_bwd", nc,
        in_specs=[pl.BlockSpec((BLK, SSM_COLS), lambda i: (nc - 1 - i, 0)),
                  pl.BlockSpec((BLK, CONV_DIM), lambda i: (nc - 1 - i, 0)),
                  pl.BlockSpec((BLK, SSM_WIDTH), lambda i: (nc - 1 - i, 0)),
                  pl.BlockSpec((1, SSM_GROUPS, 128, SSM_GW), lambda i: (nc - 1 - i, 0, 0, 0)),
                  _full((4, CONV_DIM)), _full((1, 128)), _full((1, 128)), _full((1, 128)),
                  _full((1, SSM_WIDTH)), _full((BLK, BLK)), _full((BLK, BLK)), _full((128, SSM_WIDTH)),
                  _full((SSM_WIDTH, 128))],
        out_specs=[pl.BlockSpec((BLK, SSM_COLS), lambda i: (nc - 1 - i, 0)),
                   _full((8, CONV_DIM)), _full((1, CONV_DIM)), _full((8, 128)), _full((1, SSM_WIDTH))],
        out_shape=[jax.ShapeDtypeStruct((s, SSM_COLS), ACT_DTYPE), jax.ShapeDtypeStruct((8, CONV_DIM), F32),
                   jax.ShapeDtypeStruct((1, CONV_DIM), F32), jax.ShapeDtypeStruct((8, 128), F32),
                   jax.ShapeDtypeStruct((1, SSM_WIDTH), F32)],
        scratch=[pltpu.VMEM((SSM_GROUPS, 128, SSM_GW), F32), pltpu.VMEM((16, CONV_DIM), F32),
                 pltpu.VMEM((BLK, SSM_WIDTH), F32), pltpu.VMEM((BLK, SSM_WIDTH), F32),
                 pltpu.VMEM((BLK, SSM_WIDTH), F32), pltpu.VMEM((BLK, SSM_WIDTH), F32),
                 pltpu.VMEM((16, SSM_WIDTH), F32), pltpu.VMEM((BLK, 128), F32), pltpu.VMEM((128, BLK), F32),
                 pltpu.VMEM((16, SSM_WIDTH), F32), pltpu.VMEM((BLK, 1024), F32)],
        args=(pm, pre, dy, states, cw, dtb, alog, dsk, ng, tri, tri_t, e, e_t), ex=ex)
    return (*own, hosted) if ex is not None else tuple(own)


def merge_fwd(x, ya, ys, ym, pg, wa, ws, wm, wo, g_post, target=None):
    s, d = x.shape
    tm = min(256, s)
    with_loss = target is not None

    def body(*refs):
        x_ref, ya_ref, ys_ref, ym_ref, pg_ref, wa_ref, ws_ref, wm_ref, wo_ref, g_ref = refs[:10]
        if with_loss:
            t_ref, xo_ref, l_ref, ba_ref, bs_ref, bm_ref, mg_ref, out_ref = refs[10:]
        else:
            xo_ref, ba_ref, bs_ref, bm_ref, mg_ref, out_ref = refs[10:]
        ba = _dot_tn(ya_ref[...], wa_ref[...])
        bs = _dot(ys_ref[...], ws_ref[...])
        bm = _dot(ym_ref[...], wm_ref[...])
        merged = (_sig(pg_ref[:, 0:d].astype(F32)) * ba + _sig(pg_ref[:, d:2 * d].astype(F32)) * bs
                  + _sig(pg_ref[:, 2 * d:3 * d].astype(F32)) * bm)
        out = _dot(merged, wo_ref[...])
        r = lax.rsqrt(jnp.mean(out * out, axis=-1, keepdims=True) + EPS)
        y = x_ref[...] + out * r * g_ref[...]
        if with_loss:
            @pl.when(pl.program_id(0) == 0)
            def _():
                l_ref[...] = jnp.zeros_like(l_ref)

            err = y - t_ref[...]
            xo_ref[...] = err * (1.0 / d)
            part = jnp.sum(jnp.sum(err * err, axis=-1, keepdims=True) * (1.0 / d), axis=0, keepdims=True)
            l_ref[...] += 0.5 * jnp.broadcast_to(part, l_ref.shape)
        else:
            xo_ref[...] = y
        ba_ref[...] = ba.astype(ba_ref.dtype)
        bs_ref[...] = bs.astype(bs_ref.dtype)
        bm_ref[...] = bm.astype(bm_ref.dtype)
        mg_ref[...] = merged.astype(mg_ref.dtype)
        out_ref[...] = out.astype(out_ref.dtype)

    rows = lambda w: pl.BlockSpec((tm, w), lambda i: (i, 0))
    act = jax.ShapeDtypeStruct((s, d), ACT_DTYPE)
    loss_spec = [_full((8, 128))] if with_loss else []
    loss_shape = [jax.ShapeDtypeStruct((8, 128), F32)] if with_loss else []
    return pl.pallas_call(
        body, name="merge_fwd_loss" if with_loss else "merge_fwd", grid=(s // tm,),
        in_specs=[rows(d), pl.BlockSpec((d, tm), lambda i: (0, i)), rows(d), rows(2 * d), rows(3 * d), _full((d, d)),
                  _full((d, d)), _full((2 * d, d)), _full((d, d)), _full((1, d))] + ([rows(d)] if with_loss else []),
        out_specs=[rows(d)] + loss_spec + [rows(d)] * 5,
        out_shape=[jax.ShapeDtypeStruct((s, d), F32)] + loss_shape + [act] * 5,
        compiler_params=_cparams(("arbitrary" if with_loss else "parallel",)),
    )(x, ya, ys, ym, pg, wa, ws, wm, wo, g_post, *([target] if with_loss else []))


def merge_bwd(dx, out_s, pg, ba, bs, bm, wa, ws, wm, wo, g_post):
    s, d = dx.shape
    tm = min(256, s)

    def body(dx_ref, out_ref, pg_ref, ba_ref, bs_ref, bm_ref, wa_ref, ws_ref, wm_ref, wo_ref, g_ref,
             dout_ref, dba_ref, dbs_ref, dbm_ref, dpg_ref, dya_ref, dys_ref, dym_ref, dg_ref):
        @pl.when(pl.program_id(0) == 0)
        def _():
            dg_ref[...] = jnp.zeros_like(dg_ref)

        o = out_ref[...].astype(F32)
        dxv = dx_ref[...]
        r = lax.rsqrt(jnp.mean(o * o, axis=-1, keepdims=True) + EPS)
        nrm = o * r
        dg_ref[...] += jnp.sum(dxv * nrm, axis=0, keepdims=True)
        dn = dxv * g_ref[...]
        dout = r * (dn - nrm * jnp.mean(dn * nrm, axis=-1, keepdims=True))
        dout_ref[...] = dout.astype(dout_ref.dtype)
        dmerged = _dot_nt(dout, wo_ref[...])
        for q, (b_ref, db_ref, w_ref, dy_ref) in enumerate(((ba_ref, dba_ref, wa_ref, dya_ref),
                                                            (bs_ref, dbs_ref, ws_ref, dys_ref),
                                                            (bm_ref, dbm_ref, wm_ref, dym_ref))):
            gt = _sig(pg_ref[:, q * d:(q + 1) * d].astype(F32))
            db = dmerged * gt
            db_ref[...] = db.astype(db_ref.dtype)
            dpg_ref[:, q * d:(q + 1) * d] = (dmerged * b_ref[...].astype(F32) * gt * (1.0 - gt)).astype(dpg_ref.dtype)
            if q == 0:
                dy_ref[...] = _dot_nt(w_ref[...], db).astype(dy_ref.dtype)
            else:
                dy_ref[...] = _dot_nt(db, w_ref[...]).astype(dy_ref.dtype)

    rows = lambda w: pl.BlockSpec((tm, w), lambda i: (i, 0))
    act = lambda w: jax.ShapeDtypeStruct((s, w), ACT_DTYPE)
    return pl.pallas_call(
        body, name="merge_bwd", grid=(s // tm,),
        in_specs=[rows(d), rows(d), rows(3 * d), rows(d), rows(d), rows(d), _full((d, d)), _full((d, d)),
                  _full((2 * d, d)), _full((d, d)), _full((1, d))],
        out_specs=[rows(d), rows(d), rows(d), rows(d), rows(3 * d), pl.BlockSpec((d, tm), lambda i: (0, i)), rows(d),
                   rows(2 * d), _full((1, d))],
        out_shape=[act(d), act(d), act(d), act(d), act(3 * d), jax.ShapeDtypeStruct((d, s), ACT_DTYPE), act(d),
                   act(2 * d), jax.ShapeDtypeStruct((1, d), F32)],
        compiler_params=_cparams(("arbitrary",)),
    )(dx, out_s, pg, ba, bs, bm, wa, ws, wm, wo, g_post)


def _mesh_pos():
    x, y, c = lax.axis_index("x"), lax.axis_index("y"), lax.axis_index("c")
    return x, y, c, 4 * x + 2 * y + c


def _peer(x, y, c, k):
    px = 1 - x if k & 4 else x
    py = 1 - y if k & 2 else y
    pc = 1 - c if k & 1 else c
    return (px, py, pc), 4 * px + 2 * py + pc


class Exchange:
    SAME_CORE = (2, 4, 6)

    def __init__(self, scattered, gathered):
        self.ns = len(scattered)
        self.arrays = list(scattered) + list(gathered)
        self.na = len(self.arrays)
        any_spec = pl.BlockSpec(memory_space=pl.ANY)
        self.in_specs = [any_spec] * self.na
        self.out_specs = [any_spec] * self.na
        self.out_shape = ([jax.ShapeDtypeStruct(a.shape, a.dtype) for a in scattered]
                          + [jax.ShapeDtypeStruct((N_DEV,) + a.shape, a.dtype) for a in gathered])
        self.scratch = [pltpu.SemaphoreType.DMA((self.na, N_DEV - 1)), pltpu.SemaphoreType.DMA((self.na, N_DEV - 1)),
                        pltpu.SemaphoreType.DMA((self.na,))]

    def _src(self, ins, q, slot):
        return ins[q].at[slot] if q < self.ns else ins[q]

    def _local(self, ins, outs, sems):
        me = _mesh_pos()[3]
        return [pltpu.make_async_copy(self._src(ins, q, me), outs[q].at[me], sems[2].at[q]) for q in range(self.na)]

    def _direct(self, ins, outs, sems, relations, arrays):
        x, y, c, me = _mesh_pos()
        copies = []
        for k in relations:
            peer, pidx = _peer(x, y, c, k)
            for q in arrays:
                copies.append(pltpu.make_async_remote_copy(
                    src_ref=self._src(ins, q, pidx), dst_ref=outs[q].at[me], send_sem=sems[0].at[q, k - 1],
                    recv_sem=sems[1].at[q, k - 1], device_id=peer, device_id_type=MESH))
        return copies

    def _arrivals(self, ins, outs, sems, relations, arrays):
        x, y, c, _ = _mesh_pos()
        copies = []
        for k in relations:
            peer, pidx = _peer(x, y, c, k)
            for q in arrays:
                copies.append(pltpu.make_async_remote_copy(
                    src_ref=self._src(ins, q, pidx), dst_ref=outs[q].at[pidx], send_sem=sems[0].at[q, k - 1],
                    recv_sem=sems[1].at[q, k - 1], device_id=peer, device_id_type=MESH))
        return copies

    def _relays(self, outs, sems):
        x, y, c, _ = _mesh_pos()
        sibling, _ = _peer(x, y, c, 1)
        copies = []
        for k in self.SAME_CORE:
            _, pidx = _peer(x, y, c, k)
            for q in range(self.ns, self.na):
                copies.append(pltpu.make_async_remote_copy(
                    src_ref=outs[q].at[pidx], dst_ref=outs[q].at[pidx], send_sem=sems[0].at[q, k],
                    recv_sem=sems[1].at[q, k], device_id=sibling, device_id_type=MESH))
        return copies

    def _sends(self, ins, outs, sems):
        return (self._direct(ins, outs, sems, range(1, N_DEV), range(self.ns))
                + self._direct(ins, outs, sems, (1,) + self.SAME_CORE, range(self.ns, self.na)))

    def start(self, ins, outs, sems):
        for cp in self._local(ins, outs, sems) + self._sends(ins, outs, sems):
            cp.start()

    def relay(self, ins, outs, sems):
        for cp in self._arrivals(ins, outs, sems, self.SAME_CORE, range(self.ns, self.na)):
            cp.wait_recv()
        for cp in self._relays(outs, sems):
            cp.start()

    def wait(self, ins, outs, sems):
        for cp in (self._arrivals(ins, outs, sems, range(1, N_DEV), range(self.ns))
                   + self._arrivals(ins, outs, sems, (1, 3, 5, 7), range(self.ns, self.na))):
            cp.wait_recv()
        for cp in self._sends(ins, outs, sems) + self._relays(outs, sems):
            cp.wait_send()
        for cp in self._local(ins, outs, sems):
            cp.wait()


def adamw(parts_list, w, m, v, tile, name, ex=None):
    npart, _, dp = parts_list[0].shape
    d = w.shape[-1]
    counts = [p.shape[1] // tile for p in parts_list]
    starts = [sum(counts[:q]) for q in range(len(counts))]
    n_lists = len(parts_list)

    def body(*refs):
        p_refs = refs[:n_lists]
        w_ref, m_ref, v_ref, g_ref, dw_ref, nm_ref, nv_ref = refs[n_lists:]
        i = pl.program_id(0)
        for q, p_ref in enumerate(p_refs):
            @pl.when((i >= starts[q]) & (i < starts[q] + counts[q]))
            def _(p_ref=p_ref):
                acc = p_ref[0, :, 0:d].astype(F32)
                for k in range(1, npart):
                    acc = acc + p_ref[k, :, 0:d].astype(F32)
                g_ref[...] = acc

        g = g_ref[...]
        nm = ADAM_B1 * m_ref[...] + (1.0 - ADAM_B1) * g
        nv = ADAM_B2 * v_ref[...] + (1.0 - ADAM_B2) * (g * g)
        nm_ref[...] = nm
        nv_ref[...] = nv
        m_hat = nm / (1.0 - ADAM_B1 ** ADAM_STEP)
        v_hat = nv / (1.0 - ADAM_B2 ** ADAM_STEP)
        dw_ref[...] = -ADAM_LR * (m_hat / (jnp.sqrt(v_hat) + ADAM_EPS) + ADAM_WD * w_ref[...])

    def part_rows(q):
        return lambda i: (0, jnp.clip(i - starts[q], 0, counts[q] - 1), 0)

    if w.ndim == 3:
        rows = pl.BlockSpec((None, tile, d), lambda i: (i // counts[0], i % counts[0], 0))
    else:
        rows = pl.BlockSpec((tile, d), lambda i: (i, 0))
    own, hosted = _call_hosting(
        body, name, sum(counts),
        in_specs=[pl.BlockSpec((npart, tile, dp), part_rows(q)) for q in range(n_lists)] + [rows, rows, rows],
        out_specs=[rows] * 4, out_shape=[jax.ShapeDtypeStruct(w.shape, F32)] * 4, scratch=[],
        args=(*parts_list, w, m, v), ex=ex)
    return (*own, hosted) if ex is not None else tuple(own)


def _pad_rows(a, rows):
    return jnp.pad(a, ((0, rows - a.shape[0]), (0, 0)))


def _pack_rest(w_att, w_sg, w_ssm, w_out):
    parts = []
    for l in range(2):
        parts += [w_att[l], w_sg[l], w_ssm[l], w_out[l]]
    return jnp.concatenate(parts, axis=0)


def _unpack_rest(p):
    outs = [[], [], [], []]
    o = 0
    for l in range(2):
        for q, rws in enumerate(REST_PARTS):
            outs[q].append(p[o:o + rws])
            o += rws
    return [jnp.stack(t) for t in outs]


def _pack_win(w_in):
    return jnp.pad(w_in.reshape(2 * D_MODEL, WIN_SHARD), ((0, 0), (0, WIN_LANES - WIN_SHARD)))


W_IN_MAP = ((0, 1024, "att", 0), (1024, 1280, "att", 2048), (1280, 2304, "att", 1024), (2304, 5376, "sg", 0),
            (5376, 7424, "ssm", 3072), (7424, 10496, "ssm", 0), (10496, 10528, "ssm", 5120), (10528, 13600, "gate", 0))
SLAB_COLS = {"att": ATT_COLS, "sg": SG_COLS, "ssm": SSM_COLS, "gate": GATE_COLS}


def _slabs_from_shards(g):
    slabs = {}
    for name, width in SLAB_COLS.items():
        pieces, filled = [], 0
        for ga, gb, _, off in sorted((m for m in W_IN_MAP if m[2] == name), key=lambda m: m[3]):
            assert off == filled
            a = ga
            while a < gb:
                d = a // WIN_SHARD
                hi = min(gb, WIN_SHARD * (d + 1))
                pieces.append(g[d, :, a - WIN_SHARD * d:hi - WIN_SHARD * d])
                a = hi
            filled += gb - ga
        if filled < width:
            pieces.append(jnp.zeros((D_MODEL, width - filled), g.dtype))
        slabs[name] = jnp.concatenate(pieces, axis=1)
    return slabs


def _shards_from_slabs(dslabs):
    out = []
    for d in range(N_DEV):
        a, b = WIN_SHARD * d, WIN_SHARD * (d + 1)
        pieces = []
        for ga, gb, name, off in W_IN_MAP:
            lo, hi = max(a, ga), min(b, gb)
            if lo < hi:
                pieces.append(dslabs[name][:, off + lo - ga:off + hi - ga])
        pieces.append(jnp.zeros((D_MODEL, WIN_LANES - WIN_SHARD), pieces[0].dtype))
        out.append(jnp.concatenate(pieces, axis=1).astype(WIRE_DTYPE))
    return jnp.stack(out)


SMALL_SIZES = (("norm_pre", 2048), ("norm_post", 2048), ("rel_bias", 512), ("att_sinks", 32), ("sg_ln_g", 2048),
               ("sg_ln_b", 2048), ("sg_w", 262144), ("sg_b", 2048), ("ssm_conv_b", 6144), ("ssm_dt_bias", 64),
               ("ssm_a_log", 64), ("ssm_d", 64), ("ssm_norm_g", 4096), ("conv_w_full", 24576))


def _pack_small(d):
    parts = []
    for name, size in SMALL_SIZES:
        rows = 8 * (-(-size // (8 * D_MODEL)))
        flat = d[name].reshape(-1) if name in d else jnp.zeros((size,), F32)
        parts.append(jnp.pad(flat, (0, rows * D_MODEL - size)).reshape(rows, D_MODEL))
    return _pad_rows(jnp.concatenate(parts, axis=0), SMALL_ROWS)


def _unpack_small(p, shapes):
    out, o = {}, 0
    for name, size in SMALL_SIZES:
        rows = 8 * (-(-size // (8 * D_MODEL)))
        if name in shapes:
            out[name] = p[o:o + rows].reshape(-1)[:size].reshape(shapes[name])
        o += rows
    return out


def _bucket_onehot_t():
    qi = jnp.arange(BLK, dtype=jnp.int32)[None, :]
    kj = jnp.arange(BLK, dtype=jnp.int32)[:, None]
    dd = (qi - kj) & (BLK - 1)
    in_window = dd >= 0
    max_exact = REL_BUCKETS // 2
    dist_f = jnp.maximum(dd, 1).astype(F32)
    large = max_exact + (jnp.log(dist_f / max_exact) / math.log(128 / max_exact)
                         * (REL_BUCKETS - max_exact)).astype(jnp.int32)
    large = jnp.minimum(large, REL_BUCKETS - 1)
    bucket = jnp.where(dd < max_exact, dd, large).reshape(1, -1)
    onehot_t = (bucket == jnp.arange(REL_BUCKETS, dtype=jnp.int32)[:, None]).astype(F32)
    maskadd = jnp.where(in_window, 0.0, NEG).astype(F32).reshape(1, -1)
    return onehot_t, maskadd


WEIGHTS = ['w_in', 'norm_pre', 'norm_post', 'rel_bias', 'att_sinks', 'sg_ln_g', 'sg_ln_b', 'sg_w', 'sg_b',
           'ssm_conv_w', 'ssm_conv_b', 'ssm_dt_bias', 'ssm_a_log', 'ssm_d', 'ssm_norm_g',
           'w_br_att', 'w_br_sg', 'w_br_ssm', 'w_out']
REST = ('w_br_att', 'w_br_sg', 'w_br_ssm', 'w_out')


def kernel(x, w_in, norm_pre, norm_post, rel_bias, att_sinks, sg_ln_g, sg_ln_b, sg_w, sg_b, ssm_conv_w, ssm_conv_b, ssm_dt_bias, ssm_a_log, ssm_d, ssm_norm_g, w_br_att, w_br_sg, w_br_ssm, w_out, loss_target, m_w_in, m_norm_pre, m_norm_post, m_rel_bias, m_att_sinks, m_sg_ln_g, m_sg_ln_b, m_sg_w, m_sg_b, m_ssm_conv_w, m_ssm_conv_b, m_ssm_dt_bias, m_ssm_a_log, m_ssm_d, m_ssm_norm_g, m_w_br_att, m_w_br_sg, m_w_br_ssm, m_w_out, v_w_in, v_norm_pre, v_norm_post, v_rel_bias, v_att_sinks, v_sg_ln_g, v_sg_ln_b, v_sg_w, v_sg_b, v_ssm_conv_w, v_ssm_conv_b, v_ssm_dt_bias, v_ssm_a_log, v_ssm_d, v_ssm_norm_g, v_w_br_att, v_w_br_sg, v_w_br_ssm, v_w_out):
    w = dict(w_in=w_in, norm_pre=norm_pre, norm_post=norm_post, rel_bias=rel_bias, att_sinks=att_sinks,
             sg_ln_g=sg_ln_g, sg_ln_b=sg_ln_b, sg_w=sg_w, sg_b=sg_b, ssm_conv_w=ssm_conv_w, ssm_conv_b=ssm_conv_b,
             ssm_dt_bias=ssm_dt_bias, ssm_a_log=ssm_a_log, ssm_d=ssm_d, ssm_norm_g=ssm_norm_g,
             w_br_att=w_br_att, w_br_sg=w_br_sg, w_br_ssm=w_br_ssm, w_out=w_out)
    mom = dict(w_in=m_w_in, norm_pre=m_norm_pre, norm_post=m_norm_post, rel_bias=m_rel_bias, att_sinks=m_att_sinks,
               sg_ln_g=m_sg_ln_g, sg_ln_b=m_sg_ln_b, sg_w=m_sg_w, sg_b=m_sg_b, ssm_conv_w=m_ssm_conv_w,
               ssm_conv_b=m_ssm_conv_b, ssm_dt_bias=m_ssm_dt_bias, ssm_a_log=m_ssm_a_log, ssm_d=m_ssm_d,
               ssm_norm_g=m_ssm_norm_g, w_br_att=m_w_br_att, w_br_sg=m_w_br_sg, w_br_ssm=m_w_br_ssm, w_out=m_w_out)
    var = dict(w_in=v_w_in, norm_pre=v_norm_pre, norm_post=v_norm_post, rel_bias=v_rel_bias, att_sinks=v_att_sinks,
               sg_ln_g=v_sg_ln_g, sg_ln_b=v_sg_ln_b, sg_w=v_sg_w, sg_b=v_sg_b, ssm_conv_w=v_ssm_conv_w,
               ssm_conv_b=v_ssm_conv_b, ssm_dt_bias=v_ssm_dt_bias, ssm_a_log=v_ssm_a_log, ssm_d=v_ssm_d,
               ssm_norm_g=v_ssm_norm_g, w_br_att=v_w_br_att, w_br_sg=v_w_br_sg, w_br_ssm=v_w_br_ssm, w_out=v_w_out)
    xs0 = x[0]
    target = loss_target[0]
    my_dev = 4 * lax.axis_index("x") + 2 * lax.axis_index("y") + lax.axis_index("c")

    conv_shard = _pad_rows(ssm_conv_w.reshape(-1, D_MODEL), 8)
    win_shard = _pack_win(w_in).astype(WIRE_DTYPE)
    rest_shard = _pack_rest(*[w[n] for n in REST]).astype(WIRE_DTYPE)
    layer_shards = [[win_shard[l * D_MODEL:(l + 1) * D_MODEL], rest_shard[l * LAYER_REST:(l + 1) * LAYER_REST]]
                    for l in range(2)]
    h0, (g_win0, gathered_conv) = rmsnorm_fwd(xs0, norm_pre[0][None], Exchange([], [layer_shards[0][0], conv_shard]))
    conv_full = gathered_conv[:, 0:3].reshape(N_DEV, 2, 4, 384).transpose(1, 2, 0, 3).reshape(2, 4, CONV_DIM)

    def set_rest(lw, g_rest):
        o = 0
        for name, rws in zip(("att", "sg", "ssm", "out"), REST_PARTS):
            lw[name] = g_rest[:, o:o + rws].reshape(N_DEV * rws, D_MODEL).astype(MXU_DTYPE)
            o += rws

    def layer_weights(l, g_win):
        slabs = _slabs_from_shards(g_win)
        lw = {"in_" + name: slab.astype(MXU_DTYPE) for name, slab in slabs.items()}
        lw["in_att"] = lw["in_att"].T
        tril = jnp.tril(jnp.ones((BLK, BLK), bool))
        sgw = jnp.where(tril[None], sg_w[l], 0.0)
        lw.update(
            g_pre=norm_pre[l][None], g_post=norm_post[l][None], sinks=jnp.repeat(att_sinks[l], BLK).reshape(2, GROUP_LANES),
            ln_g=sg_ln_g[l][None], ln_b=sg_ln_b[l][None], sgw=sgw.astype(MXU_DTYPE),
            sgw_t=sgw.transpose(0, 2, 1).astype(MXU_DTYPE), sgb_t=_pad_lanes(sg_b[l].T),
            cw=conv_full[l], cb=ssm_conv_b[l][None], dtb=_pad_lanes(ssm_dt_bias[l][None]),
            alog=_pad_lanes(ssm_a_log[l][None]), dsk=_pad_lanes(ssm_d[l][None]), ng=ssm_norm_g[l][None])
        return lw

    onehot_t, maskadd = _bucket_onehot_t()
    bias = bias_table(rel_bias.T, onehot_t, maskadd).reshape(2, GROUP_HEADS, BLK, BLK).transpose(0, 2, 1, 3)
    bias = bias.reshape(2, BLK, GROUP_LANES)

    saved = []
    xl = xs0
    layers = [layer_weights(0, g_win0)]
    for l in range(2):
        lw = layers[l]
        h = h0 if l == 0 else rmsnorm_fwd(xl, lw["g_pre"])
        pa = mm_nt(lw["in_att"], h, 1152, "proj_att")
        ps = mm_nn(h, lw["in_sg"], 1536, "proj_sg")
        pm = mm_nn(h, lw["in_ssm"], 1792, "proj_ssm")
        pg = mm_nn(h, lw["in_gate"], 1536, "proj_gate")
        if l == 0:
            ya, (g_rest0,) = attn_fwd(pa, bias, lw["sinks"], Exchange([], [layer_shards[0][1]]))
            set_rest(lw, g_rest0)
        else:
            ya = attn_fwd(pa, bias, lw["sinks"])
        sgu_args = (ps, lw["ln_g"], lw["ln_b"], lw["sgw"], lw["sgb_t"])
        ssd_args = (pm, lw["cw"], lw["cb"], lw["dtb"], lw["alog"], lw["dsk"], lw["ng"])
        ys = sgu_fwd(*sgu_args)
        if l == 0:
            ym, states, conv_pre, (g_win1, g_rest1) = ssd_fwd(*ssd_args, Exchange([], layer_shards[1]))
            layers.append(layer_weights(1, g_win1))
            set_rest(layers[1], g_rest1)
        else:
            ym, states, conv_pre = ssd_fwd(*ssd_args)
        merge_args = (xl, ya, ys, ym, pg, lw["att"], lw["sg"], lw["ssm"], lw["out"], lw["g_post"])
        if l == 0:
            x_next, ba, bs, bm, merged, out_s = merge_fwd(*merge_args)
        else:
            dx, loss_part, ba, bs, bm, merged, out_s = merge_fwd(*merge_args, target)
        saved.append(dict(x=xl, h=h, pa=pa, ps=ps, pm=pm, pg=pg, ya=ya, ys=ys, ym=ym, states=states,
                          conv_pre=conv_pre, ba=ba, bs=bs, bm=bm, merged=merged, out_s=out_s))
        xl = x_next

    loss = lax.psum(loss_part[0, 0], ("x", "y", "c"))

    dbias = jnp.zeros((2, BLK, GROUP_LANES), F32)
    win_grads, rest_grads = [None, None], [None, None]
    small = {n: [None, None] for n in ("norm_pre", "norm_post", "att_sinks", "sg_ln_g", "sg_ln_b", "sg_w", "sg_b",
                                       "ssm_conv_b", "ssm_dt_bias", "ssm_a_log", "ssm_d", "ssm_norm_g",
                                       "conv_w_full")}
    for l in (1, 0):
        lw, sv = layers[l], saved[l]
        dout, dba, dbs, dbm, dpg, dya, dys, dym, dg_post = merge_bwd(
            dx, sv["out_s"], sv["pg"], sv["ba"], sv["bs"], sv["bm"], lw["att"], lw["sg"], lw["ssm"], lw["out"],
            lw["g_post"])
        dw_out = mm_tn(sv["merged"], dout, 1024, "dw_out")
        dw_att = mm_kn(sv["ya"], dba, 1024, "dw_br_att")
        dw_sg = mm_tn(sv["ys"], dbs, 1024, "dw_br_sg")
        dw_ssm = mm_tn(sv["ym"], dbm, 1024, "dw_br_ssm")
        rest_grads[l] = jnp.concatenate(
            [dw_att.reshape(N_DEV, 128, D_MODEL), dw_sg.reshape(N_DEV, 128, D_MODEL),
             dw_ssm.reshape(N_DEV, 256, D_MODEL), dw_out.reshape(N_DEV, 128, D_MODEL)], axis=1).astype(WIRE_DTYPE)
        dpa, dbias, dsinks = attn_bwd(sv["pa"], dya, bias, lw["sinks"], dbias)
        dps, dsgw, dsgb_t, dln_g, dln_b = sgu_bwd(sv["ps"], dys, lw["ln_g"], lw["ln_b"], lw["sgw"], lw["sgw_t"],
                                                  lw["sgb_t"])
        ssd_args = (sv["pm"], sv["conv_pre"], dym, sv["states"], lw["cw"], lw["dtb"], lw["alog"], lw["dsk"], lw["ng"])
        if l == 0:
            dpm, dcw, dcb, dvec, dng, (recv_win1, recv_rest0) = ssd_bwd(
                *ssd_args, Exchange([win_grads[1], rest_grads[0]], []))
        else:
            dpm, dcw, dcb, dvec, dng, (recv_rest1,) = ssd_bwd(*ssd_args, Exchange([rest_grads[1]], []))
        dslabs = dict(att=mm_kn(dpa, sv["h"], 1152, "dw_in_att").T, sg=mm_tn(sv["h"], dps, 3072, "dw_in_sg"),
                      ssm=mm_tn(sv["h"], dpm, 2688, "dw_in_ssm"), gate=mm_tn(sv["h"], dpg, 3072, "dw_in_gate"))
        win_grads[l] = _shards_from_slabs(dslabs)
        dh_args = ([dpa, dps, dpm, dpg], [lw["in_att"], lw["in_sg"], lw["in_ssm"], lw["in_gate"]], sv["x"],
                   lw["g_pre"], dx)
        if l == 0:
            dx, dg_pre, (recv_win0,) = dh_norm_bwd(*dh_args, Exchange([win_grads[0]], []))
        else:
            dx, dg_pre = dh_norm_bwd(*dh_args)
        small["norm_pre"][l] = dg_pre[0]
        small["norm_post"][l] = dg_post[0]
        small["att_sinks"][l] = dsinks[0, :ATT_HEADS]
        small["sg_ln_g"][l] = dln_g[0]
        small["sg_ln_b"][l] = dln_b[0]
        small["sg_w"][l] = dsgw
        small["sg_b"][l] = dsgb_t[:, :SG_GROUPS].T
        small["ssm_conv_b"][l] = dcb[0]
        small["ssm_dt_bias"][l] = dvec[0, :SSM_HEADS]
        small["ssm_a_log"][l] = dvec[1, :SSM_HEADS]
        small["ssm_d"][l] = dvec[2, :SSM_HEADS]
        small["ssm_norm_g"][l] = dng[0]
        small["conv_w_full"][l] = dcw[0:4]
    grad_x = dx
    dbias = dbias.reshape(2, BLK, GROUP_HEADS, BLK).transpose(0, 2, 1, 3).reshape(ATT_HEADS, BLK * BLK)
    d_rel_bias = bias_table_bwd(dbias, onehot_t).T

    small_d = {n: jnp.stack(v) for n, v in small.items()}
    small_d["rel_bias"] = d_rel_bias
    *res_win, (recv_small,) = adamw([recv_win0, recv_win1], w_in, m_w_in, v_w_in, WIN_TILE, "adamw_w_in",
                                    Exchange([], [_pack_small(small_d)]))
    res_rest = adamw([recv_rest0, recv_rest1], _pack_rest(*[w[n] for n in REST]), _pack_rest(*[mom[n] for n in REST]),
                     _pack_rest(*[var[n] for n in REST]), REST_TILE, "adamw_rest")
    small_names = [n for n, _ in SMALL_SIZES if n != "conv_w_full"]
    g_s, dw_s, nm_s, nv_s = adamw([recv_small], _pack_small({n: w[n] for n in small_names}),
                                  _pack_small({n: mom[n] for n in small_names}),
                                  _pack_small({n: var[n] for n in small_names}), SMALL_TILE, "adamw_small")
    shapes = {n: w[n].shape for n in small_names}
    shapes["conv_w_full"] = (2, 4, CONV_DIM)
    g_conv_full = _unpack_small(g_s, shapes)["conv_w_full"]
    g_conv = lax.dynamic_slice_in_dim(g_conv_full, my_dev * 384, 384, axis=2)
    pack_conv = lambda a: _pad_rows(a.reshape(-1, D_MODEL), 8)
    g_c, dw_c, nm_c, nv_c = adamw([pack_conv(g_conv)[None]], pack_conv(ssm_conv_w), pack_conv(m_ssm_conv_w),
                                  pack_conv(v_ssm_conv_w), 8, "adamw_conv")

    results = {}
    for q, (tag, psm, pc) in enumerate((("grad", g_s, g_c), ("delta", dw_s, dw_c), ("new_m", nm_s, nm_c),
                                        ("new_v", nv_s, nv_c))):
        r = dict(zip(REST, _unpack_rest(res_rest[q])))
        r["w_in"] = res_win[q]
        r.update(_unpack_small(psm, {n: w[n].shape for n in small_names}))
        r["ssm_conv_w"] = pc[0:3].reshape(2, 4, 384)
        results[tag] = r
    outs = [loss, grad_x[None]]
    for tag in ("grad", "delta", "new_m", "new_v"):
        outs += [results[tag][n] for n in WEIGHTS]
    return tuple(outs)
```

```python
import math

import jax
import jax.numpy as jnp
from jax import lax
from jax.experimental import pallas as pl
from jax.experimental.pallas import tpu as pltpu

F32 = jnp.float32
MXU_DTYPE = jnp.bfloat16
ACT_DTYPE = jnp.bfloat16
WIRE_DTYPE = jnp.bfloat16
HI = lax.Precision.HIGHEST
MESH = pl.DeviceIdType.MESH

D_MODEL = 1024
N_DEV = 8
ATT_HEADS = 16
HEAD_DIM = 64
BLK = 128
SG_GROUPS = 8
SSM_WIDTH = 2048
SSM_HEADS = 32
SSM_GROUPS = 4
SSM_GW = SSM_WIDTH // SSM_GROUPS
CONV_DIM = 3072
REL_BUCKETS = 32
EPS = 1e-6
NEG = -1e30

ATT_COLS = 2304
SG_COLS = 3072
SSM_COLS = 5376
GATE_COLS = 3072
DT_OFF = 5120

VMEM_LIMIT_V7X = 56 * 2 ** 20
DH_TK = 768

ADAM_LR, ADAM_B1, ADAM_B2, ADAM_EPS, ADAM_WD, ADAM_STEP = 0.001, 0.9, 0.999, 1e-08, 0.01, 10

WIN_SHARD = 1700
WIN_LANES = 1792
REST_PARTS = (128, 128, 256, 128)
LAYER_REST = sum(REST_PARTS)
REST_TILE = 128
WIN_TILE = 128
SMALL_ROWS = 384
SMALL_TILE = 128


def _cparams(sem=None):
    return pltpu.CompilerParams(dimension_semantics=sem, vmem_limit_bytes=VMEM_LIMIT_V7X)


def _dot(a, b):
    return jnp.dot(a.astype(MXU_DTYPE), b.astype(MXU_DTYPE), preferred_element_type=F32)


def _dot_nt(a, b):
    return lax.dot_general(a.astype(MXU_DTYPE), b.astype(MXU_DTYPE), (((1,), (1,)), ((), ())),
                           preferred_element_type=F32)


def _dot_tn(a, b):
    return lax.dot_general(a.astype(MXU_DTYPE), b.astype(MXU_DTYPE), (((0,), (0,)), ((), ())),
                           preferred_element_type=F32)


def _dot_hi(a, b):
    return jnp.dot(a, b, precision=HI, preferred_element_type=F32)


def _dot_onehot(a, onehot):
    hi = a.astype(jnp.bfloat16)
    lo = (a - hi.astype(F32)).astype(jnp.bfloat16)
    return (jnp.dot(hi, onehot, preferred_element_type=F32) + jnp.dot(lo, onehot, preferred_element_type=F32))


def _dot_hi_nt(a, b):
    return lax.dot_general(a, b, (((1,), (1,)), ((), ())), precision=HI, preferred_element_type=F32)


def _sig(x):
    return 1.0 / (1.0 + jnp.exp(-x))


def _dsilu(x, s):
    return s * (1.0 + x * (1.0 - s))


def _full(shape):
    nd = len(shape)
    return pl.BlockSpec(shape, lambda *_: (0,) * nd)


def rmsnorm_fwd(x, g, ex=None):
    s, d = x.shape
    tm = min(512, s)

    def body(x_ref, g_ref, o_ref):
        xv = x_ref[...]
        r = lax.rsqrt(jnp.mean(xv * xv, axis=-1, keepdims=True) + EPS)
        o_ref[...] = (xv * r * g_ref[...]).astype(o_ref.dtype)

    (h,), hosted = _call_hosting(
        body, "rmsnorm_fwd", s // tm,
        in_specs=[pl.BlockSpec((tm, d), lambda i: (i, 0)), _full((1, d))],
        out_specs=[pl.BlockSpec((tm, d), lambda i: (i, 0))],
        out_shape=[jax.ShapeDtypeStruct((s, d), ACT_DTYPE)], scratch=[], args=(x, g), ex=ex)
    return (h, hosted) if ex is not None else h


def mm_nn(a, b, tn, name):
    s, k = a.shape
    n = b.shape[1]
    tm = min(2048, s)

    def body(a_ref, b_ref, o_ref):
        o_ref[...] = _dot(a_ref[...], b_ref[...]).astype(o_ref.dtype)

    return pl.pallas_call(
        body, name=name, grid=(s // tm, n // tn),
        in_specs=[pl.BlockSpec((tm, k), lambda i, j: (i, 0)), pl.BlockSpec((k, tn), lambda i, j: (0, j))],
        out_specs=pl.BlockSpec((tm, tn), lambda i, j: (i, j)),
        out_shape=jax.ShapeDtypeStruct((s, n), ACT_DTYPE),
        compiler_params=_cparams(("parallel", "arbitrary")),
    )(a, b)


def mm_nt(a, b, tm, name):
    m, k = a.shape
    s = b.shape[0]
    ts = min(2048, s)

    def body(a_ref, b_ref, o_ref):
        o_ref[...] = _dot_nt(a_ref[...], b_ref[...]).astype(o_ref.dtype)

    return pl.pallas_call(
        body, name=name, grid=(s // ts, m // tm),
        in_specs=[pl.BlockSpec((tm, k), lambda i, j: (j, 0)), pl.BlockSpec((ts, k), lambda i, j: (i, 0))],
        out_specs=pl.BlockSpec((tm, ts), lambda i, j: (j, i)),
        out_shape=jax.ShapeDtypeStruct((m, s), ACT_DTYPE),
        compiler_params=_cparams(("parallel", "arbitrary")),
    )(a, b)


def mm_kn(a, b, tm, name):
    m, s = a.shape
    n = b.shape[1]
    ts = min(512, s)
    nt = s // ts

    def body(a_ref, b_ref, o_ref, acc_ref):
        @pl.when(pl.program_id(1) == 0)
        def _():
            acc_ref[...] = jnp.zeros_like(acc_ref)

        acc_ref[...] += _dot(a_ref[...], b_ref[...])

        @pl.when(pl.program_id(1) == nt - 1)
        def _():
            o_ref[...] = acc_ref[...].astype(o_ref.dtype)

    return pl.pallas_call(
        body, name=name, grid=(m // tm, nt),
        in_specs=[pl.BlockSpec((tm, ts), lambda j, t: (j, t)), pl.BlockSpec((ts, n), lambda j, t: (t, 0))],
        out_specs=pl.BlockSpec((tm, n), lambda j, t: (j, 0)),
        out_shape=jax.ShapeDtypeStruct((m, n), WIRE_DTYPE),
        scratch_shapes=[pltpu.VMEM((tm, n), F32)],
        compiler_params=_cparams(("parallel", "arbitrary")),
    )(a, b)


def mm_tn(a, b, tn, name):
    s, k = a.shape
    n = b.shape[1]
    ts = min(512, s)
    nt = s // ts

    def body(a_ref, b_ref, o_ref, acc_ref):
        @pl.when(pl.program_id(1) == 0)
        def _():
            acc_ref[...] = jnp.zeros_like(acc_ref)

        acc_ref[...] += _dot_tn(a_ref[...], b_ref[...])

        @pl.when(pl.program_id(1) == nt - 1)
        def _():
            o_ref[...] = acc_ref[...].astype(o_ref.dtype)

    return pl.pallas_call(
        body, name=name, grid=(n // tn, nt),
        in_specs=[pl.BlockSpec((ts, k), lambda j, t: (t, 0)), pl.BlockSpec((ts, tn), lambda j, t: (t, j))],
        out_specs=pl.BlockSpec((k, tn), lambda j, t: (0, j)),
        out_shape=jax.ShapeDtypeStruct((k, n), WIRE_DTYPE),
        scratch_shapes=[pltpu.VMEM((k, tn), F32)],
        compiler_params=_cparams(("parallel", "arbitrary")),
    )(a, b)


def dh_norm_bwd(dslabs, wslabs, x, g, dres, ex=None):
    s, d = x.shape
    tm = min(1024, s)
    widths = [ds.shape[0 if q == 0 else 1] for q, ds in enumerate(dslabs)]
    tks = [DH_TK] * len(widths)
    counts = [wd // t for wd, t in zip(widths, tks)]
    starts = [sum(counts[:i]) for i in range(len(counts))]
    nk = sum(counts)
    ns = len(dslabs)

    hosted = ex is not None
    ni = s // tm

    def mm_body(*refs):
        (own_in, (dh_ref,), _), hosted_refs = _split_hosted(refs, 2 * ns, 1, 0, ex)
        d_refs, w_refs = own_in[:ns], own_in[ns:]
        i, k = pl.program_id(0), pl.program_id(1)
        if hosted:
            @pl.when((i == 0) & (k == 0))
            def _():
                ex.start(*hosted_refs)

            @pl.when((i == ni - 1) & (k == nk - 1))
            def _():
                ex.relay(*hosted_refs)
                ex.wait(*hosted_refs)

        @pl.when(k == 0)
        def _():
            dh_ref[...] = jnp.zeros_like(dh_ref)

        for q in range(ns):
            @pl.when((k >= starts[q]) & (k < starts[q] + counts[q]))
            def _(q=q):
                if q == 0:
                    dh_ref[...] += _dot_tn(d_refs[q][...], w_refs[q][...])
                else:
                    dh_ref[...] += _dot_nt(d_refs[q][...], w_refs[q][...])

    def clamp(q):
        if q == 0:
            return pl.BlockSpec((tks[q], tm), lambda i, k: (jnp.clip(k - starts[q], 0, counts[q] - 1), i))
        return pl.BlockSpec((tm, tks[q]), lambda i, k: (i, jnp.clip(k - starts[q], 0, counts[q] - 1)))

    def clamp_w(q):
        if q == 0:
            return pl.BlockSpec((tks[q], d), lambda i, k: (jnp.clip(k - starts[q], 0, counts[q] - 1), 0))
        return pl.BlockSpec((d, tks[q]), lambda i, k: (0, jnp.clip(k - starts[q], 0, counts[q] - 1)))

    res = pl.pallas_call(
        mm_body, name="dh_matmul_scatter" if hosted else "dh_matmul", grid=(ni, nk),
        in_specs=([clamp(q) for q in range(ns)] + [clamp_w(q) for q in range(ns)]
                  + (ex.in_specs if hosted else [])),
        out_specs=[pl.BlockSpec((tm, d), lambda i, k: (i, 0))] + (ex.out_specs if hosted else []),
        out_shape=[jax.ShapeDtypeStruct((s, d), F32)] + (ex.out_shape if hosted else []),
        scratch_shapes=ex.scratch if hosted else [],
        compiler_params=_cparams(("arbitrary" if hosted else "parallel", "arbitrary")),
    )(*dslabs, *wslabs, *(ex.arrays if hosted else []))
    dh, ex_results = res[0], res[1:]

    te = min(512, s)

    def norm_body(dh_ref, x_ref, g_ref, dres_ref, dx_ref, dg_ref):
        @pl.when(pl.program_id(0) == 0)
        def _():
            dg_ref[...] = jnp.zeros_like(dg_ref)

        xv = x_ref[...]
        r = lax.rsqrt(jnp.mean(xv * xv, axis=-1, keepdims=True) + EPS)
        xn = xv * r
        dhv = dh_ref[...]
        dg_ref[...] += jnp.sum(dhv * xn, axis=0, keepdims=True)
        dxn = dhv * g_ref[...]
        dx_ref[...] = dres_ref[...] + r * (dxn - xn * jnp.mean(dxn * xn, axis=-1, keepdims=True))

    rows = pl.BlockSpec((te, d), lambda i: (i, 0))
    dx, dg = pl.pallas_call(
        norm_body, name="norm_bwd", grid=(s // te,),
        in_specs=[rows, rows, _full((1, d)), rows],
        out_specs=[rows, _full((1, d))],
        out_shape=[jax.ShapeDtypeStruct((s, d), F32), jax.ShapeDtypeStruct((1, d), F32)],
        compiler_params=_cparams(("arbitrary",)),
    )(dh, x, g, dres)
    return (dx, dg, ex_results) if hosted else (dx, dg)


def bias_table(rel_bias_t, onehot_t, maskadd):
    n = onehot_t.shape[1]
    tn = 8192

    def body(r_ref, o_ref, m_ref, out_ref):
        out_ref[...] = _dot_hi(r_ref[...], o_ref[...]) + m_ref[...]

    return pl.pallas_call(
        body, name="bias_table", grid=(n // tn,),
        in_specs=[_full((ATT_HEADS, REL_BUCKETS)), pl.BlockSpec((REL_BUCKETS, tn), lambda j: (0, j)),
                  pl.BlockSpec((1, tn), lambda j: (0, j))],
        out_specs=pl.BlockSpec((ATT_HEADS, tn), lambda j: (0, j)),
        out_shape=jax.ShapeDtypeStruct((ATT_HEADS, n), F32),
        compiler_params=_cparams(("parallel",)),
    )(rel_bias_t, onehot_t, maskadd)


def bias_table_bwd(dbias, onehot_t):
    n = onehot_t.shape[1]
    tn = 8192

    def body(d_ref, o_ref, out_ref):
        @pl.when(pl.program_id(0) == 0)
        def _():
            out_ref[...] = jnp.zeros_like(out_ref)

        out_ref[...] += _dot_hi_nt(d_ref[...], o_ref[...])

    return pl.pallas_call(
        body, name="bias_table_bwd", grid=(n // tn,),
        in_specs=[pl.BlockSpec((ATT_HEADS, tn), lambda j: (0, j)), pl.BlockSpec((REL_BUCKETS, tn), lambda j: (0, j))],
        out_specs=_full((ATT_HEADS, REL_BUCKETS)),
        out_shape=jax.ShapeDtypeStruct((ATT_HEADS, REL_BUCKETS), F32),
        compiler_params=_cparams(("arbitrary",)),
    )(dbias, onehot_t)


def _fold(full, tri):
    return jnp.where(tri, full[BLK:2 * BLK], full[0:BLK])


def _unfold(folded, tri):
    return jnp.concatenate([jnp.where(tri, 0.0, folded), jnp.where(tri, folded, 0.0)], axis=0)


GROUP_HEADS = ATT_HEADS // 2
GROUP_LANES = GROUP_HEADS * BLK


def _att_group(qg, kcat, vt_cat, bias_g, sink_g, tri, no_prev):
    l = _fold(_dot(kcat, qg), tri) * (HEAD_DIM ** -0.5) + bias_g
    l = jnp.where(no_prev, NEG, l)
    m = jnp.maximum(jnp.max(l, axis=0, keepdims=True), sink_g)
    p = jnp.exp(l - m)
    es = jnp.exp(sink_g - m)
    inv = 1.0 / (jnp.sum(p, axis=0, keepdims=True) + es)
    p = p * inv
    pcat = _unfold(p, tri)
    return p, pcat, es * inv, _dot(vt_cat, pcat)


ATT_SUB = 4


def _heads_to_lanes(ref, row0, ln):
    return jnp.concatenate([ref[row0 + j * HEAD_DIM:row0 + (j + 1) * HEAD_DIM, ln] for j in range(GROUP_HEADS)], axis=1)


def _lanes_to_heads(ref, row0, ln, val):
    for j in range(GROUP_HEADS):
        ref[row0 + j * HEAD_DIM:row0 + (j + 1) * HEAD_DIM, ln] = val[:, j * BLK:(j + 1) * BLK].astype(ref.dtype)


def _kv_cat(kvp, kvc, g):
    lo = g * HEAD_DIM
    kt_cat = jnp.concatenate([kvp[lo:lo + HEAD_DIM], kvc[lo:lo + HEAD_DIM]], axis=1)
    vt_cat = jnp.concatenate([kvp[128 + lo:128 + lo + HEAD_DIM], kvc[128 + lo:128 + lo + HEAD_DIM]], axis=1)
    return kt_cat, vt_cat


def _tri_masks(n):
    row = lax.broadcasted_iota(jnp.int32, (BLK, GROUP_LANES), 0)
    query = lax.broadcasted_iota(jnp.int32, (BLK, GROUP_LANES), 1) & (BLK - 1)
    tri = row <= query
    return tri, (n == 0) & jnp.logical_not(tri)


def _split_hosted(refs, n_in, n_out, n_scratch, ex):
    na = ex.na if ex is not None else 0
    o = 0
    parts = []
    for cnt in (n_in, na, n_out, na, n_scratch, 3 if ex is not None else 0):
        parts.append(refs[o:o + cnt])
        o += cnt
    own_in, ex_in, own_out, ex_out, own_scr, ex_sems = parts
    return (own_in, own_out, own_scr), (ex_in, ex_out, ex_sems)


def _call_hosting(body, name, nsteps, in_specs, out_specs, out_shape, scratch, args, ex):
    n_in, n_out, n_scr = len(in_specs), len(out_specs), len(scratch)
    hosted = ex is not None

    def full_body(*refs):
        (own_in, own_out, own_scr), hosted_refs = _split_hosted(refs, n_in, n_out, n_scr, ex)
        if hosted:
            @pl.when(pl.program_id(0) == 0)
            def _():
                ex.start(*hosted_refs)

            @pl.when(pl.program_id(0) == max(nsteps - max(nsteps // 8, 4), 0))
            def _():
                ex.relay(*hosted_refs)

            @pl.when(pl.program_id(0) == nsteps - 1)
            def _():
                ex.wait(*hosted_refs)

        body(*own_in, *own_out, *own_scr)

    res = pl.pallas_call(
        full_body, name=name + "_hosting" if hosted else name, grid=(nsteps,),
        in_specs=list(in_specs) + (ex.in_specs if hosted else []),
        out_specs=list(out_specs) + (ex.out_specs if hosted else []),
        out_shape=list(out_shape) + (ex.out_shape if hosted else []),
        scratch_shapes=list(scratch) + (ex.scratch if hosted else []),
        compiler_params=_cparams(("arbitrary",)),
    )(*args, *(ex.arrays if hosted else []))
    return res[:n_out], res[n_out:]


def attn_fwd(pa, bias, sinks, ex=None):
    s = pa.shape[1]
    nsteps = s // (ATT_SUB * BLK)

    def body(pa_ref, kvp_ref, bias_ref, sink_ref, y_ref):
        for sub in range(ATT_SUB):
            n = pl.program_id(0) * ATT_SUB + sub
            ln = slice(sub * BLK, (sub + 1) * BLK)
            kvc = pa_ref[2048:2304, ln]
            kvp = kvp_ref[...] if sub == 0 else pa_ref[2048:2304, (sub - 1) * BLK:sub * BLK]
            tri, no_prev = _tri_masks(n)
            for g in range(2):
                kt_cat, vt_cat = _kv_cat(kvp, kvc, g)
                row0 = g * GROUP_HEADS * HEAD_DIM
                _, _, _, o = _att_group(_heads_to_lanes(pa_ref, row0, ln), kt_cat.astype(F32).T, vt_cat, bias_ref[g],
                                        sink_ref[g:g + 1, :], tri, no_prev)
                z = _heads_to_lanes(pa_ref, 1024 + row0, ln).astype(F32)
                _lanes_to_heads(y_ref, row0, ln, o * z * _sig(z))

    (y,), hosted = _call_hosting(
        body, "attn_fwd", nsteps,
        in_specs=[pl.BlockSpec((ATT_COLS, ATT_SUB * BLK), lambda n: (0, n)),
                  pl.BlockSpec((256, BLK), lambda n: (8, jnp.maximum(ATT_SUB * n - 1, 0))),
                  _full((2, BLK, GROUP_LANES)), _full((2, GROUP_LANES))],
        out_specs=[pl.BlockSpec((1024, ATT_SUB * BLK), lambda n: (0, n))],
        out_shape=[jax.ShapeDtypeStruct((1024, s), ACT_DTYPE)], scratch=[],
        args=(pa, pa, bias, sinks), ex=ex)
    return (y, hosted) if ex is not None else y


def attn_bwd(pa, dy, bias, sinks, dbias_in):
    s = pa.shape[1]
    nsteps = s // (ATT_SUB * BLK)

    def body(pa_ref, kvp_ref, dy_ref, bias_ref, sink_ref, dbin_ref, dpa_ref, dbias_ref, dsink_ref, carry, dsink_acc):
        i = pl.program_id(0)

        @pl.when(i == 0)
        def _():
            dbias_ref[...] = dbin_ref[...]
            dsink_acc[...] = jnp.zeros_like(dsink_acc)
            carry[...] = jnp.zeros_like(carry)

        scale = HEAD_DIM ** -0.5
        for sub in reversed(range(ATT_SUB)):
            n = (nsteps - 1 - i) * ATT_SUB + sub
            ln = slice(sub * BLK, (sub + 1) * BLK)
            kvc = pa_ref[2048:2304, ln]
            kvp = kvp_ref[...] if sub == 0 else pa_ref[2048:2304, (sub - 1) * BLK:sub * BLK]
            tri, no_prev = _tri_masks(n)
            for g in range(2):
                kt_cat, vt_cat = _kv_cat(kvp, kvc, g)
                row0 = g * GROUP_HEADS * HEAD_DIM
                qg = _heads_to_lanes(pa_ref, row0, ln)
                p, pcat, psink, o = _att_group(qg, kt_cat.astype(F32).T, vt_cat, bias_ref[g], sink_ref[g:g + 1, :],
                                               tri, no_prev)
                z = _heads_to_lanes(pa_ref, 1024 + row0, ln).astype(F32)
                dyg = _heads_to_lanes(dy_ref, row0, ln).astype(F32)
                sz = _sig(z)
                d_o = dyg * z * sz
                _lanes_to_heads(dpa_ref, 1024 + row0, ln, dyg * _dsilu(z, sz) * o)
                delta = jnp.sum(d_o * o, axis=0, keepdims=True)
                dl = p * (_fold(_dot(vt_cat.astype(F32).T, d_o), tri) - delta)
                dsink_acc[g:g + 1, :] += psink * delta
                dbias_ref[g] += dl
                dlcat = _unfold(dl, tri)
                _lanes_to_heads(dpa_ref, row0, ln, _dot(kt_cat, dlcat) * scale)
                for q, dkv in enumerate((_dot_nt(qg, dlcat) * scale, _dot_nt(d_o, pcat))):
                    r0 = q * 128 + g * HEAD_DIM
                    dpa_ref[2048 + r0:2048 + r0 + HEAD_DIM, ln] = (
                        dkv[:, BLK:2 * BLK] + carry[r0:r0 + HEAD_DIM, :]).astype(dpa_ref.dtype)
                    carry[r0:r0 + HEAD_DIM, :] = dkv[:, 0:BLK]

        @pl.when(i == nsteps - 1)
        def _():
            lane = lax.broadcasted_iota(jnp.int32, (1, 128), 1)
            dsink = jnp.zeros((1, 128), F32)
            for h in range(ATT_HEADS):
                g, j = divmod(h, GROUP_HEADS)
                tot = jnp.sum(dsink_acc[g:g + 1, j * BLK:(j + 1) * BLK], axis=1, keepdims=True)
                dsink = dsink + jnp.where(lane == h, -tot, 0.0)
            dsink_ref[...] = dsink

    return pl.pallas_call(
        body, name="attn_bwd", grid=(nsteps,),
        in_specs=[pl.BlockSpec((ATT_COLS, ATT_SUB * BLK), lambda i: (0, nsteps - 1 - i)),
                  pl.BlockSpec((256, BLK), lambda i: (8, jnp.maximum(ATT_SUB * (nsteps - 1 - i) - 1, 0))),
                  pl.BlockSpec((1024, ATT_SUB * BLK), lambda i: (0, nsteps - 1 - i)),
                  _full((2, BLK, GROUP_LANES)), _full((2, GROUP_LANES)), _full((2, BLK, GROUP_LANES))],
        out_specs=[pl.BlockSpec((ATT_COLS, ATT_SUB * BLK), lambda i: (0, nsteps - 1 - i)),
                   _full((2, BLK, GROUP_LANES)), _full((1, 128))],
        out_shape=[jax.ShapeDtypeStruct((ATT_COLS, s), ACT_DTYPE),
                   jax.ShapeDtypeStruct((2, BLK, GROUP_LANES), F32),
                   jax.ShapeDtypeStruct((1, 128), F32)],
        scratch_shapes=[pltpu.VMEM((256, BLK), F32), pltpu.VMEM((2, GROUP_LANES), F32)],
        compiler_params=_cparams(("arbitrary",)),
    )(pa, pa, dy, bias, sinks, dbias_in)


def _layernorm(v, g, b):
    mu = jnp.mean(v, axis=-1, keepdims=True)
    vc = v - mu
    rstd = lax.rsqrt(jnp.mean(vc * vc, axis=-1, keepdims=True) + EPS)
    xhat = vc * rstd
    return xhat, rstd, xhat * g + b


def sgu_fwd(ps, ln_g, ln_b, w_tril, b_t):
    s = ps.shape[0]
    rows = min(4 * BLK, s)

    def body(ps_ref, g_ref, b_ref, w_ref, bt_ref, y_ref):
        u = ps_ref[:, 0:1024].astype(F32)
        v = ps_ref[:, 1024:2048].astype(F32)
        z = ps_ref[:, 2048:3072].astype(F32)
        _, _, vn = _layernorm(v, g_ref[...], b_ref[...])
        gate = u * z * _sig(z)
        for c in range(rows // BLK):
            ch = slice(c * BLK, (c + 1) * BLK)
            for g in range(SG_GROUPS):
                sl = slice(g * 128, (g + 1) * 128)
                mixed = _dot(w_ref[g], vn[ch, sl]) + bt_ref[:, g:g + 1]
                y_ref[ch, sl] = (gate[ch, sl] * mixed).astype(y_ref.dtype)

    return pl.pallas_call(
        body, name="sgu_fwd", grid=(s // rows,),
        in_specs=[pl.BlockSpec((rows, SG_COLS), lambda c: (c, 0)), _full((1, 1024)), _full((1, 1024)),
                  _full((SG_GROUPS, BLK, BLK)), _full((BLK, 128))],
        out_specs=pl.BlockSpec((rows, 1024), lambda c: (c, 0)),
        out_shape=jax.ShapeDtypeStruct((s, 1024), ACT_DTYPE),
        compiler_params=_cparams(("parallel",)),
    )(ps, ln_g, ln_b, w_tril, b_t)


def sgu_bwd(ps, dy, ln_g, ln_b, w_tril, w_tril_t, b_t):
    s = ps.shape[0]
    rows = min(4 * BLK, s)

    def body(ps_ref, dy_ref, g_ref, b_ref, w_ref, wt_ref, bt_ref, dps_ref, dw_ref, dbt_ref, dg_ref, db_ref, dvn_scr):
        @pl.when(pl.program_id(0) == 0)
        def _():
            dw_ref[...] = jnp.zeros_like(dw_ref)
            dbt_ref[...] = jnp.zeros_like(dbt_ref)
            dg_ref[...] = jnp.zeros_like(dg_ref)
            db_ref[...] = jnp.zeros_like(db_ref)

        u = ps_ref[:, 0:1024].astype(F32)
        v = ps_ref[:, 1024:2048].astype(F32)
        z = ps_ref[:, 2048:3072].astype(F32)
        dy = dy_ref[...].astype(F32)
        xhat, rstd, vn = _layernorm(v, g_ref[...], b_ref[...])
        sz = _sig(z)
        silu = z * sz
        row = lax.broadcasted_iota(jnp.int32, (BLK, BLK), 0)
        colm = lax.broadcasted_iota(jnp.int32, (BLK, BLK), 1)
        tril = row >= colm
        dbt = jnp.zeros((BLK, 128), F32)
        dsilu_z = _dsilu(z, sz)
        for c in range(rows // BLK):
            ch = slice(c * BLK, (c + 1) * BLK)
            for g in range(SG_GROUPS):
                sl = slice(g * 128, (g + 1) * 128)
                vng = vn[ch, sl]
                mixed = _dot(w_ref[g], vng) + bt_ref[:, g:g + 1]
                dyg, ug = dy[ch, sl], u[ch, sl]
                dps_ref[ch, sl] = (dyg * mixed * silu[ch, sl]).astype(dps_ref.dtype)
                dps_ref[ch, 2048 + g * 128:2048 + (g + 1) * 128] = (
                    dyg * ug * mixed * dsilu_z[ch, sl]).astype(dps_ref.dtype)
                dm = dyg * ug * silu[ch, sl]
                dw_ref[g] += jnp.where(tril, _dot_nt(dm, vng), 0.0)
                dbt = dbt + jnp.where(colm == g, jnp.sum(dm, axis=1, keepdims=True), 0.0)
                dvn_scr[ch, sl] = _dot(wt_ref[g], dm)
        dbt_ref[...] += dbt
        dvn = dvn_scr[...]
        dg_ref[...] += jnp.sum(dvn * xhat, axis=0, keepdims=True)
        db_ref[...] += jnp.sum(dvn, axis=0, keepdims=True)
        dxh = dvn * g_ref[...]
        dv = rstd * (dxh - jnp.mean(dxh, axis=-1, keepdims=True)
                     - xhat * jnp.mean(dxh * xhat, axis=-1, keepdims=True))
        dps_ref[:, 1024:2048] = dv.astype(dps_ref.dtype)

    return pl.pallas_call(
        body, name="sgu_bwd", grid=(s // rows,),
        in_specs=[pl.BlockSpec((rows, SG_COLS), lambda c: (c, 0)), pl.BlockSpec((rows, 1024), lambda c: (c, 0)),
                  _full((1, 1024)), _full((1, 1024)), _full((SG_GROUPS, BLK, BLK)), _full((SG_GROUPS, BLK, BLK)),
                  _full((BLK, 128))],
        out_specs=[pl.BlockSpec((rows, SG_COLS), lambda c: (c, 0)), _full((SG_GROUPS, BLK, BLK)), _full((BLK, 128)),
                   _full((1, 1024)), _full((1, 1024))],
        out_shape=[jax.ShapeDtypeStruct((s, SG_COLS), ACT_DTYPE), jax.ShapeDtypeStruct((SG_GROUPS, BLK, BLK), F32),
                   jax.ShapeDtypeStruct((BLK, 128), F32), jax.ShapeDtypeStruct((1, 1024), F32),
                   jax.ShapeDtypeStruct((1, 1024), F32)],
        scratch_shapes=[pltpu.VMEM((rows, 1024), F32)],
        compiler_params=_cparams(("arbitrary",)),
    )(ps, dy, ln_g, ln_b, w_tril, w_tril_t, b_t)


def _shift_down(cur, prev16, k):
    if k == 0:
        return cur
    r = pltpu.roll(cur, k, 0)
    rp = pltpu.roll(prev16, k, 0)
    row = lax.broadcasted_iota(jnp.int32, (8, cur.shape[1]), 0)
    return jnp.concatenate([jnp.where(row < k, rp[0:8], r[0:8]), r[8:]], axis=0)


def _shift_up(cur, next16, k):
    if k == 0:
        return cur
    n = cur.shape[0]
    r = pltpu.roll(cur, n - k, 0)
    rn = pltpu.roll(next16, 16 - k, 0)
    row = lax.broadcasted_iota(jnp.int32, (8, cur.shape[1]), 0)
    return jnp.concatenate([r[:n - 8], jnp.where(row >= 8 - k, rn[8:16], r[n - 8:])], axis=0)


def _bcast8(v):
    return jnp.broadcast_to(v, (16, v.shape[1]))


def _causal_conv(xbc, prev16, cw, cbias):
    pre = cbias + cw[3:4] * xbc
    for k in (1, 2, 3):
        pre = pre + cw[3 - k:4 - k] * _shift_down(xbc, prev16, k)
    return pre


class _Ssd:
    def __init__(self, pre, dtr, dtb, alog, dsk, tri, e):
        self.pre = pre
        self.sg = _sig(pre)
        act = pre * self.sg
        self.xs = act[:, 0:SSM_WIDTH]
        self.bm = act[:, SSM_WIDTH:SSM_WIDTH + 512]
        self.cm = act[:, SSM_WIDTH + 512:CONV_DIM]
        self.dtp = dtr + dtb
        self.dt = jnp.maximum(self.dtp, 0.0) + jnp.log(1.0 + jnp.exp(-jnp.abs(self.dtp)))
        self.a = -jnp.exp(alog)
        self.acs = _dot_hi(tri, self.dt * self.a)
        self.acs_t = self.acs.T
        tot = self.acs[BLK - 1:BLK]
        self.ecs = jnp.exp(self.acs)
        self.dte = jnp.exp(tot - self.acs)
        self.cd = jnp.exp(tot)
        self.dt_x = _dot_onehot(self.dt, e)
        self.ecs_x = _dot_onehot(self.ecs, e)
        self.dte_x = _dot_onehot(self.dte, e)
        self.cd_x = _dot_onehot(_bcast8(self.cd), e)[0:1]
        self.d_x = _dot_onehot(_bcast8(dsk), e)[0:1]
        self.xdt = self.xs * self.dt_x
        row = lax.broadcasted_iota(jnp.int32, (BLK, BLK), 0)
        col = lax.broadcasted_iota(jnp.int32, (BLK, BLK), 1)
        self.tril = row >= col

    def group(self, g):
        sl = slice(g * 128, (g + 1) * 128)
        bg, cg = self.bm[:, sl], self.cm[:, sl]
        return bg, cg, _dot_nt(cg, bg)

    def decay(self, h):
        seg = self.acs[:, h:h + 1] - self.acs_t[h:h + 1, :]
        return jnp.exp(jnp.where(self.tril, seg, NEG))

    def y_pre_gate(self, ht_of, yd_scr, yoff_scr):
        for g in range(SSM_GROUPS):
            bg, cg, cb = self.group(g)
            for j in range(8):
                h = g * 8 + j
                sl = slice(h * 64, (h + 1) * 64)
                yd_scr[:, sl] = _dot(cb * self.decay(h), self.xdt[:, sl])
            gs = slice(g * SSM_GW, (g + 1) * SSM_GW)
            yoff_scr[:, gs] = _dot(cg, ht_of(g)) * self.ecs_x[:, gs]
        return yd_scr[...] + yoff_scr[...] + self.d_x * self.xs


def _ssd_consts():
    hh = lax.broadcasted_iota(jnp.int32, (128, SSM_WIDTH), 0)
    ch = lax.broadcasted_iota(jnp.int32, (128, SSM_WIDTH), 1)
    e = (ch // 64 == hh).astype(jnp.bfloat16)
    row = lax.broadcasted_iota(jnp.int32, (BLK, BLK), 0)
    col = lax.broadcasted_iota(jnp.int32, (BLK, BLK), 1)
    tri = (row >= col).astype(F32)
    return tri, e


def _pad_lanes(v, n=128):
    return jnp.pad(v, ((0, 0), (0, n - v.shape[1])))


def ssd_fwd(pm, cw, cbias, dtb, alog, dsk, ng, ex=None):
    s = pm.shape[0]
    nc = s // BLK
    tri, e = _ssd_consts()

    def body(pm_ref, prev_ref, cw_ref, cb_ref, dtb_ref, al_ref, d_ref, ng_ref, tri_ref, e_ref,
             y_ref, st_ref, pre_ref, ht_ref, yd_scr, yoff_scr):
        c = pl.program_id(0)

        @pl.when(c == 0)
        def _():
            ht_ref[...] = jnp.zeros_like(ht_ref)

        xbc = pm_ref[:, 0:CONV_DIM].astype(F32)
        prev16 = jnp.where(c == 0, 0.0, prev_ref[...].astype(F32))
        pre = _causal_conv(xbc, prev16, cw_ref[...], cb_ref[...])
        pre_ref[...] = pre.astype(pre_ref.dtype)
        f = _Ssd(pre, pm_ref[:, DT_OFF:DT_OFF + 128].astype(F32), dtb_ref[...], al_ref[...], d_ref[...],
                 tri_ref[...], e_ref[...])
        st_ref[0] = ht_ref[...]
        y = f.y_pre_gate(lambda g: ht_ref[g], yd_scr, yoff_scr)
        for g in range(SSM_GROUPS):
            bg, _, _ = f.group(g)
            gs = slice(g * SSM_GW, (g + 1) * SSM_GW)
            ht_ref[g] = ht_ref[g] * f.cd_x[:, gs] + _dot_tn(bg, f.xdt[:, gs] * f.dte_x[:, gs])
        z = pm_ref[:, CONV_DIM:CONV_DIM + SSM_WIDTH].astype(F32)
        ypre = y * z * _sig(z)
        for g in range(SSM_GROUPS):
            gs = slice(g * SSM_GW, (g + 1) * SSM_GW)
            yg = ypre[:, gs]
            rr = lax.rsqrt(jnp.mean(yg * yg, axis=-1, keepdims=True) + EPS)
            y_ref[:, gs] = (yg * rr * ng_ref[:, gs]).astype(y_ref.dtype)

    own, hosted = _call_hosting(
        body, "ssd_fwd", nc,
        in_specs=[pl.BlockSpec((BLK, SSM_COLS), lambda c: (c, 0)),
                  pl.BlockSpec((16, CONV_DIM), lambda c: (jnp.maximum(8 * c - 1, 0), 0)),
                  _full((4, CONV_DIM)), _full((1, CONV_DIM)), _full((1, 128)), _full((1, 128)), _full((1, 128)),
                  _full((1, SSM_WIDTH)), _full((BLK, BLK)), _full((128, SSM_WIDTH))],
        out_specs=[pl.BlockSpec((BLK, SSM_WIDTH), lambda c: (c, 0)),
                   pl.BlockSpec((1, SSM_GROUPS, 128, SSM_GW), lambda c: (c, 0, 0, 0)),
                   pl.BlockSpec((BLK, CONV_DIM), lambda c: (c, 0))],
        out_shape=[jax.ShapeDtypeStruct((s, SSM_WIDTH), ACT_DTYPE),
                   jax.ShapeDtypeStruct((nc, SSM_GROUPS, 128, SSM_GW), F32),
                   jax.ShapeDtypeStruct((s, CONV_DIM), ACT_DTYPE)],
        scratch=[pltpu.VMEM((SSM_GROUPS, 128, SSM_GW), F32), pltpu.VMEM((BLK, SSM_WIDTH), F32),
                 pltpu.VMEM((BLK, SSM_WIDTH), F32)],
        args=(pm, pm, cw, cbias, dtb, alog, dsk, ng, tri, e), ex=ex)
    return (*own, hosted) if ex is not None else tuple(own)


def ssd_bwd(pm, pre, dy, states, cw, dtb, alog, dsk, ng, ex=None):
    s = pm.shape[0]
    nc = s // BLK
    tri, e = _ssd_consts()
    tri_t, e_t = tri.T, e.T

    def body(pm_ref, pre_ref, dy_ref, st_ref, cw_ref, dtb_ref, al_ref, d_ref, ng_ref,
             tri_ref, trit_ref, e_ref, et_ref,
             dpm_ref, dcw_ref, dcb_ref, dvec_ref, dng_ref,
             dht_ref, dcar_ref, yd_scr, yoff_scr, dx_scr, r2_scr, hs_scr, da_scr, dat_scr, dd_scr, dbc_scr):
        i = pl.program_id(0)
        n = nc - 1 - i

        @pl.when(i == 0)
        def _():
            dht_ref[...] = jnp.zeros_like(dht_ref)
            dcar_ref[...] = jnp.zeros_like(dcar_ref)
            dcw_ref[...] = jnp.zeros_like(dcw_ref)
            dcb_ref[...] = jnp.zeros_like(dcb_ref)
            dvec_ref[...] = jnp.zeros_like(dvec_ref)
            dng_ref[...] = jnp.zeros_like(dng_ref)
            dd_scr[...] = jnp.zeros_like(dd_scr)
            da_scr[...] = jnp.zeros_like(da_scr)
            dat_scr[...] = jnp.zeros_like(dat_scr)

        cw = cw_ref[...]
        f = _Ssd(pre_ref[...].astype(F32), pm_ref[:, DT_OFF:DT_OFF + 128].astype(F32), dtb_ref[...], al_ref[...],
                 d_ref[...], tri_ref[...], e_ref[...])
        et = et_ref[...]
        y = f.y_pre_gate(lambda g: st_ref[0, g], yd_scr, yoff_scr)

        z = pm_ref[:, CONV_DIM:CONV_DIM + SSM_WIDTH].astype(F32)
        dyv = dy_ref[...].astype(F32)
        sz = _sig(z)
        silu = z * sz
        ypre = y * silu
        for g in range(SSM_GROUPS):
            gs = slice(g * SSM_GW, (g + 1) * SSM_GW)
            yg = ypre[:, gs]
            rr = lax.rsqrt(jnp.mean(yg * yg, axis=-1, keepdims=True) + EPS)
            nrm = yg * rr
            dng_ref[:, gs] += jnp.sum(dyv[:, gs] * nrm, axis=0, keepdims=True)
            dn = dyv[:, gs] * ng_ref[:, gs]
            dx_scr[:, gs] = rr * (dn - nrm * jnp.mean(dn * nrm, axis=-1, keepdims=True))
        dypre = dx_scr[...]
        d_y = dypre * silu
        dpm_ref[:, CONV_DIM:CONV_DIM + SSM_WIDTH] = (dypre * y * _dsilu(z, sz)).astype(dpm_ref.dtype)

        for g in range(SSM_GROUPS):
            bg, cg, cb = f.group(g)
            gs = slice(g * SSM_GW, (g + 1) * SSM_GW)
            htg = st_ref[0, g]
            dhn = dht_ref[g]
            dcb = jnp.zeros((BLK, BLK), F32)
            for j in range(8):
                h = g * 8 + j
                sl = slice(h * 64, (h + 1) * 64)
                dec = f.decay(h)
                dyh = d_y[:, sl]
                dmd = _dot_nt(dyh, f.xdt[:, sl]) * dec
                dcb = dcb + dmd
                gm = dmd * cb
                da_scr[:, h:h + 1] = jnp.sum(gm, axis=1, keepdims=True)
                dat_scr[h:h + 1, :] = jnp.sum(gm, axis=0, keepdims=True)
                dx_scr[:, sl] = _dot_tn(cb * dec, dyh)
            dz = f.ecs_x[:, gs] * d_y[:, gs]
            dbc_scr[:, 512 + g * 128:512 + (g + 1) * 128] = _dot(dcb, bg) + _dot_nt(dz, htg)
            dbc_scr[:, g * 128:(g + 1) * 128] = _dot_tn(dcb, cg) + _dot_nt(f.xdt[:, gs] * f.dte_x[:, gs], dhn)
            dws = _dot(bg, dhn)
            dx_scr[:, gs] += f.dte_x[:, gs] * dws
            r2_scr[:, gs] = dws * f.xdt[:, gs]
            hs_scr[:, gs] = _bcast8(jnp.sum(dhn * htg, axis=0, keepdims=True))
            dht_ref[g] = f.cd_x[:, gs] * dhn + _dot_tn(cg, dz)
        d_x = dx_scr[...]
        r1 = _dot(d_y * yoff_scr[...], et)
        r2 = _dot(r2_scr[...], et) * f.dte
        dcd = _dot_onehot(hs_scr[...], et)[0:1]
        d_tot = jnp.sum(r2, axis=0, keepdims=True) + dcd * f.cd
        row = lax.broadcasted_iota(jnp.int32, (BLK, 128), 0)
        d_a = da_scr[...] - dat_scr[...].T + r1 - r2 + jnp.where(row == BLK - 1, d_tot, 0.0)
        dadt = _dot_hi(trit_ref[...], d_a)
        ddt = dadt * f.a + _dot(d_x * f.xs, et)
        lane = lax.broadcasted_iota(jnp.int32, (BLK, 128), 1)
        dr = jnp.where(lane < SSM_HEADS, ddt * _sig(f.dtp), 0.0)
        dvec_ref[0:1, :] += jnp.sum(dr, axis=0, keepdims=True)
        dvec_ref[1:2, :] += jnp.sum(dadt * f.dt, axis=0, keepdims=True) * f.a
        dd_scr[...] += _bcast8(jnp.sum(d_y * f.xs, axis=0, keepdims=True))
        dpm_ref[:, DT_OFF:DT_OFF + 128] = dr.astype(dpm_ref.dtype)
        dpm_ref[:, DT_OFF + 128:SSM_COLS] = jnp.zeros((BLK, 128), dpm_ref.dtype)

        dxs = d_x * f.dt_x + f.d_x * d_y
        dact = jnp.concatenate([dxs, dbc_scr[...]], axis=1)
        dpre = dact * _dsilu(f.pre, f.sg)
        dcb_ref[...] += jnp.sum(dpre, axis=0, keepdims=True)
        xbc = pm_ref[:, 0:CONV_DIM].astype(F32)
        dxraw = jnp.zeros((BLK, CONV_DIM), F32)
        nxt = dcar_ref[...]
        for k in range(4):
            ahead = _shift_up(dpre, nxt, k)
            dcw_ref[3 - k:4 - k, :] += jnp.sum(ahead * xbc, axis=0, keepdims=True)
            dxraw = dxraw + cw[3 - k:4 - k] * ahead
        dcar_ref[...] = dpre[0:16]
        dpm_ref[:, 0:CONV_DIM] = dxraw.astype(dpm_ref.dtype)

        @pl.when(i == nc - 1)
        def _():
            dvec_ref[2:3, :] = _dot_onehot(dd_scr[...], et)[0:1]

    own, hosted = _call_hosting(
        body, "ssd_bwd", nc,
        in_specs=[pl.BlockSpec((BLK, SSM_COLS), lambda i: (nc - 1 - i, 0)),
                  pl.BlockSpec((BLK, CONV_DIM), lambda i: (nc - 1 - i, 0)),
                  pl.BlockSpec((BLK, SSM_WIDTH), lambda i: (nc - 1 - i, 0)),
                  pl.BlockSpec((1, SSM_GROUPS, 128, SSM_GW), lambda i: (nc - 1 - i, 0, 0, 0)),
                  _full((4, CONV_DIM)), _full((1, 128)), _full((1, 128)), _full((1, 128)),
                  _full((1, SSM_WIDTH)), _full((BLK, BLK)), _full((BLK, BLK)), _full((128, SSM_WIDTH)),
                  _full((SSM_WIDTH, 128))],
        out_specs=[pl.BlockSpec((BLK, SSM_COLS), lambda i: (nc - 1 - i, 0)),
                   _full((8, CONV_DIM)), _full((1, CONV_DIM)), _full((8, 128)), _full((1, SSM_WIDTH))],
        out_shape=[jax.ShapeDtypeStruct((s, SSM_COLS), ACT_DTYPE), jax.ShapeDtypeStruct((8, CONV_DIM), F32),
                   jax.ShapeDtypeStruct((1, CONV_DIM), F32), jax.ShapeDtypeStruct((8, 128), F32),
                   jax.ShapeDtypeStruct((1, SSM_WIDTH), F32)],
        scratch=[pltpu.VMEM((SSM_GROUPS, 128, SSM_GW), F32), pltpu.VMEM((16, CONV_DIM), F32),
                 pltpu.VMEM((BLK, SSM_WIDTH), F32), pltpu.VMEM((BLK, SSM_WIDTH), F32),
                 pltpu.VMEM((BLK, SSM_WIDTH), F32), pltpu.VMEM((BLK, SSM_WIDTH), F32),
                 pltpu.VMEM((16, SSM_WIDTH), F32), pltpu.VMEM((BLK, 128), F32), pltpu.VMEM((128, BLK), F32),
                 pltpu.VMEM((16, SSM_WIDTH), F32), pltpu.VMEM((BLK, 1024), F32)],
        args=(pm, pre, dy, states, cw, dtb, alog, dsk, ng, tri, tri_t, e, e_t), ex=ex)
    return (*own, hosted) if ex is not None else tuple(own)


def merge_fwd(x, ya, ys, ym, pg, wa, ws, wm, wo, g_post, target=None):
    s, d = x.shape
    tm = min(256, s)
    with_loss = target is not None

    def body(*refs):
        x_ref, ya_ref, ys_ref, ym_ref, pg_ref, wa_ref, ws_ref, wm_ref, wo_ref, g_ref = refs[:10]
        if with_loss:
            t_ref, xo_ref, l_ref, ba_ref, bs_ref, bm_ref, mg_ref, out_ref = refs[10:]
        else:
            xo_ref, ba_ref, bs_ref, bm_ref, mg_ref, out_ref = refs[10:]
        ba = _dot_tn(ya_ref[...], wa_ref[...])
        bs = _dot(ys_ref[...], ws_ref[...])
        bm = _dot(ym_ref[...], wm_ref[...])
        merged = (_sig(pg_ref[:, 0:d].astype(F32)) * ba + _sig(pg_ref[:, d:2 * d].astype(F32)) * bs
                  + _sig(pg_ref[:, 2 * d:3 * d].astype(F32)) * bm)
        out = _dot(merged, wo_ref[...])
        r = lax.rsqrt(jnp.mean(out * out, axis=-1, keepdims=True) + EPS)
        y = x_ref[...] + out * r * g_ref[...]
        if with_loss:
            @pl.when(pl.program_id(0) == 0)
            def _():
                l_ref[...] = jnp.zeros_like(l_ref)

            err = y - t_ref[...]
            xo_ref[...] = err * (1.0 / d)
            part = jnp.sum(jnp.sum(err * err, axis=-1, keepdims=True) * (1.0 / d), axis=0, keepdims=True)
            l_ref[...] += 0.5 * jnp.broadcast_to(part, l_ref.shape)
        else:
            xo_ref[...] = y
        ba_ref[...] = ba.astype(ba_ref.dtype)
        bs_ref[...] = bs.astype(bs_ref.dtype)
        bm_ref[...] = bm.astype(bm_ref.dtype)
        mg_ref[...] = merged.astype(mg_ref.dtype)
        out_ref[...] = out.astype(out_ref.dtype)

    rows = lambda w: pl.BlockSpec((tm, w), lambda i: (i, 0))
    act = jax.ShapeDtypeStruct((s, d), ACT_DTYPE)
    loss_spec = [_full((8, 128))] if with_loss else []
    loss_shape = [jax.ShapeDtypeStruct((8, 128), F32)] if with_loss else []
    return pl.pallas_call(
        body, name="merge_fwd_loss" if with_loss else "merge_fwd", grid=(s // tm,),
        in_specs=[rows(d), pl.BlockSpec((d, tm), lambda i: (0, i)), rows(d), rows(2 * d), rows(3 * d), _full((d, d)),
                  _full((d, d)), _full((2 * d, d)), _full((d, d)), _full((1, d))] + ([rows(d)] if with_loss else []),
        out_specs=[rows(d)] + loss_spec + [rows(d)] * 5,
        out_shape=[jax.ShapeDtypeStruct((s, d), F32)] + loss_shape + [act] * 5,
        compiler_params=_cparams(("arbitrary" if with_loss else "parallel",)),
    )(x, ya, ys, ym, pg, wa, ws, wm, wo, g_post, *([target] if with_loss else []))


def merge_bwd(dx, out_s, pg, ba, bs, bm, wa, ws, wm, wo, g_post):
    s, d = dx.shape
    tm = min(256, s)

    def body(dx_ref, out_ref, pg_ref, ba_ref, bs_ref, bm_ref, wa_ref, ws_ref, wm_ref, wo_ref, g_ref,
             dout_ref, dba_ref, dbs_ref, dbm_ref, dpg_ref, dya_ref, dys_ref, dym_ref, dg_ref):
        @pl.when(pl.program_id(0) == 0)
        def _():
            dg_ref[...] = jnp.zeros_like(dg_ref)

        o = out_ref[...].astype(F32)
        dxv = dx_ref[...]
        r = lax.rsqrt(jnp.mean(o * o, axis=-1, keepdims=True) + EPS)
        nrm = o * r
        dg_ref[...] += jnp.sum(dxv * nrm, axis=0, keepdims=True)
        dn = dxv * g_ref[...]
        dout = r * (dn - nrm * jnp.mean(dn * nrm, axis=-1, keepdims=True))
        dout_ref[...] = dout.astype(dout_ref.dtype)
        dmerged = _dot_nt(dout, wo_ref[...])
        for q, (b_ref, db_ref, w_ref, dy_ref) in enumerate(((ba_ref, dba_ref, wa_ref, dya_ref),
                                                            (bs_ref, dbs_ref, ws_ref, dys_ref),
                                                            (bm_ref, dbm_ref, wm_ref, dym_ref))):
            gt = _sig(pg_ref[:, q * d:(q + 1) * d].astype(F32))
            db = dmerged * gt
            db_ref[...] = db.astype(db_ref.dtype)
            dpg_ref[:, q * d:(q + 1) * d] = (dmerged * b_ref[...].astype(F32) * gt * (1.0 - gt)).astype(dpg_ref.dtype)
            if q == 0:
                dy_ref[...] = _dot_nt(w_ref[...], db).astype(dy_ref.dtype)
            else:
                dy_ref[...] = _dot_nt(db, w_ref[...]).astype(dy_ref.dtype)

    rows = lambda w: pl.BlockSpec((tm, w), lambda i: (i, 0))
    act = lambda w: jax.ShapeDtypeStruct((s, w), ACT_DTYPE)
    return pl.pallas_call(
        body, name="merge_bwd", grid=(s // tm,),
        in_specs=[rows(d), rows(d), rows(3 * d), rows(d), rows(d), rows(d), _full((d, d)), _full((d, d)),
                  _full((2 * d, d)), _full((d, d)), _full((1, d))],
        out_specs=[rows(d), rows(d), rows(d), rows(d), rows(3 * d), pl.BlockSpec((d, tm), lambda i: (0, i)), rows(d),
                   rows(2 * d), _full((1, d))],
        out_shape=[act(d), act(d), act(d), act(d), act(3 * d), jax.ShapeDtypeStruct((d, s), ACT_DTYPE), act(d),
                   act(2 * d), jax.ShapeDtypeStruct((1, d), F32)],
        compiler_params=_cparams(("arbitrary",)),
    )(dx, out_s, pg, ba, bs, bm, wa, ws, wm, wo, g_post)


def _mesh_pos():
    x, y, c = lax.axis_index("x"), lax.axis_index("y"), lax.axis_index("c")
    return x, y, c, 4 * x + 2 * y + c


def _peer(x, y, c, k):
    px = 1 - x if k & 4 else x
    py = 1 - y if k & 2 else y
    pc = 1 - c if k & 1 else c
    return (px, py, pc), 4 * px + 2 * py + pc


class Exchange:
    SAME_CORE = (2, 4, 6)

    def __init__(self, scattered, gathered):
        self.ns = len(scattered)
        self.arrays = list(scattered) + list(gathered)
        self.na = len(self.arrays)
        any_spec = pl.BlockSpec(memory_space=pl.ANY)
        self.in_specs = [any_spec] * self.na
        self.out_specs = [any_spec] * self.na
        self.out_shape = ([jax.ShapeDtypeStruct(a.shape, a.dtype) for a in scattered]
                          + [jax.ShapeDtypeStruct((N_DEV,) + a.shape, a.dtype) for a in gathered])
        self.scratch = [pltpu.SemaphoreType.DMA((self.na, N_DEV - 1)), pltpu.SemaphoreType.DMA((self.na, N_DEV - 1)),
                        pltpu.SemaphoreType.DMA((self.na,))]

    def _src(self, ins, q, slot):
        return ins[q].at[slot] if q < self.ns else ins[q]

    def _local(self, ins, outs, sems):
        me = _mesh_pos()[3]
        return [pltpu.make_async_copy(self._src(ins, q, me), outs[q].at[me], sems[2].at[q]) for q in range(self.na)]

    def _direct(self, ins, outs, sems, relations, arrays):
        x, y, c, me = _mesh_pos()
        copies = []
        for k in relations:
            peer, pidx = _peer(x, y, c, k)
            for q in arrays:
                copies.append(pltpu.make_async_remote_copy(
                    src_ref=self._src(ins, q, pidx), dst_ref=outs[q].at[me], send_sem=sems[0].at[q, k - 1],
                    recv_sem=sems[1].at[q, k - 1], device_id=peer, device_id_type=MESH))
        return copies

    def _arrivals(self, ins, outs, sems, relations, arrays):
        x, y, c, _ = _mesh_pos()
        copies = []
        for k in relations:
            peer, pidx = _peer(x, y, c, k)
            for q in arrays:
                copies.append(pltpu.make_async_remote_copy(
                    src_ref=self._src(ins, q, pidx), dst_ref=outs[q].at[pidx], send_sem=sems[0].at[q, k - 1],
                    recv_sem=sems[1].at[q, k - 1], device_id=peer, device_id_type=MESH))
        return copies

    def _relays(self, outs, sems):
        x, y, c, _ = _mesh_pos()
        sibling, _ = _peer(x, y, c, 1)
        copies = []
        for k in self.SAME_CORE:
            _, pidx = _peer(x, y, c, k)
            for q in range(self.ns, self.na):
                copies.append(pltpu.make_async_remote_copy(
                    src_ref=outs[q].at[pidx], dst_ref=outs[q].at[pidx], send_sem=sems[0].at[q, k],
                    recv_sem=sems[1].at[q, k], device_id=sibling, device_id_type=MESH))
        return copies

    def _sends(self, ins, outs, sems):
        return (self._direct(ins, outs, sems, range(1, N_DEV), range(self.ns))
                + self._direct(ins, outs, sems, (1,) + self.SAME_CORE, range(self.ns, self.na)))

    def start(self, ins, outs, sems):
        for cp in self._local(ins, outs, sems) + self._sends(ins, outs, sems):
            cp.start()

    def relay(self, ins, outs, sems):
        for cp in self._arrivals(ins, outs, sems, self.SAME_CORE, range(self.ns, self.na)):
            cp.wait_recv()
        for cp in self._relays(outs, sems):
            cp.start()

    def wait(self, ins, outs, sems):
        for cp in (self._arrivals(ins, outs, sems, range(1, N_DEV), range(self.ns))
                   + self._arrivals(ins, outs, sems, (1, 3, 5, 7), range(self.ns, self.na))):
            cp.wait_recv()
        for cp in self._sends(ins, outs, sems) + self._relays(outs, sems):
            cp.wait_send()
        for cp in self._local(ins, outs, sems):
            cp.wait()


def adamw(parts_list, w, m, v, tile, name, ex=None):
    npart, _, dp = parts_list[0].shape
    d = w.shape[-1]
    counts = [p.shape[1] // tile for p in parts_list]
    starts = [sum(counts[:q]) for q in range(len(counts))]
    n_lists = len(parts_list)

    def body(*refs):
        p_refs = refs[:n_lists]
        w_ref, m_ref, v_ref, g_ref, dw_ref, nm_ref, nv_ref = refs[n_lists:]
        i = pl.program_id(0)
        for q, p_ref in enumerate(p_refs):
            @pl.when((i >= starts[q]) & (i < starts[q] + counts[q]))
            def _(p_ref=p_ref):
                acc = p_ref[0, :, 0:d].astype(F32)
                for k in range(1, npart):
                    acc = acc + p_ref[k, :, 0:d].astype(F32)
                g_ref[...] = acc

        g = g_ref[...]
        nm = ADAM_B1 * m_ref[...] + (1.0 - ADAM_B1) * g
        nv = ADAM_B2 * v_ref[...] + (1.0 - ADAM_B2) * (g * g)
        nm_ref[...] = nm
        nv_ref[...] = nv
        m_hat = nm / (1.0 - ADAM_B1 ** ADAM_STEP)
        v_hat = nv / (1.0 - ADAM_B2 ** ADAM_STEP)
        dw_ref[...] = -ADAM_LR * (m_hat / (jnp.sqrt(v_hat) + ADAM_EPS) + ADAM_WD * w_ref[...])

    def part_rows(q):
        return lambda i: (0, jnp.clip(i - starts[q], 0, counts[q] - 1), 0)

    if w.ndim == 3:
        rows = pl.BlockSpec((None, tile, d), lambda i: (i // counts[0], i % counts[0], 0))
    else:
        rows = pl.BlockSpec((tile, d), lambda i: (i, 0))
    own, hosted = _call_hosting(
        body, name, sum(counts),
        in_specs=[pl.BlockSpec((npart, tile, dp), part_rows(q)) for q in range(n_lists)] + [rows, rows, rows],
        out_specs=[rows] * 4, out_shape=[jax.ShapeDtypeStruct(w.shape, F32)] * 4, scratch=[],
        args=(*parts_list, w, m, v), ex=ex)
    return (*own, hosted) if ex is not None else tuple(own)


def _pad_rows(a, rows):
    return jnp.pad(a, ((0, rows - a.shape[0]), (0, 0)))


def _pack_rest(w_att, w_sg, w_ssm, w_out):
    parts = []
    for l in range(2):
        parts += [w_att[l], w_sg[l], w_ssm[l], w_out[l]]
    return jnp.concatenate(parts, axis=0)


def _unpack_rest(p):
    outs = [[], [], [], []]
    o = 0
    for l in range(2):
        for q, rws in enumerate(REST_PARTS):
            outs[q].append(p[o:o + rws])
            o += rws
    return [jnp.stack(t) for t in outs]


def _pack_win(w_in):
    return jnp.pad(w_in.reshape(2 * D_MODEL, WIN_SHARD), ((0, 0), (0, WIN_LANES - WIN_SHARD)))


W_IN_MAP = ((0, 1024, "att", 0), (1024, 1280, "att", 2048), (1280, 2304, "att", 1024), (2304, 5376, "sg", 0),
            (5376, 7424, "ssm", 3072), (7424, 10496, "ssm", 0), (10496, 10528, "ssm", 5120), (10528, 13600, "gate", 0))
SLAB_COLS = {"att": ATT_COLS, "sg": SG_COLS, "ssm": SSM_COLS, "gate": GATE_COLS}


def _slab_pieces(name):
    pieces, filled = [], 0
    for ga, gb, _, off in sorted((m for m in W_IN_MAP if m[2] == name), key=lambda m: m[3]):
        assert off == filled
        a = ga
        while a < gb:
            d = a // WIN_SHARD
            hi = min(gb, WIN_SHARD * (d + 1))
            pieces.append((d, a - WIN_SHARD * d, hi - WIN_SHARD * d))
            a = hi
        filled += gb - ga
    return pieces, filled


def _slabs_from_shards(g):
    names = tuple(SLAB_COLS)
    tr = 256

    def body(g_ref, *out_refs):
        for name, o_ref in zip(names, out_refs):
            pieces, filled = _slab_pieces(name)
            cols = [g_ref[d, :, lo:hi].astype(F32) for d, lo, hi in pieces]
            if filled < SLAB_COLS[name]:
                cols.append(jnp.zeros((tr, SLAB_COLS[name] - filled), F32))
            o_ref[...] = jnp.concatenate(cols, axis=1).astype(o_ref.dtype)

    outs = pl.pallas_call(
        body, name="slabs_from_shards", grid=(D_MODEL // tr,),
        in_specs=[pl.BlockSpec((N_DEV, tr, WIN_LANES), lambda i: (0, i, 0))],
        out_specs=[pl.BlockSpec((tr, SLAB_COLS[n]), lambda i: (i, 0)) for n in names],
        out_shape=[jax.ShapeDtypeStruct((D_MODEL, SLAB_COLS[n]), MXU_DTYPE) for n in names],
        compiler_params=_cparams(("parallel",)),
    )(g)
    return dict(zip(names, outs))


def _shards_from_slabs(dslabs):
    names = tuple(SLAB_COLS)
    tr = 256

    def body(*refs):
        in_refs, o_ref = dict(zip(names, refs[:-1])), refs[-1]
        for d in range(N_DEV):
            a, b = WIN_SHARD * d, WIN_SHARD * (d + 1)
            cols = []
            for ga, gb, name, off in W_IN_MAP:
                lo, hi = max(a, ga), min(b, gb)
                if lo < hi:
                    cols.append(in_refs[name][:, off + lo - ga:off + hi - ga].astype(F32))
            cols.append(jnp.zeros((tr, WIN_LANES - WIN_SHARD), F32))
            o_ref[d] = jnp.concatenate(cols, axis=1).astype(o_ref.dtype)

    return pl.pallas_call(
        body, name="shards_from_slabs", grid=(D_MODEL // tr,),
        in_specs=[pl.BlockSpec((tr, SLAB_COLS[n]), lambda i: (i, 0)) for n in names],
        out_specs=pl.BlockSpec((N_DEV, tr, WIN_LANES), lambda i: (0, i, 0)),
        out_shape=jax.ShapeDtypeStruct((N_DEV, D_MODEL, WIN_LANES), WIRE_DTYPE),
        compiler_params=_cparams(("parallel",)),
    )(*[dslabs[n] for n in names])


SMALL_SIZES = (("norm_pre", 2048), ("norm_post", 2048), ("rel_bias", 512), ("att_sinks", 32), ("sg_ln_g", 2048),
               ("sg_ln_b", 2048), ("sg_w", 262144), ("sg_b", 2048), ("ssm_conv_b", 6144), ("ssm_dt_bias", 64),
               ("ssm_a_log", 64), ("ssm_d", 64), ("ssm_norm_g", 4096), ("conv_w_full", 24576))


def _pack_small(d):
    parts = []
    for name, size in SMALL_SIZES:
        rows = 8 * (-(-size // (8 * D_MODEL)))
        flat = d[name].reshape(-1) if name in d else jnp.zeros((size,), F32)
        parts.append(jnp.pad(flat, (0, rows * D_MODEL - size)).reshape(rows, D_MODEL))
    return _pad_rows(jnp.concatenate(parts, axis=0), SMALL_ROWS)


def _unpack_small(p, shapes):
    out, o = {}, 0
    for name, size in SMALL_SIZES:
        rows = 8 * (-(-size // (8 * D_MODEL)))
        if name in shapes:
            out[name] = p[o:o + rows].reshape(-1)[:size].reshape(shapes[name])
        o += rows
    return out


def _bucket_onehot_t():
    qi = jnp.arange(BLK, dtype=jnp.int32)[None, :]
    kj = jnp.arange(BLK, dtype=jnp.int32)[:, None]
    dd = (qi - kj) & (BLK - 1)
    in_window = dd >= 0
    max_exact = REL_BUCKETS // 2
    dist_f = jnp.maximum(dd, 1).astype(F32)
    large = max_exact + (jnp.log(dist_f / max_exact) / math.log(128 / max_exact)
                         * (REL_BUCKETS - max_exact)).astype(jnp.int32)
    large = jnp.minimum(large, REL_BUCKETS - 1)
    bucket = jnp.where(dd < max_exact, dd, large).reshape(1, -1)
    onehot_t = (bucket == jnp.arange(REL_BUCKETS, dtype=jnp.int32)[:, None]).astype(F32)
    maskadd = jnp.where(in_window, 0.0, NEG).astype(F32).reshape(1, -1)
    return onehot_t, maskadd


WEIGHTS = ['w_in', 'norm_pre', 'norm_post', 'rel_bias', 'att_sinks', 'sg_ln_g', 'sg_ln_b', 'sg_w', 'sg_b',
           'ssm_conv_w', 'ssm_conv_b', 'ssm_dt_bias', 'ssm_a_log', 'ssm_d', 'ssm_norm_g',
           'w_br_att', 'w_br_sg', 'w_br_ssm', 'w_out']
REST = ('w_br_att', 'w_br_sg', 'w_br_ssm', 'w_out')


def kernel(x, w_in, norm_pre, norm_post, rel_bias, att_sinks, sg_ln_g, sg_ln_b, sg_w, sg_b, ssm_conv_w, ssm_conv_b, ssm_dt_bias, ssm_a_log, ssm_d, ssm_norm_g, w_br_att, w_br_sg, w_br_ssm, w_out, loss_target, m_w_in, m_norm_pre, m_norm_post, m_rel_bias, m_att_sinks, m_sg_ln_g, m_sg_ln_b, m_sg_w, m_sg_b, m_ssm_conv_w, m_ssm_conv_b, m_ssm_dt_bias, m_ssm_a_log, m_ssm_d, m_ssm_norm_g, m_w_br_att, m_w_br_sg, m_w_br_ssm, m_w_out, v_w_in, v_norm_pre, v_norm_post, v_rel_bias, v_att_sinks, v_sg_ln_g, v_sg_ln_b, v_sg_w, v_sg_b, v_ssm_conv_w, v_ssm_conv_b, v_ssm_dt_bias, v_ssm_a_log, v_ssm_d, v_ssm_norm_g, v_w_br_att, v_w_br_sg, v_w_br_ssm, v_w_out):
    w = dict(w_in=w_in, norm_pre=norm_pre, norm_post=norm_post, rel_bias=rel_bias, att_sinks=att_sinks,
             sg_ln_g=sg_ln_g, sg_ln_b=sg_ln_b, sg_w=sg_w, sg_b=sg_b, ssm_conv_w=ssm_conv_w, ssm_conv_b=ssm_conv_b,
             ssm_dt_bias=ssm_dt_bias, ssm_a_log=ssm_a_log, ssm_d=ssm_d, ssm_norm_g=ssm_norm_g,
             w_br_att=w_br_att, w_br_sg=w_br_sg, w_br_ssm=w_br_ssm, w_out=w_out)
    mom = dict(w_in=m_w_in, norm_pre=m_norm_pre, norm_post=m_norm_post, rel_bias=m_rel_bias, att_sinks=m_att_sinks,
               sg_ln_g=m_sg_ln_g, sg_ln_b=m_sg_ln_b, sg_w=m_sg_w, sg_b=m_sg_b, ssm_conv_w=m_ssm_conv_w,
               ssm_conv_b=m_ssm_conv_b, ssm_dt_bias=m_ssm_dt_bias, ssm_a_log=m_ssm_a_log, ssm_d=m_ssm_d,
               ssm_norm_g=m_ssm_norm_g, w_br_att=m_w_br_att, w_br_sg=m_w_br_sg, w_br_ssm=m_w_br_ssm, w_out=m_w_out)
    var = dict(w_in=v_w_in, norm_pre=v_norm_pre, norm_post=v_norm_post, rel_bias=v_rel_bias, att_sinks=v_att_sinks,
               sg_ln_g=v_sg_ln_g, sg_ln_b=v_sg_ln_b, sg_w=v_sg_w, sg_b=v_sg_b, ssm_conv_w=v_ssm_conv_w,
               ssm_conv_b=v_ssm_conv_b, ssm_dt_bias=v_ssm_dt_bias, ssm_a_log=v_ssm_a_log, ssm_d=v_ssm_d,
               ssm_norm_g=v_ssm_norm_g, w_br_att=v_w_br_att, w_br_sg=v_w_br_sg, w_br_ssm=v_w_br_ssm, w_out=v_w_out)
    xs0 = x[0]
    target = loss_target[0]
    my_dev = 4 * lax.axis_index("x") + 2 * lax.axis_index("y") + lax.axis_index("c")

    conv_shard = _pad_rows(ssm_conv_w.reshape(-1, D_MODEL), 8)
    win_shard = _pack_win(w_in).astype(WIRE_DTYPE)
    rest_shard = _pack_rest(*[w[n] for n in REST]).astype(WIRE_DTYPE)
    layer_shards = [[win_shard[l * D_MODEL:(l + 1) * D_MODEL], rest_shard[l * LAYER_REST:(l + 1) * LAYER_REST]]
                    for l in range(2)]
    h0, (g_win0, gathered_conv) = rmsnorm_fwd(xs0, norm_pre[0][None], Exchange([], [layer_shards[0][0], conv_shard]))
    conv_full = gathered_conv[:, 0:3].reshape(N_DEV, 2, 4, 384).transpose(1, 2, 0, 3).reshape(2, 4, CONV_DIM)

    def set_rest(lw, g_rest):
        o = 0
        for name, rws in zip(("att", "sg", "ssm", "out"), REST_PARTS):
            lw[name] = g_rest[:, o:o + rws].reshape(N_DEV * rws, D_MODEL).astype(MXU_DTYPE)
            o += rws

    def layer_weights(l, g_win):
        slabs = _slabs_from_shards(g_win)
        lw = {"in_" + name: slab.astype(MXU_DTYPE) for name, slab in slabs.items()}
        lw["in_att"] = lw["in_att"].T
        tril = jnp.tril(jnp.ones((BLK, BLK), bool))
        sgw = jnp.where(tril[None], sg_w[l], 0.0)
        lw.update(
            g_pre=norm_pre[l][None], g_post=norm_post[l][None], sinks=jnp.repeat(att_sinks[l], BLK).reshape(2, GROUP_LANES),
            ln_g=sg_ln_g[l][None], ln_b=sg_ln_b[l][None], sgw=sgw.astype(MXU_DTYPE),
            sgw_t=sgw.transpose(0, 2, 1).astype(MXU_DTYPE), sgb_t=_pad_lanes(sg_b[l].T),
            cw=conv_full[l], cb=ssm_conv_b[l][None], dtb=_pad_lanes(ssm_dt_bias[l][None]),
            alog=_pad_lanes(ssm_a_log[l][None]), dsk=_pad_lanes(ssm_d[l][None]), ng=ssm_norm_g[l][None])
        return lw

    onehot_t, maskadd = _bucket_onehot_t()
    bias = bias_table(rel_bias.T, onehot_t, maskadd).reshape(2, GROUP_HEADS, BLK, BLK).transpose(0, 2, 1, 3)
    bias = bias.reshape(2, BLK, GROUP_LANES)

    saved = []
    xl = xs0
    layers = [layer_weights(0, g_win0)]
    for l in range(2):
        lw = layers[l]
        h = h0 if l == 0 else rmsnorm_fwd(xl, lw["g_pre"])
        pa = mm_nt(lw["in_att"], h, 1152, "proj_att")
        ps = mm_nn(h, lw["in_sg"], 1536, "proj_sg")
        pm = mm_nn(h, lw["in_ssm"], 1792, "proj_ssm")
        pg = mm_nn(h, lw["in_gate"], 1536, "proj_gate")
        if l == 0:
            ya, (g_rest0,) = attn_fwd(pa, bias, lw["sinks"], Exchange([], [layer_shards[0][1]]))
            set_rest(lw, g_rest0)
        else:
            ya = attn_fwd(pa, bias, lw["sinks"])
        sgu_args = (ps, lw["ln_g"], lw["ln_b"], lw["sgw"], lw["sgb_t"])
        ssd_args = (pm, lw["cw"], lw["cb"], lw["dtb"], lw["alog"], lw["dsk"], lw["ng"])
        ys = sgu_fwd(*sgu_args)
        if l == 0:
            ym, states, conv_pre, (g_win1, g_rest1) = ssd_fwd(*ssd_args, Exchange([], layer_shards[1]))
            layers.append(layer_weights(1, g_win1))
            set_rest(layers[1], g_rest1)
        else:
            ym, states, conv_pre = ssd_fwd(*ssd_args)
        merge_args = (xl, ya, ys, ym, pg, lw["att"], lw["sg"], lw["ssm"], lw["out"], lw["g_post"])
        if l == 0:
            x_next, ba, bs, bm, merged, out_s = merge_fwd(*merge_args)
        else:
            dx, loss_part, ba, bs, bm, merged, out_s = merge_fwd(*merge_args, target)
        saved.append(dict(x=xl, h=h, pa=pa, ps=ps, pm=pm, pg=pg, ya=ya, ys=ys, ym=ym, states=states,
                          conv_pre=conv_pre, ba=ba, bs=bs, bm=bm, merged=merged, out_s=out_s))
        xl = x_next

    loss = lax.psum(loss_part[0, 0], ("x", "y", "c"))

    dbias = jnp.zeros((2, BLK, GROUP_LANES), F32)
    win_grads, rest_grads = [None, None], [None, None]
    small = {n: [None, None] for n in ("norm_pre", "norm_post", "att_sinks", "sg_ln_g", "sg_ln_b", "sg_w", "sg_b",
                                       "ssm_conv_b", "ssm_dt_bias", "ssm_a_log", "ssm_d", "ssm_norm_g",
                                       "conv_w_full")}
    for l in (1, 0):
        lw, sv = layers[l], saved[l]
        dout, dba, dbs, dbm, dpg, dya, dys, dym, dg_post = merge_bwd(
            dx, sv["out_s"], sv["pg"], sv["ba"], sv["bs"], sv["bm"], lw["att"], lw["sg"], lw["ssm"], lw["out"],
            lw["g_post"])
        dw_out = mm_tn(sv["merged"], dout, 1024, "dw_out")
        dw_att = mm_kn(sv["ya"], dba, 1024, "dw_br_att")
        dw_sg = mm_tn(sv["ys"], dbs, 1024, "dw_br_sg")
        dw_ssm = mm_tn(sv["ym"], dbm, 1024, "dw_br_ssm")
        rest_grads[l] = jnp.concatenate(
            [dw_att.reshape(N_DEV, 128, D_MODEL), dw_sg.reshape(N_DEV, 128, D_MODEL),
             dw_ssm.reshape(N_DEV, 256, D_MODEL), dw_out.reshape(N_DEV, 128, D_MODEL)], axis=1).astype(WIRE_DTYPE)
        dpa, dbias, dsinks = attn_bwd(sv["pa"], dya, bias, lw["sinks"], dbias)
        dps, dsgw, dsgb_t, dln_g, dln_b = sgu_bwd(sv["ps"], dys, lw["ln_g"], lw["ln_b"], lw["sgw"], lw["sgw_t"],
                                                  lw["sgb_t"])
        ssd_args = (sv["pm"], sv["conv_pre"], dym, sv["states"], lw["cw"], lw["dtb"], lw["alog"], lw["dsk"], lw["ng"])
        if l == 0:
            dpm, dcw, dcb, dvec, dng, (recv_win1, recv_rest0) = ssd_bwd(
                *ssd_args, Exchange([win_grads[1], rest_grads[0]], []))
        else:
            dpm, dcw, dcb, dvec, dng, (recv_rest1,) = ssd_bwd(*ssd_args, Exchange([rest_grads[1]], []))
        dslabs = dict(att=mm_kn(dpa, sv["h"], 1152, "dw_in_att").T, sg=mm_tn(sv["h"], dps, 3072, "dw_in_sg"),
                      ssm=mm_tn(sv["h"], dpm, 2688, "dw_in_ssm"), gate=mm_tn(sv["h"], dpg, 3072, "dw_in_gate"))
        win_grads[l] = _shards_from_slabs(dslabs)
        dh_args = ([dpa, dps, dpm, dpg], [lw["in_att"], lw["in_sg"], lw["in_ssm"], lw["in_gate"]], sv["x"],
                   lw["g_pre"], dx)
        if l == 0:
            dx, dg_pre, (recv_win0,) = dh_norm_bwd(*dh_args, Exchange([win_grads[0]], []))
        else:
            dx, dg_pre = dh_norm_bwd(*dh_args)
        small["norm_pre"][l] = dg_pre[0]
        small["norm_post"][l] = dg_post[0]
        small["att_sinks"][l] = dsinks[0, :ATT_HEADS]
        small["sg_ln_g"][l] = dln_g[0]
        small["sg_ln_b"][l] = dln_b[0]
        small["sg_w"][l] = dsgw
        small["sg_b"][l] = dsgb_t[:, :SG_GROUPS].T
        small["ssm_conv_b"][l] = dcb[0]
        small["ssm_dt_bias"][l] = dvec[0, :SSM_HEADS]
        small["ssm_a_log"][l] = dvec[1, :SSM_HEADS]
        small["ssm_d"][l] = dvec[2, :SSM_HEADS]
        small["ssm_norm_g"][l] = dng[0]
        small["conv_w_full"][l] = dcw[0:4]
    grad_x = dx
    dbias = dbias.reshape(2, BLK, GROUP_HEADS, BLK).transpose(0, 2, 1, 3).reshape(ATT_HEADS, BLK * BLK)
    d_rel_bias = bias_table_bwd(dbias, onehot_t).T

    small_d = {n: jnp.stack(v) for n, v in small.items()}
    small_d["rel_bias"] = d_rel_bias
    *res_win, (recv_small,) = adamw([recv_win0, recv_win1], w_in, m_w_in, v_w_in, WIN_TILE, "adamw_w_in",
                                    Exchange([], [_pack_small(small_d)]))
    res_rest = adamw([recv_rest0, recv_rest1], _pack_rest(*[w[n] for n in REST]), _pack_rest(*[mom[n] for n in REST]),
                     _pack_rest(*[var[n] for n in REST]), REST_TILE, "adamw_rest")
    small_names = [n for n, _ in SMALL_SIZES if n != "conv_w_full"]
    g_s, dw_s, nm_s, nv_s = adamw([recv_small], _pack_small({n: w[n] for n in small_names}),
                                  _pack_small({n: mom[n] for n in small_names}),
                                  _pack_small({n: var[n] for n in small_names}), SMALL_TILE, "adamw_small")
    shapes = {n: w[n].shape for n in small_names}
    shapes["conv_w_full"] = (2, 4, CONV_DIM)
    g_conv_full = _unpack_small(g_s, shapes)["conv_w_full"]
    g_conv = lax.dynamic_slice_in_dim(g_conv_full, my_dev * 384, 384, axis=2)
    pack_conv = lambda a: _pad_rows(a.reshape(-1, D_MODEL), 8)
    g_c, dw_c, nm_c, nv_c = adamw([pack_conv(g_conv)[None]], pack_conv(ssm_conv_w), pack_conv(m_ssm_conv_w),
                                  pack_conv(v_ssm_conv_w), 8, "adamw_conv")

    results = {}
    for q, (tag, psm, pc) in enumerate((("grad", g_s, g_c), ("delta", dw_s, dw_c), ("new_m", nm_s, nm_c),
                                        ("new_v", nv_s, nv_c))):
        r = dict(zip(REST, _unpack_rest(res_rest[q])))
        r["w_in"] = res_win[q]
        r.update(_unpack_small(psm, {n: w[n].shape for n in small_names}))
        r["ssm_conv_w"] = pc[0:3].reshape(2, 4, 384)
        results[tag] = r
    outs = [loss, grad_x[None]]
    for tag in ("grad", "delta", "new_m", "new_v"):
        outs += [results[tag][n] for n in WEIGHTS]
    return tuple(outs)
```

```python
import math

import jax
import jax.numpy as jnp
from jax import lax
from jax.experimental import pallas as pl
from jax.experimental.pallas import tpu as pltpu

F32 = jnp.float32
MXU_DTYPE = jnp.bfloat16
ACT_DTYPE = jnp.bfloat16
WIRE_DTYPE = jnp.bfloat16
HI = lax.Precision.HIGHEST
MESH = pl.DeviceIdType.MESH

D_MODEL = 1024
N_DEV = 8
ATT_HEADS = 16
HEAD_DIM = 64
BLK = 128
SG_GROUPS = 8
SSM_WIDTH = 2048
SSM_HEADS = 32
SSM_GROUPS = 4
SSM_GW = SSM_WIDTH // SSM_GROUPS
CONV_DIM = 3072
REL_BUCKETS = 32
EPS = 1e-6
NEG = -1e30

ATT_COLS = 2304
SG_COLS = 3072
SSM_COLS = 5376
GATE_COLS = 3072
DT_OFF = 5120

VMEM_LIMIT_V7X = 56 * 2 ** 20
DH_TK = 768

ADAM_LR, ADAM_B1, ADAM_B2, ADAM_EPS, ADAM_WD, ADAM_STEP = 0.001, 0.9, 0.999, 1e-08, 0.01, 10

WIN_SHARD = 1700
WIN_LANES = 1792
REST_PARTS = (128, 128, 256, 128)
LAYER_REST = sum(REST_PARTS)
REST_TILE = 128
WIN_TILE = 128
SMALL_ROWS = 384
SMALL_TILE = 128


def _cparams(sem=None):
    return pltpu.CompilerParams(dimension_semantics=sem, vmem_limit_bytes=VMEM_LIMIT_V7X)


def _dot(a, b):
    return jnp.dot(a.astype(MXU_DTYPE), b.astype(MXU_DTYPE), preferred_element_type=F32)


def _dot_nt(a, b):
    return lax.dot_general(a.astype(MXU_DTYPE), b.astype(MXU_DTYPE), (((1,), (1,)), ((), ())),
                           preferred_element_type=F32)


def _dot_tn(a, b):
    return lax.dot_general(a.astype(MXU_DTYPE), b.astype(MXU_DTYPE), (((0,), (0,)), ((), ())),
                           preferred_element_type=F32)


def _dot_hi(a, b):
    return jnp.dot(a, b, precision=HI, preferred_element_type=F32)


def _dot_onehot(a, onehot):
    hi = a.astype(jnp.bfloat16)
    lo = (a - hi.astype(F32)).astype(jnp.bfloat16)
    return (jnp.dot(hi, onehot, preferred_element_type=F32) + jnp.dot(lo, onehot, preferred_element_type=F32))


def _dot_hi_nt(a, b):
    return lax.dot_general(a, b, (((1,), (1,)), ((), ())), precision=HI, preferred_element_type=F32)


def _sig(x):
    return 1.0 / (1.0 + jnp.exp(-x))


def _dsilu(x, s):
    return s * (1.0 + x * (1.0 - s))


def _full(shape):
    nd = len(shape)
    return pl.BlockSpec(shape, lambda *_: (0,) * nd)


def rmsnorm_fwd(x, g, ex=None):
    s, d = x.shape
    tm = min(512, s)

    def body(x_ref, g_ref, o_ref):
        xv = x_ref[...]
        r = lax.rsqrt(jnp.mean(xv * xv, axis=-1, keepdims=True) + EPS)
        o_ref[...] = (xv * r * g_ref[...]).astype(o_ref.dtype)

    (h,), hosted = _call_hosting(
        body, "rmsnorm_fwd", s // tm,
        in_specs=[pl.BlockSpec((tm, d), lambda i: (i, 0)), _full((1, d))],
        out_specs=[pl.BlockSpec((tm, d), lambda i: (i, 0))],
        out_shape=[jax.ShapeDtypeStruct((s, d), ACT_DTYPE)], scratch=[], args=(x, g), ex=ex)
    return (h, hosted) if ex is not None else h


def mm_nn(a, b, tn, name):
    s, k = a.shape
    n = b.shape[1]
    tm = min(2048, s)

    def body(a_ref, b_ref, o_ref):
        o_ref[...] = _dot(a_ref[...], b_ref[...]).astype(o_ref.dtype)

    return pl.pallas_call(
        body, name=name, grid=(s // tm, n // tn),
        in_specs=[pl.BlockSpec((tm, k), lambda i, j: (i, 0)), pl.BlockSpec((k, tn), lambda i, j: (0, j))],
        out_specs=pl.BlockSpec((tm, tn), lambda i, j: (i, j)),
        out_shape=jax.ShapeDtypeStruct((s, n), ACT_DTYPE),
        compiler_params=_cparams(("parallel", "arbitrary")),
    )(a, b)


def mm_nt(a, b, tm, name):
    m, k = a.shape
    s = b.shape[0]
    ts = min(2048, s)

    def body(a_ref, b_ref, o_ref):
        o_ref[...] = _dot_nt(a_ref[...], b_ref[...]).astype(o_ref.dtype)

    return pl.pallas_call(
        body, name=name, grid=(s // ts, m // tm),
        in_specs=[pl.BlockSpec((tm, k), lambda i, j: (j, 0)), pl.BlockSpec((ts, k), lambda i, j: (i, 0))],
        out_specs=pl.BlockSpec((tm, ts), lambda i, j: (j, i)),
        out_shape=jax.ShapeDtypeStruct((m, s), ACT_DTYPE),
        compiler_params=_cparams(("parallel", "arbitrary")),
    )(a, b)


def mm_kn(a, b, tm, name):
    m, s = a.shape
    n = b.shape[1]
    ts = min(512, s)
    nt = s // ts

    def body(a_ref, b_ref, o_ref, acc_ref):
        @pl.when(pl.program_id(1) == 0)
        def _():
            acc_ref[...] = jnp.zeros_like(acc_ref)

        acc_ref[...] += _dot(a_ref[...], b_ref[...])

        @pl.when(pl.program_id(1) == nt - 1)
        def _():
            o_ref[...] = acc_ref[...].astype(o_ref.dtype)

    return pl.pallas_call(
        body, name=name, grid=(m // tm, nt),
        in_specs=[pl.BlockSpec((tm, ts), lambda j, t: (j, t)), pl.BlockSpec((ts, n), lambda j, t: (t, 0))],
        out_specs=pl.BlockSpec((tm, n), lambda j, t: (j, 0)),
        out_shape=jax.ShapeDtypeStruct((m, n), WIRE_DTYPE),
        scratch_shapes=[pltpu.VMEM((tm, n), F32)],
        compiler_params=_cparams(("parallel", "arbitrary")),
    )(a, b)


def mm_tn(a, b, tn, name):
    s, k = a.shape
    n = b.shape[1]
    ts = min(512, s)
    nt = s // ts

    def body(a_ref, b_ref, o_ref, acc_ref):
        @pl.when(pl.program_id(1) == 0)
        def _():
            acc_ref[...] = jnp.zeros_like(acc_ref)

        acc_ref[...] += _dot_tn(a_ref[...], b_ref[...])

        @pl.when(pl.program_id(1) == nt - 1)
        def _():
            o_ref[...] = acc_ref[...].astype(o_ref.dtype)

    return pl.pallas_call(
        body, name=name, grid=(n // tn, nt),
        in_specs=[pl.BlockSpec((ts, k), lambda j, t: (t, 0)), pl.BlockSpec((ts, tn), lambda j, t: (t, j))],
        out_specs=pl.BlockSpec((k, tn), lambda j, t: (0, j)),
        out_shape=jax.ShapeDtypeStruct((k, n), WIRE_DTYPE),
        scratch_shapes=[pltpu.VMEM((k, tn), F32)],
        compiler_params=_cparams(("parallel", "arbitrary")),
    )(a, b)


def dh_norm_bwd(dslabs, wslabs, x, g, dres, ex=None):
    s, d = x.shape
    tm = min(1024, s)
    widths = [ds.shape[0 if q == 0 else 1] for q, ds in enumerate(dslabs)]
    tks = [DH_TK] * len(widths)
    counts = [wd // t for wd, t in zip(widths, tks)]
    starts = [sum(counts[:i]) for i in range(len(counts))]
    nk = sum(counts)
    ns = len(dslabs)

    hosted = ex is not None
    ni = s // tm

    def mm_body(*refs):
        (own_in, (dh_ref,), _), hosted_refs = _split_hosted(refs, 2 * ns, 1, 0, ex)
        d_refs, w_refs = own_in[:ns], own_in[ns:]
        i, k = pl.program_id(0), pl.program_id(1)
        if hosted:
            @pl.when((i == 0) & (k == 0))
            def _():
                ex.start(*hosted_refs)

            @pl.when((i == ni - 1) & (k == nk - 1))
            def _():
                ex.relay(*hosted_refs)
                ex.wait(*hosted_refs)

        @pl.when(k == 0)
        def _():
            dh_ref[...] = jnp.zeros_like(dh_ref)

        for q in range(ns):
            @pl.when((k >= starts[q]) & (k < starts[q] + counts[q]))
            def _(q=q):
                if q == 0:
                    dh_ref[...] += _dot_tn(d_refs[q][...], w_refs[q][...])
                else:
                    dh_ref[...] += _dot_nt(d_refs[q][...], w_refs[q][...])

    def clamp(q):
        if q == 0:
            return pl.BlockSpec((tks[q], tm), lambda i, k: (jnp.clip(k - starts[q], 0, counts[q] - 1), i))
        return pl.BlockSpec((tm, tks[q]), lambda i, k: (i, jnp.clip(k - starts[q], 0, counts[q] - 1)))

    def clamp_w(q):
        if q == 0:
            return pl.BlockSpec((tks[q], d), lambda i, k: (jnp.clip(k - starts[q], 0, counts[q] - 1), 0))
        return pl.BlockSpec((d, tks[q]), lambda i, k: (0, jnp.clip(k - starts[q], 0, counts[q] - 1)))

    res = pl.pallas_call(
        mm_body, name="dh_matmul_scatter" if hosted else "dh_matmul", grid=(ni, nk),
        in_specs=([clamp(q) for q in range(ns)] + [clamp_w(q) for q in range(ns)]
                  + (ex.in_specs if hosted else [])),
        out_specs=[pl.BlockSpec((tm, d), lambda i, k: (i, 0))] + (ex.out_specs if hosted else []),
        out_shape=[jax.ShapeDtypeStruct((s, d), F32)] + (ex.out_shape if hosted else []),
        scratch_shapes=ex.scratch if hosted else [],
        compiler_params=_cparams(("arbitrary" if hosted else "parallel", "arbitrary")),
    )(*dslabs, *wslabs, *(ex.arrays if hosted else []))
    dh, ex_results = res[0], res[1:]

    te = min(512, s)

    def norm_body(dh_ref, x_ref, g_ref, dres_ref, dx_ref, dg_ref):
        @pl.when(pl.program_id(0) == 0)
        def _():
            dg_ref[...] = jnp.zeros_like(dg_ref)

        xv = x_ref[...]
        r = lax.rsqrt(jnp.mean(xv * xv, axis=-1, keepdims=True) + EPS)
        xn = xv * r
        dhv = dh_ref[...]
        dg_ref[...] += jnp.sum(dhv * xn, axis=0, keepdims=True)
        dxn = dhv * g_ref[...]
        dx_ref[...] = dres_ref[...] + r * (dxn - xn * jnp.mean(dxn * xn, axis=-1, keepdims=True))

    rows = pl.BlockSpec((te, d), lambda i: (i, 0))
    dx, dg = pl.pallas_call(
        norm_body, name="norm_bwd", grid=(s // te,),
        in_specs=[rows, rows, _full((1, d)), rows],
        out_specs=[rows, _full((1, d))],
        out_shape=[jax.ShapeDtypeStruct((s, d), F32), jax.ShapeDtypeStruct((1, d), F32)],
        compiler_params=_cparams(("arbitrary",)),
    )(dh, x, g, dres)
    return (dx, dg, ex_results) if hosted else (dx, dg)


def bias_table(rel_bias_t, onehot_t, maskadd):
    n = onehot_t.shape[1]
    tn = 8192

    def body(r_ref, o_ref, m_ref, out_ref):
        out_ref[...] = _dot_hi(r_ref[...], o_ref[...]) + m_ref[...]

    return pl.pallas_call(
        body, name="bias_table", grid=(n // tn,),
        in_specs=[_full((ATT_HEADS, REL_BUCKETS)), pl.BlockSpec((REL_BUCKETS, tn), lambda j: (0, j)),
                  pl.BlockSpec((1, tn), lambda j: (0, j))],
        out_specs=pl.BlockSpec((ATT_HEADS, tn), lambda j: (0, j)),
        out_shape=jax.ShapeDtypeStruct((ATT_HEADS, n), F32),
        compiler_params=_cparams(("parallel",)),
    )(rel_bias_t, onehot_t, maskadd)


def bias_table_bwd(dbias, onehot_t):
    n = onehot_t.shape[1]
    tn = 8192

    def body(d_ref, o_ref, out_ref):
        @pl.when(pl.program_id(0) == 0)
        def _():
            out_ref[...] = jnp.zeros_like(out_ref)

        out_ref[...] += _dot_hi_nt(d_ref[...], o_ref[...])

    return pl.pallas_call(
        body, name="bias_table_bwd", grid=(n // tn,),
        in_specs=[pl.BlockSpec((ATT_HEADS, tn), lambda j: (0, j)), pl.BlockSpec((REL_BUCKETS, tn), lambda j: (0, j))],
        out_specs=_full((ATT_HEADS, REL_BUCKETS)),
        out_shape=jax.ShapeDtypeStruct((ATT_HEADS, REL_BUCKETS), F32),
        compiler_params=_cparams(("arbitrary",)),
    )(dbias, onehot_t)


def _fold(full, tri):
    return jnp.where(tri, full[BLK:2 * BLK], full[0:BLK])


def _unfold(folded, tri):
    return jnp.concatenate([jnp.where(tri, 0.0, folded), jnp.where(tri, folded, 0.0)], axis=0)


GROUP_HEADS = ATT_HEADS // 2
GROUP_LANES = GROUP_HEADS * BLK


def _att_group(qg, kcat, vt_cat, bias_g, sink_g, tri, no_prev):
    l = _fold(_dot(kcat, qg), tri) * (HEAD_DIM ** -0.5) + bias_g
    l = jnp.where(no_prev, NEG, l)
    m = jnp.maximum(jnp.max(l, axis=0, keepdims=True), sink_g)
    p = jnp.exp(l - m)
    es = jnp.exp(sink_g - m)
    inv = 1.0 / (jnp.sum(p, axis=0, keepdims=True) + es)
    p = p * inv
    pcat = _unfold(p, tri)
    return p, pcat, es * inv, _dot(vt_cat, pcat)


ATT_SUB = 4


def _heads_to_lanes(ref, row0, ln):
    return jnp.concatenate([ref[row0 + j * HEAD_DIM:row0 + (j + 1) * HEAD_DIM, ln] for j in range(GROUP_HEADS)], axis=1)


def _lanes_to_heads(ref, row0, ln, val):
    for j in range(GROUP_HEADS):
        ref[row0 + j * HEAD_DIM:row0 + (j + 1) * HEAD_DIM, ln] = val[:, j * BLK:(j + 1) * BLK].astype(ref.dtype)


def _kv_cat(kvp, kvc, g):
    lo = g * HEAD_DIM
    kt_cat = jnp.concatenate([kvp[lo:lo + HEAD_DIM], kvc[lo:lo + HEAD_DIM]], axis=1)
    vt_cat = jnp.concatenate([kvp[128 + lo:128 + lo + HEAD_DIM], kvc[128 + lo:128 + lo + HEAD_DIM]], axis=1)
    return kt_cat, vt_cat


def _tri_masks(n):
    row = lax.broadcasted_iota(jnp.int32, (BLK, GROUP_LANES), 0)
    query = lax.broadcasted_iota(jnp.int32, (BLK, GROUP_LANES), 1) & (BLK - 1)
    tri = row <= query
    return tri, (n == 0) & jnp.logical_not(tri)


def _split_hosted(refs, n_in, n_out, n_scratch, ex):
    na = ex.na if ex is not None else 0
    o = 0
    parts = []
    for cnt in (n_in, na, n_out, na, n_scratch, 3 if ex is not None else 0):
        parts.append(refs[o:o + cnt])
        o += cnt
    own_in, ex_in, own_out, ex_out, own_scr, ex_sems = parts
    return (own_in, own_out, own_scr), (ex_in, ex_out, ex_sems)


def _call_hosting(body, name, nsteps, in_specs, out_specs, out_shape, scratch, args, ex):
    n_in, n_out, n_scr = len(in_specs), len(out_specs), len(scratch)
    hosted = ex is not None

    def full_body(*refs):
        (own_in, own_out, own_scr), hosted_refs = _split_hosted(refs, n_in, n_out, n_scr, ex)
        if hosted:
            @pl.when(pl.program_id(0) == 0)
            def _():
                ex.start(*hosted_refs)

            @pl.when(pl.program_id(0) == max(nsteps - max(nsteps // 8, 4), 0))
            def _():
                ex.relay(*hosted_refs)

            @pl.when(pl.program_id(0) == nsteps - 1)
            def _():
                ex.wait(*hosted_refs)

        body(*own_in, *own_out, *own_scr)

    res = pl.pallas_call(
        full_body, name=name + "_hosting" if hosted else name, grid=(nsteps,),
        in_specs=list(in_specs) + (ex.in_specs if hosted else []),
        out_specs=list(out_specs) + (ex.out_specs if hosted else []),
        out_shape=list(out_shape) + (ex.out_shape if hosted else []),
        scratch_shapes=list(scratch) + (ex.scratch if hosted else []),
        compiler_params=_cparams(("arbitrary",)),
    )(*args, *(ex.arrays if hosted else []))
    return res[:n_out], res[n_out:]


def attn_fwd(pa, bias, sinks, ex=None):
    s = pa.shape[1]
    nsteps = s // (ATT_SUB * BLK)

    def body(pa_ref, kvp_ref, bias_ref, sink_ref, y_ref):
        for sub in range(ATT_SUB):
            n = pl.program_id(0) * ATT_SUB + sub
            ln = slice(sub * BLK, (sub + 1) * BLK)
            kvc = pa_ref[2048:2304, ln]
            kvp = kvp_ref[...] if sub == 0 else pa_ref[2048:2304, (sub - 1) * BLK:sub * BLK]
            tri, no_prev = _tri_masks(n)
            for g in range(2):
                kt_cat, vt_cat = _kv_cat(kvp, kvc, g)
                row0 = g * GROUP_HEADS * HEAD_DIM
                _, _, _, o = _att_group(_heads_to_lanes(pa_ref, row0, ln), kt_cat.astype(F32).T, vt_cat, bias_ref[g],
                                        sink_ref[g:g + 1, :], tri, no_prev)
                z = _heads_to_lanes(pa_ref, 1024 + row0, ln).astype(F32)
                _lanes_to_heads(y_ref, row0, ln, o * z * _sig(z))

    (y,), hosted = _call_hosting(
        body, "attn_fwd", nsteps,
        in_specs=[pl.BlockSpec((ATT_COLS, ATT_SUB * BLK), lambda n: (0, n)),
                  pl.BlockSpec((256, BLK), lambda n: (8, jnp.maximum(ATT_SUB * n - 1, 0))),
                  _full((2, BLK, GROUP_LANES)), _full((2, GROUP_LANES))],
        out_specs=[pl.BlockSpec((1024, ATT_SUB * BLK), lambda n: (0, n))],
        out_shape=[jax.ShapeDtypeStruct((1024, s), ACT_DTYPE)], scratch=[],
        args=(pa, pa, bias, sinks), ex=ex)
    return (y, hosted) if ex is not None else y


def attn_bwd(pa, dy, bias, sinks, dbias_in):
    s = pa.shape[1]
    nsteps = s // (ATT_SUB * BLK)

    def body(pa_ref, kvp_ref, dy_ref, bias_ref, sink_ref, dbin_ref, dpa_ref, dbias_ref, dsink_ref, carry, dsink_acc):
        i = pl.program_id(0)

        @pl.when(i == 0)
        def _():
            dbias_ref[...] = dbin_ref[...]
            dsink_acc[...] = jnp.zeros_like(dsink_acc)
            carry[...] = jnp.zeros_like(carry)

        scale = HEAD_DIM ** -0.5
        for sub in reversed(range(ATT_SUB)):
            n = (nsteps - 1 - i) * ATT_SUB + sub
            ln = slice(sub * BLK, (sub + 1) * BLK)
            kvc = pa_ref[2048:2304, ln]
            kvp = kvp_ref[...] if sub == 0 else pa_ref[2048:2304, (sub - 1) * BLK:sub * BLK]
            tri, no_prev = _tri_masks(n)
            for g in range(2):
                kt_cat, vt_cat = _kv_cat(kvp, kvc, g)
                row0 = g * GROUP_HEADS * HEAD_DIM
                qg = _heads_to_lanes(pa_ref, row0, ln)
                p, pcat, psink, o = _att_group(qg, kt_cat.astype(F32).T, vt_cat, bias_ref[g], sink_ref[g:g + 1, :],
                                               tri, no_prev)
                z = _heads_to_lanes(pa_ref, 1024 + row0, ln).astype(F32)
                dyg = _heads_to_lanes(dy_ref, row0, ln).astype(F32)
                sz = _sig(z)
                d_o = dyg * z * sz
                _lanes_to_heads(dpa_ref, 1024 + row0, ln, dyg * _dsilu(z, sz) * o)
                delta = jnp.sum(d_o * o, axis=0, keepdims=True)
                dl = p * (_fold(_dot(vt_cat.astype(F32).T, d_o), tri) - delta)
                dsink_acc[g:g + 1, :] += psink * delta
                dbias_ref[g] += dl
                dlcat = _unfold(dl, tri)
                _lanes_to_heads(dpa_ref, row0, ln, _dot(kt_cat, dlcat) * scale)
                for q, dkv in enumerate((_dot_nt(qg, dlcat) * scale, _dot_nt(d_o, pcat))):
                    r0 = q * 128 + g * HEAD_DIM
                    dpa_ref[2048 + r0:2048 + r0 + HEAD_DIM, ln] = (
                        dkv[:, BLK:2 * BLK] + carry[r0:r0 + HEAD_DIM, :]).astype(dpa_ref.dtype)
                    carry[r0:r0 + HEAD_DIM, :] = dkv[:, 0:BLK]

        @pl.when(i == nsteps - 1)
        def _():
            lane = lax.broadcasted_iota(jnp.int32, (1, 128), 1)
            dsink = jnp.zeros((1, 128), F32)
            for h in range(ATT_HEADS):
                g, j = divmod(h, GROUP_HEADS)
                tot = jnp.sum(dsink_acc[g:g + 1, j * BLK:(j + 1) * BLK], axis=1, keepdims=True)
                dsink = dsink + jnp.where(lane == h, -tot, 0.0)
            dsink_ref[...] = dsink

    return pl.pallas_call(
        body, name="attn_bwd", grid=(nsteps,),
        in_specs=[pl.BlockSpec((ATT_COLS, ATT_SUB * BLK), lambda i: (0, nsteps - 1 - i)),
                  pl.BlockSpec((256, BLK), lambda i: (8, jnp.maximum(ATT_SUB * (nsteps - 1 - i) - 1, 0))),
                  pl.BlockSpec((1024, ATT_SUB * BLK), lambda i: (0, nsteps - 1 - i)),
                  _full((2, BLK, GROUP_LANES)), _full((2, GROUP_LANES)), _full((2, BLK, GROUP_LANES))],
        out_specs=[pl.BlockSpec((ATT_COLS, ATT_SUB * BLK), lambda i: (0, nsteps - 1 - i)),
                   _full((2, BLK, GROUP_LANES)), _full((1, 128))],
        out_shape=[jax.ShapeDtypeStruct((ATT_COLS, s), ACT_DTYPE),
                   jax.ShapeDtypeStruct((2, BLK, GROUP_LANES), F32),
                   jax.ShapeDtypeStruct((1, 128), F32)],
        scratch_shapes=[pltpu.VMEM((256, BLK), F32), pltpu.VMEM((2, GROUP_LANES), F32)],
        compiler_params=_cparams(("arbitrary",)),
    )(pa, pa, dy, bias, sinks, dbias_in)


def _layernorm(v, g, b):
    mu = jnp.mean(v, axis=-1, keepdims=True)
    vc = v - mu
    rstd = lax.rsqrt(jnp.mean(vc * vc, axis=-1, keepdims=True) + EPS)
    xhat = vc * rstd
    return xhat, rstd, xhat * g + b


def sgu_fwd(ps, ln_g, ln_b, w_tril, b_t):
    s = ps.shape[0]
    rows = min(4 * BLK, s)

    def body(ps_ref, g_ref, b_ref, w_ref, bt_ref, y_ref):
        u = ps_ref[:, 0:1024].astype(F32)
        v = ps_ref[:, 1024:2048].astype(F32)
        z = ps_ref[:, 2048:3072].astype(F32)
        _, _, vn = _layernorm(v, g_ref[...], b_ref[...])
        gate = u * z * _sig(z)
        for c in range(rows // BLK):
            ch = slice(c * BLK, (c + 1) * BLK)
            for g in range(SG_GROUPS):
                sl = slice(g * 128, (g + 1) * 128)
                mixed = _dot(w_ref[g], vn[ch, sl]) + bt_ref[:, g:g + 1]
                y_ref[ch, sl] = (gate[ch, sl] * mixed).astype(y_ref.dtype)

    return pl.pallas_call(
        body, name="sgu_fwd", grid=(s // rows,),
        in_specs=[pl.BlockSpec((rows, SG_COLS), lambda c: (c, 0)), _full((1, 1024)), _full((1, 1024)),
                  _full((SG_GROUPS, BLK, BLK)), _full((BLK, 128))],
        out_specs=pl.BlockSpec((rows, 1024), lambda c: (c, 0)),
        out_shape=jax.ShapeDtypeStruct((s, 1024), ACT_DTYPE),
        compiler_params=_cparams(("parallel",)),
    )(ps, ln_g, ln_b, w_tril, b_t)


def sgu_bwd(ps, dy, ln_g, ln_b, w_tril, w_tril_t, b_t):
    s = ps.shape[0]
    rows = min(4 * BLK, s)

    def body(ps_ref, dy_ref, g_ref, b_ref, w_ref, wt_ref, bt_ref, dps_ref, dw_ref, dbt_ref, dg_ref, db_ref, dvn_scr):
        @pl.when(pl.program_id(0) == 0)
        def _():
            dw_ref[...] = jnp.zeros_like(dw_ref)
            dbt_ref[...] = jnp.zeros_like(dbt_ref)
            dg_ref[...] = jnp.zeros_like(dg_ref)
            db_ref[...] = jnp.zeros_like(db_ref)

        u = ps_ref[:, 0:1024].astype(F32)
        v = ps_ref[:, 1024:2048].astype(F32)
        z = ps_ref[:, 2048:3072].astype(F32)
        dy = dy_ref[...].astype(F32)
        xhat, rstd, vn = _layernorm(v, g_ref[...], b_ref[...])
        sz = _sig(z)
        silu = z * sz
        row = lax.broadcasted_iota(jnp.int32, (BLK, BLK), 0)
        colm = lax.broadcasted_iota(jnp.int32, (BLK, BLK), 1)
        tril = row >= colm
        dbt = jnp.zeros((BLK, 128), F32)
        dsilu_z = _dsilu(z, sz)
        for c in range(rows // BLK):
            ch = slice(c * BLK, (c + 1) * BLK)
            for g in range(SG_GROUPS):
                sl = slice(g * 128, (g + 1) * 128)
                vng = vn[ch, sl]
                mixed = _dot(w_ref[g], vng) + bt_ref[:, g:g + 1]
                dyg, ug = dy[ch, sl], u[ch, sl]
                dps_ref[ch, sl] = (dyg * mixed * silu[ch, sl]).astype(dps_ref.dtype)
                dps_ref[ch, 2048 + g * 128:2048 + (g + 1) * 128] = (
                    dyg * ug * mixed * dsilu_z[ch, sl]).astype(dps_ref.dtype)
                dm = dyg * ug * silu[ch, sl]
                dw_ref[g] += jnp.where(tril, _dot_nt(dm, vng), 0.0)
                dbt = dbt + jnp.where(colm == g, jnp.sum(dm, axis=1, keepdims=True), 0.0)
                dvn_scr[ch, sl] = _dot(wt_ref[g], dm)
        dbt_ref[...] += dbt
        dvn = dvn_scr[...]
        dg_ref[...] += jnp.sum(dvn * xhat, axis=0, keepdims=True)
        db_ref[...] += jnp.sum(dvn, axis=0, keepdims=True)
        dxh = dvn * g_ref[...]
        dv = rstd * (dxh - jnp.mean(dxh, axis=-1, keepdims=True)
                     - xhat * jnp.mean(dxh * xhat, axis=-1, keepdims=True))
        dps_ref[:, 1024:2048] = dv.astype(dps_ref.dtype)

    return pl.pallas_call(
        body, name="sgu_bwd", grid=(s // rows,),
        in_specs=[pl.BlockSpec((rows, SG_COLS), lambda c: (c, 0)), pl.BlockSpec((rows, 1024), lambda c: (c, 0)),
                  _full((1, 1024)), _full((1, 1024)), _full((SG_GROUPS, BLK, BLK)), _full((SG_GROUPS, BLK, BLK)),
                  _full((BLK, 128))],
        out_specs=[pl.BlockSpec((rows, SG_COLS), lambda c: (c, 0)), _full((SG_GROUPS, BLK, BLK)), _full((BLK, 128)),
                   _full((1, 1024)), _full((1, 1024))],
        out_shape=[jax.ShapeDtypeStruct((s, SG_COLS), ACT_DTYPE), jax.ShapeDtypeStruct((SG_GROUPS, BLK, BLK), F32),
                   jax.ShapeDtypeStruct((BLK, 128), F32), jax.ShapeDtypeStruct((1, 1024), F32),
                   jax.ShapeDtypeStruct((1, 1024), F32)],
        scratch_shapes=[pltpu.VMEM((rows, 1024), F32)],
        compiler_params=_cparams(("arbitrary",)),
    )(ps, dy, ln_g, ln_b, w_tril, w_tril_t, b_t)


def _shift_down(cur, prev16, k):
    if k == 0:
        return cur
    r = pltpu.roll(cur, k, 0)
    rp = pltpu.roll(prev16, k, 0)
    row = lax.broadcasted_iota(jnp.int32, (8, cur.shape[1]), 0)
    return jnp.concatenate([jnp.where(row < k, rp[0:8], r[0:8]), r[8:]], axis=0)


def _shift_up(cur, next16, k):
    if k == 0:
        return cur
    n = cur.shape[0]
    r = pltpu.roll(cur, n - k, 0)
    rn = pltpu.roll(next16, 16 - k, 0)
    row = lax.broadcasted_iota(jnp.int32, (8, cur.shape[1]), 0)
    return jnp.concatenate([r[:n - 8], jnp.where(row >= 8 - k, rn[8:16], r[n - 8:])], axis=0)


def _bcast8(v):
    return jnp.broadcast_to(v, (16, v.shape[1]))


def _causal_conv(xbc, prev16, cw, cbias):
    pre = cbias + cw[3:4] * xbc
    for k in (1, 2, 3):
        pre = pre + cw[3 - k:4 - k] * _shift_down(xbc, prev16, k)
    return pre


class _Ssd:
    def __init__(self, pre, dtr, dtb, alog, dsk, tri, e):
        self.pre = pre
        self.sg = _sig(pre)
        act = pre * self.sg
        self.xs = act[:, 0:SSM_WIDTH]
        self.bm = act[:, SSM_WIDTH:SSM_WIDTH + 512]
        self.cm = act[:, SSM_WIDTH + 512:CONV_DIM]
        self.dtp = dtr + dtb
        self.dt = jnp.maximum(self.dtp, 0.0) + jnp.log(1.0 + jnp.exp(-jnp.abs(self.dtp)))
        self.a = -jnp.exp(alog)
        self.acs = _dot_hi(tri, self.dt * self.a)
        self.acs_t = self.acs.T
        tot = self.acs[BLK - 1:BLK]
        self.ecs = jnp.exp(self.acs)
        self.dte = jnp.exp(tot - self.acs)
        self.cd = jnp.exp(tot)
        self.dt_x = _dot_onehot(self.dt, e)
        self.ecs_x = _dot_onehot(self.ecs, e)
        self.dte_x = _dot_onehot(self.dte, e)
        self.cd_x = _dot_onehot(_bcast8(self.cd), e)[0:1]
        self.d_x = _dot_onehot(_bcast8(dsk), e)[0:1]
        self.xdt = self.xs * self.dt_x
        row = lax.broadcasted_iota(jnp.int32, (BLK, BLK), 0)
        col = lax.broadcasted_iota(jnp.int32, (BLK, BLK), 1)
        self.tril = row >= col

    def group(self, g):
        sl = slice(g * 128, (g + 1) * 128)
        bg, cg = self.bm[:, sl], self.cm[:, sl]
        return bg, cg, _dot_nt(cg, bg)

    def decay(self, h):
        seg = self.acs[:, h:h + 1] - self.acs_t[h:h + 1, :]
        return jnp.exp(jnp.where(self.tril, seg, NEG))

    def y_pre_gate(self, ht_of, yd_scr, yoff_scr, dec_scr=None):
        for g in range(SSM_GROUPS):
            bg, cg, cb = self.group(g)
            for j in range(8):
                h = g * 8 + j
                sl = slice(h * 64, (h + 1) * 64)
                dec = self.decay(h)
                if dec_scr is not None:
                    dec_scr[h] = dec
                yd_scr[:, sl] = _dot(cb * dec, self.xdt[:, sl])
            gs = slice(g * SSM_GW, (g + 1) * SSM_GW)
            yoff_scr[:, gs] = _dot(cg, ht_of(g)) * self.ecs_x[:, gs]
        return yd_scr[...] + yoff_scr[...] + self.d_x * self.xs


def _ssd_consts():
    hh = lax.broadcasted_iota(jnp.int32, (128, SSM_WIDTH), 0)
    ch = lax.broadcasted_iota(jnp.int32, (128, SSM_WIDTH), 1)
    e = (ch // 64 == hh).astype(jnp.bfloat16)
    row = lax.broadcasted_iota(jnp.int32, (BLK, BLK), 0)
    col = lax.broadcasted_iota(jnp.int32, (BLK, BLK), 1)
    tri = (row >= col).astype(F32)
    return tri, e


def _pad_lanes(v, n=128):
    return jnp.pad(v, ((0, 0), (0, n - v.shape[1])))


def ssd_fwd(pm, cw, cbias, dtb, alog, dsk, ng, ex=None):
    s = pm.shape[0]
    nc = s // BLK
    tri, e = _ssd_consts()

    def body(pm_ref, prev_ref, cw_ref, cb_ref, dtb_ref, al_ref, d_ref, ng_ref, tri_ref, e_ref,
             y_ref, st_ref, pre_ref, ht_ref, yd_scr, yoff_scr):
        c = pl.program_id(0)

        @pl.when(c == 0)
        def _():
            ht_ref[...] = jnp.zeros_like(ht_ref)

        xbc = pm_ref[:, 0:CONV_DIM].astype(F32)
        prev16 = jnp.where(c == 0, 0.0, prev_ref[...].astype(F32))
        pre = _causal_conv(xbc, prev16, cw_ref[...], cb_ref[...])
        pre_ref[...] = pre.astype(pre_ref.dtype)
        f = _Ssd(pre, pm_ref[:, DT_OFF:DT_OFF + 128].astype(F32), dtb_ref[...], al_ref[...], d_ref[...],
                 tri_ref[...], e_ref[...])
        st_ref[0] = ht_ref[...]
        y = f.y_pre_gate(lambda g: ht_ref[g], yd_scr, yoff_scr)
        for g in range(SSM_GROUPS):
            bg, _, _ = f.group(g)
            gs = slice(g * SSM_GW, (g + 1) * SSM_GW)
            ht_ref[g] = ht_ref[g] * f.cd_x[:, gs] + _dot_tn(bg, f.xdt[:, gs] * f.dte_x[:, gs])
        z = pm_ref[:, CONV_DIM:CONV_DIM + SSM_WIDTH].astype(F32)
        ypre = y * z * _sig(z)
        for g in range(SSM_GROUPS):
            gs = slice(g * SSM_GW, (g + 1) * SSM_GW)
            yg = ypre[:, gs]
            rr = lax.rsqrt(jnp.mean(yg * yg, axis=-1, keepdims=True) + EPS)
            y_ref[:, gs] = (yg * rr * ng_ref[:, gs]).astype(y_ref.dtype)

    own, hosted = _call_hosting(
        body, "ssd_fwd", nc,
        in_specs=[pl.BlockSpec((BLK, SSM_COLS), lambda c: (c, 0)),
                  pl.BlockSpec((16, CONV_DIM), lambda c: (jnp.maximum(8 * c - 1, 0), 0)),
                  _full((4, CONV_DIM)), _full((1, CONV_DIM)), _full((1, 128)), _full((1, 128)), _full((1, 128)),
                  _full((1, SSM_WIDTH)), _full((BLK, BLK)), _full((128, SSM_WIDTH))],
        out_specs=[pl.BlockSpec((BLK, SSM_WIDTH), lambda c: (c, 0)),
                   pl.BlockSpec((1, SSM_GROUPS, 128, SSM_GW), lambda c: (c, 0, 0, 0)),
                   pl.BlockSpec((BLK, CONV_DIM), lambda c: (c, 0))],
        out_shape=[jax.ShapeDtypeStruct((s, SSM_WIDTH), ACT_DTYPE),
                   jax.ShapeDtypeStruct((nc, SSM_GROUPS, 128, SSM_GW), F32),
                   jax.ShapeDtypeStruct((s, CONV_DIM), ACT_DTYPE)],
        scratch=[pltpu.VMEM((SSM_GROUPS, 128, SSM_GW), F32), pltpu.VMEM((BLK, SSM_WIDTH), F32),
                 pltpu.VMEM((BLK, SSM_WIDTH), F32)],
        args=(pm, pm, cw, cbias, dtb, alog, dsk, ng, tri, e), ex=ex)
    return (*own, hosted) if ex is not None else tuple(own)


def ssd_bwd(pm, pre, dy, states, cw, dtb, alog, dsk, ng, ex=None):
    s = pm.shape[0]
    nc = s // BLK
    tri, e = _ssd_consts()
    tri_t, e_t = tri.T, e.T

    def body(pm_ref, pre_ref, dy_ref, st_ref, cw_ref, dtb_ref, al_ref, d_ref, ng_ref,
             tri_ref, trit_ref, e_ref, et_ref,
             dpm_ref, dcw_ref, dcb_ref, dvec_ref, dng_ref,
             dht_ref, dcar_ref, yd_scr, yoff_scr, dx_scr, r2_scr, hs_scr, da_scr, dat_scr, dd_scr, dbc_scr, dec_scr):
        i = pl.program_id(0)
        n = nc - 1 - i

        @pl.when(i == 0)
        def _():
            dht_ref[...] = jnp.zeros_like(dht_ref)
            dcar_ref[...] = jnp.zeros_like(dcar_ref)
            dcw_ref[...] = jnp.zeros_like(dcw_ref)
            dcb_ref[...] = jnp.zeros_like(dcb_ref)
            dvec_ref[...] = jnp.zeros_like(dvec_ref)
            dng_ref[...] = jnp.zeros_like(dng_ref)
            dd_scr[...] = jnp.zeros_like(dd_scr)
            da_scr[...] = jnp.zeros_like(da_scr)
            dat_scr[...] = jnp.zeros_like(dat_scr)

        cw = cw_ref[...]
        f = _Ssd(pre_ref[...].astype(F32), pm_ref[:, DT_OFF:DT_OFF + 128].astype(F32), dtb_ref[...], al_ref[...],
                 d_ref[...], tri_ref[...], e_ref[...])
        et = et_ref[...]
        y = f.y_pre_gate(lambda g: st_ref[0, g], yd_scr, yoff_scr, dec_scr)

        z = pm_ref[:, CONV_DIM:CONV_DIM + SSM_WIDTH].astype(F32)
        dyv = dy_ref[...].astype(F32)
        sz = _sig(z)
        silu = z * sz
        ypre = y * silu
        for g in range(SSM_GROUPS):
            gs = slice(g * SSM_GW, (g + 1) * SSM_GW)
            yg = ypre[:, gs]
            rr = lax.rsqrt(jnp.mean(yg * yg, axis=-1, keepdims=True) + EPS)
            nrm = yg * rr
            dng_ref[:, gs] += jnp.sum(dyv[:, gs] * nrm, axis=0, keepdims=True)
            dn = dyv[:, gs] * ng_ref[:, gs]
            dx_scr[:, gs] = rr * (dn - nrm * jnp.mean(dn * nrm, axis=-1, keepdims=True))
        dypre = dx_scr[...]
        d_y = dypre * silu
        dpm_ref[:, CONV_DIM:CONV_DIM + SSM_WIDTH] = (dypre * y * _dsilu(z, sz)).astype(dpm_ref.dtype)

        for g in range(SSM_GROUPS):
            bg, cg, cb = f.group(g)
            gs = slice(g * SSM_GW, (g + 1) * SSM_GW)
            htg = st_ref[0, g]
            dhn = dht_ref[g]
            dcb = jnp.zeros((BLK, BLK), F32)
            for j in range(8):
                h = g * 8 + j
                sl = slice(h * 64, (h + 1) * 64)
                dec = dec_scr[h]
                dyh = d_y[:, sl]
                dmd = _dot_nt(dyh, f.xdt[:, sl]) * dec
                dcb = dcb + dmd
                gm = dmd * cb
                da_scr[:, h:h + 1] = jnp.sum(gm, axis=1, keepdims=True)
                dat_scr[h:h + 1, :] = jnp.sum(gm, axis=0, keepdims=True)
                dx_scr[:, sl] = _dot_tn(cb * dec, dyh)
            dz = f.ecs_x[:, gs] * d_y[:, gs]
            dbc_scr[:, 512 + g * 128:512 + (g + 1) * 128] = _dot(dcb, bg) + _dot_nt(dz, htg)
            dbc_scr[:, g * 128:(g + 1) * 128] = _dot_tn(dcb, cg) + _dot_nt(f.xdt[:, gs] * f.dte_x[:, gs], dhn)
            dws = _dot(bg, dhn)
            dx_scr[:, gs] += f.dte_x[:, gs] * dws
            r2_scr[:, gs] = dws * f.xdt[:, gs]
            hs_scr[:, gs] = _bcast8(jnp.sum(dhn * htg, axis=0, keepdims=True))
            dht_ref[g] = f.cd_x[:, gs] * dhn + _dot_tn(cg, dz)
        d_x = dx_scr[...]
        r1 = _dot(d_y * yoff_scr[...], et)
        r2 = _dot(r2_scr[...], et) * f.dte
        dcd = _dot_onehot(hs_scr[...], et)[0:1]
        d_tot = jnp.sum(r2, axis=0, keepdims=True) + dcd * f.cd
        row = lax.broadcasted_iota(jnp.int32, (BLK, 128), 0)
        d_a = da_scr[...] - dat_scr[...].T + r1 - r2 + jnp.where(row == BLK - 1, d_tot, 0.0)
        dadt = _dot_hi(trit_ref[...], d_a)
        ddt = dadt * f.a + _dot(d_x * f.xs, et)
        lane = lax.broadcasted_iota(jnp.int32, (BLK, 128), 1)
        dr = jnp.where(lane < SSM_HEADS, ddt * _sig(f.dtp), 0.0)
        dvec_ref[0:1, :] += jnp.sum(dr, axis=0, keepdims=True)
        dvec_ref[1:2, :] += jnp.sum(dadt * f.dt, axis=0, keepdims=True) * f.a
        dd_scr[...] += _bcast8(jnp.sum(d_y * f.xs, axis=0, keepdims=True))
        dpm_ref[:, DT_OFF:DT_OFF + 128] = dr.astype(dpm_ref.dtype)
        dpm_ref[:, DT_OFF + 128:SSM_COLS] = jnp.zeros((BLK, 128), dpm_ref.dtype)

        dxs = d_x * f.dt_x + f.d_x * d_y
        dact = jnp.concatenate([dxs, dbc_scr[...]], axis=1)
        dpre = dact * _dsilu(f.pre, f.sg)
        dcb_ref[...] += jnp.sum(dpre, axis=0, keepdims=True)
        xbc = pm_ref[:, 0:CONV_DIM].astype(F32)
        dxraw = jnp.zeros((BLK, CONV_DIM), F32)
        nxt = dcar_ref[...]
        for k in range(4):
            ahead = _shift_up(dpre, nxt, k)
            dcw_ref[3 - k:4 - k, :] += jnp.sum(ahead * xbc, axis=0, keepdims=True)
            dxraw = dxraw + cw[3 - k:4 - k] * ahead
        dcar_ref[...] = dpre[0:16]
        dpm_ref[:, 0:CONV_DIM] = dxraw.astype(dpm_ref.dtype)

        @pl.when(i == nc - 1)
        def _():
            dvec_ref[2:3, :] = _dot_onehot(dd_scr[...], et)[0:1]

    own, hosted = _call_hosting(
        body, "ssd_bwd", nc,
        in_specs=[pl.BlockSpec((BLK, SSM_COLS), lambda i: (nc - 1 - i, 0)),
                  pl.BlockSpec((BLK, CONV_DIM), lambda i: (nc - 1 - i, 0)),
                  pl.BlockSpec((BLK, SSM_WIDTH), lambda i: (nc - 1 - i, 0)),
                  pl.BlockSpec((1, SSM_GROUPS, 128, SSM_GW), lambda i: (nc - 1 - i, 0, 0, 0)),
                  _full((4, CONV_DIM)), _full((1, 128)), _full((1, 128)), _full((1, 128)),
                  _full((1, SSM_WIDTH)), _full((BLK, BLK)), _full((BLK, BLK)), _full((128, SSM_WIDTH)),
                  _full((SSM_WIDTH, 128))],
        out_specs=[pl.BlockSpec((BLK, SSM_COLS), lambda i: (nc - 1 - i, 0)),
                   _full((8, CONV_DIM)), _full((1, CONV_DIM)), _full((8, 128)), _full((1, SSM_WIDTH))],
        out_shape=[jax.ShapeDtypeStruct((s, SSM_COLS), ACT_DTYPE), jax.ShapeDtypeStruct((8, CONV_DIM), F32),
                   jax.ShapeDtypeStruct((1, CONV_DIM), F32), jax.ShapeDtypeStruct((8, 128), F32),
                   jax.ShapeDtypeStruct((1, SSM_WIDTH), F32)],
        scratch=[pltpu.VMEM((SSM_GROUPS, 128, SSM_GW), F32), pltpu.VMEM((16, CONV_DIM), F32),
                 pltpu.VMEM((BLK, SSM_WIDTH), F32), pltpu.VMEM((BLK, SSM_WIDTH), F32),
                 pltpu.VMEM((BLK, SSM_WIDTH), F32), pltpu.VMEM((BLK, SSM_WIDTH), F32),
                 pltpu.VMEM((16, SSM_WIDTH), F32), pltpu.VMEM((BLK, 128), F32), pltpu.VMEM((128, BLK), F32),
                 pltpu.VMEM((16, SSM_WIDTH), F32), pltpu.VMEM((BLK, 1024), F32),
                 pltpu.VMEM((SSM_HEADS, BLK, BLK), F32)],
        args=(pm, pre, dy, states, cw, dtb, alog, dsk, ng, tri, tri_t, e, e_t), ex=ex)
    return (*own, hosted) if ex is not None else tuple(own)


def merge_fwd(x, ya, ys, ym, pg, wa, ws, wm, wo, g_post, target=None):
    s, d = x.shape
    tm = min(256, s)
    with_loss = target is not None

    def body(*refs):
        x_ref, ya_ref, ys_ref, ym_ref, pg_ref, wa_ref, ws_ref, wm_ref, wo_ref, g_ref = refs[:10]
        if with_loss:
            t_ref, xo_ref, l_ref, ba_ref, bs_ref, bm_ref, mg_ref, out_ref = refs[10:]
        else:
            xo_ref, ba_ref, bs_ref, bm_ref, mg_ref, out_ref = refs[10:]
        ba = _dot_tn(ya_ref[...], wa_ref[...])
        bs = _dot(ys_ref[...], ws_ref[...])
        bm = _dot(ym_ref[...], wm_ref[...])
        merged = (_sig(pg_ref[:, 0:d].astype(F32)) * ba + _sig(pg_ref[:, d:2 * d].astype(F32)) * bs
                  + _sig(pg_ref[:, 2 * d:3 * d].astype(F32)) * bm)
        out = _dot(merged, wo_ref[...])
        r = lax.rsqrt(jnp.mean(out * out, axis=-1, keepdims=True) + EPS)
        y = x_ref[...] + out * r * g_ref[...]
        if with_loss:
            @pl.when(pl.program_id(0) == 0)
            def _():
                l_ref[...] = jnp.zeros_like(l_ref)

            err = y - t_ref[...]
            xo_ref[...] = err * (1.0 / d)
            part = jnp.sum(jnp.sum(err * err, axis=-1, keepdims=True) * (1.0 / d), axis=0, keepdims=True)
            l_ref[...] += 0.5 * jnp.broadcast_to(part, l_ref.shape)
        else:
            xo_ref[...] = y
        ba_ref[...] = ba.astype(ba_ref.dtype)
        bs_ref[...] = bs.astype(bs_ref.dtype)
        bm_ref[...] = bm.astype(bm_ref.dtype)
        mg_ref[...] = merged.astype(mg_ref.dtype)
        out_ref[...] = out.astype(out_ref.dtype)

    rows = lambda w: pl.BlockSpec((tm, w), lambda i: (i, 0))
    act = jax.ShapeDtypeStruct((s, d), ACT_DTYPE)
    loss_spec = [_full((8, 128))] if with_loss else []
    loss_shape = [jax.ShapeDtypeStruct((8, 128), F32)] if with_loss else []
    return pl.pallas_call(
        body, name="merge_fwd_loss" if with_loss else "merge_fwd", grid=(s // tm,),
        in_specs=[rows(d), pl.BlockSpec((d, tm), lambda i: (0, i)), rows(d), rows(2 * d), rows(3 * d), _full((d, d)),
                  _full((d, d)), _full((2 * d, d)), _full((d, d)), _full((1, d))] + ([rows(d)] if with_loss else []),
        out_specs=[rows(d)] + loss_spec + [rows(d)] * 5,
        out_shape=[jax.ShapeDtypeStruct((s, d), F32)] + loss_shape + [act] * 5,
        compiler_params=_cparams(("arbitrary" if with_loss else "parallel",)),
    )(x, ya, ys, ym, pg, wa, ws, wm, wo, g_post, *([target] if with_loss else []))


def merge_bwd(dx, out_s, pg, ba, bs, bm, wa, ws, wm, wo, g_post):
    s, d = dx.shape
    tm = min(256, s)

    def body(dx_ref, out_ref, pg_ref, ba_ref, bs_ref, bm_ref, wa_ref, ws_ref, wm_ref, wo_ref, g_ref,
             dout_ref, dba_ref, dbs_ref, dbm_ref, dpg_ref, dya_ref, dys_ref, dym_ref, dg_ref):
        @pl.when(pl.program_id(0) == 0)
        def _():
            dg_ref[...] = jnp.zeros_like(dg_ref)

        o = out_ref[...].astype(F32)
        dxv = dx_ref[...]
        r = lax.rsqrt(jnp.mean(o * o, axis=-1, keepdims=True) + EPS)
        nrm = o * r
        dg_ref[...] += jnp.sum(dxv * nrm, axis=0, keepdims=True)
        dn = dxv * g_ref[...]
        dout = r * (dn - nrm * jnp.mean(dn * nrm, axis=-1, keepdims=True))
        dout_ref[...] = dout.astype(dout_ref.dtype)
        dmerged = _dot_nt(dout, wo_ref[...])
        for q, (b_ref, db_ref, w_ref, dy_ref) in enumerate(((ba_ref, dba_ref, wa_ref, dya_ref),
                                                            (bs_ref, dbs_ref, ws_ref, dys_ref),
                                                            (bm_ref, dbm_ref, wm_ref, dym_ref))):
            gt = _sig(pg_ref[:, q * d:(q + 1) * d].astype(F32))
            db = dmerged * gt
            db_ref[...] = db.astype(db_ref.dtype)
            dpg_ref[:, q * d:(q + 1) * d] = (dmerged * b_ref[...].astype(F32) * gt * (1.0 - gt)).astype(dpg_ref.dtype)
            if q == 0:
                dy_ref[...] = _dot_nt(w_ref[...], db).astype(dy_ref.dtype)
            else:
                dy_ref[...] = _dot_nt(db, w_ref[...]).astype(dy_ref.dtype)

    rows = lambda w: pl.BlockSpec((tm, w), lambda i: (i, 0))
    act = lambda w: jax.ShapeDtypeStruct((s, w), ACT_DTYPE)
    return pl.pallas_call(
        body, name="merge_bwd", grid=(s // tm,),
        in_specs=[rows(d), rows(d), rows(3 * d), rows(d), rows(d), rows(d), _full((d, d)), _full((d, d)),
                  _full((2 * d, d)), _full((d, d)), _full((1, d))],
        out_specs=[rows(d), rows(d), rows(d), rows(d), rows(3 * d), pl.BlockSpec((d, tm), lambda i: (0, i)), rows(d),
                   rows(2 * d), _full((1, d))],
        out_shape=[act(d), act(d), act(d), act(d), act(3 * d), jax.ShapeDtypeStruct((d, s), ACT_DTYPE), act(d),
                   act(2 * d), jax.ShapeDtypeStruct((1, d), F32)],
        compiler_params=_cparams(("arbitrary",)),
    )(dx, out_s, pg, ba, bs, bm, wa, ws, wm, wo, g_post)


def _mesh_pos():
    x, y, c = lax.axis_index("x"), lax.axis_index("y"), lax.axis_index("c")
    return x, y, c, 4 * x + 2 * y + c


def _peer(x, y, c, k):
    px = 1 - x if k & 4 else x
    py = 1 - y if k & 2 else y
    pc = 1 - c if k & 1 else c
    return (px, py, pc), 4 * px + 2 * py + pc


class Exchange:
    SAME_CORE = (2, 4, 6)

    def __init__(self, scattered, gathered):
        self.ns = len(scattered)
        self.arrays = list(scattered) + list(gathered)
        self.na = len(self.arrays)
        any_spec = pl.BlockSpec(memory_space=pl.ANY)
        self.in_specs = [any_spec] * self.na
        self.out_specs = [any_spec] * self.na
        self.out_shape = ([jax.ShapeDtypeStruct(a.shape, a.dtype) for a in scattered]
                          + [jax.ShapeDtypeStruct((N_DEV,) + a.shape, a.dtype) for a in gathered])
        self.scratch = [pltpu.SemaphoreType.DMA((self.na, N_DEV - 1)), pltpu.SemaphoreType.DMA((self.na, N_DEV - 1)),
                        pltpu.SemaphoreType.DMA((self.na,))]

    def _src(self, ins, q, slot):
        return ins[q].at[slot] if q < self.ns else ins[q]

    def _local(self, ins, outs, sems):
        me = _mesh_pos()[3]
        return [pltpu.make_async_copy(self._src(ins, q, me), outs[q].at[me], sems[2].at[q]) for q in range(self.na)]

    def _direct(self, ins, outs, sems, relations, arrays):
        x, y, c, me = _mesh_pos()
        copies = []
        for k in relations:
            peer, pidx = _peer(x, y, c, k)
            for q in arrays:
                copies.append(pltpu.make_async_remote_copy(
                    src_ref=self._src(ins, q, pidx), dst_ref=outs[q].at[me], send_sem=sems[0].at[q, k - 1],
                    recv_sem=sems[1].at[q, k - 1], device_id=peer, device_id_type=MESH))
        return copies

    def _arrivals(self, ins, outs, sems, relations, arrays):
        x, y, c, _ = _mesh_pos()
        copies = []
        for k in relations:
            peer, pidx = _peer(x, y, c, k)
            for q in arrays:
                copies.append(pltpu.make_async_remote_copy(
                    src_ref=self._src(ins, q, pidx), dst_ref=outs[q].at[pidx], send_sem=sems[0].at[q, k - 1],
                    recv_sem=sems[1].at[q, k - 1], device_id=peer, device_id_type=MESH))
        return copies

    def _relays(self, outs, sems):
        x, y, c, _ = _mesh_pos()
        sibling, _ = _peer(x, y, c, 1)
        copies = []
        for k in self.SAME_CORE:
            _, pidx = _peer(x, y, c, k)
            for q in range(self.ns, self.na):
                copies.append(pltpu.make_async_remote_copy(
                    src_ref=outs[q].at[pidx], dst_ref=outs[q].at[pidx], send_sem=sems[0].at[q, k],
                    recv_sem=sems[1].at[q, k], device_id=sibling, device_id_type=MESH))
        return copies

    def _sends(self, ins, outs, sems):
        return (self._direct(ins, outs, sems, range(1, N_DEV), range(self.ns))
                + self._direct(ins, outs, sems, (1,) + self.SAME_CORE, range(self.ns, self.na)))

    def start(self, ins, outs, sems):
        for cp in self._local(ins, outs, sems) + self._sends(ins, outs, sems):
            cp.start()

    def relay(self, ins, outs, sems):
        for cp in self._arrivals(ins, outs, sems, self.SAME_CORE, range(self.ns, self.na)):
            cp.wait_recv()
        for cp in self._relays(outs, sems):
            cp.start()

    def wait(self, ins, outs, sems):
        for cp in (self._arrivals(ins, outs, sems, range(1, N_DEV), range(self.ns))
                   + self._arrivals(ins, outs, sems, (1, 3, 5, 7), range(self.ns, self.na))):
            cp.wait_recv()
        for cp in self._sends(ins, outs, sems) + self._relays(outs, sems):
            cp.wait_send()
        for cp in self._local(ins, outs, sems):
            cp.wait()


def adamw(parts_list, w, m, v, tile, name, ex=None):
    npart, _, dp = parts_list[0].shape
    d = w.shape[-1]
    counts = [p.shape[1] // tile for p in parts_list]
    starts = [sum(counts[:q]) for q in range(len(counts))]
    n_lists = len(parts_list)

    def body(*refs):
        p_refs = refs[:n_lists]
        w_ref, m_ref, v_ref, g_ref, dw_ref, nm_ref, nv_ref = refs[n_lists:]
        i = pl.program_id(0)
        for q, p_ref in enumerate(p_refs):
            @pl.when((i >= starts[q]) & (i < starts[q] + counts[q]))
            def _(p_ref=p_ref):
                acc = p_ref[0, :, 0:d].astype(F32)
                for k in range(1, npart):
                    acc = acc + p_ref[k, :, 0:d].astype(F32)
                g_ref[...] = acc

        g = g_ref[...]
        nm = ADAM_B1 * m_ref[...] + (1.0 - ADAM_B1) * g
        nv = ADAM_B2 * v_ref[...] + (1.0 - ADAM_B2) * (g * g)
        nm_ref[...] = nm
        nv_ref[...] = nv
        m_hat = nm / (1.0 - ADAM_B1 ** ADAM_STEP)
        v_hat = nv / (1.0 - ADAM_B2 ** ADAM_STEP)
        dw_ref[...] = -ADAM_LR * (m_hat / (jnp.sqrt(v_hat) + ADAM_EPS) + ADAM_WD * w_ref[...])

    def part_rows(q):
        return lambda i: (0, jnp.clip(i - starts[q], 0, counts[q] - 1), 0)

    if w.ndim == 3:
        rows = pl.BlockSpec((None, tile, d), lambda i: (i // counts[0], i % counts[0], 0))
    else:
        rows = pl.BlockSpec((tile, d), lambda i: (i, 0))
    own, hosted = _call_hosting(
        body, name, sum(counts),
        in_specs=[pl.BlockSpec((npart, tile, dp), part_rows(q)) for q in range(n_lists)] + [rows, rows, rows],
        out_specs=[rows] * 4, out_shape=[jax.ShapeDtypeStruct(w.shape, F32)] * 4, scratch=[],
        args=(*parts_list, w, m, v), ex=ex)
    return (*own, hosted) if ex is not None else tuple(own)


def _pad_rows(a, rows):
    return jnp.pad(a, ((0, rows - a.shape[0]), (0, 0)))


def _pack_rest(w_att, w_sg, w_ssm, w_out):
    parts = []
    for l in range(2):
        parts += [w_att[l], w_sg[l], w_ssm[l], w_out[l]]
    return jnp.concatenate(parts, axis=0)


def _unpack_rest(p):
    outs = [[], [], [], []]
    o = 0
    for l in range(2):
        for q, rws in enumerate(REST_PARTS):
            outs[q].append(p[o:o + rws])
            o += rws
    return [jnp.stack(t) for t in outs]


def _pack_win(w_in):
    return jnp.pad(w_in.reshape(2 * D_MODEL, WIN_SHARD), ((0, 0), (0, WIN_LANES - WIN_SHARD)))


W_IN_MAP = ((0, 1024, "att", 0), (1024, 1280, "att", 2048), (1280, 2304, "att", 1024), (2304, 5376, "sg", 0),
            (5376, 7424, "ssm", 3072), (7424, 10496, "ssm", 0), (10496, 10528, "ssm", 5120), (10528, 13600, "gate", 0))
SLAB_COLS = {"att": ATT_COLS, "sg": SG_COLS, "ssm": SSM_COLS, "gate": GATE_COLS}


def _slab_pieces(name):
    pieces, filled = [], 0
    for ga, gb, _, off in sorted((m for m in W_IN_MAP if m[2] == name), key=lambda m: m[3]):
        assert off == filled
        a = ga
        while a < gb:
            d = a // WIN_SHARD
            hi = min(gb, WIN_SHARD * (d + 1))
            pieces.append((d, a - WIN_SHARD * d, hi - WIN_SHARD * d))
            a = hi
        filled += gb - ga
    return pieces, filled


def _slabs_from_shards(g):
    names = tuple(SLAB_COLS)
    tr = 256

    def body(g_ref, *out_refs):
        for name, o_ref in zip(names, out_refs):
            pieces, filled = _slab_pieces(name)
            cols = [g_ref[d, :, lo:hi].astype(F32) for d, lo, hi in pieces]
            if filled < SLAB_COLS[name]:
                cols.append(jnp.zeros((tr, SLAB_COLS[name] - filled), F32))
            o_ref[...] = jnp.concatenate(cols, axis=1).astype(o_ref.dtype)

    outs = pl.pallas_call(
        body, name="slabs_from_shards", grid=(D_MODEL // tr,),
        in_specs=[pl.BlockSpec((N_DEV, tr, WIN_LANES), lambda i: (0, i, 0))],
        out_specs=[pl.BlockSpec((tr, SLAB_COLS[n]), lambda i: (i, 0)) for n in names],
        out_shape=[jax.ShapeDtypeStruct((D_MODEL, SLAB_COLS[n]), MXU_DTYPE) for n in names],
        compiler_params=_cparams(("parallel",)),
    )(g)
    return dict(zip(names, outs))


def _shards_from_slabs(dslabs):
    names = tuple(SLAB_COLS)
    tr = 256

    def body(*refs):
        in_refs, o_ref = dict(zip(names, refs[:-1])), refs[-1]
        for d in range(N_DEV):
            a, b = WIN_SHARD * d, WIN_SHARD * (d + 1)
            cols = []
            for ga, gb, name, off in W_IN_MAP:
                lo, hi = max(a, ga), min(b, gb)
                if lo < hi:
                    cols.append(in_refs[name][:, off + lo - ga:off + hi - ga].astype(F32))
            cols.append(jnp.zeros((tr, WIN_LANES - WIN_SHARD), F32))
            o_ref[d] = jnp.concatenate(cols, axis=1).astype(o_ref.dtype)

    return pl.pallas_call(
        body, name="shards_from_slabs", grid=(D_MODEL // tr,),
        in_specs=[pl.BlockSpec((tr, SLAB_COLS[n]), lambda i: (i, 0)) for n in names],
        out_specs=pl.BlockSpec((N_DEV, tr, WIN_LANES), lambda i: (0, i, 0)),
        out_shape=jax.ShapeDtypeStruct((N_DEV, D_MODEL, WIN_LANES), WIRE_DTYPE),
        compiler_params=_cparams(("parallel",)),
    )(*[dslabs[n] for n in names])


SMALL_SIZES = (("norm_pre", 2048), ("norm_post", 2048), ("rel_bias", 512), ("att_sinks", 32), ("sg_ln_g", 2048),
               ("sg_ln_b", 2048), ("sg_w", 262144), ("sg_b", 2048), ("ssm_conv_b", 6144), ("ssm_dt_bias", 64),
               ("ssm_a_log", 64), ("ssm_d", 64), ("ssm_norm_g", 4096), ("conv_w_full", 24576))


def _pack_small(d):
    parts = []
    for name, size in SMALL_SIZES:
        rows = 8 * (-(-size // (8 * D_MODEL)))
        flat = d[name].reshape(-1) if name in d else jnp.zeros((size,), F32)
        parts.append(jnp.pad(flat, (0, rows * D_MODEL - size)).reshape(rows, D_MODEL))
    return _pad_rows(jnp.concatenate(parts, axis=0), SMALL_ROWS)


def _unpack_small(p, shapes):
    out, o = {}, 0
    for name, size in SMALL_SIZES:
        rows = 8 * (-(-size // (8 * D_MODEL)))
        if name in shapes:
            out[name] = p[o:o + rows].reshape(-1)[:size].reshape(shapes[name])
        o += rows
    return out


def _bucket_onehot_t():
    qi = jnp.arange(BLK, dtype=jnp.int32)[None, :]
    kj = jnp.arange(BLK, dtype=jnp.int32)[:, None]
    dd = (qi - kj) & (BLK - 1)
    in_window = dd >= 0
    max_exact = REL_BUCKETS // 2
    dist_f = jnp.maximum(dd, 1).astype(F32)
    large = max_exact + (jnp.log(dist_f / max_exact) / math.log(128 / max_exact)
                         * (REL_BUCKETS - max_exact)).astype(jnp.int32)
    large = jnp.minimum(large, REL_BUCKETS - 1)
    bucket = jnp.where(dd < max_exact, dd, large).reshape(1, -1)
    onehot_t = (bucket == jnp.arange(REL_BUCKETS, dtype=jnp.int32)[:, None]).astype(F32)
    maskadd = jnp.where(in_window, 0.0, NEG).astype(F32).reshape(1, -1)
    return onehot_t, maskadd


WEIGHTS = ['w_in', 'norm_pre', 'norm_post', 'rel_bias', 'att_sinks', 'sg_ln_g', 'sg_ln_b', 'sg_w', 'sg_b',
           'ssm_conv_w', 'ssm_conv_b', 'ssm_dt_bias', 'ssm_a_log', 'ssm_d', 'ssm_norm_g',
           'w_br_att', 'w_br_sg', 'w_br_ssm', 'w_out']
REST = ('w_br_att', 'w_br_sg', 'w_br_ssm', 'w_out')


def kernel(x, w_in, norm_pre, norm_post, rel_bias, att_sinks, sg_ln_g, sg_ln_b, sg_w, sg_b, ssm_conv_w, ssm_conv_b, ssm_dt_bias, ssm_a_log, ssm_d, ssm_norm_g, w_br_att, w_br_sg, w_br_ssm, w_out, loss_target, m_w_in, m_norm_pre, m_norm_post, m_rel_bias, m_att_sinks, m_sg_ln_g, m_sg_ln_b, m_sg_w, m_sg_b, m_ssm_conv_w, m_ssm_conv_b, m_ssm_dt_bias, m_ssm_a_log, m_ssm_d, m_ssm_norm_g, m_w_br_att, m_w_br_sg, m_w_br_ssm, m_w_out, v_w_in, v_norm_pre, v_norm_post, v_rel_bias, v_att_sinks, v_sg_ln_g, v_sg_ln_b, v_sg_w, v_sg_b, v_ssm_conv_w, v_ssm_conv_b, v_ssm_dt_bias, v_ssm_a_log, v_ssm_d, v_ssm_norm_g, v_w_br_att, v_w_br_sg, v_w_br_ssm, v_w_out):
    w = dict(w_in=w_in, norm_pre=norm_pre, norm_post=norm_post, rel_bias=rel_bias, att_sinks=att_sinks,
             sg_ln_g=sg_ln_g, sg_ln_b=sg_ln_b, sg_w=sg_w, sg_b=sg_b, ssm_conv_w=ssm_conv_w, ssm_conv_b=ssm_conv_b,
             ssm_dt_bias=ssm_dt_bias, ssm_a_log=ssm_a_log, ssm_d=ssm_d, ssm_norm_g=ssm_norm_g,
             w_br_att=w_br_att, w_br_sg=w_br_sg, w_br_ssm=w_br_ssm, w_out=w_out)
    mom = dict(w_in=m_w_in, norm_pre=m_norm_pre, norm_post=m_norm_post, rel_bias=m_rel_bias, att_sinks=m_att_sinks,
               sg_ln_g=m_sg_ln_g, sg_ln_b=m_sg_ln_b, sg_w=m_sg_w, sg_b=m_sg_b, ssm_conv_w=m_ssm_conv_w,
               ssm_conv_b=m_ssm_conv_b, ssm_dt_bias=m_ssm_dt_bias, ssm_a_log=m_ssm_a_log, ssm_d=m_ssm_d,
               ssm_norm_g=m_ssm_norm_g, w_br_att=m_w_br_att, w_br_sg=m_w_br_sg, w_br_ssm=m_w_br_ssm, w_out=m_w_out)
    var = dict(w_in=v_w_in, norm_pre=v_norm_pre, norm_post=v_norm_post, rel_bias=v_rel_bias, att_sinks=v_att_sinks,
               sg_ln_g=v_sg_ln_g, sg_ln_b=v_sg_ln_b, sg_w=v_sg_w, sg_b=v_sg_b, ssm_conv_w=v_ssm_conv_w,
               ssm_conv_b=v_ssm_conv_b, ssm_dt_bias=v_ssm_dt_bias, ssm_a_log=v_ssm_a_log, ssm_d=v_ssm_d,
               ssm_norm_g=v_ssm_norm_g, w_br_att=v_w_br_att, w_br_sg=v_w_br_sg, w_br_ssm=v_w_br_ssm, w_out=v_w_out)
    xs0 = x[0]
    target = loss_target[0]
    my_dev = 4 * lax.axis_index("x") + 2 * lax.axis_index("y") + lax.axis_index("c")

    conv_shard = _pad_rows(ssm_conv_w.reshape(-1, D_MODEL), 8)
    win_shard = _pack_win(w_in).astype(WIRE_DTYPE)
    rest_shard = _pack_rest(*[w[n] for n in REST]).astype(WIRE_DTYPE)
    layer_shards = [[win_shard[l * D_MODEL:(l + 1) * D_MODEL], rest_shard[l * LAYER_REST:(l + 1) * LAYER_REST]]
                    for l in range(2)]
    h0, (g_win0, gathered_conv) = rmsnorm_fwd(xs0, norm_pre[0][None], Exchange([], [layer_shards[0][0], conv_shard]))
    conv_full = gathered_conv[:, 0:3].reshape(N_DEV, 2, 4, 384).transpose(1, 2, 0, 3).reshape(2, 4, CONV_DIM)

    def set_rest(lw, g_rest):
        o = 0
        for name, rws in zip(("att", "sg", "ssm", "out"), REST_PARTS):
            lw[name] = g_rest[:, o:o + rws].reshape(N_DEV * rws, D_MODEL).astype(MXU_DTYPE)
            o += rws

    def layer_weights(l, g_win):
        slabs = _slabs_from_shards(g_win)
        lw = {"in_" + name: slab.astype(MXU_DTYPE) for name, slab in slabs.items()}
        lw["in_att"] = lw["in_att"].T
        tril = jnp.tril(jnp.ones((BLK, BLK), bool))
        sgw = jnp.where(tril[None], sg_w[l], 0.0)
        lw.update(
            g_pre=norm_pre[l][None], g_post=norm_post[l][None], sinks=jnp.repeat(att_sinks[l], BLK).reshape(2, GROUP_LANES),
            ln_g=sg_ln_g[l][None], ln_b=sg_ln_b[l][None], sgw=sgw.astype(MXU_DTYPE),
            sgw_t=sgw.transpose(0, 2, 1).astype(MXU_DTYPE), sgb_t=_pad_lanes(sg_b[l].T),
            cw=conv_full[l], cb=ssm_conv_b[l][None], dtb=_pad_lanes(ssm_dt_bias[l][None]),
            alog=_pad_lanes(ssm_a_log[l][None]), dsk=_pad_lanes(ssm_d[l][None]), ng=ssm_norm_g[l][None])
        return lw

    onehot_t, maskadd = _bucket_onehot_t()
    bias = bias_table(rel_bias.T, onehot_t, maskadd).reshape(2, GROUP_HEADS, BLK, BLK).transpose(0, 2, 1, 3)
    bias = bias.reshape(2, BLK, GROUP_LANES)

    saved = []
    xl = xs0
    layers = [layer_weights(0, g_win0)]
    for l in range(2):
        lw = layers[l]
        h = h0 if l == 0 else rmsnorm_fwd(xl, lw["g_pre"])
        pa = mm_nt(lw["in_att"], h, 1152, "proj_att")
        ps = mm_nn(h, lw["in_sg"], 1536, "proj_sg")
        pm = mm_nn(h, lw["in_ssm"], 1792, "proj_ssm")
        pg = mm_nn(h, lw["in_gate"], 1536, "proj_gate")
        if l == 0:
            ya, (g_rest0,) = attn_fwd(pa, bias, lw["sinks"], Exchange([], [layer_shards[0][1]]))
            set_rest(lw, g_rest0)
        else:
            ya = attn_fwd(pa, bias, lw["sinks"])
        sgu_args = (ps, lw["ln_g"], lw["ln_b"], lw["sgw"], lw["sgb_t"])
        ssd_args = (pm, lw["cw"], lw["cb"], lw["dtb"], lw["alog"], lw["dsk"], lw["ng"])
        ys = sgu_fwd(*sgu_args)
        if l == 0:
            ym, states, conv_pre, (g_win1, g_rest1) = ssd_fwd(*ssd_args, Exchange([], layer_shards[1]))
            layers.append(layer_weights(1, g_win1))
            set_rest(layers[1], g_rest1)
        else:
            ym, states, conv_pre = ssd_fwd(*ssd_args)
        merge_args = (xl, ya, ys, ym, pg, lw["att"], lw["sg"], lw["ssm"], lw["out"], lw["g_post"])
        if l == 0:
            x_next, ba, bs, bm, merged, out_s = merge_fwd(*merge_args)
        else:
            dx, loss_part, ba, bs, bm, merged, out_s = merge_fwd(*merge_args, target)
        saved.append(dict(x=xl, h=h, pa=pa, ps=ps, pm=pm, pg=pg, ya=ya, ys=ys, ym=ym, states=states,
                          conv_pre=conv_pre, ba=ba, bs=bs, bm=bm, merged=merged, out_s=out_s))
        xl = x_next

    loss = lax.psum(loss_part[0, 0], ("x", "y", "c"))

    dbias = jnp.zeros((2, BLK, GROUP_LANES), F32)
    win_grads, rest_grads = [None, None], [None, None]
    small = {n: [None, None] for n in ("norm_pre", "norm_post", "att_sinks", "sg_ln_g", "sg_ln_b", "sg_w", "sg_b",
                                       "ssm_conv_b", "ssm_dt_bias", "ssm_a_log", "ssm_d", "ssm_norm_g",
                                       "conv_w_full")}
    for l in (1, 0):
        lw, sv = layers[l], saved[l]
        dout, dba, dbs, dbm, dpg, dya, dys, dym, dg_post = merge_bwd(
            dx, sv["out_s"], sv["pg"], sv["ba"], sv["bs"], sv["bm"], lw["att"], lw["sg"], lw["ssm"], lw["out"],
            lw["g_post"])
        dw_out = mm_tn(sv["merged"], dout, 1024, "dw_out")
        dw_att = mm_kn(sv["ya"], dba, 1024, "dw_br_att")
        dw_sg = mm_tn(sv["ys"], dbs, 1024, "dw_br_sg")
        dw_ssm = mm_tn(sv["ym"], dbm, 1024, "dw_br_ssm")
        rest_grads[l] = jnp.concatenate(
            [dw_att.reshape(N_DEV, 128, D_MODEL), dw_sg.reshape(N_DEV, 128, D_MODEL),
             dw_ssm.reshape(N_DEV, 256, D_MODEL), dw_out.reshape(N_DEV, 128, D_MODEL)], axis=1).astype(WIRE_DTYPE)
        dpa, dbias, dsinks = attn_bwd(sv["pa"], dya, bias, lw["sinks"], dbias)
        dps, dsgw, dsgb_t, dln_g, dln_b = sgu_bwd(sv["ps"], dys, lw["ln_g"], lw["ln_b"], lw["sgw"], lw["sgw_t"],
                                                  lw["sgb_t"])
        ssd_args = (sv["pm"], sv["conv_pre"], dym, sv["states"], lw["cw"], lw["dtb"], lw["alog"], lw["dsk"], lw["ng"])
        if l == 0:
            dpm, dcw, dcb, dvec, dng, (recv_win1, recv_rest0) = ssd_bwd(
                *ssd_args, Exchange([win_grads[1], rest_grads[0]], []))
        else:
            dpm, dcw, dcb, dvec, dng, (recv_rest1,) = ssd_bwd(*ssd_args, Exchange([rest_grads[1]], []))
        dslabs = dict(att=mm_kn(dpa, sv["h"], 1152, "dw_in_att").T, sg=mm_tn(sv["h"], dps, 3072, "dw_in_sg"),
                      ssm=mm_tn(sv["h"], dpm, 2688, "dw_in_ssm"), gate=mm_tn(sv["h"], dpg, 3072, "dw_in_gate"))
        win_grads[l] = _shards_from_slabs(dslabs)
        dh_args = ([dpa, dps, dpm, dpg], [lw["in_att"], lw["in_sg"], lw["in_ssm"], lw["in_gate"]], sv["x"],
                   lw["g_pre"], dx)
        if l == 0:
            dx, dg_pre, (recv_win0,) = dh_norm_bwd(*dh_args, Exchange([win_grads[0]], []))
        else:
            dx, dg_pre = dh_norm_bwd(*dh_args)
        small["norm_pre"][l] = dg_pre[0]
        small["norm_post"][l] = dg_post[0]
        small["att_sinks"][l] = dsinks[0, :ATT_HEADS]
        small["sg_ln_g"][l] = dln_g[0]
        small["sg_ln_b"][l] = dln_b[0]
        small["sg_w"][l] = dsgw
        small["sg_b"][l] = dsgb_t[:, :SG_GROUPS].T
        small["ssm_conv_b"][l] = dcb[0]
        small["ssm_dt_bias"][l] = dvec[0, :SSM_HEADS]
        small["ssm_a_log"][l] = dvec[1, :SSM_HEADS]
        small["ssm_d"][l] = dvec[2, :SSM_HEADS]
        small["ssm_norm_g"][l] = dng[0]
        small["conv_w_full"][l] = dcw[0:4]
    grad_x = dx
    dbias = dbias.reshape(2, BLK, GROUP_HEADS, BLK).transpose(0, 2, 1, 3).reshape(ATT_HEADS, BLK * BLK)
    d_rel_bias = bias_table_bwd(dbias, onehot_t).T

    small_d = {n: jnp.stack(v) for n, v in small.items()}
    small_d["rel_bias"] = d_rel_bias
    *res_win, (recv_small,) = adamw([recv_win0, recv_win1], w_in, m_w_in, v_w_in, WIN_TILE, "adamw_w_in",
                                    Exchange([], [_pack_small(small_d)]))
    res_rest = adamw([recv_rest0, recv_rest1], _pack_rest(*[w[n] for n in REST]), _pack_rest(*[mom[n] for n in REST]),
                     _pack_rest(*[var[n] for n in REST]), REST_TILE, "adamw_rest")
    small_names = [n for n, _ in SMALL_SIZES if n != "conv_w_full"]
    g_s, dw_s, nm_s, nv_s = adamw([recv_small], _pack_small({n: w[n] for n in small_names}),
                                  _pack_small({n: mom[n] for n in small_names}),
                                  _pack_small({n: var[n] for n in small_names}), SMALL_TILE, "adamw_small")
    shapes = {n: w[n].shape for n in small_names}
    shapes["conv_w_full"] = (2, 4, CONV_DIM)
    g_conv_full = _unpack_small(g_s, shapes)["conv_w_full"]
    g_conv = lax.dynamic_slice_in_dim(g_conv_full, my_dev * 384, 384, axis=2)
    pack_conv = lambda a: _pad_rows(a.reshape(-1, D_MODEL), 8)
    g_c, dw_c, nm_c, nv_c = adamw([pack_conv(g_conv)[None]], pack_conv(ssm_conv_w), pack_conv(m_ssm_conv_w),
                                  pack_conv(v_ssm_conv_w), 8, "adamw_conv")

    results = {}
    for q, (tag, psm, pc) in enumerate((("grad", g_s, g_c), ("delta", dw_s, dw_c), ("new_m", nm_s, nm_c),
                                        ("new_v", nv_s, nv_c))):
        r = dict(zip(REST, _unpack_rest(res_rest[q])))
        r["w_in"] = res_win[q]
        r.update(_unpack_small(psm, {n: w[n].shape for n in small_names}))
        r["ssm_conv_w"] = pc[0:3].reshape(2, 4, 384)
        results[tag] = r
    outs = [loss, grad_x[None]]
    for tag in ("grad", "delta", "new_m", "new_v"):
        outs += [results[tag][n] for n in WEIGHTS]
    return tuple(outs)
```

```python
import math

import jax
import jax.numpy as jnp
from jax import lax
from jax.experimental import pallas as pl
from jax.experimental.pallas import tpu as pltpu

F32 = jnp.float32
MXU_DTYPE = jnp.bfloat16
ACT_DTYPE = jnp.bfloat16
WIRE_DTYPE = jnp.bfloat16
HI = lax.Precision.HIGHEST
MESH = pl.DeviceIdType.MESH

D_MODEL = 1024
N_DEV = 8
ATT_HEADS = 16
HEAD_DIM = 64
BLK = 128
SG_GROUPS = 8
SSM_WIDTH = 2048
SSM_HEADS = 32
SSM_GROUPS = 4
SSM_GW = SSM_WIDTH // SSM_GROUPS
CONV_DIM = 3072
REL_BUCKETS = 32
EPS = 1e-6
NEG = -1e30

ATT_COLS = 2304
SG_COLS = 3072
SSM_COLS = 5376
GATE_COLS = 3072
DT_OFF = 5120

VMEM_LIMIT_V7X = 56 * 2 ** 20
DH_TK = 768

ADAM_LR, ADAM_B1, ADAM_B2, ADAM_EPS, ADAM_WD, ADAM_STEP = 0.001, 0.9, 0.999, 1e-08, 0.01, 10

WIN_SHARD = 1700
WIN_LANES = 1792
REST_PARTS = (128, 128, 256, 128)
LAYER_REST = sum(REST_PARTS)
REST_TILE = 128
WIN_TILE = 128
SMALL_ROWS = 384
SMALL_TILE = 128


def _cparams(sem=None):
    return pltpu.CompilerParams(dimension_semantics=sem, vmem_limit_bytes=VMEM_LIMIT_V7X)


def _dot(a, b):
    return jnp.dot(a.astype(MXU_DTYPE), b.astype(MXU_DTYPE), preferred_element_type=F32)


def _dot_nt(a, b):
    return lax.dot_general(a.astype(MXU_DTYPE), b.astype(MXU_DTYPE), (((1,), (1,)), ((), ())),
                           preferred_element_type=F32)


def _dot_tn(a, b):
    return lax.dot_general(a.astype(MXU_DTYPE), b.astype(MXU_DTYPE), (((0,), (0,)), ((), ())),
                           preferred_element_type=F32)


def _dot_hi(a, b):
    return jnp.dot(a, b, precision=HI, preferred_element_type=F32)


def _dot_onehot(a, onehot):
    hi = a.astype(jnp.bfloat16)
    lo = (a - hi.astype(F32)).astype(jnp.bfloat16)
    return (jnp.dot(hi, onehot, preferred_element_type=F32) + jnp.dot(lo, onehot, preferred_element_type=F32))


def _dot_hi_nt(a, b):
    return lax.dot_general(a, b, (((1,), (1,)), ((), ())), precision=HI, preferred_element_type=F32)


def _sig(x):
    return 1.0 / (1.0 + jnp.exp(-x))


def _dsilu(x, s):
    return s * (1.0 + x * (1.0 - s))


def _full(shape):
    nd = len(shape)
    return pl.BlockSpec(shape, lambda *_: (0,) * nd)


def rmsnorm_fwd(x, g, ex=None):
    s, d = x.shape
    tm = min(512, s)

    def body(x_ref, g_ref, o_ref):
        xv = x_ref[...]
        r = lax.rsqrt(jnp.mean(xv * xv, axis=-1, keepdims=True) + EPS)
        o_ref[...] = (xv * r * g_ref[...]).astype(o_ref.dtype)

    (h,), hosted = _call_hosting(
        body, "rmsnorm_fwd", s // tm,
        in_specs=[pl.BlockSpec((tm, d), lambda i: (i, 0)), _full((1, d))],
        out_specs=[pl.BlockSpec((tm, d), lambda i: (i, 0))],
        out_shape=[jax.ShapeDtypeStruct((s, d), ACT_DTYPE)], scratch=[], args=(x, g), ex=ex)
    return (h, hosted) if ex is not None else h


def mm_nn(a, b, tn, name):
    s, k = a.shape
    n = b.shape[1]
    tm = min(2048, s)

    def body(a_ref, b_ref, o_ref):
        o_ref[...] = _dot(a_ref[...], b_ref[...]).astype(o_ref.dtype)

    return pl.pallas_call(
        body, name=name, grid=(s // tm, n // tn),
        in_specs=[pl.BlockSpec((tm, k), lambda i, j: (i, 0)), pl.BlockSpec((k, tn), lambda i, j: (0, j))],
        out_specs=pl.BlockSpec((tm, tn), lambda i, j: (i, j)),
        out_shape=jax.ShapeDtypeStruct((s, n), ACT_DTYPE),
        compiler_params=_cparams(("parallel", "arbitrary")),
    )(a, b)


def mm_nt(a, b, tm, name):
    m, k = a.shape
    s = b.shape[0]
    ts = min(2048, s)

    def body(a_ref, b_ref, o_ref):
        o_ref[...] = _dot_nt(a_ref[...], b_ref[...]).astype(o_ref.dtype)

    return pl.pallas_call(
        body, name=name, grid=(s // ts, m // tm),
        in_specs=[pl.BlockSpec((tm, k), lambda i, j: (j, 0)), pl.BlockSpec((ts, k), lambda i, j: (i, 0))],
        out_specs=pl.BlockSpec((tm, ts), lambda i, j: (j, i)),
        out_shape=jax.ShapeDtypeStruct((m, s), ACT_DTYPE),
        compiler_params=_cparams(("parallel", "arbitrary")),
    )(a, b)


def mm_kn(a, b, tm, name):
    m, s = a.shape
    n = b.shape[1]
    ts = min(512, s)
    nt = s // ts

    def body(a_ref, b_ref, o_ref, acc_ref):
        @pl.when(pl.program_id(1) == 0)
        def _():
            acc_ref[...] = jnp.zeros_like(acc_ref)

        acc_ref[...] += _dot(a_ref[...], b_ref[...])

        @pl.when(pl.program_id(1) == nt - 1)
        def _():
            o_ref[...] = acc_ref[...].astype(o_ref.dtype)

    return pl.pallas_call(
        body, name=name, grid=(m // tm, nt),
        in_specs=[pl.BlockSpec((tm, ts), lambda j, t: (j, t)), pl.BlockSpec((ts, n), lambda j, t: (t, 0))],
        out_specs=pl.BlockSpec((tm, n), lambda j, t: (j, 0)),
        out_shape=jax.ShapeDtypeStruct((m, n), WIRE_DTYPE),
        scratch_shapes=[pltpu.VMEM((tm, n), F32)],
        compiler_params=_cparams(("parallel", "arbitrary")),
    )(a, b)


def mm_tn(a, b, tn, name):
    s, k = a.shape
    n = b.shape[1]
    ts = min(512, s)
    nt = s // ts

    def body(a_ref, b_ref, o_ref, acc_ref):
        @pl.when(pl.program_id(1) == 0)
        def _():
            acc_ref[...] = jnp.zeros_like(acc_ref)

        acc_ref[...] += _dot_tn(a_ref[...], b_ref[...])

        @pl.when(pl.program_id(1) == nt - 1)
        def _():
            o_ref[...] = acc_ref[...].astype(o_ref.dtype)

    return pl.pallas_call(
        body, name=name, grid=(n // tn, nt),
        in_specs=[pl.BlockSpec((ts, k), lambda j, t: (t, 0)), pl.BlockSpec((ts, tn), lambda j, t: (t, j))],
        out_specs=pl.BlockSpec((k, tn), lambda j, t: (0, j)),
        out_shape=jax.ShapeDtypeStruct((k, n), WIRE_DTYPE),
        scratch_shapes=[pltpu.VMEM((k, tn), F32)],
        compiler_params=_cparams(("parallel", "arbitrary")),
    )(a, b)


def dh_norm_bwd(dslabs, wslabs, x, g, dres, ex=None):
    s, d = x.shape
    tm = min(1024, s)
    widths = [ds.shape[0 if q == 0 else 1] for q, ds in enumerate(dslabs)]
    tks = [DH_TK] * len(widths)
    counts = [wd // t for wd, t in zip(widths, tks)]
    starts = [sum(counts[:i]) for i in range(len(counts))]
    nk = sum(counts)
    ns = len(dslabs)

    hosted = ex is not None
    ni = s // tm

    def mm_body(*refs):
        (own_in, (dh_ref,), _), hosted_refs = _split_hosted(refs, 2 * ns, 1, 0, ex)
        d_refs, w_refs = own_in[:ns], own_in[ns:]
        i, k = pl.program_id(0), pl.program_id(1)
        if hosted:
            @pl.when((i == 0) & (k == 0))
            def _():
                ex.start(*hosted_refs)

            @pl.when((i == ni - 1) & (k == nk - 1))
            def _():
                ex.relay(*hosted_refs)
                ex.wait(*hosted_refs)

        @pl.when(k == 0)
        def _():
            dh_ref[...] = jnp.zeros_like(dh_ref)

        for q in range(ns):
            @pl.when((k >= starts[q]) & (k < starts[q] + counts[q]))
            def _(q=q):
                if q == 0:
                    dh_ref[...] += _dot_tn(d_refs[q][...], w_refs[q][...])
                else:
                    dh_ref[...] += _dot_nt(d_refs[q][...], w_refs[q][...])

    def clamp(q):
        if q == 0:
            return pl.BlockSpec((tks[q], tm), lambda i, k: (jnp.clip(k - starts[q], 0, counts[q] - 1), i))
        return pl.BlockSpec((tm, tks[q]), lambda i, k: (i, jnp.clip(k - starts[q], 0, counts[q] - 1)))

    def clamp_w(q):
        if q == 0:
            return pl.BlockSpec((tks[q], d), lambda i, k: (jnp.clip(k - starts[q], 0, counts[q] - 1), 0))
        return pl.BlockSpec((d, tks[q]), lambda i, k: (0, jnp.clip(k - starts[q], 0, counts[q] - 1)))

    res = pl.pallas_call(
        mm_body, name="dh_matmul_scatter" if hosted else "dh_matmul", grid=(ni, nk),
        in_specs=([clamp(q) for q in range(ns)] + [clamp_w(q) for q in range(ns)]
                  + (ex.in_specs if hosted else [])),
        out_specs=[pl.BlockSpec((tm, d), lambda i, k: (i, 0))] + (ex.out_specs if hosted else []),
        out_shape=[jax.ShapeDtypeStruct((s, d), F32)] + (ex.out_shape if hosted else []),
        scratch_shapes=ex.scratch if hosted else [],
        compiler_params=_cparams(("arbitrary" if hosted else "parallel", "arbitrary")),
    )(*dslabs, *wslabs, *(ex.arrays if hosted else []))
    dh, ex_results = res[0], res[1:]

    te = min(512, s)

    def norm_body(dh_ref, x_ref, g_ref, dres_ref, dx_ref, dg_ref):
        @pl.when(pl.program_id(0) == 0)
        def _():
            dg_ref[...] = jnp.zeros_like(dg_ref)

        xv = x_ref[...]
        r = lax.rsqrt(jnp.mean(xv * xv, axis=-1, keepdims=True) + EPS)
        xn = xv * r
        dhv = dh_ref[...]
        dg_ref[...] += jnp.sum(dhv * xn, axis=0, keepdims=True)
        dxn = dhv * g_ref[...]
        dx_ref[...] = dres_ref[...] + r * (dxn - xn * jnp.mean(dxn * xn, axis=-1, keepdims=True))

    rows = pl.BlockSpec((te, d), lambda i: (i, 0))
    dx, dg = pl.pallas_call(
        norm_body, name="norm_bwd", grid=(s // te,),
        in_specs=[rows, rows, _full((1, d)), rows],
        out_specs=[rows, _full((1, d))],
        out_shape=[jax.ShapeDtypeStruct((s, d), F32), jax.ShapeDtypeStruct((1, d), F32)],
        compiler_params=_cparams(("arbitrary",)),
    )(dh, x, g, dres)
    return (dx, dg, ex_results) if hosted else (dx, dg)


def bias_table(rel_bias_t, onehot_t, maskadd):
    n = onehot_t.shape[1]
    tn = 8192

    def body(r_ref, o_ref, m_ref, out_ref):
        out_ref[...] = _dot_hi(r_ref[...], o_ref[...]) + m_ref[...]

    return pl.pallas_call(
        body, name="bias_table", grid=(n // tn,),
        in_specs=[_full((ATT_HEADS, REL_BUCKETS)), pl.BlockSpec((REL_BUCKETS, tn), lambda j: (0, j)),
                  pl.BlockSpec((1, tn), lambda j: (0, j))],
        out_specs=pl.BlockSpec((ATT_HEADS, tn), lambda j: (0, j)),
        out_shape=jax.ShapeDtypeStruct((ATT_HEADS, n), F32),
        compiler_params=_cparams(("parallel",)),
    )(rel_bias_t, onehot_t, maskadd)


def bias_table_bwd(dbias, onehot_t):
    n = onehot_t.shape[1]
    tn = 8192

    def body(d_ref, o_ref, out_ref):
        @pl.when(pl.program_id(0) == 0)
        def _():
            out_ref[...] = jnp.zeros_like(out_ref)

        out_ref[...] += _dot_hi_nt(d_ref[...], o_ref[...])

    return pl.pallas_call(
        body, name="bias_table_bwd", grid=(n // tn,),
        in_specs=[pl.BlockSpec((ATT_HEADS, tn), lambda j: (0, j)), pl.BlockSpec((REL_BUCKETS, tn), lambda j: (0, j))],
        out_specs=_full((ATT_HEADS, REL_BUCKETS)),
        out_shape=jax.ShapeDtypeStruct((ATT_HEADS, REL_BUCKETS), F32),
        compiler_params=_cparams(("arbitrary",)),
    )(dbias, onehot_t)


def _fold(full, tri):
    return jnp.where(tri, full[BLK:2 * BLK], full[0:BLK])


def _unfold(folded, tri):
    return jnp.concatenate([jnp.where(tri, 0.0, folded), jnp.where(tri, folded, 0.0)], axis=0)


GROUP_HEADS = ATT_HEADS // 2
GROUP_LANES = GROUP_HEADS * BLK


def _att_group(qg, kcat, vt_cat, bias_g, sink_g, tri, no_prev):
    l = _fold(_dot(kcat, qg), tri) * (HEAD_DIM ** -0.5) + bias_g
    l = jnp.where(no_prev, NEG, l)
    m = jnp.maximum(jnp.max(l, axis=0, keepdims=True), sink_g)
    p = jnp.exp(l - m)
    es = jnp.exp(sink_g - m)
    inv = 1.0 / (jnp.sum(p, axis=0, keepdims=True) + es)
    p = p * inv
    pcat = _unfold(p, tri)
    return p, pcat, es * inv, _dot(vt_cat, pcat)


ATT_SUB = 8


def _heads_to_lanes(ref, row0, ln):
    return jnp.concatenate([ref[row0 + j * HEAD_DIM:row0 + (j + 1) * HEAD_DIM, ln] for j in range(GROUP_HEADS)], axis=1)


def _lanes_to_heads(ref, row0, ln, val):
    for j in range(GROUP_HEADS):
        ref[row0 + j * HEAD_DIM:row0 + (j + 1) * HEAD_DIM, ln] = val[:, j * BLK:(j + 1) * BLK].astype(ref.dtype)


def _kv_cat(kvp, kvc, g):
    lo = g * HEAD_DIM
    kt_cat = jnp.concatenate([kvp[lo:lo + HEAD_DIM], kvc[lo:lo + HEAD_DIM]], axis=1)
    vt_cat = jnp.concatenate([kvp[128 + lo:128 + lo + HEAD_DIM], kvc[128 + lo:128 + lo + HEAD_DIM]], axis=1)
    return kt_cat, vt_cat


def _tri_masks(n):
    row = lax.broadcasted_iota(jnp.int32, (BLK, GROUP_LANES), 0)
    query = lax.broadcasted_iota(jnp.int32, (BLK, GROUP_LANES), 1) & (BLK - 1)
    tri = row <= query
    return tri, (n == 0) & jnp.logical_not(tri)


def _split_hosted(refs, n_in, n_out, n_scratch, ex):
    na = ex.na if ex is not None else 0
    o = 0
    parts = []
    for cnt in (n_in, na, n_out, na, n_scratch, 3 if ex is not None else 0):
        parts.append(refs[o:o + cnt])
        o += cnt
    own_in, ex_in, own_out, ex_out, own_scr, ex_sems = parts
    return (own_in, own_out, own_scr), (ex_in, ex_out, ex_sems)


def _call_hosting(body, name, nsteps, in_specs, out_specs, out_shape, scratch, args, ex):
    n_in, n_out, n_scr = len(in_specs), len(out_specs), len(scratch)
    hosted = ex is not None

    def full_body(*refs):
        (own_in, own_out, own_scr), hosted_refs = _split_hosted(refs, n_in, n_out, n_scr, ex)
        if hosted:
            @pl.when(pl.program_id(0) == 0)
            def _():
                ex.start(*hosted_refs)

            @pl.when(pl.program_id(0) == max(nsteps - max(nsteps // 8, 4), 0))
            def _():
                ex.relay(*hosted_refs)

            @pl.when(pl.program_id(0) == nsteps - 1)
            def _():
                ex.wait(*hosted_refs)

        body(*own_in, *own_out, *own_scr)

    res = pl.pallas_call(
        full_body, name=name + "_hosting" if hosted else name, grid=(nsteps,),
        in_specs=list(in_specs) + (ex.in_specs if hosted else []),
        out_specs=list(out_specs) + (ex.out_specs if hosted else []),
        out_shape=list(out_shape) + (ex.out_shape if hosted else []),
        scratch_shapes=list(scratch) + (ex.scratch if hosted else []),
        compiler_params=_cparams(("arbitrary",)),
    )(*args, *(ex.arrays if hosted else []))
    return res[:n_out], res[n_out:]


def attn_fwd(pa, bias, sinks, ex=None):
    s = pa.shape[1]
    nsteps = s // (ATT_SUB * BLK)

    def body(pa_ref, kvp_ref, bias_ref, sink_ref, y_ref):
        for sub in range(ATT_SUB):
            n = pl.program_id(0) * ATT_SUB + sub
            ln = slice(sub * BLK, (sub + 1) * BLK)
            kvc = pa_ref[2048:2304, ln]
            kvp = kvp_ref[...] if sub == 0 else pa_ref[2048:2304, (sub - 1) * BLK:sub * BLK]
            tri, no_prev = _tri_masks(n)
            for g in range(2):
                kt_cat, vt_cat = _kv_cat(kvp, kvc, g)
                row0 = g * GROUP_HEADS * HEAD_DIM
                _, _, _, o = _att_group(_heads_to_lanes(pa_ref, row0, ln), kt_cat.astype(F32).T, vt_cat, bias_ref[g],
                                        sink_ref[g:g + 1, :], tri, no_prev)
                z = _heads_to_lanes(pa_ref, 1024 + row0, ln).astype(F32)
                _lanes_to_heads(y_ref, row0, ln, o * z * _sig(z))

    (y,), hosted = _call_hosting(
        body, "attn_fwd", nsteps,
        in_specs=[pl.BlockSpec((ATT_COLS, ATT_SUB * BLK), lambda n: (0, n)),
                  pl.BlockSpec((256, BLK), lambda n: (8, jnp.maximum(ATT_SUB * n - 1, 0))),
                  _full((2, BLK, GROUP_LANES)), _full((2, GROUP_LANES))],
        out_specs=[pl.BlockSpec((1024, ATT_SUB * BLK), lambda n: (0, n))],
        out_shape=[jax.ShapeDtypeStruct((1024, s), ACT_DTYPE)], scratch=[],
        args=(pa, pa, bias, sinks), ex=ex)
    return (y, hosted) if ex is not None else y


def attn_bwd(pa, dy, bias, sinks, dbias_in):
    s = pa.shape[1]
    nsteps = s // (ATT_SUB * BLK)

    def body(pa_ref, kvp_ref, dy_ref, bias_ref, sink_ref, dbin_ref, dpa_ref, dbias_ref, dsink_ref, carry, dsink_acc):
        i = pl.program_id(0)

        @pl.when(i == 0)
        def _():
            dbias_ref[...] = dbin_ref[...]
            dsink_acc[...] = jnp.zeros_like(dsink_acc)
            carry[...] = jnp.zeros_like(carry)

        scale = HEAD_DIM ** -0.5
        for sub in reversed(range(ATT_SUB)):
            n = (nsteps - 1 - i) * ATT_SUB + sub
            ln = slice(sub * BLK, (sub + 1) * BLK)
            kvc = pa_ref[2048:2304, ln]
            kvp = kvp_ref[...] if sub == 0 else pa_ref[2048:2304, (sub - 1) * BLK:sub * BLK]
            tri, no_prev = _tri_masks(n)
            for g in range(2):
                kt_cat, vt_cat = _kv_cat(kvp, kvc, g)
                row0 = g * GROUP_HEADS * HEAD_DIM
                qg = _heads_to_lanes(pa_ref, row0, ln)
                p, pcat, psink, o = _att_group(qg, kt_cat.astype(F32).T, vt_cat, bias_ref[g], sink_ref[g:g + 1, :],
                                               tri, no_prev)
                z = _heads_to_lanes(pa_ref, 1024 + row0, ln).astype(F32)
                dyg = _heads_to_lanes(dy_ref, row0, ln).astype(F32)
                sz = _sig(z)
                d_o = dyg * z * sz
                _lanes_to_heads(dpa_ref, 1024 + row0, ln, dyg * _dsilu(z, sz) * o)
                delta = jnp.sum(d_o * o, axis=0, keepdims=True)
                dl = p * (_fold(_dot(vt_cat.astype(F32).T, d_o), tri) - delta)
                dsink_acc[g:g + 1, :] += psink * delta
                dbias_ref[g] += dl
                dlcat = _unfold(dl, tri)
                _lanes_to_heads(dpa_ref, row0, ln, _dot(kt_cat, dlcat) * scale)
                for q, dkv in enumerate((_dot_nt(qg, dlcat) * scale, _dot_nt(d_o, pcat))):
                    r0 = q * 128 + g * HEAD_DIM
                    dpa_ref[2048 + r0:2048 + r0 + HEAD_DIM, ln] = (
                        dkv[:, BLK:2 * BLK] + carry[r0:r0 + HEAD_DIM, :]).astype(dpa_ref.dtype)
                    carry[r0:r0 + HEAD_DIM, :] = dkv[:, 0:BLK]

        @pl.when(i == nsteps - 1)
        def _():
            lane = lax.broadcasted_iota(jnp.int32, (1, 128), 1)
            dsink = jnp.zeros((1, 128), F32)
            for h in range(ATT_HEADS):
                g, j = divmod(h, GROUP_HEADS)
                tot = jnp.sum(dsink_acc[g:g + 1, j * BLK:(j + 1) * BLK], axis=1, keepdims=True)
                dsink = dsink + jnp.where(lane == h, -tot, 0.0)
            dsink_ref[...] = dsink

    return pl.pallas_call(
        body, name="attn_bwd", grid=(nsteps,),
        in_specs=[pl.BlockSpec((ATT_COLS, ATT_SUB * BLK), lambda i: (0, nsteps - 1 - i)),
                  pl.BlockSpec((256, BLK), lambda i: (8, jnp.maximum(ATT_SUB * (nsteps - 1 - i) - 1, 0))),
                  pl.BlockSpec((1024, ATT_SUB * BLK), lambda i: (0, nsteps - 1 - i)),
                  _full((2, BLK, GROUP_LANES)), _full((2, GROUP_LANES)), _full((2, BLK, GROUP_LANES))],
        out_specs=[pl.BlockSpec((ATT_COLS, ATT_SUB * BLK), lambda i: (0, nsteps - 1 - i)),
                   _full((2, BLK, GROUP_LANES)), _full((1, 128))],
        out_shape=[jax.ShapeDtypeStruct((ATT_COLS, s), ACT_DTYPE),
                   jax.ShapeDtypeStruct((2, BLK, GROUP_LANES), F32),
                   jax.ShapeDtypeStruct((1, 128), F32)],
        scratch_shapes=[pltpu.VMEM((256, BLK), F32), pltpu.VMEM((2, GROUP_LANES), F32)],
        compiler_params=_cparams(("arbitrary",)),
    )(pa, pa, dy, bias, sinks, dbias_in)


def _layernorm(v, g, b):
    mu = jnp.mean(v, axis=-1, keepdims=True)
    vc = v - mu
    rstd = lax.rsqrt(jnp.mean(vc * vc, axis=-1, keepdims=True) + EPS)
    xhat = vc * rstd
    return xhat, rstd, xhat * g + b


def sgu_fwd(ps, ln_g, ln_b, w_tril, b_t):
    s = ps.shape[0]
    rows = min(4 * BLK, s)

    def body(ps_ref, g_ref, b_ref, w_ref, bt_ref, y_ref):
        u = ps_ref[:, 0:1024].astype(F32)
        v = ps_ref[:, 1024:2048].astype(F32)
        z = ps_ref[:, 2048:3072].astype(F32)
        _, _, vn = _layernorm(v, g_ref[...], b_ref[...])
        gate = u * z * _sig(z)
        for c in range(rows // BLK):
            ch = slice(c * BLK, (c + 1) * BLK)
            for g in range(SG_GROUPS):
                sl = slice(g * 128, (g + 1) * 128)
                mixed = _dot(w_ref[g], vn[ch, sl]) + bt_ref[:, g:g + 1]
                y_ref[ch, sl] = (gate[ch, sl] * mixed).astype(y_ref.dtype)

    return pl.pallas_call(
        body, name="sgu_fwd", grid=(s // rows,),
        in_specs=[pl.BlockSpec((rows, SG_COLS), lambda c: (c, 0)), _full((1, 1024)), _full((1, 1024)),
                  _full((SG_GROUPS, BLK, BLK)), _full((BLK, 128))],
        out_specs=pl.BlockSpec((rows, 1024), lambda c: (c, 0)),
        out_shape=jax.ShapeDtypeStruct((s, 1024), ACT_DTYPE),
        compiler_params=_cparams(("parallel",)),
    )(ps, ln_g, ln_b, w_tril, b_t)


def sgu_bwd(ps, dy, ln_g, ln_b, w_tril, w_tril_t, b_t):
    s = ps.shape[0]
    rows = min(4 * BLK, s)

    def body(ps_ref, dy_ref, g_ref, b_ref, w_ref, wt_ref, bt_ref, dps_ref, dw_ref, dbt_ref, dg_ref, db_ref, dvn_scr):
        @pl.when(pl.program_id(0) == 0)
        def _():
            dw_ref[...] = jnp.zeros_like(dw_ref)
            dbt_ref[...] = jnp.zeros_like(dbt_ref)
            dg_ref[...] = jnp.zeros_like(dg_ref)
            db_ref[...] = jnp.zeros_like(db_ref)

        u = ps_ref[:, 0:1024].astype(F32)
        v = ps_ref[:, 1024:2048].astype(F32)
        z = ps_ref[:, 2048:3072].astype(F32)
        dy = dy_ref[...].astype(F32)
        xhat, rstd, vn = _layernorm(v, g_ref[...], b_ref[...])
        sz = _sig(z)
        silu = z * sz
        row = lax.broadcasted_iota(jnp.int32, (BLK, BLK), 0)
        colm = lax.broadcasted_iota(jnp.int32, (BLK, BLK), 1)
        tril = row >= colm
        dbt = jnp.zeros((BLK, 128), F32)
        dsilu_z = _dsilu(z, sz)
        for c in range(rows // BLK):
            ch = slice(c * BLK, (c + 1) * BLK)
            for g in range(SG_GROUPS):
                sl = slice(g * 128, (g + 1) * 128)
                vng = vn[ch, sl]
                mixed = _dot(w_ref[g], vng) + bt_ref[:, g:g + 1]
                dyg, ug = dy[ch, sl], u[ch, sl]
                dps_ref[ch, sl] = (dyg * mixed * silu[ch, sl]).astype(dps_ref.dtype)
                dps_ref[ch, 2048 + g * 128:2048 + (g + 1) * 128] = (
                    dyg * ug * mixed * dsilu_z[ch, sl]).astype(dps_ref.dtype)
                dm = dyg * ug * silu[ch, sl]
                dw_ref[g] += jnp.where(tril, _dot_nt(dm, vng), 0.0)
                dbt = dbt + jnp.where(colm == g, jnp.sum(dm, axis=1, keepdims=True), 0.0)
                dvn_scr[ch, sl] = _dot(wt_ref[g], dm)
        dbt_ref[...] += dbt
        dvn = dvn_scr[...]
        dg_ref[...] += jnp.sum(dvn * xhat, axis=0, keepdims=True)
        db_ref[...] += jnp.sum(dvn, axis=0, keepdims=True)
        dxh = dvn * g_ref[...]
        dv = rstd * (dxh - jnp.mean(dxh, axis=-1, keepdims=True)
                     - xhat * jnp.mean(dxh * xhat, axis=-1, keepdims=True))
        dps_ref[:, 1024:2048] = dv.astype(dps_ref.dtype)

    return pl.pallas_call(
        body, name="sgu_bwd", grid=(s // rows,),
        in_specs=[pl.BlockSpec((rows, SG_COLS), lambda c: (c, 0)), pl.BlockSpec((rows, 1024), lambda c: (c, 0)),
                  _full((1, 1024)), _full((1, 1024)), _full((SG_GROUPS, BLK, BLK)), _full((SG_GROUPS, BLK, BLK)),
                  _full((BLK, 128))],
        out_specs=[pl.BlockSpec((rows, SG_COLS), lambda c: (c, 0)), _full((SG_GROUPS, BLK, BLK)), _full((BLK, 128)),
                   _full((1, 1024)), _full((1, 1024))],
        out_shape=[jax.ShapeDtypeStruct((s, SG_COLS), ACT_DTYPE), jax.ShapeDtypeStruct((SG_GROUPS, BLK, BLK), F32),
                   jax.ShapeDtypeStruct((BLK, 128), F32), jax.ShapeDtypeStruct((1, 1024), F32),
                   jax.ShapeDtypeStruct((1, 1024), F32)],
        scratch_shapes=[pltpu.VMEM((rows, 1024), F32)],
        compiler_params=_cparams(("arbitrary",)),
    )(ps, dy, ln_g, ln_b, w_tril, w_tril_t, b_t)


def _shift_down(cur, prev16, k):
    if k == 0:
        return cur
    r = pltpu.roll(cur, k, 0)
    rp = pltpu.roll(prev16, k, 0)
    row = lax.broadcasted_iota(jnp.int32, (8, cur.shape[1]), 0)
    return jnp.concatenate([jnp.where(row < k, rp[0:8], r[0:8]), r[8:]], axis=0)


def _shift_up(cur, next16, k):
    if k == 0:
        return cur
    n = cur.shape[0]
    r = pltpu.roll(cur, n - k, 0)
    rn = pltpu.roll(next16, 16 - k, 0)
    row = lax.broadcasted_iota(jnp.int32, (8, cur.shape[1]), 0)
    return jnp.concatenate([r[:n - 8], jnp.where(row >= 8 - k, rn[8:16], r[n - 8:])], axis=0)


def _bcast8(v):
    return jnp.broadcast_to(v, (16, v.shape[1]))


def _causal_conv(xbc, prev16, cw, cbias):
    pre = cbias + cw[3:4] * xbc
    for k in (1, 2, 3):
        pre = pre + cw[3 - k:4 - k] * _shift_down(xbc, prev16, k)
    return pre


class _Ssd:
    def __init__(self, pre, dtr, dtb, alog, dsk, tri, e):
        self.pre = pre
        self.sg = _sig(pre)
        act = pre * self.sg
        self.xs = act[:, 0:SSM_WIDTH]
        self.bm = act[:, SSM_WIDTH:SSM_WIDTH + 512]
        self.cm = act[:, SSM_WIDTH + 512:CONV_DIM]
        self.dtp = dtr + dtb
        self.dt = jnp.maximum(self.dtp, 0.0) + jnp.log(1.0 + jnp.exp(-jnp.abs(self.dtp)))
        self.a = -jnp.exp(alog)
        self.acs = _dot_hi(tri, self.dt * self.a)
        self.acs_t = self.acs.T
        tot = self.acs[BLK - 1:BLK]
        self.ecs = jnp.exp(self.acs)
        self.dte = jnp.exp(tot - self.acs)
        self.cd = jnp.exp(tot)
        self.dt_x = _dot_onehot(self.dt, e)
        self.ecs_x = _dot_onehot(self.ecs, e)
        self.dte_x = _dot_onehot(self.dte, e)
        self.cd_x = _dot_onehot(_bcast8(self.cd), e)[0:1]
        self.d_x = _dot_onehot(_bcast8(dsk), e)[0:1]
        self.xdt = self.xs * self.dt_x
        row = lax.broadcasted_iota(jnp.int32, (BLK, BLK), 0)
        col = lax.broadcasted_iota(jnp.int32, (BLK, BLK), 1)
        self.tril = row >= col

    def group(self, g):
        sl = slice(g * 128, (g + 1) * 128)
        bg, cg = self.bm[:, sl], self.cm[:, sl]
        return bg, cg, _dot_nt(cg, bg)

    def decay(self, h):
        seg = self.acs[:, h:h + 1] - self.acs_t[h:h + 1, :]
        return jnp.exp(jnp.where(self.tril, seg, NEG))

    def y_pre_gate(self, ht_of, yd_scr, yoff_scr):
        for g in range(SSM_GROUPS):
            bg, cg, cb = self.group(g)
            for j in range(8):
                h = g * 8 + j
                sl = slice(h * 64, (h + 1) * 64)
                yd_scr[:, sl] = _dot(cb * self.decay(h), self.xdt[:, sl])
            gs = slice(g * SSM_GW, (g + 1) * SSM_GW)
            yoff_scr[:, gs] = _dot(cg, ht_of(g)) * self.ecs_x[:, gs]
        return yd_scr[...] + yoff_scr[...] + self.d_x * self.xs


def _ssd_consts():
    hh = lax.broadcasted_iota(jnp.int32, (128, SSM_WIDTH), 0)
    ch = lax.broadcasted_iota(jnp.int32, (128, SSM_WIDTH), 1)
    e = (ch // 64 == hh).astype(jnp.bfloat16)
    row = lax.broadcasted_iota(jnp.int32, (BLK, BLK), 0)
    col = lax.broadcasted_iota(jnp.int32, (BLK, BLK), 1)
    tri = (row >= col).astype(F32)
    return tri, e


def _pad_lanes(v, n=128):
    return jnp.pad(v, ((0, 0), (0, n - v.shape[1])))


def ssd_fwd(pm, cw, cbias, dtb, alog, dsk, ng, ex=None):
    s = pm.shape[0]
    nc = s // BLK
    tri, e = _ssd_consts()

    def body(pm_ref, prev_ref, cw_ref, cb_ref, dtb_ref, al_ref, d_ref, ng_ref, tri_ref, e_ref,
             y_ref, st_ref, pre_ref, ht_ref, yd_scr, yoff_scr):
        c = pl.program_id(0)

        @pl.when(c == 0)
        def _():
            ht_ref[...] = jnp.zeros_like(ht_ref)

        xbc = pm_ref[:, 0:CONV_DIM].astype(F32)
        prev16 = jnp.where(c == 0, 0.0, prev_ref[...].astype(F32))
        pre = _causal_conv(xbc, prev16, cw_ref[...], cb_ref[...])
        pre_ref[...] = pre.astype(pre_ref.dtype)
        f = _Ssd(pre, pm_ref[:, DT_OFF:DT_OFF + 128].astype(F32), dtb_ref[...], al_ref[...], d_ref[...],
                 tri_ref[...], e_ref[...])
        st_ref[0] = ht_ref[...]
        y = f.y_pre_gate(lambda g: ht_ref[g], yd_scr, yoff_scr)
        for g in range(SSM_GROUPS):
            bg, _, _ = f.group(g)
            gs = slice(g * SSM_GW, (g + 1) * SSM_GW)
            ht_ref[g] = ht_ref[g] * f.cd_x[:, gs] + _dot_tn(bg, f.xdt[:, gs] * f.dte_x[:, gs])
        z = pm_ref[:, CONV_DIM:CONV_DIM + SSM_WIDTH].astype(F32)
        ypre = y * z * _sig(z)
        for g in range(SSM_GROUPS):
            gs = slice(g * SSM_GW, (g + 1) * SSM_GW)
            yg = ypre[:, gs]
            rr = lax.rsqrt(jnp.mean(yg * yg, axis=-1, keepdims=True) + EPS)
            y_ref[:, gs] = (yg * rr * ng_ref[:, gs]).astype(y_ref.dtype)

    own, hosted = _call_hosting(
        body, "ssd_fwd", nc,
        in_specs=[pl.BlockSpec((BLK, SSM_COLS), lambda c: (c, 0)),
                  pl.BlockSpec((16, CONV_DIM), lambda c: (jnp.maximum(8 * c - 1, 0), 0)),
                  _full((4, CONV_DIM)), _full((1, CONV_DIM)), _full((1, 128)), _full((1, 128)), _full((1, 128)),
                  _full((1, SSM_WIDTH)), _full((BLK, BLK)), _full((128, SSM_WIDTH))],
        out_specs=[pl.BlockSpec((BLK, SSM_WIDTH), lambda c: (c, 0)),
                   pl.BlockSpec((1, SSM_GROUPS, 128, SSM_GW), lambda c: (c, 0, 0, 0)),
                   pl.BlockSpec((BLK, CONV_DIM), lambda c: (c, 0))],
        out_shape=[jax.ShapeDtypeStruct((s, SSM_WIDTH), ACT_DTYPE),
                   jax.ShapeDtypeStruct((nc, SSM_GROUPS, 128, SSM_GW), F32),
                   jax.ShapeDtypeStruct((s, CONV_DIM), ACT_DTYPE)],
        scratch=[pltpu.VMEM((SSM_GROUPS, 128, SSM_GW), F32), pltpu.VMEM((BLK, SSM_WIDTH), F32),
                 pltpu.VMEM((BLK, SSM_WIDTH), F32)],
        args=(pm, pm, cw, cbias, dtb, alog, dsk, ng, tri, e), ex=ex)
    return (*own, hosted) if ex is not None else tuple(own)


def ssd_bwd(pm, pre, dy, states, cw, dtb, alog, dsk, ng, ex=None):
    s = pm.shape[0]
    nc = s // BLK
    tri, e = _ssd_consts()
    tri_t, e_t = tri.T, e.T

    def body(pm_ref, pre_ref, dy_ref, st_ref, cw_ref, dtb_ref, al_ref, d_ref, ng_ref,
             tri_ref, trit_ref, e_ref, et_ref,
             dpm_ref, dcw_ref, dcb_ref, dvec_ref, dng_ref,
             dht_ref, dcar_ref, yd_scr, yoff_scr, dx_scr, r2_scr, hs_scr, da_scr, dat_scr, dd_scr, dbc_scr):
        i = pl.program_id(0)
        n = nc - 1 - i

        @pl.when(i == 0)
        def _():
            dht_ref[...] = jnp.zeros_like(dht_ref)
            dcar_ref[...] = jnp.zeros_like(dcar_ref)
            dcw_ref[...] = jnp.zeros_like(dcw_ref)
            dcb_ref[...] = jnp.zeros_like(dcb_ref)
            dvec_ref[...] = jnp.zeros_like(dvec_ref)
            dng_ref[...] = jnp.zeros_like(dng_ref)
            dd_scr[...] = jnp.zeros_like(dd_scr)
            da_scr[...] = jnp.zeros_like(da_scr)
            dat_scr[...] = jnp.zeros_like(dat_scr)

        cw = cw_ref[...]
        f = _Ssd(pre_ref[...].astype(F32), pm_ref[:, DT_OFF:DT_OFF + 128].astype(F32), dtb_ref[...], al_ref[...],
                 d_ref[...], tri_ref[...], e_ref[...])
        et = et_ref[...]
        y = f.y_pre_gate(lambda g: st_ref[0, g], yd_scr, yoff_scr)

        z = pm_ref[:, CONV_DIM:CONV_DIM + SSM_WIDTH].astype(F32)
        dyv = dy_ref[...].astype(F32)
        sz = _sig(z)
        silu = z * sz
        ypre = y * silu
        for g in range(SSM_GROUPS):
            gs = slice(g * SSM_GW, (g + 1) * SSM_GW)
            yg = ypre[:, gs]
            rr = lax.rsqrt(jnp.mean(yg * yg, axis=-1, keepdims=True) + EPS)
            nrm = yg * rr
            dng_ref[:, gs] += jnp.sum(dyv[:, gs] * nrm, axis=0, keepdims=True)
            dn = dyv[:, gs] * ng_ref[:, gs]
            dx_scr[:, gs] = rr * (dn - nrm * jnp.mean(dn * nrm, axis=-1, keepdims=True))
        dypre = dx_scr[...]
        d_y = dypre * silu
        dpm_ref[:, CONV_DIM:CONV_DIM + SSM_WIDTH] = (dypre * y * _dsilu(z, sz)).astype(dpm_ref.dtype)

        for g in range(SSM_GROUPS):
            bg, cg, cb = f.group(g)
            gs = slice(g * SSM_GW, (g + 1) * SSM_GW)
            htg = st_ref[0, g]
            dhn = dht_ref[g]
            dcb = jnp.zeros((BLK, BLK), F32)
            for j in range(8):
                h = g * 8 + j
                sl = slice(h * 64, (h + 1) * 64)
                dec = f.decay(h)
                dyh = d_y[:, sl]
                dmd = _dot_nt(dyh, f.xdt[:, sl]) * dec
                dcb = dcb + dmd
                gm = dmd * cb
                da_scr[:, h:h + 1] = jnp.sum(gm, axis=1, keepdims=True)
                dat_scr[h:h + 1, :] = jnp.sum(gm, axis=0, keepdims=True)
                dx_scr[:, sl] = _dot_tn(cb * dec, dyh)
            dz = f.ecs_x[:, gs] * d_y[:, gs]
            dbc_scr[:, 512 + g * 128:512 + (g + 1) * 128] = _dot(dcb, bg) + _dot_nt(dz, htg)
            dbc_scr[:, g * 128:(g + 1) * 128] = _dot_tn(dcb, cg) + _dot_nt(f.xdt[:, gs] * f.dte_x[:, gs], dhn)
            dws = _dot(bg, dhn)
            dx_scr[:, gs] += f.dte_x[:, gs] * dws
            r2_scr[:, gs] = dws * f.xdt[:, gs]
            hs_scr[:, gs] = _bcast8(jnp.sum(dhn * htg, axis=0, keepdims=True))
            dht_ref[g] = f.cd_x[:, gs] * dhn + _dot_tn(cg, dz)
        d_x = dx_scr[...]
        r1 = _dot(d_y * yoff_scr[...], et)
        r2 = _dot(r2_scr[...], et) * f.dte
        dcd = _dot_onehot(hs_scr[...], et)[0:1]
        d_tot = jnp.sum(r2, axis=0, keepdims=True) + dcd * f.cd
        row = lax.broadcasted_iota(jnp.int32, (BLK, 128), 0)
        d_a = da_scr[...] - dat_scr[...].T + r1 - r2 + jnp.where(row == BLK - 1, d_tot, 0.0)
        dadt = _dot_hi(trit_ref[...], d_a)
        ddt = dadt * f.a + _dot(d_x * f.xs, et)
        lane = lax.broadcasted_iota(jnp.int32, (BLK, 128), 1)
        dr = jnp.where(lane < SSM_HEADS, ddt * _sig(f.dtp), 0.0)
        dvec_ref[0:1, :] += jnp.sum(dr, axis=0, keepdims=True)
        dvec_ref[1:2, :] += jnp.sum(dadt * f.dt, axis=0, keepdims=True) * f.a
        dd_scr[...] += _bcast8(jnp.sum(d_y * f.xs, axis=0, keepdims=True))
        dpm_ref[:, DT_OFF:DT_OFF + 128] = dr.astype(dpm_ref.dtype)
        dpm_ref[:, DT_OFF + 128:SSM_COLS] = jnp.zeros((BLK, 128), dpm_ref.dtype)

        dxs = d_x * f.dt_x + f.d_x * d_y
        dact = jnp.concatenate([dxs, dbc_scr[...]], axis=1)
        dpre = dact * _dsilu(f.pre, f.sg)
        dcb_ref[...] += jnp.sum(dpre, axis=0, keepdims=True)
        xbc = pm_ref[:, 0:CONV_DIM].astype(F32)
        dxraw = jnp.zeros((BLK, CONV_DIM), F32)
        nxt = dcar_ref[...]
        for k in range(4):
            ahead = _shift_up(dpre, nxt, k)
            dcw_ref[3 - k:4 - k, :] += jnp.sum(ahead * xbc, axis=0, keepdims=True)
            dxraw = dxraw + cw[3 - k:4 - k] * ahead
        dcar_ref[...] = dpre[0:16]
        dpm_ref[:, 0:CONV_DIM] = dxraw.astype(dpm_ref.dtype)

        @pl.when(i == nc - 1)
        def _():
            dvec_ref[2:3, :] = _dot_onehot(dd_scr[...], et)[0:1]

    own, hosted = _call_hosting(
        body, "ssd_bwd", nc,
        in_specs=[pl.BlockSpec((BLK, SSM_COLS), lambda i: (nc - 1 - i, 0)),
                  pl.BlockSpec((BLK, CONV_DIM), lambda i: (nc - 1 - i, 0)),
                  pl.BlockSpec((BLK, SSM_WIDTH), lambda i: (nc - 1 - i, 0)),
                  pl.BlockSpec((1, SSM_GROUPS, 128, SSM_GW), lambda i: (nc - 1 - i, 0, 0, 0)),
                  _full((4, CONV_DIM)), _full((1, 128)), _full((1, 128)), _full((1, 128)),
                  _full((1, SSM_WIDTH)), _full((BLK, BLK)), _full((BLK, BLK)), _full((128, SSM_WIDTH)),
                  _full((SSM_WIDTH, 128))],
        out_specs=[pl.BlockSpec((BLK, SSM_COLS), lambda i: (nc - 1 - i, 0)),
                   _full((8, CONV_DIM)), _full((1, CONV_DIM)), _full((8, 128)), _full((1, SSM_WIDTH))],
        out_shape=[jax.ShapeDtypeStruct((s, SSM_COLS), ACT_DTYPE), jax.ShapeDtypeStruct((8, CONV_DIM), F32),
                   jax.ShapeDtypeStruct((1, CONV_DIM), F32), jax.ShapeDtypeStruct((8, 128), F32),
                   jax.ShapeDtypeStruct((1, SSM_WIDTH), F32)],
        scratch=[pltpu.VMEM((SSM_GROUPS, 128, SSM_GW), F32), pltpu.VMEM((16, CONV_DIM), F32),
                 pltpu.VMEM((BLK, SSM_WIDTH), F32), pltpu.VMEM((BLK, SSM_WIDTH), F32),
                 pltpu.VMEM((BLK, SSM_WIDTH), F32), pltpu.VMEM((BLK, SSM_WIDTH), F32),
                 pltpu.VMEM((16, SSM_WIDTH), F32), pltpu.VMEM((BLK, 128), F32), pltpu.VMEM((128, BLK), F32),
                 pltpu.VMEM((16, SSM_WIDTH), F32), pltpu.VMEM((BLK, 1024), F32)],
        args=(pm, pre, dy, states, cw, dtb, alog, dsk, ng, tri, tri_t, e, e_t), ex=ex)
    return (*own, hosted) if ex is not None else tuple(own)


def merge_fwd(x, ya, ys, ym, pg, wa, ws, wm, wo, g_post, target=None):
    s, d = x.shape
    tm = min(256, s)
    with_loss = target is not None

    def body(*refs):
        x_ref, ya_ref, ys_ref, ym_ref, pg_ref, wa_ref, ws_ref, wm_ref, wo_ref, g_ref = refs[:10]
        if with_loss:
            t_ref, xo_ref, l_ref, ba_ref, bs_ref, bm_ref, mg_ref, out_ref = refs[10:]
        else:
            xo_ref, ba_ref, bs_ref, bm_ref, mg_ref, out_ref = refs[10:]
        ba = _dot_tn(ya_ref[...], wa_ref[...])
        bs = _dot(ys_ref[...], ws_ref[...])
        bm = _dot(ym_ref[...], wm_ref[...])
        merged = (_sig(pg_ref[:, 0:d].astype(F32)) * ba + _sig(pg_ref[:, d:2 * d].astype(F32)) * bs
                  + _sig(pg_ref[:, 2 * d:3 * d].astype(F32)) * bm)
        out = _dot(merged, wo_ref[...])
        r = lax.rsqrt(jnp.mean(out * out, axis=-1, keepdims=True) + EPS)
        y = x_ref[...] + out * r * g_ref[...]
        if with_loss:
            @pl.when(pl.program_id(0) == 0)
            def _():
                l_ref[...] = jnp.zeros_like(l_ref)

            err = y - t_ref[...]
            xo_ref[...] = err * (1.0 / d)
            part = jnp.sum(jnp.sum(err * err, axis=-1, keepdims=True) * (1.0 / d), axis=0, keepdims=True)
            l_ref[...] += 0.5 * jnp.broadcast_to(part, l_ref.shape)
        else:
            xo_ref[...] = y
        ba_ref[...] = ba.astype(ba_ref.dtype)
        bs_ref[...] = bs.astype(bs_ref.dtype)
        bm_ref[...] = bm.astype(bm_ref.dtype)
        mg_ref[...] = merged.astype(mg_ref.dtype)
        out_ref[...] = out.astype(out_ref.dtype)

    rows = lambda w: pl.BlockSpec((tm, w), lambda i: (i, 0))
    act = jax.ShapeDtypeStruct((s, d), ACT_DTYPE)
    loss_spec = [_full((8, 128))] if with_loss else []
    loss_shape = [jax.ShapeDtypeStruct((8, 128), F32)] if with_loss else []
    return pl.pallas_call(
        body, name="merge_fwd_loss" if with_loss else "merge_fwd", grid=(s // tm,),
        in_specs=[rows(d), pl.BlockSpec((d, tm), lambda i: (0, i)), rows(d), rows(2 * d), rows(3 * d), _full((d, d)),
                  _full((d, d)), _full((2 * d, d)), _full((d, d)), _full((1, d))] + ([rows(d)] if with_loss else []),
        out_specs=[rows(d)] + loss_spec + [rows(d)] * 5,
        out_shape=[jax.ShapeDtypeStruct((s, d), F32)] + loss_shape + [act] * 5,
        compiler_params=_cparams(("arbitrary" if with_loss else "parallel",)),
    )(x, ya, ys, ym, pg, wa, ws, wm, wo, g_post, *([target] if with_loss else []))


def merge_bwd(dx, out_s, pg, ba, bs, bm, wa, ws, wm, wo, g_post):
    s, d = dx.shape
    tm = min(256, s)

    def body(dx_ref, out_ref, pg_ref, ba_ref, bs_ref, bm_ref, wa_ref, ws_ref, wm_ref, wo_ref, g_ref,
             dout_ref, dba_ref, dbs_ref, dbm_ref, dpg_ref, dya_ref, dys_ref, dym_ref, dg_ref):
        @pl.when(pl.program_id(0) == 0)
        def _():
            dg_ref[...] = jnp.zeros_like(dg_ref)

        o = out_ref[...].astype(F32)
        dxv = dx_ref[...]
        r = lax.rsqrt(jnp.mean(o * o, axis=-1, keepdims=True) + EPS)
        nrm = o * r
        dg_ref[...] += jnp.sum(dxv * nrm, axis=0, keepdims=True)
        dn = dxv * g_ref[...]
        dout = r * (dn - nrm * jnp.mean(dn * nrm, axis=-1, keepdims=True))
        dout_ref[...] = dout.astype(dout_ref.dtype)
        dmerged = _dot_nt(dout, wo_ref[...])
        for q, (b_ref, db_ref, w_ref, dy_ref) in enumerate(((ba_ref, dba_ref, wa_ref, dya_ref),
                                                            (bs_ref, dbs_ref, ws_ref, dys_ref),
                                                            (bm_ref, dbm_ref, wm_ref, dym_ref))):
            gt = _sig(pg_ref[:, q * d:(q + 1) * d].astype(F32))
            db = dmerged * gt
            db_ref[...] = db.astype(db_ref.dtype)
            dpg_ref[:, q * d:(q + 1) * d] = (dmerged * b_ref[...].astype(F32) * gt * (1.0 - gt)).astype(dpg_ref.dtype)
            if q == 0:
                dy_ref[...] = _dot_nt(w_ref[...], db).astype(dy_ref.dtype)
            else:
                dy_ref[...] = _dot_nt(db, w_ref[...]).astype(dy_ref.dtype)

    rows = lambda w: pl.BlockSpec((tm, w), lambda i: (i, 0))
    act = lambda w: jax.ShapeDtypeStruct((s, w), ACT_DTYPE)
    return pl.pallas_call(
        body, name="merge_bwd", grid=(s // tm,),
        in_specs=[rows(d), rows(d), rows(3 * d), rows(d), rows(d), rows(d), _full((d, d)), _full((d, d)),
                  _full((2 * d, d)), _full((d, d)), _full((1, d))],
        out_specs=[rows(d), rows(d), rows(d), rows(d), rows(3 * d), pl.BlockSpec((d, tm), lambda i: (0, i)), rows(d),
                   rows(2 * d), _full((1, d))],
        out_shape=[act(d), act(d), act(d), act(d), act(3 * d), jax.ShapeDtypeStruct((d, s), ACT_DTYPE), act(d),
                   act(2 * d), jax.ShapeDtypeStruct((1, d), F32)],
        compiler_params=_cparams(("arbitrary",)),
    )(dx, out_s, pg, ba, bs, bm, wa, ws, wm, wo, g_post)


def _mesh_pos():
    x, y, c = lax.axis_index("x"), lax.axis_index("y"), lax.axis_index("c")
    return x, y, c, 4 * x + 2 * y + c


def _peer(x, y, c, k):
    px = 1 - x if k & 4 else x
    py = 1 - y if k & 2 else y
    pc = 1 - c if k & 1 else c
    return (px, py, pc), 4 * px + 2 * py + pc


class Exchange:
    SAME_CORE = (2, 4, 6)

    def __init__(self, scattered, gathered):
        self.ns = len(scattered)
        self.arrays = list(scattered) + list(gathered)
        self.na = len(self.arrays)
        any_spec = pl.BlockSpec(memory_space=pl.ANY)
        self.in_specs = [any_spec] * self.na
        self.out_specs = [any_spec] * self.na
        self.out_shape = ([jax.ShapeDtypeStruct(a.shape, a.dtype) for a in scattered]
                          + [jax.ShapeDtypeStruct((N_DEV,) + a.shape, a.dtype) for a in gathered])
        self.scratch = [pltpu.SemaphoreType.DMA((self.na, N_DEV - 1)), pltpu.SemaphoreType.DMA((self.na, N_DEV - 1)),
                        pltpu.SemaphoreType.DMA((self.na,))]

    def _src(self, ins, q, slot):
        return ins[q].at[slot] if q < self.ns else ins[q]

    def _local(self, ins, outs, sems):
        me = _mesh_pos()[3]
        return [pltpu.make_async_copy(self._src(ins, q, me), outs[q].at[me], sems[2].at[q]) for q in range(self.na)]

    def _direct(self, ins, outs, sems, relations, arrays):
        x, y, c, me = _mesh_pos()
        copies = []
        for k in relations:
            peer, pidx = _peer(x, y, c, k)
            for q in arrays:
                copies.append(pltpu.make_async_remote_copy(
                    src_ref=self._src(ins, q, pidx), dst_ref=outs[q].at[me], send_sem=sems[0].at[q, k - 1],
                    recv_sem=sems[1].at[q, k - 1], device_id=peer, device_id_type=MESH))
        return copies

    def _arrivals(self, ins, outs, sems, relations, arrays):
        x, y, c, _ = _mesh_pos()
        copies = []
        for k in relations:
            peer, pidx = _peer(x, y, c, k)
            for q in arrays:
                copies.append(pltpu.make_async_remote_copy(
                    src_ref=self._src(ins, q, pidx), dst_ref=outs[q].at[pidx], send_sem=sems[0].at[q, k - 1],
                    recv_sem=sems[1].at[q, k - 1], device_id=peer, device_id_type=MESH))
        return copies

    def _relays(self, outs, sems):
        x, y, c, _ = _mesh_pos()
        sibling, _ = _peer(x, y, c, 1)
        copies = []
        for k in self.SAME_CORE:
            _, pidx = _peer(x, y, c, k)
            for q in range(self.ns, self.na):
                copies.append(pltpu.make_async_remote_copy(
                    src_ref=outs[q].at[pidx], dst_ref=outs[q].at[pidx], send_sem=sems[0].at[q, k],
                    recv_sem=sems[1].at[q, k], device_id=sibling, device_id_type=MESH))
        return copies

    def _sends(self, ins, outs, sems):
        return (self._direct(ins, outs, sems, range(1, N_DEV), range(self.ns))
                + self._direct(ins, outs, sems, (1,) + self.SAME_CORE, range(self.ns, self.na)))

    def start(self, ins, outs, sems):
        for cp in self._local(ins, outs, sems) + self._sends(ins, outs, sems):
            cp.start()

    def relay(self, ins, outs, sems):
        for cp in self._arrivals(ins, outs, sems, self.SAME_CORE, range(self.ns, self.na)):
            cp.wait_recv()
        for cp in self._relays(outs, sems):
            cp.start()

    def wait(self, ins, outs, sems):
        for cp in (self._arrivals(ins, outs, sems, range(1, N_DEV), range(self.ns))
                   + self._arrivals(ins, outs, sems, (1, 3, 5, 7), range(self.ns, self.na))):
            cp.wait_recv()
        for cp in self._sends(ins, outs, sems) + self._relays(outs, sems):
            cp.wait_send()
        for cp in self._local(ins, outs, sems):
            cp.wait()


def adamw(parts_list, w, m, v, tile, name, ex=None):
    npart, _, dp = parts_list[0].shape
    d = w.shape[-1]
    counts = [p.shape[1] // tile for p in parts_list]
    starts = [sum(counts[:q]) for q in range(len(counts))]
    n_lists = len(parts_list)

    def body(*refs):
        p_refs = refs[:n_lists]
        w_ref, m_ref, v_ref, g_ref, dw_ref, nm_ref, nv_ref = refs[n_lists:]
        i = pl.program_id(0)
        for q, p_ref in enumerate(p_refs):
            @pl.when((i >= starts[q]) & (i < starts[q] + counts[q]))
            def _(p_ref=p_ref):
                acc = p_ref[0, :, 0:d].astype(F32)
                for k in range(1, npart):
                    acc = acc + p_ref[k, :, 0:d].astype(F32)
                g_ref[...] = acc

        g = g_ref[...]
        nm = ADAM_B1 * m_ref[...] + (1.0 - ADAM_B1) * g
        nv = ADAM_B2 * v_ref[...] + (1.0 - ADAM_B2) * (g * g)
        nm_ref[...] = nm
        nv_ref[...] = nv
        m_hat = nm / (1.0 - ADAM_B1 ** ADAM_STEP)
        v_hat = nv / (1.0 - ADAM_B2 ** ADAM_STEP)
        dw_ref[...] = -ADAM_LR * (m_hat / (jnp.sqrt(v_hat) + ADAM_EPS) + ADAM_WD * w_ref[...])

    def part_rows(q):
        return lambda i: (0, jnp.clip(i - starts[q], 0, counts[q] - 1), 0)

    if w.ndim == 3:
        rows = pl.BlockSpec((None, tile, d), lambda i: (i // counts[0], i % counts[0], 0))
    else:
        rows = pl.BlockSpec((tile, d), lambda i: (i, 0))
    own, hosted = _call_hosting(
        body, name, sum(counts),
        in_specs=[pl.BlockSpec((npart, tile, dp), part_rows(q)) for q in range(n_lists)] + [rows, rows, rows],
        out_specs=[rows] * 4, out_shape=[jax.ShapeDtypeStruct(w.shape, F32)] * 4, scratch=[],
        args=(*parts_list, w, m, v), ex=ex)
    return (*own, hosted) if ex is not None else tuple(own)


def _pad_rows(a, rows):
    return jnp.pad(a, ((0, rows - a.shape[0]), (0, 0)))


def _pack_rest(w_att, w_sg, w_ssm, w_out):
    parts = []
    for l in range(2):
        parts += [w_att[l], w_sg[l], w_ssm[l], w_out[l]]
    return jnp.concatenate(parts, axis=0)


def _unpack_rest(p):
    outs = [[], [], [], []]
    o = 0
    for l in range(2):
        for q, rws in enumerate(REST_PARTS):
            outs[q].append(p[o:o + rws])
            o += rws
    return [jnp.stack(t) for t in outs]


def _pack_win(w_in):
    return jnp.pad(w_in.reshape(2 * D_MODEL, WIN_SHARD), ((0, 0), (0, WIN_LANES - WIN_SHARD)))


W_IN_MAP = ((0, 1024, "att", 0), (1024, 1280, "att", 2048), (1280, 2304, "att", 1024), (2304, 5376, "sg", 0),
            (5376, 7424, "ssm", 3072), (7424, 10496, "ssm", 0), (10496, 10528, "ssm", 5120), (10528, 13600, "gate", 0))
SLAB_COLS = {"att": ATT_COLS, "sg": SG_COLS, "ssm": SSM_COLS, "gate": GATE_COLS}


def _slab_pieces(name):
    pieces, filled = [], 0
    for ga, gb, _, off in sorted((m for m in W_IN_MAP if m[2] == name), key=lambda m: m[3]):
        assert off == filled
        a = ga
        while a < gb:
            d = a // WIN_SHARD
            hi = min(gb, WIN_SHARD * (d + 1))
            pieces.append((d, a - WIN_SHARD * d, hi - WIN_SHARD * d))
            a = hi
        filled += gb - ga
    return pieces, filled


def _slabs_from_shards(g):
    names = tuple(SLAB_COLS)
    tr = 256

    def body(g_ref, *out_refs):
        for name, o_ref in zip(names, out_refs):
            pieces, filled = _slab_pieces(name)
            cols = [g_ref[d, :, lo:hi].astype(F32) for d, lo, hi in pieces]
            if filled < SLAB_COLS[name]:
                cols.append(jnp.zeros((tr, SLAB_COLS[name] - filled), F32))
            o_ref[...] = jnp.concatenate(cols, axis=1).astype(o_ref.dtype)

    outs = pl.pallas_call(
        body, name="slabs_from_shards", grid=(D_MODEL // tr,),
        in_specs=[pl.BlockSpec((N_DEV, tr, WIN_LANES), lambda i: (0, i, 0))],
        out_specs=[pl.BlockSpec((tr, SLAB_COLS[n]), lambda i: (i, 0)) for n in names],
        out_shape=[jax.ShapeDtypeStruct((D_MODEL, SLAB_COLS[n]), MXU_DTYPE) for n in names],
        compiler_params=_cparams(("parallel",)),
    )(g)
    return dict(zip(names, outs))


def _shards_from_slabs(dslabs):
    names = tuple(SLAB_COLS)
    tr = 256

    def body(*refs):
        in_refs, o_ref = dict(zip(names, refs[:-1])), refs[-1]
        for d in range(N_DEV):
            a, b = WIN_SHARD * d, WIN_SHARD * (d + 1)
            cols = []
            for ga, gb, name, off in W_IN_MAP:
                lo, hi = max(a, ga), min(b, gb)
                if lo < hi:
                    cols.append(in_refs[name][:, off + lo - ga:off + hi - ga].astype(F32))
            cols.append(jnp.zeros((tr, WIN_LANES - WIN_SHARD), F32))
            o_ref[d] = jnp.concatenate(cols, axis=1).astype(o_ref.dtype)

    return pl.pallas_call(
        body, name="shards_from_slabs", grid=(D_MODEL // tr,),
        in_specs=[pl.BlockSpec((tr, SLAB_COLS[n]), lambda i: (i, 0)) for n in names],
        out_specs=pl.BlockSpec((N_DEV, tr, WIN_LANES), lambda i: (0, i, 0)),
        out_shape=jax.ShapeDtypeStruct((N_DEV, D_MODEL, WIN_LANES), WIRE_DTYPE),
        compiler_params=_cparams(("parallel",)),
    )(*[dslabs[n] for n in names])


SMALL_SIZES = (("norm_pre", 2048), ("norm_post", 2048), ("rel_bias", 512), ("att_sinks", 32), ("sg_ln_g", 2048),
               ("sg_ln_b", 2048), ("sg_w", 262144), ("sg_b", 2048), ("ssm_conv_b", 6144), ("ssm_dt_bias", 64),
               ("ssm_a_log", 64), ("ssm_d", 64), ("ssm_norm_g", 4096), ("conv_w_full", 24576))


def _pack_small(d):
    parts = []
    for name, size in SMALL_SIZES:
        rows = 8 * (-(-size // (8 * D_MODEL)))
        flat = d[name].reshape(-1) if name in d else jnp.zeros((size,), F32)
        parts.append(jnp.pad(flat, (0, rows * D_MODEL - size)).reshape(rows, D_MODEL))
    return _pad_rows(jnp.concatenate(parts, axis=0), SMALL_ROWS)


def _unpack_small(p, shapes):
    out, o = {}, 0
    for name, size in SMALL_SIZES:
        rows = 8 * (-(-size // (8 * D_MODEL)))
        if name in shapes:
            out[name] = p[o:o + rows].reshape(-1)[:size].reshape(shapes[name])
        o += rows
    return out


def _bucket_onehot_t():
    qi = jnp.arange(BLK, dtype=jnp.int32)[None, :]
    kj = jnp.arange(BLK, dtype=jnp.int32)[:, None]
    dd = (qi - kj) & (BLK - 1)
    in_window = dd >= 0
    max_exact = REL_BUCKETS // 2
    dist_f = jnp.maximum(dd, 1).astype(F32)
    large = max_exact + (jnp.log(dist_f / max_exact) / math.log(128 / max_exact)
                         * (REL_BUCKETS - max_exact)).astype(jnp.int32)
    large = jnp.minimum(large, REL_BUCKETS - 1)
    bucket = jnp.where(dd < max_exact, dd, large).reshape(1, -1)
    onehot_t = (bucket == jnp.arange(REL_BUCKETS, dtype=jnp.int32)[:, None]).astype(F32)
    maskadd = jnp.where(in_window, 0.0, NEG).astype(F32).reshape(1, -1)
    return onehot_t, maskadd


WEIGHTS = ['w_in', 'norm_pre', 'norm_post', 'rel_bias', 'att_sinks', 'sg_ln_g', 'sg_ln_b', 'sg_w', 'sg_b',
           'ssm_conv_w', 'ssm_conv_b', 'ssm_dt_bias', 'ssm_a_log', 'ssm_d', 'ssm_norm_g',
           'w_br_att', 'w_br_sg', 'w_br_ssm', 'w_out']
REST = ('w_br_att', 'w_br_sg', 'w_br_ssm', 'w_out')


def kernel(x, w_in, norm_pre, norm_post, rel_bias, att_sinks, sg_ln_g, sg_ln_b, sg_w, sg_b, ssm_conv_w, ssm_conv_b, ssm_dt_bias, ssm_a_log, ssm_d, ssm_norm_g, w_br_att, w_br_sg, w_br_ssm, w_out, loss_target, m_w_in, m_norm_pre, m_norm_post, m_rel_bias, m_att_sinks, m_sg_ln_g, m_sg_ln_b, m_sg_w, m_sg_b, m_ssm_conv_w, m_ssm_conv_b, m_ssm_dt_bias, m_ssm_a_log, m_ssm_d, m_ssm_norm_g, m_w_br_att, m_w_br_sg, m_w_br_ssm, m_w_out, v_w_in, v_norm_pre, v_norm_post, v_rel_bias, v_att_sinks, v_sg_ln_g, v_sg_ln_b, v_sg_w, v_sg_b, v_ssm_conv_w, v_ssm_conv_b, v_ssm_dt_bias, v_ssm_a_log, v_ssm_d, v_ssm_norm_g, v_w_br_att, v_w_br_sg, v_w_br_ssm, v_w_out):
    w = dict(w_in=w_in, norm_pre=norm_pre, norm_post=norm_post, rel_bias=rel_bias, att_sinks=att_sinks,
             sg_ln_g=sg_ln_g, sg_ln_b=sg_ln_b, sg_w=sg_w, sg_b=sg_b, ssm_conv_w=ssm_conv_w, ssm_conv_b=ssm_conv_b,
             ssm_dt_bias=ssm_dt_bias, ssm_a_log=ssm_a_log, ssm_d=ssm_d, ssm_norm_g=ssm_norm_g,
             w_br_att=w_br_att, w_br_sg=w_br_sg, w_br_ssm=w_br_ssm, w_out=w_out)
    mom = dict(w_in=m_w_in, norm_pre=m_norm_pre, norm_post=m_norm_post, rel_bias=m_rel_bias, att_sinks=m_att_sinks,
               sg_ln_g=m_sg_ln_g, sg_ln_b=m_sg_ln_b, sg_w=m_sg_w, sg_b=m_sg_b, ssm_conv_w=m_ssm_conv_w,
               ssm_conv_b=m_ssm_conv_b, ssm_dt_bias=m_ssm_dt_bias, ssm_a_log=m_ssm_a_log, ssm_d=m_ssm_d,
               ssm_norm_g=m_ssm_norm_g, w_br_att=m_w_br_att, w_br_sg=m_w_br_sg, w_br_ssm=m_w_br_ssm, w_out=m_w_out)
    var = dict(w_in=v_w_in, norm_pre=v_norm_pre, norm_post=v_norm_post, rel_bias=v_rel_bias, att_sinks=v_att_sinks,
               sg_ln_g=v_sg_ln_g, sg_ln_b=v_sg_ln_b, sg_w=v_sg_w, sg_b=v_sg_b, ssm_conv_w=v_ssm_conv_w,
               ssm_conv_b=v_ssm_conv_b, ssm_dt_bias=v_ssm_dt_bias, ssm_a_log=v_ssm_a_log, ssm_d=v_ssm_d,
               ssm_norm_g=v_ssm_norm_g, w_br_att=v_w_br_att, w_br_sg=v_w_br_sg, w_br_ssm=v_w_br_ssm, w_out=v_w_out)
    xs0 = x[0]
    target = loss_target[0]
    my_dev = 4 * lax.axis_index("x") + 2 * lax.axis_index("y") + lax.axis_index("c")

    conv_shard = _pad_rows(ssm_conv_w.reshape(-1, D_MODEL), 8)
    win_shard = _pack_win(w_in).astype(WIRE_DTYPE)
    rest_shard = _pack_rest(*[w[n] for n in REST]).astype(WIRE_DTYPE)
    layer_shards = [[win_shard[l * D_MODEL:(l + 1) * D_MODEL], rest_shard[l * LAYER_REST:(l + 1) * LAYER_REST]]
                    for l in range(2)]
    h0, (g_win0, gathered_conv) = rmsnorm_fwd(xs0, norm_pre[0][None], Exchange([], [layer_shards[0][0], conv_shard]))
    conv_full = gathered_conv[:, 0:3].reshape(N_DEV, 2, 4, 384).transpose(1, 2, 0, 3).reshape(2, 4, CONV_DIM)

    def set_rest(lw, g_rest):
        o = 0
        for name, rws in zip(("att", "sg", "ssm", "out"), REST_PARTS):
            lw[name] = g_rest[:, o:o + rws].reshape(N_DEV * rws, D_MODEL).astype(MXU_DTYPE)
            o += rws

    def layer_weights(l, g_win):
        slabs = _slabs_from_shards(g_win)
        lw = {"in_" + name: slab.astype(MXU_DTYPE) for name, slab in slabs.items()}
        lw["in_att"] = lw["in_att"].T
        tril = jnp.tril(jnp.ones((BLK, BLK), bool))
        sgw = jnp.where(tril[None], sg_w[l], 0.0)
        lw.update(
            g_pre=norm_pre[l][None], g_post=norm_post[l][None], sinks=jnp.repeat(att_sinks[l], BLK).reshape(2, GROUP_LANES),
            ln_g=sg_ln_g[l][None], ln_b=sg_ln_b[l][None], sgw=sgw.astype(MXU_DTYPE),
            sgw_t=sgw.transpose(0, 2, 1).astype(MXU_DTYPE), sgb_t=_pad_lanes(sg_b[l].T),
            cw=conv_full[l], cb=ssm_conv_b[l][None], dtb=_pad_lanes(ssm_dt_bias[l][None]),
            alog=_pad_lanes(ssm_a_log[l][None]), dsk=_pad_lanes(ssm_d[l][None]), ng=ssm_norm_g[l][None])
        return lw

    onehot_t, maskadd = _bucket_onehot_t()
    bias = bias_table(rel_bias.T, onehot_t, maskadd).reshape(2, GROUP_HEADS, BLK, BLK).transpose(0, 2, 1, 3)
    bias = bias.reshape(2, BLK, GROUP_LANES)

    saved = []
    xl = xs0
    layers = [layer_weights(0, g_win0)]
    for l in range(2):
        lw = layers[l]
        h = h0 if l == 0 else rmsnorm_fwd(xl, lw["g_pre"])
        pa = mm_nt(lw["in_att"], h, 1152, "proj_att")
        ps = mm_nn(h, lw["in_sg"], 1536, "proj_sg")
        pm = mm_nn(h, lw["in_ssm"], 1792, "proj_ssm")
        pg = mm_nn(h, lw["in_gate"], 1536, "proj_gate")
        if l == 0:
            ya, (g_rest0,) = attn_fwd(pa, bias, lw["sinks"], Exchange([], [layer_shards[0][1]]))
            set_rest(lw, g_rest0)
        else:
            ya = attn_fwd(pa, bias, lw["sinks"])
        sgu_args = (ps, lw["ln_g"], lw["ln_b"], lw["sgw"], lw["sgb_t"])
        ssd_args = (pm, lw["cw"], lw["cb"], lw["dtb"], lw["alog"], lw["dsk"], lw["ng"])
        ys = sgu_fwd(*sgu_args)
        if l == 0:
            ym, states, conv_pre, (g_win1, g_rest1) = ssd_fwd(*ssd_args, Exchange([], layer_shards[1]))
            layers.append(layer_weights(1, g_win1))
            set_rest(layers[1], g_rest1)
        else:
            ym, states, conv_pre = ssd_fwd(*ssd_args)
        merge_args = (xl, ya, ys, ym, pg, lw["att"], lw["sg"], lw["ssm"], lw["out"], lw["g_post"])
        if l == 0:
            x_next, ba, bs, bm, merged, out_s = merge_fwd(*merge_args)
        else:
            dx, loss_part, ba, bs, bm, merged, out_s = merge_fwd(*merge_args, target)
        saved.append(dict(x=xl, h=h, pa=pa, ps=ps, pm=pm, pg=pg, ya=ya, ys=ys, ym=ym, states=states,
                          conv_pre=conv_pre, ba=ba, bs=bs, bm=bm, merged=merged, out_s=out_s))
        xl = x_next

    loss = lax.psum(loss_part[0, 0], ("x", "y", "c"))

    dbias = jnp.zeros((2, BLK, GROUP_LANES), F32)
    win_grads, rest_grads = [None, None], [None, None]
    small = {n: [None, None] for n in ("norm_pre", "norm_post", "att_sinks", "sg_ln_g", "sg_ln_b", "sg_w", "sg_b",
                                       "ssm_conv_b", "ssm_dt_bias", "ssm_a_log", "ssm_d", "ssm_norm_g",
                                       "conv_w_full")}
    for l in (1, 0):
        lw, sv = layers[l], saved[l]
        dout, dba, dbs, dbm, dpg, dya, dys, dym, dg_post = merge_bwd(
            dx, sv["out_s"], sv["pg"], sv["ba"], sv["bs"], sv["bm"], lw["att"], lw["sg"], lw["ssm"], lw["out"],
            lw["g_post"])
        dw_out = mm_tn(sv["merged"], dout, 1024, "dw_out")
        dw_att = mm_kn(sv["ya"], dba, 1024, "dw_br_att")
        dw_sg = mm_tn(sv["ys"], dbs, 1024, "dw_br_sg")
        dw_ssm = mm_tn(sv["ym"], dbm, 1024, "dw_br_ssm")
        rest_grads[l] = jnp.concatenate(
            [dw_att.reshape(N_DEV, 128, D_MODEL), dw_sg.reshape(N_DEV, 128, D_MODEL),
             dw_ssm.reshape(N_DEV, 256, D_MODEL), dw_out.reshape(N_DEV, 128, D_MODEL)], axis=1).astype(WIRE_DTYPE)
        dpa, dbias, dsinks = attn_bwd(sv["pa"], dya, bias, lw["sinks"], dbias)
        dps, dsgw, dsgb_t, dln_g, dln_b = sgu_bwd(sv["ps"], dys, lw["ln_g"], lw["ln_b"], lw["sgw"], lw["sgw_t"],
                                                  lw["sgb_t"])
        ssd_args = (sv["pm"], sv["conv_pre"], dym, sv["states"], lw["cw"], lw["dtb"], lw["alog"], lw["dsk"], lw["ng"])
        if l == 0:
            dpm, dcw, dcb, dvec, dng, (recv_win1, recv_rest0) = ssd_bwd(
                *ssd_args, Exchange([win_grads[1], rest_grads[0]], []))
        else:
            dpm, dcw, dcb, dvec, dng, (recv_rest1,) = ssd_bwd(*ssd_args, Exchange([rest_grads[1]], []))
        dslabs = dict(att=mm_kn(dpa, sv["h"], 1152, "dw_in_att").T, sg=mm_tn(sv["h"], dps, 3072, "dw_in_sg"),
                      ssm=mm_tn(sv["h"], dpm, 2688, "dw_in_ssm"), gate=mm_tn(sv["h"], dpg, 3072, "dw_in_gate"))
        win_grads[l] = _shards_from_slabs(dslabs)
        dh_args = ([dpa, dps, dpm, dpg], [lw["in_att"], lw["in_sg"], lw["in_ssm"], lw["in_gate"]], sv["x"],
                   lw["g_pre"], dx)
        if l == 0:
            dx, dg_pre, (recv_win0,) = dh_norm_bwd(*dh_args, Exchange([win_grads[0]], []))
        else:
            dx, dg_pre = dh_norm_bwd(*dh_args)
        small["norm_pre"][l] = dg_pre[0]
        small["norm_post"][l] = dg_post[0]
        small["att_sinks"][l] = dsinks[0, :ATT_HEADS]
        small["sg_ln_g"][l] = dln_g[0]
        small["sg_ln_b"][l] = dln_b[0]
        small["sg_w"][l] = dsgw
        small["sg_b"][l] = dsgb_t[:, :SG_GROUPS].T
        small["ssm_conv_b"][l] = dcb[0]
        small["ssm_dt_bias"][l] = dvec[0, :SSM_HEADS]
        small["ssm_a_log"][l] = dvec[1, :SSM_HEADS]
        small["ssm_d"][l] = dvec[2, :SSM_HEADS]
        small["ssm_norm_g"][l] = dng[0]
        small["conv_w_full"][l] = dcw[0:4]
    grad_x = dx
    dbias = dbias.reshape(2, BLK, GROUP_HEADS, BLK).transpose(0, 2, 1, 3).reshape(ATT_HEADS, BLK * BLK)
    d_rel_bias = bias_table_bwd(dbias, onehot_t).T

    small_d = {n: jnp.stack(v) for n, v in small.items()}
    small_d["rel_bias"] = d_rel_bias
    *res_win, (recv_small,) = adamw([recv_win0, recv_win1], w_in, m_w_in, v_w_in, WIN_TILE, "adamw_w_in",
                                    Exchange([], [_pack_small(small_d)]))
    res_rest = adamw([recv_rest0, recv_rest1], _pack_rest(*[w[n] for n in REST]), _pack_rest(*[mom[n] for n in REST]),
                     _pack_rest(*[var[n] for n in REST]), REST_TILE, "adamw_rest")
    small_names = [n for n, _ in SMALL_SIZES if n != "conv_w_full"]
    g_s, dw_s, nm_s, nv_s = adamw([recv_small], _pack_small({n: w[n] for n in small_names}),
                                  _pack_small({n: mom[n] for n in small_names}),
                                  _pack_small({n: var[n] for n in small_names}), SMALL_TILE, "adamw_small")
    shapes = {n: w[n].shape for n in small_names}
    shapes["conv_w_full"] = (2, 4, CONV_DIM)
    g_conv_full = _unpack_small(g_s, shapes)["conv_w_full"]
    g_conv = lax.dynamic_slice_in_dim(g_conv_full, my_dev * 384, 384, axis=2)
    pack_conv = lambda a: _pad_rows(a.reshape(-1, D_MODEL), 8)
    g_c, dw_c, nm_c, nv_c = adamw([pack_conv(g_conv)[None]], pack_conv(ssm_conv_w), pack_conv(m_ssm_conv_w),
                                  pack_conv(v_ssm_conv_w), 8, "adamw_conv")

    results = {}
    for q, (tag, psm, pc) in enumerate((("grad", g_s, g_c), ("delta", dw_s, dw_c), ("new_m", nm_s, nm_c),
                                        ("new_v", nv_s, nv_c))):
        r = dict(zip(REST, _unpack_rest(res_rest[q])))
        r["w_in"] = res_win[q]
        r.update(_unpack_small(psm, {n: w[n].shape for n in small_names}))
        r["ssm_conv_w"] = pc[0:3].reshape(2, 4, 384)
        results[tag] = r
    outs = [loss, grad_x[None]]
    for tag in ("grad", "delta", "new_m", "new_v"):
        outs += [results[tag][n] for n in WEIGHTS]
    return tuple(outs)
```

```python
import math

import jax
import jax.numpy as jnp
from jax import lax
from jax.experimental import pallas as pl
from jax.experimental.pallas import tpu as pltpu

F32 = jnp.float32
MXU_DTYPE = jnp.bfloat16
ACT_DTYPE = jnp.bfloat16
WIRE_DTYPE = jnp.bfloat16
HI = lax.Precision.HIGHEST
MESH = pl.DeviceIdType.MESH

D_MODEL = 1024
N_DEV = 8
ATT_HEADS = 16
HEAD_DIM = 64
BLK = 128
SG_GROUPS = 8
SSM_WIDTH = 2048
SSM_HEADS = 32
SSM_GROUPS = 4
SSM_GW = SSM_WIDTH // SSM_GROUPS
CONV_DIM = 3072
REL_BUCKETS = 32
EPS = 1e-6
NEG = -1e30

ATT_COLS = 2304
SG_COLS = 3072
SSM_COLS = 5376
GATE_COLS = 3072
DT_OFF = 5120

VMEM_LIMIT_V7X = 56 * 2 ** 20
DH_TK = 768

ADAM_LR, ADAM_B1, ADAM_B2, ADAM_EPS, ADAM_WD, ADAM_STEP = 0.001, 0.9, 0.999, 1e-08, 0.01, 10

WIN_SHARD = 1700
WIN_LANES = 1792
REST_PARTS = (128, 128, 256, 128)
LAYER_REST = sum(REST_PARTS)
REST_TILE = 128
WIN_TILE = 128
SMALL_ROWS = 384
SMALL_TILE = 128


def _cparams(sem=None):
    return pltpu.CompilerParams(dimension_semantics=sem, vmem_limit_bytes=VMEM_LIMIT_V7X)


def _dot(a, b):
    return jnp.dot(a.astype(MXU_DTYPE), b.astype(MXU_DTYPE), preferred_element_type=F32)


def _dot_nt(a, b):
    return lax.dot_general(a.astype(MXU_DTYPE), b.astype(MXU_DTYPE), (((1,), (1,)), ((), ())),
                           preferred_element_type=F32)


def _dot_tn(a, b):
    return lax.dot_general(a.astype(MXU_DTYPE), b.astype(MXU_DTYPE), (((0,), (0,)), ((), ())),
                           preferred_element_type=F32)


def _dot_hi(a, b):
    return jnp.dot(a, b, precision=HI, preferred_element_type=F32)


def _dot_onehot(a, onehot):
    hi = a.astype(jnp.bfloat16)
    lo = (a - hi.astype(F32)).astype(jnp.bfloat16)
    return (jnp.dot(hi, onehot, preferred_element_type=F32) + jnp.dot(lo, onehot, preferred_element_type=F32))


def _dot_hi_nt(a, b):
    return lax.dot_general(a, b, (((1,), (1,)), ((), ())), precision=HI, preferred_element_type=F32)


def _sig(x):
    return 1.0 / (1.0 + jnp.exp(-x))


def _dsilu(x, s):
    return s * (1.0 + x * (1.0 - s))


def _full(shape):
    nd = len(shape)
    return pl.BlockSpec(shape, lambda *_: (0,) * nd)


def rmsnorm_fwd(x, g, ex=None):
    s, d = x.shape
    tm = min(512, s)

    def body(x_ref, g_ref, o_ref):
        xv = x_ref[...]
        r = lax.rsqrt(jnp.mean(xv * xv, axis=-1, keepdims=True) + EPS)
        o_ref[...] = (xv * r * g_ref[...]).astype(o_ref.dtype)

    (h,), hosted = _call_hosting(
        body, "rmsnorm_fwd", s // tm,
        in_specs=[pl.BlockSpec((tm, d), lambda i: (i, 0)), _full((1, d))],
        out_specs=[pl.BlockSpec((tm, d), lambda i: (i, 0))],
        out_shape=[jax.ShapeDtypeStruct((s, d), ACT_DTYPE)], scratch=[], args=(x, g), ex=ex)
    return (h, hosted) if ex is not None else h


def mm_nn(a, b, tn, name):
    s, k = a.shape
    n = b.shape[1]
    tm = min(2048, s)

    def body(a_ref, b_ref, o_ref):
        o_ref[...] = _dot(a_ref[...], b_ref[...]).astype(o_ref.dtype)

    return pl.pallas_call(
        body, name=name, grid=(s // tm, n // tn),
        in_specs=[pl.BlockSpec((tm, k), lambda i, j: (i, 0)), pl.BlockSpec((k, tn), lambda i, j: (0, j))],
        out_specs=pl.BlockSpec((tm, tn), lambda i, j: (i, j)),
        out_shape=jax.ShapeDtypeStruct((s, n), ACT_DTYPE),
        compiler_params=_cparams(("parallel", "arbitrary")),
    )(a, b)


def mm_nt(a, b, tm, name):
    m, k = a.shape
    s = b.shape[0]
    ts = min(2048, s)

    def body(a_ref, b_ref, o_ref):
        o_ref[...] = _dot_nt(a_ref[...], b_ref[...]).astype(o_ref.dtype)

    return pl.pallas_call(
        body, name=name, grid=(s // ts, m // tm),
        in_specs=[pl.BlockSpec((tm, k), lambda i, j: (j, 0)), pl.BlockSpec((ts, k), lambda i, j: (i, 0))],
        out_specs=pl.BlockSpec((tm, ts), lambda i, j: (j, i)),
        out_shape=jax.ShapeDtypeStruct((m, s), ACT_DTYPE),
        compiler_params=_cparams(("parallel", "arbitrary")),
    )(a, b)


def mm_kn(a, b, tm, name):
    m, s = a.shape
    n = b.shape[1]
    ts = min(512, s)
    nt = s // ts

    def body(a_ref, b_ref, o_ref, acc_ref):
        @pl.when(pl.program_id(1) == 0)
        def _():
            acc_ref[...] = jnp.zeros_like(acc_ref)

        acc_ref[...] += _dot(a_ref[...], b_ref[...])

        @pl.when(pl.program_id(1) == nt - 1)
        def _():
            o_ref[...] = acc_ref[...].astype(o_ref.dtype)

    return pl.pallas_call(
        body, name=name, grid=(m // tm, nt),
        in_specs=[pl.BlockSpec((tm, ts), lambda j, t: (j, t)), pl.BlockSpec((ts, n), lambda j, t: (t, 0))],
        out_specs=pl.BlockSpec((tm, n), lambda j, t: (j, 0)),
        out_shape=jax.ShapeDtypeStruct((m, n), WIRE_DTYPE),
        scratch_shapes=[pltpu.VMEM((tm, n), F32)],
        compiler_params=_cparams(("parallel", "arbitrary")),
    )(a, b)


def mm_tn(a, b, tn, name):
    s, k = a.shape
    n = b.shape[1]
    ts = min(512, s)
    nt = s // ts

    def body(a_ref, b_ref, o_ref, acc_ref):
        @pl.when(pl.program_id(1) == 0)
        def _():
            acc_ref[...] = jnp.zeros_like(acc_ref)

        acc_ref[...] += _dot_tn(a_ref[...], b_ref[...])

        @pl.when(pl.program_id(1) == nt - 1)
        def _():
            o_ref[...] = acc_ref[...].astype(o_ref.dtype)

    return pl.pallas_call(
        body, name=name, grid=(n // tn, nt),
        in_specs=[pl.BlockSpec((ts, k), lambda j, t: (t, 0)), pl.BlockSpec((ts, tn), lambda j, t: (t, j))],
        out_specs=pl.BlockSpec((k, tn), lambda j, t: (0, j)),
        out_shape=jax.ShapeDtypeStruct((k, n), WIRE_DTYPE),
        scratch_shapes=[pltpu.VMEM((k, tn), F32)],
        compiler_params=_cparams(("parallel", "arbitrary")),
    )(a, b)


def dh_norm_bwd(dslabs, wslabs, x, g, dres, ex=None):
    s, d = x.shape
    tm = min(1024, s)
    widths = [ds.shape[0 if q == 0 else 1] for q, ds in enumerate(dslabs)]
    tks = [DH_TK] * len(widths)
    counts = [wd // t for wd, t in zip(widths, tks)]
    starts = [sum(counts[:i]) for i in range(len(counts))]
    nk = sum(counts)
    ns = len(dslabs)

    hosted = ex is not None
    ni = s // tm

    def mm_body(*refs):
        (own_in, (dh_ref,), _), hosted_refs = _split_hosted(refs, 2 * ns, 1, 0, ex)
        d_refs, w_refs = own_in[:ns], own_in[ns:]
        i, k = pl.program_id(0), pl.program_id(1)
        if hosted:
            @pl.when((i == 0) & (k == 0))
            def _():
                ex.start(*hosted_refs)

            @pl.when((i == ni - 1) & (k == nk - 1))
            def _():
                ex.relay(*hosted_refs)
                ex.wait(*hosted_refs)

        @pl.when(k == 0)
        def _():
            dh_ref[...] = jnp.zeros_like(dh_ref)

        for q in range(ns):
            @pl.when((k >= starts[q]) & (k < starts[q] + counts[q]))
            def _(q=q):
                if q == 0:
                    dh_ref[...] += _dot_tn(d_refs[q][...], w_refs[q][...])
                else:
                    dh_ref[...] += _dot_nt(d_refs[q][...], w_refs[q][...])

    def clamp(q):
        if q == 0:
            return pl.BlockSpec((tks[q], tm), lambda i, k: (jnp.clip(k - starts[q], 0, counts[q] - 1), i))
        return pl.BlockSpec((tm, tks[q]), lambda i, k: (i, jnp.clip(k - starts[q], 0, counts[q] - 1)))

    def clamp_w(q):
        if q == 0:
            return pl.BlockSpec((tks[q], d), lambda i, k: (jnp.clip(k - starts[q], 0, counts[q] - 1), 0))
        return pl.BlockSpec((d, tks[q]), lambda i, k: (0, jnp.clip(k - starts[q], 0, counts[q] - 1)))

    res = pl.pallas_call(
        mm_body, name="dh_matmul_scatter" if hosted else "dh_matmul", grid=(ni, nk),
        in_specs=([clamp(q) for q in range(ns)] + [clamp_w(q) for q in range(ns)]
                  + (ex.in_specs if hosted else [])),
        out_specs=[pl.BlockSpec((tm, d), lambda i, k: (i, 0))] + (ex.out_specs if hosted else []),
        out_shape=[jax.ShapeDtypeStruct((s, d), F32)] + (ex.out_shape if hosted else []),
        scratch_shapes=ex.scratch if hosted else [],
        compiler_params=_cparams(("arbitrary" if hosted else "parallel", "arbitrary")),
    )(*dslabs, *wslabs, *(ex.arrays if hosted else []))
    dh, ex_results = res[0], res[1:]

    te = min(512, s)

    def norm_body(dh_ref, x_ref, g_ref, dres_ref, dx_ref, dg_ref):
        @pl.when(pl.program_id(0) == 0)
        def _():
            dg_ref[...] = jnp.zeros_like(dg_ref)

        xv = x_ref[...]
        r = lax.rsqrt(jnp.mean(xv * xv, axis=-1, keepdims=True) + EPS)
        xn = xv * r
        dhv = dh_ref[...]
        dg_ref[...] += jnp.sum(dhv * xn, axis=0, keepdims=True)
        dxn = dhv * g_ref[...]
        dx_ref[...] = dres_ref[...] + r * (dxn - xn * jnp.mean(dxn * xn, axis=-1, keepdims=True))

    rows = pl.BlockSpec((te, d), lambda i: (i, 0))
    dx, dg = pl.pallas_call(
        norm_body, name="norm_bwd", grid=(s // te,),
        in_specs=[rows, rows, _full((1, d)), rows],
        out_specs=[rows, _full((1, d))],
        out_shape=[jax.ShapeDtypeStruct((s, d), F32), jax.ShapeDtypeStruct((1, d), F32)],
        compiler_params=_cparams(("arbitrary",)),
    )(dh, x, g, dres)
    return (dx, dg, ex_results) if hosted else (dx, dg)


def bias_table(rel_bias_t, onehot_t, maskadd):
    n = onehot_t.shape[1]
    tn = 8192

    def body(r_ref, o_ref, m_ref, out_ref):
        out_ref[...] = _dot_hi(r_ref[...], o_ref[...]) + m_ref[...]

    return pl.pallas_call(
        body, name="bias_table", grid=(n // tn,),
        in_specs=[_full((ATT_HEADS, REL_BUCKETS)), pl.BlockSpec((REL_BUCKETS, tn), lambda j: (0, j)),
                  pl.BlockSpec((1, tn), lambda j: (0, j))],
        out_specs=pl.BlockSpec((ATT_HEADS, tn), lambda j: (0, j)),
        out_shape=jax.ShapeDtypeStruct((ATT_HEADS, n), F32),
        compiler_params=_cparams(("parallel",)),
    )(rel_bias_t, onehot_t, maskadd)


def bias_table_bwd(dbias, onehot_t):
    n = onehot_t.shape[1]
    tn = 8192

    def body(d_ref, o_ref, out_ref):
        @pl.when(pl.program_id(0) == 0)
        def _():
            out_ref[...] = jnp.zeros_like(out_ref)

        out_ref[...] += _dot_hi_nt(d_ref[...], o_ref[...])

    return pl.pallas_call(
        body, name="bias_table_bwd", grid=(n // tn,),
        in_specs=[pl.BlockSpec((ATT_HEADS, tn), lambda j: (0, j)), pl.BlockSpec((REL_BUCKETS, tn), lambda j: (0, j))],
        out_specs=_full((ATT_HEADS, REL_BUCKETS)),
        out_shape=jax.ShapeDtypeStruct((ATT_HEADS, REL_BUCKETS), F32),
        compiler_params=_cparams(("arbitrary",)),
    )(dbias, onehot_t)


def _fold(full, tri):
    return jnp.where(tri, full[BLK:2 * BLK], full[0:BLK])


def _unfold(folded, tri):
    return jnp.concatenate([jnp.where(tri, 0.0, folded), jnp.where(tri, folded, 0.0)], axis=0)


GROUP_HEADS = ATT_HEADS // 2
GROUP_LANES = GROUP_HEADS * BLK


def _att_group(qg, kcat, vt_cat, bias_g, sink_g, tri, no_prev):
    l = _fold(_dot(kcat, qg), tri) * (HEAD_DIM ** -0.5) + bias_g
    l = jnp.where(no_prev, NEG, l)
    m = jnp.maximum(jnp.max(l, axis=0, keepdims=True), sink_g)
    p = jnp.exp(l - m)
    es = jnp.exp(sink_g - m)
    inv = 1.0 / (jnp.sum(p, axis=0, keepdims=True) + es)
    p = p * inv
    pcat = _unfold(p, tri)
    return p, pcat, es * inv, _dot(vt_cat, pcat)


ATT_SUB = 4


def _heads_to_lanes(ref, row0, ln):
    return jnp.concatenate([ref[row0 + j * HEAD_DIM:row0 + (j + 1) * HEAD_DIM, ln] for j in range(GROUP_HEADS)], axis=1)


def _lanes_to_heads(ref, row0, ln, val):
    for j in range(GROUP_HEADS):
        ref[row0 + j * HEAD_DIM:row0 + (j + 1) * HEAD_DIM, ln] = val[:, j * BLK:(j + 1) * BLK].astype(ref.dtype)


def _kv_cat(kvp, kvc, g):
    lo = g * HEAD_DIM
    kt_cat = jnp.concatenate([kvp[lo:lo + HEAD_DIM], kvc[lo:lo + HEAD_DIM]], axis=1)
    vt_cat = jnp.concatenate([kvp[128 + lo:128 + lo + HEAD_DIM], kvc[128 + lo:128 + lo + HEAD_DIM]], axis=1)
    return kt_cat, vt_cat


def _tri_masks(n):
    row = lax.broadcasted_iota(jnp.int32, (BLK, GROUP_LANES), 0)
    query = lax.broadcasted_iota(jnp.int32, (BLK, GROUP_LANES), 1) & (BLK - 1)
    tri = row <= query
    return tri, (n == 0) & jnp.logical_not(tri)


def _split_hosted(refs, n_in, n_out, n_scratch, ex):
    na = ex.na if ex is not None else 0
    o = 0
    parts = []
    for cnt in (n_in, na, n_out, na, n_scratch, 3 if ex is not None else 0):
        parts.append(refs[o:o + cnt])
        o += cnt
    own_in, ex_in, own_out, ex_out, own_scr, ex_sems = parts
    return (own_in, own_out, own_scr), (ex_in, ex_out, ex_sems)


def _call_hosting(body, name, nsteps, in_specs, out_specs, out_shape, scratch, args, ex):
    n_in, n_out, n_scr = len(in_specs), len(out_specs), len(scratch)
    hosted = ex is not None

    def full_body(*refs):
        (own_in, own_out, own_scr), hosted_refs = _split_hosted(refs, n_in, n_out, n_scr, ex)
        if hosted:
            @pl.when(pl.program_id(0) == 0)
            def _():
                ex.start(*hosted_refs)

            @pl.when(pl.program_id(0) == max(nsteps - max(nsteps // 8, 4), 0))
            def _():
                ex.relay(*hosted_refs)

            @pl.when(pl.program_id(0) == nsteps - 1)
            def _():
                ex.wait(*hosted_refs)

        body(*own_in, *own_out, *own_scr)

    res = pl.pallas_call(
        full_body, name=name + "_hosting" if hosted else name, grid=(nsteps,),
        in_specs=list(in_specs) + (ex.in_specs if hosted else []),
        out_specs=list(out_specs) + (ex.out_specs if hosted else []),
        out_shape=list(out_shape) + (ex.out_shape if hosted else []),
        scratch_shapes=list(scratch) + (ex.scratch if hosted else []),
        compiler_params=_cparams(("arbitrary",)),
    )(*args, *(ex.arrays if hosted else []))
    return res[:n_out], res[n_out:]


def attn_fwd(pa, bias, sinks, ex=None):
    s = pa.shape[1]
    nsteps = s // (ATT_SUB * BLK)

    def body(pa_ref, kvp_ref, bias_ref, sink_ref, y_ref):
        for sub in range(ATT_SUB):
            n = pl.program_id(0) * ATT_SUB + sub
            ln = slice(sub * BLK, (sub + 1) * BLK)
            kvc = pa_ref[2048:2304, ln]
            kvp = kvp_ref[...] if sub == 0 else pa_ref[2048:2304, (sub - 1) * BLK:sub * BLK]
            tri, no_prev = _tri_masks(n)
            for g in range(2):
                kt_cat, vt_cat = _kv_cat(kvp, kvc, g)
                row0 = g * GROUP_HEADS * HEAD_DIM
                _, _, _, o = _att_group(_heads_to_lanes(pa_ref, row0, ln), kt_cat.astype(F32).T, vt_cat, bias_ref[g],
                                        sink_ref[g:g + 1, :], tri, no_prev)
                z = _heads_to_lanes(pa_ref, 1024 + row0, ln).astype(F32)
                _lanes_to_heads(y_ref, row0, ln, o * z * _sig(z))

    (y,), hosted = _call_hosting(
        body, "attn_fwd", nsteps,
        in_specs=[pl.BlockSpec((ATT_COLS, ATT_SUB * BLK), lambda n: (0, n)),
                  pl.BlockSpec((256, BLK), lambda n: (8, jnp.maximum(ATT_SUB * n - 1, 0))),
                  _full((2, BLK, GROUP_LANES)), _full((2, GROUP_LANES))],
        out_specs=[pl.BlockSpec((1024, ATT_SUB * BLK), lambda n: (0, n))],
        out_shape=[jax.ShapeDtypeStruct((1024, s), ACT_DTYPE)], scratch=[],
        args=(pa, pa, bias, sinks), ex=ex)
    return (y, hosted) if ex is not None else y


def attn_bwd(pa, dy, bias, sinks, dbias_in):
    s = pa.shape[1]
    nsteps = s // (ATT_SUB * BLK)

    def body(pa_ref, kvp_ref, dy_ref, bias_ref, sink_ref, dbin_ref, dpa_ref, dbias_ref, dsink_ref, carry, dsink_acc):
        i = pl.program_id(0)

        @pl.when(i == 0)
        def _():
            dbias_ref[...] = dbin_ref[...]
            dsink_acc[...] = jnp.zeros_like(dsink_acc)
            carry[...] = jnp.zeros_like(carry)

        scale = HEAD_DIM ** -0.5
        for sub in reversed(range(ATT_SUB)):
            n = (nsteps - 1 - i) * ATT_SUB + sub
            ln = slice(sub * BLK, (sub + 1) * BLK)
            kvc = pa_ref[2048:2304, ln]
            kvp = kvp_ref[...] if sub == 0 else pa_ref[2048:2304, (sub - 1) * BLK:sub * BLK]
            tri, no_prev = _tri_masks(n)
            for g in range(2):
                kt_cat, vt_cat = _kv_cat(kvp, kvc, g)
                row0 = g * GROUP_HEADS * HEAD_DIM
                qg = _heads_to_lanes(pa_ref, row0, ln)
                p, pcat, psink, o = _att_group(qg, kt_cat.astype(F32).T, vt_cat, bias_ref[g], sink_ref[g:g + 1, :],
                                               tri, no_prev)
                z = _heads_to_lanes(pa_ref, 1024 + row0, ln).astype(F32)
                dyg = _heads_to_lanes(dy_ref, row0, ln).astype(F32)
                sz = _sig(z)
                d_o = dyg * z * sz
                _lanes_to_heads(dpa_ref, 1024 + row0, ln, dyg * _dsilu(z, sz) * o)
                delta = jnp.sum(d_o * o, axis=0, keepdims=True)
                dl = p * (_fold(_dot(vt_cat.astype(F32).T, d_o), tri) - delta)
                dsink_acc[g:g + 1, :] += psink * delta
                dbias_ref[g] += dl
                dlcat = _unfold(dl, tri)
                _lanes_to_heads(dpa_ref, row0, ln, _dot(kt_cat, dlcat) * scale)
                for q, dkv in enumerate((_dot_nt(qg, dlcat) * scale, _dot_nt(d_o, pcat))):
                    r0 = q * 128 + g * HEAD_DIM
                    dpa_ref[2048 + r0:2048 + r0 + HEAD_DIM, ln] = (
                        dkv[:, BLK:2 * BLK] + carry[r0:r0 + HEAD_DIM, :]).astype(dpa_ref.dtype)
                    carry[r0:r0 + HEAD_DIM, :] = dkv[:, 0:BLK]

        @pl.when(i == nsteps - 1)
        def _():
            lane = lax.broadcasted_iota(jnp.int32, (1, 128), 1)
            dsink = jnp.zeros((1, 128), F32)
            for h in range(ATT_HEADS):
                g, j = divmod(h, GROUP_HEADS)
                tot = jnp.sum(dsink_acc[g:g + 1, j * BLK:(j + 1) * BLK], axis=1, keepdims=True)
                dsink = dsink + jnp.where(lane == h, -tot, 0.0)
            dsink_ref[...] = dsink

    return pl.pallas_call(
        body, name="attn_bwd", grid=(nsteps,),
        in_specs=[pl.BlockSpec((ATT_COLS, ATT_SUB * BLK), lambda i: (0, nsteps - 1 - i)),
                  pl.BlockSpec((256, BLK), lambda i: (8, jnp.maximum(ATT_SUB * (nsteps - 1 - i) - 1, 0))),
                  pl.BlockSpec((1024, ATT_SUB * BLK), lambda i: (0, nsteps - 1 - i)),
                  _full((2, BLK, GROUP_LANES)), _full((2, GROUP_LANES)), _full((2, BLK, GROUP_LANES))],
        out_specs=[pl.BlockSpec((ATT_COLS, ATT_SUB * BLK), lambda i: (0, nsteps - 1 - i)),
                   _full((2, BLK, GROUP_LANES)), _full((1, 128))],
        out_shape=[jax.ShapeDtypeStruct((ATT_COLS, s), ACT_DTYPE),
                   jax.ShapeDtypeStruct((2, BLK, GROUP_LANES), F32),
                   jax.ShapeDtypeStruct((1, 128), F32)],
        scratch_shapes=[pltpu.VMEM((256, BLK), F32), pltpu.VMEM((2, GROUP_LANES), F32)],
        compiler_params=_cparams(("arbitrary",)),
    )(pa, pa, dy, bias, sinks, dbias_in)


def _layernorm(v, g, b):
    mu = jnp.mean(v, axis=-1, keepdims=True)
    vc = v - mu
    rstd = lax.rsqrt(jnp.mean(vc * vc, axis=-1, keepdims=True) + EPS)
    xhat = vc * rstd
    return xhat, rstd, xhat * g + b


def sgu_fwd(ps, ln_g, ln_b, w_tril, b_t):
    s = ps.shape[0]
    rows = min(4 * BLK, s)

    def body(ps_ref, g_ref, b_ref, w_ref, bt_ref, y_ref):
        u = ps_ref[:, 0:1024].astype(F32)
        v = ps_ref[:, 1024:2048].astype(F32)
        z = ps_ref[:, 2048:3072].astype(F32)
        _, _, vn = _layernorm(v, g_ref[...], b_ref[...])
        gate = u * z * _sig(z)
        for c in range(rows // BLK):
            ch = slice(c * BLK, (c + 1) * BLK)
            for g in range(SG_GROUPS):
                sl = slice(g * 128, (g + 1) * 128)
                mixed = _dot(w_ref[g], vn[ch, sl]) + bt_ref[:, g:g + 1]
                y_ref[ch, sl] = (gate[ch, sl] * mixed).astype(y_ref.dtype)

    return pl.pallas_call(
        body, name="sgu_fwd", grid=(s // rows,),
        in_specs=[pl.BlockSpec((rows, SG_COLS), lambda c: (c, 0)), _full((1, 1024)), _full((1, 1024)),
                  _full((SG_GROUPS, BLK, BLK)), _full((BLK, 128))],
        out_specs=pl.BlockSpec((rows, 1024), lambda c: (c, 0)),
        out_shape=jax.ShapeDtypeStruct((s, 1024), ACT_DTYPE),
        compiler_params=_cparams(("parallel",)),
    )(ps, ln_g, ln_b, w_tril, b_t)


def sgu_bwd(ps, dy, ln_g, ln_b, w_tril, w_tril_t, b_t):
    s = ps.shape[0]
    rows = min(4 * BLK, s)

    def body(ps_ref, dy_ref, g_ref, b_ref, w_ref, wt_ref, bt_ref, dps_ref, dw_ref, dbt_ref, dg_ref, db_ref, dvn_scr):
        @pl.when(pl.program_id(0) == 0)
        def _():
            dw_ref[...] = jnp.zeros_like(dw_ref)
            dbt_ref[...] = jnp.zeros_like(dbt_ref)
            dg_ref[...] = jnp.zeros_like(dg_ref)
            db_ref[...] = jnp.zeros_like(db_ref)

        u = ps_ref[:, 0:1024].astype(F32)
        v = ps_ref[:, 1024:2048].astype(F32)
        z = ps_ref[:, 2048:3072].astype(F32)
        dy = dy_ref[...].astype(F32)
        xhat, rstd, vn = _layernorm(v, g_ref[...], b_ref[...])
        sz = _sig(z)
        silu = z * sz
        row = lax.broadcasted_iota(jnp.int32, (BLK, BLK), 0)
        colm = lax.broadcasted_iota(jnp.int32, (BLK, BLK), 1)
        tril = row >= colm
        dbt = jnp.zeros((BLK, 128), F32)
        dsilu_z = _dsilu(z, sz)
        for c in range(rows // BLK):
            ch = slice(c * BLK, (c + 1) * BLK)
            for g in range(SG_GROUPS):
                sl = slice(g * 128, (g + 1) * 128)
                vng = vn[ch, sl]
                mixed = _dot(w_ref[g], vng) + bt_ref[:, g:g + 1]
                dyg, ug = dy[ch, sl], u[ch, sl]
                dps_ref[ch, sl] = (dyg * mixed * silu[ch, sl]).astype(dps_ref.dtype)
                dps_ref[ch, 2048 + g * 128:2048 + (g + 1) * 128] = (
                    dyg * ug * mixed * dsilu_z[ch, sl]).astype(dps_ref.dtype)
                dm = dyg * ug * silu[ch, sl]
                dw_ref[g] += jnp.where(tril, _dot_nt(dm, vng), 0.0)
                dbt = dbt + jnp.where(colm == g, jnp.sum(dm, axis=1, keepdims=True), 0.0)
                dvn_scr[ch, sl] = _dot(wt_ref[g], dm)
        dbt_ref[...] += dbt
        dvn = dvn_scr[...]
        dg_ref[...] += jnp.sum(dvn * xhat, axis=0, keepdims=True)
        db_ref[...] += jnp.sum(dvn, axis=0, keepdims=True)
        dxh = dvn * g_ref[...]
        dv = rstd * (dxh - jnp.mean(dxh, axis=-1, keepdims=True)
                     - xhat * jnp.mean(dxh * xhat, axis=-1, keepdims=True))
        dps_ref[:, 1024:2048] = dv.astype(dps_ref.dtype)

    return pl.pallas_call(
        body, name="sgu_bwd", grid=(s // rows,),
        in_specs=[pl.BlockSpec((rows, SG_COLS), lambda c: (c, 0)), pl.BlockSpec((rows, 1024), lambda c: (c, 0)),
                  _full((1, 1024)), _full((1, 1024)), _full((SG_GROUPS, BLK, BLK)), _full((SG_GROUPS, BLK, BLK)),
                  _full((BLK, 128))],
        out_specs=[pl.BlockSpec((rows, SG_COLS), lambda c: (c, 0)), _full((SG_GROUPS, BLK, BLK)), _full((BLK, 128)),
                   _full((1, 1024)), _full((1, 1024))],
        out_shape=[jax.ShapeDtypeStruct((s, SG_COLS), ACT_DTYPE), jax.ShapeDtypeStruct((SG_GROUPS, BLK, BLK), F32),
                   jax.ShapeDtypeStruct((BLK, 128), F32), jax.ShapeDtypeStruct((1, 1024), F32),
                   jax.ShapeDtypeStruct((1, 1024), F32)],
        scratch_shapes=[pltpu.VMEM((rows, 1024), F32)],
        compiler_params=_cparams(("arbitrary",)),
    )(ps, dy, ln_g, ln_b, w_tril, w_tril_t, b_t)


def _shift_down(cur, prev16, k):
    if k == 0:
        return cur
    r = pltpu.roll(cur, k, 0)
    rp = pltpu.roll(prev16, k, 0)
    row = lax.broadcasted_iota(jnp.int32, (8, cur.shape[1]), 0)
    return jnp.concatenate([jnp.where(row < k, rp[0:8], r[0:8]), r[8:]], axis=0)


def _shift_up(cur, next16, k):
    if k == 0:
        return cur
    n = cur.shape[0]
    r = pltpu.roll(cur, n - k, 0)
    rn = pltpu.roll(next16, 16 - k, 0)
    row = lax.broadcasted_iota(jnp.int32, (8, cur.shape[1]), 0)
    return jnp.concatenate([r[:n - 8], jnp.where(row >= 8 - k, rn[8:16], r[n - 8:])], axis=0)


def _bcast8(v):
    return jnp.broadcast_to(v, (16, v.shape[1]))


def _causal_conv(xbc, prev16, cw, cbias):
    ext = jnp.concatenate([jnp.zeros((BLK - 16, xbc.shape[1]), F32), prev16, xbc], axis=0)
    row = lax.broadcasted_iota(jnp.int32, (BLK, 2 * BLK), 0)
    col = lax.broadcasted_iota(jnp.int32, (BLK, 2 * BLK), 1)
    pre = cbias + cw[3:4] * xbc
    for k in (1, 2, 3):
        pre = pre + cw[3 - k:4 - k] * _dot((col == row + BLK - k).astype(F32), ext)
    return pre


class _Ssd:
    def __init__(self, pre, dtr, dtb, alog, dsk, tri, e):
        self.pre = pre
        self.sg = _sig(pre)
        act = pre * self.sg
        self.xs = act[:, 0:SSM_WIDTH]
        self.bm = act[:, SSM_WIDTH:SSM_WIDTH + 512]
        self.cm = act[:, SSM_WIDTH + 512:CONV_DIM]
        self.dtp = dtr + dtb
        self.dt = jnp.maximum(self.dtp, 0.0) + jnp.log(1.0 + jnp.exp(-jnp.abs(self.dtp)))
        self.a = -jnp.exp(alog)
        self.acs = _dot_hi(tri, self.dt * self.a)
        self.acs_t = self.acs.T
        tot = self.acs[BLK - 1:BLK]
        self.ecs = jnp.exp(self.acs)
        self.dte = jnp.exp(tot - self.acs)
        self.cd = jnp.exp(tot)
        self.dt_x = _dot_onehot(self.dt, e)
        self.ecs_x = _dot_onehot(self.ecs, e)
        self.dte_x = _dot_onehot(self.dte, e)
        self.cd_x = _dot_onehot(_bcast8(self.cd), e)[0:1]
        self.d_x = _dot_onehot(_bcast8(dsk), e)[0:1]
        self.xdt = self.xs * self.dt_x
        row = lax.broadcasted_iota(jnp.int32, (BLK, BLK), 0)
        col = lax.broadcasted_iota(jnp.int32, (BLK, BLK), 1)
        self.tril = row >= col

    def group(self, g):
        sl = slice(g * 128, (g + 1) * 128)
        bg, cg = self.bm[:, sl], self.cm[:, sl]
        return bg, cg, _dot_nt(cg, bg)

    def decay(self, h):
        seg = self.acs[:, h:h + 1] - self.acs_t[h:h + 1, :]
        return jnp.exp(jnp.where(self.tril, seg, NEG))

    def y_pre_gate(self, ht_of, yd_scr, yoff_scr):
        for g in range(SSM_GROUPS):
            bg, cg, cb = self.group(g)
            for j in range(8):
                h = g * 8 + j
                sl = slice(h * 64, (h + 1) * 64)
                yd_scr[:, sl] = _dot(cb * self.decay(h), self.xdt[:, sl])
            gs = slice(g * SSM_GW, (g + 1) * SSM_GW)
            yoff_scr[:, gs] = _dot(cg, ht_of(g)) * self.ecs_x[:, gs]
        return yd_scr[...] + yoff_scr[...] + self.d_x * self.xs


def _ssd_consts():
    hh = lax.broadcasted_iota(jnp.int32, (128, SSM_WIDTH), 0)
    ch = lax.broadcasted_iota(jnp.int32, (128, SSM_WIDTH), 1)
    e = (ch // 64 == hh).astype(jnp.bfloat16)
    row = lax.broadcasted_iota(jnp.int32, (BLK, BLK), 0)
    col = lax.broadcasted_iota(jnp.int32, (BLK, BLK), 1)
    tri = (row >= col).astype(F32)
    return tri, e


def _pad_lanes(v, n=128):
    return jnp.pad(v, ((0, 0), (0, n - v.shape[1])))


def ssd_fwd(pm, cw, cbias, dtb, alog, dsk, ng, ex=None):
    s = pm.shape[0]
    nc = s // BLK
    tri, e = _ssd_consts()

    def body(pm_ref, prev_ref, cw_ref, cb_ref, dtb_ref, al_ref, d_ref, ng_ref, tri_ref, e_ref,
             y_ref, st_ref, pre_ref, ht_ref, yd_scr, yoff_scr):
        c = pl.program_id(0)

        @pl.when(c == 0)
        def _():
            ht_ref[...] = jnp.zeros_like(ht_ref)

        xbc = pm_ref[:, 0:CONV_DIM].astype(F32)
        prev16 = jnp.where(c == 0, 0.0, prev_ref[...].astype(F32))
        pre = _causal_conv(xbc, prev16, cw_ref[...], cb_ref[...])
        pre_ref[...] = pre.astype(pre_ref.dtype)
        f = _Ssd(pre, pm_ref[:, DT_OFF:DT_OFF + 128].astype(F32), dtb_ref[...], al_ref[...], d_ref[...],
                 tri_ref[...], e_ref[...])
        st_ref[0] = ht_ref[...]
        y = f.y_pre_gate(lambda g: ht_ref[g], yd_scr, yoff_scr)
        for g in range(SSM_GROUPS):
            bg, _, _ = f.group(g)
            gs = slice(g * SSM_GW, (g + 1) * SSM_GW)
            ht_ref[g] = ht_ref[g] * f.cd_x[:, gs] + _dot_tn(bg, f.xdt[:, gs] * f.dte_x[:, gs])
        z = pm_ref[:, CONV_DIM:CONV_DIM + SSM_WIDTH].astype(F32)
        ypre = y * z * _sig(z)
        for g in range(SSM_GROUPS):
            gs = slice(g * SSM_GW, (g + 1) * SSM_GW)
            yg = ypre[:, gs]
            rr = lax.rsqrt(jnp.mean(yg * yg, axis=-1, keepdims=True) + EPS)
            y_ref[:, gs] = (yg * rr * ng_ref[:, gs]).astype(y_ref.dtype)

    own, hosted = _call_hosting(
        body, "ssd_fwd", nc,
        in_specs=[pl.BlockSpec((BLK, SSM_COLS), lambda c: (c, 0)),
                  pl.BlockSpec((16, CONV_DIM), lambda c: (jnp.maximum(8 * c - 1, 0), 0)),
                  _full((4, CONV_DIM)), _full((1, CONV_DIM)), _full((1, 128)), _full((1, 128)), _full((1, 128)),
                  _full((1, SSM_WIDTH)), _full((BLK, BLK)), _full((128, SSM_WIDTH))],
        out_specs=[pl.BlockSpec((BLK, SSM_WIDTH), lambda c: (c, 0)),
                   pl.BlockSpec((1, SSM_GROUPS, 128, SSM_GW), lambda c: (c, 0, 0, 0)),
                   pl.BlockSpec((BLK, CONV_DIM), lambda c: (c, 0))],
        out_shape=[jax.ShapeDtypeStruct((s, SSM_WIDTH), ACT_DTYPE),
                   jax.ShapeDtypeStruct((nc, SSM_GROUPS, 128, SSM_GW), F32),
                   jax.ShapeDtypeStruct((s, CONV_DIM), ACT_DTYPE)],
        scratch=[pltpu.VMEM((SSM_GROUPS, 128, SSM_GW), F32), pltpu.VMEM((BLK, SSM_WIDTH), F32),
                 pltpu.VMEM((BLK, SSM_WIDTH), F32)],
        args=(pm, pm, cw, cbias, dtb, alog, dsk, ng, tri, e), ex=ex)
    return (*own, hosted) if ex is not None else tuple(own)


def ssd_bwd(pm, pre, dy, states, cw, dtb, alog, dsk, ng, ex=None):
    s = pm.shape[0]
    nc = s // BLK
    tri, e = _ssd_consts()
    tri_t, e_t = tri.T, e.T

    def body(pm_ref, pre_ref, dy_ref, st_ref, cw_ref, dtb_ref, al_ref, d_ref, ng_ref,
             tri_ref, trit_ref, e_ref, et_ref,
             dpm_ref, dcw_ref, dcb_ref, dvec_ref, dng_ref,
             dht_ref, dcar_ref, yd_scr, yoff_scr, dx_scr, r2_scr, hs_scr, da_scr, dat_scr, dd_scr, dbc_scr):
        i = pl.program_id(0)
        n = nc - 1 - i

        @pl.when(i == 0)
        def _():
            dht_ref[...] = jnp.zeros_like(dht_ref)
            dcar_ref[...] = jnp.zeros_like(dcar_ref)
            dcw_ref[...] = jnp.zeros_like(dcw_ref)
            dcb_ref[...] = jnp.zeros_like(dcb_ref)
            dvec_ref[...] = jnp.zeros_like(dvec_ref)
            dng_ref[...] = jnp.zeros_like(dng_ref)
            dd_scr[...] = jnp.zeros_like(dd_scr)
            da_scr[...] = jnp.zeros_like(da_scr)
            dat_scr[...] = jnp.zeros_like(dat_scr)

        cw = cw_ref[...]
        f = _Ssd(pre_ref[...].astype(F32), pm_ref[:, DT_OFF:DT_OFF + 128].astype(F32), dtb_ref[...], al_ref[...],
                 d_ref[...], tri_ref[...], e_ref[...])
        et = et_ref[...]
        y = f.y_pre_gate(lambda g: st_ref[0, g], yd_scr, yoff_scr)

        z = pm_ref[:, CONV_DIM:CONV_DIM + SSM_WIDTH].astype(F32)
        dyv = dy_ref[...].astype(F32)
        sz = _sig(z)
        silu = z * sz
        ypre = y * silu
        for g in range(SSM_GROUPS):
            gs = slice(g * SSM_GW, (g + 1) * SSM_GW)
            yg = ypre[:, gs]
            rr = lax.rsqrt(jnp.mean(yg * yg, axis=-1, keepdims=True) + EPS)
            nrm = yg * rr
            dng_ref[:, gs] += jnp.sum(dyv[:, gs] * nrm, axis=0, keepdims=True)
            dn = dyv[:, gs] * ng_ref[:, gs]
            dx_scr[:, gs] = rr * (dn - nrm * jnp.mean(dn * nrm, axis=-1, keepdims=True))
        dypre = dx_scr[...]
        d_y = dypre * silu
        dpm_ref[:, CONV_DIM:CONV_DIM + SSM_WIDTH] = (dypre * y * _dsilu(z, sz)).astype(dpm_ref.dtype)

        for g in range(SSM_GROUPS):
            bg, cg, cb = f.group(g)
            gs = slice(g * SSM_GW, (g + 1) * SSM_GW)
            htg = st_ref[0, g]
            dhn = dht_ref[g]
            dcb = jnp.zeros((BLK, BLK), F32)
            for j in range(8):
                h = g * 8 + j
                sl = slice(h * 64, (h + 1) * 64)
                dec = f.decay(h)
                dyh = d_y[:, sl]
                dmd = _dot_nt(dyh, f.xdt[:, sl]) * dec
                dcb = dcb + dmd
                gm = dmd * cb
                da_scr[:, h:h + 1] = jnp.sum(gm, axis=1, keepdims=True)
                dat_scr[h:h + 1, :] = jnp.sum(gm, axis=0, keepdims=True)
                dx_scr[:, sl] = _dot_tn(cb * dec, dyh)
            dz = f.ecs_x[:, gs] * d_y[:, gs]
            dbc_scr[:, 512 + g * 128:512 + (g + 1) * 128] = _dot(dcb, bg) + _dot_nt(dz, htg)
            dbc_scr[:, g * 128:(g + 1) * 128] = _dot_tn(dcb, cg) + _dot_nt(f.xdt[:, gs] * f.dte_x[:, gs], dhn)
            dws = _dot(bg, dhn)
            dx_scr[:, gs] += f.dte_x[:, gs] * dws
            r2_scr[:, gs] = dws * f.xdt[:, gs]
            hs_scr[:, gs] = _bcast8(jnp.sum(dhn * htg, axis=0, keepdims=True))
            dht_ref[g] = f.cd_x[:, gs] * dhn + _dot_tn(cg, dz)
        d_x = dx_scr[...]
        r1 = _dot(d_y * yoff_scr[...], et)
        r2 = _dot(r2_scr[...], et) * f.dte
        dcd = _dot_onehot(hs_scr[...], et)[0:1]
        d_tot = jnp.sum(r2, axis=0, keepdims=True) + dcd * f.cd
        row = lax.broadcasted_iota(jnp.int32, (BLK, 128), 0)
        d_a = da_scr[...] - dat_scr[...].T + r1 - r2 + jnp.where(row == BLK - 1, d_tot, 0.0)
        dadt = _dot_hi(trit_ref[...], d_a)
        ddt = dadt * f.a + _dot(d_x * f.xs, et)
        lane = lax.broadcasted_iota(jnp.int32, (BLK, 128), 1)
        dr = jnp.where(lane < SSM_HEADS, ddt * _sig(f.dtp), 0.0)
        dvec_ref[0:1, :] += jnp.sum(dr, axis=0, keepdims=True)
        dvec_ref[1:2, :] += jnp.sum(dadt * f.dt, axis=0, keepdims=True) * f.a
        dd_scr[...] += _bcast8(jnp.sum(d_y * f.xs, axis=0, keepdims=True))
        dpm_ref[:, DT_OFF:DT_OFF + 128] = dr.astype(dpm_ref.dtype)
        dpm_ref[:, DT_OFF + 128:SSM_COLS] = jnp.zeros((BLK, 128), dpm_ref.dtype)

        dxs = d_x * f.dt_x + f.d_x * d_y
        dact = jnp.concatenate([dxs, dbc_scr[...]], axis=1)
        dpre = dact * _dsilu(f.pre, f.sg)
        dcb_ref[...] += jnp.sum(dpre, axis=0, keepdims=True)
        xbc = pm_ref[:, 0:CONV_DIM].astype(F32)
        dxraw = jnp.zeros((BLK, CONV_DIM), F32)
        nxt = dcar_ref[...]
        for k in range(4):
            ahead = _shift_up(dpre, nxt, k)
            dcw_ref[3 - k:4 - k, :] += jnp.sum(ahead * xbc, axis=0, keepdims=True)
            dxraw = dxraw + cw[3 - k:4 - k] * ahead
        dcar_ref[...] = dpre[0:16]
        dpm_ref[:, 0:CONV_DIM] = dxraw.astype(dpm_ref.dtype)

        @pl.when(i == nc - 1)
        def _():
            dvec_ref[2:3, :] = _dot_onehot(dd_scr[...], et)[0:1]

    own, hosted = _call_hosting(
        body, "ssd_bwd", nc,
        in_specs=[pl.BlockSpec((BLK, SSM_COLS), lambda i: (nc - 1 - i, 0)),
                  pl.BlockSpec((BLK, CONV_DIM), lambda i: (nc - 1 - i, 0)),
                  pl.BlockSpec((BLK, SSM_WIDTH), lambda i: (nc - 1 - i, 0)),
                  pl.BlockSpec((1, SSM_GROUPS, 128, SSM_GW), lambda i: (nc - 1 - i, 0, 0, 0)),
                  _full((4, CONV_DIM)), _full((1, 128)), _full((1, 128)), _full((1, 128)),
                  _full((1, SSM_WIDTH)), _full((BLK, BLK)), _full((BLK, BLK)), _full((128, SSM_WIDTH)),
                  _full((SSM_WIDTH, 128))],
        out_specs=[pl.BlockSpec((BLK, SSM_COLS), lambda i: (nc - 1 - i, 0)),
                   _full((8, CONV_DIM)), _full((1, CONV_DIM)), _full((8, 128)), _full((1, SSM_WIDTH))],
        out_shape=[jax.ShapeDtypeStruct((s, SSM_COLS), ACT_DTYPE), jax.ShapeDtypeStruct((8, CONV_DIM), F32),
                   jax.ShapeDtypeStruct((1, CONV_DIM), F32), jax.ShapeDtypeStruct((8, 128), F32),
                   jax.ShapeDtypeStruct((1, SSM_WIDTH), F32)],
        scratch=[pltpu.VMEM((SSM_GROUPS, 128, SSM_GW), F32), pltpu.VMEM((16, CONV_DIM), F32),
                 pltpu.VMEM((BLK, SSM_WIDTH), F32), pltpu.VMEM((BLK, SSM_WIDTH), F32),
                 pltpu.VMEM((BLK, SSM_WIDTH), F32), pltpu.VMEM((BLK, SSM_WIDTH), F32),
                 pltpu.VMEM((16, SSM_WIDTH), F32), pltpu.VMEM((BLK, 128), F32), pltpu.VMEM((128, BLK), F32),
                 pltpu.VMEM((16, SSM_WIDTH), F32), pltpu.VMEM((BLK, 1024), F32)],
        args=(pm, pre, dy, states, cw, dtb, alog, dsk, ng, tri, tri_t, e, e_t), ex=ex)
    return (*own, hosted) if ex is not None else tuple(own)


def merge_fwd(x, ya, ys, ym, pg, wa, ws, wm, wo, g_post, target=None):
    s, d = x.shape
    tm = min(256, s)
    with_loss = target is not None

    def body(*refs):
        x_ref, ya_ref, ys_ref, ym_ref, pg_ref, wa_ref, ws_ref, wm_ref, wo_ref, g_ref = refs[:10]
        if with_loss:
            t_ref, xo_ref, l_ref, ba_ref, bs_ref, bm_ref, mg_ref, out_ref = refs[10:]
        else:
            xo_ref, ba_ref, bs_ref, bm_ref, mg_ref, out_ref = refs[10:]
        ba = _dot_tn(ya_ref[...], wa_ref[...])
        bs = _dot(ys_ref[...], ws_ref[...])
        bm = _dot(ym_ref[...], wm_ref[...])
        merged = (_sig(pg_ref[:, 0:d].astype(F32)) * ba + _sig(pg_ref[:, d:2 * d].astype(F32)) * bs
                  + _sig(pg_ref[:, 2 * d:3 * d].astype(F32)) * bm)
        out = _dot(merged, wo_ref[...])
        r = lax.rsqrt(jnp.mean(out * out, axis=-1, keepdims=True) + EPS)
        y = x_ref[...] + out * r * g_ref[...]
        if with_loss:
            @pl.when(pl.program_id(0) == 0)
            def _():
                l_ref[...] = jnp.zeros_like(l_ref)

            err = y - t_ref[...]
            xo_ref[...] = err * (1.0 / d)
            part = jnp.sum(jnp.sum(err * err, axis=-1, keepdims=True) * (1.0 / d), axis=0, keepdims=True)
            l_ref[...] += 0.5 * jnp.broadcast_to(part, l_ref.shape)
        else:
            xo_ref[...] = y
        ba_ref[...] = ba.astype(ba_ref.dtype)
        bs_ref[...] = bs.astype(bs_ref.dtype)
        bm_ref[...] = bm.astype(bm_ref.dtype)
        mg_ref[...] = merged.astype(mg_ref.dtype)
        out_ref[...] = out.astype(out_ref.dtype)

    rows = lambda w: pl.BlockSpec((tm, w), lambda i: (i, 0))
    act = jax.ShapeDtypeStruct((s, d), ACT_DTYPE)
    loss_spec = [_full((8, 128))] if with_loss else []
    loss_shape = [jax.ShapeDtypeStruct((8, 128), F32)] if with_loss else []
    return pl.pallas_call(
        body, name="merge_fwd_loss" if with_loss else "merge_fwd", grid=(s // tm,),
        in_specs=[rows(d), pl.BlockSpec((d, tm), lambda i: (0, i)), rows(d), rows(2 * d), rows(3 * d), _full((d, d)),
                  _full((d, d)), _full((2 * d, d)), _full((d, d)), _full((1, d))] + ([rows(d)] if with_loss else []),
        out_specs=[rows(d)] + loss_spec + [rows(d)] * 5,
        out_shape=[jax.ShapeDtypeStruct((s, d), F32)] + loss_shape + [act] * 5,
        compiler_params=_cparams(("arbitrary" if with_loss else "parallel",)),
    )(x, ya, ys, ym, pg, wa, ws, wm, wo, g_post, *([target] if with_loss else []))


def merge_bwd(dx, out_s, pg, ba, bs, bm, wa, ws, wm, wo, g_post):
    s, d = dx.shape
    tm = min(256, s)

    def body(dx_ref, out_ref, pg_ref, ba_ref, bs_ref, bm_ref, wa_ref, ws_ref, wm_ref, wo_ref, g_ref,
             dout_ref, dba_ref, dbs_ref, dbm_ref, dpg_ref, dya_ref, dys_ref, dym_ref, dg_ref):
        @pl.when(pl.program_id(0) == 0)
        def _():
            dg_ref[...] = jnp.zeros_like(dg_ref)

        o = out_ref[...].astype(F32)
        dxv = dx_ref[...]
        r = lax.rsqrt(jnp.mean(o * o, axis=-1, keepdims=True) + EPS)
        nrm = o * r
        dg_ref[...] += jnp.sum(dxv * nrm, axis=0, keepdims=True)
        dn = dxv * g_ref[...]
        dout = r * (dn - nrm * jnp.mean(dn * nrm, axis=-1, keepdims=True))
        dout_ref[...] = dout.astype(dout_ref.dtype)
        dmerged = _dot_nt(dout, wo_ref[...])
        for q, (b_ref, db_ref, w_ref, dy_ref) in enumerate(((ba_ref, dba_ref, wa_ref, dya_ref),
                                                            (bs_ref, dbs_ref, ws_ref, dys_ref),
                                                            (bm_ref, dbm_ref, wm_ref, dym_ref))):
            gt = _sig(pg_ref[:, q * d:(q + 1) * d].astype(F32))
            db = dmerged * gt
            db_ref[...] = db.astype(db_ref.dtype)
            dpg_ref[:, q * d:(q + 1) * d] = (dmerged * b_ref[...].astype(F32) * gt * (1.0 - gt)).astype(dpg_ref.dtype)
            if q == 0:
                dy_ref[...] = _dot_nt(w_ref[...], db).astype(dy_ref.dtype)
            else:
                dy_ref[...] = _dot_nt(db, w_ref[...]).astype(dy_ref.dtype)

    rows = lambda w: pl.BlockSpec((tm, w), lambda i: (i, 0))
    act = lambda w: jax.ShapeDtypeStruct((s, w), ACT_DTYPE)
    return pl.pallas_call(
        body, name="merge_bwd", grid=(s // tm,),
        in_specs=[rows(d), rows(d), rows(3 * d), rows(d), rows(d), rows(d), _full((d, d)), _full((d, d)),
                  _full((2 * d, d)), _full((d, d)), _full((1, d))],
        out_specs=[rows(d), rows(d), rows(d), rows(d), rows(3 * d), pl.BlockSpec((d, tm), lambda i: (0, i)), rows(d),
                   rows(2 * d), _full((1, d))],
        out_shape=[act(d), act(d), act(d), act(d), act(3 * d), jax.ShapeDtypeStruct((d, s), ACT_DTYPE), act(d),
                   act(2 * d), jax.ShapeDtypeStruct((1, d), F32)],
        compiler_params=_cparams(("arbitrary",)),
    )(dx, out_s, pg, ba, bs, bm, wa, ws, wm, wo, g_post)


def _mesh_pos():
    x, y, c = lax.axis_index("x"), lax.axis_index("y"), lax.axis_index("c")
    return x, y, c, 4 * x + 2 * y + c


def _peer(x, y, c, k):
    px = 1 - x if k & 4 else x
    py = 1 - y if k & 2 else y
    pc = 1 - c if k & 1 else c
    return (px, py, pc), 4 * px + 2 * py + pc


class Exchange:
    SAME_CORE = (2, 4, 6)

    def __init__(self, scattered, gathered):
        self.ns = len(scattered)
        self.arrays = list(scattered) + list(gathered)
        self.na = len(self.arrays)
        any_spec = pl.BlockSpec(memory_space=pl.ANY)
        self.in_specs = [any_spec] * self.na
        self.out_specs = [any_spec] * self.na
        self.out_shape = ([jax.ShapeDtypeStruct(a.shape, a.dtype) for a in scattered]
                          + [jax.ShapeDtypeStruct((N_DEV,) + a.shape, a.dtype) for a in gathered])
        self.scratch = [pltpu.SemaphoreType.DMA((self.na, N_DEV - 1)), pltpu.SemaphoreType.DMA((self.na, N_DEV - 1)),
                        pltpu.SemaphoreType.DMA((self.na,))]

    def _src(self, ins, q, slot):
        return ins[q].at[slot] if q < self.ns else ins[q]

    def _local(self, ins, outs, sems):
        me = _mesh_pos()[3]
        return [pltpu.make_async_copy(self._src(ins, q, me), outs[q].at[me], sems[2].at[q]) for q in range(self.na)]

    def _direct(self, ins, outs, sems, relations, arrays):
        x, y, c, me = _mesh_pos()
        copies = []
        for k in relations:
            peer, pidx = _peer(x, y, c, k)
            for q in arrays:
                copies.append(pltpu.make_async_remote_copy(
                    src_ref=self._src(ins, q, pidx), dst_ref=outs[q].at[me], send_sem=sems[0].at[q, k - 1],
                    recv_sem=sems[1].at[q, k - 1], device_id=peer, device_id_type=MESH))
        return copies

    def _arrivals(self, ins, outs, sems, relations, arrays):
        x, y, c, _ = _mesh_pos()
        copies = []
        for k in relations:
            peer, pidx = _peer(x, y, c, k)
            for q in arrays:
                copies.append(pltpu.make_async_remote_copy(
                    src_ref=self._src(ins, q, pidx), dst_ref=outs[q].at[pidx], send_sem=sems[0].at[q, k - 1],
                    recv_sem=sems[1].at[q, k - 1], device_id=peer, device_id_type=MESH))
        return copies

    def _relays(self, outs, sems):
        x, y, c, _ = _mesh_pos()
        sibling, _ = _peer(x, y, c, 1)
        copies = []
        for k in self.SAME_CORE:
            _, pidx = _peer(x, y, c, k)
            for q in range(self.ns, self.na):
                copies.append(pltpu.make_async_remote_copy(
                    src_ref=outs[q].at[pidx], dst_ref=outs[q].at[pidx], send_sem=sems[0].at[q, k],
                    recv_sem=sems[1].at[q, k], device_id=sibling, device_id_type=MESH))
        return copies

    def _sends(self, ins, outs, sems):
        return (self._direct(ins, outs, sems, range(1, N_DEV), range(self.ns))
                + self._direct(ins, outs, sems, (1,) + self.SAME_CORE, range(self.ns, self.na)))

    def start(self, ins, outs, sems):
        for cp in self._local(ins, outs, sems) + self._sends(ins, outs, sems):
            cp.start()

    def relay(self, ins, outs, sems):
        for cp in self._arrivals(ins, outs, sems, self.SAME_CORE, range(self.ns, self.na)):
            cp.wait_recv()
        for cp in self._relays(outs, sems):
            cp.start()

    def wait(self, ins, outs, sems):
        for cp in (self._arrivals(ins, outs, sems, range(1, N_DEV), range(self.ns))
                   + self._arrivals(ins, outs, sems, (1, 3, 5, 7), range(self.ns, self.na))):
            cp.wait_recv()
        for cp in self._sends(ins, outs, sems) + self._relays(outs, sems):
            cp.wait_send()
        for cp in self._local(ins, outs, sems):
            cp.wait()


def adamw(parts_list, w, m, v, tile, name, ex=None):
    npart, _, dp = parts_list[0].shape
    d = w.shape[-1]
    counts = [p.shape[1] // tile for p in parts_list]
    starts = [sum(counts[:q]) for q in range(len(counts))]
    n_lists = len(parts_list)

    def body(*refs):
        p_refs = refs[:n_lists]
        w_ref, m_ref, v_ref, g_ref, dw_ref, nm_ref, nv_ref = refs[n_lists:]
        i = pl.program_id(0)
        for q, p_ref in enumerate(p_refs):
            @pl.when((i >= starts[q]) & (i < starts[q] + counts[q]))
            def _(p_ref=p_ref):
                acc = p_ref[0, :, 0:d].astype(F32)
                for k in range(1, npart):
                    acc = acc + p_ref[k, :, 0:d].astype(F32)
                g_ref[...] = acc

        g = g_ref[...]
        nm = ADAM_B1 * m_ref[...] + (1.0 - ADAM_B1) * g
        nv = ADAM_B2 * v_ref[...] + (1.0 - ADAM_B2) * (g * g)
        nm_ref[...] = nm
        nv_ref[...] = nv
        m_hat = nm / (1.0 - ADAM_B1 ** ADAM_STEP)
        v_hat = nv / (1.0 - ADAM_B2 ** ADAM_STEP)
        dw_ref[...] = -ADAM_LR * (m_hat / (jnp.sqrt(v_hat) + ADAM_EPS) + ADAM_WD * w_ref[...])

    def part_rows(q):
        return lambda i: (0, jnp.clip(i - starts[q], 0, counts[q] - 1), 0)

    if w.ndim == 3:
        rows = pl.BlockSpec((None, tile, d), lambda i: (i // counts[0], i % counts[0], 0))
    else:
        rows = pl.BlockSpec((tile, d), lambda i: (i, 0))
    own, hosted = _call_hosting(
        body, name, sum(counts),
        in_specs=[pl.BlockSpec((npart, tile, dp), part_rows(q)) for q in range(n_lists)] + [rows, rows, rows],
        out_specs=[rows] * 4, out_shape=[jax.ShapeDtypeStruct(w.shape, F32)] * 4, scratch=[],
        args=(*parts_list, w, m, v), ex=ex)
    return (*own, hosted) if ex is not None else tuple(own)


def _pad_rows(a, rows):
    return jnp.pad(a, ((0, rows - a.shape[0]), (0, 0)))


def _pack_rest(w_att, w_sg, w_ssm, w_out):
    parts = []
    for l in range(2):
        parts += [w_att[l], w_sg[l], w_ssm[l], w_out[l]]
    return jnp.concatenate(parts, axis=0)


def _unpack_rest(p):
    outs = [[], [], [], []]
    o = 0
    for l in range(2):
        for q, rws in enumerate(REST_PARTS):
            outs[q].append(p[o:o + rws])
            o += rws
    return [jnp.stack(t) for t in outs]


def _pack_win(w_in):
    return jnp.pad(w_in.reshape(2 * D_MODEL, WIN_SHARD), ((0, 0), (0, WIN_LANES - WIN_SHARD)))


W_IN_MAP = ((0, 1024, "att", 0), (1024, 1280, "att", 2048), (1280, 2304, "att", 1024), (2304, 5376, "sg", 0),
            (5376, 7424, "ssm", 3072), (7424, 10496, "ssm", 0), (10496, 10528, "ssm", 5120), (10528, 13600, "gate", 0))
SLAB_COLS = {"att": ATT_COLS, "sg": SG_COLS, "ssm": SSM_COLS, "gate": GATE_COLS}


def _slab_pieces(name):
    pieces, filled = [], 0
    for ga, gb, _, off in sorted((m for m in W_IN_MAP if m[2] == name), key=lambda m: m[3]):
        assert off == filled
        a = ga
        while a < gb:
            d = a // WIN_SHARD
            hi = min(gb, WIN_SHARD * (d + 1))
            pieces.append((d, a - WIN_SHARD * d, hi - WIN_SHARD * d))
            a = hi
        filled += gb - ga
    return pieces, filled


def _slabs_from_shards(g):
    names = tuple(SLAB_COLS)
    tr = 256

    def body(g_ref, *out_refs):
        for name, o_ref in zip(names, out_refs):
            pieces, filled = _slab_pieces(name)
            cols = [g_ref[d, :, lo:hi].astype(F32) for d, lo, hi in pieces]
            if filled < SLAB_COLS[name]:
                cols.append(jnp.zeros((tr, SLAB_COLS[name] - filled), F32))
            o_ref[...] = jnp.concatenate(cols, axis=1).astype(o_ref.dtype)

    outs = pl.pallas_call(
        body, name="slabs_from_shards", grid=(D_MODEL // tr,),
        in_specs=[pl.BlockSpec((N_DEV, tr, WIN_LANES), lambda i: (0, i, 0))],
        out_specs=[pl.BlockSpec((tr, SLAB_COLS[n]), lambda i: (i, 0)) for n in names],
        out_shape=[jax.ShapeDtypeStruct((D_MODEL, SLAB_COLS[n]), MXU_DTYPE) for n in names],
        compiler_params=_cparams(("parallel",)),
    )(g)
    return dict(zip(names, outs))


def _shards_from_slabs(dslabs):
    names = tuple(SLAB_COLS)
    tr = 256

    def body(*refs):
        in_refs, o_ref = dict(zip(names, refs[:-1])), refs[-1]
        for d in range(N_DEV):
            a, b = WIN_SHARD * d, WIN_SHARD * (d + 1)
            cols = []
            for ga, gb, name, off in W_IN_MAP:
                lo, hi = max(a, ga), min(b, gb)
                if lo < hi:
                    cols.append(in_refs[name][:, off + lo - ga:off + hi - ga].astype(F32))
            cols.append(jnp.zeros((tr, WIN_LANES - WIN_SHARD), F32))
            o_ref[d] = jnp.concatenate(cols, axis=1).astype(o_ref.dtype)

    return pl.pallas_call(
        body, name="shards_from_slabs", grid=(D_MODEL // tr,),
        in_specs=[pl.BlockSpec((tr, SLAB_COLS[n]), lambda i: (i, 0)) for n in names],
        out_specs=pl.BlockSpec((N_DEV, tr, WIN_LANES), lambda i: (0, i, 0)),
        out_shape=jax.ShapeDtypeStruct((N_DEV, D_MODEL, WIN_LANES), WIRE_DTYPE),
        compiler_params=_cparams(("parallel",)),
    )(*[dslabs[n] for n in names])


SMALL_SIZES = (("norm_pre", 2048), ("norm_post", 2048), ("rel_bias", 512), ("att_sinks", 32), ("sg_ln_g", 2048),
               ("sg_ln_b", 2048), ("sg_w", 262144), ("sg_b", 2048), ("ssm_conv_b", 6144), ("ssm_dt_bias", 64),
               ("ssm_a_log", 64), ("ssm_d", 64), ("ssm_norm_g", 4096), ("conv_w_full", 24576))


def _pack_small(d):
    parts = []
    for name, size in SMALL_SIZES:
        rows = 8 * (-(-size // (8 * D_MODEL)))
        flat = d[name].reshape(-1) if name in d else jnp.zeros((size,), F32)
        parts.append(jnp.pad(flat, (0, rows * D_MODEL - size)).reshape(rows, D_MODEL))
    return _pad_rows(jnp.concatenate(parts, axis=0), SMALL_ROWS)


def _unpack_small(p, shapes):
    out, o = {}, 0
    for name, size in SMALL_SIZES:
        rows = 8 * (-(-size // (8 * D_MODEL)))
        if name in shapes:
            out[name] = p[o:o + rows].reshape(-1)[:size].reshape(shapes[name])
        o += rows
    return out


def _bucket_onehot_t():
    qi = jnp.arange(BLK, dtype=jnp.int32)[None, :]
    kj = jnp.arange(BLK, dtype=jnp.int32)[:, None]
    dd = (qi - kj) & (BLK - 1)
    in_window = dd >= 0
    max_exact = REL_BUCKETS // 2
    dist_f = jnp.maximum(dd, 1).astype(F32)
    large = max_exact + (jnp.log(dist_f / max_exact) / math.log(128 / max_exact)
                         * (REL_BUCKETS - max_exact)).astype(jnp.int32)
    large = jnp.minimum(large, REL_BUCKETS - 1)
    bucket = jnp.where(dd < max_exact, dd, large).reshape(1, -1)
    onehot_t = (bucket == jnp.arange(REL_BUCKETS, dtype=jnp.int32)[:, None]).astype(F32)
    maskadd = jnp.where(in_window, 0.0, NEG).astype(F32).reshape(1, -1)
    return onehot_t, maskadd


WEIGHTS = ['w_in', 'norm_pre', 'norm_post', 'rel_bias', 'att_sinks', 'sg_ln_g', 'sg_ln_b', 'sg_w', 'sg_b',
           'ssm_conv_w', 'ssm_conv_b', 'ssm_dt_bias', 'ssm_a_log', 'ssm_d', 'ssm_norm_g',
           'w_br_att', 'w_br_sg', 'w_br_ssm', 'w_out']
REST = ('w_br_att', 'w_br_sg', 'w_br_ssm', 'w_out')


def kernel(x, w_in, norm_pre, norm_post, rel_bias, att_sinks, sg_ln_g, sg_ln_b, sg_w, sg_b, ssm_conv_w, ssm_conv_b, ssm_dt_bias, ssm_a_log, ssm_d, ssm_norm_g, w_br_att, w_br_sg, w_br_ssm, w_out, loss_target, m_w_in, m_norm_pre, m_norm_post, m_rel_bias, m_att_sinks, m_sg_ln_g, m_sg_ln_b, m_sg_w, m_sg_b, m_ssm_conv_w, m_ssm_conv_b, m_ssm_dt_bias, m_ssm_a_log, m_ssm_d, m_ssm_norm_g, m_w_br_att, m_w_br_sg, m_w_br_ssm, m_w_out, v_w_in, v_norm_pre, v_norm_post, v_rel_bias, v_att_sinks, v_sg_ln_g, v_sg_ln_b, v_sg_w, v_sg_b, v_ssm_conv_w, v_ssm_conv_b, v_ssm_dt_bias, v_ssm_a_log, v_ssm_d, v_ssm_norm_g, v_w_br_att, v_w_br_sg, v_w_br_ssm, v_w_out):
    w = dict(w_in=w_in, norm_pre=norm_pre, norm_post=norm_post, rel_bias=rel_bias, att_sinks=att_sinks,
             sg_ln_g=sg_ln_g, sg_ln_b=sg_ln_b, sg_w=sg_w, sg_b=sg_b, ssm_conv_w=ssm_conv_w, ssm_conv_b=ssm_conv_b,
             ssm_dt_bias=ssm_dt_bias, ssm_a_log=ssm_a_log, ssm_d=ssm_d, ssm_norm_g=ssm_norm_g,
             w_br_att=w_br_att, w_br_sg=w_br_sg, w_br_ssm=w_br_ssm, w_out=w_out)
    mom = dict(w_in=m_w_in, norm_pre=m_norm_pre, norm_post=m_norm_post, rel_bias=m_rel_bias, att_sinks=m_att_sinks,
               sg_ln_g=m_sg_ln_g, sg_ln_b=m_sg_ln_b, sg_w=m_sg_w, sg_b=m_sg_b, ssm_conv_w=m_ssm_conv_w,
               ssm_conv_b=m_ssm_conv_b, ssm_dt_bias=m_ssm_dt_bias, ssm_a_log=m_ssm_a_log, ssm_d=m_ssm_d,
               ssm_norm_g=m_ssm_norm_g, w_br_att=m_w_br_att, w_br_sg=m_w_br_sg, w_br_ssm=m_w_br_ssm, w_out=m_w_out)
    var = dict(w_in=v_w_in, norm_pre=v_norm_pre, norm_post=v_norm_post, rel_bias=v_rel_bias, att_sinks=v_att_sinks,
               sg_ln_g=v_sg_ln_g, sg_ln_b=v_sg_ln_b, sg_w=v_sg_w, sg_b=v_sg_b, ssm_conv_w=v_ssm_conv_w,
               ssm_conv_b=v_ssm_conv_b, ssm_dt_bias=v_ssm_dt_bias, ssm_a_log=v_ssm_a_log, ssm_d=v_ssm_d,
               ssm_norm_g=v_ssm_norm_g, w_br_att=v_w_br_att, w_br_sg=v_w_br_sg, w_br_ssm=v_w_br_ssm, w_out=v_w_out)
    xs0 = x[0]
    target = loss_target[0]
    my_dev = 4 * lax.axis_index("x") + 2 * lax.axis_index("y") + lax.axis_index("c")

    conv_shard = _pad_rows(ssm_conv_w.reshape(-1, D_MODEL), 8)
    win_shard = _pack_win(w_in).astype(WIRE_DTYPE)
    rest_shard = _pack_rest(*[w[n] for n in REST]).astype(WIRE_DTYPE)
    layer_shards = [[win_shard[l * D_MODEL:(l + 1) * D_MODEL], rest_shard[l * LAYER_REST:(l + 1) * LAYER_REST]]
                    for l in range(2)]
    h0, (g_win0, gathered_conv) = rmsnorm_fwd(xs0, norm_pre[0][None], Exchange([], [layer_shards[0][0], conv_shard]))
    conv_full = gathered_conv[:, 0:3].reshape(N_DEV, 2, 4, 384).transpose(1, 2, 0, 3).reshape(2, 4, CONV_DIM)

    def set_rest(lw, g_rest):
        o = 0
        for name, rws in zip(("att", "sg", "ssm", "out"), REST_PARTS):
            lw[name] = g_rest[:, o:o + rws].reshape(N_DEV * rws, D_MODEL).astype(MXU_DTYPE)
            o += rws

    def layer_weights(l, g_win):
        slabs = _slabs_from_shards(g_win)
        lw = {"in_" + name: slab.astype(MXU_DTYPE) for name, slab in slabs.items()}
        lw["in_att"] = lw["in_att"].T
        tril = jnp.tril(jnp.ones((BLK, BLK), bool))
        sgw = jnp.where(tril[None], sg_w[l], 0.0)
        lw.update(
            g_pre=norm_pre[l][None], g_post=norm_post[l][None], sinks=jnp.repeat(att_sinks[l], BLK).reshape(2, GROUP_LANES),
            ln_g=sg_ln_g[l][None], ln_b=sg_ln_b[l][None], sgw=sgw.astype(MXU_DTYPE),
            sgw_t=sgw.transpose(0, 2, 1).astype(MXU_DTYPE), sgb_t=_pad_lanes(sg_b[l].T),
            cw=conv_full[l], cb=ssm_conv_b[l][None], dtb=_pad_lanes(ssm_dt_bias[l][None]),
            alog=_pad_lanes(ssm_a_log[l][None]), dsk=_pad_lanes(ssm_d[l][None]), ng=ssm_norm_g[l][None])
        return lw

    onehot_t, maskadd = _bucket_onehot_t()
    bias = bias_table(rel_bias.T, onehot_t, maskadd).reshape(2, GROUP_HEADS, BLK, BLK).transpose(0, 2, 1, 3)
    bias = bias.reshape(2, BLK, GROUP_LANES)

    saved = []
    xl = xs0
    layers = [layer_weights(0, g_win0)]
    for l in range(2):
        lw = layers[l]
        h = h0 if l == 0 else rmsnorm_fwd(xl, lw["g_pre"])
        pa = mm_nt(lw["in_att"], h, 1152, "proj_att")
        ps = mm_nn(h, lw["in_sg"], 1536, "proj_sg")
        pm = mm_nn(h, lw["in_ssm"], 1792, "proj_ssm")
        pg = mm_nn(h, lw["in_gate"], 1536, "proj_gate")
        if l == 0:
            ya, (g_rest0,) = attn_fwd(pa, bias, lw["sinks"], Exchange([], [layer_shards[0][1]]))
            set_rest(lw, g_rest0)
        else:
            ya = attn_fwd(pa, bias, lw["sinks"])
        sgu_args = (ps, lw["ln_g"], lw["ln_b"], lw["sgw"], lw["sgb_t"])
        ssd_args = (pm, lw["cw"], lw["cb"], lw["dtb"], lw["alog"], lw["dsk"], lw["ng"])
        ys = sgu_fwd(*sgu_args)
        if l == 0:
            ym, states, conv_pre, (g_win1, g_rest1) = ssd_fwd(*ssd_args, Exchange([], layer_shards[1]))
            layers.append(layer_weights(1, g_win1))
            set_rest(layers[1], g_rest1)
        else:
            ym, states, conv_pre = ssd_fwd(*ssd_args)
        merge_args = (xl, ya, ys, ym, pg, lw["att"], lw["sg"], lw["ssm"], lw["out"], lw["g_post"])
        if l == 0:
            x_next, ba, bs, bm, merged, out_s = merge_fwd(*merge_args)
        else:
            dx, loss_part, ba, bs, bm, merged, out_s = merge_fwd(*merge_args, target)
        saved.append(dict(x=xl, h=h, pa=pa, ps=ps, pm=pm, pg=pg, ya=ya, ys=ys, ym=ym, states=states,
                          conv_pre=conv_pre, ba=ba, bs=bs, bm=bm, merged=merged, out_s=out_s))
        xl = x_next

    loss = lax.psum(loss_part[0, 0], ("x", "y", "c"))

    dbias = jnp.zeros((2, BLK, GROUP_LANES), F32)
    win_grads, rest_grads = [None, None], [None, None]
    small = {n: [None, None] for n in ("norm_pre", "norm_post", "att_sinks", "sg_ln_g", "sg_ln_b", "sg_w", "sg_b",
                                       "ssm_conv_b", "ssm_dt_bias", "ssm_a_log", "ssm_d", "ssm_norm_g",
                                       "conv_w_full")}
    for l in (1, 0):
        lw, sv = layers[l], saved[l]
        dout, dba, dbs, dbm, dpg, dya, dys, dym, dg_post = merge_bwd(
            dx, sv["out_s"], sv["pg"], sv["ba"], sv["bs"], sv["bm"], lw["att"], lw["sg"], lw["ssm"], lw["out"],
            lw["g_post"])
        dw_out = mm_tn(sv["merged"], dout, 1024, "dw_out")
        dw_att = mm_kn(sv["ya"], dba, 1024, "dw_br_att")
        dw_sg = mm_tn(sv["ys"], dbs, 1024, "dw_br_sg")
        dw_ssm = mm_tn(sv["ym"], dbm, 1024, "dw_br_ssm")
        rest_grads[l] = jnp.concatenate(
            [dw_att.reshape(N_DEV, 128, D_MODEL), dw_sg.reshape(N_DEV, 128, D_MODEL),
             dw_ssm.reshape(N_DEV, 256, D_MODEL), dw_out.reshape(N_DEV, 128, D_MODEL)], axis=1).astype(WIRE_DTYPE)
        dpa, dbias, dsinks = attn_bwd(sv["pa"], dya, bias, lw["sinks"], dbias)
        dps, dsgw, dsgb_t, dln_g, dln_b = sgu_bwd(sv["ps"], dys, lw["ln_g"], lw["ln_b"], lw["sgw"], lw["sgw_t"],
                                                  lw["sgb_t"])
        ssd_args = (sv["pm"], sv["conv_pre"], dym, sv["states"], lw["cw"], lw["dtb"], lw["alog"], lw["dsk"], lw["ng"])
        if l == 0:
            dpm, dcw, dcb, dvec, dng, (recv_win1, recv_rest0) = ssd_bwd(
                *ssd_args, Exchange([win_grads[1], rest_grads[0]], []))
        else:
            dpm, dcw, dcb, dvec, dng, (recv_rest1,) = ssd_bwd(*ssd_args, Exchange([rest_grads[1]], []))
        dslabs = dict(att=mm_kn(dpa, sv["h"], 1152, "dw_in_att").T, sg=mm_tn(sv["h"], dps, 3072, "dw_in_sg"),
                      ssm=mm_tn(sv["h"], dpm, 2688, "dw_in_ssm"), gate=mm_tn(sv["h"], dpg, 3072, "dw_in_gate"))
        win_grads[l] = _shards_from_slabs(dslabs)
        dh_args = ([dpa, dps, dpm, dpg], [lw["in_att"], lw["in_sg"], lw["in_ssm"], lw["in_gate"]], sv["x"],
                   lw["g_pre"], dx)
        if l == 0:
            dx, dg_pre, (recv_win0,) = dh_norm_bwd(*dh_args, Exchange([win_grads[0]], []))
        else:
            dx, dg_pre = dh_norm_bwd(*dh_args)
        small["norm_pre"][l] = dg_pre[0]
        small["norm_post"][l] = dg_post[0]
        small["att_sinks"][l] = dsinks[0, :ATT_HEADS]
        small["sg_ln_g"][l] = dln_g[0]
        small["sg_ln_b"][l] = dln_b[0]
        small["sg_w"][l] = dsgw
        small["sg_b"][l] = dsgb_t[:, :SG_GROUPS].T
        small["ssm_conv_b"][l] = dcb[0]
        small["ssm_dt_bias"][l] = dvec[0, :SSM_HEADS]
        small["ssm_a_log"][l] = dvec[1, :SSM_HEADS]
        small["ssm_d"][l] = dvec[2, :SSM_HEADS]
        small["ssm_norm_g"][l] = dng[0]
        small["conv_w_full"][l] = dcw[0:4]
    grad_x = dx
    dbias = dbias.reshape(2, BLK, GROUP_HEADS, BLK).transpose(0, 2, 1, 3).reshape(ATT_HEADS, BLK * BLK)
    d_rel_bias = bias_table_bwd(dbias, onehot_t).T

    small_d = {n: jnp.stack(v) for n, v in small.items()}
    small_d["rel_bias"] = d_rel_bias
    *res_win, (recv_small,) = adamw([recv_win0, recv_win1], w_in, m_w_in, v_w_in, WIN_TILE, "adamw_w_in",
                                    Exchange([], [_pack_small(small_d)]))
    res_rest = adamw([recv_rest0, recv_rest1], _pack_rest(*[w[n] for n in REST]), _pack_rest(*[mom[n] for n in REST]),
                     _pack_rest(*[var[n] for n in REST]), REST_TILE, "adamw_rest")
    small_names = [n for n, _ in SMALL_SIZES if n != "conv_w_full"]
    g_s, dw_s, nm_s, nv_s = adamw([recv_small], _pack_small({n: w[n] for n in small_names}),
                                  _pack_small({n: mom[n] for n in small_names}),
                                  _pack_small({n: var[n] for n in small_names}), SMALL_TILE, "adamw_small")
    shapes = {n: w[n].shape for n in small_names}
    shapes["conv_w_full"] = (2, 4, CONV_DIM)
    g_conv_full = _unpack_small(g_s, shapes)["conv_w_full"]
    g_conv = lax.dynamic_slice_in_dim(g_conv_full, my_dev * 384, 384, axis=2)
    pack_conv = lambda a: _pad_rows(a.reshape(-1, D_MODEL), 8)
    g_c, dw_c, nm_c, nv_c = adamw([pack_conv(g_conv)[None]], pack_conv(ssm_conv_w), pack_conv(m_ssm_conv_w),
                                  pack_conv(v_ssm_conv_w), 8, "adamw_conv")

    results = {}
    for q, (tag, psm, pc) in enumerate((("grad", g_s, g_c), ("delta", dw_s, dw_c), ("new_m", nm_s, nm_c),
                                        ("new_v", nv_s, nv_c))):
        r = dict(zip(REST, _unpack_rest(res_rest[q])))
        r["w_in"] = res_win[q]
        r.update(_unpack_small(psm, {n: w[n].shape for n in small_names}))
        r["ssm_conv_w"] = pc[0:3].reshape(2, 4, 384)
        results[tag] = r
    outs = [loss, grad_x[None]]
    for tag in ("grad", "delta", "new_m", "new_v"):
        outs += [results[tag][n] for n in WEIGHTS]
    return tuple(outs)
```
